```python
import math
import jax, jax.numpy as jnp
from jax import lax
import numpy as np


D_MODEL = 1024
BATCH = 16
SEQ = 2048
DEPTH = 1

D_MIX = D_MODEL
D_LRU = D_MIX // 2
N_LRU_BLOCKS = 8
LRU_BLOCK = D_LRU // N_LRU_BLOCKS
CONV_WIDTH = 4
RG_C = 8.0
N_HEADS = 8
HEAD_DIM = (D_MIX - D_LRU) // N_HEADS
D_ATTN = N_HEADS * HEAD_DIM
ROT_DIM = HEAD_DIM // 4
ROPE_THETA = 500000.0
MOBA_BLOCK = 256
MOBA_TOPK = 3
Q_CHUNK = 8
MEM_LEN = 256
N_XHEADS = 4
XHEAD_DIM = D_MODEL // N_XHEADS
N_GROUPS = 4
EXPERTS_PER_GROUP = 8
N_EXPERTS = N_GROUPS * EXPERTS_PER_GROUP
EXPERT_TOPK = 2
D_FF_EXPERT = D_MODEL // 2
MOE_ROWS = 128
D_IN = 2 * D_LRU + 3 * D_ATTN
EPS = 1e-6

kernel_name = 'hymba_rglru_moba_hmoe_block'


def rms_norm(x, g):
    xf = x.astype(jnp.float32)
    y = xf * lax.rsqrt(jnp.mean(xf * xf, axis=-1, keepdims=True) + EPS)
    return (y * g.astype(jnp.float32)).astype(x.dtype)


def partial_rope(x, pos):
    half = ROT_DIM // 2
    inv_freq = ROPE_THETA ** (-jnp.arange(0, ROT_DIM, 2, dtype=jnp.float32) / ROT_DIM)
    ang = pos.astype(jnp.float32)[:, None] * inv_freq[None, :]
    cos, sin = jnp.cos(ang), jnp.sin(ang)
    xf = x.astype(jnp.float32)
    x1, x2, rest = xf[..., :half], xf[..., half:ROT_DIM], xf[..., ROT_DIM:]
    out = jnp.concatenate([x1 * cos - x2 * sin, x2 * cos + x1 * sin, rest], axis=-1)
    return out.astype(x.dtype)


def causal_depthwise_conv(x, w, b):
    C = x.shape[-1]
    y = lax.conv_general_dilated(x, w[:, None, :].astype(x.dtype), window_strides=(1,),
                                 padding=[(CONV_WIDTH - 1, 0)],
                                 dimension_numbers=('NWC', 'WIO', 'NWC'),
                                 feature_group_count=C)
    return y + b


def rg_lru(xc, w_rg, b_rg, w_ig, b_ig, lru_lambda):
    B, S, C = xc.shape
    xh = xc.reshape(B, S, N_LRU_BLOCKS, LRU_BLOCK)
    r = jax.nn.sigmoid((jnp.einsum('bshi,hij->bshj', xh, w_rg).reshape(B, S, C) + b_rg).astype(jnp.float32))
    i = jax.nn.sigmoid((jnp.einsum('bshi,hij->bshj', xh, w_ig).reshape(B, S, C) + b_ig).astype(jnp.float32))
    log_a = -RG_C * r * jax.nn.softplus(-lru_lambda.astype(jnp.float32))
    a = jnp.exp(log_a)
    u = jnp.sqrt(-jnp.expm1(2.0 * log_a)) * (i * xc.astype(jnp.float32))

    def step(h, au):
        a_t, u_t = au
        h = a_t * h + u_t
        return h, h

    _, hs = lax.scan(step, jnp.zeros((B, C), jnp.float32), (jnp.swapaxes(a, 0, 1), jnp.swapaxes(u, 0, 1)))
    return jnp.swapaxes(hs, 0, 1).astype(xc.dtype)


def moba_attention(q, k, v):
    B, H, S, hd = q.shape
    nb = -(-S // MOBA_BLOCK)
    sp = nb * MOBA_BLOCK
    pad = ((0, 0), (0, 0), (0, sp - S), (0, 0))
    q, k, v = jnp.pad(q, pad), jnp.pad(k, pad), jnp.pad(v, pad)
    kb = k.reshape(B, H, nb, MOBA_BLOCK, hd)
    vb = v.reshape(B, H, nb, MOBA_BLOCK, hd)
    kmean = jnp.mean(kb.astype(jnp.float32), axis=3)
    pos = jnp.arange(sp, dtype=jnp.int32)
    qblk = pos // MOBA_BLOCK
    gate = jnp.einsum('bhsd,bhnd->bhsn', q.astype(jnp.float32), kmean)
    past = jnp.arange(nb, dtype=jnp.int32)[None, :] < qblk[:, None]
    gate = jnp.where(past, gate, -jnp.inf)
    n_pick = min(MOBA_TOPK, nb)
    _, top = lax.top_k(gate, n_pick)
    top = top.astype(jnp.int32)
    top_valid = top < qblk[:, None]
    own = jnp.broadcast_to(qblk[:, None], (B, H, sp, 1))
    idx = jnp.concatenate([top, own], axis=-1)
    valid = jnp.concatenate([top_valid, jnp.ones_like(own, dtype=bool)], axis=-1)
    nsel = n_pick + 1
    nc = sp // Q_CHUNK
    kflat = kb.reshape(B * H * nb, MOBA_BLOCK, hd)
    vflat = vb.reshape(B * H * nb, MOBA_BLOCK, hd)
    bh_off = (jnp.arange(B * H, dtype=jnp.int32) * nb).reshape(B, H, 1, 1)
    scale = HEAD_DIM ** -0.5

    def to_chunks(t):
        return jnp.moveaxis(t.reshape(B, H, nc, Q_CHUNK, *t.shape[3:]), 2, 0)

    def attend(args):
        q_c, idx_c, valid_c, t_c = args
        gidx = bh_off + idx_c
        kg = kflat[gidx]
        vg = vflat[gidx]
        s = jnp.einsum('bhqd,bhqnkd->bhqnk', q_c, kg).astype(jnp.float32) * scale
        keypos = idx_c[..., None] * MOBA_BLOCK + jnp.arange(MOBA_BLOCK, dtype=jnp.int32)
        mask = valid_c[..., None] & (keypos <= t_c[None, None, :, None, None])
        s = jnp.where(mask, s, -jnp.inf).reshape(B, H, Q_CHUNK, nsel * MOBA_BLOCK)
        p = jax.nn.softmax(s, axis=-1).astype(vg.dtype)
        return jnp.einsum('bhqm,bhqmd->bhqd', p, vg.reshape(B, H, Q_CHUNK, nsel * MOBA_BLOCK, hd))

    out = lax.map(attend, (to_chunks(q), to_chunks(idx), to_chunks(valid), pos.reshape(nc, Q_CHUNK)))
    out = jnp.moveaxis(out, 0, 2).reshape(B, H, sp, hd)
    return out[:, :, :S]


def hybrid_mixer(xn, w_in, conv_w, conv_b, w_rg, b_rg, w_ig, b_ig, lru_lambda,
                 norm_lru_out, norm_attn_out, w_out):
    B, S, _ = xn.shape
    proj = xn @ w_in
    x_lru, g_lru, q, k, v = jnp.split(
        proj, [D_LRU, 2 * D_LRU, 2 * D_LRU + D_ATTN, 2 * D_LRU + 2 * D_ATTN], axis=-1)
    xc = causal_depthwise_conv(x_lru, conv_w, conv_b)
    y_lru = rg_lru(xc, w_rg, b_rg, w_ig, b_ig, lru_lambda) * jax.nn.gelu(g_lru)
    pos = jnp.arange(S, dtype=jnp.int32)
    heads = lambda t: t.reshape(B, S, N_HEADS, HEAD_DIM).transpose(0, 2, 1, 3)
    qh = partial_rope(heads(q), pos)
    kh = partial_rope(heads(k), pos)
    y_att = moba_attention(qh, kh, heads(v)).transpose(0, 2, 1, 3).reshape(B, S, D_ATTN)
    y = jnp.concatenate([rms_norm(y_lru, norm_lru_out), rms_norm(y_att, norm_attn_out)], axis=-1)
    return y @ w_out


def memory_cross_attention(hn, mem, norm_mem, w_xq, w_xkv, w_xo):
    B, S, _ = hn.shape
    M = mem.shape[1]
    q = (hn @ w_xq).reshape(B, S, N_XHEADS, XHEAD_DIM)
    k, v = jnp.split(rms_norm(mem, norm_mem) @ w_xkv, 2, axis=-1)
    k = k.reshape(B, M, N_XHEADS, XHEAD_DIM)
    v = v.reshape(B, M, N_XHEADS, XHEAD_DIM)
    s = jnp.einsum('bshd,bmhd->bhsm', q, k).astype(jnp.float32) * (XHEAD_DIM ** -0.5)
    p = jax.nn.softmax(s, axis=-1).astype(v.dtype)
    o = jnp.einsum('bhsm,bmhd->bshd', p, v).reshape(B, S, D_MODEL)
    return o @ w_xo


def hier_moe(xn, w_router_group, b_router_group, w_router_expert, b_router_expert, w_gate_up, w_down):
    B, S, D = xn.shape
    N = B * S
    xt = xn.reshape(N, D)
    g_logits = (xt @ w_router_group + b_router_group).astype(jnp.float32)
    g_prob = jax.nn.softmax(g_logits, axis=-1)
    grp = jnp.argmax(g_logits, axis=-1).astype(jnp.int32)
    g_w = jnp.take_along_axis(g_prob, grp[:, None], axis=-1)
    e_logits = (xt @ w_router_expert + b_router_expert).astype(jnp.float32).reshape(N, N_GROUPS, EXPERTS_PER_GROUP)
    e_logits = jnp.take_along_axis(e_logits, grp[:, None, None], axis=1)[:, 0]
    e_prob = jax.nn.softmax(e_logits, axis=-1)
    top_p, top_i = lax.top_k(e_prob, EXPERT_TOPK)
    gates = g_w * top_p / jnp.sum(top_p, axis=-1, keepdims=True)
    expert = grp[:, None] * EXPERTS_PER_GROUP + top_i.astype(jnp.int32)
    M = N * EXPERT_TOPK
    e_flat = expert.reshape(M)
    g_flat = gates.reshape(M)
    order = jnp.argsort(e_flat).astype(jnp.int32)
    e_sorted = e_flat[order]
    tok_sorted = order // EXPERT_TOPK
    gate_sorted = g_flat[order]
    counts = jnp.bincount(e_flat, length=N_EXPERTS).astype(jnp.int32)
    starts = jnp.cumsum(counts) - counts
    padded = (counts + MOE_ROWS - 1) // MOE_ROWS * MOE_ROWS
    pad_ends = jnp.cumsum(padded)
    pad_starts = pad_ends - padded
    dest = pad_starts[e_sorted] + jnp.arange(M, dtype=jnp.int32) - starts[e_sorted]
    n_blocks = -(-M // MOE_ROWS) + N_EXPERTS
    x_buf = jnp.zeros((n_blocks * MOE_ROWS, D), xt.dtype).at[dest].set(xt[tok_sorted])
    blk_expert = jnp.minimum(
        jnp.searchsorted(pad_ends, jnp.arange(n_blocks, dtype=jnp.int32) * MOE_ROWS, side='right'),
        N_EXPERTS - 1).astype(jnp.int32)

    def expert_block(args):
        e, xb = args
        gate, up = jnp.split(xb @ w_gate_up[e], 2, axis=-1)
        return (jax.nn.silu(gate) * up) @ w_down[e]

    y_buf = lax.map(expert_block, (blk_expert, x_buf.reshape(n_blocks, MOE_ROWS, D))).reshape(-1, D)
    y = y_buf[dest] * gate_sorted[:, None].astype(xt.dtype)
    out = jnp.zeros((N, D), xt.dtype).at[tok_sorted].add(y)
    return out.reshape(B, S, D)


def setup_inputs(seed: int = 0) -> dict:
    key = jax.random.key(seed)
    ks = jax.random.split(key, 32)
    f32 = jnp.float32

    def nrm(k, shape, scale):
        return jax.random.normal(k, shape, f32) * scale

    def gain(k, shape):
        return 1.0 + 0.02 * jax.random.normal(k, shape, f32)

    u = jax.random.uniform(ks[8], (DEPTH, D_LRU), f32, minval=0.9, maxval=0.999)
    a = u ** (1.0 / RG_C)
    lru_lambda = jnp.log(a) - jnp.log1p(-a)
    return {
        'x': nrm(ks[0], (BATCH, SEQ, D_MODEL), 1.0),
        'mem': nrm(ks[1], (BATCH, MEM_LEN, D_MODEL), 1.0),
        'norm_mix': gain(ks[2], (DEPTH, D_MODEL)),
        'w_in': nrm(ks[3], (DEPTH, D_MODEL, D_IN), D_MODEL ** -0.5),
        'conv_w': nrm(ks[4], (DEPTH, CONV_WIDTH, D_LRU), CONV_WIDTH ** -0.5),
        'conv_b': nrm(ks[5], (DEPTH, D_LRU), 0.02),
        'w_rg': nrm(ks[6], (DEPTH, N_LRU_BLOCKS, LRU_BLOCK, LRU_BLOCK), LRU_BLOCK ** -0.5),
        'b_rg': nrm(ks[7], (DEPTH, D_LRU), 0.02),
        'w_ig': nrm(ks[9], (DEPTH, N_LRU_BLOCKS, LRU_BLOCK, LRU_BLOCK), LRU_BLOCK ** -0.5),
        'b_ig': nrm(ks[10], (DEPTH, D_LRU), 0.02),
        'lru_lambda': lru_lambda,
        'norm_lru_out': gain(ks[11], (DEPTH, D_LRU)),
        'norm_attn_out': gain(ks[12], (DEPTH, D_ATTN)),
        'w_out': nrm(ks[13], (DEPTH, D_MIX, D_MODEL), D_MIX ** -0.5),
        'norm_cross': gain(ks[14], (DEPTH, D_MODEL)),
        'norm_mem': gain(ks[15], (DEPTH, D_MODEL)),
        'w_xq': nrm(ks[16], (DEPTH, D_MODEL, D_MODEL), D_MODEL ** -0.5),
        'w_xkv': nrm(ks[17], (DEPTH, D_MODEL, 2 * D_MODEL), D_MODEL ** -0.5),
        'w_xo': nrm(ks[18], (DEPTH, D_MODEL, D_MODEL), D_MODEL ** -0.5),
        'norm_ffn': gain(ks[19], (DEPTH, D_MODEL)),
        'w_router_group': nrm(ks[20], (DEPTH, D_MODEL, N_GROUPS), D_MODEL ** -0.5),
        'b_router_group': nrm(ks[21], (DEPTH, N_GROUPS), 0.01),
        'w_router_expert': nrm(ks[22], (DEPTH, D_MODEL, N_EXPERTS), D_MODEL ** -0.5),
        'b_router_expert': nrm(ks[23], (DEPTH, N_EXPERTS), 0.01),
        'w_gate_up': nrm(ks[24], (DEPTH, N_EXPERTS, D_MODEL, 2 * D_FF_EXPERT), D_MODEL ** -0.5),
        'w_down': nrm(ks[25], (DEPTH, N_EXPERTS, D_FF_EXPERT, D_MODEL), D_FF_EXPERT ** -0.5),
        'norm_final': gain(ks[26], (D_MODEL,)),
    }


def reference(x, mem, norm_mix, w_in, conv_w, conv_b, w_rg, b_rg, w_ig, b_ig, lru_lambda,
              norm_lru_out, norm_attn_out, w_out, norm_cross, norm_mem, w_xq, w_xkv, w_xo,
              norm_ffn, w_router_group, b_router_group, w_router_expert, b_router_expert,
              w_gate_up, w_down, norm_final):
    h = x
    for l in range(DEPTH):
        h = h + hybrid_mixer(rms_norm(h, norm_mix[l]), w_in[l], conv_w[l], conv_b[l], w_rg[l], b_rg[l],
                             w_ig[l], b_ig[l], lru_lambda[l], norm_lru_out[l], norm_attn_out[l], w_out[l])
        h = h + memory_cross_attention(rms_norm(h, norm_cross[l]), mem, norm_mem[l], w_xq[l], w_xkv[l], w_xo[l])
        h = h + hier_moe(rms_norm(h, norm_ffn[l]), w_router_group[l], b_router_group[l],
                         w_router_expert[l], b_router_expert[l], w_gate_up[l], w_down[l])
    return rms_norm(h, norm_final)
```

```python
import functools

import jax
import jax.numpy as jnp
from jax import lax
from jax.experimental import pallas as pl
from jax.experimental.pallas import tpu as pltpu

F32 = jnp.float32
BF16 = jnp.bfloat16

N_LRU_BLOCKS = 8
CONV_WIDTH = 4
RG_C = 8.0
N_HEADS = 8
HEAD_DIM = 64
ROT_DIM = HEAD_DIM // 4
ROPE_THETA = 500000.0
MOBA_BLOCK = 256
MOBA_TOPK = 3
N_XHEADS = 4
N_GROUPS = 4
EXPERTS_PER_GROUP = 8
N_EXPERTS = N_GROUPS * EXPERTS_PER_GROUP
EXPERT_TOPK = 2
EPS = 1e-6

LANES = 128
SUBLANES = 8
NEG_BIG = -1e30
EXPERT_ROWS = 256
ROUTER_ROWS = SUBLANES + N_EXPERTS
VMEM_LIMIT = 56 * 1024 * 1024

NT_DIMS = (((1,), (1,)), ((), ()))


def _rms(x, g):
    return x * lax.rsqrt(jnp.mean(x * x, axis=-1, keepdims=True) + EPS) * g


def _params(sem):
    return pltpu.CompilerParams(dimension_semantics=sem, vmem_limit_bytes=VMEM_LIMIT)


def _memkv_kernel(mem_ref, g_ref, w_ref, k_ref, v_ref):
    d = mem_ref.shape[-1]
    mn = _rms(mem_ref[0], g_ref[...]).astype(BF16)
    kv = jnp.dot(mn, w_ref[...], preferred_element_type=F32)
    k_ref[0] = kv[:, :d].astype(BF16)
    v_ref[0] = kv[:, d:].astype(BF16)


def _memkv(mem, g, w_xkv):
    b, m, d = mem.shape
    return pl.pallas_call(
        _memkv_kernel,
        grid=(b,),
        in_specs=[pl.BlockSpec((1, m, d), lambda i: (i, 0, 0)),
                  pl.BlockSpec((1, d), lambda i: (0, 0)),
                  pl.BlockSpec((d, 2 * d), lambda i: (0, 0))],
        out_specs=[pl.BlockSpec((1, m, d), lambda i: (i, 0, 0)),
                   pl.BlockSpec((1, m, d), lambda i: (i, 0, 0))],
        out_shape=[jax.ShapeDtypeStruct((b, m, d), BF16)] * 2,
        compiler_params=_params(("arbitrary",)),
        name="memkv",
    )(mem, g, w_xkv)


def _mixer_in_kernel(x_ref, gmix_ref, win_ref, convw_ref, convb_ref, wg_ref, bg_ref, lam_ref, glru_ref,
                     cos_ref, sina_ref, sinb_ref,
                     ylru_ref, q_ref, k_ref, v_ref, xpad_ref, hcar_ref):
    ts = x_ref.shape[1]
    c = ylru_ref.shape[-1]
    si = pl.program_id(1)

    @pl.when(si == 0)
    def _():
        xpad_ref[0:SUBLANES, :] = jnp.zeros((SUBLANES, c), F32)
        hcar_ref[...] = jnp.zeros_like(hcar_ref)

    xn = _rms(x_ref[0], gmix_ref[...]).astype(BF16)
    proj = jnp.dot(xn, win_ref[...], preferred_element_type=F32)
    x_lru = proj[:, 0:c]
    g_lru = proj[:, c:2 * c]
    q = proj[:, 2 * c:3 * c]
    k = proj[:, 3 * c:4 * c]
    v = proj[:, 4 * c:5 * c]

    xpad_ref[SUBLANES:SUBLANES + ts, :] = x_lru
    cw = convw_ref[...]
    xc = convb_ref[...] + cw[3:4] * x_lru
    for j in range(1, CONV_WIDTH):
        xc = xc + cw[3 - j:4 - j] * xpad_ref[SUBLANES - j:SUBLANES - j + ts, :]
    xpad_ref[0:SUBLANES, :] = x_lru[ts - SUBLANES:ts, :]

    gates = jnp.dot(xc.astype(BF16), wg_ref[...], preferred_element_type=F32) + bg_ref[...]
    r = jax.nn.sigmoid(gates[:, 0:c])
    i = jax.nn.sigmoid(gates[:, c:2 * c])
    neg_lam = -lam_ref[...]
    softplus = jnp.maximum(neg_lam, 0.0) + jnp.log1p(jnp.exp(-jnp.abs(neg_lam)))
    log_a = -RG_C * r * softplus
    a = jnp.exp(log_a)
    u = jnp.sqrt(-jnp.tanh(log_a) * (a * a + 1.0)) * (i * xc)

    row = lax.broadcasted_iota(jnp.int32, (ts, c), 0)
    step = 1
    while step < ts:
        keep = row >= step
        a_sh = jnp.where(keep, pltpu.roll(a, step, axis=0), 1.0)
        u_sh = jnp.where(keep, pltpu.roll(u, step, axis=0), 0.0)
        u = a * u_sh + u
        a = a * a_sh
        step *= 2
    h = a * hcar_ref[...] + u
    hcar_ref[...] = h[ts - 1:ts, :]

    y = h * jax.nn.gelu(g_lru)
    ylru_ref[0] = _rms(y, glru_ref[...]).astype(BF16)

    cos, sina, sinb = cos_ref[...], sina_ref[...], sinb_ref[...]
    half = ROT_DIM // 2

    def rope(t):
        return t * cos + pltpu.roll(t, c - half, axis=1) * sina + pltpu.roll(t, half, axis=1) * sinb

    q_ref[0] = rope(q)
    k_ref[0] = rope(k)
    v_ref[0] = v.astype(BF16)


def _mixer_in(x, gmix, w_in, conv_w, conv_b, w_gates, b_gates, lam, glru, cos_t, sina_t, sinb_t, ts):
    b, s, d = x.shape
    c = conv_w.shape[-1]
    full = lambda shape: pl.BlockSpec(shape, lambda bi, si: (0,) * len(shape))
    tab = pl.BlockSpec((ts, c), lambda bi, si: (si, 0))
    seq = lambda: pl.BlockSpec((1, ts, c), lambda bi, si: (bi, si, 0))
    return pl.pallas_call(
        _mixer_in_kernel,
        grid=(b, s // ts),
        in_specs=[pl.BlockSpec((1, ts, d), lambda bi, si: (bi, si, 0)),
                  full((1, d)), full(w_in.shape), full(conv_w.shape), full((1, c)),
                  full(w_gates.shape), full((1, 2 * c)), full((1, c)), full((1, c)),
                  tab, tab, tab],
        out_specs=[seq(), seq(), seq(), seq()],
        out_shape=[jax.ShapeDtypeStruct((b, s, c), BF16),
                   jax.ShapeDtypeStruct((b, s, c), F32),
                   jax.ShapeDtypeStruct((b, s, c), F32),
                   jax.ShapeDtypeStruct((b, s, c), BF16)],
        scratch_shapes=[pltpu.VMEM((ts + SUBLANES, c), F32), pltpu.VMEM((1, c), F32)],
        compiler_params=_params(("arbitrary", "arbitrary")),
        name="mixer_in",
    )(x, gmix, w_in, conv_w, conv_b, w_gates, b_gates, lam, glru, cos_t, sina_t, sinb_t)


def _moba_kernel(q_ref, k_ref, v_ref, o_ref, qa_ref, ka_ref):
    s_len = q_ref.shape[1]
    nb = s_len // MOBA_BLOCK
    n_pick = min(MOBA_TOPK, nb)
    q = q_ref[0]
    k = k_ref[0]
    lane = lax.broadcasted_iota(jnp.int32, (1, LANES), 1)
    kmean = jnp.concatenate(
        [jnp.sum(k[n * MOBA_BLOCK:(n + 1) * MOBA_BLOCK], axis=0, keepdims=True) for n in range(nb)],
        axis=0) * (1.0 / MOBA_BLOCK)

    blk_row = lax.broadcasted_iota(jnp.int32, (nb, s_len), 0)
    q_blk = lax.broadcasted_iota(jnp.int32, (nb, s_len), 1) // MOBA_BLOCK
    past = blk_row < q_blk
    key_blk = lax.broadcasted_iota(jnp.int32, (s_len, LANES), 0) // MOBA_BLOCK
    lane_full = lax.broadcasted_iota(jnp.int32, (s_len, LANES), 1)

    for h in range(2):
        own = (lane >= h * HEAD_DIM) & (lane < (h + 1) * HEAD_DIM)
        off = (1 - h) * HEAD_DIM
        gate = lax.dot_general(jnp.where(own, kmean, 0.0), q, NT_DIMS,
                               precision=lax.Precision.HIGHEST, preferred_element_type=F32)
        gate = jnp.where(past, gate, -jnp.inf)
        rank = jnp.zeros((nb, s_len), jnp.int32)
        for m in range(nb):
            gm = gate[m:m + 1, :]
            ahead = (gm > gate) | ((gm == gate) & (m < blk_row))
            rank = rank + ahead.astype(jnp.int32)
        allowed = ((rank < n_pick) & past) | (blk_row == q_blk)
        bias = jnp.where(allowed, 0.0, NEG_BIG)
        pieces = []
        if off:
            pieces.append(jnp.zeros((off, s_len), F32))
        pieces.append(bias)
        if LANES - off - nb:
            pieces.append(jnp.zeros((LANES - off - nb, s_len), F32))
        bias_lanes = jnp.concatenate(pieces, axis=0).T
        qa_ref[h] = jnp.where(own, q * (HEAD_DIM ** -0.5), bias_lanes).astype(BF16)
        onehot = (lane_full - off == key_blk).astype(F32)
        ka_ref[h] = jnp.where(own, k, onehot).astype(BF16)

    own0 = lane < HEAD_DIM
    for qi in range(nb):
        nk = (qi + 1) * MOBA_BLOCK
        col = lax.broadcasted_iota(jnp.int32, (MOBA_BLOCK, nk), 1)
        qpos = lax.broadcasted_iota(jnp.int32, (MOBA_BLOCK, nk), 0) + qi * MOBA_BLOCK
        causal = col <= qpos
        vv = v_ref[0, 0:nk, :]
        outs = []
        for h in range(2):
            qa = qa_ref[h, qi * MOBA_BLOCK:(qi + 1) * MOBA_BLOCK, :]
            s = lax.dot_general(qa, ka_ref[h, 0:nk, :], NT_DIMS, preferred_element_type=F32)
            s = jnp.where(causal, s, NEG_BIG)
            p = jnp.exp(s - jnp.max(s, axis=-1, keepdims=True))
            l = jnp.sum(p, axis=-1, keepdims=True)
            outs.append(jnp.dot(p.astype(BF16), vv, preferred_element_type=F32) / l)
        o_ref[0, qi * MOBA_BLOCK:(qi + 1) * MOBA_BLOCK, :] = jnp.where(own0, outs[0], outs[1])


def _moba(q, k, v):
    b, s, c = q.shape
    spec = lambda: pl.BlockSpec((1, s, LANES), lambda bi, hi: (bi, 0, hi))
    return pl.pallas_call(
        _moba_kernel,
        grid=(b, c // LANES),
        in_specs=[spec(), spec(), spec()],
        out_specs=spec(),
        out_shape=jax.ShapeDtypeStruct((b, s, c), F32),
        scratch_shapes=[pltpu.VMEM((2, s, LANES), BF16), pltpu.VMEM((2, s, LANES), BF16)],
        compiler_params=_params(("arbitrary", "arbitrary")),
        name="moba",
    )(q, k, v)


def _post_kernel(x_ref, ylru_ref, yatt_ref, gatt_ref, woa_ref, wob_ref, gcross_ref, wxq_ref, kx_ref, vx_ref,
                 wxo_ref, gffn_ref, wr_ref, br_ref, tri_ref,
                 h2_ref, hn2_ref, eid_ref, gate_ref, gcol_ref, rank_ref, cnt_ref, carry_ref):
    tm = x_ref.shape[1]
    d = x_ref.shape[2]
    xd = d // N_XHEADS
    first = (pl.program_id(0) == 0) & (pl.program_id(1) == 0)

    @pl.when(first)
    def _():
        carry_ref[...] = jnp.zeros_like(carry_ref)

    ya = _rms(yatt_ref[0], gatt_ref[...]).astype(BF16)
    mix = (jnp.dot(ylru_ref[0], woa_ref[...], preferred_element_type=F32)
           + jnp.dot(ya, wob_ref[...], preferred_element_type=F32))
    h1 = x_ref[0] + mix

    hn = _rms(h1, gcross_ref[...]).astype(BF16)
    qx = jnp.dot(hn, wxq_ref[...], preferred_element_type=F32)
    heads = []
    for hh in range(N_XHEADS):
        qh = (qx[:, hh * xd:(hh + 1) * xd] * (xd ** -0.5)).astype(BF16)
        s = lax.dot_general(qh, kx_ref[0, :, hh * xd:(hh + 1) * xd], NT_DIMS, preferred_element_type=F32)
        p = jnp.exp(s - jnp.max(s, axis=-1, keepdims=True))
        l = jnp.sum(p, axis=-1, keepdims=True)
        o = jnp.dot(p.astype(BF16), vx_ref[0, :, hh * xd:(hh + 1) * xd], preferred_element_type=F32) / l
        heads.append(o.astype(BF16))
    h2 = h1 + jnp.dot(jnp.concatenate(heads, axis=1), wxo_ref[...], preferred_element_type=F32)
    hn2 = _rms(h2, gffn_ref[...])
    h2_ref[...] = h2
    hn2_ref[...] = hn2

    logits = lax.dot_general(wr_ref[...], hn2, NT_DIMS, precision=lax.Precision.HIGHEST,
                             preferred_element_type=F32) + br_ref[...]
    gl = logits[0:N_GROUPS]
    gmax = jnp.max(gl, axis=0, keepdims=True)
    gi = lax.broadcasted_iota(jnp.int32, gl.shape, 0)
    grp = jnp.min(jnp.where(gl == gmax, gi, N_GROUPS), axis=0, keepdims=True)
    g_w = 1.0 / jnp.sum(jnp.exp(gl - gmax), axis=0, keepdims=True)
    el = jnp.zeros((EXPERTS_PER_GROUP, tm), F32)
    for g in range(N_GROUPS):
        lo = SUBLANES + g * EXPERTS_PER_GROUP
        el = jnp.where(grp == g, logits[lo:lo + EXPERTS_PER_GROUP], el)
    ee = jnp.exp(el - jnp.max(el, axis=0, keepdims=True))
    ep = ee / jnp.sum(ee, axis=0, keepdims=True)
    ei = lax.broadcasted_iota(jnp.int32, ep.shape, 0)
    p1 = jnp.max(ep, axis=0, keepdims=True)
    i1 = jnp.min(jnp.where(ep == p1, ei, EXPERTS_PER_GROUP), axis=0, keepdims=True)
    ep_rest = jnp.where(ei == i1, -1.0, ep)
    p2 = jnp.max(ep_rest, axis=0, keepdims=True)
    i2 = jnp.min(jnp.where(ep_rest == p2, ei, EXPERTS_PER_GROUP), axis=0, keepdims=True)
    den = p1 + p2
    gate1 = g_w * p1 / den
    gate2 = g_w * p2 / den
    e1 = grp * EXPERTS_PER_GROUP + i1
    e2 = grp * EXPERTS_PER_GROUP + i2
    eid_ref[0:1, :] = e1
    eid_ref[1:2, :] = e2
    gate_ref[0:1, :] = gate1
    gate_ref[1:2, :] = gate2
    li = lax.broadcasted_iota(jnp.int32, (LANES, tm), 0)
    gcol_ref[...] = jnp.where(li == 0, gate1, jnp.where(li == 1, gate2, 0.0)).T

    xi = lax.broadcasted_iota(jnp.int32, (N_EXPERTS, tm), 0)
    oh1 = xi == e1
    oh2 = xi == e2
    cnt = oh1.astype(F32) + oh2.astype(F32)
    before = jnp.dot(cnt.astype(BF16), tri_ref[...], preferred_element_type=F32) + carry_ref[:, 0:1]
    rank_ref[0:1, :] = jnp.sum(jnp.where(oh1, before, 0.0), axis=0, keepdims=True).astype(jnp.int32)
    rank_ref[1:2, :] = jnp.sum(jnp.where(oh2, before, 0.0), axis=0, keepdims=True).astype(jnp.int32)
    carry_ref[...] = carry_ref[...] + jnp.sum(cnt, axis=1, keepdims=True)
    cnt_ref[...] = carry_ref[...]


def _post(x, ylru, yatt, gatt, wo_a, wo_b, gcross, wxq, kx, vx, wxo, gffn, wr_t, br_t, tri, tm):
    b, s, d = x.shape
    c = ylru.shape[-1]
    m = kx.shape[1]
    n = b * s
    nt = s // tm
    full = lambda shape: pl.BlockSpec(shape, lambda bi, si: (0,) * len(shape))
    tok = lambda rows: pl.BlockSpec((rows, tm), lambda bi, si: (0, bi * nt + si))
    return pl.pallas_call(
        _post_kernel,
        grid=(b, nt),
        in_specs=[pl.BlockSpec((1, tm, d), lambda bi, si: (bi, si, 0)),
                  pl.BlockSpec((1, tm, c), lambda bi, si: (bi, si, 0)),
                  pl.BlockSpec((1, tm, c), lambda bi, si: (bi, si, 0)),
                  full((1, c)), full((c, d)), full((c, d)), full((1, d)), full((d, d)),
                  pl.BlockSpec((1, m, d), lambda bi, si: (bi, 0, 0)),
                  pl.BlockSpec((1, m, d), lambda bi, si: (bi, 0, 0)),
                  full((d, d)), full((1, d)), full(wr_t.shape), full(br_t.shape), full((tm, tm))],
        out_specs=[pl.BlockSpec((tm, d), lambda bi, si: (bi * nt + si, 0)),
                   pl.BlockSpec((tm, d), lambda bi, si: (bi * nt + si, 0)),
                   tok(EXPERT_TOPK), tok(EXPERT_TOPK),
                   pl.BlockSpec((tm, LANES), lambda bi, si: (bi * nt + si, 0)),
                   tok(EXPERT_TOPK), full((N_EXPERTS, LANES))],
        out_shape=[jax.ShapeDtypeStruct((n, d), F32),
                   jax.ShapeDtypeStruct((n, d), F32),
                   jax.ShapeDtypeStruct((EXPERT_TOPK, n), jnp.int32),
                   jax.ShapeDtypeStruct((EXPERT_TOPK, n), F32),
                   jax.ShapeDtypeStruct((n, LANES), F32),
                   jax.ShapeDtypeStruct((EXPERT_TOPK, n), jnp.int32),
                   jax.ShapeDtypeStruct((N_EXPERTS, LANES), F32)],
        scratch_shapes=[pltpu.VMEM((N_EXPERTS, LANES), F32)],
        compiler_params=_params(("arbitrary", "arbitrary")),
        name="post",
    )(x, ylru, yatt, gatt, wo_a, wo_b, gcross, wxq, kx, vx, wxo, gffn, wr_t, br_t, tri)


def _row_copy(src_ref, src_row, dst_ref, dst_row, sem):
    return pltpu.make_async_copy(src_ref.at[pl.ds(src_row, 1)], dst_ref.at[pl.ds(dst_row, 1)], sem)


def _dispatch_kernel(dest_ref, hn_ref, xin_ref, xout_ref, sem):
    del xin_ref
    td = hn_ref.shape[0]
    n = dest_ref.shape[0] // EXPERT_TOPK
    base = pl.program_id(0) * td

    def start(r, carry):
        for slot in range(EXPERT_TOPK):
            _row_copy(hn_ref, r, xout_ref, dest_ref[slot * n + base + r], sem).start()
        return carry

    def wait(r, carry):
        for slot in range(EXPERT_TOPK):
            _row_copy(hn_ref, r, xout_ref, dest_ref[slot * n + base + r], sem).wait()
        return carry

    lax.fori_loop(0, td, start, 0)
    lax.fori_loop(0, td, wait, 0)


def _dispatch(dest_flat, hn2, x_buf0, td):
    n, d = hn2.shape
    return pl.pallas_call(
        _dispatch_kernel,
        grid_spec=pltpu.PrefetchScalarGridSpec(
            num_scalar_prefetch=1,
            grid=(n // td,),
            in_specs=[pl.BlockSpec((td, d), lambda i, dest: (i, 0)),
                      pl.BlockSpec(memory_space=pl.ANY)],
            out_specs=pl.BlockSpec(memory_space=pl.ANY),
            scratch_shapes=[pltpu.SemaphoreType.DMA(())]),
        out_shape=jax.ShapeDtypeStruct(x_buf0.shape, x_buf0.dtype),
        input_output_aliases={2: 0},
        compiler_params=_params(("arbitrary",)),
        name="dispatch",
    )(dest_flat, hn2, x_buf0)


def _experts_kernel(be_ref, nu_ref, x_ref, wgu_ref, wd_ref, y_ref):
    j = pl.program_id(0)
    f = wd_ref.shape[1]

    @pl.when(j < nu_ref[0])
    def _():
        gu = jnp.dot(x_ref[...].astype(BF16), wgu_ref[0], preferred_element_type=F32)
        act = jax.nn.silu(gu[:, 0:f]) * gu[:, f:2 * f]
        y_ref[...] = jnp.dot(act.astype(BF16), wd_ref[0], preferred_element_type=F32)

    @pl.when(j >= nu_ref[0])
    def _():
        y_ref[...] = jnp.zeros_like(y_ref)


def _experts(blk_expert, n_used, x_buf, w_gate_up, w_down):
    p, d = x_buf.shape
    f = w_down.shape[1]
    r = EXPERT_ROWS
    return pl.pallas_call(
        _experts_kernel,
        grid_spec=pltpu.PrefetchScalarGridSpec(
            num_scalar_prefetch=2,
            grid=(p // r,),
            in_specs=[pl.BlockSpec((r, d), lambda j, be, nu: (jnp.minimum(j, nu[0] - 1), 0)),
                      pl.BlockSpec((1, d, 2 * f), lambda j, be, nu: (be[j], 0, 0)),
                      pl.BlockSpec((1, f, d), lambda j, be, nu: (be[j], 0, 0))],
            out_specs=pl.BlockSpec((r, d), lambda j, be, nu: (j, 0))),
        out_shape=jax.ShapeDtypeStruct((p, d), F32),
        compiler_params=_params(("arbitrary",)),
        name="experts",
    )(blk_expert, n_used, x_buf, w_gate_up, w_down)


def _combine_kernel(dest_ref, h2_ref, gcol_ref, gfin_ref, y_ref, o_ref, buf0_ref, buf1_ref, sem):
    tc = h2_ref.shape[0]
    n = dest_ref.shape[0] // EXPERT_TOPK
    base = pl.program_id(0) * tc
    bufs = (buf0_ref, buf1_ref)

    def start(r, carry):
        for slot in range(EXPERT_TOPK):
            _row_copy(y_ref, dest_ref[slot * n + base + r], bufs[slot], r, sem).start()
        return carry

    def wait(r, carry):
        for slot in range(EXPERT_TOPK):
            _row_copy(y_ref, dest_ref[slot * n + base + r], bufs[slot], r, sem).wait()
        return carry

    lax.fori_loop(0, tc, start, 0)
    lax.fori_loop(0, tc, wait, 0)
    g = gcol_ref[...]
    moe = g[:, 0:1] * buf0_ref[...] + g[:, 1:2] * buf1_ref[...]
    o_ref[...] = _rms(h2_ref[...] + moe, gfin_ref[...])


def _combine(dest_flat, h2, gcol, gfin, y_buf, tc):
    n, d = h2.shape
    return pl.pallas_call(
        _combine_kernel,
        grid_spec=pltpu.PrefetchScalarGridSpec(
            num_scalar_prefetch=1,
            grid=(n // tc,),
            in_specs=[pl.BlockSpec((tc, d), lambda i, dest: (i, 0)),
                      pl.BlockSpec((tc, LANES), lambda i, dest: (i, 0)),
                      pl.BlockSpec((1, d), lambda i, dest: (0, 0)),
                      pl.BlockSpec(memory_space=pl.ANY)],
            out_specs=pl.BlockSpec((tc, d), lambda i, dest: (i, 0)),
            scratch_shapes=[pltpu.VMEM((tc, d), F32), pltpu.VMEM((tc, d), F32),
                            pltpu.SemaphoreType.DMA(())]),
        out_shape=jax.ShapeDtypeStruct((n, d), F32),
        compiler_params=_params(("arbitrary",)),
        name="combine",
    )(dest_flat, h2, gcol, gfin, y_buf)


def _rope_tables(s, n_heads):
    half = ROT_DIM // 2
    inv_freq = ROPE_THETA ** (-jnp.arange(0, ROT_DIM, 2, dtype=F32) / ROT_DIM)
    ang = jnp.arange(s, dtype=jnp.int32).astype(F32)[:, None] * inv_freq[None, :]
    cos, sin = jnp.cos(ang), jnp.sin(ang)
    zeros = lambda w: jnp.zeros((s, w), F32)
    cos_h = jnp.concatenate([cos, cos, jnp.ones((s, HEAD_DIM - ROT_DIM), F32)], axis=1)
    sina_h = jnp.concatenate([-sin, zeros(HEAD_DIM - half)], axis=1)
    sinb_h = jnp.concatenate([zeros(half), sin, zeros(HEAD_DIM - ROT_DIM)], axis=1)
    tile = lambda t: jnp.tile(t, (1, n_heads))
    return tile(cos_h), tile(sina_h), tile(sinb_h)


def _block_diag(w):
    nblk, bi, bo = w.shape
    eye = jnp.eye(nblk, dtype=w.dtype)
    return jnp.einsum('hij,hg->higj', w, eye).reshape(nblk * bi, nblk * bo)


def _layer(h, mem, norm_mix, w_in, conv_w, conv_b, w_rg, b_rg, w_ig, b_ig, lru_lambda, norm_lru_out,
           norm_attn_out, w_out, norm_cross, norm_mem, w_xq, w_xkv, w_xo, norm_ffn, w_router_group,
           b_router_group, w_router_expert, b_router_expert, w_gate_up, w_down, norm_out):
    b, s, d = h.shape
    c = conv_w.shape[-1]
    n = b * s
    row = lambda t: t.reshape(1, -1)
    ts = MOBA_BLOCK
    tm = 512 if s % 512 == 0 else MOBA_BLOCK
    assert s % ts == 0 and s % tm == 0 and c == N_HEADS * HEAD_DIM

    kx, vx = _memkv(mem, row(norm_mem), w_xkv.astype(BF16))

    w_gates = jnp.concatenate([_block_diag(w_rg), _block_diag(w_ig)], axis=1).astype(BF16)
    b_gates = jnp.concatenate([b_rg, b_ig]).reshape(1, -1)
    cos_t, sina_t, sinb_t = _rope_tables(s, N_HEADS)
    ylru, q, k, v = _mixer_in(h, row(norm_mix), w_in.astype(BF16), conv_w, row(conv_b), w_gates, b_gates,
                              row(lru_lambda), row(norm_lru_out), cos_t, sina_t, sinb_t, ts)
    yatt = _moba(q, k, v)

    w_out_b = w_out.astype(BF16)
    wr_t = jnp.zeros((ROUTER_ROWS, d), F32)
    wr_t = wr_t.at[0:N_GROUPS].set(w_router_group.T).at[SUBLANES:].set(w_router_expert.T)
    br_t = jnp.zeros((ROUTER_ROWS, 1), F32)
    br_t = br_t.at[0:N_GROUPS, 0].set(b_router_group).at[SUBLANES:, 0].set(b_router_expert)
    tri = (jnp.arange(tm)[:, None] < jnp.arange(tm)[None, :]).astype(BF16)
    h2, hn2, eid, gate, gcol, rank, counts = _post(
        h, ylru, yatt, row(norm_attn_out), w_out_b[:c], w_out_b[c:], row(norm_cross), w_xq.astype(BF16),
        kx, vx, w_xo.astype(BF16), row(norm_ffn), wr_t, br_t, tri, tm)
    del gate

    counts = counts[:, 0].astype(jnp.int32)
    padded = (counts + EXPERT_ROWS - 1) // EXPERT_ROWS * EXPERT_ROWS
    pad_ends = jnp.cumsum(padded)
    pad_starts = pad_ends - padded
    dest_flat = (pad_starts[eid] + rank).reshape(-1)
    n_blocks = n * EXPERT_TOPK // EXPERT_ROWS + N_EXPERTS
    blk_expert = jnp.minimum(
        jnp.searchsorted(pad_ends, jnp.arange(n_blocks, dtype=jnp.int32) * EXPERT_ROWS, side='right'),
        N_EXPERTS - 1).astype(jnp.int32)
    n_used = (pad_ends[-1:] // EXPERT_ROWS).astype(jnp.int32)

    x_buf = _dispatch(dest_flat, hn2, jnp.zeros((n_blocks * EXPERT_ROWS, d), F32), MOBA_BLOCK)
    y_buf = _experts(blk_expert, n_used, x_buf, w_gate_up.astype(BF16), w_down.astype(BF16))
    out = _combine(dest_flat, h2, gcol, row(norm_out), y_buf, MOBA_BLOCK)
    return out.reshape(b, s, d)


def kernel(x, mem, norm_mix, w_in, conv_w, conv_b, w_rg, b_rg, w_ig, b_ig, lru_lambda, norm_lru_out, norm_attn_out,
           w_out, norm_cross, norm_mem, w_xq, w_xkv, w_xo, norm_ffn, w_router_group, b_router_group,
           w_router_expert, b_router_expert, w_gate_up, w_down, norm_final):
    depth = norm_mix.shape[0]
    assert depth == 1, "the fused final norm assumes a single layer"
    l = 0
    return _layer(x, mem, norm_mix[l], w_in[l], conv_w[l], conv_b[l], w_rg[l], b_rg[l], w_ig[l], b_ig[l],
                  lru_lambda[l], norm_lru_out[l], norm_attn_out[l], w_out[l], norm_cross[l], norm_mem[l],
                  w_xq[l], w_xkv[l], w_xo[l], norm_ffn[l], w_router_group[l], b_router_group[l],
                  w_router_expert[l], b_router_expert[l], w_gate_up[l], w_down[l], norm_final)
```

```python
import functools

import jax
import jax.numpy as jnp
import numpy as np
from jax import lax
from jax.experimental import pallas as pl
from jax.experimental.pallas import tpu as pltpu

F32 = jnp.float32
BF16 = jnp.bfloat16

N_LRU_BLOCKS = 8
CONV_WIDTH = 4
RG_C = 8.0
N_HEADS = 8
HEAD_DIM = 64
ROT_DIM = HEAD_DIM // 4
ROPE_THETA = 500000.0
MOBA_BLOCK = 256
MOBA_TOPK = 3
N_XHEADS = 4
N_GROUPS = 4
EXPERTS_PER_GROUP = 8
N_EXPERTS = N_GROUPS * EXPERTS_PER_GROUP
EXPERT_TOPK = 2
EPS = 1e-6

LANES = 128
SUBLANES = 8
NEG_BIG = -1e30
EXPERT_ROWS = 256
ROUTER_ROWS = SUBLANES + N_EXPERTS
VMEM_LIMIT = 56 * 1024 * 1024

NT_DIMS = (((1,), (1,)), ((), ()))


def _rms(x, g):
    return x * lax.rsqrt(jnp.mean(x * x, axis=-1, keepdims=True) + EPS) * g


def _params(sem):
    return pltpu.CompilerParams(dimension_semantics=sem, vmem_limit_bytes=VMEM_LIMIT)


def _memkv_kernel(mem_ref, g_ref, w_ref, k_ref, v_ref):
    d = mem_ref.shape[-1]
    mn = _rms(mem_ref[0], g_ref[...]).astype(BF16)
    kv = jnp.dot(mn, w_ref[...], preferred_element_type=F32)
    k_ref[0] = kv[:, :d].astype(BF16)
    v_ref[0] = kv[:, d:].astype(BF16)


def _memkv(mem, g, w_xkv):
    b, m, d = mem.shape
    return pl.pallas_call(
        _memkv_kernel,
        grid=(b,),
        in_specs=[pl.BlockSpec((1, m, d), lambda i: (i, 0, 0)),
                  pl.BlockSpec((1, d), lambda i: (0, 0)),
                  pl.BlockSpec((d, 2 * d), lambda i: (0, 0))],
        out_specs=[pl.BlockSpec((1, m, d), lambda i: (i, 0, 0)),
                   pl.BlockSpec((1, m, d), lambda i: (i, 0, 0))],
        out_shape=[jax.ShapeDtypeStruct((b, m, d), BF16)] * 2,
        compiler_params=_params(("arbitrary",)),
        name="memkv",
    )(mem, g, w_xkv)


def _mixer_in_kernel(x_ref, gmix_ref, win_ref, convw_ref, convb_ref, wg_ref, bg_ref, lam_ref, glru_ref,
                     cos_ref, sina_ref, sinb_ref,
                     ylru_ref, q_ref, k_ref, v_ref, xpad_ref, hcar_ref):
    ts = x_ref.shape[1]
    c = ylru_ref.shape[-1]
    si = pl.program_id(1)

    @pl.when(si == 0)
    def _():
        xpad_ref[0:SUBLANES, :] = jnp.zeros((SUBLANES, c), F32)
        hcar_ref[...] = jnp.zeros_like(hcar_ref)

    xn = _rms(x_ref[0], gmix_ref[...]).astype(BF16)
    proj = jnp.dot(xn, win_ref[...], preferred_element_type=F32)
    x_lru = proj[:, 0:c]
    g_lru = proj[:, c:2 * c]
    q = proj[:, 2 * c:3 * c]
    k = proj[:, 3 * c:4 * c]
    v = proj[:, 4 * c:5 * c]

    xpad_ref[SUBLANES:SUBLANES + ts, :] = x_lru
    cw = convw_ref[...]
    xc = convb_ref[...] + cw[3:4] * x_lru
    for j in range(1, CONV_WIDTH):
        xc = xc + cw[3 - j:4 - j] * xpad_ref[SUBLANES - j:SUBLANES - j + ts, :]
    xpad_ref[0:SUBLANES, :] = x_lru[ts - SUBLANES:ts, :]

    gates = jnp.dot(xc.astype(BF16), wg_ref[...], preferred_element_type=F32) + bg_ref[...]
    r = jax.nn.sigmoid(gates[:, 0:c])
    i = jax.nn.sigmoid(gates[:, c:2 * c])
    neg_lam = -lam_ref[...]
    softplus = jnp.maximum(neg_lam, 0.0) + jnp.log1p(jnp.exp(-jnp.abs(neg_lam)))
    log_a = -RG_C * r * softplus
    a = jnp.exp(log_a)
    u = jnp.sqrt(-jnp.tanh(log_a) * (a * a + 1.0)) * (i * xc)

    row = lax.broadcasted_iota(jnp.int32, (ts, c), 0)
    step = 1
    while step < ts:
        keep = row >= step
        a_sh = jnp.where(keep, pltpu.roll(a, step, axis=0), 1.0)
        u_sh = jnp.where(keep, pltpu.roll(u, step, axis=0), 0.0)
        u = a * u_sh + u
        a = a * a_sh
        step *= 2
    h = a * hcar_ref[...] + u
    hcar_ref[...] = h[ts - 1:ts, :]

    y = h * jax.nn.gelu(g_lru)
    ylru_ref[0] = _rms(y, glru_ref[...]).astype(BF16)

    wide = lambda t_ref: jnp.concatenate([t_ref[...]] * (c // LANES), axis=1)
    cos, sina, sinb = wide(cos_ref), wide(sina_ref), wide(sinb_ref)
    half = ROT_DIM // 2

    def rope(t):
        return t * cos + pltpu.roll(t, c - half, axis=1) * sina + pltpu.roll(t, half, axis=1) * sinb

    q_ref[0] = rope(q)
    k_ref[0] = rope(k)
    v_ref[0] = v.astype(BF16)


def _mixer_in(x, gmix, w_in, conv_w, conv_b, w_gates, b_gates, lam, glru, cos_t, sina_t, sinb_t, ts):
    b, s, d = x.shape
    c = conv_w.shape[-1]
    full = lambda shape: pl.BlockSpec(shape, lambda bi, si: (0,) * len(shape))
    tab = pl.BlockSpec((ts, LANES), lambda bi, si: (si, 0))
    seq = lambda: pl.BlockSpec((1, ts, c), lambda bi, si: (bi, si, 0))
    return pl.pallas_call(
        _mixer_in_kernel,
        grid=(b, s // ts),
        in_specs=[pl.BlockSpec((1, ts, d), lambda bi, si: (bi, si, 0)),
                  full((1, d)), full(w_in.shape), full(conv_w.shape), full((1, c)),
                  full(w_gates.shape), full((1, 2 * c)), full((1, c)), full((1, c)),
                  tab, tab, tab],
        out_specs=[seq(), seq(), seq(), seq()],
        out_shape=[jax.ShapeDtypeStruct((b, s, c), BF16),
                   jax.ShapeDtypeStruct((b, s, c), F32),
                   jax.ShapeDtypeStruct((b, s, c), F32),
                   jax.ShapeDtypeStruct((b, s, c), BF16)],
        scratch_shapes=[pltpu.VMEM((ts + SUBLANES, c), F32), pltpu.VMEM((1, c), F32)],
        compiler_params=_params(("arbitrary", "arbitrary")),
        name="mixer_in",
    )(x, gmix, w_in, conv_w, conv_b, w_gates, b_gates, lam, glru, cos_t, sina_t, sinb_t)


def _moba_kernel(q_ref, k_ref, v_ref, o_ref, qa_ref, ka_ref):
    s_len = q_ref.shape[1]
    nb = s_len // MOBA_BLOCK
    n_pick = min(MOBA_TOPK, nb)
    q = q_ref[0]
    k = k_ref[0]
    lane = lax.broadcasted_iota(jnp.int32, (1, LANES), 1)
    kmean = jnp.concatenate(
        [jnp.sum(k[n * MOBA_BLOCK:(n + 1) * MOBA_BLOCK], axis=0, keepdims=True) for n in range(nb)],
        axis=0) * (1.0 / MOBA_BLOCK)

    blk_row = lax.broadcasted_iota(jnp.int32, (nb, s_len), 0)
    q_blk = lax.broadcasted_iota(jnp.int32, (nb, s_len), 1) // MOBA_BLOCK
    past = blk_row < q_blk
    key_blk = lax.broadcasted_iota(jnp.int32, (s_len, LANES), 0) // MOBA_BLOCK
    lane_full = lax.broadcasted_iota(jnp.int32, (s_len, LANES), 1)

    for h in range(2):
        own = (lane >= h * HEAD_DIM) & (lane < (h + 1) * HEAD_DIM)
        off = (1 - h) * HEAD_DIM
        gate = lax.dot_general(jnp.where(own, kmean, 0.0), q, NT_DIMS,
                               precision=lax.Precision.HIGHEST, preferred_element_type=F32)
        gate = jnp.where(past, gate, -jnp.inf)
        rank = jnp.zeros((nb, s_len), jnp.int32)
        for m in range(nb):
            gm = gate[m:m + 1, :]
            ahead = (gm > gate) | ((gm == gate) & (m < blk_row))
            rank = rank + ahead.astype(jnp.int32)
        allowed = ((rank < n_pick) & past) | (blk_row == q_blk)
        bias = jnp.where(allowed, 0.0, NEG_BIG)
        pieces = []
        if off:
            pieces.append(jnp.zeros((off, s_len), F32))
        pieces.append(bias)
        if LANES - off - nb:
            pieces.append(jnp.zeros((LANES - off - nb, s_len), F32))
        bias_lanes = jnp.concatenate(pieces, axis=0).T
        qa_ref[h] = jnp.where(own, q * (HEAD_DIM ** -0.5), bias_lanes).astype(BF16)
        onehot = (lane_full - off == key_blk).astype(F32)
        ka_ref[h] = jnp.where(own, k, onehot).astype(BF16)

    own0 = lane < HEAD_DIM
    for qi in range(nb):
        nk = (qi + 1) * MOBA_BLOCK
        col = lax.broadcasted_iota(jnp.int32, (MOBA_BLOCK, nk), 1)
        qpos = lax.broadcasted_iota(jnp.int32, (MOBA_BLOCK, nk), 0) + qi * MOBA_BLOCK
        causal = col <= qpos
        vv = v_ref[0, 0:nk, :]
        outs = []
        for h in range(2):
            qa = qa_ref[h, qi * MOBA_BLOCK:(qi + 1) * MOBA_BLOCK, :]
            s = lax.dot_general(qa, ka_ref[h, 0:nk, :], NT_DIMS, preferred_element_type=F32)
            s = jnp.where(causal, s, NEG_BIG)
            p = jnp.exp(s - jnp.max(s, axis=-1, keepdims=True))
            l = jnp.sum(p, axis=-1, keepdims=True)
            outs.append(jnp.dot(p.astype(BF16), vv, preferred_element_type=F32) / l)
        o_ref[0, qi * MOBA_BLOCK:(qi + 1) * MOBA_BLOCK, :] = jnp.where(own0, outs[0], outs[1])


def _moba(q, k, v):
    b, s, c = q.shape
    spec = lambda: pl.BlockSpec((1, s, LANES), lambda bi, hi: (bi, 0, hi))
    return pl.pallas_call(
        _moba_kernel,
        grid=(b, c // LANES),
        in_specs=[spec(), spec(), spec()],
        out_specs=spec(),
        out_shape=jax.ShapeDtypeStruct((b, s, c), F32),
        scratch_shapes=[pltpu.VMEM((2, s, LANES), BF16), pltpu.VMEM((2, s, LANES), BF16)],
        compiler_params=_params(("arbitrary", "arbitrary")),
        name="moba",
    )(q, k, v)


def _post_kernel(x_ref, ylru_ref, yatt_ref, gatt_ref, woa_ref, wob_ref, gcross_ref, wxq_ref, kx_ref, vx_ref,
                 wxo_ref, gffn_ref, wr_ref, br_ref, tri_ref,
                 h2_ref, hn2_ref, eid_ref, gate_ref, gcol_ref, rank_ref, cnt_ref, carry_ref):
    tm = x_ref.shape[1]
    d = x_ref.shape[2]
    xd = d // N_XHEADS
    first = (pl.program_id(0) == 0) & (pl.program_id(1) == 0)

    @pl.when(first)
    def _():
        carry_ref[...] = jnp.zeros_like(carry_ref)

    ya = _rms(yatt_ref[0], gatt_ref[...]).astype(BF16)
    mix = (jnp.dot(ylru_ref[0], woa_ref[...], preferred_element_type=F32)
           + jnp.dot(ya, wob_ref[...], preferred_element_type=F32))
    h1 = x_ref[0] + mix

    hn = _rms(h1, gcross_ref[...]).astype(BF16)
    qx = jnp.dot(hn, wxq_ref[...], preferred_element_type=F32)
    heads = []
    for hh in range(N_XHEADS):
        qh = (qx[:, hh * xd:(hh + 1) * xd] * (xd ** -0.5)).astype(BF16)
        s = lax.dot_general(qh, kx_ref[0, :, hh * xd:(hh + 1) * xd], NT_DIMS, preferred_element_type=F32)
        p = jnp.exp(s - jnp.max(s, axis=-1, keepdims=True))
        l = jnp.sum(p, axis=-1, keepdims=True)
        o = jnp.dot(p.astype(BF16), vx_ref[0, :, hh * xd:(hh + 1) * xd], preferred_element_type=F32) / l
        heads.append(o.astype(BF16))
    h2 = h1 + jnp.dot(jnp.concatenate(heads, axis=1), wxo_ref[...], preferred_element_type=F32)
    hn2 = _rms(h2, gffn_ref[...])
    h2_ref[...] = h2
    hn2_ref[...] = hn2

    logits = lax.dot_general(wr_ref[...], hn2, NT_DIMS, precision=lax.Precision.HIGHEST,
                             preferred_element_type=F32) + br_ref[...]
    gl = logits[0:N_GROUPS]
    gmax = jnp.max(gl, axis=0, keepdims=True)
    gi = lax.broadcasted_iota(jnp.int32, gl.shape, 0)
    grp = jnp.min(jnp.where(gl == gmax, gi, N_GROUPS), axis=0, keepdims=True)
    g_w = 1.0 / jnp.sum(jnp.exp(gl - gmax), axis=0, keepdims=True)
    el = jnp.zeros((EXPERTS_PER_GROUP, tm), F32)
    for g in range(N_GROUPS):
        lo = SUBLANES + g * EXPERTS_PER_GROUP
        el = jnp.where(grp == g, logits[lo:lo + EXPERTS_PER_GROUP], el)
    ee = jnp.exp(el - jnp.max(el, axis=0, keepdims=True))
    ep = ee / jnp.sum(ee, axis=0, keepdims=True)
    ei = lax.broadcasted_iota(jnp.int32, ep.shape, 0)
    p1 = jnp.max(ep, axis=0, keepdims=True)
    i1 = jnp.min(jnp.where(ep == p1, ei, EXPERTS_PER_GROUP), axis=0, keepdims=True)
    ep_rest = jnp.where(ei == i1, -1.0, ep)
    p2 = jnp.max(ep_rest, axis=0, keepdims=True)
    i2 = jnp.min(jnp.where(ep_rest == p2, ei, EXPERTS_PER_GROUP), axis=0, keepdims=True)
    den = p1 + p2
    gate1 = g_w * p1 / den
    gate2 = g_w * p2 / den
    e1 = grp * EXPERTS_PER_GROUP + i1
    e2 = grp * EXPERTS_PER_GROUP + i2
    eid_ref[0:1, :] = e1
    eid_ref[1:2, :] = e2
    gate_ref[0:1, :] = gate1
    gate_ref[1:2, :] = gate2
    li = lax.broadcasted_iota(jnp.int32, (LANES, tm), 0)
    gcol_ref[...] = jnp.where(li == 0, gate1, jnp.where(li == 1, gate2, 0.0)).T

    xi = lax.broadcasted_iota(jnp.int32, (N_EXPERTS, tm), 0)
    oh1 = xi == e1
    oh2 = xi == e2
    cnt = oh1.astype(F32) + oh2.astype(F32)
    before = jnp.dot(cnt.astype(BF16), tri_ref[...], preferred_element_type=F32) + carry_ref[:, 0:1]
    rank_ref[0:1, :] = jnp.sum(jnp.where(oh1, before, 0.0), axis=0, keepdims=True).astype(jnp.int32)
    rank_ref[1:2, :] = jnp.sum(jnp.where(oh2, before, 0.0), axis=0, keepdims=True).astype(jnp.int32)
    carry_ref[...] = carry_ref[...] + jnp.sum(cnt, axis=1, keepdims=True)
    cnt_ref[...] = carry_ref[...]


def _post(x, ylru, yatt, gatt, wo_a, wo_b, gcross, wxq, kx, vx, wxo, gffn, wr_t, br_t, tri, tm):
    b, s, d = x.shape
    c = ylru.shape[-1]
    m = kx.shape[1]
    n = b * s
    nt = s // tm
    full = lambda shape: pl.BlockSpec(shape, lambda bi, si: (0,) * len(shape))
    tok = lambda rows: pl.BlockSpec((rows, tm), lambda bi, si: (0, bi * nt + si))
    return pl.pallas_call(
        _post_kernel,
        grid=(b, nt),
        in_specs=[pl.BlockSpec((1, tm, d), lambda bi, si: (bi, si, 0)),
                  pl.BlockSpec((1, tm, c), lambda bi, si: (bi, si, 0)),
                  pl.BlockSpec((1, tm, c), lambda bi, si: (bi, si, 0)),
                  full((1, c)), full((c, d)), full((c, d)), full((1, d)), full((d, d)),
                  pl.BlockSpec((1, m, d), lambda bi, si: (bi, 0, 0)),
                  pl.BlockSpec((1, m, d), lambda bi, si: (bi, 0, 0)),
                  full((d, d)), full((1, d)), full(wr_t.shape), full(br_t.shape), full((tm, tm))],
        out_specs=[pl.BlockSpec((tm, d), lambda bi, si: (bi * nt + si, 0)),
                   pl.BlockSpec((tm, d), lambda bi, si: (bi * nt + si, 0)),
                   tok(EXPERT_TOPK), tok(EXPERT_TOPK),
                   pl.BlockSpec((tm, LANES), lambda bi, si: (bi * nt + si, 0)),
                   tok(EXPERT_TOPK), full((N_EXPERTS, LANES))],
        out_shape=[jax.ShapeDtypeStruct((n, d), F32),
                   jax.ShapeDtypeStruct((n, d), F32),
                   jax.ShapeDtypeStruct((EXPERT_TOPK, n), jnp.int32),
                   jax.ShapeDtypeStruct((EXPERT_TOPK, n), F32),
                   jax.ShapeDtypeStruct((n, LANES), F32),
                   jax.ShapeDtypeStruct((EXPERT_TOPK, n), jnp.int32),
                   jax.ShapeDtypeStruct((N_EXPERTS, LANES), F32)],
        scratch_shapes=[pltpu.VMEM((N_EXPERTS, LANES), F32)],
        compiler_params=_params(("arbitrary", "arbitrary")),
        name="post",
    )(x, ylru, yatt, gatt, wo_a, wo_b, gcross, wxq, kx, vx, wxo, gffn, wr_t, br_t, tri)


def _row_copy(src_ref, src_row, dst_ref, dst_row, sem):
    return pltpu.make_async_copy(src_ref.at[pl.ds(src_row, 1)], dst_ref.at[pl.ds(dst_row, 1)], sem)


def _dest_kernel(ps_ref, eid_ref, rank_ref, dest_ref):
    eid = eid_ref[...]
    dest = rank_ref[...]
    for e in range(N_EXPERTS):
        dest = dest + jnp.where(eid == e, ps_ref[e], 0)
    dest_ref[...] = dest


def _dest(pad_starts, eid, rank):
    shape = (eid.size // LANES, LANES)
    spec = pl.BlockSpec(shape, lambda i, ps: (0, 0))
    return pl.pallas_call(
        _dest_kernel,
        grid_spec=pltpu.PrefetchScalarGridSpec(num_scalar_prefetch=1, grid=(1,), in_specs=[spec, spec],
                                               out_specs=spec),
        out_shape=jax.ShapeDtypeStruct(shape, jnp.int32),
        compiler_params=_params(("arbitrary",)),
        name="dest",
    )(pad_starts, eid.reshape(shape), rank.reshape(shape)).reshape(-1)


def _pad_pieces():
    piece = EXPERT_ROWS // 2
    while piece >= SUBLANES:
        yield piece
        piece //= 2


def _dispatch_kernel(dest_ref, fill_ref, hn_ref, xout_ref, zero_ref, sem, zsem):
    td = hn_ref.shape[0]
    n = dest_ref.shape[0] // EXPERT_TOPK
    step = pl.program_id(0)
    base = step * td

    def pad_copies(e, start):
        first = fill_ref[e]
        length = fill_ref[N_EXPERTS + e]
        head = jnp.minimum((-first) & (SUBLANES - 1), length)

        def one(i, carry):
            cp = _row_copy(zero_ref, 0, xout_ref, first + i, zsem)
            cp.start() if start else cp.wait()
            return carry

        lax.fori_loop(0, head, one, 0)
        body = length - head
        for piece in _pad_pieces():
            offset = pl.multiple_of(first + head + (body & ~(2 * piece - 1)), SUBLANES)

            @pl.when((body & piece) != 0)
            def _():
                cp = pltpu.make_async_copy(zero_ref.at[pl.ds(0, piece)], xout_ref.at[pl.ds(offset, piece)], zsem)
                cp.start() if start else cp.wait()

    def tail_copies(start):
        piece = zero_ref.shape[0]

        def one(i, carry):
            cp = pltpu.make_async_copy(zero_ref, xout_ref.at[pl.ds(pl.multiple_of(i * piece, piece), piece)], zsem)
            cp.start() if start else cp.wait()
            return carry

        lax.fori_loop(fill_ref[2 * N_EXPERTS] // piece, xout_ref.shape[0] // piece, one, 0)

    @pl.when(step == 0)
    def _():
        zero_ref[...] = jnp.zeros_like(zero_ref)
        lax.fori_loop(0, N_EXPERTS, lambda e, c: (pad_copies(e, True), c)[1], 0)
        tail_copies(True)

    def start(r, carry):
        for slot in range(EXPERT_TOPK):
            _row_copy(hn_ref, r, xout_ref, dest_ref[slot * n + base + r], sem).start()
        return carry

    lax.fori_loop(0, td, start, 0, unroll=8)
    for slot in range(EXPERT_TOPK):
        pltpu.make_async_copy(hn_ref, xout_ref.at[pl.ds(0, td)], sem).wait()

    @pl.when(step == 0)
    def _():
        lax.fori_loop(0, N_EXPERTS, lambda e, c: (pad_copies(e, False), c)[1], 0)
        tail_copies(False)


def _dispatch(dest_flat, fill, hn2, n_rows, td):
    n, d = hn2.shape
    return pl.pallas_call(
        _dispatch_kernel,
        grid_spec=pltpu.PrefetchScalarGridSpec(
            num_scalar_prefetch=2,
            grid=(n // td,),
            in_specs=[pl.BlockSpec((td, d), lambda i, dest, fill: (i, 0))],
            out_specs=pl.BlockSpec(memory_space=pl.ANY),
            scratch_shapes=[pltpu.VMEM((EXPERT_ROWS // 2, d), F32),
                            pltpu.SemaphoreType.DMA(()), pltpu.SemaphoreType.DMA(())]),
        out_shape=jax.ShapeDtypeStruct((n_rows, d), F32),
        compiler_params=_params(("arbitrary",)),
        name="dispatch",
    )(dest_flat, fill, hn2)


def _experts_kernel(be_ref, nu_ref, x_ref, wgu_ref, wd_ref, y_ref, wgu_bf_ref, wd_bf_ref):
    j = pl.program_id(0)
    f = wd_ref.shape[1]
    used = j < nu_ref[0]

    @pl.when(used & ((j == 0) | (be_ref[j] != be_ref[jnp.maximum(j, 1) - 1])))
    def _():
        wgu_bf_ref[...] = wgu_ref[0].astype(BF16)
        wd_bf_ref[...] = wd_ref[0].astype(BF16)

    @pl.when(used)
    def _():
        gu = jnp.dot(x_ref[...].astype(BF16), wgu_bf_ref[...], preferred_element_type=F32)
        act = jax.nn.silu(gu[:, 0:f]) * gu[:, f:2 * f]
        y_ref[...] = jnp.dot(act.astype(BF16), wd_bf_ref[...], preferred_element_type=F32)

    @pl.when(jnp.logical_not(used))
    def _():
        y_ref[...] = jnp.zeros_like(y_ref)


def _experts(blk_expert, n_used, x_buf, w_gate_up, w_down):
    p, d = x_buf.shape
    f = w_down.shape[1]
    r = EXPERT_ROWS
    return pl.pallas_call(
        _experts_kernel,
        grid_spec=pltpu.PrefetchScalarGridSpec(
            num_scalar_prefetch=2,
            grid=(p // r,),
            in_specs=[pl.BlockSpec((r, d), lambda j, be, nu: (jnp.minimum(j, nu[0] - 1), 0)),
                      pl.BlockSpec((1, d, 2 * f), lambda j, be, nu: (be[j], 0, 0)),
                      pl.BlockSpec((1, f, d), lambda j, be, nu: (be[j], 0, 0))],
            out_specs=pl.BlockSpec((r, d), lambda j, be, nu: (j, 0)),
            scratch_shapes=[pltpu.VMEM((d, 2 * f), BF16), pltpu.VMEM((f, d), BF16)]),
        out_shape=jax.ShapeDtypeStruct((p, d), F32),
        compiler_params=_params(("arbitrary",)),
        name="experts",
    )(blk_expert, n_used, x_buf, w_gate_up, w_down)


def _combine_kernel(dest_ref, h2_ref, gcol_ref, gfin_ref, y_ref, o_ref, buf0_ref, buf1_ref, sem):
    tc = h2_ref.shape[0]
    n = dest_ref.shape[0] // EXPERT_TOPK
    base = pl.program_id(0) * tc
    bufs = (buf0_ref, buf1_ref)

    def start(r, carry):
        for slot in range(EXPERT_TOPK):
            _row_copy(y_ref, dest_ref[slot * n + base + r], bufs[slot], r, sem).start()
        return carry

    lax.fori_loop(0, tc, start, 0, unroll=8)
    for slot in range(EXPERT_TOPK):
        pltpu.make_async_copy(y_ref.at[pl.ds(0, tc)], bufs[slot], sem).wait()
    g = gcol_ref[...]
    moe = g[:, 0:1] * buf0_ref[...] + g[:, 1:2] * buf1_ref[...]
    o_ref[...] = _rms(h2_ref[...] + moe, gfin_ref[...])


def _combine(dest_flat, h2, gcol, gfin, y_buf, tc):
    n, d = h2.shape
    return pl.pallas_call(
        _combine_kernel,
        grid_spec=pltpu.PrefetchScalarGridSpec(
            num_scalar_prefetch=1,
            grid=(n // tc,),
            in_specs=[pl.BlockSpec((tc, d), lambda i, dest: (i, 0)),
                      pl.BlockSpec((tc, LANES), lambda i, dest: (i, 0)),
                      pl.BlockSpec((1, d), lambda i, dest: (0, 0)),
                      pl.BlockSpec(memory_space=pl.ANY)],
            out_specs=pl.BlockSpec((tc, d), lambda i, dest: (i, 0)),
            scratch_shapes=[pltpu.VMEM((tc, d), F32), pltpu.VMEM((tc, d), F32),
                            pltpu.SemaphoreType.DMA(())]),
        out_shape=jax.ShapeDtypeStruct((n, d), F32),
        compiler_params=_params(("arbitrary",)),
        name="combine",
    )(dest_flat, h2, gcol, gfin, y_buf)


def _rope_tables(s):
    half = ROT_DIM // 2
    inv_freq = (ROPE_THETA ** (-np.arange(0, ROT_DIM, 2, dtype=np.float32) / ROT_DIM)).astype(np.float32)
    ang = np.arange(s, dtype=np.float32)[:, None] * inv_freq[None, :]
    cos, sin = np.cos(ang), np.sin(ang)
    zeros = lambda w: np.zeros((s, w), np.float32)
    cos_h = np.concatenate([cos, cos, np.ones((s, HEAD_DIM - ROT_DIM), np.float32)], axis=1)
    sina_h = np.concatenate([-sin, zeros(HEAD_DIM - half)], axis=1)
    sinb_h = np.concatenate([zeros(half), sin, zeros(HEAD_DIM - ROT_DIM)], axis=1)
    tile = lambda t: jnp.asarray(np.tile(t, (1, LANES // HEAD_DIM)), F32)
    return tile(cos_h), tile(sina_h), tile(sinb_h)


def _block_diag(w):
    nblk, bi, bo = w.shape
    eye = jnp.eye(nblk, dtype=w.dtype)
    return jnp.einsum('hij,hg->higj', w, eye).reshape(nblk * bi, nblk * bo)


def _layer(h, mem, norm_mix, w_in, conv_w, conv_b, w_rg, b_rg, w_ig, b_ig, lru_lambda, norm_lru_out,
           norm_attn_out, w_out, norm_cross, norm_mem, w_xq, w_xkv, w_xo, norm_ffn, w_router_group,
           b_router_group, w_router_expert, b_router_expert, w_gate_up, w_down, norm_out):
    b, s, d = h.shape
    c = conv_w.shape[-1]
    n = b * s
    row = lambda t: t.reshape(1, -1)
    ts = MOBA_BLOCK
    tm = 512 if s % 512 == 0 else MOBA_BLOCK
    assert s % ts == 0 and s % tm == 0 and c == N_HEADS * HEAD_DIM

    kx, vx = _memkv(mem, row(norm_mem), w_xkv.astype(BF16))

    w_gates = jnp.concatenate([_block_diag(w_rg), _block_diag(w_ig)], axis=1).astype(BF16)
    b_gates = jnp.concatenate([b_rg, b_ig]).reshape(1, -1)
    cos_t, sina_t, sinb_t = _rope_tables(s)
    ylru, q, k, v = _mixer_in(h, row(norm_mix), w_in.astype(BF16), conv_w, row(conv_b), w_gates, b_gates,
                              row(lru_lambda), row(norm_lru_out), cos_t, sina_t, sinb_t, ts)
    yatt = _moba(q, k, v)

    w_out_b = w_out.astype(BF16)
    wr_t = jnp.zeros((ROUTER_ROWS, d), F32)
    wr_t = wr_t.at[0:N_GROUPS].set(w_router_group.T).at[SUBLANES:].set(w_router_expert.T)
    br_t = jnp.zeros((ROUTER_ROWS, 1), F32)
    br_t = br_t.at[0:N_GROUPS, 0].set(b_router_group).at[SUBLANES:, 0].set(b_router_expert)
    tri = jnp.asarray(np.triu(np.ones((tm, tm), np.float32), 1), BF16)
    h2, hn2, eid, gate, gcol, rank, counts = _post(
        h, ylru, yatt, row(norm_attn_out), w_out_b[:c], w_out_b[c:], row(norm_cross), w_xq.astype(BF16),
        kx, vx, w_xo.astype(BF16), row(norm_ffn), wr_t, br_t, tri, tm)
    del gate

    counts = counts[:, 0].astype(jnp.int32)
    padded = (counts + EXPERT_ROWS - 1) // EXPERT_ROWS * EXPERT_ROWS
    pad_ends = jnp.cumsum(padded)
    pad_starts = pad_ends - padded
    dest_flat = _dest(pad_starts.astype(jnp.int32), eid, rank)
    n_blocks = n * EXPERT_TOPK // EXPERT_ROWS + N_EXPERTS
    blk_first = jnp.arange(n_blocks, dtype=jnp.int32) * EXPERT_ROWS
    blk_expert = jnp.minimum(jnp.sum(blk_first[:, None] >= pad_ends[None, :], axis=1), N_EXPERTS - 1).astype(jnp.int32)
    n_used = (pad_ends[-1:] // EXPERT_ROWS).astype(jnp.int32)
    fill = jnp.concatenate([pad_starts + counts, padded - counts, pad_ends[-1:]]).astype(jnp.int32)

    x_buf = _dispatch(dest_flat, fill, hn2, n_blocks * EXPERT_ROWS, MOBA_BLOCK)
    y_buf = _experts(blk_expert, n_used, x_buf, w_gate_up, w_down)
    out = _combine(dest_flat, h2, gcol, row(norm_out), y_buf, MOBA_BLOCK)
    return out.reshape(b, s, d)


def kernel(x, mem, norm_mix, w_in, conv_w, conv_b, w_rg, b_rg, w_ig, b_ig, lru_lambda, norm_lru_out, norm_attn_out,
           w_out, norm_cross, norm_mem, w_xq, w_xkv, w_xo, norm_ffn, w_router_group, b_router_group,
           w_router_expert, b_router_expert, w_gate_up, w_down, norm_final):
    depth = norm_mix.shape[0]
    assert depth == 1, "the fused final norm assumes a single layer"
    l = 0
    return _layer(x, mem, norm_mix[l], w_in[l], conv_w[l], conv_b[l], w_rg[l], b_rg[l], w_ig[l], b_ig[l],
                  lru_lambda[l], norm_lru_out[l], norm_attn_out[l], w_out[l], norm_cross[l], norm_mem[l],
                  w_xq[l], w_xkv[l], w_xo[l], norm_ffn[l], w_router_group[l], b_router_group[l],
                  w_router_expert[l], b_router_expert[l], w_gate_up[l], w_down[l], norm_final)
```

```python
import functools

import jax
import jax.numpy as jnp
import numpy as np
from jax import lax
from jax.experimental import pallas as pl
from jax.experimental.pallas import tpu as pltpu

F32 = jnp.float32
BF16 = jnp.bfloat16

N_LRU_BLOCKS = 8
CONV_WIDTH = 4
RG_C = 8.0
N_HEADS = 8
HEAD_DIM = 64
ROT_DIM = HEAD_DIM // 4
ROPE_THETA = 500000.0
MOBA_BLOCK = 256
MOBA_TOPK = 3
N_XHEADS = 4
N_GROUPS = 4
EXPERTS_PER_GROUP = 8
N_EXPERTS = N_GROUPS * EXPERTS_PER_GROUP
EXPERT_TOPK = 2
EPS = 1e-6

LANES = 128
SUBLANES = 8
NEG_BIG = -1e30
EXPERT_ROWS = 512
BF16_ROWS = 2 * SUBLANES
ROUTER_ROWS = -(-(SUBLANES + N_EXPERTS) // BF16_ROWS) * BF16_ROWS
VMEM_LIMIT = 56 * 1024 * 1024

NT_DIMS = (((1,), (1,)), ((), ()))


def _rms(x, g):
    return x * lax.rsqrt(jnp.mean(x * x, axis=-1, keepdims=True) + EPS) * g


def _params(sem):
    return pltpu.CompilerParams(dimension_semantics=sem, vmem_limit_bytes=VMEM_LIMIT)


def _memkv_kernel(mem_ref, g_ref, w_ref, k_ref, v_ref):
    d = mem_ref.shape[-1]
    mn = _rms(mem_ref[0], g_ref[...]).astype(BF16)
    kv = jnp.dot(mn, w_ref[...], preferred_element_type=F32)
    k_ref[0] = kv[:, :d].astype(BF16)
    v_ref[0] = kv[:, d:].astype(BF16)


def _memkv(mem, g, w_xkv):
    b, m, d = mem.shape
    return pl.pallas_call(
        _memkv_kernel,
        grid=(b,),
        in_specs=[pl.BlockSpec((1, m, d), lambda i: (i, 0, 0)),
                  pl.BlockSpec((1, d), lambda i: (0, 0)),
                  pl.BlockSpec((d, 2 * d), lambda i: (0, 0))],
        out_specs=[pl.BlockSpec((1, m, d), lambda i: (i, 0, 0)),
                   pl.BlockSpec((1, m, d), lambda i: (i, 0, 0))],
        out_shape=[jax.ShapeDtypeStruct((b, m, d), BF16)] * 2,
        compiler_params=_params(("arbitrary",)),
        name="memkv",
    )(mem, g, w_xkv)


def _mixer_in_kernel(x_ref, gmix_ref, win_ref, convw_ref, convb_ref, wg_ref, bg_ref, lam_ref, glru_ref,
                     cos_ref, sina_ref, sinb_ref,
                     ylru_ref, q_ref, k_ref, v_ref, xpad_ref, hcar_ref):
    ts = x_ref.shape[1]
    c = ylru_ref.shape[-1]
    si = pl.program_id(1)

    @pl.when(si == 0)
    def _():
        xpad_ref[0:SUBLANES, :] = jnp.zeros((SUBLANES, c), F32)
        hcar_ref[...] = jnp.zeros_like(hcar_ref)

    xn = _rms(x_ref[0], gmix_ref[...]).astype(BF16)
    proj = jnp.dot(xn, win_ref[...], preferred_element_type=F32)
    x_lru = proj[:, 0:c]
    g_lru = proj[:, c:2 * c]
    q = proj[:, 2 * c:3 * c]
    k = proj[:, 3 * c:4 * c]
    v = proj[:, 4 * c:5 * c]

    xpad_ref[SUBLANES:SUBLANES + ts, :] = x_lru
    cw = convw_ref[...]
    xc = convb_ref[...] + cw[3:4] * x_lru
    for j in range(1, CONV_WIDTH):
        xc = xc + cw[3 - j:4 - j] * xpad_ref[SUBLANES - j:SUBLANES - j + ts, :]
    xpad_ref[0:SUBLANES, :] = x_lru[ts - SUBLANES:ts, :]

    gates = jnp.dot(xc.astype(BF16), wg_ref[...], preferred_element_type=F32) + bg_ref[...]
    r = jax.nn.sigmoid(gates[:, 0:c])
    i = jax.nn.sigmoid(gates[:, c:2 * c])
    neg_lam = -lam_ref[...]
    softplus = jnp.maximum(neg_lam, 0.0) + jnp.log1p(jnp.exp(-jnp.abs(neg_lam)))
    log_a = -RG_C * r * softplus
    a = jnp.exp(log_a)
    u = jnp.sqrt(-jnp.tanh(log_a) * (a * a + 1.0)) * (i * xc)

    row = lax.broadcasted_iota(jnp.int32, (ts, c), 0)
    step = 1
    while step < ts:
        keep = row >= step
        a_sh = jnp.where(keep, pltpu.roll(a, step, axis=0), 1.0)
        u_sh = jnp.where(keep, pltpu.roll(u, step, axis=0), 0.0)
        u = a * u_sh + u
        a = a * a_sh
        step *= 2
    h = a * hcar_ref[...] + u
    hcar_ref[...] = h[ts - 1:ts, :]

    y = h * jax.nn.gelu(g_lru)
    ylru_ref[0] = _rms(y, glru_ref[...]).astype(BF16)

    wide = lambda t_ref: jnp.concatenate([t_ref[...]] * (c // LANES), axis=1)
    cos, sina, sinb = wide(cos_ref), wide(sina_ref), wide(sinb_ref)
    half = ROT_DIM // 2

    def rope(t):
        return t * cos + pltpu.roll(t, c - half, axis=1) * sina + pltpu.roll(t, half, axis=1) * sinb

    q_ref[0] = rope(q)
    k_ref[0] = rope(k)
    v_ref[0] = v.astype(BF16)


def _mixer_in(x, gmix, w_in, conv_w, conv_b, w_gates, b_gates, lam, glru, cos_t, sina_t, sinb_t, ts):
    b, s, d = x.shape
    c = conv_w.shape[-1]
    full = lambda shape: pl.BlockSpec(shape, lambda bi, si: (0,) * len(shape))
    tab = pl.BlockSpec((ts, LANES), lambda bi, si: (si, 0))
    seq = lambda: pl.BlockSpec((1, ts, c), lambda bi, si: (bi, si, 0))
    return pl.pallas_call(
        _mixer_in_kernel,
        grid=(b, s // ts),
        in_specs=[pl.BlockSpec((1, ts, d), lambda bi, si: (bi, si, 0)),
                  full((1, d)), full(w_in.shape), full(conv_w.shape), full((1, c)),
                  full(w_gates.shape), full((1, 2 * c)), full((1, c)), full((1, c)),
                  tab, tab, tab],
        out_specs=[seq(), seq(), seq(), seq()],
        out_shape=[jax.ShapeDtypeStruct((b, s, c), BF16),
                   jax.ShapeDtypeStruct((b, s, c), F32),
                   jax.ShapeDtypeStruct((b, s, c), F32),
                   jax.ShapeDtypeStruct((b, s, c), BF16)],
        scratch_shapes=[pltpu.VMEM((ts + SUBLANES, c), F32), pltpu.VMEM((1, c), F32)],
        compiler_params=_params(("arbitrary", "arbitrary")),
        name="mixer_in",
    )(x, gmix, w_in, conv_w, conv_b, w_gates, b_gates, lam, glru, cos_t, sina_t, sinb_t)


def _moba_kernel(q_ref, k_ref, v_ref, o_ref, qa_ref, ka_ref):
    s_len = q_ref.shape[1]
    nb = s_len // MOBA_BLOCK
    n_pick = min(MOBA_TOPK, nb)
    q = q_ref[0]
    k = k_ref[0]
    lane = lax.broadcasted_iota(jnp.int32, (1, LANES), 1)
    kmean = jnp.concatenate(
        [jnp.sum(k[n * MOBA_BLOCK:(n + 1) * MOBA_BLOCK], axis=0, keepdims=True) for n in range(nb)],
        axis=0) * (1.0 / MOBA_BLOCK)

    blk_row = lax.broadcasted_iota(jnp.int32, (nb, s_len), 0)
    q_blk = lax.broadcasted_iota(jnp.int32, (nb, s_len), 1) // MOBA_BLOCK
    past = blk_row < q_blk
    key_blk = lax.broadcasted_iota(jnp.int32, (s_len, LANES), 0) // MOBA_BLOCK
    lane_full = lax.broadcasted_iota(jnp.int32, (s_len, LANES), 1)

    for h in range(2):
        own = (lane >= h * HEAD_DIM) & (lane < (h + 1) * HEAD_DIM)
        off = (1 - h) * HEAD_DIM
        gate = lax.dot_general(jnp.where(own, kmean, 0.0), q, NT_DIMS,
                               precision=lax.Precision.HIGHEST, preferred_element_type=F32)
        gate = jnp.where(past, gate, -jnp.inf)
        rank = jnp.zeros((nb, s_len), jnp.int32)
        for m in range(nb):
            gm = gate[m:m + 1, :]
            ahead = (gm > gate) | ((gm == gate) & (m < blk_row))
            rank = rank + ahead.astype(jnp.int32)
        allowed = ((rank < n_pick) & past) | (blk_row == q_blk)
        bias = jnp.where(allowed, 0.0, NEG_BIG)
        pieces = []
        if off:
            pieces.append(jnp.zeros((off, s_len), F32))
        pieces.append(bias)
        if LANES - off - nb:
            pieces.append(jnp.zeros((LANES - off - nb, s_len), F32))
        bias_lanes = jnp.concatenate(pieces, axis=0).T
        qa_ref[h] = jnp.where(own, q * (HEAD_DIM ** -0.5), bias_lanes).astype(BF16)
        onehot = (lane_full - off == key_blk).astype(F32)
        ka_ref[h] = jnp.where(own, k, onehot).astype(BF16)

    own0 = lane < HEAD_DIM
    for qi in range(nb):
        nk = (qi + 1) * MOBA_BLOCK
        col = lax.broadcasted_iota(jnp.int32, (MOBA_BLOCK, nk), 1)
        qpos = lax.broadcasted_iota(jnp.int32, (MOBA_BLOCK, nk), 0) + qi * MOBA_BLOCK
        causal = col <= qpos
        vv = v_ref[0, 0:nk, :]
        outs = []
        for h in range(2):
            qa = qa_ref[h, qi * MOBA_BLOCK:(qi + 1) * MOBA_BLOCK, :]
            s = lax.dot_general(qa, ka_ref[h, 0:nk, :], NT_DIMS, preferred_element_type=F32)
            s = jnp.where(causal, s, NEG_BIG)
            p = jnp.exp(s - jnp.max(s, axis=-1, keepdims=True))
            l = jnp.sum(p, axis=-1, keepdims=True)
            outs.append(jnp.dot(p.astype(BF16), vv, preferred_element_type=F32) / l)
        o_ref[0, qi * MOBA_BLOCK:(qi + 1) * MOBA_BLOCK, :] = jnp.where(own0, outs[0], outs[1])


def _moba(q, k, v):
    b, s, c = q.shape
    spec = lambda: pl.BlockSpec((1, s, LANES), lambda bi, hi: (bi, 0, hi))
    return pl.pallas_call(
        _moba_kernel,
        grid=(b, c // LANES),
        in_specs=[spec(), spec(), spec()],
        out_specs=spec(),
        out_shape=jax.ShapeDtypeStruct((b, s, c), F32),
        scratch_shapes=[pltpu.VMEM((2, s, LANES), BF16), pltpu.VMEM((2, s, LANES), BF16)],
        compiler_params=_params(("arbitrary", "arbitrary")),
        name="moba",
    )(q, k, v)


def _post_kernel(x_ref, ylru_ref, yatt_ref, gatt_ref, woa_ref, wob_ref, gcross_ref, wxq_ref, kx_ref, vx_ref,
                 wxo_ref, gffn_ref, wr_ref, br_ref, tri_ref,
                 h2_ref, hn2_ref, eid_ref, gcol_ref, rank_ref, cnt_ref, carry_ref):
    tm = x_ref.shape[1]
    d = x_ref.shape[2]
    xd = d // N_XHEADS
    first = (pl.program_id(0) == 0) & (pl.program_id(1) == 0)

    @pl.when(first)
    def _():
        carry_ref[...] = jnp.zeros_like(carry_ref)

    ya = _rms(yatt_ref[0], gatt_ref[...]).astype(BF16)
    mix = (jnp.dot(ylru_ref[0], woa_ref[...], preferred_element_type=F32)
           + jnp.dot(ya, wob_ref[...], preferred_element_type=F32))
    h1 = x_ref[0] + mix

    hn = _rms(h1, gcross_ref[...]).astype(BF16)
    qx = jnp.dot(hn, wxq_ref[...], preferred_element_type=F32)
    heads = []
    for hh in range(N_XHEADS):
        qh = (qx[:, hh * xd:(hh + 1) * xd] * (xd ** -0.5)).astype(BF16)
        s = lax.dot_general(qh, kx_ref[0, :, hh * xd:(hh + 1) * xd], NT_DIMS, preferred_element_type=F32)
        p = jnp.exp(s - jnp.max(s, axis=-1, keepdims=True))
        l = jnp.sum(p, axis=-1, keepdims=True)
        o = jnp.dot(p.astype(BF16), vx_ref[0, :, hh * xd:(hh + 1) * xd], preferred_element_type=F32) / l
        heads.append(o.astype(BF16))
    h2 = h1 + jnp.dot(jnp.concatenate(heads, axis=1), wxo_ref[...], preferred_element_type=F32)
    hn2 = _rms(h2, gffn_ref[...])
    h2_ref[...] = h2
    hn2_ref[...] = hn2

    x_hi = hn2.astype(BF16)
    x_lo = (hn2 - x_hi.astype(F32)).astype(BF16)
    by_hi = lax.dot_general(wr_ref[...], x_hi, NT_DIMS, preferred_element_type=F32)
    by_lo = lax.dot_general(wr_ref[0:ROUTER_ROWS], x_lo, NT_DIMS, preferred_element_type=F32)
    logits = by_hi[0:ROUTER_ROWS] + by_hi[ROUTER_ROWS:2 * ROUTER_ROWS] + by_lo + br_ref[...]
    gl = logits[0:N_GROUPS]
    gmax = jnp.max(gl, axis=0, keepdims=True)
    gi = lax.broadcasted_iota(jnp.int32, gl.shape, 0)
    grp = jnp.min(jnp.where(gl == gmax, gi, N_GROUPS), axis=0, keepdims=True)
    g_w = 1.0 / jnp.sum(jnp.exp(gl - gmax), axis=0, keepdims=True)
    el = jnp.zeros((EXPERTS_PER_GROUP, tm), F32)
    for g in range(N_GROUPS):
        lo = SUBLANES + g * EXPERTS_PER_GROUP
        el = jnp.where(grp == g, logits[lo:lo + EXPERTS_PER_GROUP], el)
    ee = jnp.exp(el - jnp.max(el, axis=0, keepdims=True))
    ep = ee / jnp.sum(ee, axis=0, keepdims=True)
    ei = lax.broadcasted_iota(jnp.int32, ep.shape, 0)
    p1 = jnp.max(ep, axis=0, keepdims=True)
    i1 = jnp.min(jnp.where(ep == p1, ei, EXPERTS_PER_GROUP), axis=0, keepdims=True)
    ep_rest = jnp.where(ei == i1, -1.0, ep)
    p2 = jnp.max(ep_rest, axis=0, keepdims=True)
    i2 = jnp.min(jnp.where(ep_rest == p2, ei, EXPERTS_PER_GROUP), axis=0, keepdims=True)
    den = p1 + p2
    gate1 = g_w * p1 / den
    gate2 = g_w * p2 / den
    e1 = grp * EXPERTS_PER_GROUP + i1
    e2 = grp * EXPERTS_PER_GROUP + i2
    eid_ref[0:1, :] = e1
    eid_ref[1:2, :] = e2
    li = lax.broadcasted_iota(jnp.int32, (LANES, tm), 0)
    gcol_ref[...] = jnp.where(li == 0, gate1, jnp.where(li == 1, gate2, 0.0)).T

    xi = lax.broadcasted_iota(jnp.int32, (N_EXPERTS, tm), 0)
    oh1 = xi == e1
    oh2 = xi == e2
    cnt = oh1.astype(F32) + oh2.astype(F32)
    before = jnp.dot(cnt.astype(BF16), tri_ref[...], preferred_element_type=F32) + carry_ref[:, 0:1]
    rank_ref[0:1, :] = jnp.sum(jnp.where(oh1, before, 0.0), axis=0, keepdims=True).astype(jnp.int32)
    rank_ref[1:2, :] = jnp.sum(jnp.where(oh2, before, 0.0), axis=0, keepdims=True).astype(jnp.int32)
    carry_ref[...] = carry_ref[...] + jnp.sum(cnt, axis=1, keepdims=True)
    cnt_ref[...] = carry_ref[...]


def _post(x, ylru, yatt, gatt, wo_a, wo_b, gcross, wxq, kx, vx, wxo, gffn, wr_t, br_t, tri, tm):
    b, s, d = x.shape
    c = ylru.shape[-1]
    m = kx.shape[1]
    n = b * s
    nt = s // tm
    full = lambda shape: pl.BlockSpec(shape, lambda bi, si: (0,) * len(shape))
    tok = lambda rows: pl.BlockSpec((rows, tm), lambda bi, si: (0, bi * nt + si))
    return pl.pallas_call(
        _post_kernel,
        grid=(b, nt),
        in_specs=[pl.BlockSpec((1, tm, d), lambda bi, si: (bi, si, 0)),
                  pl.BlockSpec((1, tm, c), lambda bi, si: (bi, si, 0)),
                  pl.BlockSpec((1, tm, c), lambda bi, si: (bi, si, 0)),
                  full((1, c)), full((c, d)), full((c, d)), full((1, d)), full((d, d)),
                  pl.BlockSpec((1, m, d), lambda bi, si: (bi, 0, 0)),
                  pl.BlockSpec((1, m, d), lambda bi, si: (bi, 0, 0)),
                  full((d, d)), full((1, d)), full(wr_t.shape), full(br_t.shape), full((tm, tm))],
        out_specs=[pl.BlockSpec((tm, d), lambda bi, si: (bi * nt + si, 0)),
                   pl.BlockSpec((tm, d), lambda bi, si: (bi * nt + si, 0)),
                   tok(EXPERT_TOPK),
                   pl.BlockSpec((tm, LANES), lambda bi, si: (bi * nt + si, 0)),
                   tok(EXPERT_TOPK), full((N_EXPERTS, LANES))],
        out_shape=[jax.ShapeDtypeStruct((n, d), F32),
                   jax.ShapeDtypeStruct((n, d), F32),
                   jax.ShapeDtypeStruct((EXPERT_TOPK, n), jnp.int32),
                   jax.ShapeDtypeStruct((n, LANES), F32),
                   jax.ShapeDtypeStruct((EXPERT_TOPK, n), jnp.int32),
                   jax.ShapeDtypeStruct((N_EXPERTS, LANES), F32)],
        scratch_shapes=[pltpu.VMEM((N_EXPERTS, LANES), F32)],
        compiler_params=_params(("arbitrary", "arbitrary")),
        name="post",
    )(x, ylru, yatt, gatt, wo_a, wo_b, gcross, wxq, kx, vx, wxo, gffn, wr_t, br_t, tri)


def _row_copy(src_ref, src_row, dst_ref, dst_row, sem):
    return pltpu.make_async_copy(src_ref.at[pl.ds(src_row, 1)], dst_ref.at[pl.ds(dst_row, 1)], sem)


def _dest_kernel(ps_ref, eid_ref, rank_ref, dest_ref):
    eid = eid_ref[...]
    dest = rank_ref[...]
    for e in range(N_EXPERTS):
        dest = dest + jnp.where(eid == e, ps_ref[e], 0)
    dest_ref[...] = dest


def _dest(pad_starts, eid, rank):
    shape = (eid.size // LANES, LANES)
    spec = pl.BlockSpec(shape, lambda i, ps: (0, 0))
    return pl.pallas_call(
        _dest_kernel,
        grid_spec=pltpu.PrefetchScalarGridSpec(num_scalar_prefetch=1, grid=(1,), in_specs=[spec, spec],
                                               out_specs=spec),
        out_shape=jax.ShapeDtypeStruct(shape, jnp.int32),
        compiler_params=_params(("arbitrary",)),
        name="dest",
    )(pad_starts, eid.reshape(shape), rank.reshape(shape)).reshape(-1)


def _pad_pieces():
    piece = EXPERT_ROWS // 2
    while piece >= SUBLANES:
        yield piece
        piece //= 2


def _dispatch_kernel(dest_ref, fill_ref, hn_ref, xout_ref, zero_ref, sem, zsem):
    td = hn_ref.shape[0]
    n = dest_ref.shape[0] // EXPERT_TOPK
    step = pl.program_id(0)
    base = step * td

    def pad_copies(e, start):
        first = fill_ref[e]
        length = fill_ref[N_EXPERTS + e]
        head = jnp.minimum((-first) & (SUBLANES - 1), length)

        def one(i, carry):
            cp = _row_copy(zero_ref, 0, xout_ref, first + i, zsem)
            cp.start() if start else cp.wait()
            return carry

        lax.fori_loop(0, head, one, 0)
        body = length - head
        for piece in _pad_pieces():
            offset = pl.multiple_of(first + head + (body & ~(2 * piece - 1)), SUBLANES)

            @pl.when((body & piece) != 0)
            def _():
                cp = pltpu.make_async_copy(zero_ref.at[pl.ds(0, piece)], xout_ref.at[pl.ds(offset, piece)], zsem)
                cp.start() if start else cp.wait()

    def tail_copies(start):
        piece = zero_ref.shape[0]

        def one(i, carry):
            cp = pltpu.make_async_copy(zero_ref, xout_ref.at[pl.ds(pl.multiple_of(i * piece, piece), piece)], zsem)
            cp.start() if start else cp.wait()
            return carry

        lax.fori_loop(fill_ref[2 * N_EXPERTS] // piece, xout_ref.shape[0] // piece, one, 0)

    @pl.when(step == 0)
    def _():
        zero_ref[...] = jnp.zeros_like(zero_ref)
        lax.fori_loop(0, N_EXPERTS, lambda e, c: (pad_copies(e, True), c)[1], 0)
        tail_copies(True)

    def start(r, carry):
        for slot in range(EXPERT_TOPK):
            _row_copy(hn_ref, r, xout_ref, dest_ref[slot * n + base + r], sem).start(priority=slot)
        return carry

    lax.fori_loop(0, td, start, 0, unroll=8)
    for slot in range(EXPERT_TOPK):
        pltpu.make_async_copy(hn_ref, xout_ref.at[pl.ds(0, td)], sem).wait()

    @pl.when(step == 0)
    def _():
        lax.fori_loop(0, N_EXPERTS, lambda e, c: (pad_copies(e, False), c)[1], 0)
        tail_copies(False)


def _dispatch(dest_flat, fill, hn2, n_rows, td):
    n, d = hn2.shape
    return pl.pallas_call(
        _dispatch_kernel,
        grid_spec=pltpu.PrefetchScalarGridSpec(
            num_scalar_prefetch=2,
            grid=(n // td,),
            in_specs=[pl.BlockSpec((td, d), lambda i, dest, fill: (i, 0))],
            out_specs=pl.BlockSpec(memory_space=pl.ANY),
            scratch_shapes=[pltpu.VMEM((EXPERT_ROWS // 2, d), F32),
                            pltpu.SemaphoreType.DMA(()), pltpu.SemaphoreType.DMA(())]),
        out_shape=jax.ShapeDtypeStruct((n_rows, d), F32),
        compiler_params=_params(("arbitrary",)),
        name="dispatch",
    )(dest_flat, fill, hn2)


def _experts_kernel(be_ref, nu_ref, x_ref, wgu_ref, wd_ref, y_ref, wgu_bf_ref, wd_bf_ref):
    j = pl.program_id(0)
    f = wd_ref.shape[1]
    used = j < nu_ref[0]

    @pl.when(used & ((j == 0) | (be_ref[j] != be_ref[jnp.maximum(j, 1) - 1])))
    def _():
        wgu_bf_ref[...] = wgu_ref[0].astype(BF16)
        wd_bf_ref[...] = wd_ref[0].astype(BF16)

    @pl.when(used)
    def _():
        gu = jnp.dot(x_ref[...].astype(BF16), wgu_bf_ref[...], preferred_element_type=F32)
        act = jax.nn.silu(gu[:, 0:f]) * gu[:, f:2 * f]
        y_ref[...] = jnp.dot(act.astype(BF16), wd_bf_ref[...], preferred_element_type=F32)

    @pl.when(jnp.logical_not(used))
    def _():
        y_ref[...] = jnp.zeros_like(y_ref)


def _experts(blk_expert, n_used, x_buf, w_gate_up, w_down):
    p, d = x_buf.shape
    f = w_down.shape[1]
    r = EXPERT_ROWS
    return pl.pallas_call(
        _experts_kernel,
        grid_spec=pltpu.PrefetchScalarGridSpec(
            num_scalar_prefetch=2,
            grid=(p // r,),
            in_specs=[pl.BlockSpec((r, d), lambda j, be, nu: (jnp.minimum(j, jnp.maximum(nu[0], 1) - 1), 0)),
                      pl.BlockSpec((1, d, 2 * f), lambda j, be, nu: (be[j], 0, 0)),
                      pl.BlockSpec((1, f, d), lambda j, be, nu: (be[j], 0, 0))],
            out_specs=pl.BlockSpec((r, d), lambda j, be, nu: (j, 0)),
            scratch_shapes=[pltpu.VMEM((d, 2 * f), BF16), pltpu.VMEM((f, d), BF16)]),
        out_shape=jax.ShapeDtypeStruct((p, d), F32),
        compiler_params=_params(("arbitrary",)),
        name="experts",
    )(blk_expert, n_used, x_buf, w_gate_up, w_down)


def _combine_kernel(dest_ref, h2_ref, gcol_ref, gfin_ref, y_ref, o_ref, buf0_ref, buf1_ref, sem):
    tc = h2_ref.shape[0]
    n = dest_ref.shape[0] // EXPERT_TOPK
    base = pl.program_id(0) * tc
    bufs = (buf0_ref, buf1_ref)

    def start(r, carry):
        for slot in range(EXPERT_TOPK):
            _row_copy(y_ref, dest_ref[slot * n + base + r], bufs[slot], r, sem).start(priority=slot)
        return carry

    lax.fori_loop(0, tc, start, 0, unroll=8)
    for slot in range(EXPERT_TOPK):
        pltpu.make_async_copy(y_ref.at[pl.ds(0, tc)], bufs[slot], sem).wait()
    g = gcol_ref[...]
    moe = g[:, 0:1] * buf0_ref[...] + g[:, 1:2] * buf1_ref[...]
    o_ref[...] = _rms(h2_ref[...] + moe, gfin_ref[...])


def _combine(dest_flat, h2, gcol, gfin, y_buf, tc):
    n, d = h2.shape
    return pl.pallas_call(
        _combine_kernel,
        grid_spec=pltpu.PrefetchScalarGridSpec(
            num_scalar_prefetch=1,
            grid=(n // tc,),
            in_specs=[pl.BlockSpec((tc, d), lambda i, dest: (i, 0)),
                      pl.BlockSpec((tc, LANES), lambda i, dest: (i, 0)),
                      pl.BlockSpec((1, d), lambda i, dest: (0, 0)),
                      pl.BlockSpec(memory_space=pl.ANY)],
            out_specs=pl.BlockSpec((tc, d), lambda i, dest: (i, 0)),
            scratch_shapes=[pltpu.VMEM((tc, d), F32), pltpu.VMEM((tc, d), F32),
                            pltpu.SemaphoreType.DMA(())]),
        out_shape=jax.ShapeDtypeStruct((n, d), F32),
        compiler_params=_params(("arbitrary",)),
        name="combine",
    )(dest_flat, h2, gcol, gfin, y_buf)


def _rope_tables(s):
    half = ROT_DIM // 2
    inv_freq = (ROPE_THETA ** (-np.arange(0, ROT_DIM, 2, dtype=np.float32) / ROT_DIM)).astype(np.float32)
    ang = np.arange(s, dtype=np.float32)[:, None] * inv_freq[None, :]
    cos, sin = np.cos(ang), np.sin(ang)
    zeros = lambda w: np.zeros((s, w), np.float32)
    cos_h = np.concatenate([cos, cos, np.ones((s, HEAD_DIM - ROT_DIM), np.float32)], axis=1)
    sina_h = np.concatenate([-sin, zeros(HEAD_DIM - half)], axis=1)
    sinb_h = np.concatenate([zeros(half), sin, zeros(HEAD_DIM - ROT_DIM)], axis=1)
    tile = lambda t: jnp.asarray(np.tile(t, (1, LANES // HEAD_DIM)), F32)
    return tile(cos_h), tile(sina_h), tile(sinb_h)


def _block_diag(w):
    nblk, bi, bo = w.shape
    eye = jnp.eye(nblk, dtype=w.dtype)
    return jnp.einsum('hij,hg->higj', w, eye).reshape(nblk * bi, nblk * bo)


def _layer(h, mem, norm_mix, w_in, conv_w, conv_b, w_rg, b_rg, w_ig, b_ig, lru_lambda, norm_lru_out,
           norm_attn_out, w_out, norm_cross, norm_mem, w_xq, w_xkv, w_xo, norm_ffn, w_router_group,
           b_router_group, w_router_expert, b_router_expert, w_gate_up, w_down, norm_out):
    b, s, d = h.shape
    c = conv_w.shape[-1]
    n = b * s
    row = lambda t: t.reshape(1, -1)
    ts = MOBA_BLOCK
    tm = 512 if s % 512 == 0 else MOBA_BLOCK
    assert s % ts == 0 and s % tm == 0 and c == N_HEADS * HEAD_DIM

    kx, vx = _memkv(mem, row(norm_mem), w_xkv.astype(BF16))

    w_gates = jnp.concatenate([_block_diag(w_rg), _block_diag(w_ig)], axis=1).astype(BF16)
    b_gates = jnp.concatenate([b_rg, b_ig]).reshape(1, -1)
    cos_t, sina_t, sinb_t = _rope_tables(s)
    ylru, q, k, v = _mixer_in(h, row(norm_mix), w_in.astype(BF16), conv_w, row(conv_b), w_gates, b_gates,
                              row(lru_lambda), row(norm_lru_out), cos_t, sina_t, sinb_t, ts)
    yatt = _moba(q, k, v)

    w_out_b = w_out.astype(BF16)
    wr_t = jnp.zeros((ROUTER_ROWS, d), F32)
    wr_t = wr_t.at[0:N_GROUPS].set(w_router_group.T).at[SUBLANES:SUBLANES + N_EXPERTS].set(w_router_expert.T)
    wr_hi = wr_t.astype(BF16)
    wr_t = jnp.concatenate([wr_hi, (wr_t - wr_hi.astype(F32)).astype(BF16)], axis=0)
    br_t = jnp.zeros((ROUTER_ROWS, 1), F32)
    br_t = br_t.at[0:N_GROUPS, 0].set(b_router_group).at[SUBLANES:SUBLANES + N_EXPERTS, 0].set(b_router_expert)
    tri = jnp.asarray(np.triu(np.ones((tm, tm), np.float32), 1), BF16)
    h2, hn2, eid, gcol, rank, counts = _post(
        h, ylru, yatt, row(norm_attn_out), w_out_b[:c], w_out_b[c:], row(norm_cross), w_xq.astype(BF16),
        kx, vx, w_xo.astype(BF16), row(norm_ffn), wr_t, br_t, tri, tm)

    counts = counts[:, 0].astype(jnp.int32)
    padded = (counts + EXPERT_ROWS - 1) // EXPERT_ROWS * EXPERT_ROWS
    pad_ends = jnp.cumsum(padded)
    pad_starts = pad_ends - padded
    dest_flat = _dest(pad_starts.astype(jnp.int32), eid, rank)
    n_blocks = n * EXPERT_TOPK // EXPERT_ROWS + N_EXPERTS
    blk_first = jnp.arange(n_blocks, dtype=jnp.int32) * EXPERT_ROWS
    blk_expert = jnp.minimum(jnp.sum(blk_first[:, None] >= pad_ends[None, :], axis=1), N_EXPERTS - 1).astype(jnp.int32)
    n_used = (pad_ends[-1:] // EXPERT_ROWS).astype(jnp.int32)
    fill = jnp.concatenate([pad_starts + counts, padded - counts, pad_ends[-1:]]).astype(jnp.int32)

    x_buf = _dispatch(dest_flat, fill, hn2, n_blocks * EXPERT_ROWS, MOBA_BLOCK)
    y_buf = _experts(blk_expert, n_used, x_buf, w_gate_up, w_down)
    out = _combine(dest_flat, h2, gcol, row(norm_out), y_buf, MOBA_BLOCK)
    return out.reshape(b, s, d)


def kernel(x, mem, norm_mix, w_in, conv_w, conv_b, w_rg, b_rg, w_ig, b_ig, lru_lambda, norm_lru_out, norm_attn_out,
           w_out, norm_cross, norm_mem, w_xq, w_xkv, w_xo, norm_ffn, w_router_group, b_router_group,
           w_router_expert, b_router_expert, w_gate_up, w_down, norm_final):
    depth = norm_mix.shape[0]
    assert depth == 1, "the fused final norm assumes a single layer"
    l = 0
    return _layer(x, mem, norm_mix[l], w_in[l], conv_w[l], conv_b[l], w_rg[l], b_rg[l], w_ig[l], b_ig[l],
                  lru_lambda[l], norm_lru_out[l], norm_attn_out[l], w_out[l], norm_cross[l], norm_mem[l],
                  w_xq[l], w_xkv[l], w_xo[l], norm_ffn[l], w_router_group[l], b_router_group[l],
                  w_router_expert[l], b_router_expert[l], w_gate_up[l], w_down[l], norm_final)
```

```python
import functools

import jax
import jax.numpy as jnp
import numpy as np
from jax import lax
from jax.experimental import pallas as pl
from jax.experimental.pallas import tpu as pltpu

F32 = jnp.float32
BF16 = jnp.bfloat16

N_LRU_BLOCKS = 8
CONV_WIDTH = 4
RG_C = 8.0
N_HEADS = 8
HEAD_DIM = 64
ROT_DIM = HEAD_DIM // 4
ROPE_THETA = 500000.0
MOBA_BLOCK = 256
MOBA_TOPK = 3
N_XHEADS = 4
N_GROUPS = 4
EXPERTS_PER_GROUP = 8
N_EXPERTS = N_GROUPS * EXPERTS_PER_GROUP
EXPERT_TOPK = 2
EPS = 1e-6

LANES = 128
SUBLANES = 8
NEG_BIG = -1e30
LOG2_E = 1.4426950408889634
MOBA_LOOKAHEAD = 2
EXPERT_ROWS = 512
MIXER_SUB_ROWS = 256
BF16_ROWS = 2 * SUBLANES
ROUTER_ROWS = -(-(SUBLANES + N_EXPERTS) // BF16_ROWS) * BF16_ROWS
VMEM_LIMIT = 56 * 1024 * 1024

NT_DIMS = (((1,), (1,)), ((), ()))


def _rms(x, g):
    return x * lax.rsqrt(jnp.mean(x * x, axis=-1, keepdims=True) + EPS) * g


def _params(sem):
    return pltpu.CompilerParams(dimension_semantics=sem, vmem_limit_bytes=VMEM_LIMIT)


def _memkv_kernel(mem_ref, g_ref, w_ref, k_ref, v_ref):
    d = mem_ref.shape[-1]
    mn = _rms(mem_ref[0], g_ref[...]).astype(BF16)
    kv = jnp.dot(mn, w_ref[...], preferred_element_type=F32)
    k_ref[0] = kv[:, :d].astype(BF16)
    v_ref[0] = kv[:, d:].astype(BF16)


def _memkv(mem, g, w_xkv):
    b, m, d = mem.shape
    return pl.pallas_call(
        _memkv_kernel,
        grid=(b,),
        in_specs=[pl.BlockSpec((1, m, d), lambda i: (i, 0, 0)),
                  pl.BlockSpec((1, d), lambda i: (0, 0)),
                  pl.BlockSpec((d, 2 * d), lambda i: (0, 0))],
        out_specs=[pl.BlockSpec((1, m, d), lambda i: (i, 0, 0)),
                   pl.BlockSpec((1, m, d), lambda i: (i, 0, 0))],
        out_shape=[jax.ShapeDtypeStruct((b, m, d), BF16)] * 2,
        compiler_params=_params(("arbitrary",)),
        name="memkv",
    )(mem, g, w_xkv)


def _mixer_in_kernel(x_ref, gmix_ref, win_ref, convw_ref, convb_ref, wg_ref, bg_ref, lam_ref, glru_ref,
                     cos_ref, sina_ref, sinb_ref,
                     ylru_ref, q_ref, k_ref, v_ref, xpad_ref, hcar_ref, ascan_ref, uscan_ref):
    c = ylru_ref.shape[-1]
    si = pl.program_id(1)

    @pl.when(si == 0)
    def _():
        xpad_ref[0:SUBLANES, :] = jnp.zeros((SUBLANES, c), F32)
        hcar_ref[...] = jnp.zeros_like(hcar_ref)

    n_sub = x_ref.shape[1] // ascan_ref.shape[1]
    ts = ascan_ref.shape[1]
    projs = []
    for sub in range(n_sub):
        xn = _rms(x_ref[0, sub * ts:(sub + 1) * ts, :], gmix_ref[...]).astype(BF16)
        projs.append(jnp.dot(xn, win_ref[...], preferred_element_type=F32))
    for sub in range(n_sub):
        _mixer_stage2(projs[sub], slice(sub * ts, (sub + 1) * ts), convw_ref, convb_ref, wg_ref, bg_ref, lam_ref,
                      glru_ref, cos_ref, sina_ref, sinb_ref, ylru_ref, q_ref, k_ref, v_ref, xpad_ref, hcar_ref,
                      ascan_ref, uscan_ref)


def _mixer_stage2(proj, rows_out, convw_ref, convb_ref, wg_ref, bg_ref, lam_ref, glru_ref, cos_ref, sina_ref,
                  sinb_ref, ylru_ref, q_ref, k_ref, v_ref, xpad_ref, hcar_ref, ascan_ref, uscan_ref):
    ts = proj.shape[0]
    c = ylru_ref.shape[-1]
    x_lru = proj[:, 0:c]
    g_lru = proj[:, c:2 * c]
    q = proj[:, 2 * c:3 * c]
    k = proj[:, 3 * c:4 * c]
    v = proj[:, 4 * c:5 * c]

    xpad_ref[SUBLANES:SUBLANES + ts, :] = x_lru
    cw = convw_ref[...]
    xc = convb_ref[...] + cw[3:4] * x_lru
    for j in range(1, CONV_WIDTH):
        xc = xc + cw[3 - j:4 - j] * xpad_ref[SUBLANES - j:SUBLANES - j + ts, :]
    xpad_ref[0:SUBLANES, :] = x_lru[ts - SUBLANES:ts, :]

    gates = jnp.dot(xc.astype(BF16), wg_ref[...], preferred_element_type=F32) + bg_ref[...]
    r = jax.nn.sigmoid(gates[:, 0:c])
    i = jax.nn.sigmoid(gates[:, c:2 * c])
    neg_lam = -lam_ref[...]
    softplus = jnp.maximum(neg_lam, 0.0) + jnp.log1p(jnp.exp(-jnp.abs(neg_lam)))
    log_a = -RG_C * r * softplus
    a = jnp.exp(log_a)
    u = jnp.sqrt(-jnp.tanh(log_a) * (a * a + 1.0)) * (i * xc)

    groups = ts // SUBLANES
    cols = c // LANES
    for j in range(cols):
        ascan_ref[j] = a[:, j * LANES:(j + 1) * LANES]
        uscan_ref[j] = u[:, j * LANES:(j + 1) * LANES]
    for r in range(SUBLANES):
        rows = pl.ds(r, groups, stride=SUBLANES)
        a_r = jnp.concatenate([ascan_ref[j, rows, :] for j in range(cols)], axis=1)
        u_r = jnp.concatenate([uscan_ref[j, rows, :] for j in range(cols)], axis=1)
        if r:
            u_r = a_r * u_acc + u_r
            a_r = a_r * a_acc
            for j in range(cols):
                ascan_ref[j, rows, :] = a_r[:, j * LANES:(j + 1) * LANES]
                uscan_ref[j, rows, :] = u_r[:, j * LANES:(j + 1) * LANES]
        a_acc, u_acc = a_r, u_r
    carry = hcar_ref[...]
    hs = []
    for gi in range(groups):
        blk = slice(gi * SUBLANES, (gi + 1) * SUBLANES)
        a_g = jnp.concatenate([ascan_ref[j, blk, :] for j in range(cols)], axis=1)
        u_g = jnp.concatenate([uscan_ref[j, blk, :] for j in range(cols)], axis=1)
        hs.append(a_g * carry + u_g)
        carry = a_acc[gi:gi + 1] * carry + u_acc[gi:gi + 1]
    h = jnp.concatenate(hs, axis=0)
    hcar_ref[...] = carry

    y = h * jax.nn.gelu(g_lru)
    ylru_ref[0, rows_out, :] = _rms(y, glru_ref[...]).astype(BF16)

    wide = lambda t_ref: jnp.concatenate([t_ref[rows_out, :]] * (c // LANES), axis=1)
    cos, sina, sinb = wide(cos_ref), wide(sina_ref), wide(sinb_ref)
    half = ROT_DIM // 2

    def rope(t):
        return t * cos + pltpu.roll(t, c - half, axis=1) * sina + pltpu.roll(t, half, axis=1) * sinb

    q_ref[0, rows_out, :] = rope(q)
    k_ref[0, rows_out, :] = rope(k)
    v_ref[0, rows_out, :] = v.astype(BF16)


def _mixer_in(x, gmix, w_in, conv_w, conv_b, w_gates, b_gates, lam, glru, cos_t, sina_t, sinb_t, ts):
    b, s, d = x.shape
    c = conv_w.shape[-1]
    sub = min(ts, MIXER_SUB_ROWS)
    full = lambda shape: pl.BlockSpec(shape, lambda bi, si: (0,) * len(shape))
    tab = pl.BlockSpec((ts, LANES), lambda bi, si: (si, 0))
    seq = lambda: pl.BlockSpec((1, ts, c), lambda bi, si: (bi, si, 0))
    return pl.pallas_call(
        _mixer_in_kernel,
        grid=(b, s // ts),
        in_specs=[pl.BlockSpec((1, ts, d), lambda bi, si: (bi, si, 0)),
                  full((1, d)), full(w_in.shape), full(conv_w.shape), full((1, c)),
                  full(w_gates.shape), full((1, 2 * c)), full((1, c)), full((1, c)),
                  tab, tab, tab],
        out_specs=[seq(), seq(), seq(), seq()],
        out_shape=[jax.ShapeDtypeStruct((b, s, c), BF16),
                   jax.ShapeDtypeStruct((b, s, c), F32),
                   jax.ShapeDtypeStruct((b, s, c), F32),
                   jax.ShapeDtypeStruct((b, s, c), BF16)],
        scratch_shapes=[pltpu.VMEM((sub + SUBLANES, c), F32), pltpu.VMEM((1, c), F32),
                        pltpu.VMEM((c // LANES, sub, LANES), F32), pltpu.VMEM((c // LANES, sub, LANES), F32)],
        compiler_params=_params(("arbitrary", "arbitrary")),
        name="mixer_in",
    )(x, gmix, w_in, conv_w, conv_b, w_gates, b_gates, lam, glru, cos_t, sina_t, sinb_t)


def _moba_kernel(q_ref, k_ref, v_ref, o_ref, qa_ref, ka_ref):
    s_len = q_ref.shape[1]
    nb = s_len // MOBA_BLOCK
    n_pick = min(MOBA_TOPK, nb)
    q = q_ref[0]
    k = k_ref[0]
    lane = lax.broadcasted_iota(jnp.int32, (1, LANES), 1)
    kmean = jnp.concatenate(
        [jnp.sum(k[n * MOBA_BLOCK:(n + 1) * MOBA_BLOCK], axis=0, keepdims=True) for n in range(nb)],
        axis=0) * (1.0 / MOBA_BLOCK)

    blk_row = lax.broadcasted_iota(jnp.int32, (nb, s_len), 0)
    q_blk = lax.broadcasted_iota(jnp.int32, (nb, s_len), 1) // MOBA_BLOCK
    past = blk_row < q_blk
    key_blk = lax.broadcasted_iota(jnp.int32, (s_len, LANES), 0) // MOBA_BLOCK
    lane_full = lax.broadcasted_iota(jnp.int32, (s_len, LANES), 1)

    for h in range(2):
        own = (lane >= h * HEAD_DIM) & (lane < (h + 1) * HEAD_DIM)
        off = (1 - h) * HEAD_DIM
        gate = lax.dot_general(jnp.where(own, kmean, 0.0), q, NT_DIMS,
                               precision=lax.Precision.HIGHEST, preferred_element_type=F32)
        gate = jnp.where(past, gate, -jnp.inf)
        rank = jnp.zeros((nb, s_len), jnp.int32)
        for m in range(nb):
            gm = gate[m:m + 1, :]
            ahead = (gm > gate) | ((gm == gate) & (m < blk_row))
            rank = rank + ahead.astype(jnp.int32)
        allowed = ((rank < n_pick) & past) | (blk_row == q_blk)
        bias = jnp.where(allowed, 0.0, NEG_BIG)
        pieces = []
        if off:
            pieces.append(jnp.zeros((off, s_len), F32))
        pieces.append(bias)
        if LANES - off - nb:
            pieces.append(jnp.zeros((LANES - off - nb, s_len), F32))
        bias_lanes = jnp.concatenate(pieces, axis=0).T
        qa_ref[h] = jnp.where(own, q * (HEAD_DIM ** -0.5 * LOG2_E), bias_lanes).astype(BF16)
        onehot = (lane_full - off == key_blk).astype(F32)
        ka_ref[h] = jnp.where(own, k, onehot).astype(BF16)

    own0 = lane < HEAD_DIM

    def scores(qi, h):
        nk = (qi + 1) * MOBA_BLOCK
        qa = qa_ref[h, qi * MOBA_BLOCK:(qi + 1) * MOBA_BLOCK, :]
        return lax.dot_general(qa, ka_ref[h, 0:nk, :], NT_DIMS, preferred_element_type=F32)

    key_in_blk = lax.broadcasted_iota(jnp.int32, (MOBA_BLOCK, MOBA_BLOCK), 1)
    q_in_blk = lax.broadcasted_iota(jnp.int32, (MOBA_BLOCK, MOBA_BLOCK), 0)
    causal = key_in_blk <= q_in_blk
    units = [(qi, h) for qi in range(nb) for h in range(2)]
    ahead = [scores(*u) for u in units[:MOBA_LOOKAHEAD]]
    outs = []

    def weighted_values(p, l, qi, h):
        outs.append(jnp.dot(p, v_ref[0, 0:(qi + 1) * MOBA_BLOCK, :], preferred_element_type=F32) / l)
        if h:
            o_ref[0, qi * MOBA_BLOCK:(qi + 1) * MOBA_BLOCK, :] = jnp.where(own0, outs[0], outs[1])
            outs.clear()

    pending = None
    for n, (qi, h) in enumerate(units):
        s = ahead.pop(0)
        if n + MOBA_LOOKAHEAD < len(units):
            ahead.append(scores(*units[n + MOBA_LOOKAHEAD]))
        n_past = qi * MOBA_BLOCK
        s_own = jnp.where(causal, s[:, n_past:], NEG_BIG)
        s = jnp.concatenate([s[:, :n_past], s_own], axis=1) if qi else s_own
        p = jnp.exp2(s - jnp.max(s, axis=-1, keepdims=True))
        l = jnp.sum(p, axis=-1, keepdims=True)
        if pending is not None:
            weighted_values(*pending)
        pending = (p.astype(BF16), l, qi, h)
    weighted_values(*pending)


def _moba(q, k, v):
    b, s, c = q.shape
    spec = lambda: pl.BlockSpec((1, s, LANES), lambda bi, hi: (bi, 0, hi))
    return pl.pallas_call(
        _moba_kernel,
        grid=(b, c // LANES),
        in_specs=[spec(), spec(), spec()],
        out_specs=spec(),
        out_shape=jax.ShapeDtypeStruct((b, s, c), F32),
        scratch_shapes=[pltpu.VMEM((2, s, LANES), BF16), pltpu.VMEM((2, s, LANES), BF16)],
        compiler_params=_params(("arbitrary", "arbitrary")),
        name="moba",
    )(q, k, v)


def _post_kernel(x_ref, ylru_ref, yatt_ref, gatt_ref, woa_ref, wob_ref, gcross_ref, wxq_ref, kx_ref, vx_ref,
                 wxo_ref, gffn_ref, wr_ref, br_ref, tri_ref,
                 h2_ref, hn2_ref, eid_ref, gcol_ref, rank_ref, cnt_ref, carry_ref):
    tm = x_ref.shape[1]
    d = x_ref.shape[2]
    xd = d // N_XHEADS
    first = (pl.program_id(0) == 0) & (pl.program_id(1) == 0)

    @pl.when(first)
    def _():
        carry_ref[...] = jnp.zeros_like(carry_ref)

    ya = _rms(yatt_ref[0], gatt_ref[...]).astype(BF16)
    mix = (jnp.dot(ylru_ref[0], woa_ref[...], preferred_element_type=F32)
           + jnp.dot(ya, wob_ref[...], preferred_element_type=F32))
    h1 = x_ref[0] + mix

    hn = _rms(h1, gcross_ref[...]).astype(BF16)
    qx = jnp.dot(hn, wxq_ref[...], preferred_element_type=F32)
    heads = []
    for hh in range(N_XHEADS):
        qh = (qx[:, hh * xd:(hh + 1) * xd] * (xd ** -0.5)).astype(BF16)
        s = lax.dot_general(qh, kx_ref[0, :, hh * xd:(hh + 1) * xd], NT_DIMS, preferred_element_type=F32)
        p = jnp.exp(s - jnp.max(s, axis=-1, keepdims=True))
        l = jnp.sum(p, axis=-1, keepdims=True)
        o = jnp.dot(p.astype(BF16), vx_ref[0, :, hh * xd:(hh + 1) * xd], preferred_element_type=F32) / l
        heads.append(o.astype(BF16))
    h2 = h1 + jnp.dot(jnp.concatenate(heads, axis=1), wxo_ref[...], preferred_element_type=F32)
    hn2 = _rms(h2, gffn_ref[...])
    h2_ref[...] = h2
    hn2_ref[...] = hn2

    x_hi = hn2.astype(BF16)
    x_lo = (hn2 - x_hi.astype(F32)).astype(BF16)
    by_hi = lax.dot_general(wr_ref[...], x_hi, NT_DIMS, preferred_element_type=F32)
    by_lo = lax.dot_general(wr_ref[0:ROUTER_ROWS], x_lo, NT_DIMS, preferred_element_type=F32)
    logits = by_hi[0:ROUTER_ROWS] + by_hi[ROUTER_ROWS:2 * ROUTER_ROWS] + by_lo + br_ref[...]
    gl = logits[0:N_GROUPS]
    gmax = jnp.max(gl, axis=0, keepdims=True)
    gi = lax.broadcasted_iota(jnp.int32, gl.shape, 0)
    grp = jnp.min(jnp.where(gl == gmax, gi, N_GROUPS), axis=0, keepdims=True)
    g_w = 1.0 / jnp.sum(jnp.exp(gl - gmax), axis=0, keepdims=True)
    el = jnp.zeros((EXPERTS_PER_GROUP, tm), F32)
    for g in range(N_GROUPS):
        lo = SUBLANES + g * EXPERTS_PER_GROUP
        el = jnp.where(grp == g, logits[lo:lo + EXPERTS_PER_GROUP], el)
    ee = jnp.exp(el - jnp.max(el, axis=0, keepdims=True))
    ep = ee / jnp.sum(ee, axis=0, keepdims=True)
    ei = lax.broadcasted_iota(jnp.int32, ep.shape, 0)
    p1 = jnp.max(ep, axis=0, keepdims=True)
    i1 = jnp.min(jnp.where(ep == p1, ei, EXPERTS_PER_GROUP), axis=0, keepdims=True)
    ep_rest = jnp.where(ei == i1, -1.0, ep)
    p2 = jnp.max(ep_rest, axis=0, keepdims=True)
    i2 = jnp.min(jnp.where(ep_rest == p2, ei, EXPERTS_PER_GROUP), axis=0, keepdims=True)
    den = p1 + p2
    gate1 = g_w * p1 / den
    gate2 = g_w * p2 / den
    e1 = grp * EXPERTS_PER_GROUP + i1
    e2 = grp * EXPERTS_PER_GROUP + i2
    eid_ref[0:1, :] = e1
    eid_ref[1:2, :] = e2
    li = lax.broadcasted_iota(jnp.int32, (LANES, tm), 0)
    gcol_ref[...] = jnp.where(li == 0, gate1, jnp.where(li == 1, gate2, 0.0)).T

    xi = lax.broadcasted_iota(jnp.int32, (N_EXPERTS, tm), 0)
    oh1 = xi == e1
    oh2 = xi == e2
    cnt = oh1.astype(F32) + oh2.astype(F32)
    before = jnp.dot(cnt.astype(BF16), tri_ref[...], preferred_element_type=F32) + carry_ref[:, 0:1]
    rank_ref[0:1, :] = jnp.sum(jnp.where(oh1, before, 0.0), axis=0, keepdims=True).astype(jnp.int32)
    rank_ref[1:2, :] = jnp.sum(jnp.where(oh2, before, 0.0), axis=0, keepdims=True).astype(jnp.int32)
    carry_ref[...] = carry_ref[...] + jnp.sum(cnt, axis=1, keepdims=True)
    cnt_ref[...] = carry_ref[...]


def _post(x, ylru, yatt, gatt, wo_a, wo_b, gcross, wxq, kx, vx, wxo, gffn, wr_t, br_t, tri, tm):
    b, s, d = x.shape
    c = ylru.shape[-1]
    m = kx.shape[1]
    n = b * s
    nt = s // tm
    full = lambda shape: pl.BlockSpec(shape, lambda bi, si: (0,) * len(shape))
    tok = lambda rows: pl.BlockSpec((rows, tm), lambda bi, si: (0, bi * nt + si))
    return pl.pallas_call(
        _post_kernel,
        grid=(b, nt),
        in_specs=[pl.BlockSpec((1, tm, d), lambda bi, si: (bi, si, 0)),
                  pl.BlockSpec((1, tm, c), lambda bi, si: (bi, si, 0)),
                  pl.BlockSpec((1, tm, c), lambda bi, si: (bi, si, 0)),
                  full((1, c)), full((c, d)), full((c, d)), full((1, d)), full((d, d)),
                  pl.BlockSpec((1, m, d), lambda bi, si: (bi, 0, 0)),
                  pl.BlockSpec((1, m, d), lambda bi, si: (bi, 0, 0)),
                  full((d, d)), full((1, d)), full(wr_t.shape), full(br_t.shape), full((tm, tm))],
        out_specs=[pl.BlockSpec((tm, d), lambda bi, si: (bi * nt + si, 0)),
                   pl.BlockSpec((tm, d), lambda bi, si: (bi * nt + si, 0)),
                   tok(EXPERT_TOPK),
                   pl.BlockSpec((tm, LANES), lambda bi, si: (bi * nt + si, 0)),
                   tok(EXPERT_TOPK), full((N_EXPERTS, LANES))],
        out_shape=[jax.ShapeDtypeStruct((n, d), F32),
                   jax.ShapeDtypeStruct((n, d), F32),
                   jax.ShapeDtypeStruct((EXPERT_TOPK, n), jnp.int32),
                   jax.ShapeDtypeStruct((n, LANES), F32),
                   jax.ShapeDtypeStruct((EXPERT_TOPK, n), jnp.int32),
                   jax.ShapeDtypeStruct((N_EXPERTS, LANES), F32)],
        scratch_shapes=[pltpu.VMEM((N_EXPERTS, LANES), F32)],
        compiler_params=_params(("arbitrary", "arbitrary")),
        name="post",
    )(x, ylru, yatt, gatt, wo_a, wo_b, gcross, wxq, kx, vx, wxo, gffn, wr_t, br_t, tri)


def _row_copy(src_ref, src_row, dst_ref, dst_row, sem):
    return pltpu.make_async_copy(src_ref.at[pl.ds(src_row, 1)], dst_ref.at[pl.ds(dst_row, 1)], sem)


def _dest_kernel(ps_ref, eid_ref, rank_ref, dest_ref):
    eid = eid_ref[...]
    dest = rank_ref[...]
    for e in range(N_EXPERTS):
        dest = dest + jnp.where(eid == e, ps_ref[e], 0)
    dest_ref[...] = dest


def _dest(pad_starts, eid, rank):
    shape = (eid.size // LANES, LANES)
    spec = pl.BlockSpec(shape, lambda i, ps: (0, 0))
    return pl.pallas_call(
        _dest_kernel,
        grid_spec=pltpu.PrefetchScalarGridSpec(num_scalar_prefetch=1, grid=(1,), in_specs=[spec, spec],
                                               out_specs=spec),
        out_shape=jax.ShapeDtypeStruct(shape, jnp.int32),
        compiler_params=_params(("arbitrary",)),
        name="dest",
    )(pad_starts, eid.reshape(shape), rank.reshape(shape)).reshape(-1)


def _pad_pieces():
    piece = EXPERT_ROWS // 2
    while piece >= SUBLANES:
        yield piece
        piece //= 2


def _dispatch_kernel(dest_ref, fill_ref, hn_ref, xout_ref, zero_ref, sem, zsem):
    td = hn_ref.shape[0]
    n = dest_ref.shape[0] // EXPERT_TOPK
    step = pl.program_id(0)
    base = step * td

    def pad_copies(e, start):
        first = fill_ref[e]
        length = fill_ref[N_EXPERTS + e]
        head = jnp.minimum((-first) & (SUBLANES - 1), length)

        def one(i, carry):
            cp = _row_copy(zero_ref, 0, xout_ref, first + i, zsem)
            cp.start() if start else cp.wait()
            return carry

        lax.fori_loop(0, head, one, 0)
        body = length - head
        for piece in _pad_pieces():
            offset = pl.multiple_of(first + head + (body & ~(2 * piece - 1)), SUBLANES)

            @pl.when((body & piece) != 0)
            def _():
                cp = pltpu.make_async_copy(zero_ref.at[pl.ds(0, piece)], xout_ref.at[pl.ds(offset, piece)], zsem)
                cp.start() if start else cp.wait()

    def tail_copies(start):
        piece = zero_ref.shape[0]

        def one(i, carry):
            cp = pltpu.make_async_copy(zero_ref, xout_ref.at[pl.ds(pl.multiple_of(i * piece, piece), piece)], zsem)
            cp.start() if start else cp.wait()
            return carry

        lax.fori_loop(fill_ref[2 * N_EXPERTS] // piece, xout_ref.shape[0] // piece, one, 0)

    @pl.when(step == 0)
    def _():
        zero_ref[...] = jnp.zeros_like(zero_ref)
        lax.fori_loop(0, N_EXPERTS, lambda e, c: (pad_copies(e, True), c)[1], 0)
        tail_copies(True)

    def start(r, carry):
        for slot in range(EXPERT_TOPK):
            _row_copy(hn_ref, r, xout_ref, dest_ref[slot * n + base + r], sem).start(priority=slot)
        return carry

    lax.fori_loop(0, td, start, 0, unroll=8)
    for slot in range(EXPERT_TOPK):
        pltpu.make_async_copy(hn_ref, xout_ref.at[pl.ds(0, td)], sem).wait()

    @pl.when(step == 0)
    def _():
        lax.fori_loop(0, N_EXPERTS, lambda e, c: (pad_copies(e, False), c)[1], 0)
        tail_copies(False)


def _dispatch(dest_flat, fill, hn2, n_rows, td):
    n, d = hn2.shape
    return pl.pallas_call(
        _dispatch_kernel,
        grid_spec=pltpu.PrefetchScalarGridSpec(
            num_scalar_prefetch=2,
            grid=(n // td,),
            in_specs=[pl.BlockSpec((td, d), lambda i, dest, fill: (i, 0))],
            out_specs=pl.BlockSpec(memory_space=pl.ANY),
            scratch_shapes=[pltpu.VMEM((EXPERT_ROWS // 2, d), F32),
                            pltpu.SemaphoreType.DMA(()), pltpu.SemaphoreType.DMA(())]),
        out_shape=jax.ShapeDtypeStruct((n_rows, d), F32),
        compiler_params=_params(("arbitrary",)),
        name="dispatch",
    )(dest_flat, fill, hn2)


def _experts_kernel(be_ref, nu_ref, x_ref, wgu_ref, wd_ref, y_ref, wgu_bf_ref, wd_bf_ref):
    j = pl.program_id(0)
    f = wd_ref.shape[1]
    used = j < nu_ref[0]

    @pl.when(used & ((j == 0) | (be_ref[j] != be_ref[jnp.maximum(j, 1) - 1])))
    def _():
        wgu_bf_ref[...] = wgu_ref[0].astype(BF16)
        wd_bf_ref[...] = wd_ref[0].astype(BF16)

    @pl.when(used)
    def _():
        gu = jnp.dot(x_ref[...].astype(BF16), wgu_bf_ref[...], preferred_element_type=F32)
        act = jax.nn.silu(gu[:, 0:f]) * gu[:, f:2 * f]
        y_ref[...] = jnp.dot(act.astype(BF16), wd_bf_ref[...], preferred_element_type=F32)

    @pl.when(jnp.logical_not(used))
    def _():
        y_ref[...] = jnp.zeros_like(y_ref)


def _experts(blk_expert, n_used, x_buf, w_gate_up, w_down):
    p, d = x_buf.shape
    f = w_down.shape[1]
    r = EXPERT_ROWS
    return pl.pallas_call(
        _experts_kernel,
        grid_spec=pltpu.PrefetchScalarGridSpec(
            num_scalar_prefetch=2,
            grid=(p // r,),
            in_specs=[pl.BlockSpec((r, d), lambda j, be, nu: (jnp.minimum(j, jnp.maximum(nu[0], 1) - 1), 0)),
                      pl.BlockSpec((1, d, 2 * f), lambda j, be, nu: (be[j], 0, 0)),
                      pl.BlockSpec((1, f, d), lambda j, be, nu: (be[j], 0, 0))],
            out_specs=pl.BlockSpec((r, d), lambda j, be, nu: (j, 0)),
            scratch_shapes=[pltpu.VMEM((d, 2 * f), BF16), pltpu.VMEM((f, d), BF16)]),
        out_shape=jax.ShapeDtypeStruct((p, d), F32),
        compiler_params=_params(("arbitrary",)),
        name="experts",
    )(blk_expert, n_used, x_buf, w_gate_up, w_down)


def _combine_kernel(dest_ref, h2_ref, gcol_ref, gfin_ref, y_ref, o_ref, buf0_ref, buf1_ref, sem):
    tc = h2_ref.shape[0]
    n = dest_ref.shape[0] // EXPERT_TOPK
    base = pl.program_id(0) * tc
    bufs = (buf0_ref, buf1_ref)

    def start(r, carry):
        for slot in range(EXPERT_TOPK):
            _row_copy(y_ref, dest_ref[slot * n + base + r], bufs[slot], r, sem).start(priority=slot)
        return carry

    lax.fori_loop(0, tc, start, 0, unroll=8)
    for slot in range(EXPERT_TOPK):
        pltpu.make_async_copy(y_ref.at[pl.ds(0, tc)], bufs[slot], sem).wait()
    g = gcol_ref[...]
    moe = g[:, 0:1] * buf0_ref[...] + g[:, 1:2] * buf1_ref[...]
    o_ref[...] = _rms(h2_ref[...] + moe, gfin_ref[...])


def _combine(dest_flat, h2, gcol, gfin, y_buf, tc):
    n, d = h2.shape
    return pl.pallas_call(
        _combine_kernel,
        grid_spec=pltpu.PrefetchScalarGridSpec(
            num_scalar_prefetch=1,
            grid=(n // tc,),
            in_specs=[pl.BlockSpec((tc, d), lambda i, dest: (i, 0)),
                      pl.BlockSpec((tc, LANES), lambda i, dest: (i, 0)),
                      pl.BlockSpec((1, d), lambda i, dest: (0, 0)),
                      pl.BlockSpec(memory_space=pl.ANY)],
            out_specs=pl.BlockSpec((tc, d), lambda i, dest: (i, 0)),
            scratch_shapes=[pltpu.VMEM((tc, d), F32), pltpu.VMEM((tc, d), F32),
                            pltpu.SemaphoreType.DMA(())]),
        out_shape=jax.ShapeDtypeStruct((n, d), F32),
        compiler_params=_params(("arbitrary",)),
        name="combine",
    )(dest_flat, h2, gcol, gfin, y_buf)


def _rope_tables(s):
    half = ROT_DIM // 2
    inv_freq = (ROPE_THETA ** (-np.arange(0, ROT_DIM, 2, dtype=np.float32) / ROT_DIM)).astype(np.float32)
    ang = np.arange(s, dtype=np.float32)[:, None] * inv_freq[None, :]
    cos, sin = np.cos(ang), np.sin(ang)
    zeros = lambda w: np.zeros((s, w), np.float32)
    cos_h = np.concatenate([cos, cos, np.ones((s, HEAD_DIM - ROT_DIM), np.float32)], axis=1)
    sina_h = np.concatenate([-sin, zeros(HEAD_DIM - half)], axis=1)
    sinb_h = np.concatenate([zeros(half), sin, zeros(HEAD_DIM - ROT_DIM)], axis=1)
    tile = lambda t: jnp.asarray(np.tile(t, (1, LANES // HEAD_DIM)), F32)
    return tile(cos_h), tile(sina_h), tile(sinb_h)


def _block_diag(w):
    nblk, bi, bo = w.shape
    eye = jnp.eye(nblk, dtype=w.dtype)
    return jnp.einsum('hij,hg->higj', w, eye).reshape(nblk * bi, nblk * bo)


def _layer(h, mem, norm_mix, w_in, conv_w, conv_b, w_rg, b_rg, w_ig, b_ig, lru_lambda, norm_lru_out,
           norm_attn_out, w_out, norm_cross, norm_mem, w_xq, w_xkv, w_xo, norm_ffn, w_router_group,
           b_router_group, w_router_expert, b_router_expert, w_gate_up, w_down, norm_out):
    b, s, d = h.shape
    c = conv_w.shape[-1]
    n = b * s
    row = lambda t: t.reshape(1, -1)
    tm = 512 if s % 512 == 0 else MOBA_BLOCK
    ts = tm
    assert s % ts == 0 and s % tm == 0 and c == N_HEADS * HEAD_DIM

    kx, vx = _memkv(mem, row(norm_mem), w_xkv.astype(BF16))

    w_gates = jnp.concatenate([_block_diag(w_rg), _block_diag(w_ig)], axis=1).astype(BF16)
    b_gates = jnp.concatenate([b_rg, b_ig]).reshape(1, -1)
    cos_t, sina_t, sinb_t = _rope_tables(s)
    ylru, q, k, v = _mixer_in(h, row(norm_mix), w_in.astype(BF16), conv_w, row(conv_b), w_gates, b_gates,
                              row(lru_lambda), row(norm_lru_out), cos_t, sina_t, sinb_t, ts)
    yatt = _moba(q, k, v)

    w_out_b = w_out.astype(BF16)
    wr_t = jnp.zeros((ROUTER_ROWS, d), F32)
    wr_t = wr_t.at[0:N_GROUPS].set(w_router_group.T).at[SUBLANES:SUBLANES + N_EXPERTS].set(w_router_expert.T)
    wr_hi = wr_t.astype(BF16)
    wr_t = jnp.concatenate([wr_hi, (wr_t - wr_hi.astype(F32)).astype(BF16)], axis=0)
    br_t = jnp.zeros((ROUTER_ROWS, 1), F32)
    br_t = br_t.at[0:N_GROUPS, 0].set(b_router_group).at[SUBLANES:SUBLANES + N_EXPERTS, 0].set(b_router_expert)
    tri = jnp.asarray(np.triu(np.ones((tm, tm), np.float32), 1), BF16)
    h2, hn2, eid, gcol, rank, counts = _post(
        h, ylru, yatt, row(norm_attn_out), w_out_b[:c], w_out_b[c:], row(norm_cross), w_xq.astype(BF16),
        kx, vx, w_xo.astype(BF16), row(norm_ffn), wr_t, br_t, tri, tm)

    counts = counts[:, 0].astype(jnp.int32)
    padded = (counts + EXPERT_ROWS - 1) // EXPERT_ROWS * EXPERT_ROWS
    pad_ends = jnp.cumsum(padded)
    pad_starts = pad_ends - padded
    dest_flat = _dest(pad_starts.astype(jnp.int32), eid, rank)
    n_blocks = n * EXPERT_TOPK // EXPERT_ROWS + N_EXPERTS
    blk_first = jnp.arange(n_blocks, dtype=jnp.int32) * EXPERT_ROWS
    blk_expert = jnp.minimum(jnp.sum(blk_first[:, None] >= pad_ends[None, :], axis=1), N_EXPERTS - 1).astype(jnp.int32)
    n_used = (pad_ends[-1:] // EXPERT_ROWS).astype(jnp.int32)
    fill = jnp.concatenate([pad_starts + counts, padded - counts, pad_ends[-1:]]).astype(jnp.int32)

    x_buf = _dispatch(dest_flat, fill, hn2, n_blocks * EXPERT_ROWS, MOBA_BLOCK)
    y_buf = _experts(blk_expert, n_used, x_buf, w_gate_up, w_down)
    out = _combine(dest_flat, h2, gcol, row(norm_out), y_buf, MOBA_BLOCK)
    return out.reshape(b, s, d)


def kernel(x, mem, norm_mix, w_in, conv_w, conv_b, w_rg, b_rg, w_ig, b_ig, lru_lambda, norm_lru_out, norm_attn_out,
           w_out, norm_cross, norm_mem, w_xq, w_xkv, w_xo, norm_ffn, w_router_group, b_router_group,
           w_router_expert, b_router_expert, w_gate_up, w_down, norm_final):
    depth = norm_mix.shape[0]
    assert depth == 1, "the fused final norm assumes a single layer"
    l = 0
    return _layer(x, mem, norm_mix[l], w_in[l], conv_w[l], conv_b[l], w_rg[l], b_rg[l], w_ig[l], b_ig[l],
                  lru_lambda[l], norm_lru_out[l], norm_attn_out[l], w_out[l], norm_cross[l], norm_mem[l],
                  w_xq[l], w_xkv[l], w_xo[l], norm_ffn[l], w_router_group[l], b_router_group[l],
                  w_router_expert[l], b_router_expert[l], w_gate_up[l], w_down[l], norm_final)
```

```python
import functools

import jax
import jax.numpy as jnp
import numpy as np
from jax import lax
from jax.experimental import pallas as pl
from jax.experimental.pallas import tpu as pltpu

F32 = jnp.float32
BF16 = jnp.bfloat16

N_LRU_BLOCKS = 8
CONV_WIDTH = 4
RG_C = 8.0
N_HEADS = 8
HEAD_DIM = 64
ROT_DIM = HEAD_DIM // 4
ROPE_THETA = 500000.0
MOBA_BLOCK = 256
MOBA_TOPK = 3
N_XHEADS = 4
N_GROUPS = 4
EXPERTS_PER_GROUP = 8
N_EXPERTS = N_GROUPS * EXPERTS_PER_GROUP
EXPERT_TOPK = 2
EPS = 1e-6

LANES = 128
SUBLANES = 8
NEG_BIG = -1e30
LOG2_E = 1.4426950408889634
MOBA_LOOKAHEAD = 2
EXPERT_ROWS = 512
EXPERT_CHUNKS = 4
MIXER_SUB_ROWS = 256
BF16_ROWS = 2 * SUBLANES
ROUTER_ROWS = -(-(SUBLANES + N_EXPERTS) // BF16_ROWS) * BF16_ROWS
VMEM_LIMIT = 56 * 1024 * 1024

NT_DIMS = (((1,), (1,)), ((), ()))


def _rms(x, g):
    return x * lax.rsqrt(jnp.mean(x * x, axis=-1, keepdims=True) + EPS) * g


def _params(sem):
    return pltpu.CompilerParams(dimension_semantics=sem, vmem_limit_bytes=VMEM_LIMIT)


def _memkv_kernel(mem_ref, g_ref, w_ref, k_ref, v_ref):
    d = mem_ref.shape[-1]
    mn = _rms(mem_ref[0], g_ref[...]).astype(BF16)
    kv = jnp.dot(mn, w_ref[...], preferred_element_type=F32)
    k_ref[0] = kv[:, :d].astype(BF16)
    v_ref[0] = kv[:, d:].astype(BF16)


def _memkv(mem, g, w_xkv):
    b, m, d = mem.shape
    return pl.pallas_call(
        _memkv_kernel,
        grid=(b,),
        in_specs=[pl.BlockSpec((1, m, d), lambda i: (i, 0, 0)),
                  pl.BlockSpec((1, d), lambda i: (0, 0)),
                  pl.BlockSpec((d, 2 * d), lambda i: (0, 0))],
        out_specs=[pl.BlockSpec((1, m, d), lambda i: (i, 0, 0)),
                   pl.BlockSpec((1, m, d), lambda i: (i, 0, 0))],
        out_shape=[jax.ShapeDtypeStruct((b, m, d), BF16)] * 2,
        compiler_params=_params(("arbitrary",)),
        name="memkv",
    )(mem, g, w_xkv)


def _mixer_in_kernel(x_ref, gmix_ref, win_ref, convw_ref, convb_ref, wg_ref, bg_ref, lam_ref, glru_ref,
                     cos_ref, sina_ref, sinb_ref,
                     ylru_ref, q_ref, k_ref, v_ref, xpad_ref, hcar_ref, ascan_ref, uscan_ref):
    c = ylru_ref.shape[-1]
    si = pl.program_id(1)

    @pl.when(si == 0)
    def _():
        xpad_ref[0:SUBLANES, :] = jnp.zeros((SUBLANES, c), F32)
        hcar_ref[...] = jnp.zeros_like(hcar_ref)

    n_sub = x_ref.shape[1] // ascan_ref.shape[1]
    ts = ascan_ref.shape[1]
    projs = []
    for sub in range(n_sub):
        xn = _rms(x_ref[0, sub * ts:(sub + 1) * ts, :], gmix_ref[...]).astype(BF16)
        projs.append(jnp.dot(xn, win_ref[...], preferred_element_type=F32))
    for sub in range(n_sub):
        _mixer_stage2(projs[sub], slice(sub * ts, (sub + 1) * ts), convw_ref, convb_ref, wg_ref, bg_ref, lam_ref,
                      glru_ref, cos_ref, sina_ref, sinb_ref, ylru_ref, q_ref, k_ref, v_ref, xpad_ref, hcar_ref,
                      ascan_ref, uscan_ref)


def _mixer_stage2(proj, rows_out, convw_ref, convb_ref, wg_ref, bg_ref, lam_ref, glru_ref, cos_ref, sina_ref,
                  sinb_ref, ylru_ref, q_ref, k_ref, v_ref, xpad_ref, hcar_ref, ascan_ref, uscan_ref):
    ts = proj.shape[0]
    c = ylru_ref.shape[-1]
    x_lru = proj[:, 0:c]
    g_lru = proj[:, c:2 * c]
    q = proj[:, 2 * c:3 * c]
    k = proj[:, 3 * c:4 * c]
    v = proj[:, 4 * c:5 * c]

    xpad_ref[SUBLANES:SUBLANES + ts, :] = x_lru
    cw = convw_ref[...]
    xc = convb_ref[...] + cw[3:4] * x_lru
    for j in range(1, CONV_WIDTH):
        xc = xc + cw[3 - j:4 - j] * xpad_ref[SUBLANES - j:SUBLANES - j + ts, :]
    xpad_ref[0:SUBLANES, :] = x_lru[ts - SUBLANES:ts, :]

    gates = jnp.dot(xc.astype(BF16), wg_ref[...], preferred_element_type=F32) + bg_ref[...]
    r = jax.nn.sigmoid(gates[:, 0:c])
    i = jax.nn.sigmoid(gates[:, c:2 * c])
    neg_lam = -lam_ref[...]
    softplus = jnp.maximum(neg_lam, 0.0) + jnp.log1p(jnp.exp(-jnp.abs(neg_lam)))
    log_a = -RG_C * r * softplus
    a = jnp.exp(log_a)
    u = jnp.sqrt(-jnp.tanh(log_a) * (a * a + 1.0)) * (i * xc)

    groups = ts // SUBLANES
    cols = c // LANES
    for j in range(cols):
        ascan_ref[j] = a[:, j * LANES:(j + 1) * LANES]
        uscan_ref[j] = u[:, j * LANES:(j + 1) * LANES]
    for r in range(SUBLANES):
        rows = pl.ds(r, groups, stride=SUBLANES)
        a_r = jnp.concatenate([ascan_ref[j, rows, :] for j in range(cols)], axis=1)
        u_r = jnp.concatenate([uscan_ref[j, rows, :] for j in range(cols)], axis=1)
        if r:
            u_r = a_r * u_acc + u_r
            a_r = a_r * a_acc
            for j in range(cols):
                ascan_ref[j, rows, :] = a_r[:, j * LANES:(j + 1) * LANES]
                uscan_ref[j, rows, :] = u_r[:, j * LANES:(j + 1) * LANES]
        a_acc, u_acc = a_r, u_r
    carry = hcar_ref[...]
    hs = []
    for gi in range(groups):
        blk = slice(gi * SUBLANES, (gi + 1) * SUBLANES)
        a_g = jnp.concatenate([ascan_ref[j, blk, :] for j in range(cols)], axis=1)
        u_g = jnp.concatenate([uscan_ref[j, blk, :] for j in range(cols)], axis=1)
        hs.append(a_g * carry + u_g)
        carry = a_acc[gi:gi + 1] * carry + u_acc[gi:gi + 1]
    h = jnp.concatenate(hs, axis=0)
    hcar_ref[...] = carry

    y = h * jax.nn.gelu(g_lru)
    ylru_ref[0, rows_out, :] = _rms(y, glru_ref[...]).astype(BF16)

    wide = lambda t_ref: jnp.concatenate([t_ref[rows_out, :]] * (c // LANES), axis=1)
    cos, sina, sinb = wide(cos_ref), wide(sina_ref), wide(sinb_ref)
    half = ROT_DIM // 2

    def rope(t):
        return t * cos + pltpu.roll(t, c - half, axis=1) * sina + pltpu.roll(t, half, axis=1) * sinb

    q_ref[0, rows_out, :] = rope(q)
    k_ref[0, rows_out, :] = rope(k)
    v_ref[0, rows_out, :] = v.astype(BF16)


def _mixer_in(x, gmix, w_in, conv_w, conv_b, w_gates, b_gates, lam, glru, cos_t, sina_t, sinb_t, ts):
    b, s, d = x.shape
    c = conv_w.shape[-1]
    sub = min(ts, MIXER_SUB_ROWS)
    full = lambda shape: pl.BlockSpec(shape, lambda bi, si: (0,) * len(shape))
    tab = pl.BlockSpec((ts, LANES), lambda bi, si: (si, 0))
    seq = lambda: pl.BlockSpec((1, ts, c), lambda bi, si: (bi, si, 0))
    return pl.pallas_call(
        _mixer_in_kernel,
        grid=(b, s // ts),
        in_specs=[pl.BlockSpec((1, ts, d), lambda bi, si: (bi, si, 0)),
                  full((1, d)), full(w_in.shape), full(conv_w.shape), full((1, c)),
                  full(w_gates.shape), full((1, 2 * c)), full((1, c)), full((1, c)),
                  tab, tab, tab],
        out_specs=[seq(), seq(), seq(), seq()],
        out_shape=[jax.ShapeDtypeStruct((b, s, c), BF16),
                   jax.ShapeDtypeStruct((b, s, c), F32),
                   jax.ShapeDtypeStruct((b, s, c), F32),
                   jax.ShapeDtypeStruct((b, s, c), BF16)],
        scratch_shapes=[pltpu.VMEM((sub + SUBLANES, c), F32), pltpu.VMEM((1, c), F32),
                        pltpu.VMEM((c // LANES, sub, LANES), F32), pltpu.VMEM((c // LANES, sub, LANES), F32)],
        compiler_params=_params(("arbitrary", "arbitrary")),
        name="mixer_in",
    )(x, gmix, w_in, conv_w, conv_b, w_gates, b_gates, lam, glru, cos_t, sina_t, sinb_t)


def _moba_kernel(q_ref, k_ref, v_ref, o_ref, qa_ref, ka_ref):
    s_len = q_ref.shape[1]
    nb = s_len // MOBA_BLOCK
    n_pick = min(MOBA_TOPK, nb)
    q = q_ref[0]
    k = k_ref[0]
    lane = lax.broadcasted_iota(jnp.int32, (1, LANES), 1)
    kmean = jnp.concatenate(
        [jnp.sum(k[n * MOBA_BLOCK:(n + 1) * MOBA_BLOCK], axis=0, keepdims=True) for n in range(nb)],
        axis=0) * (1.0 / MOBA_BLOCK)

    blk_row = lax.broadcasted_iota(jnp.int32, (nb, s_len), 0)
    q_blk = lax.broadcasted_iota(jnp.int32, (nb, s_len), 1) // MOBA_BLOCK
    past = blk_row < q_blk
    key_blk = lax.broadcasted_iota(jnp.int32, (s_len, LANES), 0) // MOBA_BLOCK
    lane_full = lax.broadcasted_iota(jnp.int32, (s_len, LANES), 1)

    for h in range(2):
        own = (lane >= h * HEAD_DIM) & (lane < (h + 1) * HEAD_DIM)
        off = (1 - h) * HEAD_DIM
        gate = lax.dot_general(jnp.where(own, kmean, 0.0), q, NT_DIMS,
                               precision=lax.Precision.HIGHEST, preferred_element_type=F32)
        gate = jnp.where(past, gate, -jnp.inf)
        rank = jnp.zeros((nb, s_len), jnp.int32)
        for m in range(nb):
            gm = gate[m:m + 1, :]
            ahead = (gm > gate) | ((gm == gate) & (m < blk_row))
            rank = rank + ahead.astype(jnp.int32)
        allowed = ((rank < n_pick) & past) | (blk_row == q_blk)
        bias = jnp.where(allowed, 0.0, NEG_BIG)
        pieces = []
        if off:
            pieces.append(jnp.zeros((off, s_len), F32))
        pieces.append(bias)
        if LANES - off - nb:
            pieces.append(jnp.zeros((LANES - off - nb, s_len), F32))
        bias_lanes = jnp.concatenate(pieces, axis=0).T
        qa_ref[h] = jnp.where(own, q * (HEAD_DIM ** -0.5 * LOG2_E), bias_lanes).astype(BF16)
        onehot = (lane_full - off == key_blk).astype(F32)
        ka_ref[h] = jnp.where(own, k, onehot).astype(BF16)

    own0 = lane < HEAD_DIM

    def scores(qi, h):
        nk = (qi + 1) * MOBA_BLOCK
        qa = qa_ref[h, qi * MOBA_BLOCK:(qi + 1) * MOBA_BLOCK, :]
        return lax.dot_general(qa, ka_ref[h, 0:nk, :], NT_DIMS, preferred_element_type=F32)

    key_in_blk = lax.broadcasted_iota(jnp.int32, (MOBA_BLOCK, MOBA_BLOCK), 1)
    q_in_blk = lax.broadcasted_iota(jnp.int32, (MOBA_BLOCK, MOBA_BLOCK), 0)
    causal = key_in_blk <= q_in_blk
    units = [(qi, h) for qi in range(nb) for h in range(2)]
    ahead = [scores(*u) for u in units[:MOBA_LOOKAHEAD]]
    outs = []

    def weighted_values(p, l, qi, h):
        outs.append(jnp.dot(p, v_ref[0, 0:(qi + 1) * MOBA_BLOCK, :], preferred_element_type=F32) / l)
        if h:
            o_ref[0, qi * MOBA_BLOCK:(qi + 1) * MOBA_BLOCK, :] = jnp.where(own0, outs[0], outs[1])
            outs.clear()

    pending = None
    for n, (qi, h) in enumerate(units):
        s = ahead.pop(0)
        if n + MOBA_LOOKAHEAD < len(units):
            ahead.append(scores(*units[n + MOBA_LOOKAHEAD]))
        n_past = qi * MOBA_BLOCK
        s_own = jnp.where(causal, s[:, n_past:], NEG_BIG)
        s = jnp.concatenate([s[:, :n_past], s_own], axis=1) if qi else s_own
        p = jnp.exp2(s - jnp.max(s, axis=-1, keepdims=True))
        l = jnp.sum(p, axis=-1, keepdims=True)
        if pending is not None:
            weighted_values(*pending)
        pending = (p.astype(BF16), l, qi, h)
    weighted_values(*pending)


def _moba(q, k, v):
    b, s, c = q.shape
    spec = lambda: pl.BlockSpec((1, s, LANES), lambda bi, hi: (bi, 0, hi))
    return pl.pallas_call(
        _moba_kernel,
        grid=(b, c // LANES),
        in_specs=[spec(), spec(), spec()],
        out_specs=spec(),
        out_shape=jax.ShapeDtypeStruct((b, s, c), F32),
        scratch_shapes=[pltpu.VMEM((2, s, LANES), BF16), pltpu.VMEM((2, s, LANES), BF16)],
        compiler_params=_params(("arbitrary", "arbitrary")),
        name="moba",
    )(q, k, v)


def _post_kernel(x_ref, ylru_ref, yatt_ref, gatt_ref, woa_ref, wob_ref, gcross_ref, wxq_ref, kx_ref, vx_ref,
                 wxo_ref, gffn_ref, wr_ref, br_ref, tri_ref,
                 h2_ref, hn2_ref, eid_ref, gcol_ref, rank_ref, cnt_ref, carry_ref):
    tm = x_ref.shape[1]
    d = x_ref.shape[2]
    xd = d // N_XHEADS
    first = (pl.program_id(0) == 0) & (pl.program_id(1) == 0)

    @pl.when(first)
    def _():
        carry_ref[...] = jnp.zeros_like(carry_ref)

    ya = _rms(yatt_ref[0], gatt_ref[...]).astype(BF16)
    mix = (jnp.dot(ylru_ref[0], woa_ref[...], preferred_element_type=F32)
           + jnp.dot(ya, wob_ref[...], preferred_element_type=F32))
    h1 = x_ref[0] + mix

    hn = _rms(h1, gcross_ref[...]).astype(BF16)
    qx = jnp.dot(hn, wxq_ref[...], preferred_element_type=F32)
    heads = []
    for hh in range(N_XHEADS):
        qh = (qx[:, hh * xd:(hh + 1) * xd] * (xd ** -0.5)).astype(BF16)
        s = lax.dot_general(qh, kx_ref[0, :, hh * xd:(hh + 1) * xd], NT_DIMS, preferred_element_type=F32)
        p = jnp.exp(s - jnp.max(s, axis=-1, keepdims=True))
        l = jnp.sum(p, axis=-1, keepdims=True)
        o = jnp.dot(p.astype(BF16), vx_ref[0, :, hh * xd:(hh + 1) * xd], preferred_element_type=F32) / l
        heads.append(o.astype(BF16))
    h2 = h1 + jnp.dot(jnp.concatenate(heads, axis=1), wxo_ref[...], preferred_element_type=F32)
    hn2 = _rms(h2, gffn_ref[...])
    h2_ref[...] = h2
    hn2_ref[...] = hn2

    x_hi = hn2.astype(BF16)
    x_lo = (hn2 - x_hi.astype(F32)).astype(BF16)
    by_hi = lax.dot_general(wr_ref[...], x_hi, NT_DIMS, preferred_element_type=F32)
    by_lo = lax.dot_general(wr_ref[0:ROUTER_ROWS], x_lo, NT_DIMS, preferred_element_type=F32)
    logits = by_hi[0:ROUTER_ROWS] + by_hi[ROUTER_ROWS:2 * ROUTER_ROWS] + by_lo + br_ref[...]
    gl = logits[0:N_GROUPS]
    gmax = jnp.max(gl, axis=0, keepdims=True)
    gi = lax.broadcasted_iota(jnp.int32, gl.shape, 0)
    grp = jnp.min(jnp.where(gl == gmax, gi, N_GROUPS), axis=0, keepdims=True)
    g_w = 1.0 / jnp.sum(jnp.exp(gl - gmax), axis=0, keepdims=True)
    el = jnp.zeros((EXPERTS_PER_GROUP, tm), F32)
    for g in range(N_GROUPS):
        lo = SUBLANES + g * EXPERTS_PER_GROUP
        el = jnp.where(grp == g, logits[lo:lo + EXPERTS_PER_GROUP], el)
    ee = jnp.exp(el - jnp.max(el, axis=0, keepdims=True))
    ep = ee / jnp.sum(ee, axis=0, keepdims=True)
    ei = lax.broadcasted_iota(jnp.int32, ep.shape, 0)
    p1 = jnp.max(ep, axis=0, keepdims=True)
    i1 = jnp.min(jnp.where(ep == p1, ei, EXPERTS_PER_GROUP), axis=0, keepdims=True)
    ep_rest = jnp.where(ei == i1, -1.0, ep)
    p2 = jnp.max(ep_rest, axis=0, keepdims=True)
    i2 = jnp.min(jnp.where(ep_rest == p2, ei, EXPERTS_PER_GROUP), axis=0, keepdims=True)
    den = p1 + p2
    gate1 = g_w * p1 / den
    gate2 = g_w * p2 / den
    e1 = grp * EXPERTS_PER_GROUP + i1
    e2 = grp * EXPERTS_PER_GROUP + i2
    eid_ref[0:1, :] = e1
    eid_ref[1:2, :] = e2
    li = lax.broadcasted_iota(jnp.int32, (LANES, tm), 0)
    gcol_ref[...] = jnp.where(li == 0, gate1, jnp.where(li == 1, gate2, 0.0)).T

    xi = lax.broadcasted_iota(jnp.int32, (N_EXPERTS, tm), 0)
    oh1 = xi == e1
    oh2 = xi == e2
    cnt = oh1.astype(F32) + oh2.astype(F32)
    before = jnp.dot(cnt.astype(BF16), tri_ref[...], preferred_element_type=F32) + carry_ref[:, 0:1]
    rank_ref[0:1, :] = jnp.sum(jnp.where(oh1, before, 0.0), axis=0, keepdims=True).astype(jnp.int32)
    rank_ref[1:2, :] = jnp.sum(jnp.where(oh2, before, 0.0), axis=0, keepdims=True).astype(jnp.int32)
    carry_ref[...] = carry_ref[...] + jnp.sum(cnt, axis=1, keepdims=True)
    cnt_ref[...] = carry_ref[...]


def _post(x, ylru, yatt, gatt, wo_a, wo_b, gcross, wxq, kx, vx, wxo, gffn, wr_t, br_t, tri, tm):
    b, s, d = x.shape
    c = ylru.shape[-1]
    m = kx.shape[1]
    n = b * s
    nt = s // tm
    full = lambda shape: pl.BlockSpec(shape, lambda bi, si: (0,) * len(shape))
    tok = lambda rows: pl.BlockSpec((rows, tm), lambda bi, si: (0, bi * nt + si))
    return pl.pallas_call(
        _post_kernel,
        grid=(b, nt),
        in_specs=[pl.BlockSpec((1, tm, d), lambda bi, si: (bi, si, 0)),
                  pl.BlockSpec((1, tm, c), lambda bi, si: (bi, si, 0)),
                  pl.BlockSpec((1, tm, c), lambda bi, si: (bi, si, 0)),
                  full((1, c)), full((c, d)), full((c, d)), full((1, d)), full((d, d)),
                  pl.BlockSpec((1, m, d), lambda bi, si: (bi, 0, 0)),
                  pl.BlockSpec((1, m, d), lambda bi, si: (bi, 0, 0)),
                  full((d, d)), full((1, d)), full(wr_t.shape), full(br_t.shape), full((tm, tm))],
        out_specs=[pl.BlockSpec((tm, d), lambda bi, si: (bi * nt + si, 0)),
                   pl.BlockSpec((tm, d), lambda bi, si: (bi * nt + si, 0)),
                   tok(EXPERT_TOPK),
                   pl.BlockSpec((tm, LANES), lambda bi, si: (bi * nt + si, 0)),
                   tok(EXPERT_TOPK), full((N_EXPERTS, LANES))],
        out_shape=[jax.ShapeDtypeStruct((n, d), F32),
                   jax.ShapeDtypeStruct((n, d), F32),
                   jax.ShapeDtypeStruct((EXPERT_TOPK, n), jnp.int32),
                   jax.ShapeDtypeStruct((n, LANES), F32),
                   jax.ShapeDtypeStruct((EXPERT_TOPK, n), jnp.int32),
                   jax.ShapeDtypeStruct((N_EXPERTS, LANES), F32)],
        scratch_shapes=[pltpu.VMEM((N_EXPERTS, LANES), F32)],
        compiler_params=_params(("arbitrary", "arbitrary")),
        name="post",
    )(x, ylru, yatt, gatt, wo_a, wo_b, gcross, wxq, kx, vx, wxo, gffn, wr_t, br_t, tri)


def _row_copy(src_ref, src_row, dst_ref, dst_row, sem):
    return pltpu.make_async_copy(src_ref.at[pl.ds(src_row, 1)], dst_ref.at[pl.ds(dst_row, 1)], sem)


def _dest_kernel(ps_ref, eid_ref, rank_ref, dest_ref):
    eid = eid_ref[...]
    dest = rank_ref[...]
    for e in range(N_EXPERTS):
        dest = dest + jnp.where(eid == e, ps_ref[e], 0)
    dest_ref[...] = dest


def _dest(pad_starts, eid, rank):
    shape = (eid.size // LANES, LANES)
    spec = pl.BlockSpec(shape, lambda i, ps: (0, 0))
    return pl.pallas_call(
        _dest_kernel,
        grid_spec=pltpu.PrefetchScalarGridSpec(num_scalar_prefetch=1, grid=(1,), in_specs=[spec, spec],
                                               out_specs=spec),
        out_shape=jax.ShapeDtypeStruct(shape, jnp.int32),
        compiler_params=_params(("arbitrary",)),
        name="dest",
    )(pad_starts, eid.reshape(shape), rank.reshape(shape)).reshape(-1)


def _pad_pieces():
    piece = EXPERT_ROWS // 2
    while piece >= SUBLANES:
        yield piece
        piece //= 2


def _dispatch_kernel(dest_ref, fill_ref, hn_ref, xout_ref, inv_ref, zero_ref, sem, zsem):
    td = hn_ref.shape[0]
    n = dest_ref.shape[0] // EXPERT_TOPK
    step = pl.program_id(0)
    base = step * td

    def pad_copies(e, start):
        first = fill_ref[e]
        length = fill_ref[N_EXPERTS + e]
        head = jnp.minimum((-first) & (SUBLANES - 1), length)
        if start:
            def dump(i, carry):
                inv_ref[first + i] = EXPERT_TOPK * n + ((first + i) & (EXPERT_ROWS - 1))
                return carry

            lax.fori_loop(0, length, dump, 0)

        def one(i, carry):
            cp = _row_copy(zero_ref, 0, xout_ref, first + i, zsem)
            cp.start() if start else cp.wait()
            return carry

        lax.fori_loop(0, head, one, 0)
        body = length - head
        for piece in _pad_pieces():
            offset = pl.multiple_of(first + head + (body & ~(2 * piece - 1)), SUBLANES)

            @pl.when((body & piece) != 0)
            def _():
                cp = pltpu.make_async_copy(zero_ref.at[pl.ds(0, piece)], xout_ref.at[pl.ds(offset, piece)], zsem)
                cp.start() if start else cp.wait()

    def tail_copies(start):
        piece = zero_ref.shape[0]

        def one(i, carry):
            cp = pltpu.make_async_copy(zero_ref, xout_ref.at[pl.ds(pl.multiple_of(i * piece, piece), piece)], zsem)
            cp.start() if start else cp.wait()
            return carry

        lax.fori_loop(fill_ref[2 * N_EXPERTS] // piece, xout_ref.shape[0] // piece, one, 0)
        if start:
            def dump(i, carry):
                inv_ref[i] = EXPERT_TOPK * n + (i & (EXPERT_ROWS - 1))
                return carry

            lax.fori_loop(fill_ref[2 * N_EXPERTS], xout_ref.shape[0], dump, 0)

    @pl.when(step == 0)
    def _():
        zero_ref[...] = jnp.zeros_like(zero_ref)
        lax.fori_loop(0, N_EXPERTS, lambda e, c: (pad_copies(e, True), c)[1], 0)
        tail_copies(True)

    def start(r, carry):
        for slot in range(EXPERT_TOPK):
            src = slot * n + base + r
            row = dest_ref[src]
            _row_copy(hn_ref, r, xout_ref, row, sem).start()
            inv_ref[row] = src
        return carry

    lax.fori_loop(0, td, start, 0, unroll=8)
    for slot in range(EXPERT_TOPK):
        pltpu.make_async_copy(hn_ref, xout_ref.at[pl.ds(0, td)], sem).wait()

    @pl.when(step == 0)
    def _():
        lax.fori_loop(0, N_EXPERTS, lambda e, c: (pad_copies(e, False), c)[1], 0)
        tail_copies(False)


def _dispatch(dest_flat, fill, hn2, n_rows, td):
    n, d = hn2.shape
    return pl.pallas_call(
        _dispatch_kernel,
        grid_spec=pltpu.PrefetchScalarGridSpec(
            num_scalar_prefetch=2,
            grid=(n // td,),
            in_specs=[pl.BlockSpec((td, d), lambda i, dest, fill: (i, 0))],
            out_specs=[pl.BlockSpec(memory_space=pl.ANY), pl.BlockSpec(memory_space=pltpu.SMEM)],
            scratch_shapes=[pltpu.VMEM((EXPERT_ROWS // 2, d), F32),
                            pltpu.SemaphoreType.DMA(()), pltpu.SemaphoreType.DMA(())]),
        out_shape=[jax.ShapeDtypeStruct((n_rows, d), F32), jax.ShapeDtypeStruct((n_rows,), jnp.int32)],
        compiler_params=_params(("arbitrary",)),
        name="dispatch",
    )(dest_flat, fill, hn2)


def _experts_kernel(be_ref, nu_ref, inv_ref, x_ref, wgu_ref, wd_ref, ytok_ref, wgu_bf_ref, wd_bf_ref, ybuf_ref, sems):
    j = pl.program_id(0)
    rows = x_ref.shape[0]
    f = wd_ref.shape[1]
    n_used = nu_ref[0]
    par = j % 2
    last_blk = be_ref.shape[0] - 1

    def scatter_copy(blk_par, r, dst_row):
        return _row_copy(ybuf_ref.at[blk_par], r, ytok_ref, dst_row, sems.at[blk_par])

    chunk = rows // EXPERT_CHUNKS

    def start_scatter_of_previous(part):
        base = (j - 1) * rows
        for r in range(part * chunk, (part + 1) * chunk):
            scatter_copy(1 - par, r, inv_ref[base + r]).start()

    def wait_scatter(blk_par):
        pltpu.make_async_copy(ybuf_ref.at[blk_par], ytok_ref.at[pl.ds(0, rows)], sems.at[blk_par]).wait()

    def mlp(part):
        sl = slice(part * chunk, (part + 1) * chunk)
        gu = jnp.dot(x_ref[sl, :].astype(BF16), wgu_bf_ref[...], preferred_element_type=F32)
        act = jax.nn.silu(gu[:, 0:f]) * gu[:, f:2 * f]
        ybuf_ref[par, sl, :] = jnp.dot(act.astype(BF16), wd_bf_ref[...], preferred_element_type=F32)

    jc = jnp.minimum(j, last_blk)
    @pl.when((j < n_used) & ((j == 0) | (be_ref[jc] != be_ref[jnp.maximum(jc, 1) - 1])))
    def _():
        wgu_bf_ref[...] = wgu_ref[0].astype(BF16)
        wd_bf_ref[...] = wd_ref[0].astype(BF16)

    @pl.when((j >= 2) & (j <= n_used))
    def _():
        wait_scatter(par)

    @pl.when(j == 0)
    def _():
        cp = pltpu.make_async_copy(x_ref, ytok_ref.at[pl.ds(ytok_ref.shape[0] - rows, rows)], sems.at[0])
        cp.start()
        cp.wait()

    @pl.when((j == 0) & (j < n_used))
    def _():
        for part in range(EXPERT_CHUNKS):
            mlp(part)

    @pl.when((j >= 1) & (j < n_used))
    def _():
        for part in range(EXPERT_CHUNKS):
            start_scatter_of_previous(part)
            mlp(part)

    @pl.when((j >= 1) & (j == n_used))
    def _():
        for part in range(EXPERT_CHUNKS):
            start_scatter_of_previous(part)
        wait_scatter(1 - par)


def _experts(blk_expert, n_used, inv, x_buf, w_gate_up, w_down, n_out_rows):
    p, d = x_buf.shape
    f = w_down.shape[1]
    r = EXPERT_ROWS
    n_blk = p // r
    x_map = lambda j, be, nu, inv: (jnp.minimum(j, jnp.maximum(nu[0], 1) - 1), 0)
    w_map = lambda j, be, nu, inv: (be[jnp.minimum(j, n_blk - 1)], 0, 0)
    return pl.pallas_call(
        _experts_kernel,
        grid_spec=pltpu.PrefetchScalarGridSpec(
            num_scalar_prefetch=3,
            grid=(n_blk + 1,),
            in_specs=[pl.BlockSpec((r, d), x_map),
                      pl.BlockSpec((1, d, 2 * f), w_map),
                      pl.BlockSpec((1, f, d), w_map)],
            out_specs=pl.BlockSpec(memory_space=pl.ANY),
            scratch_shapes=[pltpu.VMEM((d, 2 * f), BF16), pltpu.VMEM((f, d), BF16),
                            pltpu.VMEM((2, r, d), F32), pltpu.SemaphoreType.DMA((2,))]),
        out_shape=jax.ShapeDtypeStruct((n_out_rows, d), F32),
        compiler_params=_params(("arbitrary",)),
        name="experts",
    )(blk_expert, n_used, inv, x_buf, w_gate_up, w_down)


def _combine_kernel(h2_ref, gcol_ref, gfin_ref, y0_ref, y1_ref, o_ref):
    g = gcol_ref[...]
    moe = g[:, 0:1] * y0_ref[...] + g[:, 1:2] * y1_ref[...]
    o_ref[...] = _rms(h2_ref[...] + moe, gfin_ref[...])


def _combine(h2, gcol, gfin, y_tok, tc):
    n, d = h2.shape
    nt = n // tc
    return pl.pallas_call(
        _combine_kernel,
        grid=(nt,),
        in_specs=[pl.BlockSpec((tc, d), lambda i: (i, 0)),
                  pl.BlockSpec((tc, LANES), lambda i: (i, 0)),
                  pl.BlockSpec((1, d), lambda i: (0, 0)),
                  pl.BlockSpec((tc, d), lambda i: (i, 0)),
                  pl.BlockSpec((tc, d), lambda i: (i + nt, 0))],
        out_specs=pl.BlockSpec((tc, d), lambda i: (i, 0)),
        out_shape=jax.ShapeDtypeStruct((n, d), F32),
        compiler_params=_params(("arbitrary",)),
        name="combine",
    )(h2, gcol, gfin, y_tok, y_tok)


def _rope_tables(s):
    half = ROT_DIM // 2
    inv_freq = (ROPE_THETA ** (-np.arange(0, ROT_DIM, 2, dtype=np.float32) / ROT_DIM)).astype(np.float32)
    ang = np.arange(s, dtype=np.float32)[:, None] * inv_freq[None, :]
    cos, sin = np.cos(ang), np.sin(ang)
    zeros = lambda w: np.zeros((s, w), np.float32)
    cos_h = np.concatenate([cos, cos, np.ones((s, HEAD_DIM - ROT_DIM), np.float32)], axis=1)
    sina_h = np.concatenate([-sin, zeros(HEAD_DIM - half)], axis=1)
    sinb_h = np.concatenate([zeros(half), sin, zeros(HEAD_DIM - ROT_DIM)], axis=1)
    tile = lambda t: jnp.asarray(np.tile(t, (1, LANES // HEAD_DIM)), F32)
    return tile(cos_h), tile(sina_h), tile(sinb_h)


def _block_diag(w):
    nblk, bi, bo = w.shape
    eye = jnp.eye(nblk, dtype=w.dtype)
    return jnp.einsum('hij,hg->higj', w, eye).reshape(nblk * bi, nblk * bo)


def _layer(h, mem, norm_mix, w_in, conv_w, conv_b, w_rg, b_rg, w_ig, b_ig, lru_lambda, norm_lru_out,
           norm_attn_out, w_out, norm_cross, norm_mem, w_xq, w_xkv, w_xo, norm_ffn, w_router_group,
           b_router_group, w_router_expert, b_router_expert, w_gate_up, w_down, norm_out):
    b, s, d = h.shape
    c = conv_w.shape[-1]
    n = b * s
    row = lambda t: t.reshape(1, -1)
    tm = 512 if s % 512 == 0 else MOBA_BLOCK
    ts = tm
    assert s % ts == 0 and s % tm == 0 and c == N_HEADS * HEAD_DIM

    kx, vx = _memkv(mem, row(norm_mem), w_xkv.astype(BF16))

    w_gates = jnp.concatenate([_block_diag(w_rg), _block_diag(w_ig)], axis=1).astype(BF16)
    b_gates = jnp.concatenate([b_rg, b_ig]).reshape(1, -1)
    cos_t, sina_t, sinb_t = _rope_tables(s)
    ylru, q, k, v = _mixer_in(h, row(norm_mix), w_in.astype(BF16), conv_w, row(conv_b), w_gates, b_gates,
                              row(lru_lambda), row(norm_lru_out), cos_t, sina_t, sinb_t, ts)
    yatt = _moba(q, k, v)

    w_out_b = w_out.astype(BF16)
    wr_t = jnp.zeros((ROUTER_ROWS, d), F32)
    wr_t = wr_t.at[0:N_GROUPS].set(w_router_group.T).at[SUBLANES:SUBLANES + N_EXPERTS].set(w_router_expert.T)
    wr_hi = wr_t.astype(BF16)
    wr_t = jnp.concatenate([wr_hi, (wr_t - wr_hi.astype(F32)).astype(BF16)], axis=0)
    br_t = jnp.zeros((ROUTER_ROWS, 1), F32)
    br_t = br_t.at[0:N_GROUPS, 0].set(b_router_group).at[SUBLANES:SUBLANES + N_EXPERTS, 0].set(b_router_expert)
    tri = jnp.asarray(np.triu(np.ones((tm, tm), np.float32), 1), BF16)
    h2, hn2, eid, gcol, rank, counts = _post(
        h, ylru, yatt, row(norm_attn_out), w_out_b[:c], w_out_b[c:], row(norm_cross), w_xq.astype(BF16),
        kx, vx, w_xo.astype(BF16), row(norm_ffn), wr_t, br_t, tri, tm)

    counts = counts[:, 0].astype(jnp.int32)
    padded = (counts + EXPERT_ROWS - 1) // EXPERT_ROWS * EXPERT_ROWS
    pad_ends = jnp.cumsum(padded)
    pad_starts = pad_ends - padded
    dest_flat = _dest(pad_starts.astype(jnp.int32), eid, rank)
    n_blocks = n * EXPERT_TOPK // EXPERT_ROWS + N_EXPERTS
    blk_first = jnp.arange(n_blocks, dtype=jnp.int32) * EXPERT_ROWS
    blk_expert = jnp.minimum(jnp.sum(blk_first[:, None] >= pad_ends[None, :], axis=1), N_EXPERTS - 1).astype(jnp.int32)
    n_used = (pad_ends[-1:] // EXPERT_ROWS).astype(jnp.int32)
    fill = jnp.concatenate([pad_starts + counts, padded - counts, pad_ends[-1:]]).astype(jnp.int32)

    x_buf, inv = _dispatch(dest_flat, fill, hn2, n_blocks * EXPERT_ROWS, MOBA_BLOCK)
    y_tok = _experts(blk_expert, n_used, inv, x_buf, w_gate_up, w_down, EXPERT_TOPK * n + EXPERT_ROWS)
    out = _combine(h2, gcol, row(norm_out), y_tok, tm)
    return out.reshape(b, s, d)


def kernel(x, mem, norm_mix, w_in, conv_w, conv_b, w_rg, b_rg, w_ig, b_ig, lru_lambda, norm_lru_out, norm_attn_out,
           w_out, norm_cross, norm_mem, w_xq, w_xkv, w_xo, norm_ffn, w_router_group, b_router_group,
           w_router_expert, b_router_expert, w_gate_up, w_down, norm_final):
    depth = norm_mix.shape[0]
    assert depth == 1, "the fused final norm assumes a single layer"
    l = 0
    return _layer(x, mem, norm_mix[l], w_in[l], conv_w[l], conv_b[l], w_rg[l], b_rg[l], w_ig[l], b_ig[l],
                  lru_lambda[l], norm_lru_out[l], norm_attn_out[l], w_out[l], norm_cross[l], norm_mem[l],
                  w_xq[l], w_xkv[l], w_xo[l], norm_ffn[l], w_router_group[l], b_router_group[l],
                  w_router_expert[l], b_router_expert[l], w_gate_up[l], w_down[l], norm_final)
```

```python
import functools

import jax
import jax.numpy as jnp
import numpy as np
from jax import lax
from jax.experimental import pallas as pl
from jax.experimental.pallas import tpu as pltpu

F32 = jnp.float32
BF16 = jnp.bfloat16

N_LRU_BLOCKS = 8
CONV_WIDTH = 4
RG_C = 8.0
N_HEADS = 8
HEAD_DIM = 64
ROT_DIM = HEAD_DIM // 4
ROPE_THETA = 500000.0
MOBA_BLOCK = 256
MOBA_TOPK = 3
N_XHEADS = 4
N_GROUPS = 4
EXPERTS_PER_GROUP = 8
N_EXPERTS = N_GROUPS * EXPERTS_PER_GROUP
EXPERT_TOPK = 2
EPS = 1e-6

LANES = 128
SUBLANES = 8
NEG_BIG = -1e30
LOG2_E = 1.4426950408889634
MOBA_LOOKAHEAD = 2
EXPERT_ROWS = 512
EXPERT_CHUNKS = 4
MIXER_SUB_ROWS = 256
BF16_ROWS = 2 * SUBLANES
ROUTER_ROWS = -(-(SUBLANES + N_EXPERTS) // BF16_ROWS) * BF16_ROWS
VMEM_LIMIT = 56 * 1024 * 1024

NT_DIMS = (((1,), (1,)), ((), ()))


def _rms(x, g):
    return x * lax.rsqrt(jnp.mean(x * x, axis=-1, keepdims=True) + EPS) * g


def _params(sem):
    return pltpu.CompilerParams(dimension_semantics=sem, vmem_limit_bytes=VMEM_LIMIT)


def _memkv_kernel(mem_ref, g_ref, w_ref, k_ref, v_ref):
    d = mem_ref.shape[-1]
    mn = _rms(mem_ref[0], g_ref[...]).astype(BF16)
    kv = jnp.dot(mn, w_ref[...], preferred_element_type=F32)
    k_ref[0] = kv[:, :d].astype(BF16)
    v_ref[0] = kv[:, d:].astype(BF16)


def _memkv(mem, g, w_xkv):
    b, m, d = mem.shape
    return pl.pallas_call(
        _memkv_kernel,
        grid=(b,),
        in_specs=[pl.BlockSpec((1, m, d), lambda i: (i, 0, 0)),
                  pl.BlockSpec((1, d), lambda i: (0, 0)),
                  pl.BlockSpec((d, 2 * d), lambda i: (0, 0))],
        out_specs=[pl.BlockSpec((1, m, d), lambda i: (i, 0, 0)),
                   pl.BlockSpec((1, m, d), lambda i: (i, 0, 0))],
        out_shape=[jax.ShapeDtypeStruct((b, m, d), BF16)] * 2,
        compiler_params=_params(("arbitrary",)),
        name="memkv",
    )(mem, g, w_xkv)


def _mixer_in_kernel(x_ref, gmix_ref, win_ref, convw_ref, convb_ref, wg_ref, bg_ref, lam_ref, glru_ref,
                     cos_ref, sina_ref, sinb_ref,
                     ylru_ref, q_ref, k_ref, v_ref, xpad_ref, hcar_ref, ascan_ref, uscan_ref):
    c = ylru_ref.shape[-1]
    si = pl.program_id(1)

    @pl.when(si == 0)
    def _():
        xpad_ref[0:SUBLANES, :] = jnp.zeros((SUBLANES, c), F32)
        hcar_ref[...] = jnp.zeros_like(hcar_ref)

    n_sub = x_ref.shape[1] // ascan_ref.shape[1]
    ts = ascan_ref.shape[1]
    projs = []
    for sub in range(n_sub):
        xn = _rms(x_ref[0, sub * ts:(sub + 1) * ts, :], gmix_ref[...]).astype(BF16)
        projs.append(jnp.dot(xn, win_ref[...], preferred_element_type=F32))
    for sub in range(n_sub):
        _mixer_stage2(projs[sub], slice(sub * ts, (sub + 1) * ts), convw_ref, convb_ref, wg_ref, bg_ref, lam_ref,
                      glru_ref, cos_ref, sina_ref, sinb_ref, ylru_ref, q_ref, k_ref, v_ref, xpad_ref, hcar_ref,
                      ascan_ref, uscan_ref)


def _mixer_stage2(proj, rows_out, convw_ref, convb_ref, wg_ref, bg_ref, lam_ref, glru_ref, cos_ref, sina_ref,
                  sinb_ref, ylru_ref, q_ref, k_ref, v_ref, xpad_ref, hcar_ref, ascan_ref, uscan_ref):
    ts = proj.shape[0]
    c = ylru_ref.shape[-1]
    x_lru = proj[:, 0:c]
    g_lru = proj[:, c:2 * c]
    q = proj[:, 2 * c:3 * c]
    k = proj[:, 3 * c:4 * c]
    v = proj[:, 4 * c:5 * c]

    xpad_ref[SUBLANES:SUBLANES + ts, :] = x_lru
    cw = convw_ref[...]
    xc = convb_ref[...] + cw[3:4] * x_lru
    for j in range(1, CONV_WIDTH):
        xc = xc + cw[3 - j:4 - j] * xpad_ref[SUBLANES - j:SUBLANES - j + ts, :]
    xpad_ref[0:SUBLANES, :] = x_lru[ts - SUBLANES:ts, :]

    gates = jnp.dot(xc.astype(BF16), wg_ref[...], preferred_element_type=F32) + bg_ref[...]
    r = jax.nn.sigmoid(gates[:, 0:c])
    i = jax.nn.sigmoid(gates[:, c:2 * c])
    neg_lam = -lam_ref[...]
    softplus = jnp.maximum(neg_lam, 0.0) + jnp.log1p(jnp.exp(-jnp.abs(neg_lam)))
    log_a = -RG_C * r * softplus
    a = jnp.exp(log_a)
    u = jnp.sqrt(-jnp.tanh(log_a) * (a * a + 1.0)) * (i * xc)

    groups = ts // SUBLANES
    cols = c // LANES
    for j in range(cols):
        ascan_ref[j] = a[:, j * LANES:(j + 1) * LANES]
        uscan_ref[j] = u[:, j * LANES:(j + 1) * LANES]
    for r in range(SUBLANES):
        rows = pl.ds(r, groups, stride=SUBLANES)
        a_r = jnp.concatenate([ascan_ref[j, rows, :] for j in range(cols)], axis=1)
        u_r = jnp.concatenate([uscan_ref[j, rows, :] for j in range(cols)], axis=1)
        if r:
            u_r = a_r * u_acc + u_r
            a_r = a_r * a_acc
            for j in range(cols):
                ascan_ref[j, rows, :] = a_r[:, j * LANES:(j + 1) * LANES]
                uscan_ref[j, rows, :] = u_r[:, j * LANES:(j + 1) * LANES]
        a_acc, u_acc = a_r, u_r
    carry = hcar_ref[...]
    hs = []
    for gi in range(groups):
        blk = slice(gi * SUBLANES, (gi + 1) * SUBLANES)
        a_g = jnp.concatenate([ascan_ref[j, blk, :] for j in range(cols)], axis=1)
        u_g = jnp.concatenate([uscan_ref[j, blk, :] for j in range(cols)], axis=1)
        hs.append(a_g * carry + u_g)
        carry = a_acc[gi:gi + 1] * carry + u_acc[gi:gi + 1]
    h = jnp.concatenate(hs, axis=0)
    hcar_ref[...] = carry

    y = h * jax.nn.gelu(g_lru)
    ylru_ref[0, rows_out, :] = _rms(y, glru_ref[...]).astype(BF16)

    wide = lambda t_ref: jnp.concatenate([t_ref[rows_out, :]] * (c // LANES), axis=1)
    cos, sina, sinb = wide(cos_ref), wide(sina_ref), wide(sinb_ref)
    half = ROT_DIM // 2

    def rope(t):
        return t * cos + pltpu.roll(t, c - half, axis=1) * sina + pltpu.roll(t, half, axis=1) * sinb

    q_ref[0, rows_out, :] = rope(q)
    k_ref[0, rows_out, :] = rope(k)
    v_ref[0, rows_out, :] = v.astype(BF16)


def _mixer_in(x, gmix, w_in, conv_w, conv_b, w_gates, b_gates, lam, glru, cos_t, sina_t, sinb_t, ts):
    b, s, d = x.shape
    c = conv_w.shape[-1]
    sub = min(ts, MIXER_SUB_ROWS)
    full = lambda shape: pl.BlockSpec(shape, lambda bi, si: (0,) * len(shape))
    tab = pl.BlockSpec((ts, LANES), lambda bi, si: (si, 0))
    seq = lambda: pl.BlockSpec((1, ts, c), lambda bi, si: (bi, si, 0))
    return pl.pallas_call(
        _mixer_in_kernel,
        grid=(b, s // ts),
        in_specs=[pl.BlockSpec((1, ts, d), lambda bi, si: (bi, si, 0)),
                  full((1, d)), full(w_in.shape), full(conv_w.shape), full((1, c)),
                  full(w_gates.shape), full((1, 2 * c)), full((1, c)), full((1, c)),
                  tab, tab, tab],
        out_specs=[seq(), seq(), seq(), seq()],
        out_shape=[jax.ShapeDtypeStruct((b, s, c), BF16),
                   jax.ShapeDtypeStruct((b, s, c), F32),
                   jax.ShapeDtypeStruct((b, s, c), F32),
                   jax.ShapeDtypeStruct((b, s, c), BF16)],
        scratch_shapes=[pltpu.VMEM((sub + SUBLANES, c), F32), pltpu.VMEM((1, c), F32),
                        pltpu.VMEM((c // LANES, sub, LANES), F32), pltpu.VMEM((c // LANES, sub, LANES), F32)],
        compiler_params=_params(("arbitrary", "arbitrary")),
        name="mixer_in",
    )(x, gmix, w_in, conv_w, conv_b, w_gates, b_gates, lam, glru, cos_t, sina_t, sinb_t)


def _moba_kernel(q_ref, k_ref, v_ref, o_ref, qa_ref, ka_ref):
    s_len = q_ref.shape[1]
    nb = s_len // MOBA_BLOCK
    n_pick = min(MOBA_TOPK, nb)
    q = q_ref[0]
    k = k_ref[0]
    lane = lax.broadcasted_iota(jnp.int32, (1, LANES), 1)
    kmean = jnp.concatenate(
        [jnp.sum(k[n * MOBA_BLOCK:(n + 1) * MOBA_BLOCK], axis=0, keepdims=True) for n in range(nb)],
        axis=0) * (1.0 / MOBA_BLOCK)

    blk_row = lax.broadcasted_iota(jnp.int32, (nb, s_len), 0)
    q_blk = lax.broadcasted_iota(jnp.int32, (nb, s_len), 1) // MOBA_BLOCK
    past = blk_row < q_blk
    key_blk = lax.broadcasted_iota(jnp.int32, (s_len, LANES), 0) // MOBA_BLOCK
    lane_full = lax.broadcasted_iota(jnp.int32, (s_len, LANES), 1)

    for h in range(2):
        own = (lane >= h * HEAD_DIM) & (lane < (h + 1) * HEAD_DIM)
        off = (1 - h) * HEAD_DIM
        gate = lax.dot_general(jnp.where(own, kmean, 0.0), q, NT_DIMS,
                               precision=lax.Precision.HIGHEST, preferred_element_type=F32)
        gate = jnp.where(past, gate, -jnp.inf)
        rank = jnp.zeros((nb, s_len), jnp.int32)
        for m in range(nb):
            gm = gate[m:m + 1, :]
            ahead = (gm > gate) | ((gm == gate) & (m < blk_row))
            rank = rank + ahead.astype(jnp.int32)
        allowed = ((rank < n_pick) & past) | (blk_row == q_blk)
        bias = jnp.where(allowed, 0.0, NEG_BIG)
        pieces = []
        if off:
            pieces.append(jnp.zeros((off, s_len), F32))
        pieces.append(bias)
        if LANES - off - nb:
            pieces.append(jnp.zeros((LANES - off - nb, s_len), F32))
        bias_lanes = jnp.concatenate(pieces, axis=0).T
        qa_ref[h] = jnp.where(own, q * (HEAD_DIM ** -0.5 * LOG2_E), bias_lanes).astype(BF16)
        onehot = (lane_full - off == key_blk).astype(F32)
        ka_ref[h] = jnp.where(own, k, onehot).astype(BF16)

    own0 = lane < HEAD_DIM

    def scores(qi, h):
        nk = (qi + 1) * MOBA_BLOCK
        qa = qa_ref[h, qi * MOBA_BLOCK:(qi + 1) * MOBA_BLOCK, :]
        return lax.dot_general(qa, ka_ref[h, 0:nk, :], NT_DIMS, preferred_element_type=F32)

    key_in_blk = lax.broadcasted_iota(jnp.int32, (MOBA_BLOCK, MOBA_BLOCK), 1)
    q_in_blk = lax.broadcasted_iota(jnp.int32, (MOBA_BLOCK, MOBA_BLOCK), 0)
    causal = key_in_blk <= q_in_blk
    units = [(qi, h) for qi in range(nb) for h in range(2)]
    ahead = [scores(*u) for u in units[:MOBA_LOOKAHEAD]]
    outs = []

    def weighted_values(p, l, qi, h):
        outs.append(jnp.dot(p, v_ref[0, 0:(qi + 1) * MOBA_BLOCK, :], preferred_element_type=F32) / l)
        if h:
            o_ref[0, qi * MOBA_BLOCK:(qi + 1) * MOBA_BLOCK, :] = jnp.where(own0, outs[0], outs[1])
            outs.clear()

    pending = None
    for n, (qi, h) in enumerate(units):
        s = ahead.pop(0)
        if n + MOBA_LOOKAHEAD < len(units):
            ahead.append(scores(*units[n + MOBA_LOOKAHEAD]))
        n_past = qi * MOBA_BLOCK
        s_own = jnp.where(causal, s[:, n_past:], NEG_BIG)
        s = jnp.concatenate([s[:, :n_past], s_own], axis=1) if qi else s_own
        p = jnp.exp2(s - jnp.max(s, axis=-1, keepdims=True))
        l = jnp.sum(p, axis=-1, keepdims=True)
        if pending is not None:
            weighted_values(*pending)
        pending = (p.astype(BF16), l, qi, h)
    weighted_values(*pending)


def _moba(q, k, v):
    b, s, c = q.shape
    spec = lambda: pl.BlockSpec((1, s, LANES), lambda bi, hi: (bi, 0, hi))
    return pl.pallas_call(
        _moba_kernel,
        grid=(b, c // LANES),
        in_specs=[spec(), spec(), spec()],
        out_specs=spec(),
        out_shape=jax.ShapeDtypeStruct((b, s, c), F32),
        scratch_shapes=[pltpu.VMEM((2, s, LANES), BF16), pltpu.VMEM((2, s, LANES), BF16)],
        compiler_params=_params(("arbitrary", "arbitrary")),
        name="moba",
    )(q, k, v)


def _post_kernel(x_ref, ylru_ref, yatt_ref, gatt_ref, woa_ref, wob_ref, gcross_ref, wxq_ref, kx_ref, vx_ref,
                 wxo_ref, gffn_ref, wr_ref, br_ref, tri_ref,
                 h2_ref, hn2_ref, eid_ref, gcol_ref, rank_ref, cnt_ref, carry_ref):
    tm = x_ref.shape[1]
    d = x_ref.shape[2]
    xd = d // N_XHEADS
    first = (pl.program_id(0) == 0) & (pl.program_id(1) == 0)

    @pl.when(first)
    def _():
        carry_ref[...] = jnp.zeros_like(carry_ref)

    ya = _rms(yatt_ref[0], gatt_ref[...]).astype(BF16)
    mix = (jnp.dot(ylru_ref[0], woa_ref[...], preferred_element_type=F32)
           + jnp.dot(ya, wob_ref[...], preferred_element_type=F32))
    h1 = x_ref[0] + mix

    hn = _rms(h1, gcross_ref[...]).astype(BF16)
    qx = jnp.dot(hn, wxq_ref[...], preferred_element_type=F32)
    heads = []
    for hh in range(N_XHEADS):
        qh = (qx[:, hh * xd:(hh + 1) * xd] * (xd ** -0.5)).astype(BF16)
        s = lax.dot_general(qh, kx_ref[0, :, hh * xd:(hh + 1) * xd], NT_DIMS, preferred_element_type=F32)
        p = jnp.exp(s - jnp.max(s, axis=-1, keepdims=True))
        l = jnp.sum(p, axis=-1, keepdims=True)
        o = jnp.dot(p.astype(BF16), vx_ref[0, :, hh * xd:(hh + 1) * xd], preferred_element_type=F32) / l
        heads.append(o.astype(BF16))
    h2 = h1 + jnp.dot(jnp.concatenate(heads, axis=1), wxo_ref[...], preferred_element_type=F32)
    hn2 = _rms(h2, gffn_ref[...])
    h2_ref[...] = h2
    hn2_ref[...] = hn2

    x_hi = hn2.astype(BF16)
    x_lo = (hn2 - x_hi.astype(F32)).astype(BF16)
    by_hi = lax.dot_general(wr_ref[...], x_hi, NT_DIMS, preferred_element_type=F32)
    by_lo = lax.dot_general(wr_ref[0:ROUTER_ROWS], x_lo, NT_DIMS, preferred_element_type=F32)
    logits = by_hi[0:ROUTER_ROWS] + by_hi[ROUTER_ROWS:2 * ROUTER_ROWS] + by_lo + br_ref[...]
    gl = logits[0:N_GROUPS]
    gmax = jnp.max(gl, axis=0, keepdims=True)
    gi = lax.broadcasted_iota(jnp.int32, gl.shape, 0)
    grp = jnp.min(jnp.where(gl == gmax, gi, N_GROUPS), axis=0, keepdims=True)
    g_w = 1.0 / jnp.sum(jnp.exp(gl - gmax), axis=0, keepdims=True)
    el = jnp.zeros((EXPERTS_PER_GROUP, tm), F32)
    for g in range(N_GROUPS):
        lo = SUBLANES + g * EXPERTS_PER_GROUP
        el = jnp.where(grp == g, logits[lo:lo + EXPERTS_PER_GROUP], el)
    ee = jnp.exp(el - jnp.max(el, axis=0, keepdims=True))
    ep = ee / jnp.sum(ee, axis=0, keepdims=True)
    ei = lax.broadcasted_iota(jnp.int32, ep.shape, 0)
    p1 = jnp.max(ep, axis=0, keepdims=True)
    i1 = jnp.min(jnp.where(ep == p1, ei, EXPERTS_PER_GROUP), axis=0, keepdims=True)
    ep_rest = jnp.where(ei == i1, -1.0, ep)
    p2 = jnp.max(ep_rest, axis=0, keepdims=True)
    i2 = jnp.min(jnp.where(ep_rest == p2, ei, EXPERTS_PER_GROUP), axis=0, keepdims=True)
    den = p1 + p2
    gate1 = g_w * p1 / den
    gate2 = g_w * p2 / den
    e1 = grp * EXPERTS_PER_GROUP + i1
    e2 = grp * EXPERTS_PER_GROUP + i2
    eid_ref[0:1, :] = e1
    eid_ref[1:2, :] = e2
    li = lax.broadcasted_iota(jnp.int32, (LANES, tm), 0)
    gcol_ref[...] = jnp.where(li == 0, gate1, jnp.where(li == 1, gate2, 0.0)).T

    xi = lax.broadcasted_iota(jnp.int32, (N_EXPERTS, tm), 0)
    oh1 = xi == e1
    oh2 = xi == e2
    cnt = oh1.astype(F32) + oh2.astype(F32)
    before = jnp.dot(cnt.astype(BF16), tri_ref[...], preferred_element_type=F32) + carry_ref[:, 0:1]
    rank_ref[0:1, :] = jnp.sum(jnp.where(oh1, before, 0.0), axis=0, keepdims=True).astype(jnp.int32)
    rank_ref[1:2, :] = jnp.sum(jnp.where(oh2, before, 0.0), axis=0, keepdims=True).astype(jnp.int32)
    carry_ref[...] = carry_ref[...] + jnp.sum(cnt, axis=1, keepdims=True)
    cnt_ref[...] = carry_ref[...]


def _post(x, ylru, yatt, gatt, wo_a, wo_b, gcross, wxq, kx, vx, wxo, gffn, wr_t, br_t, tri, tm):
    b, s, d = x.shape
    c = ylru.shape[-1]
    m = kx.shape[1]
    n = b * s
    nt = s // tm
    full = lambda shape: pl.BlockSpec(shape, lambda bi, si: (0,) * len(shape))
    tok = lambda rows: pl.BlockSpec((rows, tm), lambda bi, si: (0, bi * nt + si))
    return pl.pallas_call(
        _post_kernel,
        grid=(b, nt),
        in_specs=[pl.BlockSpec((1, tm, d), lambda bi, si: (bi, si, 0)),
                  pl.BlockSpec((1, tm, c), lambda bi, si: (bi, si, 0)),
                  pl.BlockSpec((1, tm, c), lambda bi, si: (bi, si, 0)),
                  full((1, c)), full((c, d)), full((c, d)), full((1, d)), full((d, d)),
                  pl.BlockSpec((1, m, d), lambda bi, si: (bi, 0, 0)),
                  pl.BlockSpec((1, m, d), lambda bi, si: (bi, 0, 0)),
                  full((d, d)), full((1, d)), full(wr_t.shape), full(br_t.shape), full((tm, tm))],
        out_specs=[pl.BlockSpec((tm, d), lambda bi, si: (bi * nt + si, 0)),
                   pl.BlockSpec((tm, d), lambda bi, si: (bi * nt + si, 0)),
                   tok(EXPERT_TOPK),
                   pl.BlockSpec((tm, LANES), lambda bi, si: (bi * nt + si, 0)),
                   tok(EXPERT_TOPK), full((N_EXPERTS, LANES))],
        out_shape=[jax.ShapeDtypeStruct((n, d), F32),
                   jax.ShapeDtypeStruct((n, d), F32),
                   jax.ShapeDtypeStruct((EXPERT_TOPK, n), jnp.int32),
                   jax.ShapeDtypeStruct((n, LANES), F32),
                   jax.ShapeDtypeStruct((EXPERT_TOPK, n), jnp.int32),
                   jax.ShapeDtypeStruct((N_EXPERTS, LANES), F32)],
        scratch_shapes=[pltpu.VMEM((N_EXPERTS, LANES), F32)],
        compiler_params=_params(("arbitrary", "arbitrary")),
        name="post",
    )(x, ylru, yatt, gatt, wo_a, wo_b, gcross, wxq, kx, vx, wxo, gffn, wr_t, br_t, tri)


def _row_copy(src_ref, src_row, dst_ref, dst_row, sem):
    return pltpu.make_async_copy(src_ref.at[pl.ds(src_row, 1)], dst_ref.at[pl.ds(dst_row, 1)], sem)


def _dest_kernel(ps_ref, eid_ref, rank_ref, dest_ref):
    eid = eid_ref[...]
    dest = rank_ref[...]
    for e in range(N_EXPERTS):
        dest = dest + jnp.where(eid == e, ps_ref[e], 0)
    dest_ref[...] = dest


def _dest(pad_starts, eid, rank):
    shape = (eid.size // LANES, LANES)
    spec = pl.BlockSpec(shape, lambda i, ps: (0, 0))
    return pl.pallas_call(
        _dest_kernel,
        grid_spec=pltpu.PrefetchScalarGridSpec(num_scalar_prefetch=1, grid=(1,), in_specs=[spec, spec],
                                               out_specs=spec),
        out_shape=jax.ShapeDtypeStruct(shape, jnp.int32),
        compiler_params=_params(("arbitrary",)),
        name="dest",
    )(pad_starts, eid.reshape(shape), rank.reshape(shape)).reshape(-1)


def _pad_pieces():
    piece = EXPERT_ROWS // 2
    while piece >= SUBLANES:
        yield piece
        piece //= 2


def _dispatch_kernel(dest_ref, fill_ref, hn_ref, xout_ref, inv_ref, zero_ref, sem, zsem):
    td = hn_ref.shape[0]
    n = dest_ref.shape[0] // EXPERT_TOPK
    step = pl.program_id(0)
    base = step * td

    def dump_rows(first, end):
        def group(g, carry):
            for i in range(SUBLANES):
                row = g * SUBLANES + i
                inv_ref[row] = EXPERT_TOPK * n + (row & (EXPERT_ROWS - 1))
            return carry

        lax.fori_loop(first // SUBLANES, end // SUBLANES, group, 0)

    def pad_copies(e, start):
        first = fill_ref[e]
        length = fill_ref[N_EXPERTS + e]
        head = jnp.minimum((-first) & (SUBLANES - 1), length)
        if start:
            dump_rows(first, first + length)

        def one(i, carry):
            cp = _row_copy(zero_ref, 0, xout_ref, first + i, zsem)
            cp.start() if start else cp.wait()
            return carry

        lax.fori_loop(0, head, one, 0)
        body = length - head
        for piece in _pad_pieces():
            offset = pl.multiple_of(first + head + (body & ~(2 * piece - 1)), SUBLANES)

            @pl.when((body & piece) != 0)
            def _():
                cp = pltpu.make_async_copy(zero_ref.at[pl.ds(0, piece)], xout_ref.at[pl.ds(offset, piece)], zsem)
                cp.start() if start else cp.wait()

    def tail_copies(start):
        piece = zero_ref.shape[0]

        def one(i, carry):
            cp = pltpu.make_async_copy(zero_ref, xout_ref.at[pl.ds(pl.multiple_of(i * piece, piece), piece)], zsem)
            cp.start() if start else cp.wait()
            return carry

        lax.fori_loop(fill_ref[2 * N_EXPERTS] // piece, xout_ref.shape[0] // piece, one, 0)
        if start:
            dump_rows(fill_ref[2 * N_EXPERTS], xout_ref.shape[0])

    @pl.when(step == 0)
    def _():
        zero_ref[...] = jnp.zeros_like(zero_ref)
        lax.fori_loop(0, N_EXPERTS, lambda e, c: (pad_copies(e, True), c)[1], 0)
        tail_copies(True)

    def start(r, carry):
        for slot in range(EXPERT_TOPK):
            src = slot * n + base + r
            row = dest_ref[src]
            _row_copy(hn_ref, r, xout_ref, row, sem).start()
            inv_ref[row] = src
        return carry

    lax.fori_loop(0, td, start, 0, unroll=8)
    for slot in range(EXPERT_TOPK):
        pltpu.make_async_copy(hn_ref, xout_ref.at[pl.ds(0, td)], sem).wait()

    @pl.when(step == 0)
    def _():
        lax.fori_loop(0, N_EXPERTS, lambda e, c: (pad_copies(e, False), c)[1], 0)
        tail_copies(False)


def _dispatch(dest_flat, fill, hn2, n_rows, td):
    n, d = hn2.shape
    return pl.pallas_call(
        _dispatch_kernel,
        grid_spec=pltpu.PrefetchScalarGridSpec(
            num_scalar_prefetch=2,
            grid=(n // td,),
            in_specs=[pl.BlockSpec((td, d), lambda i, dest, fill: (i, 0))],
            out_specs=[pl.BlockSpec(memory_space=pl.ANY), pl.BlockSpec(memory_space=pltpu.SMEM)],
            scratch_shapes=[pltpu.VMEM((EXPERT_ROWS // 2, d), F32),
                            pltpu.SemaphoreType.DMA(()), pltpu.SemaphoreType.DMA(())]),
        out_shape=[jax.ShapeDtypeStruct((n_rows, d), F32), jax.ShapeDtypeStruct((n_rows,), jnp.int32)],
        compiler_params=_params(("arbitrary",)),
        name="dispatch",
    )(dest_flat, fill, hn2)


def _experts_kernel(be_ref, nu_ref, inv_ref, x_ref, wgu_ref, wd_ref, ytok_ref, wgu_bf_ref, wd_bf_ref, ybuf_ref, sems):
    j = pl.program_id(0)
    rows = x_ref.shape[0]
    f = wd_ref.shape[1]
    n_used = nu_ref[0]
    par = j % 2
    last_blk = be_ref.shape[0] - 1

    pieces = SUBLANES

    def scatter_copy(blk_par, r, dst_row):
        dst = pl.multiple_of(dst_row * pieces, pieces)
        return pltpu.make_async_copy(ybuf_ref.at[blk_par, pl.ds(r * pieces, pieces)],
                                     ytok_ref.at[pl.ds(dst, pieces)], sems.at[blk_par])

    chunk = rows // EXPERT_CHUNKS

    def start_scatter_of_previous(part):
        base = (j - 1) * rows
        for r in range(part * chunk, (part + 1) * chunk):
            scatter_copy(1 - par, r, inv_ref[base + r]).start()

    def wait_scatter(blk_par):
        pltpu.make_async_copy(ybuf_ref.at[blk_par], ytok_ref.at[pl.ds(0, rows * pieces)], sems.at[blk_par]).wait()

    def mlp(part):
        sl = slice(part * chunk, (part + 1) * chunk)
        gu = jnp.dot(x_ref[sl, :].astype(BF16), wgu_bf_ref[...], preferred_element_type=F32)
        act = jax.nn.silu(gu[:, 0:f]) * gu[:, f:2 * f]
        y = jnp.dot(act.astype(BF16), wd_bf_ref[...], preferred_element_type=F32)
        for k in range(pieces):
            ybuf_ref[par, pl.ds(part * chunk * pieces + k, chunk, stride=pieces), :] = y[:, k * LANES:(k + 1) * LANES]

    jc = jnp.minimum(j, last_blk)
    @pl.when((j < n_used) & ((j == 0) | (be_ref[jc] != be_ref[jnp.maximum(jc, 1) - 1])))
    def _():
        wgu_bf_ref[...] = wgu_ref[0].astype(BF16)
        wd_bf_ref[...] = wd_ref[0].astype(BF16)

    @pl.when((j >= 2) & (j <= n_used))
    def _():
        wait_scatter(par)

    @pl.when(j == 0)
    def _():
        ybuf_ref[1] = jnp.zeros(ybuf_ref.shape[1:], F32)
        cp = pltpu.make_async_copy(ybuf_ref.at[1], ytok_ref.at[pl.ds(ytok_ref.shape[0] - rows * pieces, rows * pieces)],
                                   sems.at[1])
        cp.start()
        cp.wait()

    @pl.when((j == 0) & (j < n_used))
    def _():
        for part in range(EXPERT_CHUNKS):
            mlp(part)

    @pl.when((j >= 1) & (j < n_used))
    def _():
        for part in range(EXPERT_CHUNKS):
            start_scatter_of_previous(part)
            mlp(part)

    @pl.when((j >= 1) & (j == n_used))
    def _():
        for part in range(EXPERT_CHUNKS):
            start_scatter_of_previous(part)
        wait_scatter(1 - par)


def _experts(blk_expert, n_used, inv, x_buf, w_gate_up, w_down, n_out_rows):
    p, d = x_buf.shape
    f = w_down.shape[1]
    r = EXPERT_ROWS
    n_blk = p // r
    assert d == SUBLANES * LANES, "a row must be exactly one (8, 128) tile for the tile-contiguous output"
    x_map = lambda j, be, nu, inv: (jnp.minimum(j, jnp.maximum(nu[0], 1) - 1), 0)
    w_map = lambda j, be, nu, inv: (be[jnp.minimum(j, n_blk - 1)], 0, 0)
    return pl.pallas_call(
        _experts_kernel,
        grid_spec=pltpu.PrefetchScalarGridSpec(
            num_scalar_prefetch=3,
            grid=(n_blk + 1,),
            in_specs=[pl.BlockSpec((r, d), x_map),
                      pl.BlockSpec((1, d, 2 * f), w_map),
                      pl.BlockSpec((1, f, d), w_map)],
            out_specs=pl.BlockSpec(memory_space=pl.ANY),
            scratch_shapes=[pltpu.VMEM((d, 2 * f), BF16), pltpu.VMEM((f, d), BF16),
                            pltpu.VMEM((2, r * SUBLANES, LANES), F32), pltpu.SemaphoreType.DMA((2,))]),
        out_shape=jax.ShapeDtypeStruct((n_out_rows * SUBLANES, LANES), F32),
        compiler_params=_params(("arbitrary",)),
        name="experts",
    )(blk_expert, n_used, inv, x_buf, w_gate_up, w_down)


def _combine_kernel(h2_ref, gcol_ref, gfin_ref, y0_ref, y1_ref, o_ref):
    tc = h2_ref.shape[0]

    def rows_of(y_ref):
        return jnp.concatenate([y_ref[pl.ds(k, tc, stride=SUBLANES), :] for k in range(SUBLANES)], axis=1)

    g = gcol_ref[...]
    moe = g[:, 0:1] * rows_of(y0_ref) + g[:, 1:2] * rows_of(y1_ref)
    o_ref[...] = _rms(h2_ref[...] + moe, gfin_ref[...])


def _combine(h2, gcol, gfin, y_tok, tc):
    n, d = h2.shape
    nt = n // tc
    return pl.pallas_call(
        _combine_kernel,
        grid=(nt,),
        in_specs=[pl.BlockSpec((tc, d), lambda i: (i, 0)),
                  pl.BlockSpec((tc, LANES), lambda i: (i, 0)),
                  pl.BlockSpec((1, d), lambda i: (0, 0)),
                  pl.BlockSpec((tc * SUBLANES, LANES), lambda i: (i, 0)),
                  pl.BlockSpec((tc * SUBLANES, LANES), lambda i: (i + nt, 0))],
        out_specs=pl.BlockSpec((tc, d), lambda i: (i, 0)),
        out_shape=jax.ShapeDtypeStruct((n, d), F32),
        compiler_params=_params(("arbitrary",)),
        name="combine",
    )(h2, gcol, gfin, y_tok, y_tok)


def _rope_tables(s):
    half = ROT_DIM // 2
    inv_freq = (ROPE_THETA ** (-np.arange(0, ROT_DIM, 2, dtype=np.float32) / ROT_DIM)).astype(np.float32)
    ang = np.arange(s, dtype=np.float32)[:, None] * inv_freq[None, :]
    cos, sin = np.cos(ang), np.sin(ang)
    zeros = lambda w: np.zeros((s, w), np.float32)
    cos_h = np.concatenate([cos, cos, np.ones((s, HEAD_DIM - ROT_DIM), np.float32)], axis=1)
    sina_h = np.concatenate([-sin, zeros(HEAD_DIM - half)], axis=1)
    sinb_h = np.concatenate([zeros(half), sin, zeros(HEAD_DIM - ROT_DIM)], axis=1)
    tile = lambda t: jnp.asarray(np.tile(t, (1, LANES // HEAD_DIM)), F32)
    return tile(cos_h), tile(sina_h), tile(sinb_h)


def _block_diag(w):
    nblk, bi, bo = w.shape
    eye = jnp.eye(nblk, dtype=w.dtype)
    return jnp.einsum('hij,hg->higj', w, eye).reshape(nblk * bi, nblk * bo)


def _layer(h, mem, norm_mix, w_in, conv_w, conv_b, w_rg, b_rg, w_ig, b_ig, lru_lambda, norm_lru_out,
           norm_attn_out, w_out, norm_cross, norm_mem, w_xq, w_xkv, w_xo, norm_ffn, w_router_group,
           b_router_group, w_router_expert, b_router_expert, w_gate_up, w_down, norm_out):
    b, s, d = h.shape
    c = conv_w.shape[-1]
    n = b * s
    row = lambda t: t.reshape(1, -1)
    tm = 512 if s % 512 == 0 else MOBA_BLOCK
    ts = tm
    assert s % ts == 0 and s % tm == 0 and c == N_HEADS * HEAD_DIM

    kx, vx = _memkv(mem, row(norm_mem), w_xkv.astype(BF16))

    w_gates = jnp.concatenate([_block_diag(w_rg), _block_diag(w_ig)], axis=1).astype(BF16)
    b_gates = jnp.concatenate([b_rg, b_ig]).reshape(1, -1)
    cos_t, sina_t, sinb_t = _rope_tables(s)
    ylru, q, k, v = _mixer_in(h, row(norm_mix), w_in.astype(BF16), conv_w, row(conv_b), w_gates, b_gates,
                              row(lru_lambda), row(norm_lru_out), cos_t, sina_t, sinb_t, ts)
    yatt = _moba(q, k, v)

    w_out_b = w_out.astype(BF16)
    wr_t = jnp.zeros((ROUTER_ROWS, d), F32)
    wr_t = wr_t.at[0:N_GROUPS].set(w_router_group.T).at[SUBLANES:SUBLANES + N_EXPERTS].set(w_router_expert.T)
    wr_hi = wr_t.astype(BF16)
    wr_t = jnp.concatenate([wr_hi, (wr_t - wr_hi.astype(F32)).astype(BF16)], axis=0)
    br_t = jnp.zeros((ROUTER_ROWS, 1), F32)
    br_t = br_t.at[0:N_GROUPS, 0].set(b_router_group).at[SUBLANES:SUBLANES + N_EXPERTS, 0].set(b_router_expert)
    tri = jnp.asarray(np.triu(np.ones((tm, tm), np.float32), 1), BF16)
    h2, hn2, eid, gcol, rank, counts = _post(
        h, ylru, yatt, row(norm_attn_out), w_out_b[:c], w_out_b[c:], row(norm_cross), w_xq.astype(BF16),
        kx, vx, w_xo.astype(BF16), row(norm_ffn), wr_t, br_t, tri, tm)

    counts = counts[:, 0].astype(jnp.int32)
    padded = (counts + EXPERT_ROWS - 1) // EXPERT_ROWS * EXPERT_ROWS
    pad_ends = jnp.cumsum(padded)
    pad_starts = pad_ends - padded
    dest_flat = _dest(pad_starts.astype(jnp.int32), eid, rank)
    n_blocks = n * EXPERT_TOPK // EXPERT_ROWS + N_EXPERTS
    blk_first = jnp.arange(n_blocks, dtype=jnp.int32) * EXPERT_ROWS
    blk_expert = jnp.minimum(jnp.sum(blk_first[:, None] >= pad_ends[None, :], axis=1), N_EXPERTS - 1).astype(jnp.int32)
    n_used = (pad_ends[-1:] // EXPERT_ROWS).astype(jnp.int32)
    fill = jnp.concatenate([pad_starts + counts, padded - counts, pad_ends[-1:]]).astype(jnp.int32)

    x_buf, inv = _dispatch(dest_flat, fill, hn2, n_blocks * EXPERT_ROWS, MOBA_BLOCK)
    y_tok = _experts(blk_expert, n_used, inv, x_buf, w_gate_up, w_down, EXPERT_TOPK * n + EXPERT_ROWS)
    out = _combine(h2, gcol, row(norm_out), y_tok, tm)
    return out.reshape(b, s, d)


def kernel(x, mem, norm_mix, w_in, conv_w, conv_b, w_rg, b_rg, w_ig, b_ig, lru_lambda, norm_lru_out, norm_attn_out,
           w_out, norm_cross, norm_mem, w_xq, w_xkv, w_xo, norm_ffn, w_router_group, b_router_group,
           w_router_expert, b_router_expert, w_gate_up, w_down, norm_final):
    depth = norm_mix.shape[0]
    assert depth == 1, "the fused final norm assumes a single layer"
    l = 0
    return _layer(x, mem, norm_mix[l], w_in[l], conv_w[l], conv_b[l], w_rg[l], b_rg[l], w_ig[l], b_ig[l],
                  lru_lambda[l], norm_lru_out[l], norm_attn_out[l], w_out[l], norm_cross[l], norm_mem[l],
                  w_xq[l], w_xkv[l], w_xo[l], norm_ffn[l], w_router_group[l], b_router_group[l],
                  w_router_expert[l], b_router_expert[l], w_gate_up[l], w_down[l], norm_final)
```

```python
import functools

import jax
import jax.numpy as jnp
import numpy as np
from jax import lax
from jax.experimental import pallas as pl
from jax.experimental.pallas import tpu as pltpu

F32 = jnp.float32
BF16 = jnp.bfloat16

N_LRU_BLOCKS = 8
CONV_WIDTH = 4
RG_C = 8.0
N_HEADS = 8
HEAD_DIM = 64
ROT_DIM = HEAD_DIM // 4
ROPE_THETA = 500000.0
MOBA_BLOCK = 256
MOBA_TOPK = 3
N_XHEADS = 4
N_GROUPS = 4
EXPERTS_PER_GROUP = 8
N_EXPERTS = N_GROUPS * EXPERTS_PER_GROUP
EXPERT_TOPK = 2
EPS = 1e-6

LANES = 128
SUBLANES = 8
NEG_BIG = -1e30
LOG2_E = 1.4426950408889634
MOBA_LOOKAHEAD = 2
EXPERT_ROWS = 512
EXPERT_CHUNKS = 4
MIXER_SUB_ROWS = 256
BF16_ROWS = 2 * SUBLANES
ROUTER_ROWS = -(-(SUBLANES + N_EXPERTS) // BF16_ROWS) * BF16_ROWS
VMEM_LIMIT = 56 * 1024 * 1024

NT_DIMS = (((1,), (1,)), ((), ()))


def _rms(x, g):
    return x * lax.rsqrt(jnp.mean(x * x, axis=-1, keepdims=True) + EPS) * g


def _params(sem):
    return pltpu.CompilerParams(dimension_semantics=sem, vmem_limit_bytes=VMEM_LIMIT)


def _memkv_kernel(mem_ref, g_ref, w_ref, k_ref, v_ref):
    d = mem_ref.shape[-1]
    mn = _rms(mem_ref[0], g_ref[...]).astype(BF16)
    kv = jnp.dot(mn, w_ref[...], preferred_element_type=F32)
    k_ref[0] = kv[:, :d].astype(BF16)
    v_ref[0] = kv[:, d:].astype(BF16)


def _memkv(mem, g, w_xkv):
    b, m, d = mem.shape
    return pl.pallas_call(
        _memkv_kernel,
        grid=(b,),
        in_specs=[pl.BlockSpec((1, m, d), lambda i: (i, 0, 0)),
                  pl.BlockSpec((1, d), lambda i: (0, 0)),
                  pl.BlockSpec((d, 2 * d), lambda i: (0, 0))],
        out_specs=[pl.BlockSpec((1, m, d), lambda i: (i, 0, 0)),
                   pl.BlockSpec((1, m, d), lambda i: (i, 0, 0))],
        out_shape=[jax.ShapeDtypeStruct((b, m, d), BF16)] * 2,
        compiler_params=_params(("arbitrary",)),
        name="memkv",
    )(mem, g, w_xkv)


def _mixer_in_kernel(x_ref, gmix_ref, win_ref, convw_ref, convb_ref, wg_ref, bg_ref, lam_ref, glru_ref,
                     cos_ref, sina_ref, sinb_ref,
                     ylru_ref, q_ref, k_ref, v_ref, xpad_ref, hcar_ref, ascan_ref, uscan_ref):
    c = ylru_ref.shape[-1]
    si = pl.program_id(1)

    @pl.when(si == 0)
    def _():
        xpad_ref[0:SUBLANES, :] = jnp.zeros((SUBLANES, c), F32)
        hcar_ref[...] = jnp.zeros_like(hcar_ref)

    n_sub = x_ref.shape[1] // ascan_ref.shape[1]
    ts = ascan_ref.shape[1]
    projs = []
    for sub in range(n_sub):
        xn = _rms(x_ref[0, sub * ts:(sub + 1) * ts, :], gmix_ref[...]).astype(BF16)
        projs.append(jnp.dot(xn, win_ref[...], preferred_element_type=F32))
    for sub in range(n_sub):
        _mixer_stage2(projs[sub], slice(sub * ts, (sub + 1) * ts), convw_ref, convb_ref, wg_ref, bg_ref, lam_ref,
                      glru_ref, cos_ref, sina_ref, sinb_ref, ylru_ref, q_ref, k_ref, v_ref, xpad_ref, hcar_ref,
                      ascan_ref, uscan_ref)


def _mixer_stage2(proj, rows_out, convw_ref, convb_ref, wg_ref, bg_ref, lam_ref, glru_ref, cos_ref, sina_ref,
                  sinb_ref, ylru_ref, q_ref, k_ref, v_ref, xpad_ref, hcar_ref, ascan_ref, uscan_ref):
    ts = proj.shape[0]
    c = ylru_ref.shape[-1]
    x_lru = proj[:, 0:c]
    g_lru = proj[:, c:2 * c]
    q = proj[:, 2 * c:3 * c]
    k = proj[:, 3 * c:4 * c]
    v = proj[:, 4 * c:5 * c]

    xpad_ref[SUBLANES:SUBLANES + ts, :] = x_lru
    cw = convw_ref[...]
    xc = convb_ref[...] + cw[3:4] * x_lru
    for j in range(1, CONV_WIDTH):
        xc = xc + cw[3 - j:4 - j] * xpad_ref[SUBLANES - j:SUBLANES - j + ts, :]
    xpad_ref[0:SUBLANES, :] = x_lru[ts - SUBLANES:ts, :]

    gates = jnp.dot(xc.astype(BF16), wg_ref[...], preferred_element_type=F32) + bg_ref[...]
    r = jax.nn.sigmoid(gates[:, 0:c])
    i = jax.nn.sigmoid(gates[:, c:2 * c])
    neg_lam = -lam_ref[...]
    softplus = jnp.maximum(neg_lam, 0.0) + jnp.log1p(jnp.exp(-jnp.abs(neg_lam)))
    log_a = -RG_C * r * softplus
    a = jnp.exp(log_a)
    u = jnp.sqrt(-jnp.tanh(log_a) * (a * a + 1.0)) * (i * xc)

    groups = ts // SUBLANES
    cols = c // LANES
    for j in range(cols):
        ascan_ref[j] = a[:, j * LANES:(j + 1) * LANES]
        uscan_ref[j] = u[:, j * LANES:(j + 1) * LANES]
    for r in range(SUBLANES):
        rows = pl.ds(r, groups, stride=SUBLANES)
        a_r = jnp.concatenate([ascan_ref[j, rows, :] for j in range(cols)], axis=1)
        u_r = jnp.concatenate([uscan_ref[j, rows, :] for j in range(cols)], axis=1)
        if r:
            u_r = a_r * u_acc + u_r
            a_r = a_r * a_acc
            for j in range(cols):
                ascan_ref[j, rows, :] = a_r[:, j * LANES:(j + 1) * LANES]
                uscan_ref[j, rows, :] = u_r[:, j * LANES:(j + 1) * LANES]
        a_acc, u_acc = a_r, u_r
    carry = hcar_ref[...]
    hs = []
    for gi in range(groups):
        blk = slice(gi * SUBLANES, (gi + 1) * SUBLANES)
        a_g = jnp.concatenate([ascan_ref[j, blk, :] for j in range(cols)], axis=1)
        u_g = jnp.concatenate([uscan_ref[j, blk, :] for j in range(cols)], axis=1)
        hs.append(a_g * carry + u_g)
        carry = a_acc[gi:gi + 1] * carry + u_acc[gi:gi + 1]
    h = jnp.concatenate(hs, axis=0)
    hcar_ref[...] = carry

    y = h * jax.nn.gelu(g_lru)
    ylru_ref[0, rows_out, :] = _rms(y, glru_ref[...]).astype(BF16)

    wide = lambda t_ref: jnp.concatenate([t_ref[rows_out, :]] * (c // LANES), axis=1)
    cos, sina, sinb = wide(cos_ref), wide(sina_ref), wide(sinb_ref)
    half = ROT_DIM // 2

    def rope(t):
        return t * cos + pltpu.roll(t, c - half, axis=1) * sina + pltpu.roll(t, half, axis=1) * sinb

    q_ref[0, rows_out, :] = rope(q)
    k_ref[0, rows_out, :] = rope(k)
    v_ref[0, rows_out, :] = v.astype(BF16)


def _mixer_in(x, gmix, w_in, conv_w, conv_b, w_gates, b_gates, lam, glru, cos_t, sina_t, sinb_t, ts):
    b, s, d = x.shape
    c = conv_w.shape[-1]
    sub = min(ts, MIXER_SUB_ROWS)
    full = lambda shape: pl.BlockSpec(shape, lambda bi, si: (0,) * len(shape))
    tab = pl.BlockSpec((ts, LANES), lambda bi, si: (si, 0))
    seq = lambda: pl.BlockSpec((1, ts, c), lambda bi, si: (bi, si, 0))
    return pl.pallas_call(
        _mixer_in_kernel,
        grid=(b, s // ts),
        in_specs=[pl.BlockSpec((1, ts, d), lambda bi, si: (bi, si, 0)),
                  full((1, d)), full(w_in.shape), full(conv_w.shape), full((1, c)),
                  full(w_gates.shape), full((1, 2 * c)), full((1, c)), full((1, c)),
                  tab, tab, tab],
        out_specs=[seq(), seq(), seq(), seq()],
        out_shape=[jax.ShapeDtypeStruct((b, s, c), BF16),
                   jax.ShapeDtypeStruct((b, s, c), F32),
                   jax.ShapeDtypeStruct((b, s, c), F32),
                   jax.ShapeDtypeStruct((b, s, c), BF16)],
        scratch_shapes=[pltpu.VMEM((sub + SUBLANES, c), F32), pltpu.VMEM((1, c), F32),
                        pltpu.VMEM((c // LANES, sub, LANES), F32), pltpu.VMEM((c // LANES, sub, LANES), F32)],
        compiler_params=_params(("arbitrary", "arbitrary")),
        name="mixer_in",
    )(x, gmix, w_in, conv_w, conv_b, w_gates, b_gates, lam, glru, cos_t, sina_t, sinb_t)


def _moba_kernel(q_ref, k_ref, v_ref, o_ref, qa_ref, ka_ref):
    s_len = q_ref.shape[1]
    nb = s_len // MOBA_BLOCK
    n_pick = min(MOBA_TOPK, nb)
    q = q_ref[0]
    k = k_ref[0]
    lane = lax.broadcasted_iota(jnp.int32, (1, LANES), 1)
    kmean = jnp.concatenate(
        [jnp.sum(k[n * MOBA_BLOCK:(n + 1) * MOBA_BLOCK], axis=0, keepdims=True) for n in range(nb)],
        axis=0) * (1.0 / MOBA_BLOCK)

    blk_row = lax.broadcasted_iota(jnp.int32, (nb, s_len), 0)
    q_blk = lax.broadcasted_iota(jnp.int32, (nb, s_len), 1) // MOBA_BLOCK
    past = blk_row < q_blk
    key_blk = lax.broadcasted_iota(jnp.int32, (s_len, LANES), 0) // MOBA_BLOCK
    lane_full = lax.broadcasted_iota(jnp.int32, (s_len, LANES), 1)

    for h in range(2):
        own = (lane >= h * HEAD_DIM) & (lane < (h + 1) * HEAD_DIM)
        off = (1 - h) * HEAD_DIM
        gate = lax.dot_general(jnp.where(own, kmean, 0.0), q, NT_DIMS,
                               precision=lax.Precision.HIGHEST, preferred_element_type=F32)
        gate = jnp.where(past, gate, -jnp.inf)
        rank = jnp.zeros((nb, s_len), jnp.int32)
        for m in range(nb):
            gm = gate[m:m + 1, :]
            ahead = (gm > gate) | ((gm == gate) & (m < blk_row))
            rank = rank + ahead.astype(jnp.int32)
        allowed = ((rank < n_pick) & past) | (blk_row == q_blk)
        bias = jnp.where(allowed, 0.0, NEG_BIG)
        pieces = []
        if off:
            pieces.append(jnp.zeros((off, s_len), F32))
        pieces.append(bias)
        if LANES - off - nb:
            pieces.append(jnp.zeros((LANES - off - nb, s_len), F32))
        bias_lanes = jnp.concatenate(pieces, axis=0).T
        qa_ref[h] = jnp.where(own, q * (HEAD_DIM ** -0.5 * LOG2_E), bias_lanes).astype(BF16)
        onehot = (lane_full - off == key_blk).astype(F32)
        ka_ref[h] = jnp.where(own, k, onehot).astype(BF16)

    own0 = lane < HEAD_DIM

    def scores(qi, h):
        nk = (qi + 1) * MOBA_BLOCK
        qa = qa_ref[h, qi * MOBA_BLOCK:(qi + 1) * MOBA_BLOCK, :]
        return lax.dot_general(qa, ka_ref[h, 0:nk, :], NT_DIMS, preferred_element_type=F32)

    key_in_blk = lax.broadcasted_iota(jnp.int32, (MOBA_BLOCK, MOBA_BLOCK), 1)
    q_in_blk = lax.broadcasted_iota(jnp.int32, (MOBA_BLOCK, MOBA_BLOCK), 0)
    causal = key_in_blk <= q_in_blk
    units = [(qi, h) for qi in range(nb) for h in range(2)]
    ahead = [scores(*u) for u in units[:MOBA_LOOKAHEAD]]
    outs = []

    def weighted_values(p, l, qi, h):
        outs.append(jnp.dot(p, v_ref[0, 0:(qi + 1) * MOBA_BLOCK, :], preferred_element_type=F32) / l)
        if h:
            o_ref[0, qi * MOBA_BLOCK:(qi + 1) * MOBA_BLOCK, :] = jnp.where(own0, outs[0], outs[1])
            outs.clear()

    pending = None
    for n, (qi, h) in enumerate(units):
        s = ahead.pop(0)
        if n + MOBA_LOOKAHEAD < len(units):
            ahead.append(scores(*units[n + MOBA_LOOKAHEAD]))
        n_past = qi * MOBA_BLOCK
        s_own = jnp.where(causal, s[:, n_past:], NEG_BIG)
        s = jnp.concatenate([s[:, :n_past], s_own], axis=1) if qi else s_own
        p = jnp.exp2(s - jnp.max(s, axis=-1, keepdims=True))
        l = jnp.sum(p, axis=-1, keepdims=True)
        if pending is not None:
            weighted_values(*pending)
        pending = (p.astype(BF16), l, qi, h)
    weighted_values(*pending)


def _moba(q, k, v):
    b, s, c = q.shape
    spec = lambda: pl.BlockSpec((1, s, LANES), lambda bi, hi: (bi, 0, hi))
    return pl.pallas_call(
        _moba_kernel,
        grid=(b, c // LANES),
        in_specs=[spec(), spec(), spec()],
        out_specs=spec(),
        out_shape=jax.ShapeDtypeStruct((b, s, c), F32),
        scratch_shapes=[pltpu.VMEM((2, s, LANES), BF16), pltpu.VMEM((2, s, LANES), BF16)],
        compiler_params=_params(("arbitrary", "arbitrary")),
        name="moba",
    )(q, k, v)


def _post_kernel(x_ref, ylru_ref, yatt_ref, gatt_ref, woa_ref, wob_ref, gcross_ref, wxq_ref, kx_ref, vx_ref,
                 wxo_ref, gffn_ref, wr_ref, br_ref, tri_ref, elow_ref,
                 h2_ref, hn2_ref, eid_ref, gcol_ref, rank_ref, cnt_ref, pos_ref, tcnt_ref, carry_ref):
    tm = x_ref.shape[1]
    d = x_ref.shape[2]
    xd = d // N_XHEADS
    first = (pl.program_id(0) == 0) & (pl.program_id(1) == 0)

    @pl.when(first)
    def _():
        carry_ref[...] = jnp.zeros_like(carry_ref)

    ya = _rms(yatt_ref[0], gatt_ref[...]).astype(BF16)
    mix = (jnp.dot(ylru_ref[0], woa_ref[...], preferred_element_type=F32)
           + jnp.dot(ya, wob_ref[...], preferred_element_type=F32))
    h1 = x_ref[0] + mix

    hn = _rms(h1, gcross_ref[...]).astype(BF16)
    qx = jnp.dot(hn, wxq_ref[...], preferred_element_type=F32)
    heads = []
    for hh in range(N_XHEADS):
        qh = (qx[:, hh * xd:(hh + 1) * xd] * (xd ** -0.5)).astype(BF16)
        s = lax.dot_general(qh, kx_ref[0, :, hh * xd:(hh + 1) * xd], NT_DIMS, preferred_element_type=F32)
        p = jnp.exp(s - jnp.max(s, axis=-1, keepdims=True))
        l = jnp.sum(p, axis=-1, keepdims=True)
        o = jnp.dot(p.astype(BF16), vx_ref[0, :, hh * xd:(hh + 1) * xd], preferred_element_type=F32) / l
        heads.append(o.astype(BF16))
    h2 = h1 + jnp.dot(jnp.concatenate(heads, axis=1), wxo_ref[...], preferred_element_type=F32)
    hn2 = _rms(h2, gffn_ref[...])
    h2_ref[...] = h2
    hn2_ref[...] = hn2

    x_hi = hn2.astype(BF16)
    x_lo = (hn2 - x_hi.astype(F32)).astype(BF16)
    by_hi = lax.dot_general(wr_ref[...], x_hi, NT_DIMS, preferred_element_type=F32)
    by_lo = lax.dot_general(wr_ref[0:ROUTER_ROWS], x_lo, NT_DIMS, preferred_element_type=F32)
    logits = by_hi[0:ROUTER_ROWS] + by_hi[ROUTER_ROWS:2 * ROUTER_ROWS] + by_lo + br_ref[...]
    gl = logits[0:N_GROUPS]
    gmax = jnp.max(gl, axis=0, keepdims=True)
    gi = lax.broadcasted_iota(jnp.int32, gl.shape, 0)
    grp = jnp.min(jnp.where(gl == gmax, gi, N_GROUPS), axis=0, keepdims=True)
    g_w = 1.0 / jnp.sum(jnp.exp(gl - gmax), axis=0, keepdims=True)
    el = jnp.zeros((EXPERTS_PER_GROUP, tm), F32)
    for g in range(N_GROUPS):
        lo = SUBLANES + g * EXPERTS_PER_GROUP
        el = jnp.where(grp == g, logits[lo:lo + EXPERTS_PER_GROUP], el)
    ee = jnp.exp(el - jnp.max(el, axis=0, keepdims=True))
    ep = ee / jnp.sum(ee, axis=0, keepdims=True)
    ei = lax.broadcasted_iota(jnp.int32, ep.shape, 0)
    p1 = jnp.max(ep, axis=0, keepdims=True)
    i1 = jnp.min(jnp.where(ep == p1, ei, EXPERTS_PER_GROUP), axis=0, keepdims=True)
    ep_rest = jnp.where(ei == i1, -1.0, ep)
    p2 = jnp.max(ep_rest, axis=0, keepdims=True)
    i2 = jnp.min(jnp.where(ep_rest == p2, ei, EXPERTS_PER_GROUP), axis=0, keepdims=True)
    den = p1 + p2
    gate1 = g_w * p1 / den
    gate2 = g_w * p2 / den
    e1 = grp * EXPERTS_PER_GROUP + i1
    e2 = grp * EXPERTS_PER_GROUP + i2
    eid_ref[0:1, :] = e1
    eid_ref[1:2, :] = e2
    li = lax.broadcasted_iota(jnp.int32, (LANES, tm), 0)
    gcol_ref[...] = jnp.where(li == 0, gate1, jnp.where(li == 1, gate2, 0.0)).T

    xi = lax.broadcasted_iota(jnp.int32, (N_EXPERTS, tm), 0)
    oh1 = xi == e1
    oh2 = xi == e2
    cnt = oh1.astype(F32) + oh2.astype(F32)
    in_tile = jnp.dot(cnt.astype(BF16), tri_ref[...], preferred_element_type=F32)
    before = in_tile + carry_ref[:, 0:1]
    rank_ref[0:1, :] = jnp.sum(jnp.where(oh1, before, 0.0), axis=0, keepdims=True).astype(jnp.int32)
    rank_ref[1:2, :] = jnp.sum(jnp.where(oh2, before, 0.0), axis=0, keepdims=True).astype(jnp.int32)
    tile_cnt = jnp.broadcast_to(jnp.sum(cnt, axis=1, keepdims=True), (N_EXPERTS, LANES))
    carry_ref[...] = carry_ref[...] + tile_cnt
    cnt_ref[...] = carry_ref[...]
    tcnt_ref[...] = tile_cnt
    tile_start = jnp.dot(elow_ref[...], tile_cnt, precision=lax.Precision.HIGHEST,
                         preferred_element_type=F32)[:, 0:1]
    local = in_tile + tile_start
    pos_ref[0:1, :] = jnp.sum(jnp.where(oh1, local, 0.0), axis=0, keepdims=True).astype(jnp.int32)
    pos_ref[1:2, :] = jnp.sum(jnp.where(oh2, local, 0.0), axis=0, keepdims=True).astype(jnp.int32)


def _post(x, ylru, yatt, gatt, wo_a, wo_b, gcross, wxq, kx, vx, wxo, gffn, wr_t, br_t, tri, tm):
    b, s, d = x.shape
    c = ylru.shape[-1]
    m = kx.shape[1]
    n = b * s
    nt = s // tm
    full = lambda shape: pl.BlockSpec(shape, lambda bi, si: (0,) * len(shape))
    tok = lambda rows: pl.BlockSpec((rows, tm), lambda bi, si: (0, bi * nt + si))
    return pl.pallas_call(
        _post_kernel,
        grid=(b, nt),
        in_specs=[pl.BlockSpec((1, tm, d), lambda bi, si: (bi, si, 0)),
                  pl.BlockSpec((1, tm, c), lambda bi, si: (bi, si, 0)),
                  pl.BlockSpec((1, tm, c), lambda bi, si: (bi, si, 0)),
                  full((1, c)), full((c, d)), full((c, d)), full((1, d)), full((d, d)),
                  pl.BlockSpec((1, m, d), lambda bi, si: (bi, 0, 0)),
                  pl.BlockSpec((1, m, d), lambda bi, si: (bi, 0, 0)),
                  full((d, d)), full((1, d)), full(wr_t.shape), full(br_t.shape), full((tm, tm)),
                  full((N_EXPERTS, N_EXPERTS))],
        out_specs=[pl.BlockSpec((tm, d), lambda bi, si: (bi * nt + si, 0)),
                   pl.BlockSpec((tm, d), lambda bi, si: (bi * nt + si, 0)),
                   tok(EXPERT_TOPK),
                   pl.BlockSpec((tm, LANES), lambda bi, si: (bi * nt + si, 0)),
                   tok(EXPERT_TOPK), full((N_EXPERTS, LANES)),
                   tok(EXPERT_TOPK),
                   pl.BlockSpec((N_EXPERTS, LANES), lambda bi, si: (bi * nt + si, 0))],
        out_shape=[jax.ShapeDtypeStruct((n, d), F32),
                   jax.ShapeDtypeStruct((n, d), F32),
                   jax.ShapeDtypeStruct((EXPERT_TOPK, n), jnp.int32),
                   jax.ShapeDtypeStruct((n, LANES), F32),
                   jax.ShapeDtypeStruct((EXPERT_TOPK, n), jnp.int32),
                   jax.ShapeDtypeStruct((N_EXPERTS, LANES), F32),
                   jax.ShapeDtypeStruct((EXPERT_TOPK, n), jnp.int32),
                   jax.ShapeDtypeStruct((n // tm * N_EXPERTS, LANES), F32)],
        scratch_shapes=[pltpu.VMEM((N_EXPERTS, LANES), F32)],
        compiler_params=_params(("arbitrary", "arbitrary")),
        name="post",
    )(x, ylru, yatt, gatt, wo_a, wo_b, gcross, wxq, kx, vx, wxo, gffn, wr_t, br_t, tri,
      jnp.asarray(np.tril(np.ones((N_EXPERTS, N_EXPERTS), np.float32), -1)))


def _row_copy(src_ref, src_row, dst_ref, dst_row, sem):
    return pltpu.make_async_copy(src_ref.at[pl.ds(src_row, 1)], dst_ref.at[pl.ds(dst_row, 1)], sem)


def _dest_kernel(ps_ref, eid_ref, rank_ref, dest_ref):
    eid = eid_ref[...]
    dest = rank_ref[...]
    for e in range(N_EXPERTS):
        dest = dest + jnp.where(eid == e, ps_ref[e], 0)
    dest_ref[...] = dest


def _dest(pad_starts, eid, rank):
    shape = (eid.size // LANES, LANES)
    spec = pl.BlockSpec(shape, lambda i, ps: (0, 0))
    return pl.pallas_call(
        _dest_kernel,
        grid_spec=pltpu.PrefetchScalarGridSpec(num_scalar_prefetch=1, grid=(1,), in_specs=[spec, spec],
                                               out_specs=spec),
        out_shape=jax.ShapeDtypeStruct(shape, jnp.int32),
        compiler_params=_params(("arbitrary",)),
        name="dest",
    )(pad_starts, eid.reshape(shape), rank.reshape(shape)).reshape(-1)


def _pieces(largest):
    piece = largest
    while piece:
        yield piece
        piece //= 2


def _tile_rows(row, count):
    return pl.ds(pl.multiple_of(row * SUBLANES, SUBLANES), count * SUBLANES)


def _dispatch_kernel(dest_ref, fill_ref, rstart_ref, rlen_ref, hn_ref, pos_ref, xout_ref, inv_ref,
                     sorted_ref, zero_ref, sem, zsem):
    tm = hn_ref.shape[0]
    n = dest_ref.shape[0] // EXPERT_TOPK
    n_sorted = EXPERT_TOPK * tm
    step = pl.program_id(0)
    base = step * tm

    def dump_rows(first, end):
        def group(g, carry):
            for i in range(SUBLANES):
                row = g * SUBLANES + i
                inv_ref[row] = EXPERT_TOPK * n + (row & (EXPERT_ROWS - 1))
            return carry

        lax.fori_loop(first // SUBLANES, end // SUBLANES, group, 0)

    zero_rows = zero_ref.shape[0] // SUBLANES
    n_rows = xout_ref.shape[0] // SUBLANES

    def pad_copies(e, start):
        first = fill_ref[e]
        length = fill_ref[N_EXPERTS + e]
        if start:
            dump_rows(first, first + length)
        for piece in _pieces(zero_rows):
            offset = first + (length & ~(2 * piece - 1))

            @pl.when((length & piece) != 0)
            def _():
                cp = pltpu.make_async_copy(zero_ref.at[_tile_rows(0, piece)], xout_ref.at[_tile_rows(offset, piece)],
                                           zsem)
                cp.start() if start else cp.wait()

    def tail_copies(start):
        def one(i, carry):
            cp = pltpu.make_async_copy(zero_ref, xout_ref.at[_tile_rows(i * zero_rows, zero_rows)], zsem)
            cp.start() if start else cp.wait()
            return carry

        lax.fori_loop(fill_ref[2 * N_EXPERTS] // zero_rows, n_rows // zero_rows, one, 0)
        if start:
            dump_rows(fill_ref[2 * N_EXPERTS], n_rows)

    @pl.when(step == 0)
    def _():
        zero_ref[...] = jnp.zeros_like(zero_ref)
        lax.fori_loop(0, N_EXPERTS, lambda e, c: (pad_copies(e, True), c)[1], 0)
        tail_copies(True)

    pos = pos_ref[...]
    p_iota = lax.broadcasted_iota(jnp.int32, (n_sorted, tm), 0)
    picks = jnp.where((p_iota == pos[0:1, :]) | (p_iota == pos[1:2, :]), 1.0, 0.0).astype(BF16)
    for slot in range(EXPERT_TOPK):
        for r in range(tm):
            src = slot * n + base + r
            inv_ref[dest_ref[src]] = src
    xs = jnp.dot(picks, hn_ref[...].astype(BF16), preferred_element_type=F32)
    for k in range(SUBLANES):
        sorted_ref[pl.ds(k, n_sorted, stride=SUBLANES), :] = xs[:, k * LANES:(k + 1) * LANES]

    def run(e, offset):
        length = rlen_ref[step * N_EXPERTS + e]
        first = rstart_ref[step * N_EXPERTS + e]
        for piece in _pieces(tm):
            done = length & ~(2 * piece - 1)

            @pl.when((length & piece) != 0)
            def _():
                pltpu.make_async_copy(sorted_ref.at[_tile_rows(offset + done, piece)],
                                      xout_ref.at[_tile_rows(first + done, piece)], sem).start()
        return offset + length

    lax.fori_loop(0, N_EXPERTS, run, 0)
    pltpu.make_async_copy(sorted_ref, xout_ref.at[_tile_rows(0, n_sorted)], sem).wait()

    @pl.when(step == 0)
    def _():
        lax.fori_loop(0, N_EXPERTS, lambda e, c: (pad_copies(e, False), c)[1], 0)
        tail_copies(False)


def _dispatch(dest_flat, fill, run_start, run_len, hn2, pos, n_rows, tm):
    n, d = hn2.shape
    assert d == SUBLANES * LANES, "a row must be exactly one (8, 128) tile for the tile-contiguous layout"
    return pl.pallas_call(
        _dispatch_kernel,
        grid_spec=pltpu.PrefetchScalarGridSpec(
            num_scalar_prefetch=4,
            grid=(n // tm,),
            in_specs=[pl.BlockSpec((tm, d), lambda i, *_: (i, 0)),
                      pl.BlockSpec((EXPERT_TOPK, tm), lambda i, *_: (0, i))],
            out_specs=[pl.BlockSpec(memory_space=pl.ANY), pl.BlockSpec(memory_space=pltpu.SMEM)],
            scratch_shapes=[pltpu.VMEM((EXPERT_TOPK * tm * SUBLANES, LANES), F32),
                            pltpu.VMEM((EXPERT_ROWS // 2 * SUBLANES, LANES), F32),
                            pltpu.SemaphoreType.DMA(()), pltpu.SemaphoreType.DMA(())]),
        out_shape=[jax.ShapeDtypeStruct((n_rows * SUBLANES, LANES), F32),
                   jax.ShapeDtypeStruct((n_rows,), jnp.int32)],
        compiler_params=_params(("arbitrary",)),
        name="dispatch",
    )(dest_flat, fill, run_start, run_len, hn2, pos)


def _experts_kernel(be_ref, nu_ref, inv_ref, x_ref, wgu_ref, wd_ref, ytok_ref, wgu_bf_ref, wd_bf_ref, ybuf_ref, sems):
    j = pl.program_id(0)
    rows = x_ref.shape[0] // SUBLANES
    f = wd_ref.shape[1]
    n_used = nu_ref[0]
    par = j % 2
    last_blk = be_ref.shape[0] - 1

    pieces = SUBLANES

    def scatter_copy(blk_par, r, dst_row):
        dst = pl.multiple_of(dst_row * pieces, pieces)
        return pltpu.make_async_copy(ybuf_ref.at[blk_par, pl.ds(r * pieces, pieces)],
                                     ytok_ref.at[pl.ds(dst, pieces)], sems.at[blk_par])

    chunk = rows // EXPERT_CHUNKS

    def start_scatter_of_previous(part):
        base = (j - 1) * rows
        for r in range(part * chunk, (part + 1) * chunk):
            scatter_copy(1 - par, r, inv_ref[base + r]).start()

    def wait_scatter(blk_par):
        pltpu.make_async_copy(ybuf_ref.at[blk_par], ytok_ref.at[pl.ds(0, rows * pieces)], sems.at[blk_par]).wait()

    def mlp(part):
        x = jnp.concatenate([x_ref[pl.ds(part * chunk * pieces + k, chunk, stride=pieces), :] for k in range(pieces)],
                            axis=1)
        gu = jnp.dot(x.astype(BF16), wgu_bf_ref[...], preferred_element_type=F32)
        act = jax.nn.silu(gu[:, 0:f]) * gu[:, f:2 * f]
        y = jnp.dot(act.astype(BF16), wd_bf_ref[...], preferred_element_type=F32)
        for k in range(pieces):
            ybuf_ref[par, pl.ds(part * chunk * pieces + k, chunk, stride=pieces), :] = y[:, k * LANES:(k + 1) * LANES]

    jc = jnp.minimum(j, last_blk)
    @pl.when((j < n_used) & ((j == 0) | (be_ref[jc] != be_ref[jnp.maximum(jc, 1) - 1])))
    def _():
        wgu_bf_ref[...] = wgu_ref[0].astype(BF16)
        wd_bf_ref[...] = wd_ref[0].astype(BF16)

    @pl.when((j >= 2) & (j <= n_used))
    def _():
        wait_scatter(par)

    @pl.when(j == 0)
    def _():
        ybuf_ref[1] = jnp.zeros(ybuf_ref.shape[1:], F32)
        cp = pltpu.make_async_copy(ybuf_ref.at[1], ytok_ref.at[pl.ds(ytok_ref.shape[0] - rows * pieces, rows * pieces)],
                                   sems.at[1])
        cp.start()
        cp.wait()

    @pl.when((j == 0) & (j < n_used))
    def _():
        for part in range(EXPERT_CHUNKS):
            mlp(part)

    @pl.when((j >= 1) & (j < n_used))
    def _():
        for part in range(EXPERT_CHUNKS):
            start_scatter_of_previous(part)
            mlp(part)

    @pl.when((j >= 1) & (j == n_used))
    def _():
        for part in range(EXPERT_CHUNKS):
            start_scatter_of_previous(part)
        wait_scatter(1 - par)


def _experts(blk_expert, n_used, inv, x_buf, w_gate_up, w_down, n_out_rows):
    d = w_gate_up.shape[1]
    f = w_down.shape[1]
    r = EXPERT_ROWS
    n_blk = x_buf.shape[0] // (r * SUBLANES)
    assert d == SUBLANES * LANES and x_buf.shape[1] == LANES, "rows are (8, 128) tiles (tile-contiguous layout)"
    x_map = lambda j, be, nu, inv: (jnp.minimum(j, jnp.maximum(nu[0], 1) - 1), 0)
    w_map = lambda j, be, nu, inv: (be[jnp.minimum(j, n_blk - 1)], 0, 0)
    return pl.pallas_call(
        _experts_kernel,
        grid_spec=pltpu.PrefetchScalarGridSpec(
            num_scalar_prefetch=3,
            grid=(n_blk + 1,),
            in_specs=[pl.BlockSpec((r * SUBLANES, LANES), x_map),
                      pl.BlockSpec((1, d, 2 * f), w_map),
                      pl.BlockSpec((1, f, d), w_map)],
            out_specs=pl.BlockSpec(memory_space=pl.ANY),
            scratch_shapes=[pltpu.VMEM((d, 2 * f), BF16), pltpu.VMEM((f, d), BF16),
                            pltpu.VMEM((2, r * SUBLANES, LANES), F32), pltpu.SemaphoreType.DMA((2,))]),
        out_shape=jax.ShapeDtypeStruct((n_out_rows * SUBLANES, LANES), F32),
        compiler_params=_params(("arbitrary",)),
        name="experts",
    )(blk_expert, n_used, inv, x_buf, w_gate_up, w_down)


def _combine_kernel(h2_ref, gcol_ref, gfin_ref, y0_ref, y1_ref, o_ref):
    tc = h2_ref.shape[0]

    def rows_of(y_ref):
        return jnp.concatenate([y_ref[pl.ds(k, tc, stride=SUBLANES), :] for k in range(SUBLANES)], axis=1)

    g = gcol_ref[...]
    moe = g[:, 0:1] * rows_of(y0_ref) + g[:, 1:2] * rows_of(y1_ref)
    o_ref[...] = _rms(h2_ref[...] + moe, gfin_ref[...])


def _combine(h2, gcol, gfin, y_tok, tc):
    n, d = h2.shape
    nt = n // tc
    return pl.pallas_call(
        _combine_kernel,
        grid=(nt,),
        in_specs=[pl.BlockSpec((tc, d), lambda i: (i, 0)),
                  pl.BlockSpec((tc, LANES), lambda i: (i, 0)),
                  pl.BlockSpec((1, d), lambda i: (0, 0)),
                  pl.BlockSpec((tc * SUBLANES, LANES), lambda i: (i, 0)),
                  pl.BlockSpec((tc * SUBLANES, LANES), lambda i: (i + nt, 0))],
        out_specs=pl.BlockSpec((tc, d), lambda i: (i, 0)),
        out_shape=jax.ShapeDtypeStruct((n, d), F32),
        compiler_params=_params(("arbitrary",)),
        name="combine",
    )(h2, gcol, gfin, y_tok, y_tok)


def _rope_tables(s):
    half = ROT_DIM // 2
    inv_freq = (ROPE_THETA ** (-np.arange(0, ROT_DIM, 2, dtype=np.float32) / ROT_DIM)).astype(np.float32)
    ang = np.arange(s, dtype=np.float32)[:, None] * inv_freq[None, :]
    cos, sin = np.cos(ang), np.sin(ang)
    zeros = lambda w: np.zeros((s, w), np.float32)
    cos_h = np.concatenate([cos, cos, np.ones((s, HEAD_DIM - ROT_DIM), np.float32)], axis=1)
    sina_h = np.concatenate([-sin, zeros(HEAD_DIM - half)], axis=1)
    sinb_h = np.concatenate([zeros(half), sin, zeros(HEAD_DIM - ROT_DIM)], axis=1)
    tile = lambda t: jnp.asarray(np.tile(t, (1, LANES // HEAD_DIM)), F32)
    return tile(cos_h), tile(sina_h), tile(sinb_h)


def _block_diag(w):
    nblk, bi, bo = w.shape
    eye = jnp.eye(nblk, dtype=w.dtype)
    return jnp.einsum('hij,hg->higj', w, eye).reshape(nblk * bi, nblk * bo)


def _layer(h, mem, norm_mix, w_in, conv_w, conv_b, w_rg, b_rg, w_ig, b_ig, lru_lambda, norm_lru_out,
           norm_attn_out, w_out, norm_cross, norm_mem, w_xq, w_xkv, w_xo, norm_ffn, w_router_group,
           b_router_group, w_router_expert, b_router_expert, w_gate_up, w_down, norm_out):
    b, s, d = h.shape
    c = conv_w.shape[-1]
    n = b * s
    row = lambda t: t.reshape(1, -1)
    tm = 512 if s % 512 == 0 else MOBA_BLOCK
    ts = tm
    assert s % ts == 0 and s % tm == 0 and c == N_HEADS * HEAD_DIM

    kx, vx = _memkv(mem, row(norm_mem), w_xkv.astype(BF16))

    w_gates = jnp.concatenate([_block_diag(w_rg), _block_diag(w_ig)], axis=1).astype(BF16)
    b_gates = jnp.concatenate([b_rg, b_ig]).reshape(1, -1)
    cos_t, sina_t, sinb_t = _rope_tables(s)
    ylru, q, k, v = _mixer_in(h, row(norm_mix), w_in.astype(BF16), conv_w, row(conv_b), w_gates, b_gates,
                              row(lru_lambda), row(norm_lru_out), cos_t, sina_t, sinb_t, ts)
    yatt = _moba(q, k, v)

    w_out_b = w_out.astype(BF16)
    wr_t = jnp.zeros((ROUTER_ROWS, d), F32)
    wr_t = wr_t.at[0:N_GROUPS].set(w_router_group.T).at[SUBLANES:SUBLANES + N_EXPERTS].set(w_router_expert.T)
    wr_hi = wr_t.astype(BF16)
    wr_t = jnp.concatenate([wr_hi, (wr_t - wr_hi.astype(F32)).astype(BF16)], axis=0)
    br_t = jnp.zeros((ROUTER_ROWS, 1), F32)
    br_t = br_t.at[0:N_GROUPS, 0].set(b_router_group).at[SUBLANES:SUBLANES + N_EXPERTS, 0].set(b_router_expert)
    tri = jnp.asarray(np.triu(np.ones((tm, tm), np.float32), 1), BF16)
    h2, hn2, eid, gcol, rank, counts, pos, tile_cnt = _post(
        h, ylru, yatt, row(norm_attn_out), w_out_b[:c], w_out_b[c:], row(norm_cross), w_xq.astype(BF16),
        kx, vx, w_xo.astype(BF16), row(norm_ffn), wr_t, br_t, tri, tm)

    counts = counts[:, 0].astype(jnp.int32)
    padded = (counts + EXPERT_ROWS - 1) // EXPERT_ROWS * EXPERT_ROWS
    pad_ends = jnp.cumsum(padded)
    pad_starts = pad_ends - padded
    dest_flat = _dest(pad_starts.astype(jnp.int32), eid, rank)
    n_blocks = n * EXPERT_TOPK // EXPERT_ROWS + N_EXPERTS
    blk_first = jnp.arange(n_blocks, dtype=jnp.int32) * EXPERT_ROWS
    blk_expert = jnp.minimum(jnp.sum(blk_first[:, None] >= pad_ends[None, :], axis=1), N_EXPERTS - 1).astype(jnp.int32)
    n_used = (pad_ends[-1:] // EXPERT_ROWS).astype(jnp.int32)
    fill = jnp.concatenate([pad_starts + counts, padded - counts, pad_ends[-1:]]).astype(jnp.int32)

    tile_cnt = tile_cnt[:, 0].astype(jnp.int32).reshape(n // tm, N_EXPERTS)
    run_start = (pad_starts[None, :] + jnp.cumsum(tile_cnt, axis=0) - tile_cnt).astype(jnp.int32).reshape(-1)
    x_buf, inv = _dispatch(dest_flat, fill, run_start, tile_cnt.reshape(-1), hn2, pos, n_blocks * EXPERT_ROWS, tm)
    y_tok = _experts(blk_expert, n_used, inv, x_buf, w_gate_up, w_down, EXPERT_TOPK * n + EXPERT_ROWS)
    out = _combine(h2, gcol, row(norm_out), y_tok, tm)
    return out.reshape(b, s, d)


def kernel(x, mem, norm_mix, w_in, conv_w, conv_b, w_rg, b_rg, w_ig, b_ig, lru_lambda, norm_lru_out, norm_attn_out,
           w_out, norm_cross, norm_mem, w_xq, w_xkv, w_xo, norm_ffn, w_router_group, b_router_group,
           w_router_expert, b_router_expert, w_gate_up, w_down, norm_final):
    depth = norm_mix.shape[0]
    assert depth == 1, "the fused final norm assumes a single layer"
    l = 0
    return _layer(x, mem, norm_mix[l], w_in[l], conv_w[l], conv_b[l], w_rg[l], b_rg[l], w_ig[l], b_ig[l],
                  lru_lambda[l], norm_lru_out[l], norm_attn_out[l], w_out[l], norm_cross[l], norm_mem[l],
                  w_xq[l], w_xkv[l], w_xo[l], norm_ffn[l], w_router_group[l], b_router_group[l],
                  w_router_expert[l], b_router_expert[l], w_gate_up[l], w_down[l], norm_final)
```

```python
import jax
import jax.numpy as jnp
import numpy as np
from jax import lax
from jax.experimental import pallas as pl
from jax.experimental.pallas import tpu as pltpu

F32 = jnp.float32
BF16 = jnp.bfloat16

CONV_WIDTH = 4
RG_C = 8.0
N_HEADS = 8
HEAD_DIM = 64
ROT_DIM = HEAD_DIM // 4
ROPE_THETA = 500000.0
MOBA_BLOCK = 256
MOBA_TOPK = 3
N_XHEADS = 4
N_GROUPS = 4
EXPERTS_PER_GROUP = 8
N_EXPERTS = N_GROUPS * EXPERTS_PER_GROUP
EXPERT_TOPK = 2
EPS = 1e-6

LANES = 128
SUBLANES = 8
NEG_BIG = -1e30
LOG2_E = 1.4426950408889634
MOBA_LOOKAHEAD = 3
EXPERT_ROWS = 512
EXPERT_CHUNKS = 4
MIXER_SUB_ROWS = 256
BF16_ROWS = 2 * SUBLANES
ROUTER_ROWS = -(-(SUBLANES + N_EXPERTS) // BF16_ROWS) * BF16_ROWS
VMEM_LIMIT = 56 * 1024 * 1024

NT_DIMS = (((1,), (1,)), ((), ()))


def _rms(x, g):
    return x * lax.rsqrt(jnp.mean(x * x, axis=-1, keepdims=True) + EPS) * g


def _params(sem):
    return pltpu.CompilerParams(dimension_semantics=sem, vmem_limit_bytes=VMEM_LIMIT)


def _memkv_kernel(mem_ref, g_ref, w_ref, k_ref, v_ref):
    d = mem_ref.shape[-1]
    mn = _rms(mem_ref[0], g_ref[...]).astype(BF16)
    kv = jnp.dot(mn, w_ref[...], preferred_element_type=F32)
    k_ref[0] = kv[:, :d].astype(BF16)
    v_ref[0] = kv[:, d:].astype(BF16)


def _memkv(mem, g, w_xkv):
    b, m, d = mem.shape
    return pl.pallas_call(
        _memkv_kernel,
        grid=(b,),
        in_specs=[pl.BlockSpec((1, m, d), lambda i: (i, 0, 0)),
                  pl.BlockSpec((1, d), lambda i: (0, 0)),
                  pl.BlockSpec((d, 2 * d), lambda i: (0, 0))],
        out_specs=[pl.BlockSpec((1, m, d), lambda i: (i, 0, 0)),
                   pl.BlockSpec((1, m, d), lambda i: (i, 0, 0))],
        out_shape=[jax.ShapeDtypeStruct((b, m, d), BF16)] * 2,
        compiler_params=_params(("arbitrary",)),
        name="memkv",
    )(mem, g, w_xkv)


def _mixer_in_kernel(x_ref, gmix_ref, win_ref, convw_ref, convb_ref, wg_ref, bg_ref, lam_ref, glru_ref,
                     cos_ref, sina_ref, sinb_ref,
                     ylru_ref, q_ref, k_ref, v_ref, xpad_ref, hcar_ref, ascan_ref, uscan_ref):
    c = ylru_ref.shape[-1]
    si = pl.program_id(1)

    @pl.when(si == 0)
    def _():
        xpad_ref[0:SUBLANES, :] = jnp.zeros((SUBLANES, c), F32)
        hcar_ref[...] = jnp.zeros_like(hcar_ref)

    n_sub = x_ref.shape[1] // ascan_ref.shape[1]
    ts = ascan_ref.shape[1]
    projs = []
    for sub in range(n_sub):
        xn = _rms(x_ref[0, sub * ts:(sub + 1) * ts, :], gmix_ref[...]).astype(BF16)
        projs.append(jnp.dot(xn, win_ref[...], preferred_element_type=F32))
    for sub in range(n_sub):
        _mixer_stage2(projs[sub], slice(sub * ts, (sub + 1) * ts), convw_ref, convb_ref, wg_ref, bg_ref, lam_ref,
                      glru_ref, cos_ref, sina_ref, sinb_ref, ylru_ref, q_ref, k_ref, v_ref, xpad_ref, hcar_ref,
                      ascan_ref, uscan_ref)


def _mixer_stage2(proj, rows_out, convw_ref, convb_ref, wg_ref, bg_ref, lam_ref, glru_ref, cos_ref, sina_ref,
                  sinb_ref, ylru_ref, q_ref, k_ref, v_ref, xpad_ref, hcar_ref, ascan_ref, uscan_ref):
    ts = proj.shape[0]
    c = ylru_ref.shape[-1]
    x_lru = proj[:, 0:c]
    g_lru = proj[:, c:2 * c]
    q = proj[:, 2 * c:3 * c]
    k = proj[:, 3 * c:4 * c]
    v = proj[:, 4 * c:5 * c]

    xpad_ref[SUBLANES:SUBLANES + ts, :] = x_lru
    cw = convw_ref[...]
    xc = convb_ref[...] + cw[3:4] * x_lru
    for j in range(1, CONV_WIDTH):
        xc = xc + cw[3 - j:4 - j] * xpad_ref[SUBLANES - j:SUBLANES - j + ts, :]
    xpad_ref[0:SUBLANES, :] = x_lru[ts - SUBLANES:ts, :]

    gates = jnp.dot(xc.astype(BF16), wg_ref[...], preferred_element_type=F32) + bg_ref[...]
    r = jax.nn.sigmoid(gates[:, 0:c])
    i = jax.nn.sigmoid(gates[:, c:2 * c])
    neg_lam = -lam_ref[...]
    softplus = jnp.maximum(neg_lam, 0.0) + jnp.log1p(jnp.exp(-jnp.abs(neg_lam)))
    log_a = -RG_C * r * softplus
    a = jnp.exp(log_a)
    u = jnp.sqrt(-jnp.tanh(log_a) * (a * a + 1.0)) * (i * xc)

    groups = ts // SUBLANES
    cols = c // LANES
    for j in range(cols):
        ascan_ref[j] = a[:, j * LANES:(j + 1) * LANES]
        uscan_ref[j] = u[:, j * LANES:(j + 1) * LANES]
    for r in range(SUBLANES):
        rows = pl.ds(r, groups, stride=SUBLANES)
        a_r = jnp.concatenate([ascan_ref[j, rows, :] for j in range(cols)], axis=1)
        u_r = jnp.concatenate([uscan_ref[j, rows, :] for j in range(cols)], axis=1)
        if r:
            u_r = a_r * u_acc + u_r
            a_r = a_r * a_acc
            for j in range(cols):
                ascan_ref[j, rows, :] = a_r[:, j * LANES:(j + 1) * LANES]
                uscan_ref[j, rows, :] = u_r[:, j * LANES:(j + 1) * LANES]
        a_acc, u_acc = a_r, u_r
    carry = hcar_ref[...]
    hs = []
    for gi in range(groups):
        blk = slice(gi * SUBLANES, (gi + 1) * SUBLANES)
        a_g = jnp.concatenate([ascan_ref[j, blk, :] for j in range(cols)], axis=1)
        u_g = jnp.concatenate([uscan_ref[j, blk, :] for j in range(cols)], axis=1)
        hs.append(a_g * carry + u_g)
        carry = a_acc[gi:gi + 1] * carry + u_acc[gi:gi + 1]
    h = jnp.concatenate(hs, axis=0)
    hcar_ref[...] = carry

    y = h * jax.nn.gelu(g_lru)
    ylru_ref[0, rows_out, :] = _rms(y, glru_ref[...]).astype(BF16)

    wide = lambda t_ref: jnp.concatenate([t_ref[rows_out, :]] * (c // LANES), axis=1)
    cos, sina, sinb = wide(cos_ref), wide(sina_ref), wide(sinb_ref)
    half = ROT_DIM // 2

    def rope(t):
        return t * cos + pltpu.roll(t, c - half, axis=1) * sina + pltpu.roll(t, half, axis=1) * sinb

    q_ref[0, rows_out, :] = rope(q)
    k_ref[0, rows_out, :] = rope(k)
    v_ref[0, rows_out, :] = v.astype(BF16)


def _mixer_in(x, gmix, w_in, conv_w, conv_b, w_gates, b_gates, lam, glru, cos_t, sina_t, sinb_t, ts):
    b, s, d = x.shape
    c = conv_w.shape[-1]
    sub = min(ts, MIXER_SUB_ROWS)
    full = lambda shape: pl.BlockSpec(shape, lambda bi, si: (0,) * len(shape))
    tab = pl.BlockSpec((ts, LANES), lambda bi, si: (si, 0))
    seq = lambda: pl.BlockSpec((1, ts, c), lambda bi, si: (bi, si, 0))
    return pl.pallas_call(
        _mixer_in_kernel,
        grid=(b, s // ts),
        in_specs=[pl.BlockSpec((1, ts, d), lambda bi, si: (bi, si, 0)),
                  full((1, d)), full(w_in.shape), full(conv_w.shape), full((1, c)),
                  full(w_gates.shape), full((1, 2 * c)), full((1, c)), full((1, c)),
                  tab, tab, tab],
        out_specs=[seq(), seq(), seq(), seq()],
        out_shape=[jax.ShapeDtypeStruct((b, s, c), BF16),
                   jax.ShapeDtypeStruct((b, s, c), F32),
                   jax.ShapeDtypeStruct((b, s, c), F32),
                   jax.ShapeDtypeStruct((b, s, c), BF16)],
        scratch_shapes=[pltpu.VMEM((sub + SUBLANES, c), F32), pltpu.VMEM((1, c), F32),
                        pltpu.VMEM((c // LANES, sub, LANES), F32), pltpu.VMEM((c // LANES, sub, LANES), F32)],
        compiler_params=_params(("arbitrary", "arbitrary")),
        name="mixer_in",
    )(x, gmix, w_in, conv_w, conv_b, w_gates, b_gates, lam, glru, cos_t, sina_t, sinb_t)


def _moba_kernel(q_ref, k_ref, v_ref, o_ref, qa_ref, ka_ref):
    s_len = q_ref.shape[1]
    nb = s_len // MOBA_BLOCK
    n_pick = min(MOBA_TOPK, nb)
    q = q_ref[0]
    k = k_ref[0]
    lane = lax.broadcasted_iota(jnp.int32, (1, LANES), 1)
    kmean = jnp.concatenate(
        [jnp.sum(k[n * MOBA_BLOCK:(n + 1) * MOBA_BLOCK], axis=0, keepdims=True) for n in range(nb)],
        axis=0) * (1.0 / MOBA_BLOCK)

    blk_row = lax.broadcasted_iota(jnp.int32, (nb, s_len), 0)
    q_blk = lax.broadcasted_iota(jnp.int32, (nb, s_len), 1) // MOBA_BLOCK
    past = blk_row < q_blk
    key_blk = lax.broadcasted_iota(jnp.int32, (s_len, LANES), 0) // MOBA_BLOCK
    lane_full = lax.broadcasted_iota(jnp.int32, (s_len, LANES), 1)

    for h in range(2):
        own = (lane >= h * HEAD_DIM) & (lane < (h + 1) * HEAD_DIM)
        off = (1 - h) * HEAD_DIM
        gate = lax.dot_general(jnp.where(own, kmean, 0.0), q, NT_DIMS,
                               precision=lax.Precision.HIGHEST, preferred_element_type=F32)
        gate = jnp.where(past, gate, -jnp.inf)
        rank = jnp.zeros((nb, s_len), jnp.int32)
        for m in range(nb):
            gm = gate[m:m + 1, :]
            ahead = (gm > gate) | ((gm == gate) & (m < blk_row))
            rank = rank + ahead.astype(jnp.int32)
        allowed = ((rank < n_pick) & past) | (blk_row == q_blk)
        bias = jnp.where(allowed, 0.0, NEG_BIG)
        pieces = []
        if off:
            pieces.append(jnp.zeros((off, s_len), F32))
        pieces.append(bias)
        if LANES - off - nb:
            pieces.append(jnp.zeros((LANES - off - nb, s_len), F32))
        bias_lanes = jnp.concatenate(pieces, axis=0).T
        qa_ref[h] = jnp.where(own, q * (HEAD_DIM ** -0.5 * LOG2_E), bias_lanes).astype(BF16)
        onehot = (lane_full - off == key_blk).astype(F32)
        ka_ref[h] = jnp.where(own, k, onehot).astype(BF16)

    own0 = lane < HEAD_DIM

    def scores(qi, h):
        nk = (qi + 1) * MOBA_BLOCK
        qa = qa_ref[h, qi * MOBA_BLOCK:(qi + 1) * MOBA_BLOCK, :]
        return lax.dot_general(qa, ka_ref[h, 0:nk, :], NT_DIMS, preferred_element_type=F32)

    key_in_blk = lax.broadcasted_iota(jnp.int32, (MOBA_BLOCK, MOBA_BLOCK), 1)
    q_in_blk = lax.broadcasted_iota(jnp.int32, (MOBA_BLOCK, MOBA_BLOCK), 0)
    causal = key_in_blk <= q_in_blk
    units = [(qi, h) for qi in range(nb) for h in range(2)]
    ahead = [scores(*u) for u in units[:MOBA_LOOKAHEAD]]
    outs = []

    def weighted_values(p, l, qi, h):
        outs.append(jnp.dot(p, v_ref[0, 0:(qi + 1) * MOBA_BLOCK, :], preferred_element_type=F32) / l)
        if h:
            o_ref[0, qi * MOBA_BLOCK:(qi + 1) * MOBA_BLOCK, :] = jnp.where(own0, outs[0], outs[1])
            outs.clear()

    pending = None
    for n, (qi, h) in enumerate(units):
        s = ahead.pop(0)
        if n + MOBA_LOOKAHEAD < len(units):
            ahead.append(scores(*units[n + MOBA_LOOKAHEAD]))
        n_past = qi * MOBA_BLOCK
        s_own = jnp.where(causal, s[:, n_past:], NEG_BIG)
        s = jnp.concatenate([s[:, :n_past], s_own], axis=1) if qi else s_own
        p = jnp.exp2(s - jnp.max(s, axis=-1, keepdims=True))
        l = jnp.sum(p, axis=-1, keepdims=True)
        if pending is not None:
            weighted_values(*pending)
        pending = (p.astype(BF16), l, qi, h)
    weighted_values(*pending)


def _moba(q, k, v):
    b, s, c = q.shape
    spec = lambda: pl.BlockSpec((1, s, LANES), lambda bi, hi: (bi, 0, hi))
    return pl.pallas_call(
        _moba_kernel,
        grid=(b, c // LANES),
        in_specs=[spec(), spec(), spec()],
        out_specs=spec(),
        out_shape=jax.ShapeDtypeStruct((b, s, c), F32),
        scratch_shapes=[pltpu.VMEM((2, s, LANES), BF16), pltpu.VMEM((2, s, LANES), BF16)],
        compiler_params=_params(("arbitrary", "arbitrary")),
        name="moba",
    )(q, k, v)


def _post_kernel(x_ref, ylru_ref, yatt_ref, gatt_ref, woa_ref, wob_ref, gcross_ref, wxq_ref, kx_ref, vx_ref,
                 wxo_ref, gffn_ref, wr_ref, br_ref, tri_ref, elow_ref,
                 h2_ref, hn2_ref, eid_ref, gcol_ref, rank_ref, cnt_ref, pos_ref, tcnt_ref, carry_ref):
    tm = x_ref.shape[1]
    d = x_ref.shape[2]
    xd = d // N_XHEADS
    first = (pl.program_id(0) == 0) & (pl.program_id(1) == 0)

    @pl.when(first)
    def _():
        carry_ref[...] = jnp.zeros_like(carry_ref)

    ya = _rms(yatt_ref[0], gatt_ref[...]).astype(BF16)
    mix = (jnp.dot(ylru_ref[0], woa_ref[...], preferred_element_type=F32)
           + jnp.dot(ya, wob_ref[...], preferred_element_type=F32))
    h1 = x_ref[0] + mix

    hn = _rms(h1, gcross_ref[...]).astype(BF16)
    qx = jnp.dot(hn, wxq_ref[...], preferred_element_type=F32)
    heads = []
    for hh in range(N_XHEADS):
        qh = (qx[:, hh * xd:(hh + 1) * xd] * (xd ** -0.5)).astype(BF16)
        s = lax.dot_general(qh, kx_ref[0, :, hh * xd:(hh + 1) * xd], NT_DIMS, preferred_element_type=F32)
        p = jnp.exp(s - jnp.max(s, axis=-1, keepdims=True))
        l = jnp.sum(p, axis=-1, keepdims=True)
        o = jnp.dot(p.astype(BF16), vx_ref[0, :, hh * xd:(hh + 1) * xd], preferred_element_type=F32) / l
        heads.append(o.astype(BF16))
    h2 = h1 + jnp.dot(jnp.concatenate(heads, axis=1), wxo_ref[...], preferred_element_type=F32)
    hn2 = _rms(h2, gffn_ref[...])
    h2_ref[...] = h2
    hn2_ref[...] = hn2

    x_hi = hn2.astype(BF16)
    x_lo = (hn2 - x_hi.astype(F32)).astype(BF16)
    by_hi = lax.dot_general(wr_ref[...], x_hi, NT_DIMS, preferred_element_type=F32)
    by_lo = lax.dot_general(wr_ref[0:ROUTER_ROWS], x_lo, NT_DIMS, preferred_element_type=F32)
    logits = by_hi[0:ROUTER_ROWS] + by_hi[ROUTER_ROWS:2 * ROUTER_ROWS] + by_lo + br_ref[...]
    gl = logits[0:N_GROUPS]
    gmax = jnp.max(gl, axis=0, keepdims=True)
    gi = lax.broadcasted_iota(jnp.int32, gl.shape, 0)
    grp = jnp.min(jnp.where(gl == gmax, gi, N_GROUPS), axis=0, keepdims=True)
    g_w = 1.0 / jnp.sum(jnp.exp(gl - gmax), axis=0, keepdims=True)
    el = jnp.zeros((EXPERTS_PER_GROUP, tm), F32)
    for g in range(N_GROUPS):
        lo = SUBLANES + g * EXPERTS_PER_GROUP
        el = jnp.where(grp == g, logits[lo:lo + EXPERTS_PER_GROUP], el)
    ee = jnp.exp(el - jnp.max(el, axis=0, keepdims=True))
    ep = ee / jnp.sum(ee, axis=0, keepdims=True)
    ei = lax.broadcasted_iota(jnp.int32, ep.shape, 0)
    p1 = jnp.max(ep, axis=0, keepdims=True)
    i1 = jnp.min(jnp.where(ep == p1, ei, EXPERTS_PER_GROUP), axis=0, keepdims=True)
    ep_rest = jnp.where(ei == i1, -1.0, ep)
    p2 = jnp.max(ep_rest, axis=0, keepdims=True)
    i2 = jnp.min(jnp.where(ep_rest == p2, ei, EXPERTS_PER_GROUP), axis=0, keepdims=True)
    den = p1 + p2
    gate1 = g_w * p1 / den
    gate2 = g_w * p2 / den
    e1 = grp * EXPERTS_PER_GROUP + i1
    e2 = grp * EXPERTS_PER_GROUP + i2
    eid_ref[0:1, :] = e1
    eid_ref[1:2, :] = e2
    li = lax.broadcasted_iota(jnp.int32, (LANES, tm), 0)
    gcol_ref[...] = jnp.where(li == 0, gate1, jnp.where(li == 1, gate2, 0.0)).T

    xi = lax.broadcasted_iota(jnp.int32, (N_EXPERTS, tm), 0)
    oh1 = xi == e1
    oh2 = xi == e2
    cnt = oh1.astype(F32) + oh2.astype(F32)
    in_tile = jnp.dot(cnt.astype(BF16), tri_ref[...], preferred_element_type=F32)
    before = in_tile + carry_ref[:, 0:1]
    rank_ref[0:1, :] = jnp.sum(jnp.where(oh1, before, 0.0), axis=0, keepdims=True).astype(jnp.int32)
    rank_ref[1:2, :] = jnp.sum(jnp.where(oh2, before, 0.0), axis=0, keepdims=True).astype(jnp.int32)
    tile_cnt = jnp.broadcast_to(jnp.sum(cnt, axis=1, keepdims=True), (N_EXPERTS, LANES))
    carry_ref[...] = carry_ref[...] + tile_cnt
    cnt_ref[...] = carry_ref[...]
    tcnt_ref[...] = tile_cnt
    tile_start = jnp.dot(elow_ref[...], tile_cnt, precision=lax.Precision.HIGHEST,
                         preferred_element_type=F32)[:, 0:1]
    local = in_tile + tile_start
    pos_ref[0:1, :] = jnp.sum(jnp.where(oh1, local, 0.0), axis=0, keepdims=True).astype(jnp.int32)
    pos_ref[1:2, :] = jnp.sum(jnp.where(oh2, local, 0.0), axis=0, keepdims=True).astype(jnp.int32)


def _post(x, ylru, yatt, gatt, wo_a, wo_b, gcross, wxq, kx, vx, wxo, gffn, wr_t, br_t, tri, tm):
    b, s, d = x.shape
    c = ylru.shape[-1]
    m = kx.shape[1]
    n = b * s
    nt = s // tm
    full = lambda shape: pl.BlockSpec(shape, lambda bi, si: (0,) * len(shape))
    tok = lambda rows: pl.BlockSpec((rows, tm), lambda bi, si: (0, bi * nt + si))
    return pl.pallas_call(
        _post_kernel,
        grid=(b, nt),
        in_specs=[pl.BlockSpec((1, tm, d), lambda bi, si: (bi, si, 0)),
                  pl.BlockSpec((1, tm, c), lambda bi, si: (bi, si, 0)),
                  pl.BlockSpec((1, tm, c), lambda bi, si: (bi, si, 0)),
                  full((1, c)), full((c, d)), full((c, d)), full((1, d)), full((d, d)),
                  pl.BlockSpec((1, m, d), lambda bi, si: (bi, 0, 0)),
                  pl.BlockSpec((1, m, d), lambda bi, si: (bi, 0, 0)),
                  full((d, d)), full((1, d)), full(wr_t.shape), full(br_t.shape), full((tm, tm)),
                  full((N_EXPERTS, N_EXPERTS))],
        out_specs=[pl.BlockSpec((tm, d), lambda bi, si: (bi * nt + si, 0)),
                   pl.BlockSpec((tm, d), lambda bi, si: (bi * nt + si, 0)),
                   tok(EXPERT_TOPK),
                   pl.BlockSpec((tm, LANES), lambda bi, si: (bi * nt + si, 0)),
                   tok(EXPERT_TOPK), full((N_EXPERTS, LANES)),
                   tok(EXPERT_TOPK),
                   pl.BlockSpec((N_EXPERTS, LANES), lambda bi, si: (bi * nt + si, 0))],
        out_shape=[jax.ShapeDtypeStruct((n, d), F32),
                   jax.ShapeDtypeStruct((n, d), F32),
                   jax.ShapeDtypeStruct((EXPERT_TOPK, n), jnp.int32),
                   jax.ShapeDtypeStruct((n, LANES), F32),
                   jax.ShapeDtypeStruct((EXPERT_TOPK, n), jnp.int32),
                   jax.ShapeDtypeStruct((N_EXPERTS, LANES), F32),
                   jax.ShapeDtypeStruct((EXPERT_TOPK, n), jnp.int32),
                   jax.ShapeDtypeStruct((n // tm * N_EXPERTS, LANES), F32)],
        scratch_shapes=[pltpu.VMEM((N_EXPERTS, LANES), F32)],
        compiler_params=_params(("arbitrary", "arbitrary")),
        name="post",
    )(x, ylru, yatt, gatt, wo_a, wo_b, gcross, wxq, kx, vx, wxo, gffn, wr_t, br_t, tri,
      jnp.asarray(np.tril(np.ones((N_EXPERTS, N_EXPERTS), np.float32), -1)))


def _dest_kernel(ps_ref, eid_ref, rank_ref, dest_ref):
    eid = eid_ref[...]
    dest = rank_ref[...]
    for e in range(N_EXPERTS):
        dest = dest + jnp.where(eid == e, ps_ref[e], 0)
    dest_ref[...] = dest


def _dest(pad_starts, eid, rank):
    shape = (eid.size // LANES, LANES)
    spec = pl.BlockSpec(shape, lambda i, ps: (0, 0))
    return pl.pallas_call(
        _dest_kernel,
        grid_spec=pltpu.PrefetchScalarGridSpec(num_scalar_prefetch=1, grid=(1,), in_specs=[spec, spec],
                                               out_specs=spec),
        out_shape=jax.ShapeDtypeStruct(shape, jnp.int32),
        compiler_params=_params(("arbitrary",)),
        name="dest",
    )(pad_starts, eid.reshape(shape), rank.reshape(shape)).reshape(-1)


def _pieces(largest):
    piece = largest
    while piece:
        yield piece
        piece //= 2


def _tile_rows(row, count):
    return pl.ds(pl.multiple_of(row * SUBLANES, SUBLANES), count * SUBLANES)


def _dispatch_kernel(dest_ref, fill_ref, rstart_ref, rlen_ref, hn_ref, pos_ref, xout_ref, inv_ref,
                     sorted_ref, zero_ref, sem, zsem):
    tm = hn_ref.shape[0]
    n = dest_ref.shape[0] // EXPERT_TOPK
    n_sorted = EXPERT_TOPK * tm
    step = pl.program_id(0)
    base = step * tm

    def dump_rows(first, end):
        def group(g, carry):
            for i in range(SUBLANES):
                inv_ref[g * SUBLANES + i] = 0
            return carry

        lax.fori_loop(first // SUBLANES, end // SUBLANES, group, 0)

    zero_rows = zero_ref.shape[0] // SUBLANES
    n_rows = xout_ref.shape[0] // SUBLANES

    def pad_copies(e, start):
        first = fill_ref[e]
        length = fill_ref[N_EXPERTS + e]
        if start:
            dump_rows(first, first + length)
        for piece in _pieces(zero_rows):
            offset = first + (length & ~(2 * piece - 1))

            @pl.when((length & piece) != 0)
            def _():
                cp = pltpu.make_async_copy(zero_ref.at[_tile_rows(0, piece)], xout_ref.at[_tile_rows(offset, piece)],
                                           zsem)
                cp.start() if start else cp.wait()

    def tail_copies(start):
        def one(i, carry):
            cp = pltpu.make_async_copy(zero_ref, xout_ref.at[_tile_rows(i * zero_rows, zero_rows)], zsem)
            cp.start() if start else cp.wait()
            return carry

        lax.fori_loop(fill_ref[2 * N_EXPERTS] // zero_rows, n_rows // zero_rows, one, 0)
        if start:
            dump_rows(fill_ref[2 * N_EXPERTS], n_rows)

    @pl.when(step == 0)
    def _():
        zero_ref[...] = jnp.zeros_like(zero_ref)
        lax.fori_loop(0, N_EXPERTS, lambda e, c: (pad_copies(e, True), c)[1], 0)
        tail_copies(True)

    pos = pos_ref[...]
    p_iota = lax.broadcasted_iota(jnp.int32, (n_sorted, tm), 0)
    picks = jnp.where((p_iota == pos[0:1, :]) | (p_iota == pos[1:2, :]), 1.0, 0.0).astype(BF16)
    for slot in range(EXPERT_TOPK):
        for r in range(tm):
            src = slot * n + base + r
            inv_ref[dest_ref[src]] = src
    xs = jnp.dot(picks, hn_ref[...].astype(BF16), preferred_element_type=F32)
    for k in range(SUBLANES):
        sorted_ref[pl.ds(k, n_sorted, stride=SUBLANES), :] = xs[:, k * LANES:(k + 1) * LANES]

    def run(e, offset):
        length = rlen_ref[step * N_EXPERTS + e]
        first = rstart_ref[step * N_EXPERTS + e]
        for piece in _pieces(tm):
            done = length & ~(2 * piece - 1)

            @pl.when((length & piece) != 0)
            def _():
                pltpu.make_async_copy(sorted_ref.at[_tile_rows(offset + done, piece)],
                                      xout_ref.at[_tile_rows(first + done, piece)], sem).start()
        return offset + length

    lax.fori_loop(0, N_EXPERTS, run, 0)
    pltpu.make_async_copy(sorted_ref, xout_ref.at[_tile_rows(0, n_sorted)], sem).wait()

    @pl.when(step == 0)
    def _():
        lax.fori_loop(0, N_EXPERTS, lambda e, c: (pad_copies(e, False), c)[1], 0)
        tail_copies(False)


def _dispatch(dest_flat, fill, run_start, run_len, hn2, pos, n_rows, tm):
    n, d = hn2.shape
    assert d == SUBLANES * LANES, "a row must be exactly one (8, 128) tile for the tile-contiguous layout"
    return pl.pallas_call(
        _dispatch_kernel,
        grid_spec=pltpu.PrefetchScalarGridSpec(
            num_scalar_prefetch=4,
            grid=(n // tm,),
            in_specs=[pl.BlockSpec((tm, d), lambda i, *_: (i, 0)),
                      pl.BlockSpec((EXPERT_TOPK, tm), lambda i, *_: (0, i))],
            out_specs=[pl.BlockSpec(memory_space=pl.ANY), pl.BlockSpec(memory_space=pltpu.SMEM)],
            scratch_shapes=[pltpu.VMEM((EXPERT_TOPK * tm * SUBLANES, LANES), F32),
                            pltpu.VMEM((EXPERT_ROWS // 2 * SUBLANES, LANES), F32),
                            pltpu.SemaphoreType.DMA(()), pltpu.SemaphoreType.DMA(())]),
        out_shape=[jax.ShapeDtypeStruct((n_rows * SUBLANES, LANES), F32),
                   jax.ShapeDtypeStruct((n_rows,), jnp.int32)],
        compiler_params=_params(("arbitrary",)),
        name="dispatch",
    )(dest_flat, fill, run_start, run_len, hn2, pos)


def _experts_kernel(be_ref, nu_ref, nv_ref, inv_ref, x_ref, wgu_ref, wd_ref, ytok_ref, wgu_bf_ref, wd_bf_ref,
                    ybuf_ref, sems):
    j = pl.program_id(0)
    rows = x_ref.shape[0] // SUBLANES
    f = wd_ref.shape[1]
    n_used = nu_ref[0]
    par = j % 2
    last_blk = be_ref.shape[0] - 1

    pieces = SUBLANES

    def scatter_copy(blk_par, r, dst_row):
        dst = pl.multiple_of(dst_row * pieces, pieces)
        return pltpu.make_async_copy(ybuf_ref.at[blk_par, pl.ds(r * pieces, pieces)],
                                     ytok_ref.at[pl.ds(dst, pieces)], sems.at[blk_par])

    chunk = rows // EXPERT_CHUNKS

    prev_base = (j - 1) * rows
    prev_valid = nv_ref[jnp.maximum(j, 1) - 1]

    def start_scatter_of_previous(part):
        for r in range(part * chunk, (part + 1) * chunk):
            scatter_copy(1 - par, r, inv_ref[prev_base + r]).start()

    def start_partial_scatter_of_previous():
        def one(r, carry):
            scatter_copy(1 - par, r, inv_ref[prev_base + r]).start()
            return carry

        lax.fori_loop(0, prev_valid, one, 0)

    def wait_scatter(blk_par, valid):
        @pl.when(valid == rows)
        def _():
            pltpu.make_async_copy(ybuf_ref.at[blk_par], ytok_ref.at[pl.ds(0, rows * pieces)], sems.at[blk_par]).wait()

        @pl.when(valid < rows)
        def _():
            lax.fori_loop(0, valid, lambda r, c: (scatter_copy(blk_par, 0, 0).wait(), c)[1], 0)

    def mlp(part):
        x = jnp.concatenate([x_ref[pl.ds(part * chunk * pieces + k, chunk, stride=pieces), :] for k in range(pieces)],
                            axis=1)
        gu = jnp.dot(x.astype(BF16), wgu_bf_ref[...], preferred_element_type=F32)
        act = jax.nn.silu(gu[:, 0:f]) * gu[:, f:2 * f]
        y = jnp.dot(act.astype(BF16), wd_bf_ref[...], preferred_element_type=F32)
        for k in range(pieces):
            ybuf_ref[par, pl.ds(part * chunk * pieces + k, chunk, stride=pieces), :] = y[:, k * LANES:(k + 1) * LANES]

    jc = jnp.minimum(j, last_blk)
    @pl.when((j < n_used) & ((j == 0) | (be_ref[jc] != be_ref[jnp.maximum(jc, 1) - 1])))
    def _():
        wgu_bf_ref[...] = wgu_ref[0].astype(BF16)
        wd_bf_ref[...] = wd_ref[0].astype(BF16)

    @pl.when((j >= 2) & (j <= n_used))
    def _():
        wait_scatter(par, nv_ref[jnp.maximum(j, 2) - 2])

    @pl.when((j == 0) & (j < n_used))
    def _():
        for part in range(EXPERT_CHUNKS):
            mlp(part)

    @pl.when((j >= 1) & (j < n_used) & (prev_valid == rows))
    def _():
        for part in range(EXPERT_CHUNKS):
            start_scatter_of_previous(part)
            mlp(part)

    @pl.when((j >= 1) & (j < n_used) & (prev_valid < rows))
    def _():
        start_partial_scatter_of_previous()
        for part in range(EXPERT_CHUNKS):
            mlp(part)

    @pl.when((j >= 1) & (j == n_used))
    def _():
        start_partial_scatter_of_previous()
        wait_scatter(1 - par, prev_valid)


def _experts(blk_expert, n_used, blk_valid, inv, x_buf, w_gate_up, w_down, n_out_rows):
    d = w_gate_up.shape[1]
    f = w_down.shape[1]
    r = EXPERT_ROWS
    n_blk = x_buf.shape[0] // (r * SUBLANES)
    assert d == SUBLANES * LANES and x_buf.shape[1] == LANES, "rows are (8, 128) tiles (tile-contiguous layout)"
    x_map = lambda j, be, nu, nv, inv: (jnp.minimum(j, jnp.maximum(nu[0], 1) - 1), 0)
    w_map = lambda j, be, nu, nv, inv: (be[jnp.minimum(j, n_blk - 1)], 0, 0)
    return pl.pallas_call(
        _experts_kernel,
        grid_spec=pltpu.PrefetchScalarGridSpec(
            num_scalar_prefetch=4,
            grid=(n_blk + 1,),
            in_specs=[pl.BlockSpec((r * SUBLANES, LANES), x_map),
                      pl.BlockSpec((1, d, 2 * f), w_map),
                      pl.BlockSpec((1, f, d), w_map)],
            out_specs=pl.BlockSpec(memory_space=pl.ANY),
            scratch_shapes=[pltpu.VMEM((d, 2 * f), BF16), pltpu.VMEM((f, d), BF16),
                            pltpu.VMEM((2, r * SUBLANES, LANES), F32), pltpu.SemaphoreType.DMA((2,))]),
        out_shape=jax.ShapeDtypeStruct((n_out_rows * SUBLANES, LANES), F32),
        compiler_params=_params(("arbitrary",)),
        name="experts",
    )(blk_expert, n_used, blk_valid, inv, x_buf, w_gate_up, w_down)


def _combine_kernel(h2_ref, gcol_ref, gfin_ref, y0_ref, y1_ref, o_ref):
    tc = h2_ref.shape[0]

    def rows_of(y_ref):
        return jnp.concatenate([y_ref[pl.ds(k, tc, stride=SUBLANES), :] for k in range(SUBLANES)], axis=1)

    g = gcol_ref[...]
    moe = g[:, 0:1] * rows_of(y0_ref) + g[:, 1:2] * rows_of(y1_ref)
    o_ref[...] = _rms(h2_ref[...] + moe, gfin_ref[...])


def _combine(h2, gcol, gfin, y_tok, tc):
    n, d = h2.shape
    nt = n // tc
    return pl.pallas_call(
        _combine_kernel,
        grid=(nt,),
        in_specs=[pl.BlockSpec((tc, d), lambda i: (i, 0)),
                  pl.BlockSpec((tc, LANES), lambda i: (i, 0)),
                  pl.BlockSpec((1, d), lambda i: (0, 0)),
                  pl.BlockSpec((tc * SUBLANES, LANES), lambda i: (i, 0)),
                  pl.BlockSpec((tc * SUBLANES, LANES), lambda i: (i + nt, 0))],
        out_specs=pl.BlockSpec((tc, d), lambda i: (i, 0)),
        out_shape=jax.ShapeDtypeStruct((n, d), F32),
        compiler_params=_params(("arbitrary",)),
        name="combine",
    )(h2, gcol, gfin, y_tok, y_tok)


def _rope_tables(s):
    half = ROT_DIM // 2
    inv_freq = (ROPE_THETA ** (-np.arange(0, ROT_DIM, 2, dtype=np.float32) / ROT_DIM)).astype(np.float32)
    ang = np.arange(s, dtype=np.float32)[:, None] * inv_freq[None, :]
    cos, sin = np.cos(ang), np.sin(ang)
    zeros = lambda w: np.zeros((s, w), np.float32)
    cos_h = np.concatenate([cos, cos, np.ones((s, HEAD_DIM - ROT_DIM), np.float32)], axis=1)
    sina_h = np.concatenate([-sin, zeros(HEAD_DIM - half)], axis=1)
    sinb_h = np.concatenate([zeros(half), sin, zeros(HEAD_DIM - ROT_DIM)], axis=1)
    tile = lambda t: jnp.asarray(np.tile(t, (1, LANES // HEAD_DIM)), F32)
    return tile(cos_h), tile(sina_h), tile(sinb_h)


def _block_diag(w):
    nblk, bi, bo = w.shape
    eye = jnp.eye(nblk, dtype=w.dtype)
    return jnp.einsum('hij,hg->higj', w, eye).reshape(nblk * bi, nblk * bo)


def _layer(h, mem, norm_mix, w_in, conv_w, conv_b, w_rg, b_rg, w_ig, b_ig, lru_lambda, norm_lru_out,
           norm_attn_out, w_out, norm_cross, norm_mem, w_xq, w_xkv, w_xo, norm_ffn, w_router_group,
           b_router_group, w_router_expert, b_router_expert, w_gate_up, w_down, norm_out):
    b, s, d = h.shape
    c = conv_w.shape[-1]
    n = b * s
    row = lambda t: t.reshape(1, -1)
    tm = 512 if s % 512 == 0 else MOBA_BLOCK
    ts = tm
    assert s % ts == 0 and s % tm == 0 and c == N_HEADS * HEAD_DIM

    kx, vx = _memkv(mem, row(norm_mem), w_xkv.astype(BF16))

    w_gates = jnp.concatenate([_block_diag(w_rg), _block_diag(w_ig)], axis=1).astype(BF16)
    b_gates = jnp.concatenate([b_rg, b_ig]).reshape(1, -1)
    cos_t, sina_t, sinb_t = _rope_tables(s)
    ylru, q, k, v = _mixer_in(h, row(norm_mix), w_in.astype(BF16), conv_w, row(conv_b), w_gates, b_gates,
                              row(lru_lambda), row(norm_lru_out), cos_t, sina_t, sinb_t, ts)
    yatt = _moba(q, k, v)

    w_out_b = w_out.astype(BF16)
    wr_t = jnp.zeros((ROUTER_ROWS, d), F32)
    wr_t = wr_t.at[0:N_GROUPS].set(w_router_group.T).at[SUBLANES:SUBLANES + N_EXPERTS].set(w_router_expert.T)
    wr_hi = wr_t.astype(BF16)
    wr_t = jnp.concatenate([wr_hi, (wr_t - wr_hi.astype(F32)).astype(BF16)], axis=0)
    br_t = jnp.zeros((ROUTER_ROWS, 1), F32)
    br_t = br_t.at[0:N_GROUPS, 0].set(b_router_group).at[SUBLANES:SUBLANES + N_EXPERTS, 0].set(b_router_expert)
    tri = jnp.asarray(np.triu(np.ones((tm, tm), np.float32), 1), BF16)
    h2, hn2, eid, gcol, rank, counts, pos, tile_cnt = _post(
        h, ylru, yatt, row(norm_attn_out), w_out_b[:c], w_out_b[c:], row(norm_cross), w_xq.astype(BF16),
        kx, vx, w_xo.astype(BF16), row(norm_ffn), wr_t, br_t, tri, tm)

    counts = counts[:, 0].astype(jnp.int32)
    padded = (counts + EXPERT_ROWS - 1) // EXPERT_ROWS * EXPERT_ROWS
    pad_ends = jnp.cumsum(padded)
    pad_starts = pad_ends - padded
    dest_flat = _dest(pad_starts.astype(jnp.int32), eid, rank)
    n_blocks = n * EXPERT_TOPK // EXPERT_ROWS + N_EXPERTS
    blk_first = jnp.arange(n_blocks, dtype=jnp.int32) * EXPERT_ROWS
    blk_expert = jnp.minimum(jnp.sum(blk_first[:, None] >= pad_ends[None, :], axis=1), N_EXPERTS - 1).astype(jnp.int32)
    n_used = (pad_ends[-1:] // EXPERT_ROWS).astype(jnp.int32)
    fill = jnp.concatenate([pad_starts + counts, padded - counts, pad_ends[-1:]]).astype(jnp.int32)

    tile_cnt = tile_cnt[:, 0].astype(jnp.int32).reshape(n // tm, N_EXPERTS)
    run_start = (pad_starts[None, :] + jnp.cumsum(tile_cnt, axis=0) - tile_cnt).astype(jnp.int32).reshape(-1)
    x_buf, inv = _dispatch(dest_flat, fill, run_start, tile_cnt.reshape(-1), hn2, pos, n_blocks * EXPERT_ROWS, tm)
    blk_valid = jnp.clip(counts[blk_expert] - (blk_first - pad_starts[blk_expert]), 0, EXPERT_ROWS).astype(jnp.int32)
    y_tok = _experts(blk_expert, n_used, blk_valid, inv, x_buf, w_gate_up, w_down, EXPERT_TOPK * n)
    out = _combine(h2, gcol, row(norm_out), y_tok, tm)
    return out.reshape(b, s, d)


def kernel(x, mem, norm_mix, w_in, conv_w, conv_b, w_rg, b_rg, w_ig, b_ig, lru_lambda, norm_lru_out, norm_attn_out,
           w_out, norm_cross, norm_mem, w_xq, w_xkv, w_xo, norm_ffn, w_router_group, b_router_group,
           w_router_expert, b_router_expert, w_gate_up, w_down, norm_final):
    depth = norm_mix.shape[0]
    assert depth == 1, "the fused final norm assumes a single layer"
    l = 0
    return _layer(x, mem, norm_mix[l], w_in[l], conv_w[l], conv_b[l], w_rg[l], b_rg[l], w_ig[l], b_ig[l],
                  lru_lambda[l], norm_lru_out[l], norm_attn_out[l], w_out[l], norm_cross[l], norm_mem[l],
                  w_xq[l], w_xkv[l], w_xo[l], norm_ffn[l], w_router_group[l], b_router_group[l],
                  w_router_expert[l], b_router_expert[l], w_gate_up[l], w_down[l], norm_final)
```

```python
import jax
import jax.numpy as jnp
import numpy as np
from jax import lax
from jax.experimental import pallas as pl
from jax.experimental.pallas import tpu as pltpu

F32 = jnp.float32
BF16 = jnp.bfloat16

CONV_WIDTH = 4
RG_C = 8.0
N_HEADS = 8
HEAD_DIM = 64
ROT_DIM = HEAD_DIM // 4
ROPE_THETA = 500000.0
MOBA_BLOCK = 256
MOBA_TOPK = 3
N_XHEADS = 4
N_GROUPS = 4
EXPERTS_PER_GROUP = 8
N_EXPERTS = N_GROUPS * EXPERTS_PER_GROUP
EXPERT_TOPK = 2
EPS = 1e-6

LANES = 128
SUBLANES = 8
NEG_BIG = -1e30
LOG2_E = 1.4426950408889634
MOBA_LOOKAHEAD = 3
EXPERT_ROWS = 512
EXPERT_CHUNKS = 4
MIXER_SUB_ROWS = 256
BF16_ROWS = 2 * SUBLANES
ROUTER_ROWS = -(-(SUBLANES + N_EXPERTS) // BF16_ROWS) * BF16_ROWS
VMEM_LIMIT = 56 * 1024 * 1024

NT_DIMS = (((1,), (1,)), ((), ()))


def _rms(x, g):
    return x * lax.rsqrt(jnp.mean(x * x, axis=-1, keepdims=True) + EPS) * g


def _params(sem):
    return pltpu.CompilerParams(dimension_semantics=sem, vmem_limit_bytes=VMEM_LIMIT)


def _memkv_kernel(mem_ref, g_ref, w_ref, k_ref, v_ref):
    d = mem_ref.shape[-1]
    mn = _rms(mem_ref[0], g_ref[...]).astype(BF16)
    kv = jnp.dot(mn, w_ref[...], preferred_element_type=F32)
    k_ref[0] = kv[:, :d].astype(BF16)
    v_ref[0] = kv[:, d:].astype(BF16)


def _memkv(mem, g, w_xkv):
    b, m, d = mem.shape
    return pl.pallas_call(
        _memkv_kernel,
        grid=(b,),
        in_specs=[pl.BlockSpec((1, m, d), lambda i: (i, 0, 0)),
                  pl.BlockSpec((1, d), lambda i: (0, 0)),
                  pl.BlockSpec((d, 2 * d), lambda i: (0, 0))],
        out_specs=[pl.BlockSpec((1, m, d), lambda i: (i, 0, 0)),
                   pl.BlockSpec((1, m, d), lambda i: (i, 0, 0))],
        out_shape=[jax.ShapeDtypeStruct((b, m, d), BF16)] * 2,
        compiler_params=_params(("arbitrary",)),
        name="memkv",
    )(mem, g, w_xkv)


def _mixer_in_kernel(x_ref, gmix_ref, win_ref, convw_ref, convb_ref, wg_ref, bg_ref, lam_ref, glru_ref,
                     cos_ref, sina_ref, sinb_ref,
                     ylru_ref, q_ref, k_ref, v_ref, xpad_ref, hcar_ref, ascan_ref, uscan_ref):
    c = ylru_ref.shape[-1]
    si = pl.program_id(1)

    @pl.when(si == 0)
    def _():
        xpad_ref[0:SUBLANES, :] = jnp.zeros((SUBLANES, c), F32)
        hcar_ref[...] = jnp.zeros_like(hcar_ref)

    n_sub = x_ref.shape[1] // ascan_ref.shape[1]
    ts = ascan_ref.shape[1]
    projs = []
    for sub in range(n_sub):
        xn = _rms(x_ref[0, sub * ts:(sub + 1) * ts, :], gmix_ref[...]).astype(BF16)
        projs.append(jnp.dot(xn, win_ref[...], preferred_element_type=F32))
    for sub in range(n_sub):
        _mixer_stage2(projs[sub], slice(sub * ts, (sub + 1) * ts), convw_ref, convb_ref, wg_ref, bg_ref, lam_ref,
                      glru_ref, cos_ref, sina_ref, sinb_ref, ylru_ref, q_ref, k_ref, v_ref, xpad_ref, hcar_ref,
                      ascan_ref, uscan_ref)


def _mixer_stage2(proj, rows_out, convw_ref, convb_ref, wg_ref, bg_ref, lam_ref, glru_ref, cos_ref, sina_ref,
                  sinb_ref, ylru_ref, q_ref, k_ref, v_ref, xpad_ref, hcar_ref, ascan_ref, uscan_ref):
    ts = proj.shape[0]
    c = ylru_ref.shape[-1]
    x_lru = proj[:, 0:c]
    g_lru = proj[:, c:2 * c]
    q = proj[:, 2 * c:3 * c]
    k = proj[:, 3 * c:4 * c]
    v = proj[:, 4 * c:5 * c]

    xpad_ref[SUBLANES:SUBLANES + ts, :] = x_lru
    cw = convw_ref[...]
    xc = convb_ref[...] + cw[3:4] * x_lru
    for j in range(1, CONV_WIDTH):
        xc = xc + cw[3 - j:4 - j] * xpad_ref[SUBLANES - j:SUBLANES - j + ts, :]
    xpad_ref[0:SUBLANES, :] = x_lru[ts - SUBLANES:ts, :]

    gates = jnp.dot(xc.astype(BF16), wg_ref[...], preferred_element_type=F32) + bg_ref[...]
    r = jax.nn.sigmoid(gates[:, 0:c])
    i = jax.nn.sigmoid(gates[:, c:2 * c])
    neg_lam = -lam_ref[...]
    softplus = jnp.maximum(neg_lam, 0.0) + jnp.log1p(jnp.exp(-jnp.abs(neg_lam)))
    log_a = -RG_C * r * softplus
    a = jnp.exp(log_a)
    u = jnp.sqrt(-jnp.tanh(log_a) * (a * a + 1.0)) * (i * xc)

    groups = ts // SUBLANES
    cols = c // LANES
    for j in range(cols):
        ascan_ref[j] = a[:, j * LANES:(j + 1) * LANES]
        uscan_ref[j] = u[:, j * LANES:(j + 1) * LANES]
    for r in range(SUBLANES):
        rows = pl.ds(r, groups, stride=SUBLANES)
        a_r = jnp.concatenate([ascan_ref[j, rows, :] for j in range(cols)], axis=1)
        u_r = jnp.concatenate([uscan_ref[j, rows, :] for j in range(cols)], axis=1)
        if r:
            u_r = a_r * u_acc + u_r
            a_r = a_r * a_acc
            for j in range(cols):
                ascan_ref[j, rows, :] = a_r[:, j * LANES:(j + 1) * LANES]
                uscan_ref[j, rows, :] = u_r[:, j * LANES:(j + 1) * LANES]
        a_acc, u_acc = a_r, u_r
    carry = hcar_ref[...]
    hs = []
    for gi in range(groups):
        blk = slice(gi * SUBLANES, (gi + 1) * SUBLANES)
        a_g = jnp.concatenate([ascan_ref[j, blk, :] for j in range(cols)], axis=1)
        u_g = jnp.concatenate([uscan_ref[j, blk, :] for j in range(cols)], axis=1)
        hs.append(a_g * carry + u_g)
        carry = a_acc[gi:gi + 1] * carry + u_acc[gi:gi + 1]
    h = jnp.concatenate(hs, axis=0)
    hcar_ref[...] = carry

    y = h * jax.nn.gelu(g_lru)
    ylru_ref[0, rows_out, :] = _rms(y, glru_ref[...]).astype(BF16)

    wide = lambda t_ref: jnp.concatenate([t_ref[rows_out, :]] * (c // LANES), axis=1)
    cos, sina, sinb = wide(cos_ref), wide(sina_ref), wide(sinb_ref)
    half = ROT_DIM // 2

    def rope(t):
        return t * cos + pltpu.roll(t, c - half, axis=1) * sina + pltpu.roll(t, half, axis=1) * sinb

    q_ref[0, rows_out, :] = rope(q)
    k_ref[0, rows_out, :] = rope(k)
    v_ref[0, rows_out, :] = v.astype(BF16)


def _mixer_in(x, gmix, w_in, conv_w, conv_b, w_gates, b_gates, lam, glru, cos_t, sina_t, sinb_t, ts):
    b, s, d = x.shape
    c = conv_w.shape[-1]
    sub = min(ts, MIXER_SUB_ROWS)
    full = lambda shape: pl.BlockSpec(shape, lambda bi, si: (0,) * len(shape))
    tab = pl.BlockSpec((ts, LANES), lambda bi, si: (si, 0))
    seq = lambda: pl.BlockSpec((1, ts, c), lambda bi, si: (bi, si, 0))
    return pl.pallas_call(
        _mixer_in_kernel,
        grid=(b, s // ts),
        in_specs=[pl.BlockSpec((1, ts, d), lambda bi, si: (bi, si, 0)),
                  full((1, d)), full(w_in.shape), full(conv_w.shape), full((1, c)),
                  full(w_gates.shape), full((1, 2 * c)), full((1, c)), full((1, c)),
                  tab, tab, tab],
        out_specs=[seq(), seq(), seq(), seq()],
        out_shape=[jax.ShapeDtypeStruct((b, s, c), BF16),
                   jax.ShapeDtypeStruct((b, s, c), F32),
                   jax.ShapeDtypeStruct((b, s, c), F32),
                   jax.ShapeDtypeStruct((b, s, c), BF16)],
        scratch_shapes=[pltpu.VMEM((sub + SUBLANES, c), F32), pltpu.VMEM((1, c), F32),
                        pltpu.VMEM((c // LANES, sub, LANES), F32), pltpu.VMEM((c // LANES, sub, LANES), F32)],
        compiler_params=_params(("arbitrary", "arbitrary")),
        name="mixer_in",
    )(x, gmix, w_in, conv_w, conv_b, w_gates, b_gates, lam, glru, cos_t, sina_t, sinb_t)


def _moba_kernel(q_ref, k_ref, v_ref, o_ref, qa_ref, ka_ref):
    s_len = q_ref.shape[1]
    nb = s_len // MOBA_BLOCK
    n_pick = min(MOBA_TOPK, nb)
    q = q_ref[0]
    k = k_ref[0]
    lane = lax.broadcasted_iota(jnp.int32, (1, LANES), 1)
    kmean = jnp.concatenate(
        [jnp.sum(k[n * MOBA_BLOCK:(n + 1) * MOBA_BLOCK], axis=0, keepdims=True) for n in range(nb)],
        axis=0) * (1.0 / MOBA_BLOCK)

    blk_row = lax.broadcasted_iota(jnp.int32, (nb, s_len), 0)
    q_blk = lax.broadcasted_iota(jnp.int32, (nb, s_len), 1) // MOBA_BLOCK
    past = blk_row < q_blk
    key_blk = lax.broadcasted_iota(jnp.int32, (s_len, LANES), 0) // MOBA_BLOCK
    lane_full = lax.broadcasted_iota(jnp.int32, (s_len, LANES), 1)

    for h in range(2):
        own = (lane >= h * HEAD_DIM) & (lane < (h + 1) * HEAD_DIM)
        off = (1 - h) * HEAD_DIM
        gate = lax.dot_general(jnp.where(own, kmean, 0.0), q, NT_DIMS,
                               precision=lax.Precision.HIGHEST, preferred_element_type=F32)
        gate = jnp.where(past, gate, -jnp.inf)
        rank = jnp.zeros((nb, s_len), jnp.int32)
        for m in range(nb):
            gm = gate[m:m + 1, :]
            ahead = (gm > gate) | ((gm == gate) & (m < blk_row))
            rank = rank + ahead.astype(jnp.int32)
        allowed = ((rank < n_pick) & past) | (blk_row == q_blk)
        bias = jnp.where(allowed, 0.0, NEG_BIG)
        pieces = []
        if off:
            pieces.append(jnp.zeros((off, s_len), F32))
        pieces.append(bias)
        if LANES - off - nb:
            pieces.append(jnp.zeros((LANES - off - nb, s_len), F32))
        bias_lanes = jnp.concatenate(pieces, axis=0).T
        qa_ref[h] = jnp.where(own, q * (HEAD_DIM ** -0.5 * LOG2_E), bias_lanes).astype(BF16)
        onehot = (lane_full - off == key_blk).astype(F32)
        ka_ref[h] = jnp.where(own, k, onehot).astype(BF16)

    own0 = lane < HEAD_DIM

    def scores(qi, h):
        nk = (qi + 1) * MOBA_BLOCK
        qa = qa_ref[h, qi * MOBA_BLOCK:(qi + 1) * MOBA_BLOCK, :]
        return lax.dot_general(qa, ka_ref[h, 0:nk, :], NT_DIMS, preferred_element_type=F32)

    key_in_blk = lax.broadcasted_iota(jnp.int32, (MOBA_BLOCK, MOBA_BLOCK), 1)
    q_in_blk = lax.broadcasted_iota(jnp.int32, (MOBA_BLOCK, MOBA_BLOCK), 0)
    causal = key_in_blk <= q_in_blk
    units = [(qi, h) for qi in range(nb) for h in range(2)]
    ahead = [scores(*u) for u in units[:MOBA_LOOKAHEAD]]
    outs = []

    def weighted_values(p, l, qi, h):
        outs.append(jnp.dot(p, v_ref[0, 0:(qi + 1) * MOBA_BLOCK, :], preferred_element_type=F32) / l)
        if h:
            o_ref[0, qi * MOBA_BLOCK:(qi + 1) * MOBA_BLOCK, :] = jnp.where(own0, outs[0], outs[1])
            outs.clear()

    pending = None
    for n, (qi, h) in enumerate(units):
        s = ahead.pop(0)
        if n + MOBA_LOOKAHEAD < len(units):
            ahead.append(scores(*units[n + MOBA_LOOKAHEAD]))
        n_past = qi * MOBA_BLOCK
        s_own = jnp.where(causal, s[:, n_past:], NEG_BIG)
        s = jnp.concatenate([s[:, :n_past], s_own], axis=1) if qi else s_own
        p = jnp.exp2(s - jnp.max(s, axis=-1, keepdims=True))
        l = jnp.sum(p, axis=-1, keepdims=True)
        if pending is not None:
            weighted_values(*pending)
        pending = (p.astype(BF16), l, qi, h)
    weighted_values(*pending)


def _moba(q, k, v):
    b, s, c = q.shape
    spec = lambda: pl.BlockSpec((1, s, LANES), lambda bi, hi: (bi, 0, hi))
    return pl.pallas_call(
        _moba_kernel,
        grid=(b, c // LANES),
        in_specs=[spec(), spec(), spec()],
        out_specs=spec(),
        out_shape=jax.ShapeDtypeStruct((b, s, c), F32),
        scratch_shapes=[pltpu.VMEM((2, s, LANES), BF16), pltpu.VMEM((2, s, LANES), BF16)],
        compiler_params=_params(("arbitrary", "arbitrary")),
        name="moba",
    )(q, k, v)


def _post_kernel(x_ref, ylru_ref, yatt_ref, gatt_ref, woa_ref, wob_ref, gcross_ref, wxq_ref, kx_ref, vx_ref,
                 wxo_ref, gffn_ref, wr_ref, br_ref, tri_ref, elow_ref,
                 h2_ref, hn2_ref, eid_ref, gcol_ref, rank_ref, cnt_ref, pos_ref, tcnt_ref, carry_ref):
    tm = x_ref.shape[1]
    d = x_ref.shape[2]
    xd = d // N_XHEADS
    first = (pl.program_id(0) == 0) & (pl.program_id(1) == 0)

    @pl.when(first)
    def _():
        carry_ref[...] = jnp.zeros_like(carry_ref)

    ya = _rms(yatt_ref[0], gatt_ref[...]).astype(BF16)
    mix = (jnp.dot(ylru_ref[0], woa_ref[...], preferred_element_type=F32)
           + jnp.dot(ya, wob_ref[...], preferred_element_type=F32))
    h1 = x_ref[0] + mix

    hn = _rms(h1, gcross_ref[...]).astype(BF16)
    qx = jnp.dot(hn, wxq_ref[...], preferred_element_type=F32)
    heads = []
    for hh in range(N_XHEADS):
        qh = (qx[:, hh * xd:(hh + 1) * xd] * (xd ** -0.5)).astype(BF16)
        s = lax.dot_general(qh, kx_ref[0, :, hh * xd:(hh + 1) * xd], NT_DIMS, preferred_element_type=F32)
        p = jnp.exp(s - jnp.max(s, axis=-1, keepdims=True))
        l = jnp.sum(p, axis=-1, keepdims=True)
        o = jnp.dot(p.astype(BF16), vx_ref[0, :, hh * xd:(hh + 1) * xd], preferred_element_type=F32) / l
        heads.append(o.astype(BF16))
    h2 = h1 + jnp.dot(jnp.concatenate(heads, axis=1), wxo_ref[...], preferred_element_type=F32)
    hn2 = _rms(h2, gffn_ref[...])
    h2_ref[...] = h2
    hn2_ref[...] = hn2

    x_hi = hn2.astype(BF16)
    x_lo = (hn2 - x_hi.astype(F32)).astype(BF16)
    by_hi = lax.dot_general(wr_ref[...], x_hi, NT_DIMS, preferred_element_type=F32)
    by_lo = lax.dot_general(wr_ref[0:ROUTER_ROWS], x_lo, NT_DIMS, preferred_element_type=F32)
    logits = by_hi[0:ROUTER_ROWS] + by_hi[ROUTER_ROWS:2 * ROUTER_ROWS] + by_lo + br_ref[...]
    gl = logits[0:N_GROUPS]
    gmax = jnp.max(gl, axis=0, keepdims=True)
    gi = lax.broadcasted_iota(jnp.int32, gl.shape, 0)
    grp = jnp.min(jnp.where(gl == gmax, gi, N_GROUPS), axis=0, keepdims=True)
    g_w = 1.0 / jnp.sum(jnp.exp(gl - gmax), axis=0, keepdims=True)
    el = jnp.zeros((EXPERTS_PER_GROUP, tm), F32)
    for g in range(N_GROUPS):
        lo = SUBLANES + g * EXPERTS_PER_GROUP
        el = jnp.where(grp == g, logits[lo:lo + EXPERTS_PER_GROUP], el)
    ee = jnp.exp(el - jnp.max(el, axis=0, keepdims=True))
    ep = ee / jnp.sum(ee, axis=0, keepdims=True)
    ei = lax.broadcasted_iota(jnp.int32, ep.shape, 0)
    p1 = jnp.max(ep, axis=0, keepdims=True)
    i1 = jnp.min(jnp.where(ep == p1, ei, EXPERTS_PER_GROUP), axis=0, keepdims=True)
    ep_rest = jnp.where(ei == i1, -1.0, ep)
    p2 = jnp.max(ep_rest, axis=0, keepdims=True)
    i2 = jnp.min(jnp.where(ep_rest == p2, ei, EXPERTS_PER_GROUP), axis=0, keepdims=True)
    den = p1 + p2
    gate1 = g_w * p1 / den
    gate2 = g_w * p2 / den
    e1 = grp * EXPERTS_PER_GROUP + i1
    e2 = grp * EXPERTS_PER_GROUP + i2
    eid_ref[0:1, :] = e1
    eid_ref[1:2, :] = e2
    li = lax.broadcasted_iota(jnp.int32, (LANES, tm), 0)
    gcol_ref[...] = jnp.where(li == 0, gate1, jnp.where(li == 1, gate2, 0.0)).T

    xi = lax.broadcasted_iota(jnp.int32, (N_EXPERTS, tm), 0)
    oh1 = xi == e1
    oh2 = xi == e2
    cnt = oh1.astype(F32) + oh2.astype(F32)
    in_tile = jnp.dot(cnt.astype(BF16), tri_ref[...], preferred_element_type=F32)
    before = in_tile + carry_ref[:, 0:1]
    rank_ref[0:1, :] = jnp.sum(jnp.where(oh1, before, 0.0), axis=0, keepdims=True).astype(jnp.int32)
    rank_ref[1:2, :] = jnp.sum(jnp.where(oh2, before, 0.0), axis=0, keepdims=True).astype(jnp.int32)
    tile_cnt = jnp.broadcast_to(jnp.sum(cnt, axis=1, keepdims=True), (N_EXPERTS, LANES))
    carry_ref[...] = carry_ref[...] + tile_cnt
    cnt_ref[...] = carry_ref[...]
    tcnt_ref[...] = tile_cnt
    tile_start = jnp.dot(elow_ref[...], tile_cnt, precision=lax.Precision.HIGHEST,
                         preferred_element_type=F32)[:, 0:1]
    local = in_tile + tile_start
    pos_ref[0:1, :] = jnp.sum(jnp.where(oh1, local, 0.0), axis=0, keepdims=True).astype(jnp.int32)
    pos_ref[1:2, :] = jnp.sum(jnp.where(oh2, local, 0.0), axis=0, keepdims=True).astype(jnp.int32)


def _post(x, ylru, yatt, gatt, wo_a, wo_b, gcross, wxq, kx, vx, wxo, gffn, wr_t, br_t, tri, tm):
    b, s, d = x.shape
    c = ylru.shape[-1]
    m = kx.shape[1]
    n = b * s
    nt = s // tm
    full = lambda shape: pl.BlockSpec(shape, lambda bi, si: (0,) * len(shape))
    tok = lambda rows: pl.BlockSpec((rows, tm), lambda bi, si: (0, bi * nt + si))
    return pl.pallas_call(
        _post_kernel,
        grid=(b, nt),
        in_specs=[pl.BlockSpec((1, tm, d), lambda bi, si: (bi, si, 0)),
                  pl.BlockSpec((1, tm, c), lambda bi, si: (bi, si, 0)),
                  pl.BlockSpec((1, tm, c), lambda bi, si: (bi, si, 0)),
                  full((1, c)), full((c, d)), full((c, d)), full((1, d)), full((d, d)),
                  pl.BlockSpec((1, m, d), lambda bi, si: (bi, 0, 0)),
                  pl.BlockSpec((1, m, d), lambda bi, si: (bi, 0, 0)),
                  full((d, d)), full((1, d)), full(wr_t.shape), full(br_t.shape), full((tm, tm)),
                  full((N_EXPERTS, N_EXPERTS))],
        out_specs=[pl.BlockSpec((tm, d), lambda bi, si: (bi * nt + si, 0)),
                   pl.BlockSpec((tm, d), lambda bi, si: (bi * nt + si, 0)),
                   tok(EXPERT_TOPK),
                   pl.BlockSpec((tm, LANES), lambda bi, si: (bi * nt + si, 0)),
                   tok(EXPERT_TOPK), full((N_EXPERTS, LANES)),
                   tok(EXPERT_TOPK),
                   pl.BlockSpec((N_EXPERTS, LANES), lambda bi, si: (bi * nt + si, 0))],
        out_shape=[jax.ShapeDtypeStruct((n, d), F32),
                   jax.ShapeDtypeStruct((n, d), F32),
                   jax.ShapeDtypeStruct((EXPERT_TOPK, n), jnp.int32),
                   jax.ShapeDtypeStruct((n, LANES), F32),
                   jax.ShapeDtypeStruct((EXPERT_TOPK, n), jnp.int32),
                   jax.ShapeDtypeStruct((N_EXPERTS, LANES), F32),
                   jax.ShapeDtypeStruct((EXPERT_TOPK, n), jnp.int32),
                   jax.ShapeDtypeStruct((n // tm * N_EXPERTS, LANES), F32)],
        scratch_shapes=[pltpu.VMEM((N_EXPERTS, LANES), F32)],
        compiler_params=_params(("arbitrary", "arbitrary")),
        name="post",
    )(x, ylru, yatt, gatt, wo_a, wo_b, gcross, wxq, kx, vx, wxo, gffn, wr_t, br_t, tri,
      jnp.asarray(np.tril(np.ones((N_EXPERTS, N_EXPERTS), np.float32), -1)))


def _dest_kernel(ps_ref, eid_ref, rank_ref, dest_ref):
    eid = eid_ref[...]
    dest = rank_ref[...]
    for e in range(N_EXPERTS):
        dest = dest + jnp.where(eid == e, ps_ref[e], 0)
    dest_ref[...] = dest


def _dest(pad_starts, eid, rank):
    shape = (eid.size // LANES, LANES)
    spec = pl.BlockSpec(shape, lambda i, ps: (0, 0))
    return pl.pallas_call(
        _dest_kernel,
        grid_spec=pltpu.PrefetchScalarGridSpec(num_scalar_prefetch=1, grid=(1,), in_specs=[spec, spec],
                                               out_specs=spec),
        out_shape=jax.ShapeDtypeStruct(shape, jnp.int32),
        compiler_params=_params(("arbitrary",)),
        name="dest",
    )(pad_starts, eid.reshape(shape), rank.reshape(shape)).reshape(-1)


def _pieces(largest):
    piece = largest
    while piece:
        yield piece
        piece //= 2


def _tile_rows(row, count):
    return pl.ds(pl.multiple_of(row * SUBLANES, SUBLANES), count * SUBLANES)


def _dispatch_kernel(dest_ref, fill_ref, rstart_ref, rlen_ref, hn_ref, pos_ref, xout_ref, inv_ref,
                     sorted_ref, zero_ref, sems, zsem):
    tm = hn_ref.shape[0]
    n = dest_ref.shape[0] // EXPERT_TOPK
    n_sorted = EXPERT_TOPK * tm
    step = pl.program_id(0)
    base = step * tm

    def dump_rows(first, end):
        def group(g, carry):
            for i in range(SUBLANES):
                row = g * SUBLANES + i
                inv_ref[row] = EXPERT_TOPK * n + (row & (EXPERT_ROWS - 1))
            return carry

        lax.fori_loop(first // SUBLANES, end // SUBLANES, group, 0)

    zero_rows = zero_ref.shape[0] // SUBLANES
    n_rows = xout_ref.shape[0] // SUBLANES

    def pad_copies(e, start):
        first = fill_ref[e]
        length = fill_ref[N_EXPERTS + e]
        if start:
            dump_rows(first, first + length)
        for piece in _pieces(zero_rows):
            offset = first + (length & ~(2 * piece - 1))

            @pl.when((length & piece) != 0)
            def _():
                cp = pltpu.make_async_copy(zero_ref.at[_tile_rows(0, piece)], xout_ref.at[_tile_rows(offset, piece)],
                                           zsem)
                cp.start() if start else cp.wait()

    def tail_copies(start):
        def one(i, carry):
            cp = pltpu.make_async_copy(zero_ref, xout_ref.at[_tile_rows(i * zero_rows, zero_rows)], zsem)
            cp.start() if start else cp.wait()
            return carry

        lax.fori_loop(fill_ref[2 * N_EXPERTS] // zero_rows, n_rows // zero_rows, one, 0)
        if start:
            dump_rows(fill_ref[2 * N_EXPERTS], n_rows)

    @pl.when(step == 0)
    def _():
        zero_ref[...] = jnp.zeros_like(zero_ref)
        lax.fori_loop(0, N_EXPERTS, lambda e, c: (pad_copies(e, True), c)[1], 0)
        tail_copies(True)

    pos = pos_ref[...]
    p_iota = lax.broadcasted_iota(jnp.int32, (n_sorted, tm), 0)
    picks = jnp.where((p_iota == pos[0:1, :]) | (p_iota == pos[1:2, :]), 1.0, 0.0).astype(BF16)
    for slot in range(EXPERT_TOPK):
        for r in range(tm):
            src = slot * n + base + r
            inv_ref[dest_ref[src]] = src
    xs = jnp.dot(picks, hn_ref[...].astype(BF16), preferred_element_type=F32)
    par = step % 2
    for k in range(SUBLANES):
        sorted_ref[par, pl.ds(k, n_sorted, stride=SUBLANES), :] = xs[:, k * LANES:(k + 1) * LANES]

    def run(e, offset):
        length = rlen_ref[step * N_EXPERTS + e]
        first = rstart_ref[step * N_EXPERTS + e]
        for piece in _pieces(tm):
            done = length & ~(2 * piece - 1)

            @pl.when((length & piece) != 0)
            def _():
                pltpu.make_async_copy(sorted_ref.at[par, _tile_rows(offset + done, piece)],
                                      xout_ref.at[_tile_rows(first + done, piece)], sems.at[par]).start()
        return offset + length

    lax.fori_loop(0, N_EXPERTS, run, 0)

    def wait_tile(buf):
        pltpu.make_async_copy(sorted_ref.at[buf], xout_ref.at[_tile_rows(0, n_sorted)], sems.at[buf]).wait()

    @pl.when(step >= 1)
    def _():
        wait_tile(1 - par)

    @pl.when(step == pl.num_programs(0) - 1)
    def _():
        wait_tile(par)

    @pl.when(step == 0)
    def _():
        lax.fori_loop(0, N_EXPERTS, lambda e, c: (pad_copies(e, False), c)[1], 0)
        tail_copies(False)


def _dispatch(dest_flat, fill, run_start, run_len, hn2, pos, n_rows, tm):
    n, d = hn2.shape
    assert d == SUBLANES * LANES, "a row must be exactly one (8, 128) tile for the tile-contiguous layout"
    return pl.pallas_call(
        _dispatch_kernel,
        grid_spec=pltpu.PrefetchScalarGridSpec(
            num_scalar_prefetch=4,
            grid=(n // tm,),
            in_specs=[pl.BlockSpec((tm, d), lambda i, *_: (i, 0)),
                      pl.BlockSpec((EXPERT_TOPK, tm), lambda i, *_: (0, i))],
            out_specs=[pl.BlockSpec(memory_space=pl.ANY), pl.BlockSpec(memory_space=pltpu.SMEM)],
            scratch_shapes=[pltpu.VMEM((2, EXPERT_TOPK * tm * SUBLANES, LANES), F32),
                            pltpu.VMEM((EXPERT_ROWS // 2 * SUBLANES, LANES), F32),
                            pltpu.SemaphoreType.DMA((2,)), pltpu.SemaphoreType.DMA(())]),
        out_shape=[jax.ShapeDtypeStruct((n_rows * SUBLANES, LANES), F32),
                   jax.ShapeDtypeStruct((n_rows,), jnp.int32)],
        compiler_params=_params(("arbitrary",)),
        name="dispatch",
    )(dest_flat, fill, run_start, run_len, hn2, pos)


def _experts_kernel(be_ref, nu_ref, inv_ref, x_ref, wgu_ref, wd_ref, ytok_ref, wgu_bf_ref, wd_bf_ref, ybuf_ref, sems):
    j = pl.program_id(0)
    rows = x_ref.shape[0] // SUBLANES
    f = wd_ref.shape[1]
    n_used = nu_ref[0]
    par = j % 2
    last_blk = be_ref.shape[0] - 1

    pieces = SUBLANES

    def scatter_copy(blk_par, r, dst_row):
        dst = pl.multiple_of(dst_row * pieces, pieces)
        return pltpu.make_async_copy(ybuf_ref.at[blk_par, pl.ds(r * pieces, pieces)],
                                     ytok_ref.at[pl.ds(dst, pieces)], sems.at[blk_par])

    chunk = rows // EXPERT_CHUNKS

    def start_scatter_of_previous(part):
        base = (j - 1) * rows
        for r in range(part * chunk, (part + 1) * chunk):
            scatter_copy(1 - par, r, inv_ref[base + r]).start()

    def wait_scatter(blk_par):
        pltpu.make_async_copy(ybuf_ref.at[blk_par], ytok_ref.at[pl.ds(0, rows * pieces)], sems.at[blk_par]).wait()

    def mlp(part):
        x = jnp.concatenate([x_ref[pl.ds(part * chunk * pieces + k, chunk, stride=pieces), :] for k in range(pieces)],
                            axis=1)
        gu = jnp.dot(x.astype(BF16), wgu_bf_ref[...], preferred_element_type=F32)
        act = jax.nn.silu(gu[:, 0:f]) * gu[:, f:2 * f]
        y = jnp.dot(act.astype(BF16), wd_bf_ref[...], preferred_element_type=F32)
        for k in range(pieces):
            ybuf_ref[par, pl.ds(part * chunk * pieces + k, chunk, stride=pieces), :] = y[:, k * LANES:(k + 1) * LANES]

    jc = jnp.minimum(j, last_blk)
    @pl.when((j < n_used) & ((j == 0) | (be_ref[jc] != be_ref[jnp.maximum(jc, 1) - 1])))
    def _():
        wgu_bf_ref[...] = wgu_ref[0].astype(BF16)
        wd_bf_ref[...] = wd_ref[0].astype(BF16)

    @pl.when((j >= 2) & (j <= n_used))
    def _():
        wait_scatter(par)

    @pl.when(j == 0)
    def _():
        ybuf_ref[1] = jnp.zeros(ybuf_ref.shape[1:], F32)
        cp = pltpu.make_async_copy(ybuf_ref.at[1], ytok_ref.at[pl.ds(ytok_ref.shape[0] - rows * pieces, rows * pieces)],
                                   sems.at[1])
        cp.start()
        cp.wait()

    @pl.when((j == 0) & (j < n_used))
    def _():
        for part in range(EXPERT_CHUNKS):
            mlp(part)

    @pl.when((j >= 1) & (j < n_used))
    def _():
        for part in range(EXPERT_CHUNKS):
            start_scatter_of_previous(part)
            mlp(part)

    @pl.when((j >= 1) & (j == n_used))
    def _():
        for part in range(EXPERT_CHUNKS):
            start_scatter_of_previous(part)
        wait_scatter(1 - par)


def _experts(blk_expert, n_used, inv, x_buf, w_gate_up, w_down, n_out_rows):
    d = w_gate_up.shape[1]
    f = w_down.shape[1]
    r = EXPERT_ROWS
    n_blk = x_buf.shape[0] // (r * SUBLANES)
    assert d == SUBLANES * LANES and x_buf.shape[1] == LANES, "rows are (8, 128) tiles (tile-contiguous layout)"
    x_map = lambda j, be, nu, inv: (jnp.minimum(j, jnp.maximum(nu[0], 1) - 1), 0)
    w_map = lambda j, be, nu, inv: (be[jnp.minimum(j, n_blk - 1)], 0, 0)
    return pl.pallas_call(
        _experts_kernel,
        grid_spec=pltpu.PrefetchScalarGridSpec(
            num_scalar_prefetch=3,
            grid=(n_blk + 1,),
            in_specs=[pl.BlockSpec((r * SUBLANES, LANES), x_map),
                      pl.BlockSpec((1, d, 2 * f), w_map),
                      pl.BlockSpec((1, f, d), w_map)],
            out_specs=pl.BlockSpec(memory_space=pl.ANY),
            scratch_shapes=[pltpu.VMEM((d, 2 * f), BF16), pltpu.VMEM((f, d), BF16),
                            pltpu.VMEM((2, r * SUBLANES, LANES), F32), pltpu.SemaphoreType.DMA((2,))]),
        out_shape=jax.ShapeDtypeStruct((n_out_rows * SUBLANES, LANES), F32),
        compiler_params=_params(("arbitrary",)),
        name="experts",
    )(blk_expert, n_used, inv, x_buf, w_gate_up, w_down)


def _combine_kernel(h2_ref, gcol_ref, gfin_ref, y0_ref, y1_ref, o_ref):
    tc = h2_ref.shape[0]

    def rows_of(y_ref):
        return jnp.concatenate([y_ref[pl.ds(k, tc, stride=SUBLANES), :] for k in range(SUBLANES)], axis=1)

    g = gcol_ref[...]
    moe = g[:, 0:1] * rows_of(y0_ref) + g[:, 1:2] * rows_of(y1_ref)
    o_ref[...] = _rms(h2_ref[...] + moe, gfin_ref[...])


def _combine(h2, gcol, gfin, y_tok, tc):
    n, d = h2.shape
    nt = n // tc
    return pl.pallas_call(
        _combine_kernel,
        grid=(nt,),
        in_specs=[pl.BlockSpec((tc, d), lambda i: (i, 0)),
                  pl.BlockSpec((tc, LANES), lambda i: (i, 0)),
                  pl.BlockSpec((1, d), lambda i: (0, 0)),
                  pl.BlockSpec((tc * SUBLANES, LANES), lambda i: (i, 0)),
                  pl.BlockSpec((tc * SUBLANES, LANES), lambda i: (i + nt, 0))],
        out_specs=pl.BlockSpec((tc, d), lambda i: (i, 0)),
        out_shape=jax.ShapeDtypeStruct((n, d), F32),
        compiler_params=_params(("arbitrary",)),
        name="combine",
    )(h2, gcol, gfin, y_tok, y_tok)


def _rope_tables(s):
    half = ROT_DIM // 2
    inv_freq = (ROPE_THETA ** (-np.arange(0, ROT_DIM, 2, dtype=np.float32) / ROT_DIM)).astype(np.float32)
    ang = np.arange(s, dtype=np.float32)[:, None] * inv_freq[None, :]
    cos, sin = np.cos(ang), np.sin(ang)
    zeros = lambda w: np.zeros((s, w), np.float32)
    cos_h = np.concatenate([cos, cos, np.ones((s, HEAD_DIM - ROT_DIM), np.float32)], axis=1)
    sina_h = np.concatenate([-sin, zeros(HEAD_DIM - half)], axis=1)
    sinb_h = np.concatenate([zeros(half), sin, zeros(HEAD_DIM - ROT_DIM)], axis=1)
    tile = lambda t: jnp.asarray(np.tile(t, (1, LANES // HEAD_DIM)), F32)
    return tile(cos_h), tile(sina_h), tile(sinb_h)


def _block_diag(w):
    nblk, bi, bo = w.shape
    eye = jnp.eye(nblk, dtype=w.dtype)
    return jnp.einsum('hij,hg->higj', w, eye).reshape(nblk * bi, nblk * bo)


def _layer(h, mem, norm_mix, w_in, conv_w, conv_b, w_rg, b_rg, w_ig, b_ig, lru_lambda, norm_lru_out,
           norm_attn_out, w_out, norm_cross, norm_mem, w_xq, w_xkv, w_xo, norm_ffn, w_router_group,
           b_router_group, w_router_expert, b_router_expert, w_gate_up, w_down, norm_out):
    b, s, d = h.shape
    c = conv_w.shape[-1]
    n = b * s
    row = lambda t: t.reshape(1, -1)
    tm = 512 if s % 512 == 0 else MOBA_BLOCK
    ts = tm
    assert s % ts == 0 and s % tm == 0 and c == N_HEADS * HEAD_DIM

    kx, vx = _memkv(mem, row(norm_mem), w_xkv.astype(BF16))

    w_gates = jnp.concatenate([_block_diag(w_rg), _block_diag(w_ig)], axis=1).astype(BF16)
    b_gates = jnp.concatenate([b_rg, b_ig]).reshape(1, -1)
    cos_t, sina_t, sinb_t = _rope_tables(s)
    ylru, q, k, v = _mixer_in(h, row(norm_mix), w_in.astype(BF16), conv_w, row(conv_b), w_gates, b_gates,
                              row(lru_lambda), row(norm_lru_out), cos_t, sina_t, sinb_t, ts)
    yatt = _moba(q, k, v)

    w_out_b = w_out.astype(BF16)
    wr_t = jnp.zeros((ROUTER_ROWS, d), F32)
    wr_t = wr_t.at[0:N_GROUPS].set(w_router_group.T).at[SUBLANES:SUBLANES + N_EXPERTS].set(w_router_expert.T)
    wr_hi = wr_t.astype(BF16)
    wr_t = jnp.concatenate([wr_hi, (wr_t - wr_hi.astype(F32)).astype(BF16)], axis=0)
    br_t = jnp.zeros((ROUTER_ROWS, 1), F32)
    br_t = br_t.at[0:N_GROUPS, 0].set(b_router_group).at[SUBLANES:SUBLANES + N_EXPERTS, 0].set(b_router_expert)
    tri = jnp.asarray(np.triu(np.ones((tm, tm), np.float32), 1), BF16)
    h2, hn2, eid, gcol, rank, counts, pos, tile_cnt = _post(
        h, ylru, yatt, row(norm_attn_out), w_out_b[:c], w_out_b[c:], row(norm_cross), w_xq.astype(BF16),
        kx, vx, w_xo.astype(BF16), row(norm_ffn), wr_t, br_t, tri, tm)

    counts = counts[:, 0].astype(jnp.int32)
    padded = (counts + EXPERT_ROWS - 1) // EXPERT_ROWS * EXPERT_ROWS
    pad_ends = jnp.cumsum(padded)
    pad_starts = pad_ends - padded
    dest_flat = _dest(pad_starts.astype(jnp.int32), eid, rank)
    n_blocks = n * EXPERT_TOPK // EXPERT_ROWS + N_EXPERTS
    blk_first = jnp.arange(n_blocks, dtype=jnp.int32) * EXPERT_ROWS
    blk_expert = jnp.minimum(jnp.sum(blk_first[:, None] >= pad_ends[None, :], axis=1), N_EXPERTS - 1).astype(jnp.int32)
    n_used = (pad_ends[-1:] // EXPERT_ROWS).astype(jnp.int32)
    fill = jnp.concatenate([pad_starts + counts, padded - counts, pad_ends[-1:]]).astype(jnp.int32)

    tile_cnt = tile_cnt[:, 0].astype(jnp.int32).reshape(n // tm, N_EXPERTS)
    run_start = (pad_starts[None, :] + jnp.cumsum(tile_cnt, axis=0) - tile_cnt).astype(jnp.int32).reshape(-1)
    x_buf, inv = _dispatch(dest_flat, fill, run_start, tile_cnt.reshape(-1), hn2, pos, n_blocks * EXPERT_ROWS, tm)
    y_tok = _experts(blk_expert, n_used, inv, x_buf, w_gate_up, w_down, EXPERT_TOPK * n + EXPERT_ROWS)
    out = _combine(h2, gcol, row(norm_out), y_tok, tm)
    return out.reshape(b, s, d)


def kernel(x, mem, norm_mix, w_in, conv_w, conv_b, w_rg, b_rg, w_ig, b_ig, lru_lambda, norm_lru_out, norm_attn_out,
           w_out, norm_cross, norm_mem, w_xq, w_xkv, w_xo, norm_ffn, w_router_group, b_router_group,
           w_router_expert, b_router_expert, w_gate_up, w_down, norm_final):
    depth = norm_mix.shape[0]
    assert depth == 1, "the fused final norm assumes a single layer"
    l = 0
    return _layer(x, mem, norm_mix[l], w_in[l], conv_w[l], conv_b[l], w_rg[l], b_rg[l], w_ig[l], b_ig[l],
                  lru_lambda[l], norm_lru_out[l], norm_attn_out[l], w_out[l], norm_cross[l], norm_mem[l],
                  w_xq[l], w_xkv[l], w_xo[l], norm_ffn[l], w_router_group[l], b_router_group[l],
                  w_router_expert[l], b_router_expert[l], w_gate_up[l], w_down[l], norm_final)
```

```python
import jax
import jax.numpy as jnp
import numpy as np
from jax import lax
from jax.experimental import pallas as pl
from jax.experimental.pallas import tpu as pltpu

F32 = jnp.float32
BF16 = jnp.bfloat16

CONV_WIDTH = 4
RG_C = 8.0
N_HEADS = 8
HEAD_DIM = 64
ROT_DIM = HEAD_DIM // 4
ROPE_THETA = 500000.0
MOBA_BLOCK = 256
MOBA_TOPK = 3
N_XHEADS = 4
N_GROUPS = 4
EXPERTS_PER_GROUP = 8
N_EXPERTS = N_GROUPS * EXPERTS_PER_GROUP
EXPERT_TOPK = 2
EPS = 1e-6

LANES = 128
SUBLANES = 8
NEG_BIG = -1e30
LOG2_E = 1.4426950408889634
MOBA_LOOKAHEAD = 3
EXPERT_ROWS = 512
RUN_SPLIT = 64
EXPERT_CHUNKS = 4
MIXER_SUB_ROWS = 256
BF16_ROWS = 2 * SUBLANES
ROUTER_ROWS = -(-(SUBLANES + N_EXPERTS) // BF16_ROWS) * BF16_ROWS
VMEM_LIMIT = 56 * 1024 * 1024

NT_DIMS = (((1,), (1,)), ((), ()))


def _rms(x, g):
    return x * lax.rsqrt(jnp.mean(x * x, axis=-1, keepdims=True) + EPS) * g


def _params(sem):
    return pltpu.CompilerParams(dimension_semantics=sem, vmem_limit_bytes=VMEM_LIMIT)


def _memkv_kernel(mem_ref, g_ref, w_ref, k_ref, v_ref):
    d = mem_ref.shape[-1]
    mn = _rms(mem_ref[0], g_ref[...]).astype(BF16)
    kv = jnp.dot(mn, w_ref[...], preferred_element_type=F32)
    k_ref[0] = kv[:, :d].astype(BF16)
    v_ref[0] = kv[:, d:].astype(BF16)


def _memkv(mem, g, w_xkv):
    b, m, d = mem.shape
    return pl.pallas_call(
        _memkv_kernel,
        grid=(b,),
        in_specs=[pl.BlockSpec((1, m, d), lambda i: (i, 0, 0)),
                  pl.BlockSpec((1, d), lambda i: (0, 0)),
                  pl.BlockSpec((d, 2 * d), lambda i: (0, 0))],
        out_specs=[pl.BlockSpec((1, m, d), lambda i: (i, 0, 0)),
                   pl.BlockSpec((1, m, d), lambda i: (i, 0, 0))],
        out_shape=[jax.ShapeDtypeStruct((b, m, d), BF16)] * 2,
        compiler_params=_params(("arbitrary",)),
        name="memkv",
    )(mem, g, w_xkv)


def _mixer_in_kernel(x_ref, gmix_ref, win_ref, convw_ref, convb_ref, wg_ref, bg_ref, lam_ref, glru_ref,
                     cos_ref, sina_ref, sinb_ref,
                     ylru_ref, q_ref, k_ref, v_ref, xpad_ref, hcar_ref, ascan_ref, uscan_ref):
    c = ylru_ref.shape[-1]
    si = pl.program_id(1)

    @pl.when(si == 0)
    def _():
        xpad_ref[0:SUBLANES, :] = jnp.zeros((SUBLANES, c), F32)
        hcar_ref[...] = jnp.zeros_like(hcar_ref)

    n_sub = x_ref.shape[1] // ascan_ref.shape[1]
    ts = ascan_ref.shape[1]
    projs = []
    for sub in range(n_sub):
        xn = _rms(x_ref[0, sub * ts:(sub + 1) * ts, :], gmix_ref[...]).astype(BF16)
        projs.append(jnp.dot(xn, win_ref[...], preferred_element_type=F32))
    for sub in range(n_sub):
        _mixer_stage2(projs[sub], slice(sub * ts, (sub + 1) * ts), convw_ref, convb_ref, wg_ref, bg_ref, lam_ref,
                      glru_ref, cos_ref, sina_ref, sinb_ref, ylru_ref, q_ref, k_ref, v_ref, xpad_ref, hcar_ref,
                      ascan_ref, uscan_ref)


def _mixer_stage2(proj, rows_out, convw_ref, convb_ref, wg_ref, bg_ref, lam_ref, glru_ref, cos_ref, sina_ref,
                  sinb_ref, ylru_ref, q_ref, k_ref, v_ref, xpad_ref, hcar_ref, ascan_ref, uscan_ref):
    ts = proj.shape[0]
    c = ylru_ref.shape[-1]
    x_lru = proj[:, 0:c]
    g_lru = proj[:, c:2 * c]
    q = proj[:, 2 * c:3 * c]
    k = proj[:, 3 * c:4 * c]
    v = proj[:, 4 * c:5 * c]

    xpad_ref[SUBLANES:SUBLANES + ts, :] = x_lru
    cw = convw_ref[...]
    xc = convb_ref[...] + cw[3:4] * x_lru
    for j in range(1, CONV_WIDTH):
        xc = xc + cw[3 - j:4 - j] * xpad_ref[SUBLANES - j:SUBLANES - j + ts, :]
    xpad_ref[0:SUBLANES, :] = x_lru[ts - SUBLANES:ts, :]

    gates = jnp.dot(xc.astype(BF16), wg_ref[...], preferred_element_type=F32) + bg_ref[...]
    r = jax.nn.sigmoid(gates[:, 0:c])
    i = jax.nn.sigmoid(gates[:, c:2 * c])
    neg_lam = -lam_ref[...]
    softplus = jnp.maximum(neg_lam, 0.0) + jnp.log1p(jnp.exp(-jnp.abs(neg_lam)))
    log_a = -RG_C * r * softplus
    a = jnp.exp(log_a)
    u = jnp.sqrt(-jnp.tanh(log_a) * (a * a + 1.0)) * (i * xc)

    groups = ts // SUBLANES
    cols = c // LANES
    for j in range(cols):
        ascan_ref[j] = a[:, j * LANES:(j + 1) * LANES]
        uscan_ref[j] = u[:, j * LANES:(j + 1) * LANES]
    for r in range(SUBLANES):
        rows = pl.ds(r, groups, stride=SUBLANES)
        a_r = jnp.concatenate([ascan_ref[j, rows, :] for j in range(cols)], axis=1)
        u_r = jnp.concatenate([uscan_ref[j, rows, :] for j in range(cols)], axis=1)
        if r:
            u_r = a_r * u_acc + u_r
            a_r = a_r * a_acc
            for j in range(cols):
                ascan_ref[j, rows, :] = a_r[:, j * LANES:(j + 1) * LANES]
                uscan_ref[j, rows, :] = u_r[:, j * LANES:(j + 1) * LANES]
        a_acc, u_acc = a_r, u_r
    carry = hcar_ref[...]
    hs = []
    for gi in range(groups):
        blk = slice(gi * SUBLANES, (gi + 1) * SUBLANES)
        a_g = jnp.concatenate([ascan_ref[j, blk, :] for j in range(cols)], axis=1)
        u_g = jnp.concatenate([uscan_ref[j, blk, :] for j in range(cols)], axis=1)
        hs.append(a_g * carry + u_g)
        carry = a_acc[gi:gi + 1] * carry + u_acc[gi:gi + 1]
    h = jnp.concatenate(hs, axis=0)
    hcar_ref[...] = carry

    y = h * jax.nn.gelu(g_lru)
    ylru_ref[0, rows_out, :] = _rms(y, glru_ref[...]).astype(BF16)

    wide = lambda t_ref: jnp.concatenate([t_ref[rows_out, :]] * (c // LANES), axis=1)
    cos, sina, sinb = wide(cos_ref), wide(sina_ref), wide(sinb_ref)
    half = ROT_DIM // 2

    def rope(t):
        return t * cos + pltpu.roll(t, c - half, axis=1) * sina + pltpu.roll(t, half, axis=1) * sinb

    q_ref[0, rows_out, :] = rope(q)
    k_ref[0, rows_out, :] = rope(k)
    v_ref[0, rows_out, :] = v.astype(BF16)


def _mixer_in(x, gmix, w_in, conv_w, conv_b, w_gates, b_gates, lam, glru, cos_t, sina_t, sinb_t, ts):
    b, s, d = x.shape
    c = conv_w.shape[-1]
    sub = min(ts, MIXER_SUB_ROWS)
    full = lambda shape: pl.BlockSpec(shape, lambda bi, si: (0,) * len(shape))
    tab = pl.BlockSpec((ts, LANES), lambda bi, si: (si, 0))
    seq = lambda: pl.BlockSpec((1, ts, c), lambda bi, si: (bi, si, 0))
    return pl.pallas_call(
        _mixer_in_kernel,
        grid=(b, s // ts),
        in_specs=[pl.BlockSpec((1, ts, d), lambda bi, si: (bi, si, 0)),
                  full((1, d)), full(w_in.shape), full(conv_w.shape), full((1, c)),
                  full(w_gates.shape), full((1, 2 * c)), full((1, c)), full((1, c)),
                  tab, tab, tab],
        out_specs=[seq(), seq(), seq(), seq()],
        out_shape=[jax.ShapeDtypeStruct((b, s, c), BF16),
                   jax.ShapeDtypeStruct((b, s, c), F32),
                   jax.ShapeDtypeStruct((b, s, c), F32),
                   jax.ShapeDtypeStruct((b, s, c), BF16)],
        scratch_shapes=[pltpu.VMEM((sub + SUBLANES, c), F32), pltpu.VMEM((1, c), F32),
                        pltpu.VMEM((c // LANES, sub, LANES), F32), pltpu.VMEM((c // LANES, sub, LANES), F32)],
        compiler_params=_params(("arbitrary", "arbitrary")),
        name="mixer_in",
    )(x, gmix, w_in, conv_w, conv_b, w_gates, b_gates, lam, glru, cos_t, sina_t, sinb_t)


def _moba_kernel(q_ref, k_ref, v_ref, kpat_ref, o_ref, qa_ref, ka_ref):
    s_len = q_ref.shape[1]
    nb = s_len // MOBA_BLOCK
    n_pick = min(MOBA_TOPK, nb)
    q = q_ref[0]
    k = k_ref[0]
    lane = lax.broadcasted_iota(jnp.int32, (1, LANES), 1)
    kmean = jnp.concatenate(
        [jnp.sum(k[n * MOBA_BLOCK:(n + 1) * MOBA_BLOCK], axis=0, keepdims=True) for n in range(nb)],
        axis=0) * (1.0 / MOBA_BLOCK)

    blk_row = lax.broadcasted_iota(jnp.int32, (nb, s_len), 0)
    q_blk = lax.broadcasted_iota(jnp.int32, (nb, s_len), 1) // MOBA_BLOCK
    past = blk_row < q_blk

    q_hi = q.astype(BF16)
    q_lo = (q - q_hi.astype(F32)).astype(BF16)
    for h in range(2):
        own = (lane >= h * HEAD_DIM) & (lane < (h + 1) * HEAD_DIM)
        off = (1 - h) * HEAD_DIM
        km = jnp.where(own, kmean, 0.0)
        km_hi = km.astype(BF16)
        km_split = jnp.concatenate([km_hi, (km - km_hi.astype(F32)).astype(BF16)], axis=0)
        parts = (lax.dot_general(km_split, q_hi, NT_DIMS, preferred_element_type=F32)
                 + lax.dot_general(km_split, q_lo, NT_DIMS, preferred_element_type=F32))
        gate = jnp.where(past, parts[0:nb] + parts[nb:2 * nb], -jnp.inf)
        rank = jnp.zeros((nb, s_len), jnp.int32)
        for m in range(nb):
            gm = gate[m:m + 1, :]
            ahead = (gm > gate) | ((gm == gate) & (m < blk_row))
            rank = rank + ahead.astype(jnp.int32)
        allowed = ((rank < n_pick) & past) | (blk_row == q_blk)
        bias = jnp.where(allowed, 0.0, NEG_BIG)
        pieces = []
        if off:
            pieces.append(jnp.zeros((off, s_len), F32))
        pieces.append(bias)
        if LANES - off - nb:
            pieces.append(jnp.zeros((LANES - off - nb, s_len), F32))
        bias_lanes = jnp.concatenate(pieces, axis=0).T
        qa_ref[h] = jnp.where(own, q * (HEAD_DIM ** -0.5 * LOG2_E), bias_lanes).astype(BF16)
        ka_ref[h] = jnp.where(own, k.astype(BF16), kpat_ref[h])

    own0 = lane < HEAD_DIM

    def scores(qi, h):
        nk = (qi + 1) * MOBA_BLOCK
        qa = qa_ref[h, qi * MOBA_BLOCK:(qi + 1) * MOBA_BLOCK, :]
        return lax.dot_general(qa, ka_ref[h, 0:nk, :], NT_DIMS, preferred_element_type=F32)

    key_in_blk = lax.broadcasted_iota(jnp.int32, (MOBA_BLOCK, MOBA_BLOCK), 1)
    q_in_blk = lax.broadcasted_iota(jnp.int32, (MOBA_BLOCK, MOBA_BLOCK), 0)
    causal = key_in_blk <= q_in_blk
    units = [(qi, h) for qi in range(nb) for h in range(2)]
    ahead = [scores(*u) for u in units[:MOBA_LOOKAHEAD]]
    outs = []

    def weighted_values(p, l, qi, h):
        outs.append(jnp.dot(p, v_ref[0, 0:(qi + 1) * MOBA_BLOCK, :], preferred_element_type=F32) / l)
        if h:
            o_ref[0, qi * MOBA_BLOCK:(qi + 1) * MOBA_BLOCK, :] = jnp.where(own0, outs[0], outs[1])
            outs.clear()

    pending = None
    for n, (qi, h) in enumerate(units):
        s = ahead.pop(0)
        if n + MOBA_LOOKAHEAD < len(units):
            ahead.append(scores(*units[n + MOBA_LOOKAHEAD]))
        n_past = qi * MOBA_BLOCK
        s_own = jnp.where(causal, s[:, n_past:], NEG_BIG)
        s = jnp.concatenate([s[:, :n_past], s_own], axis=1) if qi else s_own
        p = jnp.exp2(s - jnp.max(s, axis=-1, keepdims=True))
        l = jnp.sum(p, axis=-1, keepdims=True)
        if pending is not None:
            weighted_values(*pending)
        pending = (p.astype(BF16), l, qi, h)
    weighted_values(*pending)


def _key_block_pattern(s):
    pat = np.zeros((2, s, LANES), np.float32)
    for h in range(2):
        off = (1 - h) * HEAD_DIM
        pat[h, np.arange(s), off + np.arange(s) // MOBA_BLOCK] = 1.0
    return jnp.asarray(pat, BF16)


def _moba(q, k, v):
    b, s, c = q.shape
    spec = lambda: pl.BlockSpec((1, s, LANES), lambda bi, hi: (bi, 0, hi))
    return pl.pallas_call(
        _moba_kernel,
        grid=(b, c // LANES),
        in_specs=[spec(), spec(), spec(), pl.BlockSpec((2, s, LANES), lambda bi, hi: (0, 0, 0))],
        out_specs=spec(),
        out_shape=jax.ShapeDtypeStruct((b, s, c), F32),
        scratch_shapes=[pltpu.VMEM((2, s, LANES), BF16), pltpu.VMEM((2, s, LANES), BF16)],
        compiler_params=_params(("arbitrary", "arbitrary")),
        name="moba",
    )(q, k, v, _key_block_pattern(s))


def _post_kernel(x_ref, ylru_ref, yatt_ref, gatt_ref, woa_ref, wob_ref, gcross_ref, wxq_ref, kx_ref, vx_ref,
                 wxo_ref, gffn_ref, wr_ref, br_ref, tri_ref, elow_ref,
                 h2_ref, hn2_ref, eid_ref, gcol_ref, rank_ref, cnt_ref, pos_ref, tcnt_ref, carry_ref):
    tm = x_ref.shape[1]
    d = x_ref.shape[2]
    xd = d // N_XHEADS
    first = (pl.program_id(0) == 0) & (pl.program_id(1) == 0)

    @pl.when(first)
    def _():
        carry_ref[...] = jnp.zeros_like(carry_ref)

    ya = _rms(yatt_ref[0], gatt_ref[...]).astype(BF16)
    mix = (jnp.dot(ylru_ref[0], woa_ref[...], preferred_element_type=F32)
           + jnp.dot(ya, wob_ref[...], preferred_element_type=F32))
    h1 = x_ref[0] + mix

    hn = _rms(h1, gcross_ref[...]).astype(BF16)
    qx = jnp.dot(hn, wxq_ref[...], preferred_element_type=F32)
    heads = []
    for hh in range(N_XHEADS):
        qh = (qx[:, hh * xd:(hh + 1) * xd] * (xd ** -0.5)).astype(BF16)
        s = lax.dot_general(qh, kx_ref[0, :, hh * xd:(hh + 1) * xd], NT_DIMS, preferred_element_type=F32)
        p = jnp.exp(s - jnp.max(s, axis=-1, keepdims=True))
        l = jnp.sum(p, axis=-1, keepdims=True)
        o = jnp.dot(p.astype(BF16), vx_ref[0, :, hh * xd:(hh + 1) * xd], preferred_element_type=F32) / l
        heads.append(o.astype(BF16))
    h2 = h1 + jnp.dot(jnp.concatenate(heads, axis=1), wxo_ref[...], preferred_element_type=F32)
    hn2 = _rms(h2, gffn_ref[...])
    h2_ref[...] = h2
    hn2_ref[...] = hn2

    x_hi = hn2.astype(BF16)
    x_lo = (hn2 - x_hi.astype(F32)).astype(BF16)
    by_hi = lax.dot_general(wr_ref[...], x_hi, NT_DIMS, preferred_element_type=F32)
    by_lo = lax.dot_general(wr_ref[0:ROUTER_ROWS], x_lo, NT_DIMS, preferred_element_type=F32)
    logits = by_hi[0:ROUTER_ROWS] + by_hi[ROUTER_ROWS:2 * ROUTER_ROWS] + by_lo + br_ref[...]
    gl = logits[0:N_GROUPS]
    gmax = jnp.max(gl, axis=0, keepdims=True)
    gi = lax.broadcasted_iota(jnp.int32, gl.shape, 0)
    grp = jnp.min(jnp.where(gl == gmax, gi, N_GROUPS), axis=0, keepdims=True)
    g_w = 1.0 / jnp.sum(jnp.exp(gl - gmax), axis=0, keepdims=True)
    el = jnp.zeros((EXPERTS_PER_GROUP, tm), F32)
    for g in range(N_GROUPS):
        lo = SUBLANES + g * EXPERTS_PER_GROUP
        el = jnp.where(grp == g, logits[lo:lo + EXPERTS_PER_GROUP], el)
    ee = jnp.exp(el - jnp.max(el, axis=0, keepdims=True))
    ep = ee / jnp.sum(ee, axis=0, keepdims=True)
    ei = lax.broadcasted_iota(jnp.int32, ep.shape, 0)
    p1 = jnp.max(ep, axis=0, keepdims=True)
    i1 = jnp.min(jnp.where(ep == p1, ei, EXPERTS_PER_GROUP), axis=0, keepdims=True)
    ep_rest = jnp.where(ei == i1, -1.0, ep)
    p2 = jnp.max(ep_rest, axis=0, keepdims=True)
    i2 = jnp.min(jnp.where(ep_rest == p2, ei, EXPERTS_PER_GROUP), axis=0, keepdims=True)
    den = p1 + p2
    gate1 = g_w * p1 / den
    gate2 = g_w * p2 / den
    e1 = grp * EXPERTS_PER_GROUP + i1
    e2 = grp * EXPERTS_PER_GROUP + i2
    eid_ref[0:1, :] = e1
    eid_ref[1:2, :] = e2
    li = lax.broadcasted_iota(jnp.int32, (LANES, tm), 0)
    gcol_ref[...] = jnp.where(li == 0, gate1, jnp.where(li == 1, gate2, 0.0)).T

    xi = lax.broadcasted_iota(jnp.int32, (N_EXPERTS, tm), 0)
    oh1 = xi == e1
    oh2 = xi == e2
    cnt = oh1.astype(F32) + oh2.astype(F32)
    in_tile = jnp.dot(cnt.astype(BF16), tri_ref[...], preferred_element_type=F32)
    before = in_tile + carry_ref[:, 0:1]
    rank_ref[0:1, :] = jnp.sum(jnp.where(oh1, before, 0.0), axis=0, keepdims=True).astype(jnp.int32)
    rank_ref[1:2, :] = jnp.sum(jnp.where(oh2, before, 0.0), axis=0, keepdims=True).astype(jnp.int32)
    tile_cnt = jnp.broadcast_to(jnp.sum(cnt, axis=1, keepdims=True), (N_EXPERTS, LANES))
    carry_ref[...] = carry_ref[...] + tile_cnt
    cnt_ref[...] = carry_ref[...]
    tcnt_ref[...] = tile_cnt
    tile_start = jnp.dot(elow_ref[...], tile_cnt, precision=lax.Precision.HIGHEST,
                         preferred_element_type=F32)[:, 0:1]
    local = in_tile + tile_start
    pos_ref[0:1, :] = jnp.sum(jnp.where(oh1, local, 0.0), axis=0, keepdims=True).astype(jnp.int32)
    pos_ref[1:2, :] = jnp.sum(jnp.where(oh2, local, 0.0), axis=0, keepdims=True).astype(jnp.int32)


def _post(x, ylru, yatt, gatt, wo_a, wo_b, gcross, wxq, kx, vx, wxo, gffn, wr_t, br_t, tri, tm):
    b, s, d = x.shape
    c = ylru.shape[-1]
    m = kx.shape[1]
    n = b * s
    nt = s // tm
    full = lambda shape: pl.BlockSpec(shape, lambda bi, si: (0,) * len(shape))
    tok = lambda rows: pl.BlockSpec((rows, tm), lambda bi, si: (0, bi * nt + si))
    return pl.pallas_call(
        _post_kernel,
        grid=(b, nt),
        in_specs=[pl.BlockSpec((1, tm, d), lambda bi, si: (bi, si, 0)),
                  pl.BlockSpec((1, tm, c), lambda bi, si: (bi, si, 0)),
                  pl.BlockSpec((1, tm, c), lambda bi, si: (bi, si, 0)),
                  full((1, c)), full((c, d)), full((c, d)), full((1, d)), full((d, d)),
                  pl.BlockSpec((1, m, d), lambda bi, si: (bi, 0, 0)),
                  pl.BlockSpec((1, m, d), lambda bi, si: (bi, 0, 0)),
                  full((d, d)), full((1, d)), full(wr_t.shape), full(br_t.shape), full((tm, tm)),
                  full((N_EXPERTS, N_EXPERTS))],
        out_specs=[pl.BlockSpec((tm, d), lambda bi, si: (bi * nt + si, 0)),
                   pl.BlockSpec((tm, d), lambda bi, si: (bi * nt + si, 0)),
                   tok(EXPERT_TOPK),
                   pl.BlockSpec((tm, LANES), lambda bi, si: (bi * nt + si, 0)),
                   tok(EXPERT_TOPK), full((N_EXPERTS, LANES)),
                   tok(EXPERT_TOPK),
                   pl.BlockSpec((N_EXPERTS, LANES), lambda bi, si: (bi * nt + si, 0))],
        out_shape=[jax.ShapeDtypeStruct((n, d), F32),
                   jax.ShapeDtypeStruct((n, d), F32),
                   jax.ShapeDtypeStruct((EXPERT_TOPK, n), jnp.int32),
                   jax.ShapeDtypeStruct((n, LANES), F32),
                   jax.ShapeDtypeStruct((EXPERT_TOPK, n), jnp.int32),
                   jax.ShapeDtypeStruct((N_EXPERTS, LANES), F32),
                   jax.ShapeDtypeStruct((EXPERT_TOPK, n), jnp.int32),
                   jax.ShapeDtypeStruct((n // tm * N_EXPERTS, LANES), F32)],
        scratch_shapes=[pltpu.VMEM((N_EXPERTS, LANES), F32)],
        compiler_params=_params(("arbitrary", "arbitrary")),
        name="post",
    )(x, ylru, yatt, gatt, wo_a, wo_b, gcross, wxq, kx, vx, wxo, gffn, wr_t, br_t, tri,
      jnp.asarray(np.tril(np.ones((N_EXPERTS, N_EXPERTS), np.float32), -1)))


def _dest_kernel(ps_ref, eid_ref, rank_ref, dest_ref):
    eid = eid_ref[...]
    dest = rank_ref[...]
    for e in range(N_EXPERTS):
        dest = dest + jnp.where(eid == e, ps_ref[e], 0)
    dest_ref[...] = dest


def _dest(pad_starts, eid, rank):
    shape = (eid.size // LANES, LANES)
    spec = pl.BlockSpec(shape, lambda i, ps: (0, 0))
    return pl.pallas_call(
        _dest_kernel,
        grid_spec=pltpu.PrefetchScalarGridSpec(num_scalar_prefetch=1, grid=(1,), in_specs=[spec, spec],
                                               out_specs=spec),
        out_shape=jax.ShapeDtypeStruct(shape, jnp.int32),
        compiler_params=_params(("arbitrary",)),
        name="dest",
    )(pad_starts, eid.reshape(shape), rank.reshape(shape)).reshape(-1)


def _pieces(largest):
    piece = largest
    while piece:
        yield piece
        piece //= 2


def _tile_rows(row, count):
    return pl.ds(pl.multiple_of(row * SUBLANES, SUBLANES), count * SUBLANES)


def _dispatch_kernel(dest_ref, fill_ref, rstart_ref, rlen_ref, hn_ref, pos_ref, xout_ref, inv_ref,
                     sorted_ref, zero_ref, sem, zsem):
    tm = hn_ref.shape[0]
    n = dest_ref.shape[0] // EXPERT_TOPK
    n_sorted = EXPERT_TOPK * tm
    step = pl.program_id(0)
    base = step * tm

    def dump_rows(first, end):
        def group(g, carry):
            for i in range(SUBLANES):
                row = g * SUBLANES + i
                inv_ref[row] = EXPERT_TOPK * n + (row & (EXPERT_ROWS - 1))
            return carry

        lax.fori_loop(first // SUBLANES, end // SUBLANES, group, 0)

    zero_rows = zero_ref.shape[0] // SUBLANES
    n_rows = xout_ref.shape[0] // SUBLANES

    def pad_copies(e, start):
        first = fill_ref[e]
        length = fill_ref[N_EXPERTS + e]
        if start:
            dump_rows(first, first + length)
        for piece in _pieces(zero_rows):
            offset = first + (length & ~(2 * piece - 1))

            @pl.when((length & piece) != 0)
            def _():
                cp = pltpu.make_async_copy(zero_ref.at[_tile_rows(0, piece)], xout_ref.at[_tile_rows(offset, piece)],
                                           zsem)
                cp.start() if start else cp.wait()

    def tail_copies(start):
        def one(i, carry):
            cp = pltpu.make_async_copy(zero_ref, xout_ref.at[_tile_rows(i * zero_rows, zero_rows)], zsem)
            cp.start() if start else cp.wait()
            return carry

        lax.fori_loop(fill_ref[2 * N_EXPERTS] // zero_rows, n_rows // zero_rows, one, 0)
        if start:
            dump_rows(fill_ref[2 * N_EXPERTS], n_rows)

    @pl.when(step == 0)
    def _():
        zero_ref[...] = jnp.zeros_like(zero_ref)
        lax.fori_loop(0, N_EXPERTS, lambda e, c: (pad_copies(e, True), c)[1], 0)
        tail_copies(True)

    pos = pos_ref[...]
    p_iota = lax.broadcasted_iota(jnp.int32, (n_sorted, tm), 0)
    picks = jnp.where((p_iota == pos[0:1, :]) | (p_iota == pos[1:2, :]), 1.0, 0.0).astype(BF16)
    for slot in range(EXPERT_TOPK):
        for r in range(tm):
            src = slot * n + base + r
            inv_ref[dest_ref[src]] = src
    xs = jnp.dot(picks, hn_ref[...].astype(BF16), preferred_element_type=F32)
    for k in range(SUBLANES):
        sorted_ref[pl.ds(k, n_sorted, stride=SUBLANES), :] = xs[:, k * LANES:(k + 1) * LANES]

    def run(e, offset):
        length = rlen_ref[step * N_EXPERTS + e]
        first = rstart_ref[step * N_EXPERTS + e]

        def copy_pieces(pieces):
            for piece in pieces:
                done = length & ~(2 * piece - 1)

                @pl.when((length & piece) != 0)
                def _():
                    pltpu.make_async_copy(sorted_ref.at[_tile_rows(offset + done, piece)],
                                          xout_ref.at[_tile_rows(first + done, piece)], sem).start()

        big = [p for p in _pieces(tm) if p >= RUN_SPLIT]

        @pl.when(length >= RUN_SPLIT)
        def _():
            copy_pieces(big)

        copy_pieces([p for p in _pieces(tm) if p < RUN_SPLIT])
        return offset + length

    lax.fori_loop(0, N_EXPERTS, run, 0)

    pltpu.make_async_copy(sorted_ref, xout_ref.at[_tile_rows(0, n_sorted)], sem).wait()

    @pl.when(step == 0)
    def _():
        lax.fori_loop(0, N_EXPERTS, lambda e, c: (pad_copies(e, False), c)[1], 0)
        tail_copies(False)


def _dispatch(dest_flat, fill, run_start, run_len, hn2, pos, n_rows, tm):
    n, d = hn2.shape
    assert d == SUBLANES * LANES, "a row must be exactly one (8, 128) tile for the tile-contiguous layout"
    return pl.pallas_call(
        _dispatch_kernel,
        grid_spec=pltpu.PrefetchScalarGridSpec(
            num_scalar_prefetch=4,
            grid=(n // tm,),
            in_specs=[pl.BlockSpec((tm, d), lambda i, *_: (i, 0)),
                      pl.BlockSpec((EXPERT_TOPK, tm), lambda i, *_: (0, i))],
            out_specs=[pl.BlockSpec(memory_space=pl.ANY), pl.BlockSpec(memory_space=pltpu.SMEM)],
            scratch_shapes=[pltpu.VMEM((EXPERT_TOPK * tm * SUBLANES, LANES), F32),
                            pltpu.VMEM((EXPERT_ROWS // 2 * SUBLANES, LANES), F32),
                            pltpu.SemaphoreType.DMA(()), pltpu.SemaphoreType.DMA(())]),
        out_shape=[jax.ShapeDtypeStruct((n_rows * SUBLANES, LANES), F32),
                   jax.ShapeDtypeStruct((n_rows,), jnp.int32)],
        compiler_params=_params(("arbitrary",)),
        name="dispatch",
    )(dest_flat, fill, run_start, run_len, hn2, pos)


def _experts_kernel(be_ref, nu_ref, inv_ref, x_ref, wgu_ref, wd_ref, ytok_ref, wgu_bf_ref, wd_bf_ref, ybuf_ref, sems):
    j = pl.program_id(0)
    rows = x_ref.shape[0] // SUBLANES
    f = wd_ref.shape[1]
    n_used = nu_ref[0]
    par = j % 2
    last_blk = be_ref.shape[0] - 1

    pieces = SUBLANES

    def scatter_copy(blk_par, r, dst_row):
        dst = pl.multiple_of(dst_row * pieces, pieces)
        return pltpu.make_async_copy(ybuf_ref.at[blk_par, pl.ds(r * pieces, pieces)],
                                     ytok_ref.at[pl.ds(dst, pieces)], sems.at[blk_par])

    chunk = rows // EXPERT_CHUNKS

    def start_scatter_of_previous(part):
        base = (j - 1) * rows
        for r in range(part * chunk, (part + 1) * chunk):
            scatter_copy(1 - par, r, inv_ref[base + r]).start()

    def wait_scatter(blk_par):
        pltpu.make_async_copy(ybuf_ref.at[blk_par], ytok_ref.at[pl.ds(0, rows * pieces)], sems.at[blk_par]).wait()

    def mlp(part):
        x = jnp.concatenate([x_ref[pl.ds(part * chunk * pieces + k, chunk, stride=pieces), :] for k in range(pieces)],
                            axis=1)
        gu = jnp.dot(x.astype(BF16), wgu_bf_ref[...], preferred_element_type=F32)
        act = jax.nn.silu(gu[:, 0:f]) * gu[:, f:2 * f]
        y = jnp.dot(act.astype(BF16), wd_bf_ref[...], preferred_element_type=F32)
        for k in range(pieces):
            ybuf_ref[par, pl.ds(part * chunk * pieces + k, chunk, stride=pieces), :] = y[:, k * LANES:(k + 1) * LANES]

    jc = jnp.minimum(j, last_blk)
    @pl.when((j < n_used) & ((j == 0) | (be_ref[jc] != be_ref[jnp.maximum(jc, 1) - 1])))
    def _():
        wgu_bf_ref[...] = wgu_ref[0].astype(BF16)
        wd_bf_ref[...] = wd_ref[0].astype(BF16)

    @pl.when((j >= 2) & (j <= n_used))
    def _():
        wait_scatter(par)

    @pl.when(j == 0)
    def _():
        ybuf_ref[1] = jnp.zeros(ybuf_ref.shape[1:], F32)
        cp = pltpu.make_async_copy(ybuf_ref.at[1], ytok_ref.at[pl.ds(ytok_ref.shape[0] - rows * pieces, rows * pieces)],
                                   sems.at[1])
        cp.start()
        cp.wait()

    @pl.when((j == 0) & (j < n_used))
    def _():
        for part in range(EXPERT_CHUNKS):
            mlp(part)

    @pl.when((j >= 1) & (j < n_used))
    def _():
        for part in range(EXPERT_CHUNKS):
            start_scatter_of_previous(part)
            mlp(part)

    @pl.when((j >= 1) & (j == n_used))
    def _():
        for part in range(EXPERT_CHUNKS):
            start_scatter_of_previous(part)
        wait_scatter(1 - par)


def _experts(blk_expert, n_used, inv, x_buf, w_gate_up, w_down, n_out_rows):
    d = w_gate_up.shape[1]
    f = w_down.shape[1]
    r = EXPERT_ROWS
    n_blk = x_buf.shape[0] // (r * SUBLANES)
    assert d == SUBLANES * LANES and x_buf.shape[1] == LANES, "rows are (8, 128) tiles (tile-contiguous layout)"
    x_map = lambda j, be, nu, inv: (jnp.minimum(j, jnp.maximum(nu[0], 1) - 1), 0)
    w_map = lambda j, be, nu, inv: (be[jnp.minimum(j, n_blk - 1)], 0, 0)
    return pl.pallas_call(
        _experts_kernel,
        grid_spec=pltpu.PrefetchScalarGridSpec(
            num_scalar_prefetch=3,
            grid=(n_blk + 1,),
            in_specs=[pl.BlockSpec((r * SUBLANES, LANES), x_map),
                      pl.BlockSpec((1, d, 2 * f), w_map),
                      pl.BlockSpec((1, f, d), w_map)],
            out_specs=pl.BlockSpec(memory_space=pl.ANY),
            scratch_shapes=[pltpu.VMEM((d, 2 * f), BF16), pltpu.VMEM((f, d), BF16),
                            pltpu.VMEM((2, r * SUBLANES, LANES), F32), pltpu.SemaphoreType.DMA((2,))]),
        out_shape=jax.ShapeDtypeStruct((n_out_rows * SUBLANES, LANES), F32),
        compiler_params=_params(("arbitrary",)),
        name="experts",
    )(blk_expert, n_used, inv, x_buf, w_gate_up, w_down)


def _combine_kernel(h2_ref, gcol_ref, gfin_ref, y0_ref, y1_ref, o_ref):
    tc = h2_ref.shape[0]

    def rows_of(y_ref):
        return jnp.concatenate([y_ref[pl.ds(k, tc, stride=SUBLANES), :] for k in range(SUBLANES)], axis=1)

    g = gcol_ref[...]
    moe = g[:, 0:1] * rows_of(y0_ref) + g[:, 1:2] * rows_of(y1_ref)
    o_ref[...] = _rms(h2_ref[...] + moe, gfin_ref[...])


def _combine(h2, gcol, gfin, y_tok, tc):
    n, d = h2.shape
    nt = n // tc
    return pl.pallas_call(
        _combine_kernel,
        grid=(nt,),
        in_specs=[pl.BlockSpec((tc, d), lambda i: (i, 0)),
                  pl.BlockSpec((tc, LANES), lambda i: (i, 0)),
                  pl.BlockSpec((1, d), lambda i: (0, 0)),
                  pl.BlockSpec((tc * SUBLANES, LANES), lambda i: (i, 0)),
                  pl.BlockSpec((tc * SUBLANES, LANES), lambda i: (i + nt, 0))],
        out_specs=pl.BlockSpec((tc, d), lambda i: (i, 0)),
        out_shape=jax.ShapeDtypeStruct((n, d), F32),
        compiler_params=_params(("arbitrary",)),
        name="combine",
    )(h2, gcol, gfin, y_tok, y_tok)


def _rope_tables(s):
    half = ROT_DIM // 2
    inv_freq = (ROPE_THETA ** (-np.arange(0, ROT_DIM, 2, dtype=np.float32) / ROT_DIM)).astype(np.float32)
    ang = np.arange(s, dtype=np.float32)[:, None] * inv_freq[None, :]
    cos, sin = np.cos(ang), np.sin(ang)
    zeros = lambda w: np.zeros((s, w), np.float32)
    cos_h = np.concatenate([cos, cos, np.ones((s, HEAD_DIM - ROT_DIM), np.float32)], axis=1)
    sina_h = np.concatenate([-sin, zeros(HEAD_DIM - half)], axis=1)
    sinb_h = np.concatenate([zeros(half), sin, zeros(HEAD_DIM - ROT_DIM)], axis=1)
    tile = lambda t: jnp.asarray(np.tile(t, (1, LANES // HEAD_DIM)), F32)
    return tile(cos_h), tile(sina_h), tile(sinb_h)


def _block_diag(w):
    nblk, bi, bo = w.shape
    eye = jnp.eye(nblk, dtype=w.dtype)
    return jnp.einsum('hij,hg->higj', w, eye).reshape(nblk * bi, nblk * bo)


def _layer(h, mem, norm_mix, w_in, conv_w, conv_b, w_rg, b_rg, w_ig, b_ig, lru_lambda, norm_lru_out,
           norm_attn_out, w_out, norm_cross, norm_mem, w_xq, w_xkv, w_xo, norm_ffn, w_router_group,
           b_router_group, w_router_expert, b_router_expert, w_gate_up, w_down, norm_out):
    b, s, d = h.shape
    c = conv_w.shape[-1]
    n = b * s
    row = lambda t: t.reshape(1, -1)
    tm = 512 if s % 512 == 0 else MOBA_BLOCK
    ts = tm
    assert s % ts == 0 and s % tm == 0 and c == N_HEADS * HEAD_DIM

    kx, vx = _memkv(mem, row(norm_mem), w_xkv.astype(BF16))

    w_gates = jnp.concatenate([_block_diag(w_rg), _block_diag(w_ig)], axis=1).astype(BF16)
    b_gates = jnp.concatenate([b_rg, b_ig]).reshape(1, -1)
    cos_t, sina_t, sinb_t = _rope_tables(s)
    ylru, q, k, v = _mixer_in(h, row(norm_mix), w_in.astype(BF16), conv_w, row(conv_b), w_gates, b_gates,
                              row(lru_lambda), row(norm_lru_out), cos_t, sina_t, sinb_t, ts)
    yatt = _moba(q, k, v)

    w_out_b = w_out.astype(BF16)
    wr_t = jnp.zeros((ROUTER_ROWS, d), F32)
    wr_t = wr_t.at[0:N_GROUPS].set(w_router_group.T).at[SUBLANES:SUBLANES + N_EXPERTS].set(w_router_expert.T)
    wr_hi = wr_t.astype(BF16)
    wr_t = jnp.concatenate([wr_hi, (wr_t - wr_hi.astype(F32)).astype(BF16)], axis=0)
    br_t = jnp.zeros((ROUTER_ROWS, 1), F32)
    br_t = br_t.at[0:N_GROUPS, 0].set(b_router_group).at[SUBLANES:SUBLANES + N_EXPERTS, 0].set(b_router_expert)
    tri = jnp.asarray(np.triu(np.ones((tm, tm), np.float32), 1), BF16)
    h2, hn2, eid, gcol, rank, counts, pos, tile_cnt = _post(
        h, ylru, yatt, row(norm_attn_out), w_out_b[:c], w_out_b[c:], row(norm_cross), w_xq.astype(BF16),
        kx, vx, w_xo.astype(BF16), row(norm_ffn), wr_t, br_t, tri, tm)

    counts = counts[:, 0].astype(jnp.int32)
    padded = (counts + EXPERT_ROWS - 1) // EXPERT_ROWS * EXPERT_ROWS
    pad_ends = jnp.cumsum(padded)
    pad_starts = pad_ends - padded
    dest_flat = _dest(pad_starts.astype(jnp.int32), eid, rank)
    n_blocks = n * EXPERT_TOPK // EXPERT_ROWS + N_EXPERTS
    blk_first = jnp.arange(n_blocks, dtype=jnp.int32) * EXPERT_ROWS
    blk_expert = jnp.minimum(jnp.sum(blk_first[:, None] >= pad_ends[None, :], axis=1), N_EXPERTS - 1).astype(jnp.int32)
    n_used = (pad_ends[-1:] // EXPERT_ROWS).astype(jnp.int32)
    fill = jnp.concatenate([pad_starts + counts, padded - counts, pad_ends[-1:]]).astype(jnp.int32)

    tile_cnt = tile_cnt[:, 0].astype(jnp.int32).reshape(n // tm, N_EXPERTS)
    run_start = (pad_starts[None, :] + jnp.cumsum(tile_cnt, axis=0) - tile_cnt).astype(jnp.int32).reshape(-1)
    x_buf, inv = _dispatch(dest_flat, fill, run_start, tile_cnt.reshape(-1), hn2, pos, n_blocks * EXPERT_ROWS, tm)
    y_tok = _experts(blk_expert, n_used, inv, x_buf, w_gate_up, w_down, EXPERT_TOPK * n + EXPERT_ROWS)
    out = _combine(h2, gcol, row(norm_out), y_tok, tm)
    return out.reshape(b, s, d)


def kernel(x, mem, norm_mix, w_in, conv_w, conv_b, w_rg, b_rg, w_ig, b_ig, lru_lambda, norm_lru_out, norm_attn_out,
           w_out, norm_cross, norm_mem, w_xq, w_xkv, w_xo, norm_ffn, w_router_group, b_router_group,
           w_router_expert, b_router_expert, w_gate_up, w_down, norm_final):
    depth = norm_mix.shape[0]
    assert depth == 1, "the fused final norm assumes a single layer"
    l = 0
    return _layer(x, mem, norm_mix[l], w_in[l], conv_w[l], conv_b[l], w_rg[l], b_rg[l], w_ig[l], b_ig[l],
                  lru_lambda[l], norm_lru_out[l], norm_attn_out[l], w_out[l], norm_cross[l], norm_mem[l],
                  w_xq[l], w_xkv[l], w_xo[l], norm_ffn[l], w_router_group[l], b_router_group[l],
                  w_router_expert[l], b_router_expert[l], w_gate_up[l], w_down[l], norm_final)
```

```python
import jax
import jax.numpy as jnp
import numpy as np
from jax import lax
from jax.experimental import pallas as pl
from jax.experimental.pallas import tpu as pltpu

F32 = jnp.float32
BF16 = jnp.bfloat16

CONV_WIDTH = 4
RG_C = 8.0
N_HEADS = 8
HEAD_DIM = 64
ROT_DIM = HEAD_DIM // 4
ROPE_THETA = 500000.0
MOBA_BLOCK = 256
MOBA_TOPK = 3
N_XHEADS = 4
N_GROUPS = 4
EXPERTS_PER_GROUP = 8
N_EXPERTS = N_GROUPS * EXPERTS_PER_GROUP
EXPERT_TOPK = 2
EPS = 1e-6

LANES = 128
SUBLANES = 8
NEG_BIG = -1e30
LOG2_E = 1.4426950408889634
MOBA_LOOKAHEAD = 3
EXPERT_ROWS = 512
RUN_SPLIT = 64
EXPERT_CHUNKS = 4
MIXER_SUB_ROWS = 256
BF16_ROWS = 2 * SUBLANES
ROUTER_ROWS = -(-(SUBLANES + N_EXPERTS) // BF16_ROWS) * BF16_ROWS
VMEM_LIMIT = 56 * 1024 * 1024

NT_DIMS = (((1,), (1,)), ((), ()))


def _rms(x, g):
    return x * lax.rsqrt(jnp.mean(x * x, axis=-1, keepdims=True) + EPS) * g


def _params(sem):
    return pltpu.CompilerParams(dimension_semantics=sem, vmem_limit_bytes=VMEM_LIMIT)


def _memkv_kernel(mem_ref, g_ref, w_ref, k_ref, v_ref):
    d = mem_ref.shape[-1]
    mn = _rms(mem_ref[0], g_ref[...]).astype(BF16)
    kv = jnp.dot(mn, w_ref[...], preferred_element_type=F32)
    k_ref[0] = kv[:, :d].astype(BF16)
    v_ref[0] = kv[:, d:].astype(BF16)


def _memkv(mem, g, w_xkv):
    b, m, d = mem.shape
    return pl.pallas_call(
        _memkv_kernel,
        grid=(b,),
        in_specs=[pl.BlockSpec((1, m, d), lambda i: (i, 0, 0)),
                  pl.BlockSpec((1, d), lambda i: (0, 0)),
                  pl.BlockSpec((d, 2 * d), lambda i: (0, 0))],
        out_specs=[pl.BlockSpec((1, m, d), lambda i: (i, 0, 0)),
                   pl.BlockSpec((1, m, d), lambda i: (i, 0, 0))],
        out_shape=[jax.ShapeDtypeStruct((b, m, d), BF16)] * 2,
        compiler_params=_params(("arbitrary",)),
        name="memkv",
    )(mem, g, w_xkv)


def _mixer_in_kernel(x_ref, gmix_ref, win_ref, convw_ref, convb_ref, wg_ref, bg_ref, lam_ref, glru_ref,
                     cos_ref, sina_ref, sinb_ref,
                     ylru_ref, q_ref, k_ref, v_ref, xpad_ref, hcar_ref, ascan_ref, uscan_ref):
    c = ylru_ref.shape[-1]
    si = pl.program_id(1)

    @pl.when(si == 0)
    def _():
        xpad_ref[0:SUBLANES, :] = jnp.zeros((SUBLANES, c), F32)
        hcar_ref[...] = jnp.zeros_like(hcar_ref)

    n_sub = x_ref.shape[1] // ascan_ref.shape[1]
    ts = ascan_ref.shape[1]
    projs = []
    for sub in range(n_sub):
        xn = _rms(x_ref[0, sub * ts:(sub + 1) * ts, :], gmix_ref[...]).astype(BF16)
        projs.append(jnp.dot(xn, win_ref[...], preferred_element_type=F32))
    for sub in range(n_sub):
        _mixer_stage2(projs[sub], slice(sub * ts, (sub + 1) * ts), convw_ref, convb_ref, wg_ref, bg_ref, lam_ref,
                      glru_ref, cos_ref, sina_ref, sinb_ref, ylru_ref, q_ref, k_ref, v_ref, xpad_ref, hcar_ref,
                      ascan_ref, uscan_ref)


def _mixer_stage2(proj, rows_out, convw_ref, convb_ref, wg_ref, bg_ref, lam_ref, glru_ref, cos_ref, sina_ref,
                  sinb_ref, ylru_ref, q_ref, k_ref, v_ref, xpad_ref, hcar_ref, ascan_ref, uscan_ref):
    ts = proj.shape[0]
    c = ylru_ref.shape[-1]
    x_lru = proj[:, 0:c]
    g_lru = proj[:, c:2 * c]
    q = proj[:, 2 * c:3 * c]
    k = proj[:, 3 * c:4 * c]
    v = proj[:, 4 * c:5 * c]

    xpad_ref[SUBLANES:SUBLANES + ts, :] = x_lru
    cw = convw_ref[...]
    xc = convb_ref[...] + cw[3:4] * x_lru
    for j in range(1, CONV_WIDTH):
        xc = xc + cw[3 - j:4 - j] * xpad_ref[SUBLANES - j:SUBLANES - j + ts, :]
    xpad_ref[0:SUBLANES, :] = x_lru[ts - SUBLANES:ts, :]

    gates = jnp.dot(xc.astype(BF16), wg_ref[...], preferred_element_type=F32) + bg_ref[...]
    r = jax.nn.sigmoid(gates[:, 0:c])
    i = jax.nn.sigmoid(gates[:, c:2 * c])
    neg_lam = -lam_ref[...]
    softplus = jnp.maximum(neg_lam, 0.0) + jnp.log1p(jnp.exp(-jnp.abs(neg_lam)))
    log_a = -RG_C * r * softplus
    a = jnp.exp(log_a)
    u = jnp.sqrt(-jnp.tanh(log_a) * (a * a + 1.0)) * (i * xc)

    groups = ts // SUBLANES
    cols = c // LANES
    for j in range(cols):
        ascan_ref[j] = a[:, j * LANES:(j + 1) * LANES]
        uscan_ref[j] = u[:, j * LANES:(j + 1) * LANES]
    for r in range(SUBLANES):
        rows = pl.ds(r, groups, stride=SUBLANES)
        a_r = jnp.concatenate([ascan_ref[j, rows, :] for j in range(cols)], axis=1)
        u_r = jnp.concatenate([uscan_ref[j, rows, :] for j in range(cols)], axis=1)
        if r:
            u_r = a_r * u_acc + u_r
            a_r = a_r * a_acc
            for j in range(cols):
                ascan_ref[j, rows, :] = a_r[:, j * LANES:(j + 1) * LANES]
                uscan_ref[j, rows, :] = u_r[:, j * LANES:(j + 1) * LANES]
        a_acc, u_acc = a_r, u_r
    carry = hcar_ref[...]
    hs = []
    for gi in range(groups):
        blk = slice(gi * SUBLANES, (gi + 1) * SUBLANES)
        a_g = jnp.concatenate([ascan_ref[j, blk, :] for j in range(cols)], axis=1)
        u_g = jnp.concatenate([uscan_ref[j, blk, :] for j in range(cols)], axis=1)
        hs.append(a_g * carry + u_g)
        carry = a_acc[gi:gi + 1] * carry + u_acc[gi:gi + 1]
    h = jnp.concatenate(hs, axis=0)
    hcar_ref[...] = carry

    y = h * jax.nn.gelu(g_lru)
    ylru_ref[0, rows_out, :] = _rms(y, glru_ref[...]).astype(BF16)

    wide = lambda t_ref: jnp.concatenate([t_ref[rows_out, :]] * (c // LANES), axis=1)
    cos, sina, sinb = wide(cos_ref), wide(sina_ref), wide(sinb_ref)
    half = ROT_DIM // 2

    def rope(t):
        return t * cos + pltpu.roll(t, c - half, axis=1) * sina + pltpu.roll(t, half, axis=1) * sinb

    q_ref[0, rows_out, :] = rope(q)
    k_ref[0, rows_out, :] = rope(k)
    v_ref[0, rows_out, :] = v.astype(BF16)


def _mixer_in(x, gmix, w_in, conv_w, conv_b, w_gates, b_gates, lam, glru, cos_t, sina_t, sinb_t, ts):
    b, s, d = x.shape
    c = conv_w.shape[-1]
    sub = min(ts, MIXER_SUB_ROWS)
    full = lambda shape: pl.BlockSpec(shape, lambda bi, si: (0,) * len(shape))
    tab = pl.BlockSpec((ts, LANES), lambda bi, si: (si, 0))
    seq = lambda: pl.BlockSpec((1, ts, c), lambda bi, si: (bi, si, 0))
    return pl.pallas_call(
        _mixer_in_kernel,
        grid=(b, s // ts),
        in_specs=[pl.BlockSpec((1, ts, d), lambda bi, si: (bi, si, 0)),
                  full((1, d)), full(w_in.shape), full(conv_w.shape), full((1, c)),
                  full(w_gates.shape), full((1, 2 * c)), full((1, c)), full((1, c)),
                  tab, tab, tab],
        out_specs=[seq(), seq(), seq(), seq()],
        out_shape=[jax.ShapeDtypeStruct((b, s, c), BF16),
                   jax.ShapeDtypeStruct((b, s, c), F32),
                   jax.ShapeDtypeStruct((b, s, c), F32),
                   jax.ShapeDtypeStruct((b, s, c), BF16)],
        scratch_shapes=[pltpu.VMEM((sub + SUBLANES, c), F32), pltpu.VMEM((1, c), F32),
                        pltpu.VMEM((c // LANES, sub, LANES), F32), pltpu.VMEM((c // LANES, sub, LANES), F32)],
        compiler_params=_params(("arbitrary", "arbitrary")),
        name="mixer_in",
    )(x, gmix, w_in, conv_w, conv_b, w_gates, b_gates, lam, glru, cos_t, sina_t, sinb_t)


def _moba_kernel(q_ref, k_ref, v_ref, kpat_ref, o_ref, qa_ref, ka_ref):
    s_len = q_ref.shape[1]
    nb = s_len // MOBA_BLOCK
    n_pick = min(MOBA_TOPK, nb)
    q = q_ref[0]
    k = k_ref[0]
    lane = lax.broadcasted_iota(jnp.int32, (1, LANES), 1)
    kmean = jnp.concatenate(
        [jnp.sum(k[n * MOBA_BLOCK:(n + 1) * MOBA_BLOCK], axis=0, keepdims=True) for n in range(nb)],
        axis=0) * (1.0 / MOBA_BLOCK)

    blk_row = lax.broadcasted_iota(jnp.int32, (nb, s_len), 0)
    q_blk = lax.broadcasted_iota(jnp.int32, (nb, s_len), 1) // MOBA_BLOCK
    past = blk_row < q_blk

    q_hi = q.astype(BF16)
    q_lo = (q - q_hi.astype(F32)).astype(BF16)
    for h in range(2):
        own = (lane >= h * HEAD_DIM) & (lane < (h + 1) * HEAD_DIM)
        off = (1 - h) * HEAD_DIM
        km = jnp.where(own, kmean, 0.0)
        km_hi = km.astype(BF16)
        km_split = jnp.concatenate([km_hi, (km - km_hi.astype(F32)).astype(BF16)], axis=0)
        parts = (lax.dot_general(km_split, q_hi, NT_DIMS, preferred_element_type=F32)
                 + lax.dot_general(km_split, q_lo, NT_DIMS, preferred_element_type=F32))
        gate = jnp.where(past, parts[0:nb] + parts[nb:2 * nb], -jnp.inf)
        rank = jnp.zeros((nb, s_len), jnp.int32)
        for m in range(nb):
            gm = gate[m:m + 1, :]
            ahead = (gm > gate) | ((gm == gate) & (m < blk_row))
            rank = rank + ahead.astype(jnp.int32)
        allowed = ((rank < n_pick) & past) | (blk_row == q_blk)
        bias = jnp.where(allowed, 0.0, NEG_BIG)
        pieces = []
        if off:
            pieces.append(jnp.zeros((off, s_len), F32))
        pieces.append(bias)
        if LANES - off - nb:
            pieces.append(jnp.zeros((LANES - off - nb, s_len), F32))
        bias_lanes = jnp.concatenate(pieces, axis=0).T
        qa_ref[h] = jnp.where(own, q * (HEAD_DIM ** -0.5 * LOG2_E), bias_lanes).astype(BF16)
        ka_ref[h] = jnp.where(own, k.astype(BF16), kpat_ref[h])

    own0 = lane < HEAD_DIM

    def scores(qi, h):
        nk = (qi + 1) * MOBA_BLOCK
        qa = qa_ref[h, qi * MOBA_BLOCK:(qi + 1) * MOBA_BLOCK, :]
        return lax.dot_general(qa, ka_ref[h, 0:nk, :], NT_DIMS, preferred_element_type=F32)

    key_in_blk = lax.broadcasted_iota(jnp.int32, (MOBA_BLOCK, MOBA_BLOCK), 1)
    q_in_blk = lax.broadcasted_iota(jnp.int32, (MOBA_BLOCK, MOBA_BLOCK), 0)
    causal = key_in_blk <= q_in_blk
    units = [(qi, h) for qi in range(nb) for h in range(2)]
    ahead = [scores(*u) for u in units[:MOBA_LOOKAHEAD]]
    outs = []

    def weighted_values(p, l, qi, h):
        outs.append(jnp.dot(p, v_ref[0, 0:(qi + 1) * MOBA_BLOCK, :], preferred_element_type=F32) / l)
        if h:
            o_ref[0, qi * MOBA_BLOCK:(qi + 1) * MOBA_BLOCK, :] = jnp.where(own0, outs[0], outs[1])
            outs.clear()

    pending = None
    for n, (qi, h) in enumerate(units):
        s = ahead.pop(0)
        if n + MOBA_LOOKAHEAD < len(units):
            ahead.append(scores(*units[n + MOBA_LOOKAHEAD]))
        n_past = qi * MOBA_BLOCK
        s_own = jnp.where(causal, s[:, n_past:], NEG_BIG)
        s = jnp.concatenate([s[:, :n_past], s_own], axis=1) if qi else s_own
        p = jnp.exp2(s - jnp.max(s, axis=-1, keepdims=True))
        l = jnp.sum(p, axis=-1, keepdims=True)
        if pending is not None:
            weighted_values(*pending)
        pending = (p.astype(BF16), l, qi, h)
    weighted_values(*pending)


def _key_block_pattern(s):
    pat = np.zeros((2, s, LANES), np.float32)
    for h in range(2):
        off = (1 - h) * HEAD_DIM
        pat[h, np.arange(s), off + np.arange(s) // MOBA_BLOCK] = 1.0
    return jnp.asarray(pat, BF16)


def _moba(q, k, v):
    b, s, c = q.shape
    spec = lambda: pl.BlockSpec((1, s, LANES), lambda bi, hi: (bi, 0, hi))
    return pl.pallas_call(
        _moba_kernel,
        grid=(b, c // LANES),
        in_specs=[spec(), spec(), spec(), pl.BlockSpec((2, s, LANES), lambda bi, hi: (0, 0, 0))],
        out_specs=spec(),
        out_shape=jax.ShapeDtypeStruct((b, s, c), F32),
        scratch_shapes=[pltpu.VMEM((2, s, LANES), BF16), pltpu.VMEM((2, s, LANES), BF16)],
        compiler_params=_params(("arbitrary", "arbitrary")),
        name="moba",
    )(q, k, v, _key_block_pattern(s))


def _post_kernel(x_ref, ylru_ref, yatt_ref, gatt_ref, woa_ref, wob_ref, gcross_ref, wxq_ref, kx_ref, vx_ref,
                 wxo_ref, gffn_ref, wr_ref, br_ref, tri_ref, elow_ref,
                 h2_ref, hn2_ref, eid_ref, gcol_ref, rank_ref, cnt_ref, pos_ref, tcnt_ref, carry_ref):
    tm = x_ref.shape[1]
    d = x_ref.shape[2]
    xd = d // N_XHEADS
    first = (pl.program_id(0) == 0) & (pl.program_id(1) == 0)

    @pl.when(first)
    def _():
        carry_ref[...] = jnp.zeros_like(carry_ref)

    ya = _rms(yatt_ref[0], gatt_ref[...]).astype(BF16)
    mix = (jnp.dot(ylru_ref[0], woa_ref[...], preferred_element_type=F32)
           + jnp.dot(ya, wob_ref[...], preferred_element_type=F32))
    h1 = x_ref[0] + mix

    hn = _rms(h1, gcross_ref[...]).astype(BF16)
    qx = jnp.dot(hn, wxq_ref[...], preferred_element_type=F32)
    heads = []
    for hh in range(N_XHEADS):
        qh = (qx[:, hh * xd:(hh + 1) * xd] * (xd ** -0.5)).astype(BF16)
        s = lax.dot_general(qh, kx_ref[0, :, hh * xd:(hh + 1) * xd], NT_DIMS, preferred_element_type=F32)
        p = jnp.exp(s - jnp.max(s, axis=-1, keepdims=True))
        l = jnp.sum(p, axis=-1, keepdims=True)
        o = jnp.dot(p.astype(BF16), vx_ref[0, :, hh * xd:(hh + 1) * xd], preferred_element_type=F32) / l
        heads.append(o.astype(BF16))
    h2 = h1 + jnp.dot(jnp.concatenate(heads, axis=1), wxo_ref[...], preferred_element_type=F32)
    hn2 = _rms(h2, gffn_ref[...])
    h2_ref[...] = h2
    hn2_ref[...] = hn2.astype(BF16)

    x_hi = hn2.astype(BF16)
    x_lo = (hn2 - x_hi.astype(F32)).astype(BF16)
    by_hi = lax.dot_general(wr_ref[...], x_hi, NT_DIMS, preferred_element_type=F32)
    by_lo = lax.dot_general(wr_ref[0:ROUTER_ROWS], x_lo, NT_DIMS, preferred_element_type=F32)
    logits = by_hi[0:ROUTER_ROWS] + by_hi[ROUTER_ROWS:2 * ROUTER_ROWS] + by_lo + br_ref[...]
    gl = logits[0:N_GROUPS]
    gmax = jnp.max(gl, axis=0, keepdims=True)
    gi = lax.broadcasted_iota(jnp.int32, gl.shape, 0)
    grp = jnp.min(jnp.where(gl == gmax, gi, N_GROUPS), axis=0, keepdims=True)
    g_w = 1.0 / jnp.sum(jnp.exp(gl - gmax), axis=0, keepdims=True)
    el = jnp.zeros((EXPERTS_PER_GROUP, tm), F32)
    for g in range(N_GROUPS):
        lo = SUBLANES + g * EXPERTS_PER_GROUP
        el = jnp.where(grp == g, logits[lo:lo + EXPERTS_PER_GROUP], el)
    ee = jnp.exp(el - jnp.max(el, axis=0, keepdims=True))
    ep = ee / jnp.sum(ee, axis=0, keepdims=True)
    ei = lax.broadcasted_iota(jnp.int32, ep.shape, 0)
    p1 = jnp.max(ep, axis=0, keepdims=True)
    i1 = jnp.min(jnp.where(ep == p1, ei, EXPERTS_PER_GROUP), axis=0, keepdims=True)
    ep_rest = jnp.where(ei == i1, -1.0, ep)
    p2 = jnp.max(ep_rest, axis=0, keepdims=True)
    i2 = jnp.min(jnp.where(ep_rest == p2, ei, EXPERTS_PER_GROUP), axis=0, keepdims=True)
    den = p1 + p2
    gate1 = g_w * p1 / den
    gate2 = g_w * p2 / den
    e1 = grp * EXPERTS_PER_GROUP + i1
    e2 = grp * EXPERTS_PER_GROUP + i2
    eid_ref[0:1, :] = e1
    eid_ref[1:2, :] = e2
    li = lax.broadcasted_iota(jnp.int32, (LANES, tm), 0)
    gcol_ref[...] = jnp.where(li == 0, gate1, jnp.where(li == 1, gate2, 0.0)).T

    xi = lax.broadcasted_iota(jnp.int32, (N_EXPERTS, tm), 0)
    oh1 = xi == e1
    oh2 = xi == e2
    cnt = oh1.astype(F32) + oh2.astype(F32)
    in_tile = jnp.dot(cnt.astype(BF16), tri_ref[...], preferred_element_type=F32)
    before = in_tile + carry_ref[:, 0:1]
    rank_ref[0:1, :] = jnp.sum(jnp.where(oh1, before, 0.0), axis=0, keepdims=True).astype(jnp.int32)
    rank_ref[1:2, :] = jnp.sum(jnp.where(oh2, before, 0.0), axis=0, keepdims=True).astype(jnp.int32)
    tile_cnt = jnp.broadcast_to(jnp.sum(cnt, axis=1, keepdims=True), (N_EXPERTS, LANES))
    carry_ref[...] = carry_ref[...] + tile_cnt
    cnt_ref[...] = carry_ref[...]
    tcnt_ref[...] = tile_cnt
    tile_start = jnp.dot(elow_ref[...], tile_cnt, precision=lax.Precision.HIGHEST,
                         preferred_element_type=F32)[:, 0:1]
    local = in_tile + tile_start
    pos_ref[0:1, :] = jnp.sum(jnp.where(oh1, local, 0.0), axis=0, keepdims=True).astype(jnp.int32)
    pos_ref[1:2, :] = jnp.sum(jnp.where(oh2, local, 0.0), axis=0, keepdims=True).astype(jnp.int32)


def _post(x, ylru, yatt, gatt, wo_a, wo_b, gcross, wxq, kx, vx, wxo, gffn, wr_t, br_t, tri, tm):
    b, s, d = x.shape
    c = ylru.shape[-1]
    m = kx.shape[1]
    n = b * s
    nt = s // tm
    full = lambda shape: pl.BlockSpec(shape, lambda bi, si: (0,) * len(shape))
    tok = lambda rows: pl.BlockSpec((rows, tm), lambda bi, si: (0, bi * nt + si))
    return pl.pallas_call(
        _post_kernel,
        grid=(b, nt),
        in_specs=[pl.BlockSpec((1, tm, d), lambda bi, si: (bi, si, 0)),
                  pl.BlockSpec((1, tm, c), lambda bi, si: (bi, si, 0)),
                  pl.BlockSpec((1, tm, c), lambda bi, si: (bi, si, 0)),
                  full((1, c)), full((c, d)), full((c, d)), full((1, d)), full((d, d)),
                  pl.BlockSpec((1, m, d), lambda bi, si: (bi, 0, 0)),
                  pl.BlockSpec((1, m, d), lambda bi, si: (bi, 0, 0)),
                  full((d, d)), full((1, d)), full(wr_t.shape), full(br_t.shape), full((tm, tm)),
                  full((N_EXPERTS, N_EXPERTS))],
        out_specs=[pl.BlockSpec((tm, d), lambda bi, si: (bi * nt + si, 0)),
                   pl.BlockSpec((tm, d), lambda bi, si: (bi * nt + si, 0)),
                   tok(EXPERT_TOPK),
                   pl.BlockSpec((tm, LANES), lambda bi, si: (bi * nt + si, 0)),
                   tok(EXPERT_TOPK), full((N_EXPERTS, LANES)),
                   tok(EXPERT_TOPK),
                   pl.BlockSpec((N_EXPERTS, LANES), lambda bi, si: (bi * nt + si, 0))],
        out_shape=[jax.ShapeDtypeStruct((n, d), F32),
                   jax.ShapeDtypeStruct((n, d), BF16),
                   jax.ShapeDtypeStruct((EXPERT_TOPK, n), jnp.int32),
                   jax.ShapeDtypeStruct((n, LANES), F32),
                   jax.ShapeDtypeStruct((EXPERT_TOPK, n), jnp.int32),
                   jax.ShapeDtypeStruct((N_EXPERTS, LANES), F32),
                   jax.ShapeDtypeStruct((EXPERT_TOPK, n), jnp.int32),
                   jax.ShapeDtypeStruct((n // tm * N_EXPERTS, LANES), F32)],
        scratch_shapes=[pltpu.VMEM((N_EXPERTS, LANES), F32)],
        compiler_params=_params(("arbitrary", "arbitrary")),
        name="post",
    )(x, ylru, yatt, gatt, wo_a, wo_b, gcross, wxq, kx, vx, wxo, gffn, wr_t, br_t, tri,
      jnp.asarray(np.tril(np.ones((N_EXPERTS, N_EXPERTS), np.float32), -1)))


def _dest_kernel(ps_ref, eid_ref, rank_ref, dest_ref):
    eid = eid_ref[...]
    dest = rank_ref[...]
    for e in range(N_EXPERTS):
        dest = dest + jnp.where(eid == e, ps_ref[e], 0)
    dest_ref[...] = dest


def _dest(pad_starts, eid, rank):
    shape = (eid.size // LANES, LANES)
    spec = pl.BlockSpec(shape, lambda i, ps: (0, 0))
    return pl.pallas_call(
        _dest_kernel,
        grid_spec=pltpu.PrefetchScalarGridSpec(num_scalar_prefetch=1, grid=(1,), in_specs=[spec, spec],
                                               out_specs=spec),
        out_shape=jax.ShapeDtypeStruct(shape, jnp.int32),
        compiler_params=_params(("arbitrary",)),
        name="dest",
    )(pad_starts, eid.reshape(shape), rank.reshape(shape)).reshape(-1)


def _pieces(largest):
    piece = largest
    while piece:
        yield piece
        piece //= 2


def _tile_rows(row, count):
    return pl.ds(pl.multiple_of(row * SUBLANES, SUBLANES), count * SUBLANES)


def _dispatch_kernel(dest_ref, fill_ref, rstart_ref, rlen_ref, hn_ref, pos_ref, xout_ref, inv_ref,
                     sorted_ref, zero_ref, sem, zsem):
    tm = hn_ref.shape[0]
    n = dest_ref.shape[0] // EXPERT_TOPK
    n_sorted = EXPERT_TOPK * tm
    step = pl.program_id(0)
    base = step * tm

    def dump_rows(first, end):
        def group(g, carry):
            for i in range(SUBLANES):
                row = g * SUBLANES + i
                inv_ref[row] = EXPERT_TOPK * n + (row & (EXPERT_ROWS - 1))
            return carry

        lax.fori_loop(first // SUBLANES, end // SUBLANES, group, 0)

    zero_rows = zero_ref.shape[0] // SUBLANES
    n_rows = xout_ref.shape[0] // SUBLANES

    def pad_copies(e, start):
        first = fill_ref[e]
        length = fill_ref[N_EXPERTS + e]
        if start:
            dump_rows(first, first + length)
        for piece in _pieces(zero_rows):
            offset = first + (length & ~(2 * piece - 1))

            @pl.when((length & piece) != 0)
            def _():
                cp = pltpu.make_async_copy(zero_ref.at[_tile_rows(0, piece)], xout_ref.at[_tile_rows(offset, piece)],
                                           zsem)
                cp.start() if start else cp.wait()

    def tail_copies(start):
        def one(i, carry):
            cp = pltpu.make_async_copy(zero_ref, xout_ref.at[_tile_rows(i * zero_rows, zero_rows)], zsem)
            cp.start() if start else cp.wait()
            return carry

        lax.fori_loop(fill_ref[2 * N_EXPERTS] // zero_rows, n_rows // zero_rows, one, 0)
        if start:
            dump_rows(fill_ref[2 * N_EXPERTS], n_rows)

    @pl.when(step == 0)
    def _():
        zero_ref[...] = jnp.zeros_like(zero_ref)
        lax.fori_loop(0, N_EXPERTS, lambda e, c: (pad_copies(e, True), c)[1], 0)
        tail_copies(True)

    pos = pos_ref[...]
    p_iota = lax.broadcasted_iota(jnp.int32, (n_sorted, tm), 0)
    picks = jnp.where((p_iota == pos[0:1, :]) | (p_iota == pos[1:2, :]), 1.0, 0.0).astype(BF16)
    for slot in range(EXPERT_TOPK):
        for r in range(tm):
            src = slot * n + base + r
            inv_ref[dest_ref[src]] = src
    xs = jnp.dot(picks, hn_ref[...], preferred_element_type=F32)
    for k in range(SUBLANES):
        sorted_ref[pl.ds(k, n_sorted, stride=SUBLANES), :] = xs[:, k * LANES:(k + 1) * LANES]

    def run(e, offset):
        length = rlen_ref[step * N_EXPERTS + e]
        first = rstart_ref[step * N_EXPERTS + e]

        def copy_pieces(pieces):
            for piece in pieces:
                done = length & ~(2 * piece - 1)

                @pl.when((length & piece) != 0)
                def _():
                    pltpu.make_async_copy(sorted_ref.at[_tile_rows(offset + done, piece)],
                                          xout_ref.at[_tile_rows(first + done, piece)], sem).start()

        big = [p for p in _pieces(tm) if p >= RUN_SPLIT]

        @pl.when(length >= RUN_SPLIT)
        def _():
            copy_pieces(big)

        copy_pieces([p for p in _pieces(tm) if p < RUN_SPLIT])
        return offset + length

    lax.fori_loop(0, N_EXPERTS, run, 0)

    pltpu.make_async_copy(sorted_ref, xout_ref.at[_tile_rows(0, n_sorted)], sem).wait()

    @pl.when(step == 0)
    def _():
        lax.fori_loop(0, N_EXPERTS, lambda e, c: (pad_copies(e, False), c)[1], 0)
        tail_copies(False)


def _dispatch(dest_flat, fill, run_start, run_len, hn2, pos, n_rows, tm):
    n, d = hn2.shape
    assert d == SUBLANES * LANES, "a row must be exactly one (8, 128) tile for the tile-contiguous layout"
    return pl.pallas_call(
        _dispatch_kernel,
        grid_spec=pltpu.PrefetchScalarGridSpec(
            num_scalar_prefetch=4,
            grid=(n // tm,),
            in_specs=[pl.BlockSpec((tm, d), lambda i, *_: (i, 0)),
                      pl.BlockSpec((EXPERT_TOPK, tm), lambda i, *_: (0, i))],
            out_specs=[pl.BlockSpec(memory_space=pl.ANY), pl.BlockSpec(memory_space=pltpu.SMEM)],
            scratch_shapes=[pltpu.VMEM((EXPERT_TOPK * tm * SUBLANES, LANES), F32),
                            pltpu.VMEM((EXPERT_ROWS // 2 * SUBLANES, LANES), F32),
                            pltpu.SemaphoreType.DMA(()), pltpu.SemaphoreType.DMA(())]),
        out_shape=[jax.ShapeDtypeStruct((n_rows * SUBLANES, LANES), F32),
                   jax.ShapeDtypeStruct((n_rows,), jnp.int32)],
        compiler_params=_params(("arbitrary",)),
        name="dispatch",
    )(dest_flat, fill, run_start, run_len, hn2, pos)


def _experts_kernel(be_ref, nu_ref, inv_ref, x_ref, wgu_ref, wd_ref, ytok_ref, wgu_bf_ref, wd_bf_ref, ybuf_ref, sems):
    j = pl.program_id(0)
    rows = x_ref.shape[0] // SUBLANES
    f = wd_ref.shape[1]
    n_used = nu_ref[0]
    par = j % 2
    last_blk = be_ref.shape[0] - 1

    pieces = SUBLANES

    def scatter_copy(blk_par, r, dst_row):
        dst = pl.multiple_of(dst_row * pieces, pieces)
        return pltpu.make_async_copy(ybuf_ref.at[blk_par, pl.ds(r * pieces, pieces)],
                                     ytok_ref.at[pl.ds(dst, pieces)], sems.at[blk_par])

    chunk = rows // EXPERT_CHUNKS

    def start_scatter_of_previous(part):
        base = (j - 1) * rows
        for r in range(part * chunk, (part + 1) * chunk):
            scatter_copy(1 - par, r, inv_ref[base + r]).start()

    def wait_scatter(blk_par):
        pltpu.make_async_copy(ybuf_ref.at[blk_par], ytok_ref.at[pl.ds(0, rows * pieces)], sems.at[blk_par]).wait()

    def mlp(part):
        x = jnp.concatenate([x_ref[pl.ds(part * chunk * pieces + k, chunk, stride=pieces), :] for k in range(pieces)],
                            axis=1)
        gu = jnp.dot(x.astype(BF16), wgu_bf_ref[...], preferred_element_type=F32)
        act = jax.nn.silu(gu[:, 0:f]) * gu[:, f:2 * f]
        y = jnp.dot(act.astype(BF16), wd_bf_ref[...], preferred_element_type=F32)
        for k in range(pieces):
            ybuf_ref[par, pl.ds(part * chunk * pieces + k, chunk, stride=pieces), :] = y[:, k * LANES:(k + 1) * LANES]

    jc = jnp.minimum(j, last_blk)
    @pl.when((j < n_used) & ((j == 0) | (be_ref[jc] != be_ref[jnp.maximum(jc, 1) - 1])))
    def _():
        wgu_bf_ref[...] = wgu_ref[0].astype(BF16)
        wd_bf_ref[...] = wd_ref[0].astype(BF16)

    @pl.when((j >= 2) & (j <= n_used))
    def _():
        wait_scatter(par)

    @pl.when(j == 0)
    def _():
        ybuf_ref[1] = jnp.zeros(ybuf_ref.shape[1:], F32)
        cp = pltpu.make_async_copy(ybuf_ref.at[1], ytok_ref.at[pl.ds(ytok_ref.shape[0] - rows * pieces, rows * pieces)],
                                   sems.at[1])
        cp.start()
        cp.wait()

    @pl.when((j == 0) & (j < n_used))
    def _():
        for part in range(EXPERT_CHUNKS):
            mlp(part)

    @pl.when((j >= 1) & (j < n_used))
    def _():
        for part in range(EXPERT_CHUNKS):
            start_scatter_of_previous(part)
            mlp(part)

    @pl.when((j >= 1) & (j == n_used))
    def _():
        for part in range(EXPERT_CHUNKS):
            start_scatter_of_previous(part)
        wait_scatter(1 - par)


def _experts(blk_expert, n_used, inv, x_buf, w_gate_up, w_down, n_out_rows):
    d = w_gate_up.shape[1]
    f = w_down.shape[1]
    r = EXPERT_ROWS
    n_blk = x_buf.shape[0] // (r * SUBLANES)
    assert d == SUBLANES * LANES and x_buf.shape[1] == LANES, "rows are (8, 128) tiles (tile-contiguous layout)"
    x_map = lambda j, be, nu, inv: (jnp.minimum(j, jnp.maximum(nu[0], 1) - 1), 0)
    w_map = lambda j, be, nu, inv: (be[jnp.minimum(j, n_blk - 1)], 0, 0)
    return pl.pallas_call(
        _experts_kernel,
        grid_spec=pltpu.PrefetchScalarGridSpec(
            num_scalar_prefetch=3,
            grid=(n_blk + 1,),
            in_specs=[pl.BlockSpec((r * SUBLANES, LANES), x_map),
                      pl.BlockSpec((1, d, 2 * f), w_map),
                      pl.BlockSpec((1, f, d), w_map)],
            out_specs=pl.BlockSpec(memory_space=pl.ANY),
            scratch_shapes=[pltpu.VMEM((d, 2 * f), BF16), pltpu.VMEM((f, d), BF16),
                            pltpu.VMEM((2, r * SUBLANES, LANES), F32), pltpu.SemaphoreType.DMA((2,))]),
        out_shape=jax.ShapeDtypeStruct((n_out_rows * SUBLANES, LANES), F32),
        compiler_params=_params(("arbitrary",)),
        name="experts",
    )(blk_expert, n_used, inv, x_buf, w_gate_up, w_down)


def _combine_kernel(h2_ref, gcol_ref, gfin_ref, y0_ref, y1_ref, o_ref):
    tc = h2_ref.shape[0]

    def rows_of(y_ref):
        return jnp.concatenate([y_ref[pl.ds(k, tc, stride=SUBLANES), :] for k in range(SUBLANES)], axis=1)

    g = gcol_ref[...]
    moe = g[:, 0:1] * rows_of(y0_ref) + g[:, 1:2] * rows_of(y1_ref)
    o_ref[...] = _rms(h2_ref[...] + moe, gfin_ref[...])


def _combine(h2, gcol, gfin, y_tok, tc):
    n, d = h2.shape
    nt = n // tc
    return pl.pallas_call(
        _combine_kernel,
        grid=(nt,),
        in_specs=[pl.BlockSpec((tc, d), lambda i: (i, 0)),
                  pl.BlockSpec((tc, LANES), lambda i: (i, 0)),
                  pl.BlockSpec((1, d), lambda i: (0, 0)),
                  pl.BlockSpec((tc * SUBLANES, LANES), lambda i: (i, 0)),
                  pl.BlockSpec((tc * SUBLANES, LANES), lambda i: (i + nt, 0))],
        out_specs=pl.BlockSpec((tc, d), lambda i: (i, 0)),
        out_shape=jax.ShapeDtypeStruct((n, d), F32),
        compiler_params=_params(("arbitrary",)),
        name="combine",
    )(h2, gcol, gfin, y_tok, y_tok)


def _rope_tables(s):
    half = ROT_DIM // 2
    inv_freq = (ROPE_THETA ** (-np.arange(0, ROT_DIM, 2, dtype=np.float32) / ROT_DIM)).astype(np.float32)
    ang = np.arange(s, dtype=np.float32)[:, None] * inv_freq[None, :]
    cos, sin = np.cos(ang), np.sin(ang)
    zeros = lambda w: np.zeros((s, w), np.float32)
    cos_h = np.concatenate([cos, cos, np.ones((s, HEAD_DIM - ROT_DIM), np.float32)], axis=1)
    sina_h = np.concatenate([-sin, zeros(HEAD_DIM - half)], axis=1)
    sinb_h = np.concatenate([zeros(half), sin, zeros(HEAD_DIM - ROT_DIM)], axis=1)
    tile = lambda t: jnp.asarray(np.tile(t, (1, LANES // HEAD_DIM)), F32)
    return tile(cos_h), tile(sina_h), tile(sinb_h)


def _block_diag(w):
    nblk, bi, bo = w.shape
    eye = jnp.eye(nblk, dtype=w.dtype)
    return jnp.einsum('hij,hg->higj', w, eye).reshape(nblk * bi, nblk * bo)


def _layer(h, mem, norm_mix, w_in, conv_w, conv_b, w_rg, b_rg, w_ig, b_ig, lru_lambda, norm_lru_out,
           norm_attn_out, w_out, norm_cross, norm_mem, w_xq, w_xkv, w_xo, norm_ffn, w_router_group,
           b_router_group, w_router_expert, b_router_expert, w_gate_up, w_down, norm_out):
    b, s, d = h.shape
    c = conv_w.shape[-1]
    n = b * s
    row = lambda t: t.reshape(1, -1)
    tm = 512 if s % 512 == 0 else MOBA_BLOCK
    ts = tm
    assert s % ts == 0 and s % tm == 0 and c == N_HEADS * HEAD_DIM

    kx, vx = _memkv(mem, row(norm_mem), w_xkv.astype(BF16))

    w_gates = jnp.concatenate([_block_diag(w_rg), _block_diag(w_ig)], axis=1).astype(BF16)
    b_gates = jnp.concatenate([b_rg, b_ig]).reshape(1, -1)
    cos_t, sina_t, sinb_t = _rope_tables(s)
    ylru, q, k, v = _mixer_in(h, row(norm_mix), w_in.astype(BF16), conv_w, row(conv_b), w_gates, b_gates,
                              row(lru_lambda), row(norm_lru_out), cos_t, sina_t, sinb_t, ts)
    yatt = _moba(q, k, v)

    w_out_b = w_out.astype(BF16)
    wr_t = jnp.zeros((ROUTER_ROWS, d), F32)
    wr_t = wr_t.at[0:N_GROUPS].set(w_router_group.T).at[SUBLANES:SUBLANES + N_EXPERTS].set(w_router_expert.T)
    wr_hi = wr_t.astype(BF16)
    wr_t = jnp.concatenate([wr_hi, (wr_t - wr_hi.astype(F32)).astype(BF16)], axis=0)
    br_t = jnp.zeros((ROUTER_ROWS, 1), F32)
    br_t = br_t.at[0:N_GROUPS, 0].set(b_router_group).at[SUBLANES:SUBLANES + N_EXPERTS, 0].set(b_router_expert)
    tri = jnp.asarray(np.triu(np.ones((tm, tm), np.float32), 1), BF16)
    h2, hn2, eid, gcol, rank, counts, pos, tile_cnt = _post(
        h, ylru, yatt, row(norm_attn_out), w_out_b[:c], w_out_b[c:], row(norm_cross), w_xq.astype(BF16),
        kx, vx, w_xo.astype(BF16), row(norm_ffn), wr_t, br_t, tri, tm)

    counts = counts[:, 0].astype(jnp.int32)
    padded = (counts + EXPERT_ROWS - 1) // EXPERT_ROWS * EXPERT_ROWS
    pad_ends = jnp.cumsum(padded)
    pad_starts = pad_ends - padded
    dest_flat = _dest(pad_starts.astype(jnp.int32), eid, rank)
    n_blocks = n * EXPERT_TOPK // EXPERT_ROWS + N_EXPERTS
    blk_first = jnp.arange(n_blocks, dtype=jnp.int32) * EXPERT_ROWS
    blk_expert = jnp.minimum(jnp.sum(blk_first[:, None] >= pad_ends[None, :], axis=1), N_EXPERTS - 1).astype(jnp.int32)
    n_used = (pad_ends[-1:] // EXPERT_ROWS).astype(jnp.int32)
    fill = jnp.concatenate([pad_starts + counts, padded - counts, pad_ends[-1:]]).astype(jnp.int32)

    tile_cnt = tile_cnt[:, 0].astype(jnp.int32).reshape(n // tm, N_EXPERTS)
    run_start = (pad_starts[None, :] + jnp.cumsum(tile_cnt, axis=0) - tile_cnt).astype(jnp.int32).reshape(-1)
    x_buf, inv = _dispatch(dest_flat, fill, run_start, tile_cnt.reshape(-1), hn2, pos, n_blocks * EXPERT_ROWS, tm)
    y_tok = _experts(blk_expert, n_used, inv, x_buf, w_gate_up, w_down, EXPERT_TOPK * n + EXPERT_ROWS)
    out = _combine(h2, gcol, row(norm_out), y_tok, tm)
    return out.reshape(b, s, d)


def kernel(x, mem, norm_mix, w_in, conv_w, conv_b, w_rg, b_rg, w_ig, b_ig, lru_lambda, norm_lru_out, norm_attn_out,
           w_out, norm_cross, norm_mem, w_xq, w_xkv, w_xo, norm_ffn, w_router_group, b_router_group,
           w_router_expert, b_router_expert, w_gate_up, w_down, norm_final):
    depth = norm_mix.shape[0]
    assert depth == 1, "the fused final norm assumes a single layer"
    l = 0
    return _layer(x, mem, norm_mix[l], w_in[l], conv_w[l], conv_b[l], w_rg[l], b_rg[l], w_ig[l], b_ig[l],
                  lru_lambda[l], norm_lru_out[l], norm_attn_out[l], w_out[l], norm_cross[l], norm_mem[l],
                  w_xq[l], w_xkv[l], w_xo[l], norm_ffn[l], w_router_group[l], b_router_group[l],
                  w_router_expert[l], b_router_expert[l], w_gate_up[l], w_down[l], norm_final)
```

```python
import jax
import jax.numpy as jnp
import numpy as np
from jax import lax
from jax.experimental import pallas as pl
from jax.experimental.pallas import tpu as pltpu

F32 = jnp.float32
BF16 = jnp.bfloat16

CONV_WIDTH = 4
RG_C = 8.0
N_HEADS = 8
HEAD_DIM = 64
ROT_DIM = HEAD_DIM // 4
ROPE_THETA = 500000.0
MOBA_BLOCK = 256
MOBA_TOPK = 3
N_XHEADS = 4
N_GROUPS = 4
EXPERTS_PER_GROUP = 8
N_EXPERTS = N_GROUPS * EXPERTS_PER_GROUP
EXPERT_TOPK = 2
EPS = 1e-6

LANES = 128
SUBLANES = 8
NEG_BIG = -1e30
LOG2_E = 1.4426950408889634
MOBA_LOOKAHEAD = 3
EXPERT_ROWS = 512
RUN_SPLIT = 64
EXPERT_CHUNKS = 1
MIXER_SUB_ROWS = 256
BF16_ROWS = 2 * SUBLANES
ROUTER_ROWS = -(-(SUBLANES + N_EXPERTS) // BF16_ROWS) * BF16_ROWS
VMEM_LIMIT = 56 * 1024 * 1024

NT_DIMS = (((1,), (1,)), ((), ()))


def _rms(x, g):
    return x * lax.rsqrt(jnp.mean(x * x, axis=-1, keepdims=True) + EPS) * g


def _params(sem):
    return pltpu.CompilerParams(dimension_semantics=sem, vmem_limit_bytes=VMEM_LIMIT)


def _memkv_kernel(mem_ref, g_ref, w_ref, k_ref, v_ref):
    d = mem_ref.shape[-1]
    mn = _rms(mem_ref[0], g_ref[...]).astype(BF16)
    kv = jnp.dot(mn, w_ref[...], preferred_element_type=F32)
    k_ref[0] = kv[:, :d].astype(BF16)
    v_ref[0] = kv[:, d:].astype(BF16)


def _memkv(mem, g, w_xkv):
    b, m, d = mem.shape
    return pl.pallas_call(
        _memkv_kernel,
        grid=(b,),
        in_specs=[pl.BlockSpec((1, m, d), lambda i: (i, 0, 0)),
                  pl.BlockSpec((1, d), lambda i: (0, 0)),
                  pl.BlockSpec((d, 2 * d), lambda i: (0, 0))],
        out_specs=[pl.BlockSpec((1, m, d), lambda i: (i, 0, 0)),
                   pl.BlockSpec((1, m, d), lambda i: (i, 0, 0))],
        out_shape=[jax.ShapeDtypeStruct((b, m, d), BF16)] * 2,
        compiler_params=_params(("arbitrary",)),
        name="memkv",
    )(mem, g, w_xkv)


def _mixer_in_kernel(x_ref, gmix_ref, win_ref, convw_ref, convb_ref, wg_ref, bg_ref, lam_ref, glru_ref,
                     cos_ref, sina_ref, sinb_ref,
                     ylru_ref, q_ref, k_ref, v_ref, xpad_ref, hcar_ref, ascan_ref, uscan_ref):
    c = ylru_ref.shape[-1]
    si = pl.program_id(1)

    @pl.when(si == 0)
    def _():
        xpad_ref[0:SUBLANES, :] = jnp.zeros((SUBLANES, c), F32)
        hcar_ref[...] = jnp.zeros_like(hcar_ref)

    n_sub = x_ref.shape[1] // ascan_ref.shape[1]
    ts = ascan_ref.shape[1]
    projs = []
    for sub in range(n_sub):
        xn = _rms(x_ref[0, sub * ts:(sub + 1) * ts, :], gmix_ref[...]).astype(BF16)
        projs.append(jnp.dot(xn, win_ref[...], preferred_element_type=F32))
    for sub in range(n_sub):
        _mixer_stage2(projs[sub], slice(sub * ts, (sub + 1) * ts), convw_ref, convb_ref, wg_ref, bg_ref, lam_ref,
                      glru_ref, cos_ref, sina_ref, sinb_ref, ylru_ref, q_ref, k_ref, v_ref, xpad_ref, hcar_ref,
                      ascan_ref, uscan_ref)


def _mixer_stage2(proj, rows_out, convw_ref, convb_ref, wg_ref, bg_ref, lam_ref, glru_ref, cos_ref, sina_ref,
                  sinb_ref, ylru_ref, q_ref, k_ref, v_ref, xpad_ref, hcar_ref, ascan_ref, uscan_ref):
    ts = proj.shape[0]
    c = ylru_ref.shape[-1]
    x_lru = proj[:, 0:c]
    g_lru = proj[:, c:2 * c]
    q = proj[:, 2 * c:3 * c]
    k = proj[:, 3 * c:4 * c]
    v = proj[:, 4 * c:5 * c]

    xpad_ref[SUBLANES:SUBLANES + ts, :] = x_lru
    cw = convw_ref[...]
    xc = convb_ref[...] + cw[3:4] * x_lru
    for j in range(1, CONV_WIDTH):
        xc = xc + cw[3 - j:4 - j] * xpad_ref[SUBLANES - j:SUBLANES - j + ts, :]
    xpad_ref[0:SUBLANES, :] = x_lru[ts - SUBLANES:ts, :]

    gates = jnp.dot(xc.astype(BF16), wg_ref[...], preferred_element_type=F32) + bg_ref[...]
    r = jax.nn.sigmoid(gates[:, 0:c])
    i = jax.nn.sigmoid(gates[:, c:2 * c])
    neg_lam = -lam_ref[...]
    softplus = jnp.maximum(neg_lam, 0.0) + jnp.log1p(jnp.exp(-jnp.abs(neg_lam)))
    log_a = -RG_C * r * softplus
    a = jnp.exp(log_a)
    u = jnp.sqrt(-jnp.tanh(log_a) * (a * a + 1.0)) * (i * xc)

    groups = ts // SUBLANES
    cols = c // LANES
    for j in range(cols):
        ascan_ref[j] = a[:, j * LANES:(j + 1) * LANES]
        uscan_ref[j] = u[:, j * LANES:(j + 1) * LANES]
    for r in range(SUBLANES):
        rows = pl.ds(r, groups, stride=SUBLANES)
        a_r = jnp.concatenate([ascan_ref[j, rows, :] for j in range(cols)], axis=1)
        u_r = jnp.concatenate([uscan_ref[j, rows, :] for j in range(cols)], axis=1)
        if r:
            u_r = a_r * u_acc + u_r
            a_r = a_r * a_acc
            for j in range(cols):
                ascan_ref[j, rows, :] = a_r[:, j * LANES:(j + 1) * LANES]
                uscan_ref[j, rows, :] = u_r[:, j * LANES:(j + 1) * LANES]
        a_acc, u_acc = a_r, u_r
    carry = hcar_ref[...]
    hs = []
    for gi in range(groups):
        blk = slice(gi * SUBLANES, (gi + 1) * SUBLANES)
        a_g = jnp.concatenate([ascan_ref[j, blk, :] for j in range(cols)], axis=1)
        u_g = jnp.concatenate([uscan_ref[j, blk, :] for j in range(cols)], axis=1)
        hs.append(a_g * carry + u_g)
        carry = a_acc[gi:gi + 1] * carry + u_acc[gi:gi + 1]
    h = jnp.concatenate(hs, axis=0)
    hcar_ref[...] = carry

    y = h * jax.nn.gelu(g_lru)
    ylru_ref[0, rows_out, :] = _rms(y, glru_ref[...]).astype(BF16)

    wide = lambda t_ref: jnp.concatenate([t_ref[rows_out, :]] * (c // LANES), axis=1)
    cos, sina, sinb = wide(cos_ref), wide(sina_ref), wide(sinb_ref)
    half = ROT_DIM // 2

    def rope(t):
        return t * cos + pltpu.roll(t, c - half, axis=1) * sina + pltpu.roll(t, half, axis=1) * sinb

    q_ref[0, rows_out, :] = rope(q)
    k_ref[0, rows_out, :] = rope(k)
    v_ref[0, rows_out, :] = v.astype(BF16)


def _mixer_in(x, gmix, w_in, conv_w, conv_b, w_gates, b_gates, lam, glru, cos_t, sina_t, sinb_t, ts):
    b, s, d = x.shape
    c = conv_w.shape[-1]
    sub = min(ts, MIXER_SUB_ROWS)
    full = lambda shape: pl.BlockSpec(shape, lambda bi, si: (0,) * len(shape))
    tab = pl.BlockSpec((ts, LANES), lambda bi, si: (si, 0))
    seq = lambda: pl.BlockSpec((1, ts, c), lambda bi, si: (bi, si, 0))
    return pl.pallas_call(
        _mixer_in_kernel,
        grid=(b, s // ts),
        in_specs=[pl.BlockSpec((1, ts, d), lambda bi, si: (bi, si, 0)),
                  full((1, d)), full(w_in.shape), full(conv_w.shape), full((1, c)),
                  full(w_gates.shape), full((1, 2 * c)), full((1, c)), full((1, c)),
                  tab, tab, tab],
        out_specs=[seq(), seq(), seq(), seq()],
        out_shape=[jax.ShapeDtypeStruct((b, s, c), BF16),
                   jax.ShapeDtypeStruct((b, s, c), F32),
                   jax.ShapeDtypeStruct((b, s, c), F32),
                   jax.ShapeDtypeStruct((b, s, c), BF16)],
        scratch_shapes=[pltpu.VMEM((sub + SUBLANES, c), F32), pltpu.VMEM((1, c), F32),
                        pltpu.VMEM((c // LANES, sub, LANES), F32), pltpu.VMEM((c // LANES, sub, LANES), F32)],
        compiler_params=_params(("arbitrary", "arbitrary")),
        name="mixer_in",
    )(x, gmix, w_in, conv_w, conv_b, w_gates, b_gates, lam, glru, cos_t, sina_t, sinb_t)


def _moba_kernel(q_ref, k_ref, v_ref, kpat_ref, o_ref, qa_ref, ka_ref):
    s_len = q_ref.shape[1]
    nb = s_len // MOBA_BLOCK
    n_pick = min(MOBA_TOPK, nb)
    q = q_ref[0]
    k = k_ref[0]
    lane = lax.broadcasted_iota(jnp.int32, (1, LANES), 1)
    kmean = jnp.concatenate(
        [jnp.sum(k[n * MOBA_BLOCK:(n + 1) * MOBA_BLOCK], axis=0, keepdims=True) for n in range(nb)],
        axis=0) * (1.0 / MOBA_BLOCK)

    blk_row = lax.broadcasted_iota(jnp.int32, (nb, s_len), 0)
    q_blk = lax.broadcasted_iota(jnp.int32, (nb, s_len), 1) // MOBA_BLOCK
    past = blk_row < q_blk

    q_hi = q.astype(BF16)
    q_lo = (q - q_hi.astype(F32)).astype(BF16)
    for h in range(2):
        own = (lane >= h * HEAD_DIM) & (lane < (h + 1) * HEAD_DIM)
        off = (1 - h) * HEAD_DIM
        km = jnp.where(own, kmean, 0.0)
        km_hi = km.astype(BF16)
        km_split = jnp.concatenate([km_hi, (km - km_hi.astype(F32)).astype(BF16)], axis=0)
        parts = (lax.dot_general(km_split, q_hi, NT_DIMS, preferred_element_type=F32)
                 + lax.dot_general(km_split, q_lo, NT_DIMS, preferred_element_type=F32))
        gate = jnp.where(past, parts[0:nb] + parts[nb:2 * nb], -jnp.inf)
        rank = jnp.zeros((nb, s_len), jnp.int32)
        for m in range(nb):
            gm = gate[m:m + 1, :]
            ahead = (gm > gate) | ((gm == gate) & (m < blk_row))
            rank = rank + ahead.astype(jnp.int32)
        allowed = ((rank < n_pick) & past) | (blk_row == q_blk)
        bias = jnp.where(allowed, 0.0, NEG_BIG)
        pieces = []
        if off:
            pieces.append(jnp.zeros((off, s_len), F32))
        pieces.append(bias)
        if LANES - off - nb:
            pieces.append(jnp.zeros((LANES - off - nb, s_len), F32))
        bias_lanes = jnp.concatenate(pieces, axis=0).T
        qa_ref[h] = jnp.where(own, q * (HEAD_DIM ** -0.5 * LOG2_E), bias_lanes).astype(BF16)
        ka_ref[h] = jnp.where(own, k.astype(BF16), kpat_ref[h])

    own0 = lane < HEAD_DIM

    def scores(qi, h):
        nk = (qi + 1) * MOBA_BLOCK
        qa = qa_ref[h, qi * MOBA_BLOCK:(qi + 1) * MOBA_BLOCK, :]
        return lax.dot_general(qa, ka_ref[h, 0:nk, :], NT_DIMS, preferred_element_type=F32)

    key_in_blk = lax.broadcasted_iota(jnp.int32, (MOBA_BLOCK, MOBA_BLOCK), 1)
    q_in_blk = lax.broadcasted_iota(jnp.int32, (MOBA_BLOCK, MOBA_BLOCK), 0)
    causal = key_in_blk <= q_in_blk
    units = [(qi, h) for qi in range(nb) for h in range(2)]
    ahead = [scores(*u) for u in units[:MOBA_LOOKAHEAD]]
    outs = []

    def weighted_values(p, l, qi, h):
        outs.append(jnp.dot(p, v_ref[0, 0:(qi + 1) * MOBA_BLOCK, :], preferred_element_type=F32) / l)
        if h:
            o_ref[0, qi * MOBA_BLOCK:(qi + 1) * MOBA_BLOCK, :] = jnp.where(own0, outs[0], outs[1])
            outs.clear()

    pending = None
    for n, (qi, h) in enumerate(units):
        s = ahead.pop(0)
        if n + MOBA_LOOKAHEAD < len(units):
            ahead.append(scores(*units[n + MOBA_LOOKAHEAD]))
        n_past = qi * MOBA_BLOCK
        s_own = jnp.where(causal, s[:, n_past:], NEG_BIG)
        s = jnp.concatenate([s[:, :n_past], s_own], axis=1) if qi else s_own
        p = jnp.exp2(s - jnp.max(s, axis=-1, keepdims=True))
        l = jnp.sum(p, axis=-1, keepdims=True)
        if pending is not None:
            weighted_values(*pending)
        pending = (p.astype(BF16), l, qi, h)
    weighted_values(*pending)


def _key_block_pattern(s):
    pat = np.zeros((2, s, LANES), np.float32)
    for h in range(2):
        off = (1 - h) * HEAD_DIM
        pat[h, np.arange(s), off + np.arange(s) // MOBA_BLOCK] = 1.0
    return jnp.asarray(pat, BF16)


def _moba(q, k, v):
    b, s, c = q.shape
    spec = lambda: pl.BlockSpec((1, s, LANES), lambda bi, hi: (bi, 0, hi))
    return pl.pallas_call(
        _moba_kernel,
        grid=(b, c // LANES),
        in_specs=[spec(), spec(), spec(), pl.BlockSpec((2, s, LANES), lambda bi, hi: (0, 0, 0))],
        out_specs=spec(),
        out_shape=jax.ShapeDtypeStruct((b, s, c), F32),
        scratch_shapes=[pltpu.VMEM((2, s, LANES), BF16), pltpu.VMEM((2, s, LANES), BF16)],
        compiler_params=_params(("arbitrary", "arbitrary")),
        name="moba",
    )(q, k, v, _key_block_pattern(s))


def _post_kernel(x_ref, ylru_ref, yatt_ref, gatt_ref, woa_ref, wob_ref, gcross_ref, wxq_ref, kx_ref, vx_ref,
                 wxo_ref, gffn_ref, wr_ref, br_ref, tri_ref, elow_ref,
                 h2_ref, hn2_ref, gcol_ref, cnt_ref, pos_ref, tcnt_ref, carry_ref):
    tm = x_ref.shape[1]
    d = x_ref.shape[2]
    xd = d // N_XHEADS
    first = (pl.program_id(0) == 0) & (pl.program_id(1) == 0)

    @pl.when(first)
    def _():
        carry_ref[...] = jnp.zeros_like(carry_ref)

    ya = _rms(yatt_ref[0], gatt_ref[...]).astype(BF16)
    mix = (jnp.dot(ylru_ref[0], woa_ref[...], preferred_element_type=F32)
           + jnp.dot(ya, wob_ref[...], preferred_element_type=F32))
    h1 = x_ref[0] + mix

    hn = _rms(h1, gcross_ref[...]).astype(BF16)
    qx = jnp.dot(hn, wxq_ref[...], preferred_element_type=F32)
    heads = []
    for hh in range(N_XHEADS):
        qh = (qx[:, hh * xd:(hh + 1) * xd] * (xd ** -0.5)).astype(BF16)
        s = lax.dot_general(qh, kx_ref[0, :, hh * xd:(hh + 1) * xd], NT_DIMS, preferred_element_type=F32)
        p = jnp.exp(s - jnp.max(s, axis=-1, keepdims=True))
        l = jnp.sum(p, axis=-1, keepdims=True)
        o = jnp.dot(p.astype(BF16), vx_ref[0, :, hh * xd:(hh + 1) * xd], preferred_element_type=F32) / l
        heads.append(o.astype(BF16))
    h2 = h1 + jnp.dot(jnp.concatenate(heads, axis=1), wxo_ref[...], preferred_element_type=F32)
    hn2 = _rms(h2, gffn_ref[...])
    h2_ref[...] = h2
    hn2_ref[...] = hn2.astype(BF16)

    x_hi = hn2.astype(BF16)
    x_lo = (hn2 - x_hi.astype(F32)).astype(BF16)
    by_hi = lax.dot_general(wr_ref[...], x_hi, NT_DIMS, preferred_element_type=F32)
    by_lo = lax.dot_general(wr_ref[0:ROUTER_ROWS], x_lo, NT_DIMS, preferred_element_type=F32)
    logits = by_hi[0:ROUTER_ROWS] + by_hi[ROUTER_ROWS:2 * ROUTER_ROWS] + by_lo + br_ref[...]
    gl = logits[0:N_GROUPS]
    gmax = jnp.max(gl, axis=0, keepdims=True)
    gi = lax.broadcasted_iota(jnp.int32, gl.shape, 0)
    grp = jnp.min(jnp.where(gl == gmax, gi, N_GROUPS), axis=0, keepdims=True)
    g_w = 1.0 / jnp.sum(jnp.exp(gl - gmax), axis=0, keepdims=True)
    el = jnp.zeros((EXPERTS_PER_GROUP, tm), F32)
    for g in range(N_GROUPS):
        lo = SUBLANES + g * EXPERTS_PER_GROUP
        el = jnp.where(grp == g, logits[lo:lo + EXPERTS_PER_GROUP], el)
    ee = jnp.exp(el - jnp.max(el, axis=0, keepdims=True))
    ep = ee / jnp.sum(ee, axis=0, keepdims=True)
    ei = lax.broadcasted_iota(jnp.int32, ep.shape, 0)
    p1 = jnp.max(ep, axis=0, keepdims=True)
    i1 = jnp.min(jnp.where(ep == p1, ei, EXPERTS_PER_GROUP), axis=0, keepdims=True)
    ep_rest = jnp.where(ei == i1, -1.0, ep)
    p2 = jnp.max(ep_rest, axis=0, keepdims=True)
    i2 = jnp.min(jnp.where(ep_rest == p2, ei, EXPERTS_PER_GROUP), axis=0, keepdims=True)
    den = p1 + p2
    gate1 = g_w * p1 / den
    gate2 = g_w * p2 / den
    e1 = grp * EXPERTS_PER_GROUP + i1
    e2 = grp * EXPERTS_PER_GROUP + i2
    li = lax.broadcasted_iota(jnp.int32, (LANES, tm), 0)
    gcol_ref[...] = jnp.where(li == 0, gate1, jnp.where(li == 1, gate2, 0.0)).T

    xi = lax.broadcasted_iota(jnp.int32, (N_EXPERTS, tm), 0)
    oh1 = xi == e1
    oh2 = xi == e2
    cnt = oh1.astype(F32) + oh2.astype(F32)
    tile_cnt = jnp.broadcast_to(jnp.sum(cnt, axis=1, keepdims=True), (N_EXPERTS, LANES))
    carry_ref[...] = carry_ref[...] + tile_cnt
    cnt_ref[...] = carry_ref[...]
    tcnt_ref[...] = tile_cnt
    in_tile = jnp.dot(cnt.astype(BF16), tri_ref[...], preferred_element_type=F32)
    tile_start = jnp.dot(elow_ref[...], tile_cnt, precision=lax.Precision.HIGHEST,
                         preferred_element_type=F32)[:, 0:1]
    local = in_tile + tile_start
    pos_ref[0:1, :] = jnp.sum(jnp.where(oh1, local, 0.0), axis=0, keepdims=True).astype(jnp.int32)
    pos_ref[1:2, :] = jnp.sum(jnp.where(oh2, local, 0.0), axis=0, keepdims=True).astype(jnp.int32)


def _post(x, ylru, yatt, gatt, wo_a, wo_b, gcross, wxq, kx, vx, wxo, gffn, wr_t, br_t, tri, tm):
    b, s, d = x.shape
    c = ylru.shape[-1]
    m = kx.shape[1]
    n = b * s
    nt = s // tm
    full = lambda shape: pl.BlockSpec(shape, lambda bi, si: (0,) * len(shape))
    tok = lambda rows: pl.BlockSpec((rows, tm), lambda bi, si: (0, bi * nt + si))
    return pl.pallas_call(
        _post_kernel,
        grid=(b, nt),
        in_specs=[pl.BlockSpec((1, tm, d), lambda bi, si: (bi, si, 0)),
                  pl.BlockSpec((1, tm, c), lambda bi, si: (bi, si, 0)),
                  pl.BlockSpec((1, tm, c), lambda bi, si: (bi, si, 0)),
                  full((1, c)), full((c, d)), full((c, d)), full((1, d)), full((d, d)),
                  pl.BlockSpec((1, m, d), lambda bi, si: (bi, 0, 0)),
                  pl.BlockSpec((1, m, d), lambda bi, si: (bi, 0, 0)),
                  full((d, d)), full((1, d)), full(wr_t.shape), full(br_t.shape), full((tm, tm)),
                  full((N_EXPERTS, N_EXPERTS))],
        out_specs=[pl.BlockSpec((tm, d), lambda bi, si: (bi * nt + si, 0)),
                   pl.BlockSpec((tm, d), lambda bi, si: (bi * nt + si, 0)),
                   pl.BlockSpec((tm, LANES), lambda bi, si: (bi * nt + si, 0)),
                   full((N_EXPERTS, LANES)),
                   tok(EXPERT_TOPK),
                   pl.BlockSpec((N_EXPERTS, LANES), lambda bi, si: (bi * nt + si, 0))],
        out_shape=[jax.ShapeDtypeStruct((n, d), F32),
                   jax.ShapeDtypeStruct((n, d), BF16),
                   jax.ShapeDtypeStruct((n, LANES), F32),
                   jax.ShapeDtypeStruct((N_EXPERTS, LANES), F32),
                   jax.ShapeDtypeStruct((EXPERT_TOPK, n), jnp.int32),
                   jax.ShapeDtypeStruct((n // tm * N_EXPERTS, LANES), F32)],
        scratch_shapes=[pltpu.VMEM((N_EXPERTS, LANES), F32)],
        compiler_params=_params(("arbitrary", "arbitrary")),
        name="post",
    )(x, ylru, yatt, gatt, wo_a, wo_b, gcross, wxq, kx, vx, wxo, gffn, wr_t, br_t, tri,
      jnp.asarray(np.tril(np.ones((N_EXPERTS, N_EXPERTS), np.float32), -1)))


def _pieces(largest):
    piece = largest
    while piece:
        yield piece
        piece //= 2


def _tile_rows(row, count):
    return pl.ds(pl.multiple_of(row * SUBLANES, SUBLANES), count * SUBLANES)


def _dispatch_kernel(fill_ref, rstart_ref, rlen_ref, hn_ref, pos_ref, xout_ref, sorted_ref, zero_ref, sem, zsem):
    tm = hn_ref.shape[0]
    n_sorted = EXPERT_TOPK * tm
    step = pl.program_id(0)

    zero_rows = zero_ref.shape[0] // SUBLANES
    n_rows = xout_ref.shape[0] // SUBLANES

    def pad_copies(e, start):
        first = fill_ref[e]
        length = fill_ref[N_EXPERTS + e]
        for piece in _pieces(zero_rows):
            offset = first + (length & ~(2 * piece - 1))

            @pl.when((length & piece) != 0)
            def _():
                cp = pltpu.make_async_copy(zero_ref.at[_tile_rows(0, piece)], xout_ref.at[_tile_rows(offset, piece)],
                                           zsem)
                cp.start() if start else cp.wait()

    def tail_copies(start):
        def one(i, carry):
            cp = pltpu.make_async_copy(zero_ref, xout_ref.at[_tile_rows(i * zero_rows, zero_rows)], zsem)
            cp.start() if start else cp.wait()
            return carry

        lax.fori_loop(fill_ref[2 * N_EXPERTS] // zero_rows, n_rows // zero_rows, one, 0)

    @pl.when(step == 0)
    def _():
        zero_ref[...] = jnp.zeros_like(zero_ref)
        lax.fori_loop(0, N_EXPERTS, lambda e, c: (pad_copies(e, True), c)[1], 0)
        tail_copies(True)

    pos = pos_ref[...]
    p_iota = lax.broadcasted_iota(jnp.int32, (n_sorted, tm), 0)
    picks = jnp.where((p_iota == pos[0:1, :]) | (p_iota == pos[1:2, :]), 1.0, 0.0).astype(BF16)
    xs = jnp.dot(picks, hn_ref[...], preferred_element_type=F32)
    for k in range(SUBLANES):
        sorted_ref[pl.ds(k, n_sorted, stride=SUBLANES), :] = xs[:, k * LANES:(k + 1) * LANES]

    def run(e, offset):
        length = rlen_ref[step * N_EXPERTS + e]
        first = rstart_ref[step * N_EXPERTS + e]

        def copy_pieces(pieces):
            for piece in pieces:
                done = length & ~(2 * piece - 1)

                @pl.when((length & piece) != 0)
                def _():
                    pltpu.make_async_copy(sorted_ref.at[_tile_rows(offset + done, piece)],
                                          xout_ref.at[_tile_rows(first + done, piece)], sem).start()

        big = [p for p in _pieces(tm) if p >= RUN_SPLIT]

        @pl.when(length >= RUN_SPLIT)
        def _():
            copy_pieces(big)

        copy_pieces([p for p in _pieces(tm) if p < RUN_SPLIT])
        return offset + length

    lax.fori_loop(0, N_EXPERTS, run, 0)

    pltpu.make_async_copy(sorted_ref, xout_ref.at[_tile_rows(0, n_sorted)], sem).wait()

    @pl.when(step == 0)
    def _():
        lax.fori_loop(0, N_EXPERTS, lambda e, c: (pad_copies(e, False), c)[1], 0)
        tail_copies(False)


def _dispatch(fill, run_start, run_len, hn2, pos, n_rows, tm):
    n, d = hn2.shape
    assert d == SUBLANES * LANES, "a row must be exactly one (8, 128) tile for the tile-contiguous layout"
    return pl.pallas_call(
        _dispatch_kernel,
        grid_spec=pltpu.PrefetchScalarGridSpec(
            num_scalar_prefetch=3,
            grid=(n // tm,),
            in_specs=[pl.BlockSpec((tm, d), lambda i, *_: (i, 0)),
                      pl.BlockSpec((EXPERT_TOPK, tm), lambda i, *_: (0, i))],
            out_specs=pl.BlockSpec(memory_space=pl.ANY),
            scratch_shapes=[pltpu.VMEM((EXPERT_TOPK * tm * SUBLANES, LANES), F32),
                            pltpu.VMEM((EXPERT_ROWS // 2 * SUBLANES, LANES), F32),
                            pltpu.SemaphoreType.DMA(()), pltpu.SemaphoreType.DMA(())]),
        out_shape=jax.ShapeDtypeStruct((n_rows * SUBLANES, LANES), F32),
        compiler_params=_params(("arbitrary",)),
        name="dispatch",
    )(fill, run_start, run_len, hn2, pos)


def _experts_kernel(be_ref, nu_ref, kfirst_ref, kend_ref, rfirst_ref, rlen_ref, rdst_ref,
                    x_ref, wgu_ref, wd_ref, yts_ref, wgu_bf_ref, wd_bf_ref, ybuf_ref, sems):
    j = pl.program_id(0)
    rows = x_ref.shape[0] // SUBLANES
    f = wd_ref.shape[1]
    n_used = nu_ref[0]
    par = j % 2
    last_blk = be_ref.shape[0] - 1
    pieces = SUBLANES
    chunk = rows // EXPERT_CHUNKS

    def start_copies_of_previous():
        blk_first = (j - 1) * rows

        def run(k, carry):
            lo = jnp.maximum(rfirst_ref[k], blk_first)
            hi = jnp.minimum(rfirst_ref[k] + rlen_ref[k], blk_first + rows)
            length = jnp.maximum(hi - lo, 0)
            src = lo - blk_first
            dst = rdst_ref[k] + (lo - rfirst_ref[k])
            for piece in _pieces(rows):
                done = length & ~(2 * piece - 1)

                @pl.when((length & piece) != 0)
                def _():
                    pltpu.make_async_copy(ybuf_ref.at[1 - par, _tile_rows(src + done, piece)],
                                          yts_ref.at[_tile_rows(dst + done, piece)], sems.at[1 - par]).start()
            return carry

        lax.fori_loop(kfirst_ref[j - 1], kend_ref[j - 1], run, 0)

    def wait_copies(blk_par):
        pltpu.make_async_copy(ybuf_ref.at[blk_par], yts_ref.at[pl.ds(0, rows * pieces)], sems.at[blk_par]).wait()

    def mlp(part):
        x = jnp.concatenate([x_ref[pl.ds(part * chunk * pieces + k, chunk, stride=pieces), :] for k in range(pieces)],
                            axis=1)
        gu = jnp.dot(x.astype(BF16), wgu_bf_ref[...], preferred_element_type=F32)
        act = jax.nn.silu(gu[:, 0:f]) * gu[:, f:2 * f]
        y = jnp.dot(act.astype(BF16), wd_bf_ref[...], preferred_element_type=F32)
        for k in range(pieces):
            ybuf_ref[par, pl.ds(part * chunk * pieces + k, chunk, stride=pieces), :] = y[:, k * LANES:(k + 1) * LANES]

    jc = jnp.minimum(j, last_blk)
    @pl.when((j < n_used) & ((j == 0) | (be_ref[jc] != be_ref[jnp.maximum(jc, 1) - 1])))
    def _():
        wgu_bf_ref[...] = wgu_ref[0].astype(BF16)
        wd_bf_ref[...] = wd_ref[0].astype(BF16)

    @pl.when((j >= 2) & (j <= n_used))
    def _():
        wait_copies(par)

    @pl.when(j == 0)
    def _():
        ybuf_ref[1] = jnp.zeros(ybuf_ref.shape[1:], F32)
        cp = pltpu.make_async_copy(ybuf_ref.at[1], yts_ref.at[pl.ds(yts_ref.shape[0] - rows * pieces, rows * pieces)],
                                   sems.at[1])
        cp.start()
        cp.wait()

    @pl.when((j >= 1) & (j <= n_used))
    def _():
        start_copies_of_previous()

    @pl.when(j < n_used)
    def _():
        for part in range(EXPERT_CHUNKS):
            mlp(part)

    @pl.when((j >= 1) & (j == n_used))
    def _():
        wait_copies(1 - par)


def _experts(blk_expert, n_used, run_tables, x_buf, w_gate_up, w_down, n_out_rows):
    d = w_gate_up.shape[1]
    f = w_down.shape[1]
    r = EXPERT_ROWS
    n_blk = x_buf.shape[0] // (r * SUBLANES)
    assert d == SUBLANES * LANES and x_buf.shape[1] == LANES, "rows are (8, 128) tiles (tile-contiguous layout)"
    x_map = lambda j, be, nu, *_: (jnp.minimum(j, jnp.maximum(nu[0], 1) - 1), 0)
    w_map = lambda j, be, nu, *_: (be[jnp.minimum(j, n_blk - 1)], 0, 0)
    return pl.pallas_call(
        _experts_kernel,
        grid_spec=pltpu.PrefetchScalarGridSpec(
            num_scalar_prefetch=7,
            grid=(n_blk + 1,),
            in_specs=[pl.BlockSpec((r * SUBLANES, LANES), x_map),
                      pl.BlockSpec((1, d, 2 * f), w_map),
                      pl.BlockSpec((1, f, d), w_map)],
            out_specs=pl.BlockSpec(memory_space=pl.ANY),
            scratch_shapes=[pltpu.VMEM((d, 2 * f), BF16), pltpu.VMEM((f, d), BF16),
                            pltpu.VMEM((2, r * SUBLANES, LANES), F32), pltpu.SemaphoreType.DMA((2,))]),
        out_shape=jax.ShapeDtypeStruct((n_out_rows * SUBLANES, LANES), F32),
        compiler_params=_params(("arbitrary",)),
        name="experts",
    )(blk_expert, n_used, *run_tables, x_buf, w_gate_up, w_down)


def _combine_kernel(pos_ref, h2_ref, gcol_ref, gfin_ref, yts_ref, o_ref, pick0_ref, pick1_ref):
    tc = h2_ref.shape[0]
    n = pos_ref.shape[0] // EXPERT_TOPK
    base = pl.program_id(0) * tc
    for t in range(tc):
        for slot, pick_ref in enumerate((pick0_ref, pick1_ref)):
            row = pos_ref[slot * n + base + t]
            pick_ref[t * SUBLANES:(t + 1) * SUBLANES, :] = yts_ref[_tile_rows(row, 1), :]

    def rows_of(y_ref):
        return jnp.concatenate([y_ref[pl.ds(k, tc, stride=SUBLANES), :] for k in range(SUBLANES)], axis=1)

    g = gcol_ref[...]
    moe = g[:, 0:1] * rows_of(pick0_ref) + g[:, 1:2] * rows_of(pick1_ref)
    o_ref[...] = _rms(h2_ref[...] + moe, gfin_ref[...])


def _combine(pos_flat, h2, gcol, gfin, y_ts, tc):
    n, d = h2.shape
    return pl.pallas_call(
        _combine_kernel,
        grid_spec=pltpu.PrefetchScalarGridSpec(
            num_scalar_prefetch=1,
            grid=(n // tc,),
            in_specs=[pl.BlockSpec((tc, d), lambda i, pos: (i, 0)),
                      pl.BlockSpec((tc, LANES), lambda i, pos: (i, 0)),
                      pl.BlockSpec((1, d), lambda i, pos: (0, 0)),
                      pl.BlockSpec((EXPERT_TOPK * tc * SUBLANES, LANES), lambda i, pos: (i, 0))],
            out_specs=pl.BlockSpec((tc, d), lambda i, pos: (i, 0)),
            scratch_shapes=[pltpu.VMEM((tc * SUBLANES, LANES), F32), pltpu.VMEM((tc * SUBLANES, LANES), F32)]),
        out_shape=jax.ShapeDtypeStruct((n, d), F32),
        compiler_params=_params(("arbitrary",)),
        name="combine",
    )(pos_flat, h2, gcol, gfin, y_ts)


def _rope_tables(s):
    half = ROT_DIM // 2
    inv_freq = (ROPE_THETA ** (-np.arange(0, ROT_DIM, 2, dtype=np.float32) / ROT_DIM)).astype(np.float32)
    ang = np.arange(s, dtype=np.float32)[:, None] * inv_freq[None, :]
    cos, sin = np.cos(ang), np.sin(ang)
    zeros = lambda w: np.zeros((s, w), np.float32)
    cos_h = np.concatenate([cos, cos, np.ones((s, HEAD_DIM - ROT_DIM), np.float32)], axis=1)
    sina_h = np.concatenate([-sin, zeros(HEAD_DIM - half)], axis=1)
    sinb_h = np.concatenate([zeros(half), sin, zeros(HEAD_DIM - ROT_DIM)], axis=1)
    tile = lambda t: jnp.asarray(np.tile(t, (1, LANES // HEAD_DIM)), F32)
    return tile(cos_h), tile(sina_h), tile(sinb_h)


def _block_diag(w):
    nblk, bi, bo = w.shape
    eye = jnp.eye(nblk, dtype=w.dtype)
    return jnp.einsum('hij,hg->higj', w, eye).reshape(nblk * bi, nblk * bo)


def _layer(h, mem, norm_mix, w_in, conv_w, conv_b, w_rg, b_rg, w_ig, b_ig, lru_lambda, norm_lru_out,
           norm_attn_out, w_out, norm_cross, norm_mem, w_xq, w_xkv, w_xo, norm_ffn, w_router_group,
           b_router_group, w_router_expert, b_router_expert, w_gate_up, w_down, norm_out):
    b, s, d = h.shape
    c = conv_w.shape[-1]
    n = b * s
    row = lambda t: t.reshape(1, -1)
    tm = 512 if s % 512 == 0 else MOBA_BLOCK
    ts = tm
    assert s % ts == 0 and s % tm == 0 and c == N_HEADS * HEAD_DIM

    kx, vx = _memkv(mem, row(norm_mem), w_xkv.astype(BF16))

    w_gates = jnp.concatenate([_block_diag(w_rg), _block_diag(w_ig)], axis=1).astype(BF16)
    b_gates = jnp.concatenate([b_rg, b_ig]).reshape(1, -1)
    cos_t, sina_t, sinb_t = _rope_tables(s)
    ylru, q, k, v = _mixer_in(h, row(norm_mix), w_in.astype(BF16), conv_w, row(conv_b), w_gates, b_gates,
                              row(lru_lambda), row(norm_lru_out), cos_t, sina_t, sinb_t, ts)
    yatt = _moba(q, k, v)

    w_out_b = w_out.astype(BF16)
    wr_t = jnp.zeros((ROUTER_ROWS, d), F32)
    wr_t = wr_t.at[0:N_GROUPS].set(w_router_group.T).at[SUBLANES:SUBLANES + N_EXPERTS].set(w_router_expert.T)
    wr_hi = wr_t.astype(BF16)
    wr_t = jnp.concatenate([wr_hi, (wr_t - wr_hi.astype(F32)).astype(BF16)], axis=0)
    br_t = jnp.zeros((ROUTER_ROWS, 1), F32)
    br_t = br_t.at[0:N_GROUPS, 0].set(b_router_group).at[SUBLANES:SUBLANES + N_EXPERTS, 0].set(b_router_expert)
    tri = jnp.asarray(np.triu(np.ones((tm, tm), np.float32), 1), BF16)
    h2, hn2, gcol, counts, pos, tile_cnt = _post(
        h, ylru, yatt, row(norm_attn_out), w_out_b[:c], w_out_b[c:], row(norm_cross), w_xq.astype(BF16),
        kx, vx, w_xo.astype(BF16), row(norm_ffn), wr_t, br_t, tri, tm)

    counts = counts[:, 0].astype(jnp.int32)
    padded = (counts + EXPERT_ROWS - 1) // EXPERT_ROWS * EXPERT_ROWS
    pad_ends = jnp.cumsum(padded)
    pad_starts = pad_ends - padded
    n_blocks = n * EXPERT_TOPK // EXPERT_ROWS + N_EXPERTS
    blk_first = jnp.arange(n_blocks, dtype=jnp.int32) * EXPERT_ROWS
    blk_expert = jnp.minimum(jnp.sum(blk_first[:, None] >= pad_ends[None, :], axis=1), N_EXPERTS - 1).astype(jnp.int32)
    n_used = (pad_ends[-1:] // EXPERT_ROWS).astype(jnp.int32)
    fill = jnp.concatenate([pad_starts + counts, padded - counts, pad_ends[-1:]]).astype(jnp.int32)

    n_tiles = n // tm
    tile_cnt = tile_cnt[:, 0].astype(jnp.int32).reshape(n_tiles, N_EXPERTS)
    run_start = (pad_starts[None, :] + jnp.cumsum(tile_cnt, axis=0) - tile_cnt).astype(jnp.int32)
    x_buf = _dispatch(fill, run_start.reshape(-1), tile_cnt.reshape(-1), hn2, pos, n_blocks * EXPERT_ROWS, tm)

    in_tile = jnp.cumsum(tile_cnt, axis=1) - tile_cnt
    out_row = jnp.arange(n_tiles, dtype=jnp.int32)[:, None] * (EXPERT_TOPK * tm) + in_tile
    pad_first = pad_starts + counts
    run_first = jnp.concatenate([run_start.T, pad_first[:, None]], axis=1).reshape(-1)
    run_size = jnp.concatenate([tile_cnt.T, (padded - counts)[:, None]], axis=1).reshape(-1)
    run_out = jnp.concatenate([out_row.T, (EXPERT_TOPK * n + pad_first % EXPERT_ROWS)[:, None]], axis=1).reshape(-1)
    blk_end = blk_first + EXPERT_ROWS
    k_first = jnp.sum((run_first + run_size)[None, :] <= blk_first[:, None], axis=1)
    k_end = jnp.sum(run_first[None, :] < blk_end[:, None], axis=1)
    tables = [t.astype(jnp.int32) for t in (k_first, k_end, run_first, run_size, run_out)]
    y_ts = _experts(blk_expert, n_used, tables, x_buf, w_gate_up, w_down, EXPERT_TOPK * n + EXPERT_ROWS)
    out = _combine(pos.reshape(-1), h2, gcol, row(norm_out), y_ts, tm)
    return out.reshape(b, s, d)


def kernel(x, mem, norm_mix, w_in, conv_w, conv_b, w_rg, b_rg, w_ig, b_ig, lru_lambda, norm_lru_out, norm_attn_out,
           w_out, norm_cross, norm_mem, w_xq, w_xkv, w_xo, norm_ffn, w_router_group, b_router_group,
           w_router_expert, b_router_expert, w_gate_up, w_down, norm_final):
    depth = norm_mix.shape[0]
    assert depth == 1, "the fused final norm assumes a single layer"
    l = 0
    return _layer(x, mem, norm_mix[l], w_in[l], conv_w[l], conv_b[l], w_rg[l], b_rg[l], w_ig[l], b_ig[l],
                  lru_lambda[l], norm_lru_out[l], norm_attn_out[l], w_out[l], norm_cross[l], norm_mem[l],
                  w_xq[l], w_xkv[l], w_xo[l], norm_ffn[l], w_router_group[l], b_router_group[l],
                  w_router_expert[l], b_router_expert[l], w_gate_up[l], w_down[l], norm_final)
```

```python
import jax
import jax.numpy as jnp
import numpy as np
from jax import lax
from jax.experimental import pallas as pl
from jax.experimental.pallas import tpu as pltpu

F32 = jnp.float32
BF16 = jnp.bfloat16

CONV_WIDTH = 4
RG_C = 8.0
N_HEADS = 8
HEAD_DIM = 64
ROT_DIM = HEAD_DIM // 4
ROPE_THETA = 500000.0
MOBA_BLOCK = 256
MOBA_TOPK = 3
N_XHEADS = 4
N_GROUPS = 4
EXPERTS_PER_GROUP = 8
N_EXPERTS = N_GROUPS * EXPERTS_PER_GROUP
EXPERT_TOPK = 2
EPS = 1e-6

LANES = 128
SUBLANES = 8
NEG_BIG = -1e30
LOG2_E = 1.4426950408889634
MOBA_LOOKAHEAD = 3
EXPERT_ROWS = 512
RUN_SPLIT = 64
EXPERT_CHUNKS = 1
MIXER_SUB_ROWS = 256
BF16_ROWS = 2 * SUBLANES
ROUTER_ROWS = -(-(SUBLANES + N_EXPERTS) // BF16_ROWS) * BF16_ROWS
VMEM_LIMIT = 56 * 1024 * 1024

NT_DIMS = (((1,), (1,)), ((), ()))


def _rms(x, g):
    return x * lax.rsqrt(jnp.mean(x * x, axis=-1, keepdims=True) + EPS) * g


def _params(sem):
    return pltpu.CompilerParams(dimension_semantics=sem, vmem_limit_bytes=VMEM_LIMIT)


def _memkv_kernel(mem_ref, g_ref, w_ref, k_ref, v_ref):
    d = mem_ref.shape[-1]
    mn = _rms(mem_ref[0], g_ref[...]).astype(BF16)
    kv = jnp.dot(mn, w_ref[...], preferred_element_type=F32)
    k_ref[0] = kv[:, :d].astype(BF16)
    v_ref[0] = kv[:, d:].astype(BF16)


def _memkv(mem, g, w_xkv):
    b, m, d = mem.shape
    return pl.pallas_call(
        _memkv_kernel,
        grid=(b,),
        in_specs=[pl.BlockSpec((1, m, d), lambda i: (i, 0, 0)),
                  pl.BlockSpec((1, d), lambda i: (0, 0)),
                  pl.BlockSpec((d, 2 * d), lambda i: (0, 0))],
        out_specs=[pl.BlockSpec((1, m, d), lambda i: (i, 0, 0)),
                   pl.BlockSpec((1, m, d), lambda i: (i, 0, 0))],
        out_shape=[jax.ShapeDtypeStruct((b, m, d), BF16)] * 2,
        compiler_params=_params(("arbitrary",)),
        name="memkv",
    )(mem, g, w_xkv)


def _mixer_in_kernel(x_ref, gmix_ref, win_ref, convw_ref, convb_ref, wg_ref, bg_ref, lam_ref, glru_ref,
                     cos_ref, sina_ref, sinb_ref,
                     ylru_ref, q_ref, k_ref, v_ref, xpad_ref, hcar_ref, ascan_ref, uscan_ref):
    c = ylru_ref.shape[-1]
    si = pl.program_id(1)

    @pl.when(si == 0)
    def _():
        xpad_ref[0:SUBLANES, :] = jnp.zeros((SUBLANES, c), F32)
        hcar_ref[...] = jnp.zeros_like(hcar_ref)

    n_sub = x_ref.shape[1] // ascan_ref.shape[1]
    ts = ascan_ref.shape[1]
    projs = []
    for sub in range(n_sub):
        xn = _rms(x_ref[0, sub * ts:(sub + 1) * ts, :], gmix_ref[...]).astype(BF16)
        projs.append(jnp.dot(xn, win_ref[...], preferred_element_type=F32))
    for sub in range(n_sub):
        _mixer_stage2(projs[sub], slice(sub * ts, (sub + 1) * ts), convw_ref, convb_ref, wg_ref, bg_ref, lam_ref,
                      glru_ref, cos_ref, sina_ref, sinb_ref, ylru_ref, q_ref, k_ref, v_ref, xpad_ref, hcar_ref,
                      ascan_ref, uscan_ref)


def _mixer_stage2(proj, rows_out, convw_ref, convb_ref, wg_ref, bg_ref, lam_ref, glru_ref, cos_ref, sina_ref,
                  sinb_ref, ylru_ref, q_ref, k_ref, v_ref, xpad_ref, hcar_ref, ascan_ref, uscan_ref):
    ts = proj.shape[0]
    c = ylru_ref.shape[-1]
    x_lru = proj[:, 0:c]
    g_lru = proj[:, c:2 * c]
    q = proj[:, 2 * c:3 * c]
    k = proj[:, 3 * c:4 * c]
    v = proj[:, 4 * c:5 * c]

    xpad_ref[SUBLANES:SUBLANES + ts, :] = x_lru
    cw = convw_ref[...]
    xc = convb_ref[...] + cw[3:4] * x_lru
    for j in range(1, CONV_WIDTH):
        xc = xc + cw[3 - j:4 - j] * xpad_ref[SUBLANES - j:SUBLANES - j + ts, :]
    xpad_ref[0:SUBLANES, :] = x_lru[ts - SUBLANES:ts, :]

    gates = jnp.dot(xc.astype(BF16), wg_ref[...], preferred_element_type=F32) + bg_ref[...]
    r = jax.nn.sigmoid(gates[:, 0:c])
    i = jax.nn.sigmoid(gates[:, c:2 * c])
    neg_lam = -lam_ref[...]
    softplus = jnp.maximum(neg_lam, 0.0) + jnp.log1p(jnp.exp(-jnp.abs(neg_lam)))
    log_a = -RG_C * r * softplus
    a = jnp.exp(log_a)
    u = jnp.sqrt(-jnp.tanh(log_a) * (a * a + 1.0)) * (i * xc)

    groups = ts // SUBLANES
    cols = c // LANES
    for j in range(cols):
        ascan_ref[j] = a[:, j * LANES:(j + 1) * LANES]
        uscan_ref[j] = u[:, j * LANES:(j + 1) * LANES]
    for r in range(SUBLANES):
        rows = pl.ds(r, groups, stride=SUBLANES)
        a_r = jnp.concatenate([ascan_ref[j, rows, :] for j in range(cols)], axis=1)
        u_r = jnp.concatenate([uscan_ref[j, rows, :] for j in range(cols)], axis=1)
        if r:
            u_r = a_r * u_acc + u_r
            a_r = a_r * a_acc
            for j in range(cols):
                ascan_ref[j, rows, :] = a_r[:, j * LANES:(j + 1) * LANES]
                uscan_ref[j, rows, :] = u_r[:, j * LANES:(j + 1) * LANES]
        a_acc, u_acc = a_r, u_r
    carry = hcar_ref[...]
    hs = []
    for gi in range(groups):
        blk = slice(gi * SUBLANES, (gi + 1) * SUBLANES)
        a_g = jnp.concatenate([ascan_ref[j, blk, :] for j in range(cols)], axis=1)
        u_g = jnp.concatenate([uscan_ref[j, blk, :] for j in range(cols)], axis=1)
        hs.append(a_g * carry + u_g)
        carry = a_acc[gi:gi + 1] * carry + u_acc[gi:gi + 1]
    h = jnp.concatenate(hs, axis=0)
    hcar_ref[...] = carry

    y = h * jax.nn.gelu(g_lru)
    ylru_ref[0, rows_out, :] = _rms(y, glru_ref[...]).astype(BF16)

    wide = lambda t_ref: jnp.concatenate([t_ref[rows_out, :]] * (c // LANES), axis=1)
    cos, sina, sinb = wide(cos_ref), wide(sina_ref), wide(sinb_ref)
    half = ROT_DIM // 2

    def rope(t):
        return t * cos + pltpu.roll(t, c - half, axis=1) * sina + pltpu.roll(t, half, axis=1) * sinb

    q_ref[0, rows_out, :] = rope(q)
    k_ref[0, rows_out, :] = rope(k)
    v_ref[0, rows_out, :] = v.astype(BF16)


def _mixer_in(x, gmix, w_in, conv_w, conv_b, w_gates, b_gates, lam, glru, cos_t, sina_t, sinb_t, ts):
    b, s, d = x.shape
    c = conv_w.shape[-1]
    sub = min(ts, MIXER_SUB_ROWS)
    full = lambda shape: pl.BlockSpec(shape, lambda bi, si: (0,) * len(shape))
    tab = pl.BlockSpec((ts, LANES), lambda bi, si: (si, 0))
    seq = lambda: pl.BlockSpec((1, ts, c), lambda bi, si: (bi, si, 0))
    return pl.pallas_call(
        _mixer_in_kernel,
        grid=(b, s // ts),
        in_specs=[pl.BlockSpec((1, ts, d), lambda bi, si: (bi, si, 0)),
                  full((1, d)), full(w_in.shape), full(conv_w.shape), full((1, c)),
                  full(w_gates.shape), full((1, 2 * c)), full((1, c)), full((1, c)),
                  tab, tab, tab],
        out_specs=[seq(), seq(), seq(), seq()],
        out_shape=[jax.ShapeDtypeStruct((b, s, c), BF16),
                   jax.ShapeDtypeStruct((b, s, c), F32),
                   jax.ShapeDtypeStruct((b, s, c), F32),
                   jax.ShapeDtypeStruct((b, s, c), BF16)],
        scratch_shapes=[pltpu.VMEM((sub + SUBLANES, c), F32), pltpu.VMEM((1, c), F32),
                        pltpu.VMEM((c // LANES, sub, LANES), F32), pltpu.VMEM((c // LANES, sub, LANES), F32)],
        compiler_params=_params(("arbitrary", "arbitrary")),
        name="mixer_in",
    )(x, gmix, w_in, conv_w, conv_b, w_gates, b_gates, lam, glru, cos_t, sina_t, sinb_t)


def _moba_kernel(q_ref, k_ref, v_ref, kpat_ref, o_ref, qa_ref, ka_ref):
    s_len = q_ref.shape[1]
    nb = s_len // MOBA_BLOCK
    n_pick = min(MOBA_TOPK, nb)
    q = q_ref[0]
    k = k_ref[0]
    lane = lax.broadcasted_iota(jnp.int32, (1, LANES), 1)
    kmean = jnp.concatenate(
        [jnp.sum(k[n * MOBA_BLOCK:(n + 1) * MOBA_BLOCK], axis=0, keepdims=True) for n in range(nb)],
        axis=0) * (1.0 / MOBA_BLOCK)

    blk_row = lax.broadcasted_iota(jnp.int32, (nb, s_len), 0)
    q_blk = lax.broadcasted_iota(jnp.int32, (nb, s_len), 1) // MOBA_BLOCK
    past = blk_row < q_blk

    q_hi = q.astype(BF16)
    q_lo = (q - q_hi.astype(F32)).astype(BF16)
    for h in range(2):
        own = (lane >= h * HEAD_DIM) & (lane < (h + 1) * HEAD_DIM)
        off = (1 - h) * HEAD_DIM
        km = jnp.where(own, kmean, 0.0)
        km_hi = km.astype(BF16)
        km_split = jnp.concatenate([km_hi, (km - km_hi.astype(F32)).astype(BF16)], axis=0)
        parts = (lax.dot_general(km_split, q_hi, NT_DIMS, preferred_element_type=F32)
                 + lax.dot_general(km_split, q_lo, NT_DIMS, preferred_element_type=F32))
        gate = jnp.where(past, parts[0:nb] + parts[nb:2 * nb], -jnp.inf)
        rank = jnp.zeros((nb, s_len), jnp.int32)
        for m in range(nb):
            gm = gate[m:m + 1, :]
            ahead = (gm > gate) | ((gm == gate) & (m < blk_row))
            rank = rank + ahead.astype(jnp.int32)
        allowed = ((rank < n_pick) & past) | (blk_row == q_blk)
        bias = jnp.where(allowed, 0.0, NEG_BIG)
        pieces = []
        if off:
            pieces.append(jnp.zeros((off, s_len), F32))
        pieces.append(bias)
        if LANES - off - nb:
            pieces.append(jnp.zeros((LANES - off - nb, s_len), F32))
        bias_lanes = jnp.concatenate(pieces, axis=0).T
        qa_ref[h] = jnp.where(own, q * (HEAD_DIM ** -0.5 * LOG2_E), bias_lanes).astype(BF16)
        ka_ref[h] = jnp.where(own, k.astype(BF16), kpat_ref[h])

    own0 = lane < HEAD_DIM

    def scores(qi, h):
        nk = (qi + 1) * MOBA_BLOCK
        qa = qa_ref[h, qi * MOBA_BLOCK:(qi + 1) * MOBA_BLOCK, :]
        return lax.dot_general(qa, ka_ref[h, 0:nk, :], NT_DIMS, preferred_element_type=F32)

    key_in_blk = lax.broadcasted_iota(jnp.int32, (MOBA_BLOCK, MOBA_BLOCK), 1)
    q_in_blk = lax.broadcasted_iota(jnp.int32, (MOBA_BLOCK, MOBA_BLOCK), 0)
    causal = key_in_blk <= q_in_blk
    units = [(qi, h) for qi in range(nb) for h in range(2)]
    ahead = [scores(*u) for u in units[:MOBA_LOOKAHEAD]]
    outs = []

    def weighted_values(p, l, qi, h):
        outs.append(jnp.dot(p, v_ref[0, 0:(qi + 1) * MOBA_BLOCK, :], preferred_element_type=F32) / l)
        if h:
            o_ref[0, qi * MOBA_BLOCK:(qi + 1) * MOBA_BLOCK, :] = jnp.where(own0, outs[0], outs[1])
            outs.clear()

    pending = None
    for n, (qi, h) in enumerate(units):
        s = ahead.pop(0)
        if n + MOBA_LOOKAHEAD < len(units):
            ahead.append(scores(*units[n + MOBA_LOOKAHEAD]))
        n_past = qi * MOBA_BLOCK
        s_own = jnp.where(causal, s[:, n_past:], NEG_BIG)
        s = jnp.concatenate([s[:, :n_past], s_own], axis=1) if qi else s_own
        p = jnp.exp2(s - jnp.max(s, axis=-1, keepdims=True))
        l = jnp.sum(p, axis=-1, keepdims=True)
        if pending is not None:
            weighted_values(*pending)
        pending = (p.astype(BF16), l, qi, h)
    weighted_values(*pending)


def _key_block_pattern(s):
    pat = np.zeros((2, s, LANES), np.float32)
    for h in range(2):
        off = (1 - h) * HEAD_DIM
        pat[h, np.arange(s), off + np.arange(s) // MOBA_BLOCK] = 1.0
    return jnp.asarray(pat, BF16)


def _moba(q, k, v):
    b, s, c = q.shape
    spec = lambda: pl.BlockSpec((1, s, LANES), lambda bi, hi: (bi, 0, hi))
    return pl.pallas_call(
        _moba_kernel,
        grid=(b, c // LANES),
        in_specs=[spec(), spec(), spec(), pl.BlockSpec((2, s, LANES), lambda bi, hi: (0, 0, 0))],
        out_specs=spec(),
        out_shape=jax.ShapeDtypeStruct((b, s, c), F32),
        scratch_shapes=[pltpu.VMEM((2, s, LANES), BF16), pltpu.VMEM((2, s, LANES), BF16)],
        compiler_params=_params(("arbitrary", "arbitrary")),
        name="moba",
    )(q, k, v, _key_block_pattern(s))


def _post_kernel(x_ref, ylru_ref, yatt_ref, gatt_ref, woa_ref, wob_ref, gcross_ref, wxq_ref, kx_ref, vx_ref,
                 wxo_ref, gffn_ref, wr_ref, br_ref, tri_ref, elow_ref,
                 h2_ref, hn2_ref, gcol_ref, cnt_ref, pos_ref, tcnt_ref, carry_ref):
    tm = x_ref.shape[1]
    d = x_ref.shape[2]
    xd = d // N_XHEADS
    first = (pl.program_id(0) == 0) & (pl.program_id(1) == 0)

    @pl.when(first)
    def _():
        carry_ref[...] = jnp.zeros_like(carry_ref)

    ya = _rms(yatt_ref[0], gatt_ref[...]).astype(BF16)
    mix = (jnp.dot(ylru_ref[0], woa_ref[...], preferred_element_type=F32)
           + jnp.dot(ya, wob_ref[...], preferred_element_type=F32))
    h1 = x_ref[0] + mix

    hn = _rms(h1, gcross_ref[...]).astype(BF16)
    qx = jnp.dot(hn, wxq_ref[...], preferred_element_type=F32)
    heads = []
    for hh in range(N_XHEADS):
        qh = (qx[:, hh * xd:(hh + 1) * xd] * (xd ** -0.5)).astype(BF16)
        s = lax.dot_general(qh, kx_ref[0, :, hh * xd:(hh + 1) * xd], NT_DIMS, preferred_element_type=F32)
        p = jnp.exp(s - jnp.max(s, axis=-1, keepdims=True))
        l = jnp.sum(p, axis=-1, keepdims=True)
        o = jnp.dot(p.astype(BF16), vx_ref[0, :, hh * xd:(hh + 1) * xd], preferred_element_type=F32) / l
        heads.append(o.astype(BF16))
    h2 = h1 + jnp.dot(jnp.concatenate(heads, axis=1), wxo_ref[...], preferred_element_type=F32)
    hn2 = _rms(h2, gffn_ref[...])
    h2_ref[...] = h2
    hn2_ref[...] = hn2.astype(BF16)

    x_hi = hn2.astype(BF16)
    x_lo = (hn2 - x_hi.astype(F32)).astype(BF16)
    by_hi = lax.dot_general(wr_ref[...], x_hi, NT_DIMS, preferred_element_type=F32)
    by_lo = lax.dot_general(wr_ref[0:ROUTER_ROWS], x_lo, NT_DIMS, preferred_element_type=F32)
    logits = by_hi[0:ROUTER_ROWS] + by_hi[ROUTER_ROWS:2 * ROUTER_ROWS] + by_lo + br_ref[...]
    gl = logits[0:N_GROUPS]
    gmax = jnp.max(gl, axis=0, keepdims=True)
    gi = lax.broadcasted_iota(jnp.int32, gl.shape, 0)
    grp = jnp.min(jnp.where(gl == gmax, gi, N_GROUPS), axis=0, keepdims=True)
    g_w = 1.0 / jnp.sum(jnp.exp(gl - gmax), axis=0, keepdims=True)
    el = jnp.zeros((EXPERTS_PER_GROUP, tm), F32)
    for g in range(N_GROUPS):
        lo = SUBLANES + g * EXPERTS_PER_GROUP
        el = jnp.where(grp == g, logits[lo:lo + EXPERTS_PER_GROUP], el)
    ee = jnp.exp(el - jnp.max(el, axis=0, keepdims=True))
    ep = ee / jnp.sum(ee, axis=0, keepdims=True)
    ei = lax.broadcasted_iota(jnp.int32, ep.shape, 0)
    p1 = jnp.max(ep, axis=0, keepdims=True)
    i1 = jnp.min(jnp.where(ep == p1, ei, EXPERTS_PER_GROUP), axis=0, keepdims=True)
    ep_rest = jnp.where(ei == i1, -1.0, ep)
    p2 = jnp.max(ep_rest, axis=0, keepdims=True)
    i2 = jnp.min(jnp.where(ep_rest == p2, ei, EXPERTS_PER_GROUP), axis=0, keepdims=True)
    den = p1 + p2
    gate1 = g_w * p1 / den
    gate2 = g_w * p2 / den
    e1 = grp * EXPERTS_PER_GROUP + i1
    e2 = grp * EXPERTS_PER_GROUP + i2
    li = lax.broadcasted_iota(jnp.int32, (LANES, tm), 0)
    gcol_ref[...] = jnp.where(li == 0, gate1, jnp.where(li == 1, gate2, 0.0)).T

    xi = lax.broadcasted_iota(jnp.int32, (N_EXPERTS, tm), 0)
    oh1 = xi == e1
    oh2 = xi == e2
    cnt = oh1.astype(F32) + oh2.astype(F32)
    tile_cnt = jnp.broadcast_to(jnp.sum(cnt, axis=1, keepdims=True), (N_EXPERTS, LANES))
    carry_ref[...] = carry_ref[...] + tile_cnt
    cnt_ref[...] = carry_ref[...]
    tcnt_ref[...] = tile_cnt
    in_tile = jnp.dot(cnt.astype(BF16), tri_ref[...], preferred_element_type=F32)
    tile_start = jnp.dot(elow_ref[...], tile_cnt, precision=lax.Precision.HIGHEST,
                         preferred_element_type=F32)[:, 0:1]
    local = in_tile + tile_start
    pos_ref[0:1, :] = jnp.sum(jnp.where(oh1, local, 0.0), axis=0, keepdims=True).astype(jnp.int32)
    pos_ref[1:2, :] = jnp.sum(jnp.where(oh2, local, 0.0), axis=0, keepdims=True).astype(jnp.int32)


def _post(x, ylru, yatt, gatt, wo_a, wo_b, gcross, wxq, kx, vx, wxo, gffn, wr_t, br_t, tri, tm):
    b, s, d = x.shape
    c = ylru.shape[-1]
    m = kx.shape[1]
    n = b * s
    nt = s // tm
    full = lambda shape: pl.BlockSpec(shape, lambda bi, si: (0,) * len(shape))
    tok = lambda rows: pl.BlockSpec((rows, tm), lambda bi, si: (0, bi * nt + si))
    return pl.pallas_call(
        _post_kernel,
        grid=(b, nt),
        in_specs=[pl.BlockSpec((1, tm, d), lambda bi, si: (bi, si, 0)),
                  pl.BlockSpec((1, tm, c), lambda bi, si: (bi, si, 0)),
                  pl.BlockSpec((1, tm, c), lambda bi, si: (bi, si, 0)),
                  full((1, c)), full((c, d)), full((c, d)), full((1, d)), full((d, d)),
                  pl.BlockSpec((1, m, d), lambda bi, si: (bi, 0, 0)),
                  pl.BlockSpec((1, m, d), lambda bi, si: (bi, 0, 0)),
                  full((d, d)), full((1, d)), full(wr_t.shape), full(br_t.shape), full((tm, tm)),
                  full((N_EXPERTS, N_EXPERTS))],
        out_specs=[pl.BlockSpec((tm, d), lambda bi, si: (bi * nt + si, 0)),
                   pl.BlockSpec((tm, d), lambda bi, si: (bi * nt + si, 0)),
                   pl.BlockSpec((tm, LANES), lambda bi, si: (bi * nt + si, 0)),
                   full((N_EXPERTS, LANES)),
                   tok(EXPERT_TOPK),
                   pl.BlockSpec((N_EXPERTS, LANES), lambda bi, si: (bi * nt + si, 0))],
        out_shape=[jax.ShapeDtypeStruct((n, d), F32),
                   jax.ShapeDtypeStruct((n, d), BF16),
                   jax.ShapeDtypeStruct((n, LANES), F32),
                   jax.ShapeDtypeStruct((N_EXPERTS, LANES), F32),
                   jax.ShapeDtypeStruct((EXPERT_TOPK, n), jnp.int32),
                   jax.ShapeDtypeStruct((n // tm * N_EXPERTS, LANES), F32)],
        scratch_shapes=[pltpu.VMEM((N_EXPERTS, LANES), F32)],
        compiler_params=_params(("arbitrary", "arbitrary")),
        name="post",
    )(x, ylru, yatt, gatt, wo_a, wo_b, gcross, wxq, kx, vx, wxo, gffn, wr_t, br_t, tri,
      jnp.asarray(np.tril(np.ones((N_EXPERTS, N_EXPERTS), np.float32), -1)))


def _pieces(largest):
    piece = largest
    while piece:
        yield piece
        piece //= 2


def _start_run_copies(length, largest, copy_of):
    def start(pieces):
        for piece in pieces:
            done = length & ~(2 * piece - 1)

            @pl.when((length & piece) != 0)
            def _():
                copy_of(done, piece).start()

    @pl.when(length >= RUN_SPLIT)
    def _():
        start([p for p in _pieces(largest) if p >= RUN_SPLIT])

    start([p for p in _pieces(largest) if p < RUN_SPLIT])


def _tile_rows(row, count):
    return pl.ds(pl.multiple_of(row * SUBLANES, SUBLANES), count * SUBLANES)


def _dispatch_kernel(fill_ref, rstart_ref, rlen_ref, hn_ref, pos_ref, xout_ref, sorted_ref, zero_ref, sem, zsem):
    tm = hn_ref.shape[0]
    n_sorted = EXPERT_TOPK * tm
    step = pl.program_id(0)

    zero_rows = zero_ref.shape[0] // SUBLANES
    n_rows = xout_ref.shape[0] // SUBLANES

    def pad_copies(e, start):
        first = fill_ref[e]
        length = fill_ref[N_EXPERTS + e]
        for piece in _pieces(zero_rows):
            offset = first + (length & ~(2 * piece - 1))

            @pl.when((length & piece) != 0)
            def _():
                cp = pltpu.make_async_copy(zero_ref.at[_tile_rows(0, piece)], xout_ref.at[_tile_rows(offset, piece)],
                                           zsem)
                cp.start() if start else cp.wait()

    def tail_copies(start):
        def one(i, carry):
            cp = pltpu.make_async_copy(zero_ref, xout_ref.at[_tile_rows(i * zero_rows, zero_rows)], zsem)
            cp.start() if start else cp.wait()
            return carry

        lax.fori_loop(fill_ref[2 * N_EXPERTS] // zero_rows, n_rows // zero_rows, one, 0)

    @pl.when(step == 0)
    def _():
        zero_ref[...] = jnp.zeros_like(zero_ref)
        lax.fori_loop(0, N_EXPERTS, lambda e, c: (pad_copies(e, True), c)[1], 0)
        tail_copies(True)

    pos = pos_ref[...]
    p_iota = lax.broadcasted_iota(jnp.int32, (n_sorted, tm), 0)
    picks = jnp.where((p_iota == pos[0:1, :]) | (p_iota == pos[1:2, :]), 1.0, 0.0).astype(BF16)
    xs = jnp.dot(picks, hn_ref[...], preferred_element_type=F32)
    for k in range(SUBLANES):
        sorted_ref[pl.ds(k, n_sorted, stride=SUBLANES), :] = xs[:, k * LANES:(k + 1) * LANES]

    def run(e, offset):
        length = rlen_ref[step * N_EXPERTS + e]
        first = rstart_ref[step * N_EXPERTS + e]

        _start_run_copies(length, tm, lambda done, piece: pltpu.make_async_copy(
            sorted_ref.at[_tile_rows(offset + done, piece)], xout_ref.at[_tile_rows(first + done, piece)], sem))
        return offset + length

    lax.fori_loop(0, N_EXPERTS, run, 0)

    pltpu.make_async_copy(sorted_ref, xout_ref.at[_tile_rows(0, n_sorted)], sem).wait()

    @pl.when(step == 0)
    def _():
        lax.fori_loop(0, N_EXPERTS, lambda e, c: (pad_copies(e, False), c)[1], 0)
        tail_copies(False)


def _dispatch(fill, run_start, run_len, hn2, pos, n_rows, tm):
    n, d = hn2.shape
    assert d == SUBLANES * LANES, "a row must be exactly one (8, 128) tile for the tile-contiguous layout"
    return pl.pallas_call(
        _dispatch_kernel,
        grid_spec=pltpu.PrefetchScalarGridSpec(
            num_scalar_prefetch=3,
            grid=(n // tm,),
            in_specs=[pl.BlockSpec((tm, d), lambda i, *_: (i, 0)),
                      pl.BlockSpec((EXPERT_TOPK, tm), lambda i, *_: (0, i))],
            out_specs=pl.BlockSpec(memory_space=pl.ANY),
            scratch_shapes=[pltpu.VMEM((EXPERT_TOPK * tm * SUBLANES, LANES), F32),
                            pltpu.VMEM((EXPERT_ROWS // 2 * SUBLANES, LANES), F32),
                            pltpu.SemaphoreType.DMA(()), pltpu.SemaphoreType.DMA(())]),
        out_shape=jax.ShapeDtypeStruct((n_rows * SUBLANES, LANES), F32),
        compiler_params=_params(("arbitrary",)),
        name="dispatch",
    )(fill, run_start, run_len, hn2, pos)


def _experts_kernel(be_ref, nu_ref, kfirst_ref, kend_ref, rfirst_ref, rlen_ref, rdst_ref,
                    x_ref, wgu_ref, wd_ref, yts_ref, wgu_bf_ref, wd_bf_ref, ybuf_ref, sems):
    j = pl.program_id(0)
    rows = x_ref.shape[0] // SUBLANES
    f = wd_ref.shape[1]
    n_used = nu_ref[0]
    par = j % 2
    last_blk = be_ref.shape[0] - 1
    pieces = SUBLANES
    chunk = rows // EXPERT_CHUNKS

    def start_copies_of_previous():
        blk_first = (j - 1) * rows

        def run(k, carry):
            lo = jnp.maximum(rfirst_ref[k], blk_first)
            hi = jnp.minimum(rfirst_ref[k] + rlen_ref[k], blk_first + rows)
            length = jnp.maximum(hi - lo, 0)
            src = lo - blk_first
            dst = rdst_ref[k] + (lo - rfirst_ref[k])
            _start_run_copies(length, rows, lambda done, piece: pltpu.make_async_copy(
                ybuf_ref.at[1 - par, _tile_rows(src + done, piece)], yts_ref.at[_tile_rows(dst + done, piece)],
                sems.at[1 - par]))
            return carry

        lax.fori_loop(kfirst_ref[j - 1], kend_ref[j - 1], run, 0)

    def wait_copies(blk_par):
        pltpu.make_async_copy(ybuf_ref.at[blk_par], yts_ref.at[pl.ds(0, rows * pieces)], sems.at[blk_par]).wait()

    def mlp(part):
        x = jnp.concatenate([x_ref[pl.ds(part * chunk * pieces + k, chunk, stride=pieces), :] for k in range(pieces)],
                            axis=1)
        gu = jnp.dot(x.astype(BF16), wgu_bf_ref[...], preferred_element_type=F32)
        act = jax.nn.silu(gu[:, 0:f]) * gu[:, f:2 * f]
        y = jnp.dot(act.astype(BF16), wd_bf_ref[...], preferred_element_type=F32)
        for k in range(pieces):
            ybuf_ref[par, pl.ds(part * chunk * pieces + k, chunk, stride=pieces), :] = y[:, k * LANES:(k + 1) * LANES]

    jc = jnp.minimum(j, last_blk)
    @pl.when((j < n_used) & ((j == 0) | (be_ref[jc] != be_ref[jnp.maximum(jc, 1) - 1])))
    def _():
        wgu_bf_ref[...] = wgu_ref[0].astype(BF16)
        wd_bf_ref[...] = wd_ref[0].astype(BF16)

    @pl.when((j >= 2) & (j <= n_used))
    def _():
        wait_copies(par)

    @pl.when(j == 0)
    def _():
        ybuf_ref[1] = jnp.zeros(ybuf_ref.shape[1:], F32)
        cp = pltpu.make_async_copy(ybuf_ref.at[1], yts_ref.at[pl.ds(yts_ref.shape[0] - rows * pieces, rows * pieces)],
                                   sems.at[1])
        cp.start()
        cp.wait()

    @pl.when((j >= 1) & (j <= n_used))
    def _():
        start_copies_of_previous()

    @pl.when(j < n_used)
    def _():
        for part in range(EXPERT_CHUNKS):
            mlp(part)

    @pl.when((j >= 1) & (j == n_used))
    def _():
        wait_copies(1 - par)


def _experts(blk_expert, n_used, run_tables, x_buf, w_gate_up, w_down, n_out_rows):
    d = w_gate_up.shape[1]
    f = w_down.shape[1]
    r = EXPERT_ROWS
    n_blk = x_buf.shape[0] // (r * SUBLANES)
    assert d == SUBLANES * LANES and x_buf.shape[1] == LANES, "rows are (8, 128) tiles (tile-contiguous layout)"
    x_map = lambda j, be, nu, *_: (jnp.minimum(j, jnp.maximum(nu[0], 1) - 1), 0)
    w_map = lambda j, be, nu, *_: (be[jnp.minimum(j, n_blk - 1)], 0, 0)
    return pl.pallas_call(
        _experts_kernel,
        grid_spec=pltpu.PrefetchScalarGridSpec(
            num_scalar_prefetch=7,
            grid=(n_blk + 1,),
            in_specs=[pl.BlockSpec((r * SUBLANES, LANES), x_map),
                      pl.BlockSpec((1, d, 2 * f), w_map),
                      pl.BlockSpec((1, f, d), w_map)],
            out_specs=pl.BlockSpec(memory_space=pl.ANY),
            scratch_shapes=[pltpu.VMEM((d, 2 * f), BF16), pltpu.VMEM((f, d), BF16),
                            pltpu.VMEM((2, r * SUBLANES, LANES), F32), pltpu.SemaphoreType.DMA((2,))]),
        out_shape=jax.ShapeDtypeStruct((n_out_rows * SUBLANES, LANES), F32),
        compiler_params=_params(("arbitrary",)),
        name="experts",
    )(blk_expert, n_used, *run_tables, x_buf, w_gate_up, w_down)


def _combine_kernel(pos_ref, h2_ref, gcol_ref, gfin_ref, yts_ref, o_ref, pick0_ref, pick1_ref):
    tc = h2_ref.shape[0]
    n = pos_ref.shape[0] // EXPERT_TOPK
    base = pl.program_id(0) * tc
    for t in range(tc):
        for slot, pick_ref in enumerate((pick0_ref, pick1_ref)):
            row = pos_ref[slot * n + base + t]
            pick_ref[t * SUBLANES:(t + 1) * SUBLANES, :] = yts_ref[_tile_rows(row, 1), :]

    def rows_of(y_ref):
        return jnp.concatenate([y_ref[pl.ds(k, tc, stride=SUBLANES), :] for k in range(SUBLANES)], axis=1)

    g = gcol_ref[...]
    moe = g[:, 0:1] * rows_of(pick0_ref) + g[:, 1:2] * rows_of(pick1_ref)
    o_ref[...] = _rms(h2_ref[...] + moe, gfin_ref[...])


def _combine(pos_flat, h2, gcol, gfin, y_ts, tc):
    n, d = h2.shape
    return pl.pallas_call(
        _combine_kernel,
        grid_spec=pltpu.PrefetchScalarGridSpec(
            num_scalar_prefetch=1,
            grid=(n // tc,),
            in_specs=[pl.BlockSpec((tc, d), lambda i, pos: (i, 0)),
                      pl.BlockSpec((tc, LANES), lambda i, pos: (i, 0)),
                      pl.BlockSpec((1, d), lambda i, pos: (0, 0)),
                      pl.BlockSpec((EXPERT_TOPK * tc * SUBLANES, LANES), lambda i, pos: (i, 0))],
            out_specs=pl.BlockSpec((tc, d), lambda i, pos: (i, 0)),
            scratch_shapes=[pltpu.VMEM((tc * SUBLANES, LANES), F32), pltpu.VMEM((tc * SUBLANES, LANES), F32)]),
        out_shape=jax.ShapeDtypeStruct((n, d), F32),
        compiler_params=_params(("arbitrary",)),
        name="combine",
    )(pos_flat, h2, gcol, gfin, y_ts)


def _rope_tables(s):
    half = ROT_DIM // 2
    inv_freq = (ROPE_THETA ** (-np.arange(0, ROT_DIM, 2, dtype=np.float32) / ROT_DIM)).astype(np.float32)
    ang = np.arange(s, dtype=np.float32)[:, None] * inv_freq[None, :]
    cos, sin = np.cos(ang), np.sin(ang)
    zeros = lambda w: np.zeros((s, w), np.float32)
    cos_h = np.concatenate([cos, cos, np.ones((s, HEAD_DIM - ROT_DIM), np.float32)], axis=1)
    sina_h = np.concatenate([-sin, zeros(HEAD_DIM - half)], axis=1)
    sinb_h = np.concatenate([zeros(half), sin, zeros(HEAD_DIM - ROT_DIM)], axis=1)
    tile = lambda t: jnp.asarray(np.tile(t, (1, LANES // HEAD_DIM)), F32)
    return tile(cos_h), tile(sina_h), tile(sinb_h)


def _block_diag(w):
    nblk, bi, bo = w.shape
    eye = jnp.eye(nblk, dtype=w.dtype)
    return jnp.einsum('hij,hg->higj', w, eye).reshape(nblk * bi, nblk * bo)


def _layer(h, mem, norm_mix, w_in, conv_w, conv_b, w_rg, b_rg, w_ig, b_ig, lru_lambda, norm_lru_out,
           norm_attn_out, w_out, norm_cross, norm_mem, w_xq, w_xkv, w_xo, norm_ffn, w_router_group,
           b_router_group, w_router_expert, b_router_expert, w_gate_up, w_down, norm_out):
    b, s, d = h.shape
    c = conv_w.shape[-1]
    n = b * s
    row = lambda t: t.reshape(1, -1)
    tm = 512 if s % 512 == 0 else MOBA_BLOCK
    ts = tm
    assert s % ts == 0 and s % tm == 0 and c == N_HEADS * HEAD_DIM

    kx, vx = _memkv(mem, row(norm_mem), w_xkv.astype(BF16))

    w_gates = jnp.concatenate([_block_diag(w_rg), _block_diag(w_ig)], axis=1).astype(BF16)
    b_gates = jnp.concatenate([b_rg, b_ig]).reshape(1, -1)
    cos_t, sina_t, sinb_t = _rope_tables(s)
    ylru, q, k, v = _mixer_in(h, row(norm_mix), w_in.astype(BF16), conv_w, row(conv_b), w_gates, b_gates,
                              row(lru_lambda), row(norm_lru_out), cos_t, sina_t, sinb_t, ts)
    yatt = _moba(q, k, v)

    w_out_b = w_out.astype(BF16)
    wr_t = jnp.zeros((ROUTER_ROWS, d), F32)
    wr_t = wr_t.at[0:N_GROUPS].set(w_router_group.T).at[SUBLANES:SUBLANES + N_EXPERTS].set(w_router_expert.T)
    wr_hi = wr_t.astype(BF16)
    wr_t = jnp.concatenate([wr_hi, (wr_t - wr_hi.astype(F32)).astype(BF16)], axis=0)
    br_t = jnp.zeros((ROUTER_ROWS, 1), F32)
    br_t = br_t.at[0:N_GROUPS, 0].set(b_router_group).at[SUBLANES:SUBLANES + N_EXPERTS, 0].set(b_router_expert)
    tri = jnp.asarray(np.triu(np.ones((tm, tm), np.float32), 1), BF16)
    h2, hn2, gcol, counts, pos, tile_cnt = _post(
        h, ylru, yatt, row(norm_attn_out), w_out_b[:c], w_out_b[c:], row(norm_cross), w_xq.astype(BF16),
        kx, vx, w_xo.astype(BF16), row(norm_ffn), wr_t, br_t, tri, tm)

    counts = counts[:, 0].astype(jnp.int32)
    padded = (counts + EXPERT_ROWS - 1) // EXPERT_ROWS * EXPERT_ROWS
    pad_ends = jnp.cumsum(padded)
    pad_starts = pad_ends - padded
    n_blocks = n * EXPERT_TOPK // EXPERT_ROWS + N_EXPERTS
    blk_first = jnp.arange(n_blocks, dtype=jnp.int32) * EXPERT_ROWS
    blk_expert = jnp.minimum(jnp.sum(blk_first[:, None] >= pad_ends[None, :], axis=1), N_EXPERTS - 1).astype(jnp.int32)
    n_used = (pad_ends[-1:] // EXPERT_ROWS).astype(jnp.int32)
    fill = jnp.concatenate([pad_starts + counts, padded - counts, pad_ends[-1:]]).astype(jnp.int32)

    n_tiles = n // tm
    tile_cnt = tile_cnt[:, 0].astype(jnp.int32).reshape(n_tiles, N_EXPERTS)
    run_start = (pad_starts[None, :] + jnp.cumsum(tile_cnt, axis=0) - tile_cnt).astype(jnp.int32)
    x_buf = _dispatch(fill, run_start.reshape(-1), tile_cnt.reshape(-1), hn2, pos, n_blocks * EXPERT_ROWS, tm)

    in_tile = jnp.cumsum(tile_cnt, axis=1) - tile_cnt
    out_row = jnp.arange(n_tiles, dtype=jnp.int32)[:, None] * (EXPERT_TOPK * tm) + in_tile
    pad_first = pad_starts + counts
    run_first = jnp.concatenate([run_start.T, pad_first[:, None]], axis=1).reshape(-1)
    run_size = jnp.concatenate([tile_cnt.T, (padded - counts)[:, None]], axis=1).reshape(-1)
    run_out = jnp.concatenate([out_row.T, (EXPERT_TOPK * n + pad_first % EXPERT_ROWS)[:, None]], axis=1).reshape(-1)
    blk_end = blk_first + EXPERT_ROWS
    k_first = jnp.sum((run_first + run_size)[None, :] <= blk_first[:, None], axis=1)
    k_end = jnp.sum(run_first[None, :] < blk_end[:, None], axis=1)
    tables = [t.astype(jnp.int32) for t in (k_first, k_end, run_first, run_size, run_out)]
    y_ts = _experts(blk_expert, n_used, tables, x_buf, w_gate_up, w_down, EXPERT_TOPK * n + EXPERT_ROWS)
    out = _combine(pos.reshape(-1), h2, gcol, row(norm_out), y_ts, tm)
    return out.reshape(b, s, d)


def kernel(x, mem, norm_mix, w_in, conv_w, conv_b, w_rg, b_rg, w_ig, b_ig, lru_lambda, norm_lru_out, norm_attn_out,
           w_out, norm_cross, norm_mem, w_xq, w_xkv, w_xo, norm_ffn, w_router_group, b_router_group,
           w_router_expert, b_router_expert, w_gate_up, w_down, norm_final):
    depth = norm_mix.shape[0]
    assert depth == 1, "the fused final norm assumes a single layer"
    l = 0
    return _layer(x, mem, norm_mix[l], w_in[l], conv_w[l], conv_b[l], w_rg[l], b_rg[l], w_ig[l], b_ig[l],
                  lru_lambda[l], norm_lru_out[l], norm_attn_out[l], w_out[l], norm_cross[l], norm_mem[l],
                  w_xq[l], w_xkv[l], w_xo[l], norm_ffn[l], w_router_group[l], b_router_group[l],
                  w_router_expert[l], b_router_expert[l], w_gate_up[l], w_down[l], norm_final)
```

```python
import jax
import jax.numpy as jnp
import numpy as np
from jax import lax
from jax.experimental import pallas as pl
from jax.experimental.pallas import tpu as pltpu

F32 = jnp.float32
BF16 = jnp.bfloat16

CONV_WIDTH = 4
RG_C = 8.0
N_HEADS = 8
HEAD_DIM = 64
ROT_DIM = HEAD_DIM // 4
ROPE_THETA = 500000.0
MOBA_BLOCK = 256
MOBA_TOPK = 3
N_XHEADS = 4
N_GROUPS = 4
EXPERTS_PER_GROUP = 8
N_EXPERTS = N_GROUPS * EXPERTS_PER_GROUP
EXPERT_TOPK = 2
EPS = 1e-6

LANES = 128
SUBLANES = 8
NEG_BIG = -1e30
LOG2_E = 1.4426950408889634
MOBA_LOOKAHEAD = 3
EXPERT_ROWS = 512
RUN_SPLIT = 64
BF16_ROWS = 2 * SUBLANES
ROUTER_ROWS = -(-(SUBLANES + N_EXPERTS) // BF16_ROWS) * BF16_ROWS
VMEM_LIMIT = 56 * 1024 * 1024

NT_DIMS = (((1,), (1,)), ((), ()))


def _rms(x, g):
    return x * lax.rsqrt(jnp.mean(x * x, axis=-1, keepdims=True) + EPS) * g


def _params(sem):
    return pltpu.CompilerParams(dimension_semantics=sem, vmem_limit_bytes=VMEM_LIMIT)


def _memkv_kernel(mem_ref, g_ref, w_ref, k_ref, v_ref):
    d = mem_ref.shape[-1]
    mn = _rms(mem_ref[0], g_ref[...]).astype(BF16)
    kv = jnp.dot(mn, w_ref[...], preferred_element_type=F32)
    k_ref[0] = kv[:, :d].astype(BF16)
    v_ref[0] = kv[:, d:].astype(BF16)


def _memkv(mem, g, w_xkv):
    b, m, d = mem.shape
    return pl.pallas_call(
        _memkv_kernel,
        grid=(b,),
        in_specs=[pl.BlockSpec((1, m, d), lambda i: (i, 0, 0)),
                  pl.BlockSpec((1, d), lambda i: (0, 0)),
                  pl.BlockSpec((d, 2 * d), lambda i: (0, 0))],
        out_specs=[pl.BlockSpec((1, m, d), lambda i: (i, 0, 0)),
                   pl.BlockSpec((1, m, d), lambda i: (i, 0, 0))],
        out_shape=[jax.ShapeDtypeStruct((b, m, d), BF16)] * 2,
        compiler_params=_params(("arbitrary",)),
        name="memkv",
    )(mem, g, w_xkv)


def _mixer_in_kernel(x_ref, gmix_ref, win_ref, convw_ref, convb_ref, wg_ref, bg_ref, lam_ref, glru_ref,
                     cos_ref, sina_ref, sinb_ref,
                     ylru_ref, q_ref, k_ref, v_ref, xpad_ref, hcar_ref, ascan_ref, uscan_ref):
    c = ylru_ref.shape[-1]
    si = pl.program_id(1)

    @pl.when(si == 0)
    def _():
        xpad_ref[0:SUBLANES, :] = jnp.zeros((SUBLANES, c), F32)
        hcar_ref[...] = jnp.zeros_like(hcar_ref)

    ts = x_ref.shape[1]
    xn = _rms(x_ref[0], gmix_ref[...]).astype(BF16)
    proj = jnp.dot(xn, win_ref[...], preferred_element_type=F32)
    x_lru = proj[:, 0:c]
    g_lru = proj[:, c:2 * c]
    q = proj[:, 2 * c:3 * c]
    k = proj[:, 3 * c:4 * c]
    v = proj[:, 4 * c:5 * c]

    xpad_ref[SUBLANES:SUBLANES + ts, :] = x_lru
    cw = convw_ref[...]
    xc = convb_ref[...] + cw[3:4] * x_lru
    for j in range(1, CONV_WIDTH):
        xc = xc + cw[3 - j:4 - j] * xpad_ref[SUBLANES - j:SUBLANES - j + ts, :]
    xpad_ref[0:SUBLANES, :] = x_lru[ts - SUBLANES:ts, :]

    gates = jnp.dot(xc.astype(BF16), wg_ref[...], preferred_element_type=F32) + bg_ref[...]
    r = jax.nn.sigmoid(gates[:, 0:c])
    i = jax.nn.sigmoid(gates[:, c:2 * c])
    neg_lam = -lam_ref[...]
    softplus = jnp.maximum(neg_lam, 0.0) + jnp.log1p(jnp.exp(-jnp.abs(neg_lam)))
    log_a = -RG_C * r * softplus
    a = jnp.exp(log_a)
    u = jnp.sqrt(-jnp.tanh(log_a) * (a * a + 1.0)) * (i * xc)

    groups = ts // SUBLANES
    cols = c // LANES
    for j in range(cols):
        ascan_ref[j] = a[:, j * LANES:(j + 1) * LANES]
        uscan_ref[j] = u[:, j * LANES:(j + 1) * LANES]
    for r in range(SUBLANES):
        rows = pl.ds(r, groups, stride=SUBLANES)
        a_r = jnp.concatenate([ascan_ref[j, rows, :] for j in range(cols)], axis=1)
        u_r = jnp.concatenate([uscan_ref[j, rows, :] for j in range(cols)], axis=1)
        if r:
            u_r = a_r * u_acc + u_r
            a_r = a_r * a_acc
            for j in range(cols):
                ascan_ref[j, rows, :] = a_r[:, j * LANES:(j + 1) * LANES]
                uscan_ref[j, rows, :] = u_r[:, j * LANES:(j + 1) * LANES]
        a_acc, u_acc = a_r, u_r
    carry = hcar_ref[...]
    hs = []
    for gi in range(groups):
        blk = slice(gi * SUBLANES, (gi + 1) * SUBLANES)
        a_g = jnp.concatenate([ascan_ref[j, blk, :] for j in range(cols)], axis=1)
        u_g = jnp.concatenate([uscan_ref[j, blk, :] for j in range(cols)], axis=1)
        hs.append(a_g * carry + u_g)
        carry = a_acc[gi:gi + 1] * carry + u_acc[gi:gi + 1]
    h = jnp.concatenate(hs, axis=0)
    hcar_ref[...] = carry

    y = h * jax.nn.gelu(g_lru)
    ylru_ref[0] = _rms(y, glru_ref[...]).astype(BF16)

    wide = lambda t_ref: jnp.concatenate([t_ref[...]] * (c // LANES), axis=1)
    cos, sina, sinb = wide(cos_ref), wide(sina_ref), wide(sinb_ref)
    half = ROT_DIM // 2

    def rope(t):
        return t * cos + pltpu.roll(t, c - half, axis=1) * sina + pltpu.roll(t, half, axis=1) * sinb

    q_ref[0] = rope(q)
    k_ref[0] = rope(k)
    v_ref[0] = v.astype(BF16)


def _mixer_in(x, gmix, w_in, conv_w, conv_b, w_gates, b_gates, lam, glru, cos_t, sina_t, sinb_t, ts):
    b, s, d = x.shape
    c = conv_w.shape[-1]
    full = lambda shape: pl.BlockSpec(shape, lambda bi, si: (0,) * len(shape))
    tab = pl.BlockSpec((ts, LANES), lambda bi, si: (si, 0))
    seq = lambda: pl.BlockSpec((1, ts, c), lambda bi, si: (bi, si, 0))
    return pl.pallas_call(
        _mixer_in_kernel,
        grid=(b, s // ts),
        in_specs=[pl.BlockSpec((1, ts, d), lambda bi, si: (bi, si, 0)),
                  full((1, d)), full(w_in.shape), full(conv_w.shape), full((1, c)),
                  full(w_gates.shape), full((1, 2 * c)), full((1, c)), full((1, c)),
                  tab, tab, tab],
        out_specs=[seq(), seq(), seq(), seq()],
        out_shape=[jax.ShapeDtypeStruct((b, s, c), BF16),
                   jax.ShapeDtypeStruct((b, s, c), F32),
                   jax.ShapeDtypeStruct((b, s, c), F32),
                   jax.ShapeDtypeStruct((b, s, c), BF16)],
        scratch_shapes=[pltpu.VMEM((ts + SUBLANES, c), F32), pltpu.VMEM((1, c), F32),
                        pltpu.VMEM((c // LANES, ts, LANES), F32), pltpu.VMEM((c // LANES, ts, LANES), F32)],
        compiler_params=_params(("arbitrary", "arbitrary")),
        name="mixer_in",
    )(x, gmix, w_in, conv_w, conv_b, w_gates, b_gates, lam, glru, cos_t, sina_t, sinb_t)


def _moba_kernel(q_ref, k_ref, v_ref, kpat_ref, o_ref, qa_ref, ka_ref):
    s_len = q_ref.shape[1]
    nb = s_len // MOBA_BLOCK
    n_pick = min(MOBA_TOPK, nb)
    q = q_ref[0]
    k = k_ref[0]
    lane = lax.broadcasted_iota(jnp.int32, (1, LANES), 1)
    kmean = jnp.concatenate(
        [jnp.sum(k[n * MOBA_BLOCK:(n + 1) * MOBA_BLOCK], axis=0, keepdims=True) for n in range(nb)],
        axis=0) * (1.0 / MOBA_BLOCK)

    blk_row = lax.broadcasted_iota(jnp.int32, (nb, s_len), 0)
    q_blk = lax.broadcasted_iota(jnp.int32, (nb, s_len), 1) // MOBA_BLOCK
    past = blk_row < q_blk

    q_hi = q.astype(BF16)
    q_lo = (q - q_hi.astype(F32)).astype(BF16)
    for h in range(2):
        own = (lane >= h * HEAD_DIM) & (lane < (h + 1) * HEAD_DIM)
        off = (1 - h) * HEAD_DIM
        km = jnp.where(own, kmean, 0.0)
        km_hi = km.astype(BF16)
        km_split = jnp.concatenate([km_hi, (km - km_hi.astype(F32)).astype(BF16)], axis=0)
        parts = (lax.dot_general(km_split, q_hi, NT_DIMS, preferred_element_type=F32)
                 + lax.dot_general(km_split, q_lo, NT_DIMS, preferred_element_type=F32))
        gate = jnp.where(past, parts[0:nb] + parts[nb:2 * nb], -jnp.inf)
        rank = jnp.zeros((nb, s_len), jnp.int32)
        for m in range(nb):
            gm = gate[m:m + 1, :]
            ahead = (gm > gate) | ((gm == gate) & (m < blk_row))
            rank = rank + ahead.astype(jnp.int32)
        allowed = ((rank < n_pick) & past) | (blk_row == q_blk)
        bias = jnp.where(allowed, 0.0, NEG_BIG)
        pieces = []
        if off:
            pieces.append(jnp.zeros((off, s_len), F32))
        pieces.append(bias)
        if LANES - off - nb:
            pieces.append(jnp.zeros((LANES - off - nb, s_len), F32))
        bias_lanes = jnp.concatenate(pieces, axis=0).T
        qa_ref[h] = jnp.where(own, q * (HEAD_DIM ** -0.5 * LOG2_E), bias_lanes).astype(BF16)
        ka_ref[h] = jnp.where(own, k.astype(BF16), kpat_ref[h])

    own0 = lane < HEAD_DIM

    def scores(qi, h):
        nk = (qi + 1) * MOBA_BLOCK
        qa = qa_ref[h, qi * MOBA_BLOCK:(qi + 1) * MOBA_BLOCK, :]
        return lax.dot_general(qa, ka_ref[h, 0:nk, :], NT_DIMS, preferred_element_type=F32)

    key_in_blk = lax.broadcasted_iota(jnp.int32, (MOBA_BLOCK, MOBA_BLOCK), 1)
    q_in_blk = lax.broadcasted_iota(jnp.int32, (MOBA_BLOCK, MOBA_BLOCK), 0)
    causal = key_in_blk <= q_in_blk
    units = [(qi, h) for qi in range(nb) for h in range(2)]
    ahead = [scores(*u) for u in units[:MOBA_LOOKAHEAD]]
    outs = []

    def weighted_values(p, l, qi, h):
        outs.append(jnp.dot(p, v_ref[0, 0:(qi + 1) * MOBA_BLOCK, :], preferred_element_type=F32) / l)
        if h:
            o_ref[0, qi * MOBA_BLOCK:(qi + 1) * MOBA_BLOCK, :] = jnp.where(own0, outs[0], outs[1])
            outs.clear()

    pending = None
    for n, (qi, h) in enumerate(units):
        s = ahead.pop(0)
        if n + MOBA_LOOKAHEAD < len(units):
            ahead.append(scores(*units[n + MOBA_LOOKAHEAD]))
        n_past = qi * MOBA_BLOCK
        s_own = jnp.where(causal, s[:, n_past:], NEG_BIG)
        s = jnp.concatenate([s[:, :n_past], s_own], axis=1) if qi else s_own
        p = jnp.exp2(s - jnp.max(s, axis=-1, keepdims=True))
        l = jnp.sum(p, axis=-1, keepdims=True)
        if pending is not None:
            weighted_values(*pending)
        pending = (p.astype(BF16), l, qi, h)
    weighted_values(*pending)


def _key_block_pattern(s):
    pat = np.zeros((2, s, LANES), np.float32)
    for h in range(2):
        off = (1 - h) * HEAD_DIM
        pat[h, np.arange(s), off + np.arange(s) // MOBA_BLOCK] = 1.0
    return jnp.asarray(pat, BF16)


def _moba(q, k, v):
    b, s, c = q.shape
    spec = lambda: pl.BlockSpec((1, s, LANES), lambda bi, hi: (bi, 0, hi))
    return pl.pallas_call(
        _moba_kernel,
        grid=(b, c // LANES),
        in_specs=[spec(), spec(), spec(), pl.BlockSpec((2, s, LANES), lambda bi, hi: (0, 0, 0))],
        out_specs=spec(),
        out_shape=jax.ShapeDtypeStruct((b, s, c), F32),
        scratch_shapes=[pltpu.VMEM((2, s, LANES), BF16), pltpu.VMEM((2, s, LANES), BF16)],
        compiler_params=_params(("arbitrary", "arbitrary")),
        name="moba",
    )(q, k, v, _key_block_pattern(s))


def _post_kernel(x_ref, ylru_ref, yatt_ref, gatt_ref, woa_ref, wob_ref, gcross_ref, wxq_ref, kx_ref, vx_ref,
                 wxo_ref, gffn_ref, wr_ref, br_ref, tri_ref, elow_ref,
                 h2_ref, hn2_ref, gcol_ref, cnt_ref, pos_ref, tcnt_ref, carry_ref):
    tm = x_ref.shape[1]
    d = x_ref.shape[2]
    xd = d // N_XHEADS
    first = (pl.program_id(0) == 0) & (pl.program_id(1) == 0)

    @pl.when(first)
    def _():
        carry_ref[...] = jnp.zeros_like(carry_ref)

    ya = _rms(yatt_ref[0], gatt_ref[...]).astype(BF16)
    mix = (jnp.dot(ylru_ref[0], woa_ref[...], preferred_element_type=F32)
           + jnp.dot(ya, wob_ref[...], preferred_element_type=F32))
    h1 = x_ref[0] + mix

    hn = _rms(h1, gcross_ref[...]).astype(BF16)
    qx = jnp.dot(hn, wxq_ref[...], preferred_element_type=F32)
    heads = []
    for hh in range(N_XHEADS):
        qh = (qx[:, hh * xd:(hh + 1) * xd] * (xd ** -0.5)).astype(BF16)
        s = lax.dot_general(qh, kx_ref[0, :, hh * xd:(hh + 1) * xd], NT_DIMS, preferred_element_type=F32)
        p = jnp.exp(s - jnp.max(s, axis=-1, keepdims=True))
        l = jnp.sum(p, axis=-1, keepdims=True)
        o = jnp.dot(p.astype(BF16), vx_ref[0, :, hh * xd:(hh + 1) * xd], preferred_element_type=F32) / l
        heads.append(o.astype(BF16))
    h2 = h1 + jnp.dot(jnp.concatenate(heads, axis=1), wxo_ref[...], preferred_element_type=F32)
    hn2 = _rms(h2, gffn_ref[...])
    h2_ref[...] = h2
    hn2_ref[...] = hn2.astype(BF16)

    x_hi = hn2.astype(BF16)
    x_lo = (hn2 - x_hi.astype(F32)).astype(BF16)
    by_hi = lax.dot_general(wr_ref[...], x_hi, NT_DIMS, preferred_element_type=F32)
    by_lo = lax.dot_general(wr_ref[0:ROUTER_ROWS], x_lo, NT_DIMS, preferred_element_type=F32)
    logits = by_hi[0:ROUTER_ROWS] + by_hi[ROUTER_ROWS:2 * ROUTER_ROWS] + by_lo + br_ref[...]
    gl = logits[0:N_GROUPS]
    gmax = jnp.max(gl, axis=0, keepdims=True)
    gi = lax.broadcasted_iota(jnp.int32, gl.shape, 0)
    grp = jnp.min(jnp.where(gl == gmax, gi, N_GROUPS), axis=0, keepdims=True)
    g_w = 1.0 / jnp.sum(jnp.exp(gl - gmax), axis=0, keepdims=True)
    el = jnp.zeros((EXPERTS_PER_GROUP, tm), F32)
    for g in range(N_GROUPS):
        lo = SUBLANES + g * EXPERTS_PER_GROUP
        el = jnp.where(grp == g, logits[lo:lo + EXPERTS_PER_GROUP], el)
    ee = jnp.exp(el - jnp.max(el, axis=0, keepdims=True))
    ep = ee / jnp.sum(ee, axis=0, keepdims=True)
    ei = lax.broadcasted_iota(jnp.int32, ep.shape, 0)
    p1 = jnp.max(ep, axis=0, keepdims=True)
    i1 = jnp.min(jnp.where(ep == p1, ei, EXPERTS_PER_GROUP), axis=0, keepdims=True)
    ep_rest = jnp.where(ei == i1, -1.0, ep)
    p2 = jnp.max(ep_rest, axis=0, keepdims=True)
    i2 = jnp.min(jnp.where(ep_rest == p2, ei, EXPERTS_PER_GROUP), axis=0, keepdims=True)
    den = p1 + p2
    gate1 = g_w * p1 / den
    gate2 = g_w * p2 / den
    e1 = grp * EXPERTS_PER_GROUP + i1
    e2 = grp * EXPERTS_PER_GROUP + i2
    li = lax.broadcasted_iota(jnp.int32, (LANES, tm), 0)
    gcol_ref[...] = jnp.where(li == 0, gate1, jnp.where(li == 1, gate2, 0.0)).T

    xi = lax.broadcasted_iota(jnp.int32, (N_EXPERTS, tm), 0)
    oh1 = xi == e1
    oh2 = xi == e2
    cnt = oh1.astype(F32) + oh2.astype(F32)
    tile_cnt = jnp.broadcast_to(jnp.sum(cnt, axis=1, keepdims=True), (N_EXPERTS, LANES))
    carry_ref[...] = carry_ref[...] + tile_cnt
    cnt_ref[...] = carry_ref[...]
    tcnt_ref[...] = tile_cnt
    in_tile = jnp.dot(cnt.astype(BF16), tri_ref[...], preferred_element_type=F32)
    tile_start = jnp.dot(elow_ref[...], tile_cnt, precision=lax.Precision.HIGHEST,
                         preferred_element_type=F32)[:, 0:1]
    local = in_tile + tile_start
    pos_ref[0:1, :] = jnp.sum(jnp.where(oh1, local, 0.0), axis=0, keepdims=True).astype(jnp.int32)
    pos_ref[1:2, :] = jnp.sum(jnp.where(oh2, local, 0.0), axis=0, keepdims=True).astype(jnp.int32)


def _post(x, ylru, yatt, gatt, wo_a, wo_b, gcross, wxq, kx, vx, wxo, gffn, wr_t, br_t, tri, tm):
    b, s, d = x.shape
    c = ylru.shape[-1]
    m = kx.shape[1]
    n = b * s
    nt = s // tm
    full = lambda shape: pl.BlockSpec(shape, lambda bi, si: (0,) * len(shape))
    tok = lambda rows: pl.BlockSpec((rows, tm), lambda bi, si: (0, bi * nt + si))
    return pl.pallas_call(
        _post_kernel,
        grid=(b, nt),
        in_specs=[pl.BlockSpec((1, tm, d), lambda bi, si: (bi, si, 0)),
                  pl.BlockSpec((1, tm, c), lambda bi, si: (bi, si, 0)),
                  pl.BlockSpec((1, tm, c), lambda bi, si: (bi, si, 0)),
                  full((1, c)), full((c, d)), full((c, d)), full((1, d)), full((d, d)),
                  pl.BlockSpec((1, m, d), lambda bi, si: (bi, 0, 0)),
                  pl.BlockSpec((1, m, d), lambda bi, si: (bi, 0, 0)),
                  full((d, d)), full((1, d)), full(wr_t.shape), full(br_t.shape), full((tm, tm)),
                  full((N_EXPERTS, N_EXPERTS))],
        out_specs=[pl.BlockSpec((tm, d), lambda bi, si: (bi * nt + si, 0)),
                   pl.BlockSpec((tm, d), lambda bi, si: (bi * nt + si, 0)),
                   pl.BlockSpec((tm, LANES), lambda bi, si: (bi * nt + si, 0)),
                   full((N_EXPERTS, LANES)),
                   tok(EXPERT_TOPK),
                   pl.BlockSpec((N_EXPERTS, LANES), lambda bi, si: (bi * nt + si, 0))],
        out_shape=[jax.ShapeDtypeStruct((n, d), F32),
                   jax.ShapeDtypeStruct((n, d), BF16),
                   jax.ShapeDtypeStruct((n, LANES), F32),
                   jax.ShapeDtypeStruct((N_EXPERTS, LANES), F32),
                   jax.ShapeDtypeStruct((EXPERT_TOPK, n), jnp.int32),
                   jax.ShapeDtypeStruct((n // tm * N_EXPERTS, LANES), F32)],
        scratch_shapes=[pltpu.VMEM((N_EXPERTS, LANES), F32)],
        compiler_params=_params(("arbitrary", "arbitrary")),
        name="post",
    )(x, ylru, yatt, gatt, wo_a, wo_b, gcross, wxq, kx, vx, wxo, gffn, wr_t, br_t, tri,
      jnp.asarray(np.tril(np.ones((N_EXPERTS, N_EXPERTS), np.float32), -1)))


def _pieces(largest):
    piece = largest
    while piece:
        yield piece
        piece //= 2


def _start_run_copies(length, largest, copy_of):
    def start(pieces):
        for piece in pieces:
            done = length & ~(2 * piece - 1)

            @pl.when((length & piece) != 0)
            def _():
                copy_of(done, piece).start()

    @pl.when(length >= RUN_SPLIT)
    def _():
        start([p for p in _pieces(largest) if p >= RUN_SPLIT])

    start([p for p in _pieces(largest) if p < RUN_SPLIT])


def _tile_rows(row, count):
    return pl.ds(pl.multiple_of(row * SUBLANES, SUBLANES), count * SUBLANES)


def _dispatch_kernel(fill_ref, rstart_ref, rlen_ref, hn_ref, pos_ref, xout_ref, sorted_ref, zero_ref, sem, zsem):
    tm = hn_ref.shape[0]
    n_sorted = EXPERT_TOPK * tm
    step = pl.program_id(0)

    zero_rows = zero_ref.shape[0] // SUBLANES
    n_rows = xout_ref.shape[0] // SUBLANES

    def pad_copies(e, start):
        first = fill_ref[e]
        length = fill_ref[N_EXPERTS + e]
        for piece in _pieces(zero_rows):
            offset = first + (length & ~(2 * piece - 1))

            @pl.when((length & piece) != 0)
            def _():
                cp = pltpu.make_async_copy(zero_ref.at[_tile_rows(0, piece)], xout_ref.at[_tile_rows(offset, piece)],
                                           zsem)
                cp.start() if start else cp.wait()

    def tail_copies(start):
        def one(i, carry):
            cp = pltpu.make_async_copy(zero_ref, xout_ref.at[_tile_rows(i * zero_rows, zero_rows)], zsem)
            cp.start() if start else cp.wait()
            return carry

        lax.fori_loop(fill_ref[2 * N_EXPERTS] // zero_rows, n_rows // zero_rows, one, 0)

    @pl.when(step == 0)
    def _():
        zero_ref[...] = jnp.zeros_like(zero_ref)
        lax.fori_loop(0, N_EXPERTS, lambda e, c: (pad_copies(e, True), c)[1], 0)
        tail_copies(True)

    pos = pos_ref[...]
    p_iota = lax.broadcasted_iota(jnp.int32, (n_sorted, tm), 0)
    picks = jnp.where((p_iota == pos[0:1, :]) | (p_iota == pos[1:2, :]), 1.0, 0.0).astype(BF16)
    xs = jnp.dot(picks, hn_ref[...], preferred_element_type=F32)
    for k in range(SUBLANES):
        sorted_ref[pl.ds(k, n_sorted, stride=SUBLANES), :] = xs[:, k * LANES:(k + 1) * LANES]

    def run(e, offset):
        length = rlen_ref[step * N_EXPERTS + e]
        first = rstart_ref[step * N_EXPERTS + e]

        _start_run_copies(length, tm, lambda done, piece: pltpu.make_async_copy(
            sorted_ref.at[_tile_rows(offset + done, piece)], xout_ref.at[_tile_rows(first + done, piece)], sem))
        return offset + length

    lax.fori_loop(0, N_EXPERTS, run, 0)

    pltpu.make_async_copy(sorted_ref, xout_ref.at[_tile_rows(0, n_sorted)], sem).wait()

    @pl.when(step == 0)
    def _():
        lax.fori_loop(0, N_EXPERTS, lambda e, c: (pad_copies(e, False), c)[1], 0)
        tail_copies(False)


def _dispatch(fill, run_start, run_len, hn2, pos, n_rows, tm):
    n, d = hn2.shape
    assert d == SUBLANES * LANES, "a row must be exactly one (8, 128) tile for the tile-contiguous layout"
    return pl.pallas_call(
        _dispatch_kernel,
        grid_spec=pltpu.PrefetchScalarGridSpec(
            num_scalar_prefetch=3,
            grid=(n // tm,),
            in_specs=[pl.BlockSpec((tm, d), lambda i, *_: (i, 0)),
                      pl.BlockSpec((EXPERT_TOPK, tm), lambda i, *_: (0, i))],
            out_specs=pl.BlockSpec(memory_space=pl.ANY),
            scratch_shapes=[pltpu.VMEM((EXPERT_TOPK * tm * SUBLANES, LANES), F32),
                            pltpu.VMEM((EXPERT_ROWS // 2 * SUBLANES, LANES), F32),
                            pltpu.SemaphoreType.DMA(()), pltpu.SemaphoreType.DMA(())]),
        out_shape=jax.ShapeDtypeStruct((n_rows * SUBLANES, LANES), F32),
        compiler_params=_params(("arbitrary",)),
        name="dispatch",
    )(fill, run_start, run_len, hn2, pos)


def _experts_kernel(be_ref, nu_ref, kfirst_ref, kend_ref, rfirst_ref, rlen_ref, rdst_ref,
                    x_ref, wgu_ref, wd_ref, yts_ref, wgu_bf_ref, wd_bf_ref, ybuf_ref, sems):
    j = pl.program_id(0)
    rows = x_ref.shape[0] // SUBLANES
    f = wd_ref.shape[1]
    n_used = nu_ref[0]
    par = j % 2
    last_blk = be_ref.shape[0] - 1
    pieces = SUBLANES

    def start_copies_of_previous():
        blk_first = (j - 1) * rows

        def run(k, carry):
            lo = jnp.maximum(rfirst_ref[k], blk_first)
            hi = jnp.minimum(rfirst_ref[k] + rlen_ref[k], blk_first + rows)
            length = jnp.maximum(hi - lo, 0)
            src = lo - blk_first
            dst = rdst_ref[k] + (lo - rfirst_ref[k])
            _start_run_copies(length, rows, lambda done, piece: pltpu.make_async_copy(
                ybuf_ref.at[1 - par, _tile_rows(src + done, piece)], yts_ref.at[_tile_rows(dst + done, piece)],
                sems.at[1 - par]))
            return carry

        lax.fori_loop(kfirst_ref[j - 1], kend_ref[j - 1], run, 0)

    def wait_copies(blk_par):
        pltpu.make_async_copy(ybuf_ref.at[blk_par], yts_ref.at[pl.ds(0, rows * pieces)], sems.at[blk_par]).wait()

    def mlp():
        x = jnp.concatenate([x_ref[pl.ds(k, rows, stride=pieces), :] for k in range(pieces)], axis=1)
        gu = jnp.dot(x.astype(BF16), wgu_bf_ref[...], preferred_element_type=F32)
        act = jax.nn.silu(gu[:, 0:f]) * gu[:, f:2 * f]
        y = jnp.dot(act.astype(BF16), wd_bf_ref[...], preferred_element_type=F32)
        for k in range(pieces):
            ybuf_ref[par, pl.ds(k, rows, stride=pieces), :] = y[:, k * LANES:(k + 1) * LANES]

    jc = jnp.minimum(j, last_blk)
    @pl.when((j < n_used) & ((j == 0) | (be_ref[jc] != be_ref[jnp.maximum(jc, 1) - 1])))
    def _():
        wgu_bf_ref[...] = wgu_ref[0].astype(BF16)
        wd_bf_ref[...] = wd_ref[0].astype(BF16)

    @pl.when((j >= 2) & (j <= n_used))
    def _():
        wait_copies(par)

    @pl.when(j == 0)
    def _():
        ybuf_ref[1] = jnp.zeros(ybuf_ref.shape[1:], F32)
        cp = pltpu.make_async_copy(ybuf_ref.at[1], yts_ref.at[pl.ds(yts_ref.shape[0] - rows * pieces, rows * pieces)],
                                   sems.at[1])
        cp.start()
        cp.wait()

    @pl.when((j >= 1) & (j <= n_used))
    def _():
        start_copies_of_previous()

    @pl.when(j < n_used)
    def _():
        mlp()

    @pl.when((j >= 1) & (j == n_used))
    def _():
        wait_copies(1 - par)


def _experts(blk_expert, n_used, run_tables, x_buf, w_gate_up, w_down, n_out_rows):
    d = w_gate_up.shape[1]
    f = w_down.shape[1]
    r = EXPERT_ROWS
    n_blk = x_buf.shape[0] // (r * SUBLANES)
    assert d == SUBLANES * LANES and x_buf.shape[1] == LANES, "rows are (8, 128) tiles (tile-contiguous layout)"
    x_map = lambda j, be, nu, *_: (jnp.minimum(j, jnp.maximum(nu[0], 1) - 1), 0)
    w_map = lambda j, be, nu, *_: (be[jnp.minimum(j, n_blk - 1)], 0, 0)
    return pl.pallas_call(
        _experts_kernel,
        grid_spec=pltpu.PrefetchScalarGridSpec(
            num_scalar_prefetch=7,
            grid=(n_blk + 1,),
            in_specs=[pl.BlockSpec((r * SUBLANES, LANES), x_map),
                      pl.BlockSpec((1, d, 2 * f), w_map),
                      pl.BlockSpec((1, f, d), w_map)],
            out_specs=pl.BlockSpec(memory_space=pl.ANY),
            scratch_shapes=[pltpu.VMEM((d, 2 * f), BF16), pltpu.VMEM((f, d), BF16),
                            pltpu.VMEM((2, r * SUBLANES, LANES), F32), pltpu.SemaphoreType.DMA((2,))]),
        out_shape=jax.ShapeDtypeStruct((n_out_rows * SUBLANES, LANES), F32),
        compiler_params=_params(("arbitrary",)),
        name="experts",
    )(blk_expert, n_used, *run_tables, x_buf, w_gate_up, w_down)


def _combine_kernel(pos_ref, h2_ref, gcol_ref, gfin_ref, yts_ref, o_ref, pick0_ref, pick1_ref):
    tc = h2_ref.shape[0]
    n = pos_ref.shape[0] // EXPERT_TOPK
    base = pl.program_id(0) * tc
    for t in range(tc):
        for slot, pick_ref in enumerate((pick0_ref, pick1_ref)):
            row = pos_ref[slot * n + base + t]
            pick_ref[t * SUBLANES:(t + 1) * SUBLANES, :] = yts_ref[_tile_rows(row, 1), :]

    def rows_of(y_ref):
        return jnp.concatenate([y_ref[pl.ds(k, tc, stride=SUBLANES), :] for k in range(SUBLANES)], axis=1)

    g = gcol_ref[...]
    moe = g[:, 0:1] * rows_of(pick0_ref) + g[:, 1:2] * rows_of(pick1_ref)
    o_ref[...] = _rms(h2_ref[...] + moe, gfin_ref[...])


def _combine(pos_flat, h2, gcol, gfin, y_ts, tc):
    n, d = h2.shape
    return pl.pallas_call(
        _combine_kernel,
        grid_spec=pltpu.PrefetchScalarGridSpec(
            num_scalar_prefetch=1,
            grid=(n // tc,),
            in_specs=[pl.BlockSpec((tc, d), lambda i, pos: (i, 0)),
                      pl.BlockSpec((tc, LANES), lambda i, pos: (i, 0)),
                      pl.BlockSpec((1, d), lambda i, pos: (0, 0)),
                      pl.BlockSpec((EXPERT_TOPK * tc * SUBLANES, LANES), lambda i, pos: (i, 0))],
            out_specs=pl.BlockSpec((tc, d), lambda i, pos: (i, 0)),
            scratch_shapes=[pltpu.VMEM((tc * SUBLANES, LANES), F32), pltpu.VMEM((tc * SUBLANES, LANES), F32)]),
        out_shape=jax.ShapeDtypeStruct((n, d), F32),
        compiler_params=_params(("arbitrary",)),
        name="combine",
    )(pos_flat, h2, gcol, gfin, y_ts)


def _rope_tables(s):
    half = ROT_DIM // 2
    inv_freq = (ROPE_THETA ** (-np.arange(0, ROT_DIM, 2, dtype=np.float32) / ROT_DIM)).astype(np.float32)
    ang = np.arange(s, dtype=np.float32)[:, None] * inv_freq[None, :]
    cos, sin = np.cos(ang), np.sin(ang)
    zeros = lambda w: np.zeros((s, w), np.float32)
    cos_h = np.concatenate([cos, cos, np.ones((s, HEAD_DIM - ROT_DIM), np.float32)], axis=1)
    sina_h = np.concatenate([-sin, zeros(HEAD_DIM - half)], axis=1)
    sinb_h = np.concatenate([zeros(half), sin, zeros(HEAD_DIM - ROT_DIM)], axis=1)
    tile = lambda t: jnp.asarray(np.tile(t, (1, LANES // HEAD_DIM)), F32)
    return tile(cos_h), tile(sina_h), tile(sinb_h)


def _block_diag(w):
    nblk, bi, bo = w.shape
    eye = jnp.eye(nblk, dtype=w.dtype)
    return jnp.einsum('hij,hg->higj', w, eye).reshape(nblk * bi, nblk * bo)


def _layer(h, mem, norm_mix, w_in, conv_w, conv_b, w_rg, b_rg, w_ig, b_ig, lru_lambda, norm_lru_out,
           norm_attn_out, w_out, norm_cross, norm_mem, w_xq, w_xkv, w_xo, norm_ffn, w_router_group,
           b_router_group, w_router_expert, b_router_expert, w_gate_up, w_down, norm_out):
    b, s, d = h.shape
    c = conv_w.shape[-1]
    n = b * s
    row = lambda t: t.reshape(1, -1)
    tm = 512 if s % 512 == 0 else MOBA_BLOCK
    ts = tm
    assert s % ts == 0 and s % tm == 0 and c == N_HEADS * HEAD_DIM

    kx, vx = _memkv(mem, row(norm_mem), w_xkv.astype(BF16))

    w_gates = jnp.concatenate([_block_diag(w_rg), _block_diag(w_ig)], axis=1).astype(BF16)
    b_gates = jnp.concatenate([b_rg, b_ig]).reshape(1, -1)
    cos_t, sina_t, sinb_t = _rope_tables(s)
    ylru, q, k, v = _mixer_in(h, row(norm_mix), w_in.astype(BF16), conv_w, row(conv_b), w_gates, b_gates,
                              row(lru_lambda), row(norm_lru_out), cos_t, sina_t, sinb_t, ts)
    yatt = _moba(q, k, v)

    w_out_b = w_out.astype(BF16)
    wr_t = jnp.zeros((ROUTER_ROWS, d), F32)
    wr_t = wr_t.at[0:N_GROUPS].set(w_router_group.T).at[SUBLANES:SUBLANES + N_EXPERTS].set(w_router_expert.T)
    wr_hi = wr_t.astype(BF16)
    wr_t = jnp.concatenate([wr_hi, (wr_t - wr_hi.astype(F32)).astype(BF16)], axis=0)
    br_t = jnp.zeros((ROUTER_ROWS, 1), F32)
    br_t = br_t.at[0:N_GROUPS, 0].set(b_router_group).at[SUBLANES:SUBLANES + N_EXPERTS, 0].set(b_router_expert)
    tri = jnp.asarray(np.triu(np.ones((tm, tm), np.float32), 1), BF16)
    h2, hn2, gcol, counts, pos, tile_cnt = _post(
        h, ylru, yatt, row(norm_attn_out), w_out_b[:c], w_out_b[c:], row(norm_cross), w_xq.astype(BF16),
        kx, vx, w_xo.astype(BF16), row(norm_ffn), wr_t, br_t, tri, tm)

    counts = counts[:, 0].astype(jnp.int32)
    padded = (counts + EXPERT_ROWS - 1) // EXPERT_ROWS * EXPERT_ROWS
    pad_ends = jnp.cumsum(padded)
    pad_starts = pad_ends - padded
    n_blocks = n * EXPERT_TOPK // EXPERT_ROWS + N_EXPERTS
    blk_first = jnp.arange(n_blocks, dtype=jnp.int32) * EXPERT_ROWS
    blk_expert = jnp.minimum(jnp.sum(blk_first[:, None] >= pad_ends[None, :], axis=1), N_EXPERTS - 1).astype(jnp.int32)
    n_used = (pad_ends[-1:] // EXPERT_ROWS).astype(jnp.int32)
    fill = jnp.concatenate([pad_starts + counts, padded - counts, pad_ends[-1:]]).astype(jnp.int32)

    n_tiles = n // tm
    tile_cnt = tile_cnt[:, 0].astype(jnp.int32).reshape(n_tiles, N_EXPERTS)
    run_start = (pad_starts[None, :] + jnp.cumsum(tile_cnt, axis=0) - tile_cnt).astype(jnp.int32)
    x_buf = _dispatch(fill, run_start.reshape(-1), tile_cnt.reshape(-1), hn2, pos, n_blocks * EXPERT_ROWS, tm)

    in_tile = jnp.cumsum(tile_cnt, axis=1) - tile_cnt
    out_row = jnp.arange(n_tiles, dtype=jnp.int32)[:, None] * (EXPERT_TOPK * tm) + in_tile
    pad_first = pad_starts + counts
    run_first = jnp.concatenate([run_start.T, pad_first[:, None]], axis=1).reshape(-1)
    run_size = jnp.concatenate([tile_cnt.T, (padded - counts)[:, None]], axis=1).reshape(-1)
    run_out = jnp.concatenate([out_row.T, (EXPERT_TOPK * n + pad_first % EXPERT_ROWS)[:, None]], axis=1).reshape(-1)
    blk_end = blk_first + EXPERT_ROWS
    k_first = jnp.sum((run_first + run_size)[None, :] <= blk_first[:, None], axis=1)
    k_end = jnp.sum(run_first[None, :] < blk_end[:, None], axis=1)
    tables = [t.astype(jnp.int32) for t in (k_first, k_end, run_first, run_size, run_out)]
    y_ts = _experts(blk_expert, n_used, tables, x_buf, w_gate_up, w_down, EXPERT_TOPK * n + EXPERT_ROWS)
    out = _combine(pos.reshape(-1), h2, gcol, row(norm_out), y_ts, tm)
    return out.reshape(b, s, d)


def kernel(x, mem, norm_mix, w_in, conv_w, conv_b, w_rg, b_rg, w_ig, b_ig, lru_lambda, norm_lru_out, norm_attn_out,
           w_out, norm_cross, norm_mem, w_xq, w_xkv, w_xo, norm_ffn, w_router_group, b_router_group,
           w_router_expert, b_router_expert, w_gate_up, w_down, norm_final):
    depth = norm_mix.shape[0]
    assert depth == 1, "the fused final norm assumes a single layer"
    l = 0
    return _layer(x, mem, norm_mix[l], w_in[l], conv_w[l], conv_b[l], w_rg[l], b_rg[l], w_ig[l], b_ig[l],
                  lru_lambda[l], norm_lru_out[l], norm_attn_out[l], w_out[l], norm_cross[l], norm_mem[l],
                  w_xq[l], w_xkv[l], w_xo[l], norm_ffn[l], w_router_group[l], b_router_group[l],
                  w_router_expert[l], b_router_expert[l], w_gate_up[l], w_down[l], norm_final)
```

```python
import jax
import jax.numpy as jnp
import numpy as np
from jax import lax
from jax.experimental import pallas as pl
from jax.experimental.pallas import tpu as pltpu

F32 = jnp.float32
BF16 = jnp.bfloat16

CONV_WIDTH = 4
RG_C = 8.0
N_HEADS = 8
HEAD_DIM = 64
ROT_DIM = HEAD_DIM // 4
ROPE_THETA = 500000.0
MOBA_BLOCK = 256
MOBA_TOPK = 3
N_XHEADS = 4
N_GROUPS = 4
EXPERTS_PER_GROUP = 8
N_EXPERTS = N_GROUPS * EXPERTS_PER_GROUP
EXPERT_TOPK = 2
EPS = 1e-6

LANES = 128
SUBLANES = 8
NEG_BIG = -1e30
LOG2_E = 1.4426950408889634
MOBA_QROWS = 128
MOBA_LOOKAHEAD = 5
EXPERT_ROWS = 512
RUN_SPLIT = 64
BF16_ROWS = 2 * SUBLANES
ROUTER_ROWS = -(-(SUBLANES + N_EXPERTS) // BF16_ROWS) * BF16_ROWS
VMEM_LIMIT = 56 * 1024 * 1024

NT_DIMS = (((1,), (1,)), ((), ()))


def _rms(x, g):
    return x * lax.rsqrt(jnp.mean(x * x, axis=-1, keepdims=True) + EPS) * g


def _params(sem):
    return pltpu.CompilerParams(dimension_semantics=sem, vmem_limit_bytes=VMEM_LIMIT)


def _memkv_kernel(mem_ref, g_ref, w_ref, k_ref, v_ref):
    d = mem_ref.shape[-1]
    mn = _rms(mem_ref[0], g_ref[...]).astype(BF16)
    kv = jnp.dot(mn, w_ref[...], preferred_element_type=F32)
    k_ref[0] = kv[:, :d].astype(BF16)
    v_ref[0] = kv[:, d:].astype(BF16)


def _memkv(mem, g, w_xkv):
    b, m, d = mem.shape
    return pl.pallas_call(
        _memkv_kernel,
        grid=(b,),
        in_specs=[pl.BlockSpec((1, m, d), lambda i: (i, 0, 0)),
                  pl.BlockSpec((1, d), lambda i: (0, 0)),
                  pl.BlockSpec((d, 2 * d), lambda i: (0, 0))],
        out_specs=[pl.BlockSpec((1, m, d), lambda i: (i, 0, 0)),
                   pl.BlockSpec((1, m, d), lambda i: (i, 0, 0))],
        out_shape=[jax.ShapeDtypeStruct((b, m, d), BF16)] * 2,
        compiler_params=_params(("arbitrary",)),
        name="memkv",
    )(mem, g, w_xkv)


def _mixer_in_kernel(x_ref, gmix_ref, win_ref, convw_ref, convb_ref, wg_ref, bg_ref, lam_ref, glru_ref,
                     cos_ref, sina_ref, sinb_ref,
                     ylru_ref, q_ref, k_ref, v_ref, xpad_ref, hcar_ref, ascan_ref, uscan_ref):
    c = ylru_ref.shape[-1]
    si = pl.program_id(1)

    @pl.when(si == 0)
    def _():
        xpad_ref[0:SUBLANES, :] = jnp.zeros((SUBLANES, c), F32)
        hcar_ref[...] = jnp.zeros_like(hcar_ref)

    ts = x_ref.shape[1]
    xn = _rms(x_ref[0], gmix_ref[...]).astype(BF16)
    proj = jnp.dot(xn, win_ref[...], preferred_element_type=F32)
    x_lru = proj[:, 0:c]
    g_lru = proj[:, c:2 * c]
    q = proj[:, 2 * c:3 * c]
    k = proj[:, 3 * c:4 * c]
    v = proj[:, 4 * c:5 * c]

    xpad_ref[SUBLANES:SUBLANES + ts, :] = x_lru
    cw = convw_ref[...]
    xc = convb_ref[...] + cw[3:4] * x_lru
    for j in range(1, CONV_WIDTH):
        xc = xc + cw[3 - j:4 - j] * xpad_ref[SUBLANES - j:SUBLANES - j + ts, :]
    xpad_ref[0:SUBLANES, :] = x_lru[ts - SUBLANES:ts, :]

    gates = jnp.dot(xc.astype(BF16), wg_ref[...], preferred_element_type=F32) + bg_ref[...]
    r = jax.nn.sigmoid(gates[:, 0:c])
    i = jax.nn.sigmoid(gates[:, c:2 * c])
    neg_lam = -lam_ref[...]
    softplus = jnp.maximum(neg_lam, 0.0) + jnp.log1p(jnp.exp(-jnp.abs(neg_lam)))
    log_a = -RG_C * r * softplus
    a = jnp.exp(log_a)
    u = jnp.sqrt(-jnp.tanh(log_a) * (a * a + 1.0)) * (i * xc)

    groups = ts // SUBLANES
    cols = c // LANES
    for j in range(cols):
        ascan_ref[j] = a[:, j * LANES:(j + 1) * LANES]
        uscan_ref[j] = u[:, j * LANES:(j + 1) * LANES]
    for r in range(SUBLANES):
        rows = pl.ds(r, groups, stride=SUBLANES)
        a_r = jnp.concatenate([ascan_ref[j, rows, :] for j in range(cols)], axis=1)
        u_r = jnp.concatenate([uscan_ref[j, rows, :] for j in range(cols)], axis=1)
        if r:
            u_r = a_r * u_acc + u_r
            a_r = a_r * a_acc
            for j in range(cols):
                ascan_ref[j, rows, :] = a_r[:, j * LANES:(j + 1) * LANES]
                uscan_ref[j, rows, :] = u_r[:, j * LANES:(j + 1) * LANES]
        a_acc, u_acc = a_r, u_r
    carry = hcar_ref[...]
    hs = []
    for gi in range(groups):
        blk = slice(gi * SUBLANES, (gi + 1) * SUBLANES)
        a_g = jnp.concatenate([ascan_ref[j, blk, :] for j in range(cols)], axis=1)
        u_g = jnp.concatenate([uscan_ref[j, blk, :] for j in range(cols)], axis=1)
        hs.append(a_g * carry + u_g)
        carry = a_acc[gi:gi + 1] * carry + u_acc[gi:gi + 1]
    h = jnp.concatenate(hs, axis=0)
    hcar_ref[...] = carry

    y = h * jax.nn.gelu(g_lru)
    ylru_ref[0] = _rms(y, glru_ref[...]).astype(BF16)

    wide = lambda t_ref: jnp.concatenate([t_ref[...]] * (c // LANES), axis=1)
    cos, sina, sinb = wide(cos_ref), wide(sina_ref), wide(sinb_ref)
    half = ROT_DIM // 2

    def rope(t):
        return t * cos + pltpu.roll(t, c - half, axis=1) * sina + pltpu.roll(t, half, axis=1) * sinb

    q_ref[0] = rope(q)
    k_ref[0] = rope(k)
    v_ref[0] = v.astype(BF16)


def _mixer_in(x, gmix, w_in, conv_w, conv_b, w_gates, b_gates, lam, glru, cos_t, sina_t, sinb_t, ts):
    b, s, d = x.shape
    c = conv_w.shape[-1]
    full = lambda shape: pl.BlockSpec(shape, lambda bi, si: (0,) * len(shape))
    tab = pl.BlockSpec((ts, LANES), lambda bi, si: (si, 0))
    seq = lambda: pl.BlockSpec((1, ts, c), lambda bi, si: (bi, si, 0))
    return pl.pallas_call(
        _mixer_in_kernel,
        grid=(b, s // ts),
        in_specs=[pl.BlockSpec((1, ts, d), lambda bi, si: (bi, si, 0)),
                  full((1, d)), full(w_in.shape), full(conv_w.shape), full((1, c)),
                  full(w_gates.shape), full((1, 2 * c)), full((1, c)), full((1, c)),
                  tab, tab, tab],
        out_specs=[seq(), seq(), seq(), seq()],
        out_shape=[jax.ShapeDtypeStruct((b, s, c), BF16),
                   jax.ShapeDtypeStruct((b, s, c), F32),
                   jax.ShapeDtypeStruct((b, s, c), F32),
                   jax.ShapeDtypeStruct((b, s, c), BF16)],
        scratch_shapes=[pltpu.VMEM((ts + SUBLANES, c), F32), pltpu.VMEM((1, c), F32),
                        pltpu.VMEM((c // LANES, ts, LANES), F32), pltpu.VMEM((c // LANES, ts, LANES), F32)],
        compiler_params=_params(("arbitrary", "arbitrary")),
        name="mixer_in",
    )(x, gmix, w_in, conv_w, conv_b, w_gates, b_gates, lam, glru, cos_t, sina_t, sinb_t)


def _moba_kernel(q_ref, k_ref, v_ref, kpat_ref, o_ref, qa_ref, ka_ref):
    s_len = q_ref.shape[1]
    nb = s_len // MOBA_BLOCK
    n_pick = min(MOBA_TOPK, nb)
    q = q_ref[0]
    k = k_ref[0]
    lane = lax.broadcasted_iota(jnp.int32, (1, LANES), 1)
    kmean = jnp.concatenate(
        [jnp.sum(k[n * MOBA_BLOCK:(n + 1) * MOBA_BLOCK], axis=0, keepdims=True) for n in range(nb)],
        axis=0) * (1.0 / MOBA_BLOCK)

    blk_row = lax.broadcasted_iota(jnp.int32, (nb, s_len), 0)
    q_blk = lax.broadcasted_iota(jnp.int32, (nb, s_len), 1) // MOBA_BLOCK
    past = blk_row < q_blk

    q_hi = q.astype(BF16)
    q_lo = (q - q_hi.astype(F32)).astype(BF16)
    for h in range(2):
        own = (lane >= h * HEAD_DIM) & (lane < (h + 1) * HEAD_DIM)
        off = (1 - h) * HEAD_DIM
        km = jnp.where(own, kmean, 0.0)
        km_hi = km.astype(BF16)
        km_split = jnp.concatenate([km_hi, (km - km_hi.astype(F32)).astype(BF16)], axis=0)
        parts = (lax.dot_general(km_split, q_hi, NT_DIMS, preferred_element_type=F32)
                 + lax.dot_general(km_split, q_lo, NT_DIMS, preferred_element_type=F32))
        gate = jnp.where(past, parts[0:nb] + parts[nb:2 * nb], -jnp.inf)
        rank = jnp.zeros((nb, s_len), jnp.int32)
        for m in range(nb):
            gm = gate[m:m + 1, :]
            ahead = (gm > gate) | ((gm == gate) & (m < blk_row))
            rank = rank + ahead.astype(jnp.int32)
        allowed = ((rank < n_pick) & past) | (blk_row == q_blk)
        bias = jnp.where(allowed, 0.0, NEG_BIG)
        pieces = []
        if off:
            pieces.append(jnp.zeros((off, s_len), F32))
        pieces.append(bias)
        if LANES - off - nb:
            pieces.append(jnp.zeros((LANES - off - nb, s_len), F32))
        bias_lanes = jnp.concatenate(pieces, axis=0).T
        qa_ref[h] = jnp.where(own, q * (HEAD_DIM ** -0.5 * LOG2_E), bias_lanes).astype(BF16)
        ka_ref[h] = jnp.where(own, k.astype(BF16), kpat_ref[h])

    own0 = lane < HEAD_DIM

    parts = MOBA_BLOCK // MOBA_QROWS

    def q_rows(qi, part):
        return slice(qi * MOBA_BLOCK + part * MOBA_QROWS, qi * MOBA_BLOCK + (part + 1) * MOBA_QROWS)

    def n_keys(qi, part):
        return qi * MOBA_BLOCK + (part + 1) * MOBA_QROWS

    def scores(qi, part, h):
        return lax.dot_general(qa_ref[h, q_rows(qi, part), :], ka_ref[h, 0:n_keys(qi, part), :], NT_DIMS,
                               preferred_element_type=F32)

    units = [(qi, part, h) for qi in range(nb) for part in range(parts) for h in range(2)]
    ahead = [scores(*u) for u in units[:MOBA_LOOKAHEAD]]
    outs = []

    def weighted_values(p, l, qi, part, h):
        outs.append(jnp.dot(p, v_ref[0, 0:n_keys(qi, part), :], preferred_element_type=F32) / l)
        if h:
            o_ref[0, q_rows(qi, part), :] = jnp.where(own0, outs[0], outs[1])
            outs.clear()

    pending = None
    for n, (qi, part, h) in enumerate(units):
        s = ahead.pop(0)
        if n + MOBA_LOOKAHEAD < len(units):
            ahead.append(scores(*units[n + MOBA_LOOKAHEAD]))
        n_past = qi * MOBA_BLOCK
        own_keys = (part + 1) * MOBA_QROWS
        key_in_blk = lax.broadcasted_iota(jnp.int32, (MOBA_QROWS, own_keys), 1)
        q_in_blk = lax.broadcasted_iota(jnp.int32, (MOBA_QROWS, own_keys), 0) + part * MOBA_QROWS
        s_own = jnp.where(key_in_blk <= q_in_blk, s[:, n_past:], NEG_BIG)
        s = jnp.concatenate([s[:, :n_past], s_own], axis=1) if qi else s_own
        p = jnp.exp2(s - jnp.max(s, axis=-1, keepdims=True))
        l = jnp.sum(p, axis=-1, keepdims=True)
        if pending is not None:
            weighted_values(*pending)
        pending = (p.astype(BF16), l, qi, part, h)
    weighted_values(*pending)


def _key_block_pattern(s):
    pat = np.zeros((2, s, LANES), np.float32)
    for h in range(2):
        off = (1 - h) * HEAD_DIM
        pat[h, np.arange(s), off + np.arange(s) // MOBA_BLOCK] = 1.0
    return jnp.asarray(pat, BF16)


def _moba(q, k, v):
    b, s, c = q.shape
    spec = lambda: pl.BlockSpec((1, s, LANES), lambda bi, hi: (bi, 0, hi))
    return pl.pallas_call(
        _moba_kernel,
        grid=(b, c // LANES),
        in_specs=[spec(), spec(), spec(), pl.BlockSpec((2, s, LANES), lambda bi, hi: (0, 0, 0))],
        out_specs=spec(),
        out_shape=jax.ShapeDtypeStruct((b, s, c), F32),
        scratch_shapes=[pltpu.VMEM((2, s, LANES), BF16), pltpu.VMEM((2, s, LANES), BF16)],
        compiler_params=_params(("arbitrary", "arbitrary")),
        name="moba",
    )(q, k, v, _key_block_pattern(s))


def _post_kernel(x_ref, ylru_ref, yatt_ref, gatt_ref, woa_ref, wob_ref, gcross_ref, wxq_ref, kx_ref, vx_ref,
                 wxo_ref, gffn_ref, wr_ref, br_ref, tri_ref, elow_ref,
                 h2_ref, hn2_ref, gcol_ref, cnt_ref, pos_ref, tcnt_ref, carry_ref):
    tm = x_ref.shape[1]
    d = x_ref.shape[2]
    xd = d // N_XHEADS
    first = (pl.program_id(0) == 0) & (pl.program_id(1) == 0)

    @pl.when(first)
    def _():
        carry_ref[...] = jnp.zeros_like(carry_ref)

    ya = _rms(yatt_ref[0], gatt_ref[...]).astype(BF16)
    mix = (jnp.dot(ylru_ref[0], woa_ref[...], preferred_element_type=F32)
           + jnp.dot(ya, wob_ref[...], preferred_element_type=F32))
    h1 = x_ref[0] + mix

    hn = _rms(h1, gcross_ref[...]).astype(BF16)
    qx = jnp.dot(hn, wxq_ref[...], preferred_element_type=F32)
    heads = []
    for hh in range(N_XHEADS):
        qh = (qx[:, hh * xd:(hh + 1) * xd] * (xd ** -0.5)).astype(BF16)
        s = lax.dot_general(qh, kx_ref[0, :, hh * xd:(hh + 1) * xd], NT_DIMS, preferred_element_type=F32)
        p = jnp.exp(s - jnp.max(s, axis=-1, keepdims=True))
        l = jnp.sum(p, axis=-1, keepdims=True)
        o = jnp.dot(p.astype(BF16), vx_ref[0, :, hh * xd:(hh + 1) * xd], preferred_element_type=F32) / l
        heads.append(o.astype(BF16))
    h2 = h1 + jnp.dot(jnp.concatenate(heads, axis=1), wxo_ref[...], preferred_element_type=F32)
    hn2 = _rms(h2, gffn_ref[...])
    h2_ref[...] = h2
    hn2_ref[...] = hn2.astype(BF16)

    x_hi = hn2.astype(BF16)
    x_lo = (hn2 - x_hi.astype(F32)).astype(BF16)
    by_hi = lax.dot_general(wr_ref[...], x_hi, NT_DIMS, preferred_element_type=F32)
    by_lo = lax.dot_general(wr_ref[0:ROUTER_ROWS], x_lo, NT_DIMS, preferred_element_type=F32)
    logits = by_hi[0:ROUTER_ROWS] + by_hi[ROUTER_ROWS:2 * ROUTER_ROWS] + by_lo + br_ref[...]
    gl = logits[0:N_GROUPS]
    gmax = jnp.max(gl, axis=0, keepdims=True)
    gi = lax.broadcasted_iota(jnp.int32, gl.shape, 0)
    grp = jnp.min(jnp.where(gl == gmax, gi, N_GROUPS), axis=0, keepdims=True)
    g_w = 1.0 / jnp.sum(jnp.exp(gl - gmax), axis=0, keepdims=True)
    el = jnp.zeros((EXPERTS_PER_GROUP, tm), F32)
    for g in range(N_GROUPS):
        lo = SUBLANES + g * EXPERTS_PER_GROUP
        el = jnp.where(grp == g, logits[lo:lo + EXPERTS_PER_GROUP], el)
    ee = jnp.exp(el - jnp.max(el, axis=0, keepdims=True))
    ep = ee / jnp.sum(ee, axis=0, keepdims=True)
    ei = lax.broadcasted_iota(jnp.int32, ep.shape, 0)
    p1 = jnp.max(ep, axis=0, keepdims=True)
    i1 = jnp.min(jnp.where(ep == p1, ei, EXPERTS_PER_GROUP), axis=0, keepdims=True)
    ep_rest = jnp.where(ei == i1, -1.0, ep)
    p2 = jnp.max(ep_rest, axis=0, keepdims=True)
    i2 = jnp.min(jnp.where(ep_rest == p2, ei, EXPERTS_PER_GROUP), axis=0, keepdims=True)
    den = p1 + p2
    gate1 = g_w * p1 / den
    gate2 = g_w * p2 / den
    e1 = grp * EXPERTS_PER_GROUP + i1
    e2 = grp * EXPERTS_PER_GROUP + i2
    li = lax.broadcasted_iota(jnp.int32, (LANES, tm), 0)
    gcol_ref[...] = jnp.where(li == 0, gate1, jnp.where(li == 1, gate2, 0.0)).T

    xi = lax.broadcasted_iota(jnp.int32, (N_EXPERTS, tm), 0)
    oh1 = xi == e1
    oh2 = xi == e2
    cnt = oh1.astype(F32) + oh2.astype(F32)
    tile_cnt = jnp.broadcast_to(jnp.sum(cnt, axis=1, keepdims=True), (N_EXPERTS, LANES))
    carry_ref[...] = carry_ref[...] + tile_cnt
    cnt_ref[...] = carry_ref[...]
    tcnt_ref[...] = tile_cnt
    in_tile = jnp.dot(cnt.astype(BF16), tri_ref[...], preferred_element_type=F32)
    tile_start = jnp.dot(elow_ref[...], tile_cnt, precision=lax.Precision.HIGHEST,
                         preferred_element_type=F32)[:, 0:1]
    local = in_tile + tile_start
    pos_ref[0:1, :] = jnp.sum(jnp.where(oh1, local, 0.0), axis=0, keepdims=True).astype(jnp.int32)
    pos_ref[1:2, :] = jnp.sum(jnp.where(oh2, local, 0.0), axis=0, keepdims=True).astype(jnp.int32)


def _post(x, ylru, yatt, gatt, wo_a, wo_b, gcross, wxq, kx, vx, wxo, gffn, wr_t, br_t, tri, tm):
    b, s, d = x.shape
    c = ylru.shape[-1]
    m = kx.shape[1]
    n = b * s
    nt = s // tm
    full = lambda shape: pl.BlockSpec(shape, lambda bi, si: (0,) * len(shape))
    tok = lambda rows: pl.BlockSpec((rows, tm), lambda bi, si: (0, bi * nt + si))
    return pl.pallas_call(
        _post_kernel,
        grid=(b, nt),
        in_specs=[pl.BlockSpec((1, tm, d), lambda bi, si: (bi, si, 0)),
                  pl.BlockSpec((1, tm, c), lambda bi, si: (bi, si, 0)),
                  pl.BlockSpec((1, tm, c), lambda bi, si: (bi, si, 0)),
                  full((1, c)), full((c, d)), full((c, d)), full((1, d)), full((d, d)),
                  pl.BlockSpec((1, m, d), lambda bi, si: (bi, 0, 0)),
                  pl.BlockSpec((1, m, d), lambda bi, si: (bi, 0, 0)),
                  full((d, d)), full((1, d)), full(wr_t.shape), full(br_t.shape), full((tm, tm)),
                  full((N_EXPERTS, N_EXPERTS))],
        out_specs=[pl.BlockSpec((tm, d), lambda bi, si: (bi * nt + si, 0)),
                   pl.BlockSpec((tm, d), lambda bi, si: (bi * nt + si, 0)),
                   pl.BlockSpec((tm, LANES), lambda bi, si: (bi * nt + si, 0)),
                   full((N_EXPERTS, LANES)),
                   tok(EXPERT_TOPK),
                   pl.BlockSpec((N_EXPERTS, LANES), lambda bi, si: (bi * nt + si, 0))],
        out_shape=[jax.ShapeDtypeStruct((n, d), F32),
                   jax.ShapeDtypeStruct((n, d), BF16),
                   jax.ShapeDtypeStruct((n, LANES), F32),
                   jax.ShapeDtypeStruct((N_EXPERTS, LANES), F32),
                   jax.ShapeDtypeStruct((EXPERT_TOPK, n), jnp.int32),
                   jax.ShapeDtypeStruct((n // tm * N_EXPERTS, LANES), F32)],
        scratch_shapes=[pltpu.VMEM((N_EXPERTS, LANES), F32)],
        compiler_params=_params(("arbitrary", "arbitrary")),
        name="post",
    )(x, ylru, yatt, gatt, wo_a, wo_b, gcross, wxq, kx, vx, wxo, gffn, wr_t, br_t, tri,
      jnp.asarray(np.tril(np.ones((N_EXPERTS, N_EXPERTS), np.float32), -1)))


def _pieces(largest):
    piece = largest
    while piece:
        yield piece
        piece //= 2


def _start_run_copies(length, largest, copy_of):
    def start(pieces):
        for piece in pieces:
            done = length & ~(2 * piece - 1)

            @pl.when((length & piece) != 0)
            def _():
                copy_of(done, piece).start()

    @pl.when(length >= RUN_SPLIT)
    def _():
        start([p for p in _pieces(largest) if p >= RUN_SPLIT])

    start([p for p in _pieces(largest) if p < RUN_SPLIT])


def _tile_rows(row, count):
    return pl.ds(pl.multiple_of(row * SUBLANES, SUBLANES), count * SUBLANES)


def _dispatch_kernel(fill_ref, rstart_ref, rlen_ref, hn_ref, pos_ref, xout_ref, sorted_ref, zero_ref, sem, zsem):
    tm = hn_ref.shape[0]
    n_sorted = EXPERT_TOPK * tm
    step = pl.program_id(0)

    zero_rows = zero_ref.shape[0] // SUBLANES
    n_rows = xout_ref.shape[0] // SUBLANES

    def pad_copies(e, start):
        first = fill_ref[e]
        length = fill_ref[N_EXPERTS + e]
        for piece in _pieces(zero_rows):
            offset = first + (length & ~(2 * piece - 1))

            @pl.when((length & piece) != 0)
            def _():
                cp = pltpu.make_async_copy(zero_ref.at[_tile_rows(0, piece)], xout_ref.at[_tile_rows(offset, piece)],
                                           zsem)
                cp.start() if start else cp.wait()

    def tail_copies(start):
        def one(i, carry):
            cp = pltpu.make_async_copy(zero_ref, xout_ref.at[_tile_rows(i * zero_rows, zero_rows)], zsem)
            cp.start() if start else cp.wait()
            return carry

        lax.fori_loop(fill_ref[2 * N_EXPERTS] // zero_rows, n_rows // zero_rows, one, 0)

    @pl.when(step == 0)
    def _():
        zero_ref[...] = jnp.zeros_like(zero_ref)
        lax.fori_loop(0, N_EXPERTS, lambda e, c: (pad_copies(e, True), c)[1], 0)
        tail_copies(True)

    pos = pos_ref[...]
    p_iota = lax.broadcasted_iota(jnp.int32, (n_sorted, tm), 0)
    picks = jnp.where((p_iota == pos[0:1, :]) | (p_iota == pos[1:2, :]), 1.0, 0.0).astype(BF16)
    xs = jnp.dot(picks, hn_ref[...], preferred_element_type=F32)
    for k in range(SUBLANES):
        sorted_ref[pl.ds(k, n_sorted, stride=SUBLANES), :] = xs[:, k * LANES:(k + 1) * LANES]

    def run(e, offset):
        length = rlen_ref[step * N_EXPERTS + e]
        first = rstart_ref[step * N_EXPERTS + e]

        _start_run_copies(length, tm, lambda done, piece: pltpu.make_async_copy(
            sorted_ref.at[_tile_rows(offset + done, piece)], xout_ref.at[_tile_rows(first + done, piece)], sem))
        return offset + length

    lax.fori_loop(0, N_EXPERTS, run, 0)

    pltpu.make_async_copy(sorted_ref, xout_ref.at[_tile_rows(0, n_sorted)], sem).wait()

    @pl.when(step == 0)
    def _():
        lax.fori_loop(0, N_EXPERTS, lambda e, c: (pad_copies(e, False), c)[1], 0)
        tail_copies(False)


def _dispatch(fill, run_start, run_len, hn2, pos, n_rows, tm):
    n, d = hn2.shape
    assert d == SUBLANES * LANES, "a row must be exactly one (8, 128) tile for the tile-contiguous layout"
    return pl.pallas_call(
        _dispatch_kernel,
        grid_spec=pltpu.PrefetchScalarGridSpec(
            num_scalar_prefetch=3,
            grid=(n // tm,),
            in_specs=[pl.BlockSpec((tm, d), lambda i, *_: (i, 0)),
                      pl.BlockSpec((EXPERT_TOPK, tm), lambda i, *_: (0, i))],
            out_specs=pl.BlockSpec(memory_space=pl.ANY),
            scratch_shapes=[pltpu.VMEM((EXPERT_TOPK * tm * SUBLANES, LANES), F32),
                            pltpu.VMEM((EXPERT_ROWS // 2 * SUBLANES, LANES), F32),
                            pltpu.SemaphoreType.DMA(()), pltpu.SemaphoreType.DMA(())]),
        out_shape=jax.ShapeDtypeStruct((n_rows * SUBLANES, LANES), F32),
        compiler_params=_params(("arbitrary",)),
        name="dispatch",
    )(fill, run_start, run_len, hn2, pos)


def _experts_kernel(be_ref, nu_ref, kfirst_ref, kend_ref, rfirst_ref, rlen_ref, rdst_ref,
                    x_ref, wgu_ref, wd_ref, yts_ref, wgu_bf_ref, wd_bf_ref, ybuf_ref, sems):
    j = pl.program_id(0)
    rows = x_ref.shape[0] // SUBLANES
    f = wd_ref.shape[1]
    n_used = nu_ref[0]
    par = j % 2
    last_blk = be_ref.shape[0] - 1
    pieces = SUBLANES

    def start_copies_of_previous():
        blk_first = (j - 1) * rows

        def run(k, carry):
            lo = jnp.maximum(rfirst_ref[k], blk_first)
            hi = jnp.minimum(rfirst_ref[k] + rlen_ref[k], blk_first + rows)
            length = jnp.maximum(hi - lo, 0)
            src = lo - blk_first
            dst = rdst_ref[k] + (lo - rfirst_ref[k])
            _start_run_copies(length, rows, lambda done, piece: pltpu.make_async_copy(
                ybuf_ref.at[1 - par, _tile_rows(src + done, piece)], yts_ref.at[_tile_rows(dst + done, piece)],
                sems.at[1 - par]))
            return carry

        lax.fori_loop(kfirst_ref[j - 1], kend_ref[j - 1], run, 0)

    def wait_copies(blk_par):
        pltpu.make_async_copy(ybuf_ref.at[blk_par], yts_ref.at[pl.ds(0, rows * pieces)], sems.at[blk_par]).wait()

    def mlp():
        x = jnp.concatenate([x_ref[pl.ds(k, rows, stride=pieces), :] for k in range(pieces)], axis=1)
        gu = jnp.dot(x.astype(BF16), wgu_bf_ref[...], preferred_element_type=F32)
        act = jax.nn.silu(gu[:, 0:f]) * gu[:, f:2 * f]
        y = jnp.dot(act.astype(BF16), wd_bf_ref[...], preferred_element_type=F32)
        for k in range(pieces):
            ybuf_ref[par, pl.ds(k, rows, stride=pieces), :] = y[:, k * LANES:(k + 1) * LANES]

    jc = jnp.minimum(j, last_blk)
    @pl.when((j < n_used) & ((j == 0) | (be_ref[jc] != be_ref[jnp.maximum(jc, 1) - 1])))
    def _():
        wgu_bf_ref[...] = wgu_ref[0].astype(BF16)
        wd_bf_ref[...] = wd_ref[0].astype(BF16)

    @pl.when((j >= 2) & (j <= n_used))
    def _():
        wait_copies(par)

    @pl.when(j == 0)
    def _():
        ybuf_ref[1] = jnp.zeros(ybuf_ref.shape[1:], F32)
        cp = pltpu.make_async_copy(ybuf_ref.at[1], yts_ref.at[pl.ds(yts_ref.shape[0] - rows * pieces, rows * pieces)],
                                   sems.at[1])
        cp.start()
        cp.wait()

    @pl.when((j >= 1) & (j <= n_used))
    def _():
        start_copies_of_previous()

    @pl.when(j < n_used)
    def _():
        mlp()

    @pl.when((j >= 1) & (j == n_used))
    def _():
        wait_copies(1 - par)


def _experts(blk_expert, n_used, run_tables, x_buf, w_gate_up, w_down, n_out_rows):
    d = w_gate_up.shape[1]
    f = w_down.shape[1]
    r = EXPERT_ROWS
    n_blk = x_buf.shape[0] // (r * SUBLANES)
    assert d == SUBLANES * LANES and x_buf.shape[1] == LANES, "rows are (8, 128) tiles (tile-contiguous layout)"
    x_map = lambda j, be, nu, *_: (jnp.minimum(j, jnp.maximum(nu[0], 1) - 1), 0)
    w_map = lambda j, be, nu, *_: (be[jnp.minimum(j, n_blk - 1)], 0, 0)
    return pl.pallas_call(
        _experts_kernel,
        grid_spec=pltpu.PrefetchScalarGridSpec(
            num_scalar_prefetch=7,
            grid=(n_blk + 1,),
            in_specs=[pl.BlockSpec((r * SUBLANES, LANES), x_map),
                      pl.BlockSpec((1, d, 2 * f), w_map),
                      pl.BlockSpec((1, f, d), w_map)],
            out_specs=pl.BlockSpec(memory_space=pl.ANY),
            scratch_shapes=[pltpu.VMEM((d, 2 * f), BF16), pltpu.VMEM((f, d), BF16),
                            pltpu.VMEM((2, r * SUBLANES, LANES), F32), pltpu.SemaphoreType.DMA((2,))]),
        out_shape=jax.ShapeDtypeStruct((n_out_rows * SUBLANES, LANES), F32),
        compiler_params=_params(("arbitrary",)),
        name="experts",
    )(blk_expert, n_used, *run_tables, x_buf, w_gate_up, w_down)


def _combine_kernel(pos_ref, h2_ref, gcol_ref, gfin_ref, yts_ref, o_ref, pick0_ref, pick1_ref):
    tc = h2_ref.shape[0]
    n = pos_ref.shape[0] // EXPERT_TOPK
    base = pl.program_id(0) * tc
    for t in range(tc):
        for slot, pick_ref in enumerate((pick0_ref, pick1_ref)):
            row = pos_ref[slot * n + base + t]
            pick_ref[t * SUBLANES:(t + 1) * SUBLANES, :] = yts_ref[_tile_rows(row, 1), :]

    def rows_of(y_ref):
        return jnp.concatenate([y_ref[pl.ds(k, tc, stride=SUBLANES), :] for k in range(SUBLANES)], axis=1)

    g = gcol_ref[...]
    moe = g[:, 0:1] * rows_of(pick0_ref) + g[:, 1:2] * rows_of(pick1_ref)
    o_ref[...] = _rms(h2_ref[...] + moe, gfin_ref[...])


def _combine(pos_flat, h2, gcol, gfin, y_ts, tc):
    n, d = h2.shape
    return pl.pallas_call(
        _combine_kernel,
        grid_spec=pltpu.PrefetchScalarGridSpec(
            num_scalar_prefetch=1,
            grid=(n // tc,),
            in_specs=[pl.BlockSpec((tc, d), lambda i, pos: (i, 0)),
                      pl.BlockSpec((tc, LANES), lambda i, pos: (i, 0)),
                      pl.BlockSpec((1, d), lambda i, pos: (0, 0)),
                      pl.BlockSpec((EXPERT_TOPK * tc * SUBLANES, LANES), lambda i, pos: (i, 0))],
            out_specs=pl.BlockSpec((tc, d), lambda i, pos: (i, 0)),
            scratch_shapes=[pltpu.VMEM((tc * SUBLANES, LANES), F32), pltpu.VMEM((tc * SUBLANES, LANES), F32)]),
        out_shape=jax.ShapeDtypeStruct((n, d), F32),
        compiler_params=_params(("arbitrary",)),
        name="combine",
    )(pos_flat, h2, gcol, gfin, y_ts)


def _rope_tables(s):
    half = ROT_DIM // 2
    inv_freq = (ROPE_THETA ** (-np.arange(0, ROT_DIM, 2, dtype=np.float32) / ROT_DIM)).astype(np.float32)
    ang = np.arange(s, dtype=np.float32)[:, None] * inv_freq[None, :]
    cos, sin = np.cos(ang), np.sin(ang)
    zeros = lambda w: np.zeros((s, w), np.float32)
    cos_h = np.concatenate([cos, cos, np.ones((s, HEAD_DIM - ROT_DIM), np.float32)], axis=1)
    sina_h = np.concatenate([-sin, zeros(HEAD_DIM - half)], axis=1)
    sinb_h = np.concatenate([zeros(half), sin, zeros(HEAD_DIM - ROT_DIM)], axis=1)
    tile = lambda t: jnp.asarray(np.tile(t, (1, LANES // HEAD_DIM)), F32)
    return tile(cos_h), tile(sina_h), tile(sinb_h)


def _block_diag(w):
    nblk, bi, bo = w.shape
    eye = jnp.eye(nblk, dtype=w.dtype)
    return jnp.einsum('hij,hg->higj', w, eye).reshape(nblk * bi, nblk * bo)


def _layer(h, mem, norm_mix, w_in, conv_w, conv_b, w_rg, b_rg, w_ig, b_ig, lru_lambda, norm_lru_out,
           norm_attn_out, w_out, norm_cross, norm_mem, w_xq, w_xkv, w_xo, norm_ffn, w_router_group,
           b_router_group, w_router_expert, b_router_expert, w_gate_up, w_down, norm_out):
    b, s, d = h.shape
    c = conv_w.shape[-1]
    n = b * s
    row = lambda t: t.reshape(1, -1)
    tm = 512 if s % 512 == 0 else MOBA_BLOCK
    ts = tm
    assert s % ts == 0 and s % tm == 0 and c == N_HEADS * HEAD_DIM

    kx, vx = _memkv(mem, row(norm_mem), w_xkv.astype(BF16))

    w_gates = jnp.concatenate([_block_diag(w_rg), _block_diag(w_ig)], axis=1).astype(BF16)
    b_gates = jnp.concatenate([b_rg, b_ig]).reshape(1, -1)
    cos_t, sina_t, sinb_t = _rope_tables(s)
    ylru, q, k, v = _mixer_in(h, row(norm_mix), w_in.astype(BF16), conv_w, row(conv_b), w_gates, b_gates,
                              row(lru_lambda), row(norm_lru_out), cos_t, sina_t, sinb_t, ts)
    yatt = _moba(q, k, v)

    w_out_b = w_out.astype(BF16)
    wr_t = jnp.zeros((ROUTER_ROWS, d), F32)
    wr_t = wr_t.at[0:N_GROUPS].set(w_router_group.T).at[SUBLANES:SUBLANES + N_EXPERTS].set(w_router_expert.T)
    wr_hi = wr_t.astype(BF16)
    wr_t = jnp.concatenate([wr_hi, (wr_t - wr_hi.astype(F32)).astype(BF16)], axis=0)
    br_t = jnp.zeros((ROUTER_ROWS, 1), F32)
    br_t = br_t.at[0:N_GROUPS, 0].set(b_router_group).at[SUBLANES:SUBLANES + N_EXPERTS, 0].set(b_router_expert)
    tri = jnp.asarray(np.triu(np.ones((tm, tm), np.float32), 1), BF16)
    h2, hn2, gcol, counts, pos, tile_cnt = _post(
        h, ylru, yatt, row(norm_attn_out), w_out_b[:c], w_out_b[c:], row(norm_cross), w_xq.astype(BF16),
        kx, vx, w_xo.astype(BF16), row(norm_ffn), wr_t, br_t, tri, tm)

    counts = counts[:, 0].astype(jnp.int32)
    padded = (counts + EXPERT_ROWS - 1) // EXPERT_ROWS * EXPERT_ROWS
    pad_ends = jnp.cumsum(padded)
    pad_starts = pad_ends - padded
    n_blocks = n * EXPERT_TOPK // EXPERT_ROWS + N_EXPERTS
    blk_first = jnp.arange(n_blocks, dtype=jnp.int32) * EXPERT_ROWS
    blk_expert = jnp.minimum(jnp.sum(blk_first[:, None] >= pad_ends[None, :], axis=1), N_EXPERTS - 1).astype(jnp.int32)
    n_used = (pad_ends[-1:] // EXPERT_ROWS).astype(jnp.int32)
    fill = jnp.concatenate([pad_starts + counts, padded - counts, pad_ends[-1:]]).astype(jnp.int32)

    n_tiles = n // tm
    tile_cnt = tile_cnt[:, 0].astype(jnp.int32).reshape(n_tiles, N_EXPERTS)
    run_start = (pad_starts[None, :] + jnp.cumsum(tile_cnt, axis=0) - tile_cnt).astype(jnp.int32)
    x_buf = _dispatch(fill, run_start.reshape(-1), tile_cnt.reshape(-1), hn2, pos, n_blocks * EXPERT_ROWS, tm)

    in_tile = jnp.cumsum(tile_cnt, axis=1) - tile_cnt
    out_row = jnp.arange(n_tiles, dtype=jnp.int32)[:, None] * (EXPERT_TOPK * tm) + in_tile
    pad_first = pad_starts + counts
    run_first = jnp.concatenate([run_start.T, pad_first[:, None]], axis=1).reshape(-1)
    run_size = jnp.concatenate([tile_cnt.T, (padded - counts)[:, None]], axis=1).reshape(-1)
    run_out = jnp.concatenate([out_row.T, (EXPERT_TOPK * n + pad_first % EXPERT_ROWS)[:, None]], axis=1).reshape(-1)
    blk_end = blk_first + EXPERT_ROWS
    k_first = jnp.sum((run_first + run_size)[None, :] <= blk_first[:, None], axis=1)
    k_end = jnp.sum(run_first[None, :] < blk_end[:, None], axis=1)
    tables = [t.astype(jnp.int32) for t in (k_first, k_end, run_first, run_size, run_out)]
    y_ts = _experts(blk_expert, n_used, tables, x_buf, w_gate_up, w_down, EXPERT_TOPK * n + EXPERT_ROWS)
    out = _combine(pos.reshape(-1), h2, gcol, row(norm_out), y_ts, tm)
    return out.reshape(b, s, d)


def kernel(x, mem, norm_mix, w_in, conv_w, conv_b, w_rg, b_rg, w_ig, b_ig, lru_lambda, norm_lru_out, norm_attn_out,
           w_out, norm_cross, norm_mem, w_xq, w_xkv, w_xo, norm_ffn, w_router_group, b_router_group,
           w_router_expert, b_router_expert, w_gate_up, w_down, norm_final):
    depth = norm_mix.shape[0]
    assert depth == 1, "the fused final norm assumes a single layer"
    l = 0
    return _layer(x, mem, norm_mix[l], w_in[l], conv_w[l], conv_b[l], w_rg[l], b_rg[l], w_ig[l], b_ig[l],
                  lru_lambda[l], norm_lru_out[l], norm_attn_out[l], w_out[l], norm_cross[l], norm_mem[l],
                  w_xq[l], w_xkv[l], w_xo[l], norm_ffn[l], w_router_group[l], b_router_group[l],
                  w_router_expert[l], b_router_expert[l], w_gate_up[l], w_down[l], norm_final)
```

```python
import jax
import jax.numpy as jnp
import numpy as np
from jax import lax
from jax.experimental import pallas as pl
from jax.experimental.pallas import tpu as pltpu

F32 = jnp.float32
BF16 = jnp.bfloat16

CONV_WIDTH = 4
RG_C = 8.0
N_HEADS = 8
HEAD_DIM = 64
ROT_DIM = HEAD_DIM // 4
ROPE_THETA = 500000.0
MOBA_BLOCK = 256
MOBA_TOPK = 3
N_XHEADS = 4
N_GROUPS = 4
EXPERTS_PER_GROUP = 8
N_EXPERTS = N_GROUPS * EXPERTS_PER_GROUP
EXPERT_TOPK = 2
EPS = 1e-6

LANES = 128
SUBLANES = 8
NEG_BIG = -1e30
LOG2_E = 1.4426950408889634
MOBA_QROWS = 128
MOBA_LOOKAHEAD = 5
EXPERT_ROWS = 512
RUN_SPLIT = 64
BF16_ROWS = 2 * SUBLANES
ROUTER_ROWS = -(-(SUBLANES + N_EXPERTS) // BF16_ROWS) * BF16_ROWS
VMEM_LIMIT = 56 * 1024 * 1024

NT_DIMS = (((1,), (1,)), ((), ()))


def _rms(x, g):
    return x * lax.rsqrt(jnp.mean(x * x, axis=-1, keepdims=True) + EPS) * g


def _params(sem):
    return pltpu.CompilerParams(dimension_semantics=sem, vmem_limit_bytes=VMEM_LIMIT)


def _memkv_kernel(mem_ref, g_ref, w_ref, k_ref, v_ref):
    d = mem_ref.shape[-1]
    mn = _rms(mem_ref[0], g_ref[...]).astype(BF16)
    kv = jnp.dot(mn, w_ref[...], preferred_element_type=F32)
    k_ref[0] = kv[:, :d].astype(BF16)
    v_ref[0] = kv[:, d:].astype(BF16)


def _memkv(mem, g, w_xkv):
    b, m, d = mem.shape
    return pl.pallas_call(
        _memkv_kernel,
        grid=(b,),
        in_specs=[pl.BlockSpec((1, m, d), lambda i: (i, 0, 0)),
                  pl.BlockSpec((1, d), lambda i: (0, 0)),
                  pl.BlockSpec((d, 2 * d), lambda i: (0, 0))],
        out_specs=[pl.BlockSpec((1, m, d), lambda i: (i, 0, 0)),
                   pl.BlockSpec((1, m, d), lambda i: (i, 0, 0))],
        out_shape=[jax.ShapeDtypeStruct((b, m, d), BF16)] * 2,
        compiler_params=_params(("arbitrary",)),
        name="memkv",
    )(mem, g, w_xkv)


def _mixer_in_kernel(x_ref, gmix_ref, win_ref, convw_ref, convb_ref, wg_ref, bg_ref, lam_ref, glru_ref,
                     cos_ref, sina_ref, sinb_ref,
                     ylru_ref, q_ref, k_ref, v_ref, xpad_ref, hcar_ref, ascan_ref, uscan_ref):
    c = ylru_ref.shape[-1]
    si = pl.program_id(1)

    @pl.when(si == 0)
    def _():
        xpad_ref[0:SUBLANES, :] = jnp.zeros((SUBLANES, c), F32)
        hcar_ref[...] = jnp.zeros_like(hcar_ref)

    ts = x_ref.shape[1]
    xn = _rms(x_ref[0], gmix_ref[...]).astype(BF16)
    proj = jnp.dot(xn, win_ref[...], preferred_element_type=F32)
    x_lru = proj[:, 0:c]
    g_lru = proj[:, c:2 * c]
    q = proj[:, 2 * c:3 * c]
    k = proj[:, 3 * c:4 * c]
    v = proj[:, 4 * c:5 * c]

    xpad_ref[SUBLANES:SUBLANES + ts, :] = x_lru
    cw = convw_ref[...]
    xc = convb_ref[...] + cw[3:4] * x_lru
    for j in range(1, CONV_WIDTH):
        xc = xc + cw[3 - j:4 - j] * xpad_ref[SUBLANES - j:SUBLANES - j + ts, :]
    xpad_ref[0:SUBLANES, :] = x_lru[ts - SUBLANES:ts, :]

    gates = jnp.dot(xc.astype(BF16), wg_ref[...], preferred_element_type=F32) + bg_ref[...]
    r = jax.nn.sigmoid(gates[:, 0:c])
    i = jax.nn.sigmoid(gates[:, c:2 * c])
    neg_lam = -lam_ref[...]
    softplus = jnp.maximum(neg_lam, 0.0) + jnp.log1p(jnp.exp(-jnp.abs(neg_lam)))
    log_a = -RG_C * r * softplus
    a = jnp.exp(log_a)
    u = jnp.sqrt(-jnp.tanh(log_a) * (a * a + 1.0)) * (i * xc)

    groups = ts // SUBLANES
    cols = c // LANES
    for j in range(cols):
        ascan_ref[j] = a[:, j * LANES:(j + 1) * LANES]
        uscan_ref[j] = u[:, j * LANES:(j + 1) * LANES]
    for r in range(SUBLANES):
        rows = pl.ds(r, groups, stride=SUBLANES)
        a_r = jnp.concatenate([ascan_ref[j, rows, :] for j in range(cols)], axis=1)
        u_r = jnp.concatenate([uscan_ref[j, rows, :] for j in range(cols)], axis=1)
        if r:
            u_r = a_r * u_acc + u_r
            a_r = a_r * a_acc
            for j in range(cols):
                ascan_ref[j, rows, :] = a_r[:, j * LANES:(j + 1) * LANES]
                uscan_ref[j, rows, :] = u_r[:, j * LANES:(j + 1) * LANES]
        a_acc, u_acc = a_r, u_r
    carry = hcar_ref[...]
    hs = []
    for gi in range(groups):
        blk = slice(gi * SUBLANES, (gi + 1) * SUBLANES)
        a_g = jnp.concatenate([ascan_ref[j, blk, :] for j in range(cols)], axis=1)
        u_g = jnp.concatenate([uscan_ref[j, blk, :] for j in range(cols)], axis=1)
        hs.append(a_g * carry + u_g)
        carry = a_acc[gi:gi + 1] * carry + u_acc[gi:gi + 1]
    h = jnp.concatenate(hs, axis=0)
    hcar_ref[...] = carry

    y = h * jax.nn.gelu(g_lru)
    ylru_ref[0] = _rms(y, glru_ref[...]).astype(BF16)

    wide = lambda t_ref: jnp.concatenate([t_ref[...]] * (c // LANES), axis=1)
    cos, sina, sinb = wide(cos_ref), wide(sina_ref), wide(sinb_ref)
    half = ROT_DIM // 2

    def rope(t):
        return t * cos + pltpu.roll(t, c - half, axis=1) * sina + pltpu.roll(t, half, axis=1) * sinb

    q_ref[0] = rope(q)
    k_ref[0] = rope(k)
    v_ref[0] = v.astype(BF16)


def _mixer_in(x, gmix, w_in, conv_w, conv_b, w_gates, b_gates, lam, glru, cos_t, sina_t, sinb_t, ts):
    b, s, d = x.shape
    c = conv_w.shape[-1]
    full = lambda shape: pl.BlockSpec(shape, lambda bi, si: (0,) * len(shape))
    tab = pl.BlockSpec((ts, LANES), lambda bi, si: (si, 0))
    seq = lambda: pl.BlockSpec((1, ts, c), lambda bi, si: (bi, si, 0))
    return pl.pallas_call(
        _mixer_in_kernel,
        grid=(b, s // ts),
        in_specs=[pl.BlockSpec((1, ts, d), lambda bi, si: (bi, si, 0)),
                  full((1, d)), full(w_in.shape), full(conv_w.shape), full((1, c)),
                  full(w_gates.shape), full((1, 2 * c)), full((1, c)), full((1, c)),
                  tab, tab, tab],
        out_specs=[seq(), seq(), seq(), seq()],
        out_shape=[jax.ShapeDtypeStruct((b, s, c), BF16),
                   jax.ShapeDtypeStruct((b, s, c), F32),
                   jax.ShapeDtypeStruct((b, s, c), F32),
                   jax.ShapeDtypeStruct((b, s, c), BF16)],
        scratch_shapes=[pltpu.VMEM((ts + SUBLANES, c), F32), pltpu.VMEM((1, c), F32),
                        pltpu.VMEM((c // LANES, ts, LANES), F32), pltpu.VMEM((c // LANES, ts, LANES), F32)],
        compiler_params=_params(("arbitrary", "arbitrary")),
        name="mixer_in",
    )(x, gmix, w_in, conv_w, conv_b, w_gates, b_gates, lam, glru, cos_t, sina_t, sinb_t)


def _moba_kernel(q_ref, k_ref, v_ref, kpat_ref, o_ref, qa_ref, ka_ref):
    s_len = q_ref.shape[1]
    nb = s_len // MOBA_BLOCK
    n_pick = min(MOBA_TOPK, nb)
    q = q_ref[0]
    k = k_ref[0]
    lane = lax.broadcasted_iota(jnp.int32, (1, LANES), 1)
    kmean = jnp.concatenate(
        [jnp.sum(k[n * MOBA_BLOCK:(n + 1) * MOBA_BLOCK], axis=0, keepdims=True) for n in range(nb)],
        axis=0) * (1.0 / MOBA_BLOCK)

    blk_row = lax.broadcasted_iota(jnp.int32, (nb, s_len), 0)
    q_blk = lax.broadcasted_iota(jnp.int32, (nb, s_len), 1) // MOBA_BLOCK
    past = blk_row < q_blk

    q_hi = q.astype(BF16)
    q_lo = (q - q_hi.astype(F32)).astype(BF16)
    for h in range(2):
        own = (lane >= h * HEAD_DIM) & (lane < (h + 1) * HEAD_DIM)
        off = (1 - h) * HEAD_DIM
        km = jnp.where(own, kmean, 0.0)
        km_hi = km.astype(BF16)
        km_split = jnp.concatenate([km_hi, (km - km_hi.astype(F32)).astype(BF16)], axis=0)
        parts = (lax.dot_general(km_split, q_hi, NT_DIMS, preferred_element_type=F32)
                 + lax.dot_general(km_split, q_lo, NT_DIMS, preferred_element_type=F32))
        gate = jnp.where(past, parts[0:nb] + parts[nb:2 * nb], -jnp.inf)
        rank = jnp.zeros((nb, s_len), jnp.int32)
        for m in range(nb):
            gm = gate[m:m + 1, :]
            ahead = (gm > gate) | ((gm == gate) & (m < blk_row))
            rank = rank + ahead.astype(jnp.int32)
        allowed = ((rank < n_pick) & past) | (blk_row == q_blk)
        bias = jnp.where(allowed, 0.0, NEG_BIG)
        pieces = []
        if off:
            pieces.append(jnp.zeros((off, s_len), F32))
        pieces.append(bias)
        if LANES - off - nb:
            pieces.append(jnp.zeros((LANES - off - nb, s_len), F32))
        bias_lanes = jnp.concatenate(pieces, axis=0).T
        qa_ref[h] = jnp.where(own, q * (HEAD_DIM ** -0.5 * LOG2_E), bias_lanes).astype(BF16)
        ka_ref[h] = jnp.where(own, k.astype(BF16), kpat_ref[h])

    own0 = lane < HEAD_DIM

    parts = MOBA_BLOCK // MOBA_QROWS

    def q_rows(qi, part):
        return slice(qi * MOBA_BLOCK + part * MOBA_QROWS, qi * MOBA_BLOCK + (part + 1) * MOBA_QROWS)

    def n_keys(qi, part):
        return qi * MOBA_BLOCK + (part + 1) * MOBA_QROWS

    def scores(qi, part, h):
        return lax.dot_general(qa_ref[h, q_rows(qi, part), :], ka_ref[h, 0:n_keys(qi, part), :], NT_DIMS,
                               preferred_element_type=F32)

    units = [(qi, part, h) for qi in range(nb) for part in range(parts) for h in range(2)]
    ahead = [scores(*u) for u in units[:MOBA_LOOKAHEAD]]
    outs = []

    def weighted_values(p, l, qi, part, h):
        outs.append(jnp.dot(p, v_ref[0, 0:n_keys(qi, part), :], preferred_element_type=F32) / l)
        if h:
            o_ref[0, q_rows(qi, part), :] = jnp.where(own0, outs[0], outs[1])
            outs.clear()

    pending = None
    for n, (qi, part, h) in enumerate(units):
        s = ahead.pop(0)
        if n + MOBA_LOOKAHEAD < len(units):
            ahead.append(scores(*units[n + MOBA_LOOKAHEAD]))
        n_past = qi * MOBA_BLOCK
        own_keys = (part + 1) * MOBA_QROWS
        key_in_blk = lax.broadcasted_iota(jnp.int32, (MOBA_QROWS, own_keys), 1)
        q_in_blk = lax.broadcasted_iota(jnp.int32, (MOBA_QROWS, own_keys), 0) + part * MOBA_QROWS
        s_own = jnp.where(key_in_blk <= q_in_blk, s[:, n_past:], NEG_BIG)
        s = jnp.concatenate([s[:, :n_past], s_own], axis=1) if qi else s_own
        p = jnp.exp2(s - jnp.max(s, axis=-1, keepdims=True))
        l = jnp.sum(p, axis=-1, keepdims=True)
        if pending is not None:
            weighted_values(*pending)
        pending = (p.astype(BF16), l, qi, part, h)
    weighted_values(*pending)


def _key_block_pattern(s):
    pat = np.zeros((2, s, LANES), np.float32)
    for h in range(2):
        off = (1 - h) * HEAD_DIM
        pat[h, np.arange(s), off + np.arange(s) // MOBA_BLOCK] = 1.0
    return jnp.asarray(pat, BF16)


def _moba(q, k, v):
    b, s, c = q.shape
    spec = lambda: pl.BlockSpec((1, s, LANES), lambda bi, hi: (bi, 0, hi))
    return pl.pallas_call(
        _moba_kernel,
        grid=(b, c // LANES),
        in_specs=[spec(), spec(), spec(), pl.BlockSpec((2, s, LANES), lambda bi, hi: (0, 0, 0))],
        out_specs=spec(),
        out_shape=jax.ShapeDtypeStruct((b, s, c), F32),
        scratch_shapes=[pltpu.VMEM((2, s, LANES), BF16), pltpu.VMEM((2, s, LANES), BF16)],
        compiler_params=_params(("arbitrary", "arbitrary")),
        name="moba",
    )(q, k, v, _key_block_pattern(s))


def _post_kernel(x_ref, ylru_ref, yatt_ref, gatt_ref, woa_ref, wob_ref, gcross_ref, wxq_ref, kx_ref, vx_ref,
                 wxo_ref, gffn_ref, wr_ref, br_ref, tri_ref, elow_ref,
                 h2_ref, hn2_ref, gcol_ref, cnt_ref, pos_ref, tcnt_ref, carry_ref):
    tm = x_ref.shape[1]
    d = x_ref.shape[2]
    xd = d // N_XHEADS
    first = (pl.program_id(0) == 0) & (pl.program_id(1) == 0)

    @pl.when(first)
    def _():
        carry_ref[...] = jnp.zeros_like(carry_ref)

    ya = _rms(yatt_ref[0], gatt_ref[...]).astype(BF16)
    mix = (jnp.dot(ylru_ref[0], woa_ref[...], preferred_element_type=F32)
           + jnp.dot(ya, wob_ref[...], preferred_element_type=F32))
    h1 = x_ref[0] + mix

    hn = _rms(h1, gcross_ref[...]).astype(BF16)
    qx = jnp.dot(hn, wxq_ref[...], preferred_element_type=F32)
    heads = []
    for hh in range(N_XHEADS):
        qh = (qx[:, hh * xd:(hh + 1) * xd] * (xd ** -0.5)).astype(BF16)
        s = lax.dot_general(qh, kx_ref[0, :, hh * xd:(hh + 1) * xd], NT_DIMS, preferred_element_type=F32)
        p = jnp.exp(s - jnp.max(s, axis=-1, keepdims=True))
        l = jnp.sum(p, axis=-1, keepdims=True)
        o = jnp.dot(p.astype(BF16), vx_ref[0, :, hh * xd:(hh + 1) * xd], preferred_element_type=F32) / l
        heads.append(o.astype(BF16))
    h2 = h1 + jnp.dot(jnp.concatenate(heads, axis=1), wxo_ref[...], preferred_element_type=F32)
    hn2 = _rms(h2, gffn_ref[...])
    h2_ref[...] = h2
    hn2_ref[...] = hn2.astype(BF16)

    x_hi = hn2.astype(BF16)
    x_lo = (hn2 - x_hi.astype(F32)).astype(BF16)
    by_hi = lax.dot_general(wr_ref[...], x_hi, NT_DIMS, preferred_element_type=F32)
    by_lo = lax.dot_general(wr_ref[0:ROUTER_ROWS], x_lo, NT_DIMS, preferred_element_type=F32)
    logits = by_hi[0:ROUTER_ROWS] + by_hi[ROUTER_ROWS:2 * ROUTER_ROWS] + by_lo + br_ref[...]
    gl = logits[0:N_GROUPS]
    gmax = jnp.max(gl, axis=0, keepdims=True)
    gi = lax.broadcasted_iota(jnp.int32, gl.shape, 0)
    grp = jnp.min(jnp.where(gl == gmax, gi, N_GROUPS), axis=0, keepdims=True)
    g_w = 1.0 / jnp.sum(jnp.exp(gl - gmax), axis=0, keepdims=True)
    el = jnp.zeros((EXPERTS_PER_GROUP, tm), F32)
    for g in range(N_GROUPS):
        lo = SUBLANES + g * EXPERTS_PER_GROUP
        el = jnp.where(grp == g, logits[lo:lo + EXPERTS_PER_GROUP], el)
    ee = jnp.exp(el - jnp.max(el, axis=0, keepdims=True))
    ep = ee / jnp.sum(ee, axis=0, keepdims=True)
    ei = lax.broadcasted_iota(jnp.int32, ep.shape, 0)
    p1 = jnp.max(ep, axis=0, keepdims=True)
    i1 = jnp.min(jnp.where(ep == p1, ei, EXPERTS_PER_GROUP), axis=0, keepdims=True)
    ep_rest = jnp.where(ei == i1, -1.0, ep)
    p2 = jnp.max(ep_rest, axis=0, keepdims=True)
    i2 = jnp.min(jnp.where(ep_rest == p2, ei, EXPERTS_PER_GROUP), axis=0, keepdims=True)
    den = p1 + p2
    gate1 = g_w * p1 / den
    gate2 = g_w * p2 / den
    e1 = grp * EXPERTS_PER_GROUP + i1
    e2 = grp * EXPERTS_PER_GROUP + i2
    li = lax.broadcasted_iota(jnp.int32, (LANES, tm), 0)
    gcol_ref[...] = jnp.where(li == 0, gate1, jnp.where(li == 1, gate2, 0.0)).T

    xi = lax.broadcasted_iota(jnp.int32, (N_EXPERTS, tm), 0)
    oh1 = xi == e1
    oh2 = xi == e2
    cnt = oh1.astype(F32) + oh2.astype(F32)
    tile_cnt = jnp.broadcast_to(jnp.sum(cnt, axis=1, keepdims=True), (N_EXPERTS, LANES))
    carry_ref[...] = carry_ref[...] + tile_cnt
    cnt_ref[...] = carry_ref[...]
    tcnt_ref[...] = tile_cnt
    in_tile = jnp.dot(cnt.astype(BF16), tri_ref[...], preferred_element_type=F32)
    tile_start = jnp.dot(elow_ref[...], tile_cnt, precision=lax.Precision.HIGHEST,
                         preferred_element_type=F32)[:, 0:1]
    local = in_tile + tile_start
    pos_ref[0:1, :] = jnp.sum(jnp.where(oh1, local, 0.0), axis=0, keepdims=True).astype(jnp.int32)
    pos_ref[1:2, :] = jnp.sum(jnp.where(oh2, local, 0.0), axis=0, keepdims=True).astype(jnp.int32)


def _post(x, ylru, yatt, gatt, wo_a, wo_b, gcross, wxq, kx, vx, wxo, gffn, wr_t, br_t, tri, tm):
    b, s, d = x.shape
    c = ylru.shape[-1]
    m = kx.shape[1]
    n = b * s
    nt = s // tm
    full = lambda shape: pl.BlockSpec(shape, lambda bi, si: (0,) * len(shape))
    tok = lambda rows: pl.BlockSpec((rows, tm), lambda bi, si: (0, bi * nt + si))
    return pl.pallas_call(
        _post_kernel,
        grid=(b, nt),
        in_specs=[pl.BlockSpec((1, tm, d), lambda bi, si: (bi, si, 0)),
                  pl.BlockSpec((1, tm, c), lambda bi, si: (bi, si, 0)),
                  pl.BlockSpec((1, tm, c), lambda bi, si: (bi, si, 0)),
                  full((1, c)), full((c, d)), full((c, d)), full((1, d)), full((d, d)),
                  pl.BlockSpec((1, m, d), lambda bi, si: (bi, 0, 0)),
                  pl.BlockSpec((1, m, d), lambda bi, si: (bi, 0, 0)),
                  full((d, d)), full((1, d)), full(wr_t.shape), full(br_t.shape), full((tm, tm)),
                  full((N_EXPERTS, N_EXPERTS))],
        out_specs=[pl.BlockSpec((tm, d), lambda bi, si: (bi * nt + si, 0)),
                   pl.BlockSpec((tm, d), lambda bi, si: (bi * nt + si, 0)),
                   pl.BlockSpec((tm, LANES), lambda bi, si: (bi * nt + si, 0)),
                   full((N_EXPERTS, LANES)),
                   tok(EXPERT_TOPK),
                   pl.BlockSpec((N_EXPERTS, LANES), lambda bi, si: (bi * nt + si, 0))],
        out_shape=[jax.ShapeDtypeStruct((n, d), F32),
                   jax.ShapeDtypeStruct((n, d), BF16),
                   jax.ShapeDtypeStruct((n, LANES), F32),
                   jax.ShapeDtypeStruct((N_EXPERTS, LANES), F32),
                   jax.ShapeDtypeStruct((EXPERT_TOPK, n), jnp.int32),
                   jax.ShapeDtypeStruct((n // tm * N_EXPERTS, LANES), F32)],
        scratch_shapes=[pltpu.VMEM((N_EXPERTS, LANES), F32)],
        compiler_params=_params(("arbitrary", "arbitrary")),
        name="post",
    )(x, ylru, yatt, gatt, wo_a, wo_b, gcross, wxq, kx, vx, wxo, gffn, wr_t, br_t, tri,
      jnp.asarray(np.tril(np.ones((N_EXPERTS, N_EXPERTS), np.float32), -1)))


def _pieces(largest):
    piece = largest
    while piece:
        yield piece
        piece //= 2


def _start_run_copies(length, largest, copy_of):
    def start(pieces):
        for piece in pieces:
            done = length & ~(2 * piece - 1)

            @pl.when((length & piece) != 0)
            def _():
                copy_of(done, piece).start()

    @pl.when(length >= RUN_SPLIT)
    def _():
        start([p for p in _pieces(largest) if p >= RUN_SPLIT])

    start([p for p in _pieces(largest) if p < RUN_SPLIT])


def _tile_rows(row, count):
    return pl.ds(pl.multiple_of(row * SUBLANES, SUBLANES), count * SUBLANES)


def _dispatch_kernel(fill_ref, rstart_ref, rlen_ref, hn_ref, pos_ref, xout_ref, sorted_ref, zero_ref, sem, zsem):
    tm = hn_ref.shape[0]
    n_sorted = EXPERT_TOPK * tm
    step = pl.program_id(0)

    zero_rows = zero_ref.shape[0] // SUBLANES
    n_rows = xout_ref.shape[0] // SUBLANES

    def pad_copies(e, start):
        first = fill_ref[e]
        length = fill_ref[N_EXPERTS + e]
        for piece in _pieces(zero_rows):
            offset = first + (length & ~(2 * piece - 1))

            @pl.when((length & piece) != 0)
            def _():
                cp = pltpu.make_async_copy(zero_ref.at[_tile_rows(0, piece)], xout_ref.at[_tile_rows(offset, piece)],
                                           zsem)
                cp.start() if start else cp.wait()

    def tail_copies(start):
        def one(i, carry):
            cp = pltpu.make_async_copy(zero_ref, xout_ref.at[_tile_rows(i * zero_rows, zero_rows)], zsem)
            cp.start() if start else cp.wait()
            return carry

        lax.fori_loop(fill_ref[2 * N_EXPERTS] // zero_rows, n_rows // zero_rows, one, 0)

    @pl.when(step == 0)
    def _():
        zero_ref[...] = jnp.zeros_like(zero_ref)
        lax.fori_loop(0, N_EXPERTS, lambda e, c: (pad_copies(e, True), c)[1], 0)
        tail_copies(True)

    pos = pos_ref[...]
    p_iota = lax.broadcasted_iota(jnp.int32, (n_sorted, tm), 0)
    picks = jnp.where((p_iota == pos[0:1, :]) | (p_iota == pos[1:2, :]), 1.0, 0.0).astype(BF16)
    xs = jnp.dot(picks, hn_ref[...], preferred_element_type=F32)
    for k in range(SUBLANES):
        sorted_ref[pl.ds(k, n_sorted, stride=SUBLANES), :] = xs[:, k * LANES:(k + 1) * LANES]

    def run(e, offset):
        length = rlen_ref[step * N_EXPERTS + e]
        first = rstart_ref[step * N_EXPERTS + e]

        _start_run_copies(length, tm, lambda done, piece: pltpu.make_async_copy(
            sorted_ref.at[_tile_rows(offset + done, piece)], xout_ref.at[_tile_rows(first + done, piece)], sem))
        return offset + length

    lax.fori_loop(0, N_EXPERTS, run, 0)

    pltpu.make_async_copy(sorted_ref, xout_ref.at[_tile_rows(0, n_sorted)], sem).wait()

    @pl.when(step == 0)
    def _():
        lax.fori_loop(0, N_EXPERTS, lambda e, c: (pad_copies(e, False), c)[1], 0)
        tail_copies(False)


def _dispatch(fill, run_start, run_len, hn2, pos, n_rows, tm):
    n, d = hn2.shape
    assert d == SUBLANES * LANES, "a row must be exactly one (8, 128) tile for the tile-contiguous layout"
    return pl.pallas_call(
        _dispatch_kernel,
        grid_spec=pltpu.PrefetchScalarGridSpec(
            num_scalar_prefetch=3,
            grid=(n // tm,),
            in_specs=[pl.BlockSpec((tm, d), lambda i, *_: (i, 0)),
                      pl.BlockSpec((EXPERT_TOPK, tm), lambda i, *_: (0, i))],
            out_specs=pl.BlockSpec(memory_space=pl.ANY),
            scratch_shapes=[pltpu.VMEM((EXPERT_TOPK * tm * SUBLANES, LANES), F32),
                            pltpu.VMEM((EXPERT_ROWS // 2 * SUBLANES, LANES), F32),
                            pltpu.SemaphoreType.DMA(()), pltpu.SemaphoreType.DMA(())]),
        out_shape=jax.ShapeDtypeStruct((n_rows * SUBLANES, LANES), F32),
        compiler_params=_params(("arbitrary",)),
        name="dispatch",
    )(fill, run_start, run_len, hn2, pos)


def _experts_kernel(be_ref, nu_ref, x_ref, wgu_ref, wd_ref, y_ref, wgu_bf_ref, wd_bf_ref):
    j = pl.program_id(0)
    rows = x_ref.shape[0] // SUBLANES
    f = wd_ref.shape[1]
    used = j < nu_ref[0]

    @pl.when(used & ((j == 0) | (be_ref[j] != be_ref[jnp.maximum(j, 1) - 1])))
    def _():
        wgu_bf_ref[...] = wgu_ref[0].astype(BF16)
        wd_bf_ref[...] = wd_ref[0].astype(BF16)

    @pl.when(used)
    def _():
        x = jnp.concatenate([x_ref[pl.ds(k, rows, stride=SUBLANES), :] for k in range(SUBLANES)], axis=1)
        gu = jnp.dot(x.astype(BF16), wgu_bf_ref[...], preferred_element_type=F32)
        act = jax.nn.silu(gu[:, 0:f]) * gu[:, f:2 * f]
        y = jnp.dot(act.astype(BF16), wd_bf_ref[...], preferred_element_type=F32)
        for k in range(SUBLANES):
            y_ref[pl.ds(k, rows, stride=SUBLANES), :] = y[:, k * LANES:(k + 1) * LANES]

    @pl.when(jnp.logical_not(used))
    def _():
        y_ref[...] = jnp.zeros_like(y_ref)


def _experts(blk_expert, n_used, x_buf, w_gate_up, w_down):
    d = w_gate_up.shape[1]
    f = w_down.shape[1]
    r = EXPERT_ROWS
    n_blk = x_buf.shape[0] // (r * SUBLANES)
    assert d == SUBLANES * LANES and x_buf.shape[1] == LANES, "rows are (8, 128) tiles (tile-contiguous layout)"
    return pl.pallas_call(
        _experts_kernel,
        grid_spec=pltpu.PrefetchScalarGridSpec(
            num_scalar_prefetch=2,
            grid=(n_blk,),
            in_specs=[pl.BlockSpec((r * SUBLANES, LANES), lambda j, be, nu: (jnp.minimum(j, jnp.maximum(nu[0], 1) - 1), 0)),
                      pl.BlockSpec((1, d, 2 * f), lambda j, be, nu: (be[j], 0, 0)),
                      pl.BlockSpec((1, f, d), lambda j, be, nu: (be[j], 0, 0))],
            out_specs=pl.BlockSpec((r * SUBLANES, LANES), lambda j, be, nu: (j, 0)),
            scratch_shapes=[pltpu.VMEM((d, 2 * f), BF16), pltpu.VMEM((f, d), BF16)]),
        out_shape=jax.ShapeDtypeStruct(x_buf.shape, F32),
        compiler_params=_params(("arbitrary",)),
        name="experts",
    )(blk_expert, n_used, x_buf, w_gate_up, w_down)


def _combine_kernel(pos_ref, rstart_ref, rlen_ref, h2_ref, gcol_ref, gfin_ref, y_ref, o_ref,
                    ysort_ref, pick0_ref, pick1_ref, sems):
    tc = h2_ref.shape[0]
    n = pos_ref.shape[0] // EXPERT_TOPK
    n_sorted = EXPERT_TOPK * tc
    step = pl.program_id(0)
    par = step % 2
    base = step * tc

    def start_gather(tile, buf):
        def run(e, offset):
            length = rlen_ref[tile * N_EXPERTS + e]
            first = rstart_ref[tile * N_EXPERTS + e]
            _start_run_copies(length, tc, lambda done, piece: pltpu.make_async_copy(
                y_ref.at[_tile_rows(first + done, piece)], ysort_ref.at[buf, _tile_rows(offset + done, piece)],
                sems.at[buf]))
            return offset + length

        lax.fori_loop(0, N_EXPERTS, run, 0)

    @pl.when(step == 0)
    def _():
        start_gather(0, 0)

    @pl.when(step + 1 < pl.num_programs(0))
    def _():
        start_gather(step + 1, 1 - par)

    pltpu.make_async_copy(y_ref.at[_tile_rows(0, n_sorted)], ysort_ref.at[par], sems.at[par]).wait()
    for t in range(tc):
        for slot, pick_ref in enumerate((pick0_ref, pick1_ref)):
            row = pos_ref[slot * n + base + t]
            pick_ref[t * SUBLANES:(t + 1) * SUBLANES, :] = ysort_ref[par, _tile_rows(row, 1), :]

    def rows_of(y_ref):
        return jnp.concatenate([y_ref[pl.ds(k, tc, stride=SUBLANES), :] for k in range(SUBLANES)], axis=1)

    g = gcol_ref[...]
    moe = g[:, 0:1] * rows_of(pick0_ref) + g[:, 1:2] * rows_of(pick1_ref)
    o_ref[...] = _rms(h2_ref[...] + moe, gfin_ref[...])


def _combine(pos_flat, run_start, run_len, h2, gcol, gfin, y_buf, tc):
    n, d = h2.shape
    return pl.pallas_call(
        _combine_kernel,
        grid_spec=pltpu.PrefetchScalarGridSpec(
            num_scalar_prefetch=3,
            grid=(n // tc,),
            in_specs=[pl.BlockSpec((tc, d), lambda i, *_: (i, 0)),
                      pl.BlockSpec((tc, LANES), lambda i, *_: (i, 0)),
                      pl.BlockSpec((1, d), lambda i, *_: (0, 0)),
                      pl.BlockSpec(memory_space=pl.ANY)],
            out_specs=pl.BlockSpec((tc, d), lambda i, *_: (i, 0)),
            scratch_shapes=[pltpu.VMEM((2, EXPERT_TOPK * tc * SUBLANES, LANES), F32),
                            pltpu.VMEM((tc * SUBLANES, LANES), F32), pltpu.VMEM((tc * SUBLANES, LANES), F32),
                            pltpu.SemaphoreType.DMA((2,))]),
        out_shape=jax.ShapeDtypeStruct((n, d), F32),
        compiler_params=_params(("arbitrary",)),
        name="combine",
    )(pos_flat, run_start, run_len, h2, gcol, gfin, y_buf)


def _rope_tables(s):
    half = ROT_DIM // 2
    inv_freq = (ROPE_THETA ** (-np.arange(0, ROT_DIM, 2, dtype=np.float32) / ROT_DIM)).astype(np.float32)
    ang = np.arange(s, dtype=np.float32)[:, None] * inv_freq[None, :]
    cos, sin = np.cos(ang), np.sin(ang)
    zeros = lambda w: np.zeros((s, w), np.float32)
    cos_h = np.concatenate([cos, cos, np.ones((s, HEAD_DIM - ROT_DIM), np.float32)], axis=1)
    sina_h = np.concatenate([-sin, zeros(HEAD_DIM - half)], axis=1)
    sinb_h = np.concatenate([zeros(half), sin, zeros(HEAD_DIM - ROT_DIM)], axis=1)
    tile = lambda t: jnp.asarray(np.tile(t, (1, LANES // HEAD_DIM)), F32)
    return tile(cos_h), tile(sina_h), tile(sinb_h)


def _block_diag(w):
    nblk, bi, bo = w.shape
    eye = jnp.eye(nblk, dtype=w.dtype)
    return jnp.einsum('hij,hg->higj', w, eye).reshape(nblk * bi, nblk * bo)


def _layer(h, mem, norm_mix, w_in, conv_w, conv_b, w_rg, b_rg, w_ig, b_ig, lru_lambda, norm_lru_out,
           norm_attn_out, w_out, norm_cross, norm_mem, w_xq, w_xkv, w_xo, norm_ffn, w_router_group,
           b_router_group, w_router_expert, b_router_expert, w_gate_up, w_down, norm_out):
    b, s, d = h.shape
    c = conv_w.shape[-1]
    n = b * s
    row = lambda t: t.reshape(1, -1)
    tm = 512 if s % 512 == 0 else MOBA_BLOCK
    ts = tm
    assert s % ts == 0 and s % tm == 0 and c == N_HEADS * HEAD_DIM

    kx, vx = _memkv(mem, row(norm_mem), w_xkv.astype(BF16))

    w_gates = jnp.concatenate([_block_diag(w_rg), _block_diag(w_ig)], axis=1).astype(BF16)
    b_gates = jnp.concatenate([b_rg, b_ig]).reshape(1, -1)
    cos_t, sina_t, sinb_t = _rope_tables(s)
    ylru, q, k, v = _mixer_in(h, row(norm_mix), w_in.astype(BF16), conv_w, row(conv_b), w_gates, b_gates,
                              row(lru_lambda), row(norm_lru_out), cos_t, sina_t, sinb_t, ts)
    yatt = _moba(q, k, v)

    w_out_b = w_out.astype(BF16)
    wr_t = jnp.zeros((ROUTER_ROWS, d), F32)
    wr_t = wr_t.at[0:N_GROUPS].set(w_router_group.T).at[SUBLANES:SUBLANES + N_EXPERTS].set(w_router_expert.T)
    wr_hi = wr_t.astype(BF16)
    wr_t = jnp.concatenate([wr_hi, (wr_t - wr_hi.astype(F32)).astype(BF16)], axis=0)
    br_t = jnp.zeros((ROUTER_ROWS, 1), F32)
    br_t = br_t.at[0:N_GROUPS, 0].set(b_router_group).at[SUBLANES:SUBLANES + N_EXPERTS, 0].set(b_router_expert)
    tri = jnp.asarray(np.triu(np.ones((tm, tm), np.float32), 1), BF16)
    h2, hn2, gcol, counts, pos, tile_cnt = _post(
        h, ylru, yatt, row(norm_attn_out), w_out_b[:c], w_out_b[c:], row(norm_cross), w_xq.astype(BF16),
        kx, vx, w_xo.astype(BF16), row(norm_ffn), wr_t, br_t, tri, tm)

    counts = counts[:, 0].astype(jnp.int32)
    padded = (counts + EXPERT_ROWS - 1) // EXPERT_ROWS * EXPERT_ROWS
    pad_ends = jnp.cumsum(padded)
    pad_starts = pad_ends - padded
    n_blocks = n * EXPERT_TOPK // EXPERT_ROWS + N_EXPERTS
    blk_first = jnp.arange(n_blocks, dtype=jnp.int32) * EXPERT_ROWS
    blk_expert = jnp.minimum(jnp.sum(blk_first[:, None] >= pad_ends[None, :], axis=1), N_EXPERTS - 1).astype(jnp.int32)
    n_used = (pad_ends[-1:] // EXPERT_ROWS).astype(jnp.int32)
    fill = jnp.concatenate([pad_starts + counts, padded - counts, pad_ends[-1:]]).astype(jnp.int32)

    n_tiles = n // tm
    tile_cnt = tile_cnt[:, 0].astype(jnp.int32).reshape(n_tiles, N_EXPERTS)
    run_start = (pad_starts[None, :] + jnp.cumsum(tile_cnt, axis=0) - tile_cnt).astype(jnp.int32).reshape(-1)
    run_len = tile_cnt.reshape(-1)
    x_buf = _dispatch(fill, run_start, run_len, hn2, pos, n_blocks * EXPERT_ROWS, tm)
    y_buf = _experts(blk_expert, n_used, x_buf, w_gate_up, w_down)
    out = _combine(pos.reshape(-1), run_start, run_len, h2, gcol, row(norm_out), y_buf, tm)
    return out.reshape(b, s, d)


def kernel(x, mem, norm_mix, w_in, conv_w, conv_b, w_rg, b_rg, w_ig, b_ig, lru_lambda, norm_lru_out, norm_attn_out,
           w_out, norm_cross, norm_mem, w_xq, w_xkv, w_xo, norm_ffn, w_router_group, b_router_group,
           w_router_expert, b_router_expert, w_gate_up, w_down, norm_final):
    depth = norm_mix.shape[0]
    assert depth == 1, "the fused final norm assumes a single layer"
    l = 0
    return _layer(x, mem, norm_mix[l], w_in[l], conv_w[l], conv_b[l], w_rg[l], b_rg[l], w_ig[l], b_ig[l],
                  lru_lambda[l], norm_lru_out[l], norm_attn_out[l], w_out[l], norm_cross[l], norm_mem[l],
                  w_xq[l], w_xkv[l], w_xo[l], norm_ffn[l], w_router_group[l], b_router_group[l],
                  w_router_expert[l], b_router_expert[l], w_gate_up[l], w_down[l], norm_final)
```

```python
import jax
import jax.numpy as jnp
import numpy as np
from jax import lax
from jax.experimental import pallas as pl
from jax.experimental.pallas import tpu as pltpu

F32 = jnp.float32
BF16 = jnp.bfloat16

CONV_WIDTH = 4
RG_C = 8.0
N_HEADS = 8
HEAD_DIM = 64
ROT_DIM = HEAD_DIM // 4
ROPE_THETA = 500000.0
MOBA_BLOCK = 256
MOBA_TOPK = 3
N_XHEADS = 4
N_GROUPS = 4
EXPERTS_PER_GROUP = 8
N_EXPERTS = N_GROUPS * EXPERTS_PER_GROUP
EXPERT_TOPK = 2
EPS = 1e-6

LANES = 128
SUBLANES = 8
NEG_BIG = -1e30
LOG2_E = 1.4426950408889634
MOBA_QROWS = 128
MOBA_LOOKAHEAD = 5
EXPERT_ROWS = 512
RUN_SPLIT = 64
BF16_ROWS = 2 * SUBLANES
ROUTER_ROWS = -(-(SUBLANES + N_EXPERTS) // BF16_ROWS) * BF16_ROWS
VMEM_LIMIT = 56 * 1024 * 1024

NT_DIMS = (((1,), (1,)), ((), ()))


def _rms(x, g):
    return x * lax.rsqrt(jnp.mean(x * x, axis=-1, keepdims=True) + EPS) * g


def _params(sem):
    return pltpu.CompilerParams(dimension_semantics=sem, vmem_limit_bytes=VMEM_LIMIT)


def _memkv_kernel(mem_ref, g_ref, w_ref, k_ref, v_ref):
    d = mem_ref.shape[-1]
    mn = _rms(mem_ref[0], g_ref[...]).astype(BF16)
    kv = jnp.dot(mn, w_ref[...], preferred_element_type=F32)
    k_ref[0] = kv[:, :d].astype(BF16)
    v_ref[0] = kv[:, d:].astype(BF16)


def _memkv(mem, g, w_xkv):
    b, m, d = mem.shape
    return pl.pallas_call(
        _memkv_kernel,
        grid=(b,),
        in_specs=[pl.BlockSpec((1, m, d), lambda i: (i, 0, 0)),
                  pl.BlockSpec((1, d), lambda i: (0, 0)),
                  pl.BlockSpec((d, 2 * d), lambda i: (0, 0))],
        out_specs=[pl.BlockSpec((1, m, d), lambda i: (i, 0, 0)),
                   pl.BlockSpec((1, m, d), lambda i: (i, 0, 0))],
        out_shape=[jax.ShapeDtypeStruct((b, m, d), BF16)] * 2,
        compiler_params=_params(("arbitrary",)),
        name="memkv",
    )(mem, g, w_xkv)


def _mixer_in_kernel(x_ref, gmix_ref, win_ref, convw_ref, convb_ref, wg_ref, bg_ref, lam_ref, glru_ref,
                     cos_ref, sina_ref, sinb_ref,
                     ylru_ref, q_ref, k_ref, v_ref, xpad_ref, hcar_ref, ascan_ref, uscan_ref):
    c = ylru_ref.shape[-1]
    si = pl.program_id(1)

    @pl.when(si == 0)
    def _():
        xpad_ref[0:SUBLANES, :] = jnp.zeros((SUBLANES, c), F32)
        hcar_ref[...] = jnp.zeros_like(hcar_ref)

    ts = x_ref.shape[1]
    xn = _rms(x_ref[0], gmix_ref[...]).astype(BF16)
    proj = jnp.dot(xn, win_ref[...], preferred_element_type=F32)
    x_lru = proj[:, 0:c]
    g_lru = proj[:, c:2 * c]
    q = proj[:, 2 * c:3 * c]
    k = proj[:, 3 * c:4 * c]
    v = proj[:, 4 * c:5 * c]

    xpad_ref[SUBLANES:SUBLANES + ts, :] = x_lru
    cw = convw_ref[...]
    xc = convb_ref[...] + cw[3:4] * x_lru
    for j in range(1, CONV_WIDTH):
        xc = xc + cw[3 - j:4 - j] * xpad_ref[SUBLANES - j:SUBLANES - j + ts, :]
    xpad_ref[0:SUBLANES, :] = x_lru[ts - SUBLANES:ts, :]

    gates = jnp.dot(xc.astype(BF16), wg_ref[...], preferred_element_type=F32) + bg_ref[...]
    r = jax.nn.sigmoid(gates[:, 0:c])
    i = jax.nn.sigmoid(gates[:, c:2 * c])
    neg_lam = -lam_ref[...]
    softplus = jnp.maximum(neg_lam, 0.0) + jnp.log1p(jnp.exp(-jnp.abs(neg_lam)))
    log_a = -RG_C * r * softplus
    a = jnp.exp(log_a)
    u = jnp.sqrt(-jnp.tanh(log_a) * (a * a + 1.0)) * (i * xc)

    groups = ts // SUBLANES
    cols = c // LANES
    for j in range(cols):
        ascan_ref[j] = a[:, j * LANES:(j + 1) * LANES]
        uscan_ref[j] = u[:, j * LANES:(j + 1) * LANES]
    for r in range(SUBLANES):
        rows = pl.ds(r, groups, stride=SUBLANES)
        a_r = jnp.concatenate([ascan_ref[j, rows, :] for j in range(cols)], axis=1)
        u_r = jnp.concatenate([uscan_ref[j, rows, :] for j in range(cols)], axis=1)
        if r:
            u_r = a_r * u_acc + u_r
            a_r = a_r * a_acc
            for j in range(cols):
                ascan_ref[j, rows, :] = a_r[:, j * LANES:(j + 1) * LANES]
                uscan_ref[j, rows, :] = u_r[:, j * LANES:(j + 1) * LANES]
        a_acc, u_acc = a_r, u_r
    carry = hcar_ref[...]
    hs = []
    for gi in range(groups):
        blk = slice(gi * SUBLANES, (gi + 1) * SUBLANES)
        a_g = jnp.concatenate([ascan_ref[j, blk, :] for j in range(cols)], axis=1)
        u_g = jnp.concatenate([uscan_ref[j, blk, :] for j in range(cols)], axis=1)
        hs.append(a_g * carry + u_g)
        carry = a_acc[gi:gi + 1] * carry + u_acc[gi:gi + 1]
    h = jnp.concatenate(hs, axis=0)
    hcar_ref[...] = carry

    y = h * jax.nn.gelu(g_lru)
    ylru_ref[0] = _rms(y, glru_ref[...]).astype(BF16)

    wide = lambda t_ref: jnp.concatenate([t_ref[...]] * (c // LANES), axis=1)
    cos, sina, sinb = wide(cos_ref), wide(sina_ref), wide(sinb_ref)
    half = ROT_DIM // 2

    def rope(t):
        return t * cos + pltpu.roll(t, c - half, axis=1) * sina + pltpu.roll(t, half, axis=1) * sinb

    q_ref[0] = rope(q)
    k_ref[0] = rope(k)
    v_ref[0] = v.astype(BF16)


def _mixer_in(x, gmix, w_in, conv_w, conv_b, w_gates, b_gates, lam, glru, cos_t, sina_t, sinb_t, ts):
    b, s, d = x.shape
    c = conv_w.shape[-1]
    full = lambda shape: pl.BlockSpec(shape, lambda bi, si: (0,) * len(shape))
    tab = pl.BlockSpec((ts, LANES), lambda bi, si: (si, 0))
    seq = lambda: pl.BlockSpec((1, ts, c), lambda bi, si: (bi, si, 0))
    return pl.pallas_call(
        _mixer_in_kernel,
        grid=(b, s // ts),
        in_specs=[pl.BlockSpec((1, ts, d), lambda bi, si: (bi, si, 0)),
                  full((1, d)), full(w_in.shape), full(conv_w.shape), full((1, c)),
                  full(w_gates.shape), full((1, 2 * c)), full((1, c)), full((1, c)),
                  tab, tab, tab],
        out_specs=[seq(), seq(), seq(), seq()],
        out_shape=[jax.ShapeDtypeStruct((b, s, c), BF16),
                   jax.ShapeDtypeStruct((b, s, c), F32),
                   jax.ShapeDtypeStruct((b, s, c), F32),
                   jax.ShapeDtypeStruct((b, s, c), BF16)],
        scratch_shapes=[pltpu.VMEM((ts + SUBLANES, c), F32), pltpu.VMEM((1, c), F32),
                        pltpu.VMEM((c // LANES, ts, LANES), F32), pltpu.VMEM((c // LANES, ts, LANES), F32)],
        compiler_params=_params(("arbitrary", "arbitrary")),
        name="mixer_in",
    )(x, gmix, w_in, conv_w, conv_b, w_gates, b_gates, lam, glru, cos_t, sina_t, sinb_t)


def _moba_kernel(q_ref, k_ref, v_ref, kpat_ref, o_ref, qa_ref, ka_ref):
    s_len = q_ref.shape[1]
    nb = s_len // MOBA_BLOCK
    n_pick = min(MOBA_TOPK, nb)
    q = q_ref[0]
    k = k_ref[0]
    lane = lax.broadcasted_iota(jnp.int32, (1, LANES), 1)
    kmean = jnp.concatenate(
        [jnp.sum(k[n * MOBA_BLOCK:(n + 1) * MOBA_BLOCK], axis=0, keepdims=True) for n in range(nb)],
        axis=0) * (1.0 / MOBA_BLOCK)

    blk_row = lax.broadcasted_iota(jnp.int32, (nb, s_len), 0)
    q_blk = lax.broadcasted_iota(jnp.int32, (nb, s_len), 1) // MOBA_BLOCK
    past = blk_row < q_blk

    q_hi = q.astype(BF16)
    q_lo = (q - q_hi.astype(F32)).astype(BF16)
    for h in range(2):
        own = (lane >= h * HEAD_DIM) & (lane < (h + 1) * HEAD_DIM)
        off = (1 - h) * HEAD_DIM
        km = jnp.where(own, kmean, 0.0)
        km_hi = km.astype(BF16)
        km_split = jnp.concatenate([km_hi, (km - km_hi.astype(F32)).astype(BF16)], axis=0)
        parts = (lax.dot_general(km_split, q_hi, NT_DIMS, preferred_element_type=F32)
                 + lax.dot_general(km_split, q_lo, NT_DIMS, preferred_element_type=F32))
        gate = jnp.where(past, parts[0:nb] + parts[nb:2 * nb], -jnp.inf)
        rank = jnp.zeros((nb, s_len), jnp.int32)
        for m in range(nb):
            gm = gate[m:m + 1, :]
            ahead = (gm > gate) | ((gm == gate) & (m < blk_row))
            rank = rank + ahead.astype(jnp.int32)
        allowed = ((rank < n_pick) & past) | (blk_row == q_blk)
        bias = jnp.where(allowed, 0.0, NEG_BIG)
        pieces = []
        if off:
            pieces.append(jnp.zeros((off, s_len), F32))
        pieces.append(bias)
        if LANES - off - nb:
            pieces.append(jnp.zeros((LANES - off - nb, s_len), F32))
        bias_lanes = jnp.concatenate(pieces, axis=0).T
        qa_ref[h] = jnp.where(own, q * (HEAD_DIM ** -0.5 * LOG2_E), bias_lanes).astype(BF16)
        ka_ref[h] = jnp.where(own, k.astype(BF16), kpat_ref[h])

    own0 = lane < HEAD_DIM

    parts = MOBA_BLOCK // MOBA_QROWS

    def q_rows(qi, part):
        return slice(qi * MOBA_BLOCK + part * MOBA_QROWS, qi * MOBA_BLOCK + (part + 1) * MOBA_QROWS)

    def n_keys(qi, part):
        return qi * MOBA_BLOCK + (part + 1) * MOBA_QROWS

    def scores(qi, part, h):
        return lax.dot_general(qa_ref[h, q_rows(qi, part), :], ka_ref[h, 0:n_keys(qi, part), :], NT_DIMS,
                               preferred_element_type=F32)

    units = [(qi, part, h) for qi in range(nb) for part in range(parts) for h in range(2)]
    ahead = [scores(*u) for u in units[:MOBA_LOOKAHEAD]]
    outs = []

    def weighted_values(p, l, qi, part, h):
        outs.append(jnp.dot(p, v_ref[0, 0:n_keys(qi, part), :], preferred_element_type=F32) / l)
        if h:
            o_ref[0, q_rows(qi, part), :] = jnp.where(own0, outs[0], outs[1])
            outs.clear()

    pending = None
    for n, (qi, part, h) in enumerate(units):
        s = ahead.pop(0)
        if n + MOBA_LOOKAHEAD < len(units):
            ahead.append(scores(*units[n + MOBA_LOOKAHEAD]))
        n_past = qi * MOBA_BLOCK
        own_keys = (part + 1) * MOBA_QROWS
        key_in_blk = lax.broadcasted_iota(jnp.int32, (MOBA_QROWS, own_keys), 1)
        q_in_blk = lax.broadcasted_iota(jnp.int32, (MOBA_QROWS, own_keys), 0) + part * MOBA_QROWS
        s_own = jnp.where(key_in_blk <= q_in_blk, s[:, n_past:], NEG_BIG)
        s = jnp.concatenate([s[:, :n_past], s_own], axis=1) if qi else s_own
        p = jnp.exp2(s - jnp.max(s, axis=-1, keepdims=True))
        l = jnp.sum(p, axis=-1, keepdims=True)
        if pending is not None:
            weighted_values(*pending)
        pending = (p.astype(BF16), l, qi, part, h)
    weighted_values(*pending)


def _key_block_pattern(s):
    pat = np.zeros((2, s, LANES), np.float32)
    for h in range(2):
        off = (1 - h) * HEAD_DIM
        pat[h, np.arange(s), off + np.arange(s) // MOBA_BLOCK] = 1.0
    return jnp.asarray(pat, BF16)


def _moba(q, k, v):
    b, s, c = q.shape
    spec = lambda: pl.BlockSpec((1, s, LANES), lambda bi, hi: (bi, 0, hi))
    return pl.pallas_call(
        _moba_kernel,
        grid=(b, c // LANES),
        in_specs=[spec(), spec(), spec(), pl.BlockSpec((2, s, LANES), lambda bi, hi: (0, 0, 0))],
        out_specs=spec(),
        out_shape=jax.ShapeDtypeStruct((b, s, c), F32),
        scratch_shapes=[pltpu.VMEM((2, s, LANES), BF16), pltpu.VMEM((2, s, LANES), BF16)],
        compiler_params=_params(("arbitrary", "arbitrary")),
        name="moba",
    )(q, k, v, _key_block_pattern(s))


def _post_kernel(x_ref, ylru_ref, yatt_ref, gatt_ref, woa_ref, wob_ref, gcross_ref, wxq_ref, kx_ref, vx_ref,
                 wxo_ref, gffn_ref, wr_ref, br_ref, tri_ref, elow_ref,
                 h2_ref, hn2_ref, gcol_ref, cnt_ref, pos_ref, tcnt_ref, carry_ref):
    tm = x_ref.shape[1]
    d = x_ref.shape[2]
    xd = d // N_XHEADS
    first = (pl.program_id(0) == 0) & (pl.program_id(1) == 0)

    @pl.when(first)
    def _():
        carry_ref[...] = jnp.zeros_like(carry_ref)

    ya = _rms(yatt_ref[0], gatt_ref[...]).astype(BF16)
    mix = (jnp.dot(ylru_ref[0], woa_ref[...], preferred_element_type=F32)
           + jnp.dot(ya, wob_ref[...], preferred_element_type=F32))
    h1 = x_ref[0] + mix

    hn = _rms(h1, gcross_ref[...]).astype(BF16)
    qx = jnp.dot(hn, wxq_ref[...], preferred_element_type=F32)
    heads = []
    for hh in range(N_XHEADS):
        qh = (qx[:, hh * xd:(hh + 1) * xd] * (xd ** -0.5)).astype(BF16)
        s = lax.dot_general(qh, kx_ref[0, :, hh * xd:(hh + 1) * xd], NT_DIMS, preferred_element_type=F32)
        p = jnp.exp(s - jnp.max(s, axis=-1, keepdims=True))
        l = jnp.sum(p, axis=-1, keepdims=True)
        o = jnp.dot(p.astype(BF16), vx_ref[0, :, hh * xd:(hh + 1) * xd], preferred_element_type=F32) / l
        heads.append(o.astype(BF16))
    h2 = h1 + jnp.dot(jnp.concatenate(heads, axis=1), wxo_ref[...], preferred_element_type=F32)
    hn2 = _rms(h2, gffn_ref[...])
    h2_ref[...] = h2
    hn2_ref[...] = hn2.astype(BF16)

    x_hi = hn2.astype(BF16)
    x_lo = (hn2 - x_hi.astype(F32)).astype(BF16)
    by_hi = lax.dot_general(wr_ref[...], x_hi, NT_DIMS, preferred_element_type=F32)
    by_lo = lax.dot_general(wr_ref[0:ROUTER_ROWS], x_lo, NT_DIMS, preferred_element_type=F32)
    logits = by_hi[0:ROUTER_ROWS] + by_hi[ROUTER_ROWS:2 * ROUTER_ROWS] + by_lo + br_ref[...]
    gl = logits[0:N_GROUPS]
    gmax = jnp.max(gl, axis=0, keepdims=True)
    gi = lax.broadcasted_iota(jnp.int32, gl.shape, 0)
    grp = jnp.min(jnp.where(gl == gmax, gi, N_GROUPS), axis=0, keepdims=True)
    g_w = 1.0 / jnp.sum(jnp.exp(gl - gmax), axis=0, keepdims=True)
    el = jnp.zeros((EXPERTS_PER_GROUP, tm), F32)
    for g in range(N_GROUPS):
        lo = SUBLANES + g * EXPERTS_PER_GROUP
        el = jnp.where(grp == g, logits[lo:lo + EXPERTS_PER_GROUP], el)
    ee = jnp.exp(el - jnp.max(el, axis=0, keepdims=True))
    ep = ee / jnp.sum(ee, axis=0, keepdims=True)
    ei = lax.broadcasted_iota(jnp.int32, ep.shape, 0)
    p1 = jnp.max(ep, axis=0, keepdims=True)
    i1 = jnp.min(jnp.where(ep == p1, ei, EXPERTS_PER_GROUP), axis=0, keepdims=True)
    ep_rest = jnp.where(ei == i1, -1.0, ep)
    p2 = jnp.max(ep_rest, axis=0, keepdims=True)
    i2 = jnp.min(jnp.where(ep_rest == p2, ei, EXPERTS_PER_GROUP), axis=0, keepdims=True)
    den = p1 + p2
    gate1 = g_w * p1 / den
    gate2 = g_w * p2 / den
    e1 = grp * EXPERTS_PER_GROUP + i1
    e2 = grp * EXPERTS_PER_GROUP + i2
    li = lax.broadcasted_iota(jnp.int32, (LANES, tm), 0)
    gcol_ref[...] = jnp.where(li == 0, gate1, jnp.where(li == 1, gate2, 0.0)).T

    xi = lax.broadcasted_iota(jnp.int32, (N_EXPERTS, tm), 0)
    oh1 = xi == e1
    oh2 = xi == e2
    cnt = oh1.astype(F32) + oh2.astype(F32)
    tile_cnt = jnp.broadcast_to(jnp.sum(cnt, axis=1, keepdims=True), (N_EXPERTS, LANES))
    carry_ref[...] = carry_ref[...] + tile_cnt
    cnt_ref[...] = carry_ref[...]
    tcnt_ref[...] = tile_cnt
    in_tile = jnp.dot(cnt.astype(BF16), tri_ref[...], preferred_element_type=F32)
    tile_start = jnp.dot(elow_ref[...], tile_cnt, precision=lax.Precision.HIGHEST,
                         preferred_element_type=F32)[:, 0:1]
    local = in_tile + tile_start
    pos_ref[0:1, :] = jnp.sum(jnp.where(oh1, local, 0.0), axis=0, keepdims=True).astype(jnp.int32)
    pos_ref[1:2, :] = jnp.sum(jnp.where(oh2, local, 0.0), axis=0, keepdims=True).astype(jnp.int32)


def _post(x, ylru, yatt, gatt, wo_a, wo_b, gcross, wxq, kx, vx, wxo, gffn, wr_t, br_t, tri, tm):
    b, s, d = x.shape
    c = ylru.shape[-1]
    m = kx.shape[1]
    n = b * s
    nt = s // tm
    full = lambda shape: pl.BlockSpec(shape, lambda bi, si: (0,) * len(shape))
    tok = lambda rows: pl.BlockSpec((rows, tm), lambda bi, si: (0, bi * nt + si))
    return pl.pallas_call(
        _post_kernel,
        grid=(b, nt),
        in_specs=[pl.BlockSpec((1, tm, d), lambda bi, si: (bi, si, 0)),
                  pl.BlockSpec((1, tm, c), lambda bi, si: (bi, si, 0)),
                  pl.BlockSpec((1, tm, c), lambda bi, si: (bi, si, 0)),
                  full((1, c)), full((c, d)), full((c, d)), full((1, d)), full((d, d)),
                  pl.BlockSpec((1, m, d), lambda bi, si: (bi, 0, 0)),
                  pl.BlockSpec((1, m, d), lambda bi, si: (bi, 0, 0)),
                  full((d, d)), full((1, d)), full(wr_t.shape), full(br_t.shape), full((tm, tm)),
                  full((N_EXPERTS, N_EXPERTS))],
        out_specs=[pl.BlockSpec((tm, d), lambda bi, si: (bi * nt + si, 0)),
                   pl.BlockSpec((tm, d), lambda bi, si: (bi * nt + si, 0)),
                   pl.BlockSpec((tm, LANES), lambda bi, si: (bi * nt + si, 0)),
                   full((N_EXPERTS, LANES)),
                   tok(EXPERT_TOPK),
                   pl.BlockSpec((N_EXPERTS, LANES), lambda bi, si: (bi * nt + si, 0))],
        out_shape=[jax.ShapeDtypeStruct((n, d), F32),
                   jax.ShapeDtypeStruct((n, d), BF16),
                   jax.ShapeDtypeStruct((n, LANES), F32),
                   jax.ShapeDtypeStruct((N_EXPERTS, LANES), F32),
                   jax.ShapeDtypeStruct((EXPERT_TOPK, n), jnp.int32),
                   jax.ShapeDtypeStruct((n // tm * N_EXPERTS, LANES), F32)],
        scratch_shapes=[pltpu.VMEM((N_EXPERTS, LANES), F32)],
        compiler_params=_params(("arbitrary", "arbitrary")),
        name="post",
    )(x, ylru, yatt, gatt, wo_a, wo_b, gcross, wxq, kx, vx, wxo, gffn, wr_t, br_t, tri,
      jnp.asarray(np.tril(np.ones((N_EXPERTS, N_EXPERTS), np.float32), -1)))


def _pieces(largest):
    piece = largest
    while piece:
        yield piece
        piece //= 2


def _start_run_copies(length, largest, copy_of):
    def start(pieces):
        for piece in pieces:
            done = length & ~(2 * piece - 1)

            @pl.when((length & piece) != 0)
            def _():
                copy_of(done, piece).start()

    @pl.when(length >= RUN_SPLIT)
    def _():
        start([p for p in _pieces(largest) if p >= RUN_SPLIT])

    start([p for p in _pieces(largest) if p < RUN_SPLIT])


def _tile_rows(row, count):
    return pl.ds(pl.multiple_of(row * SUBLANES, SUBLANES), count * SUBLANES)


def _dispatch_kernel(fill_ref, rstart_ref, rlen_ref, hn_ref, pos_ref, xout_ref, sorted_ref, zero_ref, sem, zsem):
    tm = hn_ref.shape[0]
    n_sorted = EXPERT_TOPK * tm
    step = pl.program_id(0)

    zero_rows = zero_ref.shape[0] // SUBLANES
    n_rows = xout_ref.shape[0] // SUBLANES

    def pad_copies(e, start):
        first = fill_ref[e]
        length = fill_ref[N_EXPERTS + e]
        for piece in _pieces(zero_rows):
            offset = first + (length & ~(2 * piece - 1))

            @pl.when((length & piece) != 0)
            def _():
                cp = pltpu.make_async_copy(zero_ref.at[_tile_rows(0, piece)], xout_ref.at[_tile_rows(offset, piece)],
                                           zsem)
                cp.start() if start else cp.wait()

    def tail_copies(start):
        def one(i, carry):
            cp = pltpu.make_async_copy(zero_ref, xout_ref.at[_tile_rows(i * zero_rows, zero_rows)], zsem)
            cp.start() if start else cp.wait()
            return carry

        lax.fori_loop(fill_ref[2 * N_EXPERTS] // zero_rows, n_rows // zero_rows, one, 0)

    @pl.when(step == 0)
    def _():
        zero_ref[...] = jnp.zeros_like(zero_ref)
        lax.fori_loop(0, N_EXPERTS, lambda e, c: (pad_copies(e, True), c)[1], 0)
        tail_copies(True)

    pos = pos_ref[...]
    p_iota = lax.broadcasted_iota(jnp.int32, (n_sorted, tm), 0)
    picks = jnp.where((p_iota == pos[0:1, :]) | (p_iota == pos[1:2, :]), 1.0, 0.0).astype(BF16)
    xs = jnp.dot(picks, hn_ref[...], preferred_element_type=F32)
    for k in range(SUBLANES):
        sorted_ref[pl.ds(k, n_sorted, stride=SUBLANES), :] = xs[:, k * LANES:(k + 1) * LANES]

    def run(e, offset):
        length = rlen_ref[step * N_EXPERTS + e]
        first = rstart_ref[step * N_EXPERTS + e]

        _start_run_copies(length, tm, lambda done, piece: pltpu.make_async_copy(
            sorted_ref.at[_tile_rows(offset + done, piece)], xout_ref.at[_tile_rows(first + done, piece)], sem))
        return offset + length

    lax.fori_loop(0, N_EXPERTS, run, 0)

    pltpu.make_async_copy(sorted_ref, xout_ref.at[_tile_rows(0, n_sorted)], sem).wait()

    @pl.when(step == 0)
    def _():
        lax.fori_loop(0, N_EXPERTS, lambda e, c: (pad_copies(e, False), c)[1], 0)
        tail_copies(False)


def _dispatch(fill, run_start, run_len, hn2, pos, n_rows, tm):
    n, d = hn2.shape
    assert d == SUBLANES * LANES, "a row must be exactly one (8, 128) tile for the tile-contiguous layout"
    return pl.pallas_call(
        _dispatch_kernel,
        grid_spec=pltpu.PrefetchScalarGridSpec(
            num_scalar_prefetch=3,
            grid=(n // tm,),
            in_specs=[pl.BlockSpec((tm, d), lambda i, *_: (i, 0)),
                      pl.BlockSpec((EXPERT_TOPK, tm), lambda i, *_: (0, i))],
            out_specs=pl.BlockSpec(memory_space=pl.ANY),
            scratch_shapes=[pltpu.VMEM((EXPERT_TOPK * tm * SUBLANES, LANES), F32),
                            pltpu.VMEM((EXPERT_ROWS // 2 * SUBLANES, LANES), F32),
                            pltpu.SemaphoreType.DMA(()), pltpu.SemaphoreType.DMA(())]),
        out_shape=jax.ShapeDtypeStruct((n_rows * SUBLANES, LANES), F32),
        compiler_params=_params(("arbitrary",)),
        name="dispatch",
    )(fill, run_start, run_len, hn2, pos)


def _experts_kernel(be_ref, nu_ref, x_ref, wgu_hbm, wd_hbm, y_ref, wgu_f32_ref, wd_f32_ref, wgu_bf_ref, wd_bf_ref,
                    sems):
    j = pl.program_id(0)
    rows = x_ref.shape[0] // SUBLANES
    f = wd_bf_ref.shape[0]
    n_used = nu_ref[0]
    used = j < n_used

    def fetch(e):
        return (pltpu.make_async_copy(wgu_hbm.at[e], wgu_f32_ref, sems.at[0]),
                pltpu.make_async_copy(wd_hbm.at[e], wd_f32_ref, sems.at[1]))

    @pl.when(used & (j == 0))
    def _():
        for cp in fetch(be_ref[0]):
            cp.start()

    @pl.when(used & ((j == 0) | (be_ref[j] != be_ref[jnp.maximum(j, 1) - 1])))
    def _():
        for cp in fetch(be_ref[j]):
            cp.wait()
        wgu_bf_ref[...] = wgu_f32_ref[...].astype(BF16)
        wd_bf_ref[...] = wd_f32_ref[...].astype(BF16)
        nxt = lax.while_loop(lambda k: (k < n_used) & (be_ref[jnp.minimum(k, be_ref.shape[0] - 1)] == be_ref[j]),
                             lambda k: k + 1, j + 1)

        @pl.when(nxt < n_used)
        def _():
            for cp in fetch(be_ref[jnp.minimum(nxt, be_ref.shape[0] - 1)]):
                cp.start()

    @pl.when(used)
    def _():
        x = jnp.concatenate([x_ref[pl.ds(k, rows, stride=SUBLANES), :] for k in range(SUBLANES)], axis=1)
        gu = jnp.dot(x.astype(BF16), wgu_bf_ref[...], preferred_element_type=F32)
        act = jax.nn.silu(gu[:, 0:f]) * gu[:, f:2 * f]
        y = jnp.dot(act.astype(BF16), wd_bf_ref[...], preferred_element_type=F32)
        for k in range(SUBLANES):
            y_ref[pl.ds(k, rows, stride=SUBLANES), :] = y[:, k * LANES:(k + 1) * LANES]

    @pl.when(jnp.logical_not(used))
    def _():
        y_ref[...] = jnp.zeros_like(y_ref)


def _experts(blk_expert, n_used, x_buf, w_gate_up, w_down):
    d = w_gate_up.shape[1]
    f = w_down.shape[1]
    r = EXPERT_ROWS
    n_blk = x_buf.shape[0] // (r * SUBLANES)
    assert d == SUBLANES * LANES and x_buf.shape[1] == LANES, "rows are (8, 128) tiles (tile-contiguous layout)"
    return pl.pallas_call(
        _experts_kernel,
        grid_spec=pltpu.PrefetchScalarGridSpec(
            num_scalar_prefetch=2,
            grid=(n_blk,),
            in_specs=[pl.BlockSpec((r * SUBLANES, LANES), lambda j, be, nu: (jnp.minimum(j, jnp.maximum(nu[0], 1) - 1), 0)),
                      pl.BlockSpec(memory_space=pl.ANY), pl.BlockSpec(memory_space=pl.ANY)],
            out_specs=pl.BlockSpec((r * SUBLANES, LANES), lambda j, be, nu: (j, 0)),
            scratch_shapes=[pltpu.VMEM((d, 2 * f), F32), pltpu.VMEM((f, d), F32),
                            pltpu.VMEM((d, 2 * f), BF16), pltpu.VMEM((f, d), BF16),
                            pltpu.SemaphoreType.DMA((2,))]),
        out_shape=jax.ShapeDtypeStruct(x_buf.shape, F32),
        compiler_params=_params(("arbitrary",)),
        name="experts",
    )(blk_expert, n_used, x_buf, w_gate_up, w_down)


def _combine_kernel(pos_ref, rstart_ref, rlen_ref, h2_ref, gcol_ref, gfin_ref, y_ref, o_ref,
                    ysort_ref, pick0_ref, pick1_ref, sems):
    tc = h2_ref.shape[0]
    n = pos_ref.shape[0] // EXPERT_TOPK
    n_sorted = EXPERT_TOPK * tc
    step = pl.program_id(0)
    par = step % 2
    base = step * tc

    def start_gather(tile, buf):
        def run(e, offset):
            length = rlen_ref[tile * N_EXPERTS + e]
            first = rstart_ref[tile * N_EXPERTS + e]
            _start_run_copies(length, tc, lambda done, piece: pltpu.make_async_copy(
                y_ref.at[_tile_rows(first + done, piece)], ysort_ref.at[buf, _tile_rows(offset + done, piece)],
                sems.at[buf]))
            return offset + length

        lax.fori_loop(0, N_EXPERTS, run, 0)

    @pl.when(step == 0)
    def _():
        start_gather(0, 0)

    @pl.when(step + 1 < pl.num_programs(0))
    def _():
        start_gather(step + 1, 1 - par)

    pltpu.make_async_copy(y_ref.at[_tile_rows(0, n_sorted)], ysort_ref.at[par], sems.at[par]).wait()
    for t in range(tc):
        for slot, pick_ref in enumerate((pick0_ref, pick1_ref)):
            row = pos_ref[slot * n + base + t]
            pick_ref[t * SUBLANES:(t + 1) * SUBLANES, :] = ysort_ref[par, _tile_rows(row, 1), :]

    def rows_of(y_ref):
        return jnp.concatenate([y_ref[pl.ds(k, tc, stride=SUBLANES), :] for k in range(SUBLANES)], axis=1)

    g = gcol_ref[...]
    moe = g[:, 0:1] * rows_of(pick0_ref) + g[:, 1:2] * rows_of(pick1_ref)
    o_ref[...] = _rms(h2_ref[...] + moe, gfin_ref[...])


def _combine(pos_flat, run_start, run_len, h2, gcol, gfin, y_buf, tc):
    n, d = h2.shape
    return pl.pallas_call(
        _combine_kernel,
        grid_spec=pltpu.PrefetchScalarGridSpec(
            num_scalar_prefetch=3,
            grid=(n // tc,),
            in_specs=[pl.BlockSpec((tc, d), lambda i, *_: (i, 0)),
                      pl.BlockSpec((tc, LANES), lambda i, *_: (i, 0)),
                      pl.BlockSpec((1, d), lambda i, *_: (0, 0)),
                      pl.BlockSpec(memory_space=pl.ANY)],
            out_specs=pl.BlockSpec((tc, d), lambda i, *_: (i, 0)),
            scratch_shapes=[pltpu.VMEM((2, EXPERT_TOPK * tc * SUBLANES, LANES), F32),
                            pltpu.VMEM((tc * SUBLANES, LANES), F32), pltpu.VMEM((tc * SUBLANES, LANES), F32),
                            pltpu.SemaphoreType.DMA((2,))]),
        out_shape=jax.ShapeDtypeStruct((n, d), F32),
        compiler_params=_params(("arbitrary",)),
        name="combine",
    )(pos_flat, run_start, run_len, h2, gcol, gfin, y_buf)


def _rope_tables(s):
    half = ROT_DIM // 2
    inv_freq = (ROPE_THETA ** (-np.arange(0, ROT_DIM, 2, dtype=np.float32) / ROT_DIM)).astype(np.float32)
    ang = np.arange(s, dtype=np.float32)[:, None] * inv_freq[None, :]
    cos, sin = np.cos(ang), np.sin(ang)
    zeros = lambda w: np.zeros((s, w), np.float32)
    cos_h = np.concatenate([cos, cos, np.ones((s, HEAD_DIM - ROT_DIM), np.float32)], axis=1)
    sina_h = np.concatenate([-sin, zeros(HEAD_DIM - half)], axis=1)
    sinb_h = np.concatenate([zeros(half), sin, zeros(HEAD_DIM - ROT_DIM)], axis=1)
    tile = lambda t: jnp.asarray(np.tile(t, (1, LANES // HEAD_DIM)), F32)
    return tile(cos_h), tile(sina_h), tile(sinb_h)


def _block_diag(w):
    nblk, bi, bo = w.shape
    eye = jnp.eye(nblk, dtype=w.dtype)
    return jnp.einsum('hij,hg->higj', w, eye).reshape(nblk * bi, nblk * bo)


def _layer(h, mem, norm_mix, w_in, conv_w, conv_b, w_rg, b_rg, w_ig, b_ig, lru_lambda, norm_lru_out,
           norm_attn_out, w_out, norm_cross, norm_mem, w_xq, w_xkv, w_xo, norm_ffn, w_router_group,
           b_router_group, w_router_expert, b_router_expert, w_gate_up, w_down, norm_out):
    b, s, d = h.shape
    c = conv_w.shape[-1]
    n = b * s
    row = lambda t: t.reshape(1, -1)
    tm = 512 if s % 512 == 0 else MOBA_BLOCK
    ts = tm
    assert s % ts == 0 and s % tm == 0 and c == N_HEADS * HEAD_DIM

    kx, vx = _memkv(mem, row(norm_mem), w_xkv.astype(BF16))

    w_gates = jnp.concatenate([_block_diag(w_rg), _block_diag(w_ig)], axis=1).astype(BF16)
    b_gates = jnp.concatenate([b_rg, b_ig]).reshape(1, -1)
    cos_t, sina_t, sinb_t = _rope_tables(s)
    ylru, q, k, v = _mixer_in(h, row(norm_mix), w_in.astype(BF16), conv_w, row(conv_b), w_gates, b_gates,
                              row(lru_lambda), row(norm_lru_out), cos_t, sina_t, sinb_t, ts)
    yatt = _moba(q, k, v)

    w_out_b = w_out.astype(BF16)
    wr_t = jnp.zeros((ROUTER_ROWS, d), F32)
    wr_t = wr_t.at[0:N_GROUPS].set(w_router_group.T).at[SUBLANES:SUBLANES + N_EXPERTS].set(w_router_expert.T)
    wr_hi = wr_t.astype(BF16)
    wr_t = jnp.concatenate([wr_hi, (wr_t - wr_hi.astype(F32)).astype(BF16)], axis=0)
    br_t = jnp.zeros((ROUTER_ROWS, 1), F32)
    br_t = br_t.at[0:N_GROUPS, 0].set(b_router_group).at[SUBLANES:SUBLANES + N_EXPERTS, 0].set(b_router_expert)
    tri = jnp.asarray(np.triu(np.ones((tm, tm), np.float32), 1), BF16)
    h2, hn2, gcol, counts, pos, tile_cnt = _post(
        h, ylru, yatt, row(norm_attn_out), w_out_b[:c], w_out_b[c:], row(norm_cross), w_xq.astype(BF16),
        kx, vx, w_xo.astype(BF16), row(norm_ffn), wr_t, br_t, tri, tm)

    counts = counts[:, 0].astype(jnp.int32)
    padded = (counts + EXPERT_ROWS - 1) // EXPERT_ROWS * EXPERT_ROWS
    pad_ends = jnp.cumsum(padded)
    pad_starts = pad_ends - padded
    n_blocks = n * EXPERT_TOPK // EXPERT_ROWS + N_EXPERTS
    blk_first = jnp.arange(n_blocks, dtype=jnp.int32) * EXPERT_ROWS
    blk_expert = jnp.minimum(jnp.sum(blk_first[:, None] >= pad_ends[None, :], axis=1), N_EXPERTS - 1).astype(jnp.int32)
    n_used = (pad_ends[-1:] // EXPERT_ROWS).astype(jnp.int32)
    fill = jnp.concatenate([pad_starts + counts, padded - counts, pad_ends[-1:]]).astype(jnp.int32)

    n_tiles = n // tm
    tile_cnt = tile_cnt[:, 0].astype(jnp.int32).reshape(n_tiles, N_EXPERTS)
    run_start = (pad_starts[None, :] + jnp.cumsum(tile_cnt, axis=0) - tile_cnt).astype(jnp.int32).reshape(-1)
    run_len = tile_cnt.reshape(-1)
    x_buf = _dispatch(fill, run_start, run_len, hn2, pos, n_blocks * EXPERT_ROWS, tm)
    y_buf = _experts(blk_expert, n_used, x_buf, w_gate_up, w_down)
    out = _combine(pos.reshape(-1), run_start, run_len, h2, gcol, row(norm_out), y_buf, tm)
    return out.reshape(b, s, d)


def kernel(x, mem, norm_mix, w_in, conv_w, conv_b, w_rg, b_rg, w_ig, b_ig, lru_lambda, norm_lru_out, norm_attn_out,
           w_out, norm_cross, norm_mem, w_xq, w_xkv, w_xo, norm_ffn, w_router_group, b_router_group,
           w_router_expert, b_router_expert, w_gate_up, w_down, norm_final):
    depth = norm_mix.shape[0]
    assert depth == 1, "the fused final norm assumes a single layer"
    l = 0
    return _layer(x, mem, norm_mix[l], w_in[l], conv_w[l], conv_b[l], w_rg[l], b_rg[l], w_ig[l], b_ig[l],
                  lru_lambda[l], norm_lru_out[l], norm_attn_out[l], w_out[l], norm_cross[l], norm_mem[l],
                  w_xq[l], w_xkv[l], w_xo[l], norm_ffn[l], w_router_group[l], b_router_group[l],
                  w_router_expert[l], b_router_expert[l], w_gate_up[l], w_down[l], norm_final)
```

```python
import jax
import jax.numpy as jnp
import numpy as np
from jax import lax
from jax.experimental import pallas as pl
from jax.experimental.pallas import tpu as pltpu

F32 = jnp.float32
BF16 = jnp.bfloat16

CONV_WIDTH = 4
RG_C = 8.0
N_HEADS = 8
HEAD_DIM = 64
ROT_DIM = HEAD_DIM // 4
ROPE_THETA = 500000.0
MOBA_BLOCK = 256
MOBA_TOPK = 3
N_XHEADS = 4
N_GROUPS = 4
EXPERTS_PER_GROUP = 8
N_EXPERTS = N_GROUPS * EXPERTS_PER_GROUP
EXPERT_TOPK = 2
EPS = 1e-6

LANES = 128
SUBLANES = 8
NEG_BIG = -1e30
LOG2_E = 1.4426950408889634
MOBA_QROWS = 128
MOBA_LOOKAHEAD = 5
EXPERT_ROWS = 512
EXPERT_BLOCKS_PER_STEP = 2
RUN_SPLIT = 64
BF16_ROWS = 2 * SUBLANES
ROUTER_ROWS = -(-(SUBLANES + N_EXPERTS) // BF16_ROWS) * BF16_ROWS
VMEM_LIMIT = 56 * 1024 * 1024

NT_DIMS = (((1,), (1,)), ((), ()))


def _rms(x, g):
    return x * lax.rsqrt(jnp.mean(x * x, axis=-1, keepdims=True) + EPS) * g


def _params(sem):
    return pltpu.CompilerParams(dimension_semantics=sem, vmem_limit_bytes=VMEM_LIMIT)


def _memkv_kernel(mem_ref, g_ref, w_ref, k_ref, v_ref):
    d = mem_ref.shape[-1]
    mn = _rms(mem_ref[0], g_ref[...]).astype(BF16)
    kv = jnp.dot(mn, w_ref[...], preferred_element_type=F32)
    k_ref[0] = kv[:, :d].astype(BF16)
    v_ref[0] = kv[:, d:].astype(BF16)


def _memkv(mem, g, w_xkv):
    b, m, d = mem.shape
    return pl.pallas_call(
        _memkv_kernel,
        grid=(b,),
        in_specs=[pl.BlockSpec((1, m, d), lambda i: (i, 0, 0)),
                  pl.BlockSpec((1, d), lambda i: (0, 0)),
                  pl.BlockSpec((d, 2 * d), lambda i: (0, 0))],
        out_specs=[pl.BlockSpec((1, m, d), lambda i: (i, 0, 0)),
                   pl.BlockSpec((1, m, d), lambda i: (i, 0, 0))],
        out_shape=[jax.ShapeDtypeStruct((b, m, d), BF16)] * 2,
        compiler_params=_params(("arbitrary",)),
        name="memkv",
    )(mem, g, w_xkv)


def _mixer_in_kernel(x_ref, gmix_ref, win_ref, convw_ref, convb_ref, wg_ref, bg_ref, lam_ref, glru_ref,
                     cos_ref, sina_ref, sinb_ref,
                     ylru_ref, q_ref, k_ref, v_ref, xpad_ref, hcar_ref, ascan_ref, uscan_ref):
    c = ylru_ref.shape[-1]
    si = pl.program_id(1)

    @pl.when(si == 0)
    def _():
        xpad_ref[0:SUBLANES, :] = jnp.zeros((SUBLANES, c), F32)
        hcar_ref[...] = jnp.zeros_like(hcar_ref)

    ts = x_ref.shape[1]
    xn = _rms(x_ref[0], gmix_ref[...]).astype(BF16)
    proj = jnp.dot(xn, win_ref[...], preferred_element_type=F32)
    x_lru = proj[:, 0:c]
    g_lru = proj[:, c:2 * c]
    q = proj[:, 2 * c:3 * c]
    k = proj[:, 3 * c:4 * c]
    v = proj[:, 4 * c:5 * c]

    xpad_ref[SUBLANES:SUBLANES + ts, :] = x_lru
    cw = convw_ref[...]
    xc = convb_ref[...] + cw[3:4] * x_lru
    for j in range(1, CONV_WIDTH):
        xc = xc + cw[3 - j:4 - j] * xpad_ref[SUBLANES - j:SUBLANES - j + ts, :]
    xpad_ref[0:SUBLANES, :] = x_lru[ts - SUBLANES:ts, :]

    gates = jnp.dot(xc.astype(BF16), wg_ref[...], preferred_element_type=F32) + bg_ref[...]
    r = jax.nn.sigmoid(gates[:, 0:c])
    i = jax.nn.sigmoid(gates[:, c:2 * c])
    neg_lam = -lam_ref[...]
    softplus = jnp.maximum(neg_lam, 0.0) + jnp.log1p(jnp.exp(-jnp.abs(neg_lam)))
    log_a = -RG_C * r * softplus
    a = jnp.exp(log_a)
    u = jnp.sqrt(-jnp.tanh(log_a) * (a * a + 1.0)) * (i * xc)

    groups = ts // SUBLANES
    cols = c // LANES
    for j in range(cols):
        ascan_ref[j] = a[:, j * LANES:(j + 1) * LANES]
        uscan_ref[j] = u[:, j * LANES:(j + 1) * LANES]
    for r in range(SUBLANES):
        rows = pl.ds(r, groups, stride=SUBLANES)
        a_r = jnp.concatenate([ascan_ref[j, rows, :] for j in range(cols)], axis=1)
        u_r = jnp.concatenate([uscan_ref[j, rows, :] for j in range(cols)], axis=1)
        if r:
            u_r = a_r * u_acc + u_r
            a_r = a_r * a_acc
            for j in range(cols):
                ascan_ref[j, rows, :] = a_r[:, j * LANES:(j + 1) * LANES]
                uscan_ref[j, rows, :] = u_r[:, j * LANES:(j + 1) * LANES]
        a_acc, u_acc = a_r, u_r
    carry = hcar_ref[...]
    hs = []
    for gi in range(groups):
        blk = slice(gi * SUBLANES, (gi + 1) * SUBLANES)
        a_g = jnp.concatenate([ascan_ref[j, blk, :] for j in range(cols)], axis=1)
        u_g = jnp.concatenate([uscan_ref[j, blk, :] for j in range(cols)], axis=1)
        hs.append(a_g * carry + u_g)
        carry = a_acc[gi:gi + 1] * carry + u_acc[gi:gi + 1]
    h = jnp.concatenate(hs, axis=0)
    hcar_ref[...] = carry

    y = h * jax.nn.gelu(g_lru)
    ylru_ref[0] = _rms(y, glru_ref[...]).astype(BF16)

    wide = lambda t_ref: jnp.concatenate([t_ref[...]] * (c // LANES), axis=1)
    cos, sina, sinb = wide(cos_ref), wide(sina_ref), wide(sinb_ref)
    half = ROT_DIM // 2

    def rope(t):
        return t * cos + pltpu.roll(t, c - half, axis=1) * sina + pltpu.roll(t, half, axis=1) * sinb

    q_ref[0] = rope(q)
    k_ref[0] = rope(k)
    v_ref[0] = v.astype(BF16)


def _mixer_in(x, gmix, w_in, conv_w, conv_b, w_gates, b_gates, lam, glru, cos_t, sina_t, sinb_t, ts):
    b, s, d = x.shape
    c = conv_w.shape[-1]
    full = lambda shape: pl.BlockSpec(shape, lambda bi, si: (0,) * len(shape))
    tab = pl.BlockSpec((ts, LANES), lambda bi, si: (si, 0))
    seq = lambda: pl.BlockSpec((1, ts, c), lambda bi, si: (bi, si, 0))
    return pl.pallas_call(
        _mixer_in_kernel,
        grid=(b, s // ts),
        in_specs=[pl.BlockSpec((1, ts, d), lambda bi, si: (bi, si, 0)),
                  full((1, d)), full(w_in.shape), full(conv_w.shape), full((1, c)),
                  full(w_gates.shape), full((1, 2 * c)), full((1, c)), full((1, c)),
                  tab, tab, tab],
        out_specs=[seq(), seq(), seq(), seq()],
        out_shape=[jax.ShapeDtypeStruct((b, s, c), BF16),
                   jax.ShapeDtypeStruct((b, s, c), F32),
                   jax.ShapeDtypeStruct((b, s, c), F32),
                   jax.ShapeDtypeStruct((b, s, c), BF16)],
        scratch_shapes=[pltpu.VMEM((ts + SUBLANES, c), F32), pltpu.VMEM((1, c), F32),
                        pltpu.VMEM((c // LANES, ts, LANES), F32), pltpu.VMEM((c // LANES, ts, LANES), F32)],
        compiler_params=_params(("arbitrary", "arbitrary")),
        name="mixer_in",
    )(x, gmix, w_in, conv_w, conv_b, w_gates, b_gates, lam, glru, cos_t, sina_t, sinb_t)


def _moba_kernel(q_ref, k_ref, v_ref, kpat_ref, o_ref, qa_ref, ka_ref):
    s_len = q_ref.shape[1]
    nb = s_len // MOBA_BLOCK
    n_pick = min(MOBA_TOPK, nb)
    q = q_ref[0]
    k = k_ref[0]
    lane = lax.broadcasted_iota(jnp.int32, (1, LANES), 1)
    kmean = jnp.concatenate(
        [jnp.sum(k[n * MOBA_BLOCK:(n + 1) * MOBA_BLOCK], axis=0, keepdims=True) for n in range(nb)],
        axis=0) * (1.0 / MOBA_BLOCK)

    blk_row = lax.broadcasted_iota(jnp.int32, (nb, s_len), 0)
    q_blk = lax.broadcasted_iota(jnp.int32, (nb, s_len), 1) // MOBA_BLOCK
    past = blk_row < q_blk

    q_hi = q.astype(BF16)
    q_lo = (q - q_hi.astype(F32)).astype(BF16)
    for h in range(2):
        own = (lane >= h * HEAD_DIM) & (lane < (h + 1) * HEAD_DIM)
        off = (1 - h) * HEAD_DIM
        km = jnp.where(own, kmean, 0.0)
        km_hi = km.astype(BF16)
        km_split = jnp.concatenate([km_hi, (km - km_hi.astype(F32)).astype(BF16)], axis=0)
        parts = (lax.dot_general(km_split, q_hi, NT_DIMS, preferred_element_type=F32)
                 + lax.dot_general(km_split, q_lo, NT_DIMS, preferred_element_type=F32))
        gate = jnp.where(past, parts[0:nb] + parts[nb:2 * nb], -jnp.inf)
        rank = jnp.zeros((nb, s_len), jnp.int32)
        for m in range(nb):
            gm = gate[m:m + 1, :]
            ahead = (gm > gate) | ((gm == gate) & (m < blk_row))
            rank = rank + ahead.astype(jnp.int32)
        allowed = ((rank < n_pick) & past) | (blk_row == q_blk)
        bias = jnp.where(allowed, 0.0, NEG_BIG)
        pieces = []
        if off:
            pieces.append(jnp.zeros((off, s_len), F32))
        pieces.append(bias)
        if LANES - off - nb:
            pieces.append(jnp.zeros((LANES - off - nb, s_len), F32))
        bias_lanes = jnp.concatenate(pieces, axis=0).T
        qa_ref[h] = jnp.where(own, q * (HEAD_DIM ** -0.5 * LOG2_E), bias_lanes).astype(BF16)
        ka_ref[h] = jnp.where(own, k.astype(BF16), kpat_ref[h])

    own0 = lane < HEAD_DIM

    parts = MOBA_BLOCK // MOBA_QROWS

    def q_rows(qi, part):
        return slice(qi * MOBA_BLOCK + part * MOBA_QROWS, qi * MOBA_BLOCK + (part + 1) * MOBA_QROWS)

    def n_keys(qi, part):
        return qi * MOBA_BLOCK + (part + 1) * MOBA_QROWS

    def scores(qi, part, h):
        return lax.dot_general(qa_ref[h, q_rows(qi, part), :], ka_ref[h, 0:n_keys(qi, part), :], NT_DIMS,
                               preferred_element_type=F32)

    units = [(qi, part, h) for qi in range(nb) for part in range(parts) for h in range(2)]
    ahead = [scores(*u) for u in units[:MOBA_LOOKAHEAD]]
    outs = []

    def weighted_values(p, l, qi, part, h):
        outs.append(jnp.dot(p, v_ref[0, 0:n_keys(qi, part), :], preferred_element_type=F32) / l)
        if h:
            o_ref[0, q_rows(qi, part), :] = jnp.where(own0, outs[0], outs[1])
            outs.clear()

    pending = None
    for n, (qi, part, h) in enumerate(units):
        s = ahead.pop(0)
        if n + MOBA_LOOKAHEAD < len(units):
            ahead.append(scores(*units[n + MOBA_LOOKAHEAD]))
        n_past = qi * MOBA_BLOCK
        own_keys = (part + 1) * MOBA_QROWS
        key_in_blk = lax.broadcasted_iota(jnp.int32, (MOBA_QROWS, own_keys), 1)
        q_in_blk = lax.broadcasted_iota(jnp.int32, (MOBA_QROWS, own_keys), 0) + part * MOBA_QROWS
        s_own = jnp.where(key_in_blk <= q_in_blk, s[:, n_past:], NEG_BIG)
        s = jnp.concatenate([s[:, :n_past], s_own], axis=1) if qi else s_own
        p = jnp.exp2(s - jnp.max(s, axis=-1, keepdims=True))
        l = jnp.sum(p, axis=-1, keepdims=True)
        if pending is not None:
            weighted_values(*pending)
        pending = (p.astype(BF16), l, qi, part, h)
    weighted_values(*pending)


def _key_block_pattern(s):
    pat = np.zeros((2, s, LANES), np.float32)
    for h in range(2):
        off = (1 - h) * HEAD_DIM
        pat[h, np.arange(s), off + np.arange(s) // MOBA_BLOCK] = 1.0
    return jnp.asarray(pat, BF16)


def _moba(q, k, v):
    b, s, c = q.shape
    spec = lambda: pl.BlockSpec((1, s, LANES), lambda bi, hi: (bi, 0, hi))
    return pl.pallas_call(
        _moba_kernel,
        grid=(b, c // LANES),
        in_specs=[spec(), spec(), spec(), pl.BlockSpec((2, s, LANES), lambda bi, hi: (0, 0, 0))],
        out_specs=spec(),
        out_shape=jax.ShapeDtypeStruct((b, s, c), F32),
        scratch_shapes=[pltpu.VMEM((2, s, LANES), BF16), pltpu.VMEM((2, s, LANES), BF16)],
        compiler_params=_params(("arbitrary", "arbitrary")),
        name="moba",
    )(q, k, v, _key_block_pattern(s))


def _post_kernel(x_ref, ylru_ref, yatt_ref, gatt_ref, woa_ref, wob_ref, gcross_ref, wxq_ref, kx_ref, vx_ref,
                 wxo_ref, gffn_ref, wr_ref, br_ref, tri_ref, elow_ref,
                 h2_ref, hn2_ref, gcol_ref, cnt_ref, pos_ref, tcnt_ref, carry_ref):
    tm = x_ref.shape[1]
    d = x_ref.shape[2]
    xd = d // N_XHEADS
    first = (pl.program_id(0) == 0) & (pl.program_id(1) == 0)

    @pl.when(first)
    def _():
        carry_ref[...] = jnp.zeros_like(carry_ref)

    ya = _rms(yatt_ref[0], gatt_ref[...]).astype(BF16)
    mix = (jnp.dot(ylru_ref[0], woa_ref[...], preferred_element_type=F32)
           + jnp.dot(ya, wob_ref[...], preferred_element_type=F32))
    h1 = x_ref[0] + mix

    hn = _rms(h1, gcross_ref[...]).astype(BF16)
    qx = jnp.dot(hn, wxq_ref[...], preferred_element_type=F32)
    heads = []
    for hh in range(N_XHEADS):
        qh = (qx[:, hh * xd:(hh + 1) * xd] * (xd ** -0.5)).astype(BF16)
        s = lax.dot_general(qh, kx_ref[0, :, hh * xd:(hh + 1) * xd], NT_DIMS, preferred_element_type=F32)
        p = jnp.exp(s - jnp.max(s, axis=-1, keepdims=True))
        l = jnp.sum(p, axis=-1, keepdims=True)
        o = jnp.dot(p.astype(BF16), vx_ref[0, :, hh * xd:(hh + 1) * xd], preferred_element_type=F32) / l
        heads.append(o.astype(BF16))
    h2 = h1 + jnp.dot(jnp.concatenate(heads, axis=1), wxo_ref[...], preferred_element_type=F32)
    hn2 = _rms(h2, gffn_ref[...])
    h2_ref[...] = h2
    hn2_ref[...] = hn2.astype(BF16)

    x_hi = hn2.astype(BF16)
    x_lo = (hn2 - x_hi.astype(F32)).astype(BF16)
    by_hi = lax.dot_general(wr_ref[...], x_hi, NT_DIMS, preferred_element_type=F32)
    by_lo = lax.dot_general(wr_ref[0:ROUTER_ROWS], x_lo, NT_DIMS, preferred_element_type=F32)
    logits = by_hi[0:ROUTER_ROWS] + by_hi[ROUTER_ROWS:2 * ROUTER_ROWS] + by_lo + br_ref[...]
    gl = logits[0:N_GROUPS]
    gmax = jnp.max(gl, axis=0, keepdims=True)
    gi = lax.broadcasted_iota(jnp.int32, gl.shape, 0)
    grp = jnp.min(jnp.where(gl == gmax, gi, N_GROUPS), axis=0, keepdims=True)
    g_w = 1.0 / jnp.sum(jnp.exp(gl - gmax), axis=0, keepdims=True)
    el = jnp.zeros((EXPERTS_PER_GROUP, tm), F32)
    for g in range(N_GROUPS):
        lo = SUBLANES + g * EXPERTS_PER_GROUP
        el = jnp.where(grp == g, logits[lo:lo + EXPERTS_PER_GROUP], el)
    ee = jnp.exp(el - jnp.max(el, axis=0, keepdims=True))
    ep = ee / jnp.sum(ee, axis=0, keepdims=True)
    ei = lax.broadcasted_iota(jnp.int32, ep.shape, 0)
    p1 = jnp.max(ep, axis=0, keepdims=True)
    i1 = jnp.min(jnp.where(ep == p1, ei, EXPERTS_PER_GROUP), axis=0, keepdims=True)
    ep_rest = jnp.where(ei == i1, -1.0, ep)
    p2 = jnp.max(ep_rest, axis=0, keepdims=True)
    i2 = jnp.min(jnp.where(ep_rest == p2, ei, EXPERTS_PER_GROUP), axis=0, keepdims=True)
    den = p1 + p2
    gate1 = g_w * p1 / den
    gate2 = g_w * p2 / den
    e1 = grp * EXPERTS_PER_GROUP + i1
    e2 = grp * EXPERTS_PER_GROUP + i2
    li = lax.broadcasted_iota(jnp.int32, (LANES, tm), 0)
    gcol_ref[...] = jnp.where(li == 0, gate1, jnp.where(li == 1, gate2, 0.0)).T

    xi = lax.broadcasted_iota(jnp.int32, (N_EXPERTS, tm), 0)
    oh1 = xi == e1
    oh2 = xi == e2
    cnt = oh1.astype(F32) + oh2.astype(F32)
    tile_cnt = jnp.broadcast_to(jnp.sum(cnt, axis=1, keepdims=True), (N_EXPERTS, LANES))
    carry_ref[...] = carry_ref[...] + tile_cnt
    cnt_ref[...] = carry_ref[...]
    tcnt_ref[...] = tile_cnt
    in_tile = jnp.dot(cnt.astype(BF16), tri_ref[...], preferred_element_type=F32)
    tile_start = jnp.dot(elow_ref[...], tile_cnt, precision=lax.Precision.HIGHEST,
                         preferred_element_type=F32)[:, 0:1]
    local = in_tile + tile_start
    pos_ref[0:1, :] = jnp.sum(jnp.where(oh1, local, 0.0), axis=0, keepdims=True).astype(jnp.int32)
    pos_ref[1:2, :] = jnp.sum(jnp.where(oh2, local, 0.0), axis=0, keepdims=True).astype(jnp.int32)


def _post(x, ylru, yatt, gatt, wo_a, wo_b, gcross, wxq, kx, vx, wxo, gffn, wr_t, br_t, tri, tm):
    b, s, d = x.shape
    c = ylru.shape[-1]
    m = kx.shape[1]
    n = b * s
    nt = s // tm
    full = lambda shape: pl.BlockSpec(shape, lambda bi, si: (0,) * len(shape))
    tok = lambda rows: pl.BlockSpec((rows, tm), lambda bi, si: (0, bi * nt + si))
    return pl.pallas_call(
        _post_kernel,
        grid=(b, nt),
        in_specs=[pl.BlockSpec((1, tm, d), lambda bi, si: (bi, si, 0)),
                  pl.BlockSpec((1, tm, c), lambda bi, si: (bi, si, 0)),
                  pl.BlockSpec((1, tm, c), lambda bi, si: (bi, si, 0)),
                  full((1, c)), full((c, d)), full((c, d)), full((1, d)), full((d, d)),
                  pl.BlockSpec((1, m, d), lambda bi, si: (bi, 0, 0)),
                  pl.BlockSpec((1, m, d), lambda bi, si: (bi, 0, 0)),
                  full((d, d)), full((1, d)), full(wr_t.shape), full(br_t.shape), full((tm, tm)),
                  full((N_EXPERTS, N_EXPERTS))],
        out_specs=[pl.BlockSpec((tm, d), lambda bi, si: (bi * nt + si, 0)),
                   pl.BlockSpec((tm, d), lambda bi, si: (bi * nt + si, 0)),
                   pl.BlockSpec((tm, LANES), lambda bi, si: (bi * nt + si, 0)),
                   full((N_EXPERTS, LANES)),
                   tok(EXPERT_TOPK),
                   pl.BlockSpec((N_EXPERTS, LANES), lambda bi, si: (bi * nt + si, 0))],
        out_shape=[jax.ShapeDtypeStruct((n, d), F32),
                   jax.ShapeDtypeStruct((n, d), BF16),
                   jax.ShapeDtypeStruct((n, LANES), F32),
                   jax.ShapeDtypeStruct((N_EXPERTS, LANES), F32),
                   jax.ShapeDtypeStruct((EXPERT_TOPK, n), jnp.int32),
                   jax.ShapeDtypeStruct((n // tm * N_EXPERTS, LANES), F32)],
        scratch_shapes=[pltpu.VMEM((N_EXPERTS, LANES), F32)],
        compiler_params=_params(("arbitrary", "arbitrary")),
        name="post",
    )(x, ylru, yatt, gatt, wo_a, wo_b, gcross, wxq, kx, vx, wxo, gffn, wr_t, br_t, tri,
      jnp.asarray(np.tril(np.ones((N_EXPERTS, N_EXPERTS), np.float32), -1)))


def _pieces(largest):
    piece = largest
    while piece:
        yield piece
        piece //= 2


def _start_run_copies(length, largest, copy_of):
    def start(pieces):
        for piece in pieces:
            done = length & ~(2 * piece - 1)

            @pl.when((length & piece) != 0)
            def _():
                copy_of(done, piece).start()

    @pl.when(length >= RUN_SPLIT)
    def _():
        start([p for p in _pieces(largest) if p >= RUN_SPLIT])

    start([p for p in _pieces(largest) if p < RUN_SPLIT])


def _tile_rows(row, count):
    return pl.ds(pl.multiple_of(row * SUBLANES, SUBLANES), count * SUBLANES)


def _dispatch_kernel(fill_ref, rstart_ref, rlen_ref, hn_ref, pos_ref, xout_ref, sorted_ref, zero_ref, sem, zsem):
    tm = hn_ref.shape[0]
    n_sorted = EXPERT_TOPK * tm
    step = pl.program_id(0)

    zero_rows = zero_ref.shape[0] // SUBLANES
    n_rows = xout_ref.shape[0] // SUBLANES

    def pad_copies(e, start):
        first = fill_ref[e]
        length = fill_ref[N_EXPERTS + e]
        for piece in _pieces(zero_rows):
            offset = first + (length & ~(2 * piece - 1))

            @pl.when((length & piece) != 0)
            def _():
                cp = pltpu.make_async_copy(zero_ref.at[_tile_rows(0, piece)], xout_ref.at[_tile_rows(offset, piece)],
                                           zsem)
                cp.start() if start else cp.wait()

    def tail_copies(start):
        def one(i, carry):
            cp = pltpu.make_async_copy(zero_ref, xout_ref.at[_tile_rows(i * zero_rows, zero_rows)], zsem)
            cp.start() if start else cp.wait()
            return carry

        lax.fori_loop(fill_ref[2 * N_EXPERTS] // zero_rows, n_rows // zero_rows, one, 0)

    @pl.when(step == 0)
    def _():
        zero_ref[...] = jnp.zeros_like(zero_ref)
        lax.fori_loop(0, N_EXPERTS, lambda e, c: (pad_copies(e, True), c)[1], 0)
        tail_copies(True)

    pos = pos_ref[...]
    p_iota = lax.broadcasted_iota(jnp.int32, (n_sorted, tm), 0)
    picks = jnp.where((p_iota == pos[0:1, :]) | (p_iota == pos[1:2, :]), 1.0, 0.0).astype(BF16)
    xs = jnp.dot(picks, hn_ref[...], preferred_element_type=F32)
    for k in range(SUBLANES):
        sorted_ref[pl.ds(k, n_sorted, stride=SUBLANES), :] = xs[:, k * LANES:(k + 1) * LANES]

    def run(e, offset):
        length = rlen_ref[step * N_EXPERTS + e]
        first = rstart_ref[step * N_EXPERTS + e]

        _start_run_copies(length, tm, lambda done, piece: pltpu.make_async_copy(
            sorted_ref.at[_tile_rows(offset + done, piece)], xout_ref.at[_tile_rows(first + done, piece)], sem))
        return offset + length

    lax.fori_loop(0, N_EXPERTS, run, 0)

    pltpu.make_async_copy(sorted_ref, xout_ref.at[_tile_rows(0, n_sorted)], sem).wait()

    @pl.when(step == 0)
    def _():
        lax.fori_loop(0, N_EXPERTS, lambda e, c: (pad_copies(e, False), c)[1], 0)
        tail_copies(False)


def _dispatch(fill, run_start, run_len, hn2, pos, n_rows, tm):
    n, d = hn2.shape
    assert d == SUBLANES * LANES, "a row must be exactly one (8, 128) tile for the tile-contiguous layout"
    return pl.pallas_call(
        _dispatch_kernel,
        grid_spec=pltpu.PrefetchScalarGridSpec(
            num_scalar_prefetch=3,
            grid=(n // tm,),
            in_specs=[pl.BlockSpec((tm, d), lambda i, *_: (i, 0)),
                      pl.BlockSpec((EXPERT_TOPK, tm), lambda i, *_: (0, i))],
            out_specs=pl.BlockSpec(memory_space=pl.ANY),
            scratch_shapes=[pltpu.VMEM((EXPERT_TOPK * tm * SUBLANES, LANES), F32),
                            pltpu.VMEM((EXPERT_ROWS // 2 * SUBLANES, LANES), F32),
                            pltpu.SemaphoreType.DMA(()), pltpu.SemaphoreType.DMA(())]),
        out_shape=jax.ShapeDtypeStruct((n_rows * SUBLANES, LANES), F32),
        compiler_params=_params(("arbitrary",)),
        name="dispatch",
    )(fill, run_start, run_len, hn2, pos)


def _experts_kernel(be_ref, nu_ref, x_ref, wgu_hbm, wd_hbm, y_ref, wgu_f32_ref, wd_f32_ref, wgu_bf_ref, wd_bf_ref,
                    sems):
    rows = EXPERT_ROWS
    lines = rows * SUBLANES
    f = wd_bf_ref.shape[0]
    n_used = nu_ref[0]

    def fetch(e):
        return (pltpu.make_async_copy(wgu_hbm.at[e], wgu_f32_ref, sems.at[0]),
                pltpu.make_async_copy(wd_hbm.at[e], wd_f32_ref, sems.at[1]))

    def block(j, base):
        used = j < n_used

        @pl.when(used & (j == 0))
        def _():
            for cp in fetch(be_ref[0]):
                cp.start()

        @pl.when(used & ((j == 0) | (be_ref[j] != be_ref[jnp.maximum(j, 1) - 1])))
        def _():
            for cp in fetch(be_ref[j]):
                cp.wait()
            wgu_bf_ref[...] = wgu_f32_ref[...].astype(BF16)
            wd_bf_ref[...] = wd_f32_ref[...].astype(BF16)
            nxt = lax.while_loop(lambda k: (k < n_used) & (be_ref[jnp.minimum(k, be_ref.shape[0] - 1)] == be_ref[j]),
                                 lambda k: k + 1, j + 1)

            @pl.when(nxt < n_used)
            def _():
                for cp in fetch(be_ref[jnp.minimum(nxt, be_ref.shape[0] - 1)]):
                    cp.start()

        @pl.when(used)
        def _():
            x = jnp.concatenate([x_ref[pl.ds(base + k, rows, stride=SUBLANES), :] for k in range(SUBLANES)], axis=1)
            gu = jnp.dot(x.astype(BF16), wgu_bf_ref[...], preferred_element_type=F32)
            act = jax.nn.silu(gu[:, 0:f]) * gu[:, f:2 * f]
            y = jnp.dot(act.astype(BF16), wd_bf_ref[...], preferred_element_type=F32)
            for k in range(SUBLANES):
                y_ref[pl.ds(base + k, rows, stride=SUBLANES), :] = y[:, k * LANES:(k + 1) * LANES]

        @pl.when(jnp.logical_not(used))
        def _():
            y_ref[base:base + lines, :] = jnp.zeros((lines, LANES), F32)

    per_step = x_ref.shape[0] // lines
    for sub in range(per_step):
        block(pl.program_id(0) * per_step + sub, sub * lines)


def _experts(blk_expert, n_used, x_buf, w_gate_up, w_down):
    d = w_gate_up.shape[1]
    f = w_down.shape[1]
    r = EXPERT_ROWS
    n_blk = x_buf.shape[0] // (r * SUBLANES)
    per_step = EXPERT_BLOCKS_PER_STEP
    assert d == SUBLANES * LANES and x_buf.shape[1] == LANES, "rows are (8, 128) tiles (tile-contiguous layout)"
    assert n_blk % per_step == 0
    last_step = lambda nu: (jnp.maximum(nu[0], 1) - 1) // per_step
    return pl.pallas_call(
        _experts_kernel,
        grid_spec=pltpu.PrefetchScalarGridSpec(
            num_scalar_prefetch=2,
            grid=(n_blk // per_step,),
            in_specs=[pl.BlockSpec((per_step * r * SUBLANES, LANES), lambda s, be, nu: (jnp.minimum(s, last_step(nu)), 0)),
                      pl.BlockSpec(memory_space=pl.ANY), pl.BlockSpec(memory_space=pl.ANY)],
            out_specs=pl.BlockSpec((per_step * r * SUBLANES, LANES), lambda s, be, nu: (s, 0)),
            scratch_shapes=[pltpu.VMEM((d, 2 * f), F32), pltpu.VMEM((f, d), F32),
                            pltpu.VMEM((d, 2 * f), BF16), pltpu.VMEM((f, d), BF16),
                            pltpu.SemaphoreType.DMA((2,))]),
        out_shape=jax.ShapeDtypeStruct(x_buf.shape, F32),
        compiler_params=_params(("arbitrary",)),
        name="experts",
    )(blk_expert, n_used, x_buf, w_gate_up, w_down)


def _combine_kernel(pos_ref, rstart_ref, rlen_ref, h2_ref, gcol_ref, gfin_ref, y_ref, o_ref,
                    ysort_ref, pick0_ref, pick1_ref, sems):
    tc = h2_ref.shape[0]
    n = pos_ref.shape[0] // EXPERT_TOPK
    n_sorted = EXPERT_TOPK * tc
    step = pl.program_id(0)
    par = step % 2
    base = step * tc

    def start_gather(tile, buf):
        def run(e, offset):
            length = rlen_ref[tile * N_EXPERTS + e]
            first = rstart_ref[tile * N_EXPERTS + e]
            _start_run_copies(length, tc, lambda done, piece: pltpu.make_async_copy(
                y_ref.at[_tile_rows(first + done, piece)], ysort_ref.at[buf, _tile_rows(offset + done, piece)],
                sems.at[buf]))
            return offset + length

        lax.fori_loop(0, N_EXPERTS, run, 0)

    @pl.when(step == 0)
    def _():
        start_gather(0, 0)

    @pl.when(step + 1 < pl.num_programs(0))
    def _():
        start_gather(step + 1, 1 - par)

    pltpu.make_async_copy(y_ref.at[_tile_rows(0, n_sorted)], ysort_ref.at[par], sems.at[par]).wait()
    for t in range(tc):
        for slot, pick_ref in enumerate((pick0_ref, pick1_ref)):
            row = pos_ref[slot * n + base + t]
            pick_ref[t * SUBLANES:(t + 1) * SUBLANES, :] = ysort_ref[par, _tile_rows(row, 1), :]

    def rows_of(y_ref):
        return jnp.concatenate([y_ref[pl.ds(k, tc, stride=SUBLANES), :] for k in range(SUBLANES)], axis=1)

    g = gcol_ref[...]
    moe = g[:, 0:1] * rows_of(pick0_ref) + g[:, 1:2] * rows_of(pick1_ref)
    o_ref[...] = _rms(h2_ref[...] + moe, gfin_ref[...])


def _combine(pos_flat, run_start, run_len, h2, gcol, gfin, y_buf, tc):
    n, d = h2.shape
    return pl.pallas_call(
        _combine_kernel,
        grid_spec=pltpu.PrefetchScalarGridSpec(
            num_scalar_prefetch=3,
            grid=(n // tc,),
            in_specs=[pl.BlockSpec((tc, d), lambda i, *_: (i, 0)),
                      pl.BlockSpec((tc, LANES), lambda i, *_: (i, 0)),
                      pl.BlockSpec((1, d), lambda i, *_: (0, 0)),
                      pl.BlockSpec(memory_space=pl.ANY)],
            out_specs=pl.BlockSpec((tc, d), lambda i, *_: (i, 0)),
            scratch_shapes=[pltpu.VMEM((2, EXPERT_TOPK * tc * SUBLANES, LANES), F32),
                            pltpu.VMEM((tc * SUBLANES, LANES), F32), pltpu.VMEM((tc * SUBLANES, LANES), F32),
                            pltpu.SemaphoreType.DMA((2,))]),
        out_shape=jax.ShapeDtypeStruct((n, d), F32),
        compiler_params=_params(("arbitrary",)),
        name="combine",
    )(pos_flat, run_start, run_len, h2, gcol, gfin, y_buf)


def _rope_tables(s):
    half = ROT_DIM // 2
    inv_freq = (ROPE_THETA ** (-np.arange(0, ROT_DIM, 2, dtype=np.float32) / ROT_DIM)).astype(np.float32)
    ang = np.arange(s, dtype=np.float32)[:, None] * inv_freq[None, :]
    cos, sin = np.cos(ang), np.sin(ang)
    zeros = lambda w: np.zeros((s, w), np.float32)
    cos_h = np.concatenate([cos, cos, np.ones((s, HEAD_DIM - ROT_DIM), np.float32)], axis=1)
    sina_h = np.concatenate([-sin, zeros(HEAD_DIM - half)], axis=1)
    sinb_h = np.concatenate([zeros(half), sin, zeros(HEAD_DIM - ROT_DIM)], axis=1)
    tile = lambda t: jnp.asarray(np.tile(t, (1, LANES // HEAD_DIM)), F32)
    return tile(cos_h), tile(sina_h), tile(sinb_h)


def _block_diag(w):
    nblk, bi, bo = w.shape
    eye = jnp.eye(nblk, dtype=w.dtype)
    return jnp.einsum('hij,hg->higj', w, eye).reshape(nblk * bi, nblk * bo)


def _layer(h, mem, norm_mix, w_in, conv_w, conv_b, w_rg, b_rg, w_ig, b_ig, lru_lambda, norm_lru_out,
           norm_attn_out, w_out, norm_cross, norm_mem, w_xq, w_xkv, w_xo, norm_ffn, w_router_group,
           b_router_group, w_router_expert, b_router_expert, w_gate_up, w_down, norm_out):
    b, s, d = h.shape
    c = conv_w.shape[-1]
    n = b * s
    row = lambda t: t.reshape(1, -1)
    tm = 512 if s % 512 == 0 else MOBA_BLOCK
    ts = tm
    assert s % ts == 0 and s % tm == 0 and c == N_HEADS * HEAD_DIM

    kx, vx = _memkv(mem, row(norm_mem), w_xkv.astype(BF16))

    w_gates = jnp.concatenate([_block_diag(w_rg), _block_diag(w_ig)], axis=1).astype(BF16)
    b_gates = jnp.concatenate([b_rg, b_ig]).reshape(1, -1)
    cos_t, sina_t, sinb_t = _rope_tables(s)
    ylru, q, k, v = _mixer_in(h, row(norm_mix), w_in.astype(BF16), conv_w, row(conv_b), w_gates, b_gates,
                              row(lru_lambda), row(norm_lru_out), cos_t, sina_t, sinb_t, ts)
    yatt = _moba(q, k, v)

    w_out_b = w_out.astype(BF16)
    wr_t = jnp.zeros((ROUTER_ROWS, d), F32)
    wr_t = wr_t.at[0:N_GROUPS].set(w_router_group.T).at[SUBLANES:SUBLANES + N_EXPERTS].set(w_router_expert.T)
    wr_hi = wr_t.astype(BF16)
    wr_t = jnp.concatenate([wr_hi, (wr_t - wr_hi.astype(F32)).astype(BF16)], axis=0)
    br_t = jnp.zeros((ROUTER_ROWS, 1), F32)
    br_t = br_t.at[0:N_GROUPS, 0].set(b_router_group).at[SUBLANES:SUBLANES + N_EXPERTS, 0].set(b_router_expert)
    tri = jnp.asarray(np.triu(np.ones((tm, tm), np.float32), 1), BF16)
    h2, hn2, gcol, counts, pos, tile_cnt = _post(
        h, ylru, yatt, row(norm_attn_out), w_out_b[:c], w_out_b[c:], row(norm_cross), w_xq.astype(BF16),
        kx, vx, w_xo.astype(BF16), row(norm_ffn), wr_t, br_t, tri, tm)

    counts = counts[:, 0].astype(jnp.int32)
    padded = (counts + EXPERT_ROWS - 1) // EXPERT_ROWS * EXPERT_ROWS
    pad_ends = jnp.cumsum(padded)
    pad_starts = pad_ends - padded
    n_blocks = n * EXPERT_TOPK // EXPERT_ROWS + N_EXPERTS
    blk_first = jnp.arange(n_blocks, dtype=jnp.int32) * EXPERT_ROWS
    blk_expert = jnp.minimum(jnp.sum(blk_first[:, None] >= pad_ends[None, :], axis=1), N_EXPERTS - 1).astype(jnp.int32)
    n_used = (pad_ends[-1:] // EXPERT_ROWS).astype(jnp.int32)
    fill = jnp.concatenate([pad_starts + counts, padded - counts, pad_ends[-1:]]).astype(jnp.int32)

    n_tiles = n // tm
    tile_cnt = tile_cnt[:, 0].astype(jnp.int32).reshape(n_tiles, N_EXPERTS)
    run_start = (pad_starts[None, :] + jnp.cumsum(tile_cnt, axis=0) - tile_cnt).astype(jnp.int32).reshape(-1)
    run_len = tile_cnt.reshape(-1)
    x_buf = _dispatch(fill, run_start, run_len, hn2, pos, n_blocks * EXPERT_ROWS, tm)
    y_buf = _experts(blk_expert, n_used, x_buf, w_gate_up, w_down)
    out = _combine(pos.reshape(-1), run_start, run_len, h2, gcol, row(norm_out), y_buf, tm)
    return out.reshape(b, s, d)


def kernel(x, mem, norm_mix, w_in, conv_w, conv_b, w_rg, b_rg, w_ig, b_ig, lru_lambda, norm_lru_out, norm_attn_out,
           w_out, norm_cross, norm_mem, w_xq, w_xkv, w_xo, norm_ffn, w_router_group, b_router_group,
           w_router_expert, b_router_expert, w_gate_up, w_down, norm_final):
    depth = norm_mix.shape[0]
    assert depth == 1, "the fused final norm assumes a single layer"
    l = 0
    return _layer(x, mem, norm_mix[l], w_in[l], conv_w[l], conv_b[l], w_rg[l], b_rg[l], w_ig[l], b_ig[l],
                  lru_lambda[l], norm_lru_out[l], norm_attn_out[l], w_out[l], norm_cross[l], norm_mem[l],
                  w_xq[l], w_xkv[l], w_xo[l], norm_ffn[l], w_router_group[l], b_router_group[l],
                  w_router_expert[l], b_router_expert[l], w_gate_up[l], w_down[l], norm_final)
```

```python
import jax
import jax.numpy as jnp
import numpy as np
from jax import lax
from jax.experimental import pallas as pl
from jax.experimental.pallas import tpu as pltpu

F32 = jnp.float32
BF16 = jnp.bfloat16

CONV_WIDTH = 4
RG_C = 8.0
N_HEADS = 8
HEAD_DIM = 64
ROT_DIM = HEAD_DIM // 4
ROPE_THETA = 500000.0
MOBA_BLOCK = 256
MOBA_TOPK = 3
N_XHEADS = 4
N_GROUPS = 4
EXPERTS_PER_GROUP = 8
N_EXPERTS = N_GROUPS * EXPERTS_PER_GROUP
EXPERT_TOPK = 2
EPS = 1e-6

LANES = 128
SUBLANES = 8
NEG_BIG = -1e30
LOG2_E = 1.4426950408889634
MOBA_QROWS = 128
MOBA_LOOKAHEAD = 5
EXPERT_ROWS = 512
EXPERT_BLOCKS_PER_STEP = 4
RUN_SPLIT = 64
BF16_ROWS = 2 * SUBLANES
ROUTER_ROWS = -(-(SUBLANES + N_EXPERTS) // BF16_ROWS) * BF16_ROWS
VMEM_LIMIT = 56 * 1024 * 1024

NT_DIMS = (((1,), (1,)), ((), ()))


def _rms(x, g):
    return x * lax.rsqrt(jnp.mean(x * x, axis=-1, keepdims=True) + EPS) * g


def _params(sem):
    return pltpu.CompilerParams(dimension_semantics=sem, vmem_limit_bytes=VMEM_LIMIT)


def _memkv_kernel(mem_ref, g_ref, w_ref, k_ref, v_ref):
    d = mem_ref.shape[-1]
    mn = _rms(mem_ref[0], g_ref[...]).astype(BF16)
    kv = jnp.dot(mn, w_ref[...], preferred_element_type=F32)
    k_ref[0] = kv[:, :d].astype(BF16)
    v_ref[0] = kv[:, d:].astype(BF16)


def _memkv(mem, g, w_xkv):
    b, m, d = mem.shape
    return pl.pallas_call(
        _memkv_kernel,
        grid=(b,),
        in_specs=[pl.BlockSpec((1, m, d), lambda i: (i, 0, 0)),
                  pl.BlockSpec((1, d), lambda i: (0, 0)),
                  pl.BlockSpec((d, 2 * d), lambda i: (0, 0))],
        out_specs=[pl.BlockSpec((1, m, d), lambda i: (i, 0, 0)),
                   pl.BlockSpec((1, m, d), lambda i: (i, 0, 0))],
        out_shape=[jax.ShapeDtypeStruct((b, m, d), BF16)] * 2,
        compiler_params=_params(("arbitrary",)),
        name="memkv",
    )(mem, g, w_xkv)


def _mixer_in_kernel(x_ref, gmix_ref, win_ref, convw_ref, convb_ref, wg_ref, bg_ref, lam_ref, glru_ref,
                     cos_ref, sina_ref, sinb_ref,
                     ylru_ref, q_ref, k_ref, v_ref, xpad_ref, hcar_ref, ascan_ref, uscan_ref):
    c = ylru_ref.shape[-1]
    si = pl.program_id(1)

    @pl.when(si == 0)
    def _():
        xpad_ref[0:SUBLANES, :] = jnp.zeros((SUBLANES, c), F32)
        hcar_ref[...] = jnp.zeros_like(hcar_ref)

    ts = x_ref.shape[1]
    xn = _rms(x_ref[0], gmix_ref[...]).astype(BF16)
    proj = jnp.dot(xn, win_ref[...], preferred_element_type=F32)
    x_lru = proj[:, 0:c]
    g_lru = proj[:, c:2 * c]
    q = proj[:, 2 * c:3 * c]
    k = proj[:, 3 * c:4 * c]
    v = proj[:, 4 * c:5 * c]

    xpad_ref[SUBLANES:SUBLANES + ts, :] = x_lru
    cw = convw_ref[...]
    xc = convb_ref[...] + cw[3:4] * x_lru
    for j in range(1, CONV_WIDTH):
        xc = xc + cw[3 - j:4 - j] * xpad_ref[SUBLANES - j:SUBLANES - j + ts, :]
    xpad_ref[0:SUBLANES, :] = x_lru[ts - SUBLANES:ts, :]

    gates = jnp.dot(xc.astype(BF16), wg_ref[...], preferred_element_type=F32) + bg_ref[...]
    r = jax.nn.sigmoid(gates[:, 0:c])
    i = jax.nn.sigmoid(gates[:, c:2 * c])
    neg_lam = -lam_ref[...]
    softplus = jnp.maximum(neg_lam, 0.0) + jnp.log1p(jnp.exp(-jnp.abs(neg_lam)))
    log_a = -RG_C * r * softplus
    a = jnp.exp(log_a)
    u = jnp.sqrt(-jnp.tanh(log_a) * (a * a + 1.0)) * (i * xc)

    groups = ts // SUBLANES
    cols = c // LANES
    for j in range(cols):
        ascan_ref[j] = a[:, j * LANES:(j + 1) * LANES]
        uscan_ref[j] = u[:, j * LANES:(j + 1) * LANES]
    for r in range(SUBLANES):
        rows = pl.ds(r, groups, stride=SUBLANES)
        a_r = jnp.concatenate([ascan_ref[j, rows, :] for j in range(cols)], axis=1)
        u_r = jnp.concatenate([uscan_ref[j, rows, :] for j in range(cols)], axis=1)
        if r:
            u_r = a_r * u_acc + u_r
            a_r = a_r * a_acc
            for j in range(cols):
                ascan_ref[j, rows, :] = a_r[:, j * LANES:(j + 1) * LANES]
                uscan_ref[j, rows, :] = u_r[:, j * LANES:(j + 1) * LANES]
        a_acc, u_acc = a_r, u_r
    carry = hcar_ref[...]
    hs = []
    for gi in range(groups):
        blk = slice(gi * SUBLANES, (gi + 1) * SUBLANES)
        a_g = jnp.concatenate([ascan_ref[j, blk, :] for j in range(cols)], axis=1)
        u_g = jnp.concatenate([uscan_ref[j, blk, :] for j in range(cols)], axis=1)
        hs.append(a_g * carry + u_g)
        carry = a_acc[gi:gi + 1] * carry + u_acc[gi:gi + 1]
    h = jnp.concatenate(hs, axis=0)
    hcar_ref[...] = carry

    y = h * jax.nn.gelu(g_lru)
    ylru_ref[0] = _rms(y, glru_ref[...]).astype(BF16)

    wide = lambda t_ref: jnp.concatenate([t_ref[...]] * (c // LANES), axis=1)
    cos, sina, sinb = wide(cos_ref), wide(sina_ref), wide(sinb_ref)
    half = ROT_DIM // 2

    def rope(t):
        return t * cos + pltpu.roll(t, c - half, axis=1) * sina + pltpu.roll(t, half, axis=1) * sinb

    q_ref[0] = rope(q)
    k_ref[0] = rope(k)
    v_ref[0] = v.astype(BF16)


def _mixer_in(x, gmix, w_in, conv_w, conv_b, w_gates, b_gates, lam, glru, cos_t, sina_t, sinb_t, ts):
    b, s, d = x.shape
    c = conv_w.shape[-1]
    full = lambda shape: pl.BlockSpec(shape, lambda bi, si: (0,) * len(shape))
    tab = pl.BlockSpec((ts, LANES), lambda bi, si: (si, 0))
    seq = lambda: pl.BlockSpec((1, ts, c), lambda bi, si: (bi, si, 0))
    return pl.pallas_call(
        _mixer_in_kernel,
        grid=(b, s // ts),
        in_specs=[pl.BlockSpec((1, ts, d), lambda bi, si: (bi, si, 0)),
                  full((1, d)), full(w_in.shape), full(conv_w.shape), full((1, c)),
                  full(w_gates.shape), full((1, 2 * c)), full((1, c)), full((1, c)),
                  tab, tab, tab],
        out_specs=[seq(), seq(), seq(), seq()],
        out_shape=[jax.ShapeDtypeStruct((b, s, c), BF16),
                   jax.ShapeDtypeStruct((b, s, c), F32),
                   jax.ShapeDtypeStruct((b, s, c), F32),
                   jax.ShapeDtypeStruct((b, s, c), BF16)],
        scratch_shapes=[pltpu.VMEM((ts + SUBLANES, c), F32), pltpu.VMEM((1, c), F32),
                        pltpu.VMEM((c // LANES, ts, LANES), F32), pltpu.VMEM((c // LANES, ts, LANES), F32)],
        compiler_params=_params(("arbitrary", "arbitrary")),
        name="mixer_in",
    )(x, gmix, w_in, conv_w, conv_b, w_gates, b_gates, lam, glru, cos_t, sina_t, sinb_t)


def _moba_kernel(q_ref, k_ref, v_ref, kpat_ref, o_ref, qa_ref, ka_ref):
    s_len = q_ref.shape[1]
    nb = s_len // MOBA_BLOCK
    n_pick = min(MOBA_TOPK, nb)
    q = q_ref[0]
    k = k_ref[0]
    lane = lax.broadcasted_iota(jnp.int32, (1, LANES), 1)
    kmean = jnp.concatenate(
        [jnp.sum(k[n * MOBA_BLOCK:(n + 1) * MOBA_BLOCK], axis=0, keepdims=True) for n in range(nb)],
        axis=0) * (1.0 / MOBA_BLOCK)

    blk_row = lax.broadcasted_iota(jnp.int32, (nb, s_len), 0)
    q_blk = lax.broadcasted_iota(jnp.int32, (nb, s_len), 1) // MOBA_BLOCK
    past = blk_row < q_blk

    q_hi = q.astype(BF16)
    q_lo = (q - q_hi.astype(F32)).astype(BF16)
    for h in range(2):
        own = (lane >= h * HEAD_DIM) & (lane < (h + 1) * HEAD_DIM)
        off = (1 - h) * HEAD_DIM
        km = jnp.where(own, kmean, 0.0)
        km_hi = km.astype(BF16)
        km_split = jnp.concatenate([km_hi, (km - km_hi.astype(F32)).astype(BF16)], axis=0)
        parts = (lax.dot_general(km_split, q_hi, NT_DIMS, preferred_element_type=F32)
                 + lax.dot_general(km_split, q_lo, NT_DIMS, preferred_element_type=F32))
        gate = jnp.where(past, parts[0:nb] + parts[nb:2 * nb], -jnp.inf)
        rank = jnp.zeros((nb, s_len), jnp.int32)
        for m in range(nb):
            gm = gate[m:m + 1, :]
            ahead = (gm > gate) | ((gm == gate) & (m < blk_row))
            rank = rank + ahead.astype(jnp.int32)
        allowed = ((rank < n_pick) & past) | (blk_row == q_blk)
        bias = jnp.where(allowed, 0.0, NEG_BIG)
        pieces = []
        if off:
            pieces.append(jnp.zeros((off, s_len), F32))
        pieces.append(bias)
        if LANES - off - nb:
            pieces.append(jnp.zeros((LANES - off - nb, s_len), F32))
        bias_lanes = jnp.concatenate(pieces, axis=0).T
        qa_ref[h] = jnp.where(own, q * (HEAD_DIM ** -0.5 * LOG2_E), bias_lanes).astype(BF16)
        ka_ref[h] = jnp.where(own, k.astype(BF16), kpat_ref[h])

    own0 = lane < HEAD_DIM

    parts = MOBA_BLOCK // MOBA_QROWS

    def q_rows(qi, part):
        return slice(qi * MOBA_BLOCK + part * MOBA_QROWS, qi * MOBA_BLOCK + (part + 1) * MOBA_QROWS)

    def n_keys(qi, part):
        return qi * MOBA_BLOCK + (part + 1) * MOBA_QROWS

    def scores(qi, part, h):
        return lax.dot_general(qa_ref[h, q_rows(qi, part), :], ka_ref[h, 0:n_keys(qi, part), :], NT_DIMS,
                               preferred_element_type=F32)

    units = [(qi, part, h) for qi in range(nb) for part in range(parts) for h in range(2)]
    ahead = [scores(*u) for u in units[:MOBA_LOOKAHEAD]]
    outs = []

    def weighted_values(p, l, qi, part, h):
        outs.append(jnp.dot(p, v_ref[0, 0:n_keys(qi, part), :], preferred_element_type=F32) / l)
        if h:
            o_ref[0, q_rows(qi, part), :] = jnp.where(own0, outs[0], outs[1])
            outs.clear()

    pending = None
    for n, (qi, part, h) in enumerate(units):
        s = ahead.pop(0)
        if n + MOBA_LOOKAHEAD < len(units):
            ahead.append(scores(*units[n + MOBA_LOOKAHEAD]))
        n_past = qi * MOBA_BLOCK
        own_keys = (part + 1) * MOBA_QROWS
        key_in_blk = lax.broadcasted_iota(jnp.int32, (MOBA_QROWS, own_keys), 1)
        q_in_blk = lax.broadcasted_iota(jnp.int32, (MOBA_QROWS, own_keys), 0) + part * MOBA_QROWS
        s_own = jnp.where(key_in_blk <= q_in_blk, s[:, n_past:], NEG_BIG)
        s = jnp.concatenate([s[:, :n_past], s_own], axis=1) if qi else s_own
        p = jnp.exp2(s - jnp.max(s, axis=-1, keepdims=True))
        l = jnp.sum(p, axis=-1, keepdims=True)
        if pending is not None:
            weighted_values(*pending)
        pending = (p.astype(BF16), l, qi, part, h)
    weighted_values(*pending)


def _key_block_pattern(s):
    pat = np.zeros((2, s, LANES), np.float32)
    for h in range(2):
        off = (1 - h) * HEAD_DIM
        pat[h, np.arange(s), off + np.arange(s) // MOBA_BLOCK] = 1.0
    return jnp.asarray(pat, BF16)


def _moba(q, k, v):
    b, s, c = q.shape
    spec = lambda: pl.BlockSpec((1, s, LANES), lambda bi, hi: (bi, 0, hi))
    return pl.pallas_call(
        _moba_kernel,
        grid=(b, c // LANES),
        in_specs=[spec(), spec(), spec(), pl.BlockSpec((2, s, LANES), lambda bi, hi: (0, 0, 0))],
        out_specs=spec(),
        out_shape=jax.ShapeDtypeStruct((b, s, c), F32),
        scratch_shapes=[pltpu.VMEM((2, s, LANES), BF16), pltpu.VMEM((2, s, LANES), BF16)],
        compiler_params=_params(("arbitrary", "arbitrary")),
        name="moba",
    )(q, k, v, _key_block_pattern(s))


def _post_kernel(x_ref, ylru_ref, yatt_ref, gatt_ref, woa_ref, wob_ref, gcross_ref, wxq_ref, kx_ref, vx_ref,
                 wxo_ref, gffn_ref, wr_ref, br_ref, tri_ref, elow_ref,
                 h2_ref, hn2_ref, gcol_ref, cnt_ref, pos_ref, tcnt_ref, carry_ref):
    tm = x_ref.shape[1]
    d = x_ref.shape[2]
    xd = d // N_XHEADS
    first = (pl.program_id(0) == 0) & (pl.program_id(1) == 0)

    @pl.when(first)
    def _():
        carry_ref[...] = jnp.zeros_like(carry_ref)

    ya = _rms(yatt_ref[0], gatt_ref[...]).astype(BF16)
    mix = (jnp.dot(ylru_ref[0], woa_ref[...], preferred_element_type=F32)
           + jnp.dot(ya, wob_ref[...], preferred_element_type=F32))
    h1 = x_ref[0] + mix

    hn = _rms(h1, gcross_ref[...]).astype(BF16)
    qx = jnp.dot(hn, wxq_ref[...], preferred_element_type=F32)
    heads = []
    for hh in range(N_XHEADS):
        qh = (qx[:, hh * xd:(hh + 1) * xd] * (xd ** -0.5)).astype(BF16)
        s = lax.dot_general(qh, kx_ref[0, :, hh * xd:(hh + 1) * xd], NT_DIMS, preferred_element_type=F32)
        p = jnp.exp(s - jnp.max(s, axis=-1, keepdims=True))
        l = jnp.sum(p, axis=-1, keepdims=True)
        o = jnp.dot(p.astype(BF16), vx_ref[0, :, hh * xd:(hh + 1) * xd], preferred_element_type=F32) / l
        heads.append(o.astype(BF16))
    h2 = h1 + jnp.dot(jnp.concatenate(heads, axis=1), wxo_ref[...], preferred_element_type=F32)
    hn2 = _rms(h2, gffn_ref[...])
    h2_ref[...] = h2
    hn2_ref[...] = hn2.astype(BF16)

    x_hi = hn2.astype(BF16)
    x_lo = (hn2 - x_hi.astype(F32)).astype(BF16)
    by_hi = lax.dot_general(wr_ref[...], x_hi, NT_DIMS, preferred_element_type=F32)
    by_lo = lax.dot_general(wr_ref[0:ROUTER_ROWS], x_lo, NT_DIMS, preferred_element_type=F32)
    logits = by_hi[0:ROUTER_ROWS] + by_hi[ROUTER_ROWS:2 * ROUTER_ROWS] + by_lo + br_ref[...]
    gl = logits[0:N_GROUPS]
    gmax = jnp.max(gl, axis=0, keepdims=True)
    gi = lax.broadcasted_iota(jnp.int32, gl.shape, 0)
    grp = jnp.min(jnp.where(gl == gmax, gi, N_GROUPS), axis=0, keepdims=True)
    g_w = 1.0 / jnp.sum(jnp.exp(gl - gmax), axis=0, keepdims=True)
    el = jnp.zeros((EXPERTS_PER_GROUP, tm), F32)
    for g in range(N_GROUPS):
        lo = SUBLANES + g * EXPERTS_PER_GROUP
        el = jnp.where(grp == g, logits[lo:lo + EXPERTS_PER_GROUP], el)
    ee = jnp.exp(el - jnp.max(el, axis=0, keepdims=True))
    ep = ee / jnp.sum(ee, axis=0, keepdims=True)
    ei = lax.broadcasted_iota(jnp.int32, ep.shape, 0)
    p1 = jnp.max(ep, axis=0, keepdims=True)
    i1 = jnp.min(jnp.where(ep == p1, ei, EXPERTS_PER_GROUP), axis=0, keepdims=True)
    ep_rest = jnp.where(ei == i1, -1.0, ep)
    p2 = jnp.max(ep_rest, axis=0, keepdims=True)
    i2 = jnp.min(jnp.where(ep_rest == p2, ei, EXPERTS_PER_GROUP), axis=0, keepdims=True)
    den = p1 + p2
    gate1 = g_w * p1 / den
    gate2 = g_w * p2 / den
    e1 = grp * EXPERTS_PER_GROUP + i1
    e2 = grp * EXPERTS_PER_GROUP + i2
    li = lax.broadcasted_iota(jnp.int32, (LANES, tm), 0)
    gcol_ref[...] = jnp.where(li == 0, gate1, jnp.where(li == 1, gate2, 0.0)).T

    xi = lax.broadcasted_iota(jnp.int32, (N_EXPERTS, tm), 0)
    oh1 = xi == e1
    oh2 = xi == e2
    cnt = oh1.astype(F32) + oh2.astype(F32)
    tile_cnt = jnp.broadcast_to(jnp.sum(cnt, axis=1, keepdims=True), (N_EXPERTS, LANES))
    carry_ref[...] = carry_ref[...] + tile_cnt
    cnt_ref[...] = carry_ref[...]
    tcnt_ref[...] = tile_cnt
    in_tile = jnp.dot(cnt.astype(BF16), tri_ref[...], preferred_element_type=F32)
    tile_start = jnp.dot(elow_ref[...], tile_cnt, precision=lax.Precision.HIGHEST,
                         preferred_element_type=F32)[:, 0:1]
    local = in_tile + tile_start
    pos_ref[0:1, :] = jnp.sum(jnp.where(oh1, local, 0.0), axis=0, keepdims=True).astype(jnp.int32)
    pos_ref[1:2, :] = jnp.sum(jnp.where(oh2, local, 0.0), axis=0, keepdims=True).astype(jnp.int32)


def _post(x, ylru, yatt, gatt, wo_a, wo_b, gcross, wxq, kx, vx, wxo, gffn, wr_t, br_t, tri, tm):
    b, s, d = x.shape
    c = ylru.shape[-1]
    m = kx.shape[1]
    n = b * s
    nt = s // tm
    full = lambda shape: pl.BlockSpec(shape, lambda bi, si: (0,) * len(shape))
    tok = lambda rows: pl.BlockSpec((rows, tm), lambda bi, si: (0, bi * nt + si))
    return pl.pallas_call(
        _post_kernel,
        grid=(b, nt),
        in_specs=[pl.BlockSpec((1, tm, d), lambda bi, si: (bi, si, 0)),
                  pl.BlockSpec((1, tm, c), lambda bi, si: (bi, si, 0)),
                  pl.BlockSpec((1, tm, c), lambda bi, si: (bi, si, 0)),
                  full((1, c)), full((c, d)), full((c, d)), full((1, d)), full((d, d)),
                  pl.BlockSpec((1, m, d), lambda bi, si: (bi, 0, 0)),
                  pl.BlockSpec((1, m, d), lambda bi, si: (bi, 0, 0)),
                  full((d, d)), full((1, d)), full(wr_t.shape), full(br_t.shape), full((tm, tm)),
                  full((N_EXPERTS, N_EXPERTS))],
        out_specs=[pl.BlockSpec((tm, d), lambda bi, si: (bi * nt + si, 0)),
                   pl.BlockSpec((tm, d), lambda bi, si: (bi * nt + si, 0)),
                   pl.BlockSpec((tm, LANES), lambda bi, si: (bi * nt + si, 0)),
                   full((N_EXPERTS, LANES)),
                   tok(EXPERT_TOPK),
                   pl.BlockSpec((N_EXPERTS, LANES), lambda bi, si: (bi * nt + si, 0))],
        out_shape=[jax.ShapeDtypeStruct((n, d), F32),
                   jax.ShapeDtypeStruct((n, d), BF16),
                   jax.ShapeDtypeStruct((n, LANES), F32),
                   jax.ShapeDtypeStruct((N_EXPERTS, LANES), F32),
                   jax.ShapeDtypeStruct((EXPERT_TOPK, n), jnp.int32),
                   jax.ShapeDtypeStruct((n // tm * N_EXPERTS, LANES), F32)],
        scratch_shapes=[pltpu.VMEM((N_EXPERTS, LANES), F32)],
        compiler_params=_params(("arbitrary", "arbitrary")),
        name="post",
    )(x, ylru, yatt, gatt, wo_a, wo_b, gcross, wxq, kx, vx, wxo, gffn, wr_t, br_t, tri,
      jnp.asarray(np.tril(np.ones((N_EXPERTS, N_EXPERTS), np.float32), -1)))


def _pieces(largest):
    piece = largest
    while piece:
        yield piece
        piece //= 2


def _start_run_copies(length, largest, copy_of):
    def start(pieces):
        for piece in pieces:
            done = length & ~(2 * piece - 1)

            @pl.when((length & piece) != 0)
            def _():
                copy_of(done, piece).start()

    @pl.when(length >= RUN_SPLIT)
    def _():
        start([p for p in _pieces(largest) if p >= RUN_SPLIT])

    start([p for p in _pieces(largest) if p < RUN_SPLIT])


def _tile_rows(row, count):
    return pl.ds(pl.multiple_of(row * SUBLANES, SUBLANES), count * SUBLANES)


def _dispatch_kernel(fill_ref, rstart_ref, rlen_ref, hn_ref, pos_ref, xout_ref, sorted_ref, zero_ref, sem, zsem):
    tm = hn_ref.shape[0]
    n_sorted = EXPERT_TOPK * tm
    step = pl.program_id(0)

    zero_rows = zero_ref.shape[0] // SUBLANES
    n_rows = xout_ref.shape[0] // SUBLANES

    def pad_copies(e, start):
        first = fill_ref[e]
        length = fill_ref[N_EXPERTS + e]
        for piece in _pieces(zero_rows):
            offset = first + (length & ~(2 * piece - 1))

            @pl.when((length & piece) != 0)
            def _():
                cp = pltpu.make_async_copy(zero_ref.at[_tile_rows(0, piece)], xout_ref.at[_tile_rows(offset, piece)],
                                           zsem)
                cp.start() if start else cp.wait()

    def tail_copies(start):
        def one(i, carry):
            cp = pltpu.make_async_copy(zero_ref, xout_ref.at[_tile_rows(i * zero_rows, zero_rows)], zsem)
            cp.start() if start else cp.wait()
            return carry

        lax.fori_loop(fill_ref[2 * N_EXPERTS] // zero_rows, n_rows // zero_rows, one, 0)

    @pl.when(step == 0)
    def _():
        zero_ref[...] = jnp.zeros_like(zero_ref)
        lax.fori_loop(0, N_EXPERTS, lambda e, c: (pad_copies(e, True), c)[1], 0)
        tail_copies(True)

    pos = pos_ref[...]
    p_iota = lax.broadcasted_iota(jnp.int32, (n_sorted, tm), 0)
    picks = jnp.where((p_iota == pos[0:1, :]) | (p_iota == pos[1:2, :]), 1.0, 0.0).astype(BF16)
    xs = jnp.dot(picks, hn_ref[...], preferred_element_type=F32)
    for k in range(SUBLANES):
        sorted_ref[pl.ds(k, n_sorted, stride=SUBLANES), :] = xs[:, k * LANES:(k + 1) * LANES]

    def run(e, offset):
        length = rlen_ref[step * N_EXPERTS + e]
        first = rstart_ref[step * N_EXPERTS + e]

        _start_run_copies(length, tm, lambda done, piece: pltpu.make_async_copy(
            sorted_ref.at[_tile_rows(offset + done, piece)], xout_ref.at[_tile_rows(first + done, piece)], sem))
        return offset + length

    lax.fori_loop(0, N_EXPERTS, run, 0)

    pltpu.make_async_copy(sorted_ref, xout_ref.at[_tile_rows(0, n_sorted)], sem).wait()

    @pl.when(step == 0)
    def _():
        lax.fori_loop(0, N_EXPERTS, lambda e, c: (pad_copies(e, False), c)[1], 0)
        tail_copies(False)


def _dispatch(fill, run_start, run_len, hn2, pos, n_rows, tm):
    n, d = hn2.shape
    assert d == SUBLANES * LANES, "a row must be exactly one (8, 128) tile for the tile-contiguous layout"
    return pl.pallas_call(
        _dispatch_kernel,
        grid_spec=pltpu.PrefetchScalarGridSpec(
            num_scalar_prefetch=3,
            grid=(n // tm,),
            in_specs=[pl.BlockSpec((tm, d), lambda i, *_: (i, 0)),
                      pl.BlockSpec((EXPERT_TOPK, tm), lambda i, *_: (0, i))],
            out_specs=pl.BlockSpec(memory_space=pl.ANY),
            scratch_shapes=[pltpu.VMEM((EXPERT_TOPK * tm * SUBLANES, LANES), F32),
                            pltpu.VMEM((EXPERT_ROWS // 2 * SUBLANES, LANES), F32),
                            pltpu.SemaphoreType.DMA(()), pltpu.SemaphoreType.DMA(())]),
        out_shape=jax.ShapeDtypeStruct((n_rows * SUBLANES, LANES), F32),
        compiler_params=_params(("arbitrary",)),
        name="dispatch",
    )(fill, run_start, run_len, hn2, pos)


def _experts_kernel(be_ref, nu_ref, x_ref, wgu_hbm, wd_hbm, y_ref, wgu_f32_ref, wd_f32_ref, wgu_bf_ref, wd_bf_ref,
                    sems):
    rows = EXPERT_ROWS
    lines = rows * SUBLANES
    f = wd_bf_ref.shape[0]
    n_used = nu_ref[0]

    def fetch(e):
        return (pltpu.make_async_copy(wgu_hbm.at[e], wgu_f32_ref, sems.at[0]),
                pltpu.make_async_copy(wd_hbm.at[e], wd_f32_ref, sems.at[1]))

    def block(j, base):
        used = j < n_used

        @pl.when(used & (j == 0))
        def _():
            for cp in fetch(be_ref[0]):
                cp.start()

        @pl.when(used & ((j == 0) | (be_ref[j] != be_ref[jnp.maximum(j, 1) - 1])))
        def _():
            for cp in fetch(be_ref[j]):
                cp.wait()
            wgu_bf_ref[...] = wgu_f32_ref[...].astype(BF16)
            wd_bf_ref[...] = wd_f32_ref[...].astype(BF16)
            nxt = lax.while_loop(lambda k: (k < n_used) & (be_ref[jnp.minimum(k, be_ref.shape[0] - 1)] == be_ref[j]),
                                 lambda k: k + 1, j + 1)

            @pl.when(nxt < n_used)
            def _():
                for cp in fetch(be_ref[jnp.minimum(nxt, be_ref.shape[0] - 1)]):
                    cp.start()

        @pl.when(used)
        def _():
            x = jnp.concatenate([x_ref[pl.ds(base + k, rows, stride=SUBLANES), :] for k in range(SUBLANES)], axis=1)
            gu = jnp.dot(x.astype(BF16), wgu_bf_ref[...], preferred_element_type=F32)
            act = jax.nn.silu(gu[:, 0:f]) * gu[:, f:2 * f]
            y = jnp.dot(act.astype(BF16), wd_bf_ref[...], preferred_element_type=F32)
            for k in range(SUBLANES):
                y_ref[pl.ds(base + k, rows, stride=SUBLANES), :] = y[:, k * LANES:(k + 1) * LANES]

        @pl.when(jnp.logical_not(used))
        def _():
            y_ref[base:base + lines, :] = jnp.zeros((lines, LANES), F32)

    per_step = x_ref.shape[0] // lines
    for sub in range(per_step):
        block(pl.program_id(0) * per_step + sub, sub * lines)


def _experts(blk_expert, n_used, x_buf, w_gate_up, w_down):
    d = w_gate_up.shape[1]
    f = w_down.shape[1]
    r = EXPERT_ROWS
    n_blk = x_buf.shape[0] // (r * SUBLANES)
    per_step = EXPERT_BLOCKS_PER_STEP
    assert d == SUBLANES * LANES and x_buf.shape[1] == LANES, "rows are (8, 128) tiles (tile-contiguous layout)"
    assert n_blk % per_step == 0
    last_step = lambda nu: (jnp.maximum(nu[0], 1) - 1) // per_step
    return pl.pallas_call(
        _experts_kernel,
        grid_spec=pltpu.PrefetchScalarGridSpec(
            num_scalar_prefetch=2,
            grid=(n_blk // per_step,),
            in_specs=[pl.BlockSpec((per_step * r * SUBLANES, LANES), lambda s, be, nu: (jnp.minimum(s, last_step(nu)), 0)),
                      pl.BlockSpec(memory_space=pl.ANY), pl.BlockSpec(memory_space=pl.ANY)],
            out_specs=pl.BlockSpec((per_step * r * SUBLANES, LANES), lambda s, be, nu: (s, 0)),
            scratch_shapes=[pltpu.VMEM((d, 2 * f), F32), pltpu.VMEM((f, d), F32),
                            pltpu.VMEM((d, 2 * f), BF16), pltpu.VMEM((f, d), BF16),
                            pltpu.SemaphoreType.DMA((2,))]),
        out_shape=jax.ShapeDtypeStruct(x_buf.shape, F32),
        compiler_params=_params(("arbitrary",)),
        name="experts",
    )(blk_expert, n_used, x_buf, w_gate_up, w_down)


def _combine_kernel(pos_ref, rstart_ref, rlen_ref, h2_ref, gcol_ref, gfin_ref, y_ref, o_ref,
                    ysort_ref, pick0_ref, pick1_ref, sems):
    tc = h2_ref.shape[0]
    n = pos_ref.shape[0] // EXPERT_TOPK
    n_sorted = EXPERT_TOPK * tc
    step = pl.program_id(0)
    par = step % 2
    base = step * tc

    def start_gather(tile, buf):
        def run(e, offset):
            length = rlen_ref[tile * N_EXPERTS + e]
            first = rstart_ref[tile * N_EXPERTS + e]
            _start_run_copies(length, tc, lambda done, piece: pltpu.make_async_copy(
                y_ref.at[_tile_rows(first + done, piece)], ysort_ref.at[buf, _tile_rows(offset + done, piece)],
                sems.at[buf]))
            return offset + length

        lax.fori_loop(0, N_EXPERTS, run, 0)

    @pl.when(step == 0)
    def _():
        start_gather(0, 0)

    @pl.when(step + 1 < pl.num_programs(0))
    def _():
        start_gather(step + 1, 1 - par)

    pltpu.make_async_copy(y_ref.at[_tile_rows(0, n_sorted)], ysort_ref.at[par], sems.at[par]).wait()
    for t in range(tc):
        for slot, pick_ref in enumerate((pick0_ref, pick1_ref)):
            row = pos_ref[slot * n + base + t]
            pick_ref[t * SUBLANES:(t + 1) * SUBLANES, :] = ysort_ref[par, _tile_rows(row, 1), :]

    def rows_of(y_ref):
        return jnp.concatenate([y_ref[pl.ds(k, tc, stride=SUBLANES), :] for k in range(SUBLANES)], axis=1)

    g = gcol_ref[...]
    moe = g[:, 0:1] * rows_of(pick0_ref) + g[:, 1:2] * rows_of(pick1_ref)
    o_ref[...] = _rms(h2_ref[...] + moe, gfin_ref[...])


def _combine(pos_flat, run_start, run_len, h2, gcol, gfin, y_buf, tc):
    n, d = h2.shape
    return pl.pallas_call(
        _combine_kernel,
        grid_spec=pltpu.PrefetchScalarGridSpec(
            num_scalar_prefetch=3,
            grid=(n // tc,),
            in_specs=[pl.BlockSpec((tc, d), lambda i, *_: (i, 0)),
                      pl.BlockSpec((tc, LANES), lambda i, *_: (i, 0)),
                      pl.BlockSpec((1, d), lambda i, *_: (0, 0)),
                      pl.BlockSpec(memory_space=pl.ANY)],
            out_specs=pl.BlockSpec((tc, d), lambda i, *_: (i, 0)),
            scratch_shapes=[pltpu.VMEM((2, EXPERT_TOPK * tc * SUBLANES, LANES), F32),
                            pltpu.VMEM((tc * SUBLANES, LANES), F32), pltpu.VMEM((tc * SUBLANES, LANES), F32),
                            pltpu.SemaphoreType.DMA((2,))]),
        out_shape=jax.ShapeDtypeStruct((n, d), F32),
        compiler_params=_params(("arbitrary",)),
        name="combine",
    )(pos_flat, run_start, run_len, h2, gcol, gfin, y_buf)


def _rope_tables(s):
    half = ROT_DIM // 2
    inv_freq = (ROPE_THETA ** (-np.arange(0, ROT_DIM, 2, dtype=np.float32) / ROT_DIM)).astype(np.float32)
    ang = np.arange(s, dtype=np.float32)[:, None] * inv_freq[None, :]
    cos, sin = np.cos(ang), np.sin(ang)
    zeros = lambda w: np.zeros((s, w), np.float32)
    cos_h = np.concatenate([cos, cos, np.ones((s, HEAD_DIM - ROT_DIM), np.float32)], axis=1)
    sina_h = np.concatenate([-sin, zeros(HEAD_DIM - half)], axis=1)
    sinb_h = np.concatenate([zeros(half), sin, zeros(HEAD_DIM - ROT_DIM)], axis=1)
    tile = lambda t: jnp.asarray(np.tile(t, (1, LANES // HEAD_DIM)), F32)
    return tile(cos_h), tile(sina_h), tile(sinb_h)


def _block_diag(w):
    nblk, bi, bo = w.shape
    eye = jnp.eye(nblk, dtype=w.dtype)
    return jnp.einsum('hij,hg->higj', w, eye).reshape(nblk * bi, nblk * bo)


def _layer(h, mem, norm_mix, w_in, conv_w, conv_b, w_rg, b_rg, w_ig, b_ig, lru_lambda, norm_lru_out,
           norm_attn_out, w_out, norm_cross, norm_mem, w_xq, w_xkv, w_xo, norm_ffn, w_router_group,
           b_router_group, w_router_expert, b_router_expert, w_gate_up, w_down, norm_out):
    b, s, d = h.shape
    c = conv_w.shape[-1]
    n = b * s
    row = lambda t: t.reshape(1, -1)
    tm = 512 if s % 512 == 0 else MOBA_BLOCK
    ts = tm
    assert s % ts == 0 and s % tm == 0 and c == N_HEADS * HEAD_DIM

    kx, vx = _memkv(mem, row(norm_mem), w_xkv.astype(BF16))

    w_gates = jnp.concatenate([_block_diag(w_rg), _block_diag(w_ig)], axis=1).astype(BF16)
    b_gates = jnp.concatenate([b_rg, b_ig]).reshape(1, -1)
    cos_t, sina_t, sinb_t = _rope_tables(s)
    ylru, q, k, v = _mixer_in(h, row(norm_mix), w_in.astype(BF16), conv_w, row(conv_b), w_gates, b_gates,
                              row(lru_lambda), row(norm_lru_out), cos_t, sina_t, sinb_t, ts)
    yatt = _moba(q, k, v)

    w_out_b = w_out.astype(BF16)
    wr_t = jnp.zeros((ROUTER_ROWS, d), F32)
    wr_t = wr_t.at[0:N_GROUPS].set(w_router_group.T).at[SUBLANES:SUBLANES + N_EXPERTS].set(w_router_expert.T)
    wr_hi = wr_t.astype(BF16)
    wr_t = jnp.concatenate([wr_hi, (wr_t - wr_hi.astype(F32)).astype(BF16)], axis=0)
    br_t = jnp.zeros((ROUTER_ROWS, 1), F32)
    br_t = br_t.at[0:N_GROUPS, 0].set(b_router_group).at[SUBLANES:SUBLANES + N_EXPERTS, 0].set(b_router_expert)
    tri = jnp.asarray(np.triu(np.ones((tm, tm), np.float32), 1), BF16)
    h2, hn2, gcol, counts, pos, tile_cnt = _post(
        h, ylru, yatt, row(norm_attn_out), w_out_b[:c], w_out_b[c:], row(norm_cross), w_xq.astype(BF16),
        kx, vx, w_xo.astype(BF16), row(norm_ffn), wr_t, br_t, tri, tm)

    counts = counts[:, 0].astype(jnp.int32)
    padded = (counts + EXPERT_ROWS - 1) // EXPERT_ROWS * EXPERT_ROWS
    pad_ends = jnp.cumsum(padded)
    pad_starts = pad_ends - padded
    n_blocks = n * EXPERT_TOPK // EXPERT_ROWS + N_EXPERTS
    blk_first = jnp.arange(n_blocks, dtype=jnp.int32) * EXPERT_ROWS
    blk_expert = jnp.minimum(jnp.sum(blk_first[:, None] >= pad_ends[None, :], axis=1), N_EXPERTS - 1).astype(jnp.int32)
    n_used = (pad_ends[-1:] // EXPERT_ROWS).astype(jnp.int32)
    fill = jnp.concatenate([pad_starts + counts, padded - counts, pad_ends[-1:]]).astype(jnp.int32)

    n_tiles = n // tm
    tile_cnt = tile_cnt[:, 0].astype(jnp.int32).reshape(n_tiles, N_EXPERTS)
    run_start = (pad_starts[None, :] + jnp.cumsum(tile_cnt, axis=0) - tile_cnt).astype(jnp.int32).reshape(-1)
    run_len = tile_cnt.reshape(-1)
    x_buf = _dispatch(fill, run_start, run_len, hn2, pos, n_blocks * EXPERT_ROWS, tm)
    y_buf = _experts(blk_expert, n_used, x_buf, w_gate_up, w_down)
    out = _combine(pos.reshape(-1), run_start, run_len, h2, gcol, row(norm_out), y_buf, tm)
    return out.reshape(b, s, d)


def kernel(x, mem, norm_mix, w_in, conv_w, conv_b, w_rg, b_rg, w_ig, b_ig, lru_lambda, norm_lru_out, norm_attn_out,
           w_out, norm_cross, norm_mem, w_xq, w_xkv, w_xo, norm_ffn, w_router_group, b_router_group,
           w_router_expert, b_router_expert, w_gate_up, w_down, norm_final):
    depth = norm_mix.shape[0]
    assert depth == 1, "the fused final norm assumes a single layer"
    l = 0
    return _layer(x, mem, norm_mix[l], w_in[l], conv_w[l], conv_b[l], w_rg[l], b_rg[l], w_ig[l], b_ig[l],
                  lru_lambda[l], norm_lru_out[l], norm_attn_out[l], w_out[l], norm_cross[l], norm_mem[l],
                  w_xq[l], w_xkv[l], w_xo[l], norm_ffn[l], w_router_group[l], b_router_group[l],
                  w_router_expert[l], b_router_expert[l], w_gate_up[l], w_down[l], norm_final)
```

```python
import jax
import jax.numpy as jnp
import numpy as np
from jax import lax
from jax.experimental import pallas as pl
from jax.experimental.pallas import tpu as pltpu

F32 = jnp.float32
BF16 = jnp.bfloat16

CONV_WIDTH = 4
RG_C = 8.0
N_HEADS = 8
HEAD_DIM = 64
ROT_DIM = HEAD_DIM // 4
ROPE_THETA = 500000.0
MOBA_BLOCK = 256
MOBA_TOPK = 3
N_XHEADS = 4
N_GROUPS = 4
EXPERTS_PER_GROUP = 8
N_EXPERTS = N_GROUPS * EXPERTS_PER_GROUP
EXPERT_TOPK = 2
EPS = 1e-6

LANES = 128
SUBLANES = 8
NEG_BIG = -1e30
LOG2_E = 1.4426950408889634
MOBA_PAIRS_PER_STEP = 2
MOBA_QROWS = 128
MOBA_LOOKAHEAD = 5
EXPERT_ROWS = 512
EXPERT_BLOCKS_PER_STEP = 4
RUN_SPLIT = 64
BF16_ROWS = 2 * SUBLANES
ROUTER_ROWS = -(-(SUBLANES + N_EXPERTS) // BF16_ROWS) * BF16_ROWS
VMEM_LIMIT = 56 * 1024 * 1024

NT_DIMS = (((1,), (1,)), ((), ()))


def _rms(x, g):
    return x * lax.rsqrt(jnp.mean(x * x, axis=-1, keepdims=True) + EPS) * g


def _params(sem):
    return pltpu.CompilerParams(dimension_semantics=sem, vmem_limit_bytes=VMEM_LIMIT)


def _memkv_kernel(mem_ref, g_ref, w_ref, k_ref, v_ref):
    d = mem_ref.shape[-1]
    mn = _rms(mem_ref[0], g_ref[...]).astype(BF16)
    kv = jnp.dot(mn, w_ref[...], preferred_element_type=F32)
    k_ref[0] = kv[:, :d].astype(BF16)
    v_ref[0] = kv[:, d:].astype(BF16)


def _memkv(mem, g, w_xkv):
    b, m, d = mem.shape
    return pl.pallas_call(
        _memkv_kernel,
        grid=(b,),
        in_specs=[pl.BlockSpec((1, m, d), lambda i: (i, 0, 0)),
                  pl.BlockSpec((1, d), lambda i: (0, 0)),
                  pl.BlockSpec((d, 2 * d), lambda i: (0, 0))],
        out_specs=[pl.BlockSpec((1, m, d), lambda i: (i, 0, 0)),
                   pl.BlockSpec((1, m, d), lambda i: (i, 0, 0))],
        out_shape=[jax.ShapeDtypeStruct((b, m, d), BF16)] * 2,
        compiler_params=_params(("arbitrary",)),
        name="memkv",
    )(mem, g, w_xkv)


def _mixer_in_kernel(x_ref, gmix_ref, win_ref, convw_ref, convb_ref, wg_ref, bg_ref, lam_ref, glru_ref,
                     cos_ref, sina_ref, sinb_ref,
                     ylru_ref, q_ref, k_ref, v_ref, xpad_ref, hcar_ref, ascan_ref, uscan_ref):
    c = ylru_ref.shape[-1]
    si = pl.program_id(1)

    @pl.when(si == 0)
    def _():
        xpad_ref[0:SUBLANES, :] = jnp.zeros((SUBLANES, c), F32)
        hcar_ref[...] = jnp.zeros_like(hcar_ref)

    ts = x_ref.shape[1]
    xn = _rms(x_ref[0], gmix_ref[...]).astype(BF16)
    proj = jnp.dot(xn, win_ref[...], preferred_element_type=F32)
    x_lru = proj[:, 0:c]
    g_lru = proj[:, c:2 * c]
    q = proj[:, 2 * c:3 * c]
    k = proj[:, 3 * c:4 * c]
    v = proj[:, 4 * c:5 * c]

    xpad_ref[SUBLANES:SUBLANES + ts, :] = x_lru
    cw = convw_ref[...]
    xc = convb_ref[...] + cw[3:4] * x_lru
    for j in range(1, CONV_WIDTH):
        xc = xc + cw[3 - j:4 - j] * xpad_ref[SUBLANES - j:SUBLANES - j + ts, :]
    xpad_ref[0:SUBLANES, :] = x_lru[ts - SUBLANES:ts, :]

    gates = jnp.dot(xc.astype(BF16), wg_ref[...], preferred_element_type=F32) + bg_ref[...]
    r = jax.nn.sigmoid(gates[:, 0:c])
    i = jax.nn.sigmoid(gates[:, c:2 * c])
    neg_lam = -lam_ref[...]
    softplus = jnp.maximum(neg_lam, 0.0) + jnp.log1p(jnp.exp(-jnp.abs(neg_lam)))
    log_a = -RG_C * r * softplus
    a = jnp.exp(log_a)
    u = jnp.sqrt(-jnp.tanh(log_a) * (a * a + 1.0)) * (i * xc)

    groups = ts // SUBLANES
    cols = c // LANES
    for j in range(cols):
        ascan_ref[j] = a[:, j * LANES:(j + 1) * LANES]
        uscan_ref[j] = u[:, j * LANES:(j + 1) * LANES]
    for r in range(SUBLANES):
        rows = pl.ds(r, groups, stride=SUBLANES)
        a_r = jnp.concatenate([ascan_ref[j, rows, :] for j in range(cols)], axis=1)
        u_r = jnp.concatenate([uscan_ref[j, rows, :] for j in range(cols)], axis=1)
        if r:
            u_r = a_r * u_acc + u_r
            a_r = a_r * a_acc
            for j in range(cols):
                ascan_ref[j, rows, :] = a_r[:, j * LANES:(j + 1) * LANES]
                uscan_ref[j, rows, :] = u_r[:, j * LANES:(j + 1) * LANES]
        a_acc, u_acc = a_r, u_r
    carry = hcar_ref[...]
    hs = []
    for gi in range(groups):
        blk = slice(gi * SUBLANES, (gi + 1) * SUBLANES)
        a_g = jnp.concatenate([ascan_ref[j, blk, :] for j in range(cols)], axis=1)
        u_g = jnp.concatenate([uscan_ref[j, blk, :] for j in range(cols)], axis=1)
        hs.append(a_g * carry + u_g)
        carry = a_acc[gi:gi + 1] * carry + u_acc[gi:gi + 1]
    h = jnp.concatenate(hs, axis=0)
    hcar_ref[...] = carry

    y = h * jax.nn.gelu(g_lru)
    ylru_ref[0] = _rms(y, glru_ref[...]).astype(BF16)

    wide = lambda t_ref: jnp.concatenate([t_ref[...]] * (c // LANES), axis=1)
    cos, sina, sinb = wide(cos_ref), wide(sina_ref), wide(sinb_ref)
    half = ROT_DIM // 2

    def rope(t):
        return t * cos + pltpu.roll(t, c - half, axis=1) * sina + pltpu.roll(t, half, axis=1) * sinb

    q_ref[0] = rope(q)
    k_ref[0] = rope(k)
    v_ref[0] = v.astype(BF16)


def _mixer_in(x, gmix, w_in, conv_w, conv_b, w_gates, b_gates, lam, glru, cos_t, sina_t, sinb_t, ts):
    b, s, d = x.shape
    c = conv_w.shape[-1]
    full = lambda shape: pl.BlockSpec(shape, lambda bi, si: (0,) * len(shape))
    tab = pl.BlockSpec((ts, LANES), lambda bi, si: (si, 0))
    seq = lambda: pl.BlockSpec((1, ts, c), lambda bi, si: (bi, si, 0))
    return pl.pallas_call(
        _mixer_in_kernel,
        grid=(b, s // ts),
        in_specs=[pl.BlockSpec((1, ts, d), lambda bi, si: (bi, si, 0)),
                  full((1, d)), full(w_in.shape), full(conv_w.shape), full((1, c)),
                  full(w_gates.shape), full((1, 2 * c)), full((1, c)), full((1, c)),
                  tab, tab, tab],
        out_specs=[seq(), seq(), seq(), seq()],
        out_shape=[jax.ShapeDtypeStruct((b, s, c), BF16),
                   jax.ShapeDtypeStruct((b, s, c), F32),
                   jax.ShapeDtypeStruct((b, s, c), F32),
                   jax.ShapeDtypeStruct((b, s, c), BF16)],
        scratch_shapes=[pltpu.VMEM((ts + SUBLANES, c), F32), pltpu.VMEM((1, c), F32),
                        pltpu.VMEM((c // LANES, ts, LANES), F32), pltpu.VMEM((c // LANES, ts, LANES), F32)],
        compiler_params=_params(("arbitrary", "arbitrary")),
        name="mixer_in",
    )(x, gmix, w_in, conv_w, conv_b, w_gates, b_gates, lam, glru, cos_t, sina_t, sinb_t)


def _moba_kernel(q_ref, k_ref, v_ref, kpat_ref, o_ref, qa_ref, ka_ref):
    for pair in range(q_ref.shape[2] // LANES):
        _moba_pair(q_ref, k_ref, v_ref, kpat_ref, o_ref, qa_ref, ka_ref, slice(pair * LANES, (pair + 1) * LANES))


def _moba_pair(q_ref, k_ref, v_ref, kpat_ref, o_ref, qa_ref, ka_ref, lanes):
    s_len = q_ref.shape[1]
    nb = s_len // MOBA_BLOCK
    n_pick = min(MOBA_TOPK, nb)
    q = q_ref[0, :, lanes]
    k = k_ref[0, :, lanes]
    lane = lax.broadcasted_iota(jnp.int32, (1, LANES), 1)
    kmean = jnp.concatenate(
        [jnp.sum(k[n * MOBA_BLOCK:(n + 1) * MOBA_BLOCK], axis=0, keepdims=True) for n in range(nb)],
        axis=0) * (1.0 / MOBA_BLOCK)

    blk_row = lax.broadcasted_iota(jnp.int32, (nb, s_len), 0)
    q_blk = lax.broadcasted_iota(jnp.int32, (nb, s_len), 1) // MOBA_BLOCK
    past = blk_row < q_blk

    q_hi = q.astype(BF16)
    q_lo = (q - q_hi.astype(F32)).astype(BF16)
    for h in range(2):
        own = (lane >= h * HEAD_DIM) & (lane < (h + 1) * HEAD_DIM)
        off = (1 - h) * HEAD_DIM
        km = jnp.where(own, kmean, 0.0)
        km_hi = km.astype(BF16)
        km_split = jnp.concatenate([km_hi, (km - km_hi.astype(F32)).astype(BF16)], axis=0)
        parts = (lax.dot_general(km_split, q_hi, NT_DIMS, preferred_element_type=F32)
                 + lax.dot_general(km_split, q_lo, NT_DIMS, preferred_element_type=F32))
        gate = jnp.where(past, parts[0:nb] + parts[nb:2 * nb], -jnp.inf)
        rank = jnp.zeros((nb, s_len), jnp.int32)
        for m in range(nb):
            gm = gate[m:m + 1, :]
            ahead = (gm > gate) | ((gm == gate) & (m < blk_row))
            rank = rank + ahead.astype(jnp.int32)
        allowed = ((rank < n_pick) & past) | (blk_row == q_blk)
        bias = jnp.where(allowed, 0.0, NEG_BIG)
        pieces = []
        if off:
            pieces.append(jnp.zeros((off, s_len), F32))
        pieces.append(bias)
        if LANES - off - nb:
            pieces.append(jnp.zeros((LANES - off - nb, s_len), F32))
        bias_lanes = jnp.concatenate(pieces, axis=0).T
        qa_ref[h] = jnp.where(own, q * (HEAD_DIM ** -0.5 * LOG2_E), bias_lanes).astype(BF16)
        ka_ref[h] = jnp.where(own, k.astype(BF16), kpat_ref[h])

    own0 = lane < HEAD_DIM

    parts = MOBA_BLOCK // MOBA_QROWS

    def q_rows(qi, part):
        return slice(qi * MOBA_BLOCK + part * MOBA_QROWS, qi * MOBA_BLOCK + (part + 1) * MOBA_QROWS)

    def n_keys(qi, part):
        return qi * MOBA_BLOCK + (part + 1) * MOBA_QROWS

    def scores(qi, part, h):
        return lax.dot_general(qa_ref[h, q_rows(qi, part), :], ka_ref[h, 0:n_keys(qi, part), :], NT_DIMS,
                               preferred_element_type=F32)

    units = [(qi, part, h) for qi in range(nb) for part in range(parts) for h in range(2)]
    ahead = [scores(*u) for u in units[:MOBA_LOOKAHEAD]]
    outs = []

    def weighted_values(p, l, qi, part, h):
        outs.append(jnp.dot(p, v_ref[0, 0:n_keys(qi, part), lanes], preferred_element_type=F32) / l)
        if h:
            o_ref[0, q_rows(qi, part), lanes] = jnp.where(own0, outs[0], outs[1])
            outs.clear()

    pending = None
    for n, (qi, part, h) in enumerate(units):
        s = ahead.pop(0)
        if n + MOBA_LOOKAHEAD < len(units):
            ahead.append(scores(*units[n + MOBA_LOOKAHEAD]))
        n_past = qi * MOBA_BLOCK
        own_keys = (part + 1) * MOBA_QROWS
        key_in_blk = lax.broadcasted_iota(jnp.int32, (MOBA_QROWS, own_keys), 1)
        q_in_blk = lax.broadcasted_iota(jnp.int32, (MOBA_QROWS, own_keys), 0) + part * MOBA_QROWS
        s_own = jnp.where(key_in_blk <= q_in_blk, s[:, n_past:], NEG_BIG)
        s = jnp.concatenate([s[:, :n_past], s_own], axis=1) if qi else s_own
        p = jnp.exp2(s - jnp.max(s, axis=-1, keepdims=True))
        l = jnp.sum(p, axis=-1, keepdims=True)
        if pending is not None:
            weighted_values(*pending)
        pending = (p.astype(BF16), l, qi, part, h)
    weighted_values(*pending)


def _key_block_pattern(s):
    pat = np.zeros((2, s, LANES), np.float32)
    for h in range(2):
        off = (1 - h) * HEAD_DIM
        pat[h, np.arange(s), off + np.arange(s) // MOBA_BLOCK] = 1.0
    return jnp.asarray(pat, BF16)


def _moba(q, k, v):
    b, s, c = q.shape
    width = MOBA_PAIRS_PER_STEP * LANES
    assert c % width == 0
    spec = lambda: pl.BlockSpec((1, s, width), lambda bi, hi: (bi, 0, hi))
    return pl.pallas_call(
        _moba_kernel,
        grid=(b, c // width),
        in_specs=[spec(), spec(), spec(), pl.BlockSpec((2, s, LANES), lambda bi, hi: (0, 0, 0))],
        out_specs=spec(),
        out_shape=jax.ShapeDtypeStruct((b, s, c), F32),
        scratch_shapes=[pltpu.VMEM((2, s, LANES), BF16), pltpu.VMEM((2, s, LANES), BF16)],
        compiler_params=_params(("arbitrary", "arbitrary")),
        name="moba",
    )(q, k, v, _key_block_pattern(s))


def _post_kernel(x_ref, ylru_ref, yatt_ref, gatt_ref, woa_ref, wob_ref, gcross_ref, wxq_ref, kx_ref, vx_ref,
                 wxo_ref, gffn_ref, wr_ref, br_ref, tri_ref, elow_ref,
                 h2_ref, hn2_ref, gcol_ref, cnt_ref, pos_ref, tcnt_ref, carry_ref):
    tm = x_ref.shape[1]
    d = x_ref.shape[2]
    xd = d // N_XHEADS
    first = (pl.program_id(0) == 0) & (pl.program_id(1) == 0)

    @pl.when(first)
    def _():
        carry_ref[...] = jnp.zeros_like(carry_ref)

    ya = _rms(yatt_ref[0], gatt_ref[...]).astype(BF16)
    mix = (jnp.dot(ylru_ref[0], woa_ref[...], preferred_element_type=F32)
           + jnp.dot(ya, wob_ref[...], preferred_element_type=F32))
    h1 = x_ref[0] + mix

    hn = _rms(h1, gcross_ref[...]).astype(BF16)
    qx = jnp.dot(hn, wxq_ref[...], preferred_element_type=F32)
    heads = []
    for hh in range(N_XHEADS):
        qh = (qx[:, hh * xd:(hh + 1) * xd] * (xd ** -0.5)).astype(BF16)
        s = lax.dot_general(qh, kx_ref[0, :, hh * xd:(hh + 1) * xd], NT_DIMS, preferred_element_type=F32)
        p = jnp.exp(s - jnp.max(s, axis=-1, keepdims=True))
        l = jnp.sum(p, axis=-1, keepdims=True)
        o = jnp.dot(p.astype(BF16), vx_ref[0, :, hh * xd:(hh + 1) * xd], preferred_element_type=F32) / l
        heads.append(o.astype(BF16))
    h2 = h1 + jnp.dot(jnp.concatenate(heads, axis=1), wxo_ref[...], preferred_element_type=F32)
    hn2 = _rms(h2, gffn_ref[...])
    h2_ref[...] = h2
    hn2_ref[...] = hn2.astype(BF16)

    x_hi = hn2.astype(BF16)
    x_lo = (hn2 - x_hi.astype(F32)).astype(BF16)
    by_hi = lax.dot_general(wr_ref[...], x_hi, NT_DIMS, preferred_element_type=F32)
    by_lo = lax.dot_general(wr_ref[0:ROUTER_ROWS], x_lo, NT_DIMS, preferred_element_type=F32)
    logits = by_hi[0:ROUTER_ROWS] + by_hi[ROUTER_ROWS:2 * ROUTER_ROWS] + by_lo + br_ref[...]
    gl = logits[0:N_GROUPS]
    gmax = jnp.max(gl, axis=0, keepdims=True)
    gi = lax.broadcasted_iota(jnp.int32, gl.shape, 0)
    grp = jnp.min(jnp.where(gl == gmax, gi, N_GROUPS), axis=0, keepdims=True)
    g_w = 1.0 / jnp.sum(jnp.exp(gl - gmax), axis=0, keepdims=True)
    el = jnp.zeros((EXPERTS_PER_GROUP, tm), F32)
    for g in range(N_GROUPS):
        lo = SUBLANES + g * EXPERTS_PER_GROUP
        el = jnp.where(grp == g, logits[lo:lo + EXPERTS_PER_GROUP], el)
    ee = jnp.exp(el - jnp.max(el, axis=0, keepdims=True))
    ep = ee / jnp.sum(ee, axis=0, keepdims=True)
    ei = lax.broadcasted_iota(jnp.int32, ep.shape, 0)
    p1 = jnp.max(ep, axis=0, keepdims=True)
    i1 = jnp.min(jnp.where(ep == p1, ei, EXPERTS_PER_GROUP), axis=0, keepdims=True)
    ep_rest = jnp.where(ei == i1, -1.0, ep)
    p2 = jnp.max(ep_rest, axis=0, keepdims=True)
    i2 = jnp.min(jnp.where(ep_rest == p2, ei, EXPERTS_PER_GROUP), axis=0, keepdims=True)
    den = p1 + p2
    gate1 = g_w * p1 / den
    gate2 = g_w * p2 / den
    e1 = grp * EXPERTS_PER_GROUP + i1
    e2 = grp * EXPERTS_PER_GROUP + i2
    li = lax.broadcasted_iota(jnp.int32, (LANES, tm), 0)
    gcol_ref[...] = jnp.where(li == 0, gate1, jnp.where(li == 1, gate2, 0.0)).T

    xi = lax.broadcasted_iota(jnp.int32, (N_EXPERTS, tm), 0)
    oh1 = xi == e1
    oh2 = xi == e2
    cnt = oh1.astype(F32) + oh2.astype(F32)
    tile_cnt = jnp.broadcast_to(jnp.sum(cnt, axis=1, keepdims=True), (N_EXPERTS, LANES))
    carry_ref[...] = carry_ref[...] + tile_cnt
    cnt_ref[...] = carry_ref[...]
    tcnt_ref[...] = tile_cnt
    in_tile = jnp.dot(cnt.astype(BF16), tri_ref[...], preferred_element_type=F32)
    tile_start = jnp.dot(elow_ref[...], tile_cnt, precision=lax.Precision.HIGHEST,
                         preferred_element_type=F32)[:, 0:1]
    local = in_tile + tile_start
    pos_ref[0:1, :] = jnp.sum(jnp.where(oh1, local, 0.0), axis=0, keepdims=True).astype(jnp.int32)
    pos_ref[1:2, :] = jnp.sum(jnp.where(oh2, local, 0.0), axis=0, keepdims=True).astype(jnp.int32)


def _post(x, ylru, yatt, gatt, wo_a, wo_b, gcross, wxq, kx, vx, wxo, gffn, wr_t, br_t, tri, tm):
    b, s, d = x.shape
    c = ylru.shape[-1]
    m = kx.shape[1]
    n = b * s
    nt = s // tm
    full = lambda shape: pl.BlockSpec(shape, lambda bi, si: (0,) * len(shape))
    tok = lambda rows: pl.BlockSpec((rows, tm), lambda bi, si: (0, bi * nt + si))
    return pl.pallas_call(
        _post_kernel,
        grid=(b, nt),
        in_specs=[pl.BlockSpec((1, tm, d), lambda bi, si: (bi, si, 0)),
                  pl.BlockSpec((1, tm, c), lambda bi, si: (bi, si, 0)),
                  pl.BlockSpec((1, tm, c), lambda bi, si: (bi, si, 0)),
                  full((1, c)), full((c, d)), full((c, d)), full((1, d)), full((d, d)),
                  pl.BlockSpec((1, m, d), lambda bi, si: (bi, 0, 0)),
                  pl.BlockSpec((1, m, d), lambda bi, si: (bi, 0, 0)),
                  full((d, d)), full((1, d)), full(wr_t.shape), full(br_t.shape), full((tm, tm)),
                  full((N_EXPERTS, N_EXPERTS))],
        out_specs=[pl.BlockSpec((tm, d), lambda bi, si: (bi * nt + si, 0)),
                   pl.BlockSpec((tm, d), lambda bi, si: (bi * nt + si, 0)),
                   pl.BlockSpec((tm, LANES), lambda bi, si: (bi * nt + si, 0)),
                   full((N_EXPERTS, LANES)),
                   tok(EXPERT_TOPK),
                   pl.BlockSpec((N_EXPERTS, LANES), lambda bi, si: (bi * nt + si, 0))],
        out_shape=[jax.ShapeDtypeStruct((n, d), F32),
                   jax.ShapeDtypeStruct((n, d), BF16),
                   jax.ShapeDtypeStruct((n, LANES), F32),
                   jax.ShapeDtypeStruct((N_EXPERTS, LANES), F32),
                   jax.ShapeDtypeStruct((EXPERT_TOPK, n), jnp.int32),
                   jax.ShapeDtypeStruct((n // tm * N_EXPERTS, LANES), F32)],
        scratch_shapes=[pltpu.VMEM((N_EXPERTS, LANES), F32)],
        compiler_params=_params(("arbitrary", "arbitrary")),
        name="post",
    )(x, ylru, yatt, gatt, wo_a, wo_b, gcross, wxq, kx, vx, wxo, gffn, wr_t, br_t, tri,
      jnp.asarray(np.tril(np.ones((N_EXPERTS, N_EXPERTS), np.float32), -1)))


def _pieces(largest):
    piece = largest
    while piece:
        yield piece
        piece //= 2


def _start_run_copies(length, largest, copy_of):
    def start(pieces):
        for piece in pieces:
            done = length & ~(2 * piece - 1)

            @pl.when((length & piece) != 0)
            def _():
                copy_of(done, piece).start()

    @pl.when(length >= RUN_SPLIT)
    def _():
        start([p for p in _pieces(largest) if p >= RUN_SPLIT])

    start([p for p in _pieces(largest) if p < RUN_SPLIT])


def _tile_rows(row, count):
    return pl.ds(pl.multiple_of(row * SUBLANES, SUBLANES), count * SUBLANES)


def _dispatch_kernel(fill_ref, rstart_ref, rlen_ref, hn_ref, pos_ref, xout_ref, sorted_ref, zero_ref, sem, zsem):
    tm = hn_ref.shape[0]
    n_sorted = EXPERT_TOPK * tm
    step = pl.program_id(0)

    zero_rows = zero_ref.shape[0] // SUBLANES
    n_rows = xout_ref.shape[0] // SUBLANES

    def pad_copies(e, start):
        first = fill_ref[e]
        length = fill_ref[N_EXPERTS + e]
        for piece in _pieces(zero_rows):
            offset = first + (length & ~(2 * piece - 1))

            @pl.when((length & piece) != 0)
            def _():
                cp = pltpu.make_async_copy(zero_ref.at[_tile_rows(0, piece)], xout_ref.at[_tile_rows(offset, piece)],
                                           zsem)
                cp.start() if start else cp.wait()

    def tail_copies(start):
        def one(i, carry):
            cp = pltpu.make_async_copy(zero_ref, xout_ref.at[_tile_rows(i * zero_rows, zero_rows)], zsem)
            cp.start() if start else cp.wait()
            return carry

        lax.fori_loop(fill_ref[2 * N_EXPERTS] // zero_rows, n_rows // zero_rows, one, 0)

    @pl.when(step == 0)
    def _():
        zero_ref[...] = jnp.zeros_like(zero_ref)
        lax.fori_loop(0, N_EXPERTS, lambda e, c: (pad_copies(e, True), c)[1], 0)
        tail_copies(True)

    pos = pos_ref[...]
    p_iota = lax.broadcasted_iota(jnp.int32, (n_sorted, tm), 0)
    picks = jnp.where((p_iota == pos[0:1, :]) | (p_iota == pos[1:2, :]), 1.0, 0.0).astype(BF16)
    xs = jnp.dot(picks, hn_ref[...], preferred_element_type=F32)
    for k in range(SUBLANES):
        sorted_ref[pl.ds(k, n_sorted, stride=SUBLANES), :] = xs[:, k * LANES:(k + 1) * LANES]

    def run(e, offset):
        length = rlen_ref[step * N_EXPERTS + e]
        first = rstart_ref[step * N_EXPERTS + e]

        _start_run_copies(length, tm, lambda done, piece: pltpu.make_async_copy(
            sorted_ref.at[_tile_rows(offset + done, piece)], xout_ref.at[_tile_rows(first + done, piece)], sem))
        return offset + length

    lax.fori_loop(0, N_EXPERTS, run, 0)

    pltpu.make_async_copy(sorted_ref, xout_ref.at[_tile_rows(0, n_sorted)], sem).wait()

    @pl.when(step == 0)
    def _():
        lax.fori_loop(0, N_EXPERTS, lambda e, c: (pad_copies(e, False), c)[1], 0)
        tail_copies(False)


def _dispatch(fill, run_start, run_len, hn2, pos, n_rows, tm):
    n, d = hn2.shape
    assert d == SUBLANES * LANES, "a row must be exactly one (8, 128) tile for the tile-contiguous layout"
    return pl.pallas_call(
        _dispatch_kernel,
        grid_spec=pltpu.PrefetchScalarGridSpec(
            num_scalar_prefetch=3,
            grid=(n // tm,),
            in_specs=[pl.BlockSpec((tm, d), lambda i, *_: (i, 0)),
                      pl.BlockSpec((EXPERT_TOPK, tm), lambda i, *_: (0, i))],
            out_specs=pl.BlockSpec(memory_space=pl.ANY),
            scratch_shapes=[pltpu.VMEM((EXPERT_TOPK * tm * SUBLANES, LANES), F32),
                            pltpu.VMEM((EXPERT_ROWS // 2 * SUBLANES, LANES), F32),
                            pltpu.SemaphoreType.DMA(()), pltpu.SemaphoreType.DMA(())]),
        out_shape=jax.ShapeDtypeStruct((n_rows * SUBLANES, LANES), F32),
        compiler_params=_params(("arbitrary",)),
        name="dispatch",
    )(fill, run_start, run_len, hn2, pos)


def _experts_kernel(be_ref, nu_ref, x_ref, wgu_hbm, wd_hbm, y_ref, wgu_f32_ref, wd_f32_ref, wgu_bf_ref, wd_bf_ref,
                    sems):
    rows = EXPERT_ROWS
    lines = rows * SUBLANES
    f = wd_bf_ref.shape[0]
    n_used = nu_ref[0]

    def fetch(e):
        return (pltpu.make_async_copy(wgu_hbm.at[e], wgu_f32_ref, sems.at[0]),
                pltpu.make_async_copy(wd_hbm.at[e], wd_f32_ref, sems.at[1]))

    def block(j, base):
        used = j < n_used

        @pl.when(used & (j == 0))
        def _():
            for cp in fetch(be_ref[0]):
                cp.start()

        @pl.when(used & ((j == 0) | (be_ref[j] != be_ref[jnp.maximum(j, 1) - 1])))
        def _():
            for cp in fetch(be_ref[j]):
                cp.wait()
            wgu_bf_ref[...] = wgu_f32_ref[...].astype(BF16)
            wd_bf_ref[...] = wd_f32_ref[...].astype(BF16)
            nxt = lax.while_loop(lambda k: (k < n_used) & (be_ref[jnp.minimum(k, be_ref.shape[0] - 1)] == be_ref[j]),
                                 lambda k: k + 1, j + 1)

            @pl.when(nxt < n_used)
            def _():
                for cp in fetch(be_ref[jnp.minimum(nxt, be_ref.shape[0] - 1)]):
                    cp.start()

        @pl.when(used)
        def _():
            x = jnp.concatenate([x_ref[pl.ds(base + k, rows, stride=SUBLANES), :] for k in range(SUBLANES)], axis=1)
            gu = jnp.dot(x.astype(BF16), wgu_bf_ref[...], preferred_element_type=F32)
            act = jax.nn.silu(gu[:, 0:f]) * gu[:, f:2 * f]
            y = jnp.dot(act.astype(BF16), wd_bf_ref[...], preferred_element_type=F32)
            for k in range(SUBLANES):
                y_ref[pl.ds(base + k, rows, stride=SUBLANES), :] = y[:, k * LANES:(k + 1) * LANES]

        @pl.when(jnp.logical_not(used))
        def _():
            y_ref[base:base + lines, :] = jnp.zeros((lines, LANES), F32)

    per_step = x_ref.shape[0] // lines
    for sub in range(per_step):
        block(pl.program_id(0) * per_step + sub, sub * lines)


def _experts(blk_expert, n_used, x_buf, w_gate_up, w_down):
    d = w_gate_up.shape[1]
    f = w_down.shape[1]
    r = EXPERT_ROWS
    n_blk = x_buf.shape[0] // (r * SUBLANES)
    per_step = EXPERT_BLOCKS_PER_STEP
    assert d == SUBLANES * LANES and x_buf.shape[1] == LANES, "rows are (8, 128) tiles (tile-contiguous layout)"
    assert n_blk % per_step == 0
    last_step = lambda nu: (jnp.maximum(nu[0], 1) - 1) // per_step
    return pl.pallas_call(
        _experts_kernel,
        grid_spec=pltpu.PrefetchScalarGridSpec(
            num_scalar_prefetch=2,
            grid=(n_blk // per_step,),
            in_specs=[pl.BlockSpec((per_step * r * SUBLANES, LANES), lambda s, be, nu: (jnp.minimum(s, last_step(nu)), 0)),
                      pl.BlockSpec(memory_space=pl.ANY), pl.BlockSpec(memory_space=pl.ANY)],
            out_specs=pl.BlockSpec((per_step * r * SUBLANES, LANES), lambda s, be, nu: (s, 0)),
            scratch_shapes=[pltpu.VMEM((d, 2 * f), F32), pltpu.VMEM((f, d), F32),
                            pltpu.VMEM((d, 2 * f), BF16), pltpu.VMEM((f, d), BF16),
                            pltpu.SemaphoreType.DMA((2,))]),
        out_shape=jax.ShapeDtypeStruct(x_buf.shape, F32),
        compiler_params=_params(("arbitrary",)),
        name="experts",
    )(blk_expert, n_used, x_buf, w_gate_up, w_down)


def _combine_kernel(pos_ref, rstart_ref, rlen_ref, h2_ref, gcol_ref, gfin_ref, y_ref, o_ref,
                    ysort_ref, pick0_ref, pick1_ref, sems):
    tc = h2_ref.shape[0]
    n = pos_ref.shape[0] // EXPERT_TOPK
    n_sorted = EXPERT_TOPK * tc
    step = pl.program_id(0)
    par = step % 2
    base = step * tc

    def start_gather(tile, buf):
        def run(e, offset):
            length = rlen_ref[tile * N_EXPERTS + e]
            first = rstart_ref[tile * N_EXPERTS + e]
            _start_run_copies(length, tc, lambda done, piece: pltpu.make_async_copy(
                y_ref.at[_tile_rows(first + done, piece)], ysort_ref.at[buf, _tile_rows(offset + done, piece)],
                sems.at[buf]))
            return offset + length

        lax.fori_loop(0, N_EXPERTS, run, 0)

    @pl.when(step == 0)
    def _():
        start_gather(0, 0)

    @pl.when(step + 1 < pl.num_programs(0))
    def _():
        start_gather(step + 1, 1 - par)

    pltpu.make_async_copy(y_ref.at[_tile_rows(0, n_sorted)], ysort_ref.at[par], sems.at[par]).wait()
    for t in range(tc):
        for slot, pick_ref in enumerate((pick0_ref, pick1_ref)):
            row = pos_ref[slot * n + base + t]
            pick_ref[t * SUBLANES:(t + 1) * SUBLANES, :] = ysort_ref[par, _tile_rows(row, 1), :]

    def rows_of(y_ref):
        return jnp.concatenate([y_ref[pl.ds(k, tc, stride=SUBLANES), :] for k in range(SUBLANES)], axis=1)

    g = gcol_ref[...]
    moe = g[:, 0:1] * rows_of(pick0_ref) + g[:, 1:2] * rows_of(pick1_ref)
    o_ref[...] = _rms(h2_ref[...] + moe, gfin_ref[...])


def _combine(pos_flat, run_start, run_len, h2, gcol, gfin, y_buf, tc):
    n, d = h2.shape
    return pl.pallas_call(
        _combine_kernel,
        grid_spec=pltpu.PrefetchScalarGridSpec(
            num_scalar_prefetch=3,
            grid=(n // tc,),
            in_specs=[pl.BlockSpec((tc, d), lambda i, *_: (i, 0)),
                      pl.BlockSpec((tc, LANES), lambda i, *_: (i, 0)),
                      pl.BlockSpec((1, d), lambda i, *_: (0, 0)),
                      pl.BlockSpec(memory_space=pl.ANY)],
            out_specs=pl.BlockSpec((tc, d), lambda i, *_: (i, 0)),
            scratch_shapes=[pltpu.VMEM((2, EXPERT_TOPK * tc * SUBLANES, LANES), F32),
                            pltpu.VMEM((tc * SUBLANES, LANES), F32), pltpu.VMEM((tc * SUBLANES, LANES), F32),
                            pltpu.SemaphoreType.DMA((2,))]),
        out_shape=jax.ShapeDtypeStruct((n, d), F32),
        compiler_params=_params(("arbitrary",)),
        name="combine",
    )(pos_flat, run_start, run_len, h2, gcol, gfin, y_buf)


def _rope_tables(s):
    half = ROT_DIM // 2
    inv_freq = (ROPE_THETA ** (-np.arange(0, ROT_DIM, 2, dtype=np.float32) / ROT_DIM)).astype(np.float32)
    ang = np.arange(s, dtype=np.float32)[:, None] * inv_freq[None, :]
    cos, sin = np.cos(ang), np.sin(ang)
    zeros = lambda w: np.zeros((s, w), np.float32)
    cos_h = np.concatenate([cos, cos, np.ones((s, HEAD_DIM - ROT_DIM), np.float32)], axis=1)
    sina_h = np.concatenate([-sin, zeros(HEAD_DIM - half)], axis=1)
    sinb_h = np.concatenate([zeros(half), sin, zeros(HEAD_DIM - ROT_DIM)], axis=1)
    tile = lambda t: jnp.asarray(np.tile(t, (1, LANES // HEAD_DIM)), F32)
    return tile(cos_h), tile(sina_h), tile(sinb_h)


def _block_diag(w):
    nblk, bi, bo = w.shape
    eye = jnp.eye(nblk, dtype=w.dtype)
    return jnp.einsum('hij,hg->higj', w, eye).reshape(nblk * bi, nblk * bo)


def _layer(h, mem, norm_mix, w_in, conv_w, conv_b, w_rg, b_rg, w_ig, b_ig, lru_lambda, norm_lru_out,
           norm_attn_out, w_out, norm_cross, norm_mem, w_xq, w_xkv, w_xo, norm_ffn, w_router_group,
           b_router_group, w_router_expert, b_router_expert, w_gate_up, w_down, norm_out):
    b, s, d = h.shape
    c = conv_w.shape[-1]
    n = b * s
    row = lambda t: t.reshape(1, -1)
    tm = 512 if s % 512 == 0 else MOBA_BLOCK
    ts = tm
    assert s % ts == 0 and s % tm == 0 and c == N_HEADS * HEAD_DIM

    kx, vx = _memkv(mem, row(norm_mem), w_xkv.astype(BF16))

    w_gates = jnp.concatenate([_block_diag(w_rg), _block_diag(w_ig)], axis=1).astype(BF16)
    b_gates = jnp.concatenate([b_rg, b_ig]).reshape(1, -1)
    cos_t, sina_t, sinb_t = _rope_tables(s)
    ylru, q, k, v = _mixer_in(h, row(norm_mix), w_in.astype(BF16), conv_w, row(conv_b), w_gates, b_gates,
                              row(lru_lambda), row(norm_lru_out), cos_t, sina_t, sinb_t, ts)
    yatt = _moba(q, k, v)

    w_out_b = w_out.astype(BF16)
    wr_t = jnp.zeros((ROUTER_ROWS, d), F32)
    wr_t = wr_t.at[0:N_GROUPS].set(w_router_group.T).at[SUBLANES:SUBLANES + N_EXPERTS].set(w_router_expert.T)
    wr_hi = wr_t.astype(BF16)
    wr_t = jnp.concatenate([wr_hi, (wr_t - wr_hi.astype(F32)).astype(BF16)], axis=0)
    br_t = jnp.zeros((ROUTER_ROWS, 1), F32)
    br_t = br_t.at[0:N_GROUPS, 0].set(b_router_group).at[SUBLANES:SUBLANES + N_EXPERTS, 0].set(b_router_expert)
    tri = jnp.asarray(np.triu(np.ones((tm, tm), np.float32), 1), BF16)
    h2, hn2, gcol, counts, pos, tile_cnt = _post(
        h, ylru, yatt, row(norm_attn_out), w_out_b[:c], w_out_b[c:], row(norm_cross), w_xq.astype(BF16),
        kx, vx, w_xo.astype(BF16), row(norm_ffn), wr_t, br_t, tri, tm)

    counts = counts[:, 0].astype(jnp.int32)
    padded = (counts + EXPERT_ROWS - 1) // EXPERT_ROWS * EXPERT_ROWS
    pad_ends = jnp.cumsum(padded)
    pad_starts = pad_ends - padded
    n_blocks = n * EXPERT_TOPK // EXPERT_ROWS + N_EXPERTS
    blk_first = jnp.arange(n_blocks, dtype=jnp.int32) * EXPERT_ROWS
    blk_expert = jnp.minimum(jnp.sum(blk_first[:, None] >= pad_ends[None, :], axis=1), N_EXPERTS - 1).astype(jnp.int32)
    n_used = (pad_ends[-1:] // EXPERT_ROWS).astype(jnp.int32)
    fill = jnp.concatenate([pad_starts + counts, padded - counts, pad_ends[-1:]]).astype(jnp.int32)

    n_tiles = n // tm
    tile_cnt = tile_cnt[:, 0].astype(jnp.int32).reshape(n_tiles, N_EXPERTS)
    run_start = (pad_starts[None, :] + jnp.cumsum(tile_cnt, axis=0) - tile_cnt).astype(jnp.int32).reshape(-1)
    run_len = tile_cnt.reshape(-1)
    x_buf = _dispatch(fill, run_start, run_len, hn2, pos, n_blocks * EXPERT_ROWS, tm)
    y_buf = _experts(blk_expert, n_used, x_buf, w_gate_up, w_down)
    out = _combine(pos.reshape(-1), run_start, run_len, h2, gcol, row(norm_out), y_buf, tm)
    return out.reshape(b, s, d)


def kernel(x, mem, norm_mix, w_in, conv_w, conv_b, w_rg, b_rg, w_ig, b_ig, lru_lambda, norm_lru_out, norm_attn_out,
           w_out, norm_cross, norm_mem, w_xq, w_xkv, w_xo, norm_ffn, w_router_group, b_router_group,
           w_router_expert, b_router_expert, w_gate_up, w_down, norm_final):
    depth = norm_mix.shape[0]
    assert depth == 1, "the fused final norm assumes a single layer"
    l = 0
    return _layer(x, mem, norm_mix[l], w_in[l], conv_w[l], conv_b[l], w_rg[l], b_rg[l], w_ig[l], b_ig[l],
                  lru_lambda[l], norm_lru_out[l], norm_attn_out[l], w_out[l], norm_cross[l], norm_mem[l],
                  w_xq[l], w_xkv[l], w_xo[l], norm_ffn[l], w_router_group[l], b_router_group[l],
                  w_router_expert[l], b_router_expert[l], w_gate_up[l], w_down[l], norm_final)
```

```python
import jax
import jax.numpy as jnp
import numpy as np
from jax import lax
from jax.experimental import pallas as pl
from jax.experimental.pallas import tpu as pltpu

F32 = jnp.float32
BF16 = jnp.bfloat16

CONV_WIDTH = 4
RG_C = 8.0
N_HEADS = 8
HEAD_DIM = 64
ROT_DIM = HEAD_DIM // 4
ROPE_THETA = 500000.0
MOBA_BLOCK = 256
MOBA_TOPK = 3
N_XHEADS = 4
N_GROUPS = 4
EXPERTS_PER_GROUP = 8
N_EXPERTS = N_GROUPS * EXPERTS_PER_GROUP
EXPERT_TOPK = 2
EPS = 1e-6

LANES = 128
SUBLANES = 8
NEG_BIG = -1e30
LOG2_E = 1.4426950408889634
MOBA_QROWS = 128
MOBA_LOOKAHEAD = 5
EXPERT_ROWS = 512
EXPERT_BLOCKS_PER_STEP = 4
RUN_SPLIT = 64
BF16_ROWS = 2 * SUBLANES
ROUTER_ROWS = -(-(SUBLANES + N_EXPERTS) // BF16_ROWS) * BF16_ROWS
VMEM_LIMIT = 56 * 1024 * 1024

NT_DIMS = (((1,), (1,)), ((), ()))


def _rms(x, g):
    return x * lax.rsqrt(jnp.mean(x * x, axis=-1, keepdims=True) + EPS) * g


def _params(sem):
    return pltpu.CompilerParams(dimension_semantics=sem, vmem_limit_bytes=VMEM_LIMIT)


def _memkv_kernel(mem_ref, g_ref, w_ref, k_ref, v_ref):
    d = mem_ref.shape[-1]
    mn = _rms(mem_ref[0], g_ref[...]).astype(BF16)
    kv = jnp.dot(mn, w_ref[...], preferred_element_type=F32)
    k_ref[0] = kv[:, :d].astype(BF16)
    v_ref[0] = kv[:, d:].astype(BF16)


def _memkv(mem, g, w_xkv):
    b, m, d = mem.shape
    return pl.pallas_call(
        _memkv_kernel,
        grid=(b,),
        in_specs=[pl.BlockSpec((1, m, d), lambda i: (i, 0, 0)),
                  pl.BlockSpec((1, d), lambda i: (0, 0)),
                  pl.BlockSpec((d, 2 * d), lambda i: (0, 0))],
        out_specs=[pl.BlockSpec((1, m, d), lambda i: (i, 0, 0)),
                   pl.BlockSpec((1, m, d), lambda i: (i, 0, 0))],
        out_shape=[jax.ShapeDtypeStruct((b, m, d), BF16)] * 2,
        compiler_params=_params(("arbitrary",)),
        name="memkv",
    )(mem, g, w_xkv)


def _mixer_in_kernel(x_ref, gmix_ref, win_ref, convw_ref, convb_ref, wg_ref, bg_ref, lam_ref, glru_ref,
                     cos_ref, sina_ref, sinb_ref,
                     ylru_ref, q_ref, k_ref, v_ref, xpad_ref, hcar_ref, ascan_ref, uscan_ref):
    c = ylru_ref.shape[-1]
    si = pl.program_id(1)

    @pl.when(si == 0)
    def _():
        xpad_ref[0:SUBLANES, :] = jnp.zeros((SUBLANES, c), F32)
        hcar_ref[...] = jnp.zeros_like(hcar_ref)

    ts = x_ref.shape[1]
    xn = _rms(x_ref[0], gmix_ref[...]).astype(BF16)
    proj = jnp.dot(xn, win_ref[...], preferred_element_type=F32)
    x_lru = proj[:, 0:c]
    g_lru = proj[:, c:2 * c]
    q = proj[:, 2 * c:3 * c]
    k = proj[:, 3 * c:4 * c]
    v = proj[:, 4 * c:5 * c]

    xpad_ref[SUBLANES:SUBLANES + ts, :] = x_lru
    cw = convw_ref[...]
    xc = convb_ref[...] + cw[3:4] * x_lru
    for j in range(1, CONV_WIDTH):
        xc = xc + cw[3 - j:4 - j] * xpad_ref[SUBLANES - j:SUBLANES - j + ts, :]
    xpad_ref[0:SUBLANES, :] = x_lru[ts - SUBLANES:ts, :]

    gates = jnp.dot(xc.astype(BF16), wg_ref[...], preferred_element_type=F32) + bg_ref[...]
    r = jax.nn.sigmoid(gates[:, 0:c])
    i = jax.nn.sigmoid(gates[:, c:2 * c])
    neg_lam = -lam_ref[...]
    softplus = jnp.maximum(neg_lam, 0.0) + jnp.log1p(jnp.exp(-jnp.abs(neg_lam)))
    log_a = -RG_C * r * softplus
    a = jnp.exp(log_a)
    u = jnp.sqrt(-jnp.tanh(log_a) * (a * a + 1.0)) * (i * xc)

    groups = ts // SUBLANES
    cols = c // LANES
    for j in range(cols):
        ascan_ref[j] = a[:, j * LANES:(j + 1) * LANES]
        uscan_ref[j] = u[:, j * LANES:(j + 1) * LANES]
    for r in range(SUBLANES):
        rows = pl.ds(r, groups, stride=SUBLANES)
        a_r = jnp.concatenate([ascan_ref[j, rows, :] for j in range(cols)], axis=1)
        u_r = jnp.concatenate([uscan_ref[j, rows, :] for j in range(cols)], axis=1)
        if r:
            u_r = a_r * u_acc + u_r
            a_r = a_r * a_acc
            for j in range(cols):
                ascan_ref[j, rows, :] = a_r[:, j * LANES:(j + 1) * LANES]
                uscan_ref[j, rows, :] = u_r[:, j * LANES:(j + 1) * LANES]
        a_acc, u_acc = a_r, u_r
    carry = hcar_ref[...]
    hs = []
    for gi in range(groups):
        blk = slice(gi * SUBLANES, (gi + 1) * SUBLANES)
        a_g = jnp.concatenate([ascan_ref[j, blk, :] for j in range(cols)], axis=1)
        u_g = jnp.concatenate([uscan_ref[j, blk, :] for j in range(cols)], axis=1)
        hs.append(a_g * carry + u_g)
        carry = a_acc[gi:gi + 1] * carry + u_acc[gi:gi + 1]
    h = jnp.concatenate(hs, axis=0)
    hcar_ref[...] = carry

    y = h * jax.nn.gelu(g_lru)
    ylru_ref[0] = _rms(y, glru_ref[...]).astype(BF16)

    wide = lambda t_ref: jnp.concatenate([t_ref[...]] * (c // LANES), axis=1)
    cos, sina, sinb = wide(cos_ref), wide(sina_ref), wide(sinb_ref)
    half = ROT_DIM // 2

    def rope(t):
        return t * cos + pltpu.roll(t, c - half, axis=1) * sina + pltpu.roll(t, half, axis=1) * sinb

    q_ref[0] = rope(q)
    k_ref[0] = rope(k)
    v_ref[0] = v.astype(BF16)


def _mixer_in(x, gmix, w_in, conv_w, conv_b, w_gates, b_gates, lam, glru, cos_t, sina_t, sinb_t, ts):
    b, s, d = x.shape
    c = conv_w.shape[-1]
    full = lambda shape: pl.BlockSpec(shape, lambda bi, si: (0,) * len(shape))
    tab = pl.BlockSpec((ts, LANES), lambda bi, si: (si, 0))
    seq = lambda: pl.BlockSpec((1, ts, c), lambda bi, si: (bi, si, 0))
    return pl.pallas_call(
        _mixer_in_kernel,
        grid=(b, s // ts),
        in_specs=[pl.BlockSpec((1, ts, d), lambda bi, si: (bi, si, 0)),
                  full((1, d)), full(w_in.shape), full(conv_w.shape), full((1, c)),
                  full(w_gates.shape), full((1, 2 * c)), full((1, c)), full((1, c)),
                  tab, tab, tab],
        out_specs=[seq(), seq(), seq(), seq()],
        out_shape=[jax.ShapeDtypeStruct((b, s, c), BF16),
                   jax.ShapeDtypeStruct((b, s, c), F32),
                   jax.ShapeDtypeStruct((b, s, c), F32),
                   jax.ShapeDtypeStruct((b, s, c), BF16)],
        scratch_shapes=[pltpu.VMEM((ts + SUBLANES, c), F32), pltpu.VMEM((1, c), F32),
                        pltpu.VMEM((c // LANES, ts, LANES), F32), pltpu.VMEM((c // LANES, ts, LANES), F32)],
        compiler_params=_params(("arbitrary", "arbitrary")),
        name="mixer_in",
    )(x, gmix, w_in, conv_w, conv_b, w_gates, b_gates, lam, glru, cos_t, sina_t, sinb_t)


def _moba_kernel(q_ref, k_ref, v_ref, kpat_ref, o_ref, qa_ref, ka_ref):
    s_len = q_ref.shape[1]
    nb = s_len // MOBA_BLOCK
    n_pick = min(MOBA_TOPK, nb)
    q = q_ref[0]
    k = k_ref[0]
    lane = lax.broadcasted_iota(jnp.int32, (1, LANES), 1)
    kmean = jnp.concatenate(
        [jnp.sum(k[n * MOBA_BLOCK:(n + 1) * MOBA_BLOCK], axis=0, keepdims=True) for n in range(nb)],
        axis=0) * (1.0 / MOBA_BLOCK)

    blk_row = lax.broadcasted_iota(jnp.int32, (nb, s_len), 0)
    q_blk = lax.broadcasted_iota(jnp.int32, (nb, s_len), 1) // MOBA_BLOCK
    past = blk_row < q_blk

    q_hi = q.astype(BF16)
    q_lo = (q - q_hi.astype(F32)).astype(BF16)
    for h in range(2):
        own = (lane >= h * HEAD_DIM) & (lane < (h + 1) * HEAD_DIM)
        off = (1 - h) * HEAD_DIM
        km = jnp.where(own, kmean, 0.0)
        km_hi = km.astype(BF16)
        km_split = jnp.concatenate([km_hi, (km - km_hi.astype(F32)).astype(BF16)], axis=0)
        parts = (lax.dot_general(km_split, q_hi, NT_DIMS, preferred_element_type=F32)
                 + lax.dot_general(km_split, q_lo, NT_DIMS, preferred_element_type=F32))
        gate = jnp.where(past, parts[0:nb] + parts[nb:2 * nb], -jnp.inf)
        rank = jnp.zeros((nb, s_len), jnp.int32)
        for m in range(nb):
            gm = gate[m:m + 1, :]
            ahead = (gm > gate) | ((gm == gate) & (m < blk_row))
            rank = rank + ahead.astype(jnp.int32)
        allowed = ((rank < n_pick) & past) | (blk_row == q_blk)
        bias = jnp.where(allowed, 0.0, NEG_BIG)
        pieces = []
        if off:
            pieces.append(jnp.zeros((off, s_len), F32))
        pieces.append(bias)
        if LANES - off - nb:
            pieces.append(jnp.zeros((LANES - off - nb, s_len), F32))
        bias_lanes = jnp.concatenate(pieces, axis=0).T
        qa_ref[h] = jnp.where(own, q * (HEAD_DIM ** -0.5 * LOG2_E), bias_lanes).astype(BF16)
        ka_ref[h] = jnp.where(own, k.astype(BF16), kpat_ref[h])

    own0 = lane < HEAD_DIM

    parts = MOBA_BLOCK // MOBA_QROWS

    def q_rows(qi, part):
        return slice(qi * MOBA_BLOCK + part * MOBA_QROWS, qi * MOBA_BLOCK + (part + 1) * MOBA_QROWS)

    def n_keys(qi, part):
        return qi * MOBA_BLOCK + (part + 1) * MOBA_QROWS

    def scores(qi, part, h):
        return lax.dot_general(qa_ref[h, q_rows(qi, part), :], ka_ref[h, 0:n_keys(qi, part), :], NT_DIMS,
                               preferred_element_type=F32)

    units = [(qi, part, h) for qi in range(nb) for part in range(parts) for h in range(2)]
    ahead = [scores(*u) for u in units[:MOBA_LOOKAHEAD]]
    outs = []

    def weighted_values(p, l, qi, part, h):
        outs.append(jnp.dot(p, v_ref[0, 0:n_keys(qi, part), :], preferred_element_type=F32) / l)
        if h:
            o_ref[0, q_rows(qi, part), :] = jnp.where(own0, outs[0], outs[1])
            outs.clear()

    pending = None
    for n, (qi, part, h) in enumerate(units):
        s = ahead.pop(0)
        if n + MOBA_LOOKAHEAD < len(units):
            ahead.append(scores(*units[n + MOBA_LOOKAHEAD]))
        n_past = qi * MOBA_BLOCK
        own_keys = (part + 1) * MOBA_QROWS
        key_in_blk = lax.broadcasted_iota(jnp.int32, (MOBA_QROWS, own_keys), 1)
        q_in_blk = lax.broadcasted_iota(jnp.int32, (MOBA_QROWS, own_keys), 0) + part * MOBA_QROWS
        s_own = jnp.where(key_in_blk <= q_in_blk, s[:, n_past:], NEG_BIG)
        s = jnp.concatenate([s[:, :n_past], s_own], axis=1) if qi else s_own
        p = jnp.exp2(s - jnp.max(s, axis=-1, keepdims=True))
        l = jnp.sum(p, axis=-1, keepdims=True)
        if pending is not None:
            weighted_values(*pending)
        pending = (p.astype(BF16), l, qi, part, h)
    weighted_values(*pending)


def _key_block_pattern(s):
    pat = np.zeros((2, s, LANES), np.float32)
    for h in range(2):
        off = (1 - h) * HEAD_DIM
        pat[h, np.arange(s), off + np.arange(s) // MOBA_BLOCK] = 1.0
    return jnp.asarray(pat, BF16)


def _moba(q, k, v):
    b, s, c = q.shape
    spec = lambda: pl.BlockSpec((1, s, LANES), lambda bi, hi: (bi, 0, hi))
    return pl.pallas_call(
        _moba_kernel,
        grid=(b, c // LANES),
        in_specs=[spec(), spec(), spec(), pl.BlockSpec((2, s, LANES), lambda bi, hi: (0, 0, 0))],
        out_specs=spec(),
        out_shape=jax.ShapeDtypeStruct((b, s, c), F32),
        scratch_shapes=[pltpu.VMEM((2, s, LANES), BF16), pltpu.VMEM((2, s, LANES), BF16)],
        compiler_params=_params(("arbitrary", "arbitrary")),
        name="moba",
    )(q, k, v, _key_block_pattern(s))


def _post_kernel(x_ref, ylru_ref, yatt_ref, gatt_ref, woa_ref, wob_ref, gcross_ref, wxq_ref, kx_ref, vx_ref,
                 wxo_ref, gffn_ref, wr_ref, br_ref, tri_ref, elow_ref,
                 h2_ref, hn2_ref, gcol_ref, cnt_ref, pos_ref, tcnt_ref, carry_ref):
    tm = x_ref.shape[1]
    d = x_ref.shape[2]
    xd = d // N_XHEADS
    first = (pl.program_id(0) == 0) & (pl.program_id(1) == 0)

    @pl.when(first)
    def _():
        carry_ref[...] = jnp.zeros_like(carry_ref)

    ya = _rms(yatt_ref[0], gatt_ref[...]).astype(BF16)
    mix = (jnp.dot(ylru_ref[0], woa_ref[...], preferred_element_type=F32)
           + jnp.dot(ya, wob_ref[...], preferred_element_type=F32))
    h1 = x_ref[0] + mix

    hn = _rms(h1, gcross_ref[...]).astype(BF16)
    qx = jnp.dot(hn, wxq_ref[...], preferred_element_type=F32)
    heads = []
    for hh in range(N_XHEADS):
        qh = (qx[:, hh * xd:(hh + 1) * xd] * (xd ** -0.5)).astype(BF16)
        s = lax.dot_general(qh, kx_ref[0, :, hh * xd:(hh + 1) * xd], NT_DIMS, preferred_element_type=F32)
        p = jnp.exp(s - jnp.max(s, axis=-1, keepdims=True))
        l = jnp.sum(p, axis=-1, keepdims=True)
        o = jnp.dot(p.astype(BF16), vx_ref[0, :, hh * xd:(hh + 1) * xd], preferred_element_type=F32) / l
        heads.append(o.astype(BF16))
    h2 = h1 + jnp.dot(jnp.concatenate(heads, axis=1), wxo_ref[...], preferred_element_type=F32)
    hn2 = _rms(h2, gffn_ref[...])
    h2_ref[...] = h2
    hn2_ref[...] = hn2.astype(BF16)

    x_hi = hn2.astype(BF16)
    x_lo = (hn2 - x_hi.astype(F32)).astype(BF16)
    by_hi = lax.dot_general(wr_ref[...], x_hi, NT_DIMS, preferred_element_type=F32)
    by_lo = lax.dot_general(wr_ref[0:ROUTER_ROWS], x_lo, NT_DIMS, preferred_element_type=F32)
    logits = by_hi[0:ROUTER_ROWS] + by_hi[ROUTER_ROWS:2 * ROUTER_ROWS] + by_lo + br_ref[...]
    gl = logits[0:N_GROUPS]
    gmax = jnp.max(gl, axis=0, keepdims=True)
    gi = lax.broadcasted_iota(jnp.int32, gl.shape, 0)
    grp = jnp.min(jnp.where(gl == gmax, gi, N_GROUPS), axis=0, keepdims=True)
    g_w = 1.0 / jnp.sum(jnp.exp(gl - gmax), axis=0, keepdims=True)
    el = jnp.zeros((EXPERTS_PER_GROUP, tm), F32)
    for g in range(N_GROUPS):
        lo = SUBLANES + g * EXPERTS_PER_GROUP
        el = jnp.where(grp == g, logits[lo:lo + EXPERTS_PER_GROUP], el)
    ee = jnp.exp(el - jnp.max(el, axis=0, keepdims=True))
    ep = ee / jnp.sum(ee, axis=0, keepdims=True)
    ei = lax.broadcasted_iota(jnp.int32, ep.shape, 0)
    p1 = jnp.max(ep, axis=0, keepdims=True)
    i1 = jnp.min(jnp.where(ep == p1, ei, EXPERTS_PER_GROUP), axis=0, keepdims=True)
    ep_rest = jnp.where(ei == i1, -1.0, ep)
    p2 = jnp.max(ep_rest, axis=0, keepdims=True)
    i2 = jnp.min(jnp.where(ep_rest == p2, ei, EXPERTS_PER_GROUP), axis=0, keepdims=True)
    den = p1 + p2
    gate1 = g_w * p1 / den
    gate2 = g_w * p2 / den
    e1 = grp * EXPERTS_PER_GROUP + i1
    e2 = grp * EXPERTS_PER_GROUP + i2
    li = lax.broadcasted_iota(jnp.int32, (LANES, tm), 0)
    gcol_ref[...] = jnp.where(li == 0, gate1, jnp.where(li == 1, gate2, 0.0)).T

    xi = lax.broadcasted_iota(jnp.int32, (N_EXPERTS, tm), 0)
    oh1 = xi == e1
    oh2 = xi == e2
    cnt = oh1.astype(F32) + oh2.astype(F32)
    tile_cnt = jnp.broadcast_to(jnp.sum(cnt, axis=1, keepdims=True), (N_EXPERTS, LANES))
    carry_ref[...] = carry_ref[...] + tile_cnt
    cnt_ref[...] = carry_ref[...]
    tcnt_ref[...] = tile_cnt
    in_tile = jnp.dot(cnt.astype(BF16), tri_ref[...], preferred_element_type=F32)
    tile_start = jnp.dot(elow_ref[...], tile_cnt, precision=lax.Precision.HIGHEST,
                         preferred_element_type=F32)[:, 0:1]
    local = in_tile + tile_start
    pos_ref[0:1, :] = jnp.sum(jnp.where(oh1, local, 0.0), axis=0, keepdims=True).astype(jnp.int32)
    pos_ref[1:2, :] = jnp.sum(jnp.where(oh2, local, 0.0), axis=0, keepdims=True).astype(jnp.int32)


def _post(x, ylru, yatt, gatt, wo_a, wo_b, gcross, wxq, kx, vx, wxo, gffn, wr_t, br_t, tri, tm):
    b, s, d = x.shape
    c = ylru.shape[-1]
    m = kx.shape[1]
    n = b * s
    nt = s // tm
    full = lambda shape: pl.BlockSpec(shape, lambda bi, si: (0,) * len(shape))
    tok = lambda rows: pl.BlockSpec((rows, tm), lambda bi, si: (0, bi * nt + si))
    return pl.pallas_call(
        _post_kernel,
        grid=(b, nt),
        in_specs=[pl.BlockSpec((1, tm, d), lambda bi, si: (bi, si, 0)),
                  pl.BlockSpec((1, tm, c), lambda bi, si: (bi, si, 0)),
                  pl.BlockSpec((1, tm, c), lambda bi, si: (bi, si, 0)),
                  full((1, c)), full((c, d)), full((c, d)), full((1, d)), full((d, d)),
                  pl.BlockSpec((1, m, d), lambda bi, si: (bi, 0, 0)),
                  pl.BlockSpec((1, m, d), lambda bi, si: (bi, 0, 0)),
                  full((d, d)), full((1, d)), full(wr_t.shape), full(br_t.shape), full((tm, tm)),
                  full((N_EXPERTS, N_EXPERTS))],
        out_specs=[pl.BlockSpec((tm, d), lambda bi, si: (bi * nt + si, 0)),
                   pl.BlockSpec((tm, d), lambda bi, si: (bi * nt + si, 0)),
                   pl.BlockSpec((tm, LANES), lambda bi, si: (bi * nt + si, 0)),
                   full((N_EXPERTS, LANES)),
                   tok(EXPERT_TOPK),
                   pl.BlockSpec((N_EXPERTS, LANES), lambda bi, si: (bi * nt + si, 0))],
        out_shape=[jax.ShapeDtypeStruct((n, d), F32),
                   jax.ShapeDtypeStruct((n, d), BF16),
                   jax.ShapeDtypeStruct((n, LANES), F32),
                   jax.ShapeDtypeStruct((N_EXPERTS, LANES), F32),
                   jax.ShapeDtypeStruct((EXPERT_TOPK, n), jnp.int32),
                   jax.ShapeDtypeStruct((n // tm * N_EXPERTS, LANES), F32)],
        scratch_shapes=[pltpu.VMEM((N_EXPERTS, LANES), F32)],
        compiler_params=_params(("arbitrary", "arbitrary")),
        name="post",
    )(x, ylru, yatt, gatt, wo_a, wo_b, gcross, wxq, kx, vx, wxo, gffn, wr_t, br_t, tri,
      jnp.asarray(np.tril(np.ones((N_EXPERTS, N_EXPERTS), np.float32), -1)))


def _pieces(largest):
    piece = largest
    while piece:
        yield piece
        piece //= 2


def _start_run_copies(length, largest, copy_of):
    def start(pieces):
        for piece in pieces:
            done = length & ~(2 * piece - 1)

            @pl.when((length & piece) != 0)
            def _():
                copy_of(done, piece).start()

    @pl.when(length >= RUN_SPLIT)
    def _():
        start([p for p in _pieces(largest) if p >= RUN_SPLIT])

    start([p for p in _pieces(largest) if p < RUN_SPLIT])


def _tile_rows(row, count):
    return pl.ds(pl.multiple_of(row * SUBLANES, SUBLANES), count * SUBLANES)


def _dispatch_kernel(fill_ref, rstart_ref, rlen_ref, hn_ref, pos_ref, xout_ref, sorted_ref, zero_ref, sem, zsem):
    tm = hn_ref.shape[0]
    n_sorted = EXPERT_TOPK * tm
    step = pl.program_id(0)

    zero_rows = zero_ref.shape[0] // SUBLANES
    n_rows = xout_ref.shape[0] // SUBLANES

    def pad_copies(e, start):
        first = fill_ref[e]
        length = fill_ref[N_EXPERTS + e]
        for piece in _pieces(zero_rows):
            offset = first + (length & ~(2 * piece - 1))

            @pl.when((length & piece) != 0)
            def _():
                cp = pltpu.make_async_copy(zero_ref.at[_tile_rows(0, piece)], xout_ref.at[_tile_rows(offset, piece)],
                                           zsem)
                cp.start() if start else cp.wait()

    def tail_copies(start):
        def one(i, carry):
            cp = pltpu.make_async_copy(zero_ref, xout_ref.at[_tile_rows(i * zero_rows, zero_rows)], zsem)
            cp.start() if start else cp.wait()
            return carry

        lax.fori_loop(fill_ref[2 * N_EXPERTS] // zero_rows, n_rows // zero_rows, one, 0)

    @pl.when(step == 0)
    def _():
        zero_ref[...] = jnp.zeros_like(zero_ref)
        lax.fori_loop(0, N_EXPERTS, lambda e, c: (pad_copies(e, True), c)[1], 0)
        tail_copies(True)

    pos = pos_ref[...]
    p_iota = lax.broadcasted_iota(jnp.int32, (n_sorted, tm), 0)
    picks = jnp.where((p_iota == pos[0:1, :]) | (p_iota == pos[1:2, :]), 1.0, 0.0).astype(BF16)
    xs = jnp.dot(picks, hn_ref[...], preferred_element_type=F32)
    for k in range(SUBLANES):
        sorted_ref[pl.ds(k, n_sorted, stride=SUBLANES), :] = xs[:, k * LANES:(k + 1) * LANES]

    def run(e, offset):
        length = rlen_ref[step * N_EXPERTS + e]
        first = rstart_ref[step * N_EXPERTS + e]

        _start_run_copies(length, tm, lambda done, piece: pltpu.make_async_copy(
            sorted_ref.at[_tile_rows(offset + done, piece)], xout_ref.at[_tile_rows(first + done, piece)], sem))
        return offset + length

    lax.fori_loop(0, N_EXPERTS, run, 0)

    pltpu.make_async_copy(sorted_ref, xout_ref.at[_tile_rows(0, n_sorted)], sem).wait()

    @pl.when(step == pl.num_programs(0) - 1)
    def _():
        lax.fori_loop(0, N_EXPERTS, lambda e, c: (pad_copies(e, False), c)[1], 0)
        tail_copies(False)


def _dispatch(fill, run_start, run_len, hn2, pos, n_rows, tm):
    n, d = hn2.shape
    assert d == SUBLANES * LANES, "a row must be exactly one (8, 128) tile for the tile-contiguous layout"
    return pl.pallas_call(
        _dispatch_kernel,
        grid_spec=pltpu.PrefetchScalarGridSpec(
            num_scalar_prefetch=3,
            grid=(n // tm,),
            in_specs=[pl.BlockSpec((tm, d), lambda i, *_: (i, 0)),
                      pl.BlockSpec((EXPERT_TOPK, tm), lambda i, *_: (0, i))],
            out_specs=pl.BlockSpec(memory_space=pl.ANY),
            scratch_shapes=[pltpu.VMEM((EXPERT_TOPK * tm * SUBLANES, LANES), F32),
                            pltpu.VMEM((EXPERT_ROWS // 2 * SUBLANES, LANES), F32),
                            pltpu.SemaphoreType.DMA(()), pltpu.SemaphoreType.DMA(())]),
        out_shape=jax.ShapeDtypeStruct((n_rows * SUBLANES, LANES), F32),
        compiler_params=_params(("arbitrary",)),
        name="dispatch",
    )(fill, run_start, run_len, hn2, pos)


def _experts_kernel(be_ref, nu_ref, x_ref, wgu_hbm, wd_hbm, y_ref, wgu_f32_ref, wd_f32_ref, wgu_bf_ref, wd_bf_ref,
                    sems):
    rows = EXPERT_ROWS
    lines = rows * SUBLANES
    f = wd_bf_ref.shape[0]
    n_used = nu_ref[0]

    def fetch(e):
        return (pltpu.make_async_copy(wgu_hbm.at[e], wgu_f32_ref, sems.at[0]),
                pltpu.make_async_copy(wd_hbm.at[e], wd_f32_ref, sems.at[1]))

    def block(j, base):
        used = j < n_used

        @pl.when(used & (j == 0))
        def _():
            for cp in fetch(be_ref[0]):
                cp.start()

        @pl.when(used & ((j == 0) | (be_ref[j] != be_ref[jnp.maximum(j, 1) - 1])))
        def _():
            for cp in fetch(be_ref[j]):
                cp.wait()
            wgu_bf_ref[...] = wgu_f32_ref[...].astype(BF16)
            wd_bf_ref[...] = wd_f32_ref[...].astype(BF16)
            nxt = lax.while_loop(lambda k: (k < n_used) & (be_ref[jnp.minimum(k, be_ref.shape[0] - 1)] == be_ref[j]),
                                 lambda k: k + 1, j + 1)

            @pl.when(nxt < n_used)
            def _():
                for cp in fetch(be_ref[jnp.minimum(nxt, be_ref.shape[0] - 1)]):
                    cp.start()

        @pl.when(used)
        def _():
            x = jnp.concatenate([x_ref[pl.ds(base + k, rows, stride=SUBLANES), :] for k in range(SUBLANES)], axis=1)
            gu = jnp.dot(x.astype(BF16), wgu_bf_ref[...], preferred_element_type=F32)
            act = jax.nn.silu(gu[:, 0:f]) * gu[:, f:2 * f]
            y = jnp.dot(act.astype(BF16), wd_bf_ref[...], preferred_element_type=F32)
            for k in range(SUBLANES):
                y_ref[pl.ds(base + k, rows, stride=SUBLANES), :] = y[:, k * LANES:(k + 1) * LANES]

        @pl.when(jnp.logical_not(used))
        def _():
            y_ref[base:base + lines, :] = jnp.zeros((lines, LANES), F32)

    per_step = x_ref.shape[0] // lines
    for sub in range(per_step):
        block(pl.program_id(0) * per_step + sub, sub * lines)


def _experts(blk_expert, n_used, x_buf, w_gate_up, w_down):
    d = w_gate_up.shape[1]
    f = w_down.shape[1]
    r = EXPERT_ROWS
    n_blk = x_buf.shape[0] // (r * SUBLANES)
    per_step = EXPERT_BLOCKS_PER_STEP
    assert d == SUBLANES * LANES and x_buf.shape[1] == LANES, "rows are (8, 128) tiles (tile-contiguous layout)"
    assert n_blk % per_step == 0
    last_step = lambda nu: (jnp.maximum(nu[0], 1) - 1) // per_step
    return pl.pallas_call(
        _experts_kernel,
        grid_spec=pltpu.PrefetchScalarGridSpec(
            num_scalar_prefetch=2,
            grid=(n_blk // per_step,),
            in_specs=[pl.BlockSpec((per_step * r * SUBLANES, LANES), lambda s, be, nu: (jnp.minimum(s, last_step(nu)), 0)),
                      pl.BlockSpec(memory_space=pl.ANY), pl.BlockSpec(memory_space=pl.ANY)],
            out_specs=pl.BlockSpec((per_step * r * SUBLANES, LANES), lambda s, be, nu: (s, 0)),
            scratch_shapes=[pltpu.VMEM((d, 2 * f), F32), pltpu.VMEM((f, d), F32),
                            pltpu.VMEM((d, 2 * f), BF16), pltpu.VMEM((f, d), BF16),
                            pltpu.SemaphoreType.DMA((2,))]),
        out_shape=jax.ShapeDtypeStruct(x_buf.shape, F32),
        compiler_params=_params(("arbitrary",)),
        name="experts",
    )(blk_expert, n_used, x_buf, w_gate_up, w_down)


def _combine_kernel(pos_ref, rstart_ref, rlen_ref, h2_ref, gcol_ref, gfin_ref, y_ref, o_ref,
                    ysort_ref, pick0_ref, pick1_ref, sems):
    tc = h2_ref.shape[0]
    n = pos_ref.shape[0] // EXPERT_TOPK
    n_sorted = EXPERT_TOPK * tc
    step = pl.program_id(0)
    par = step % 2
    base = step * tc

    def start_gather(tile, buf):
        def run(e, offset):
            length = rlen_ref[tile * N_EXPERTS + e]
            first = rstart_ref[tile * N_EXPERTS + e]
            _start_run_copies(length, tc, lambda done, piece: pltpu.make_async_copy(
                y_ref.at[_tile_rows(first + done, piece)], ysort_ref.at[buf, _tile_rows(offset + done, piece)],
                sems.at[buf]))
            return offset + length

        lax.fori_loop(0, N_EXPERTS, run, 0)

    @pl.when(step == 0)
    def _():
        start_gather(0, 0)

    @pl.when(step + 1 < pl.num_programs(0))
    def _():
        start_gather(step + 1, 1 - par)

    pltpu.make_async_copy(y_ref.at[_tile_rows(0, n_sorted)], ysort_ref.at[par], sems.at[par]).wait()
    for t in range(tc):
        for slot, pick_ref in enumerate((pick0_ref, pick1_ref)):
            row = pos_ref[slot * n + base + t]
            pick_ref[t * SUBLANES:(t + 1) * SUBLANES, :] = ysort_ref[par, _tile_rows(row, 1), :]

    def rows_of(y_ref):
        return jnp.concatenate([y_ref[pl.ds(k, tc, stride=SUBLANES), :] for k in range(SUBLANES)], axis=1)

    g = gcol_ref[...]
    moe = g[:, 0:1] * rows_of(pick0_ref) + g[:, 1:2] * rows_of(pick1_ref)
    o_ref[...] = _rms(h2_ref[...] + moe, gfin_ref[...])


def _combine(pos_flat, run_start, run_len, h2, gcol, gfin, y_buf, tc):
    n, d = h2.shape
    return pl.pallas_call(
        _combine_kernel,
        grid_spec=pltpu.PrefetchScalarGridSpec(
            num_scalar_prefetch=3,
            grid=(n // tc,),
            in_specs=[pl.BlockSpec((tc, d), lambda i, *_: (i, 0)),
                      pl.BlockSpec((tc, LANES), lambda i, *_: (i, 0)),
                      pl.BlockSpec((1, d), lambda i, *_: (0, 0)),
                      pl.BlockSpec(memory_space=pl.ANY)],
            out_specs=pl.BlockSpec((tc, d), lambda i, *_: (i, 0)),
            scratch_shapes=[pltpu.VMEM((2, EXPERT_TOPK * tc * SUBLANES, LANES), F32),
                            pltpu.VMEM((tc * SUBLANES, LANES), F32), pltpu.VMEM((tc * SUBLANES, LANES), F32),
                            pltpu.SemaphoreType.DMA((2,))]),
        out_shape=jax.ShapeDtypeStruct((n, d), F32),
        compiler_params=_params(("arbitrary",)),
        name="combine",
    )(pos_flat, run_start, run_len, h2, gcol, gfin, y_buf)


def _rope_tables(s):
    half = ROT_DIM // 2
    inv_freq = (ROPE_THETA ** (-np.arange(0, ROT_DIM, 2, dtype=np.float32) / ROT_DIM)).astype(np.float32)
    ang = np.arange(s, dtype=np.float32)[:, None] * inv_freq[None, :]
    cos, sin = np.cos(ang), np.sin(ang)
    zeros = lambda w: np.zeros((s, w), np.float32)
    cos_h = np.concatenate([cos, cos, np.ones((s, HEAD_DIM - ROT_DIM), np.float32)], axis=1)
    sina_h = np.concatenate([-sin, zeros(HEAD_DIM - half)], axis=1)
    sinb_h = np.concatenate([zeros(half), sin, zeros(HEAD_DIM - ROT_DIM)], axis=1)
    tile = lambda t: jnp.asarray(np.tile(t, (1, LANES // HEAD_DIM)), F32)
    return tile(cos_h), tile(sina_h), tile(sinb_h)


def _block_diag(w):
    nblk, bi, bo = w.shape
    eye = jnp.eye(nblk, dtype=w.dtype)
    return jnp.einsum('hij,hg->higj', w, eye).reshape(nblk * bi, nblk * bo)


def _layer(h, mem, norm_mix, w_in, conv_w, conv_b, w_rg, b_rg, w_ig, b_ig, lru_lambda, norm_lru_out,
           norm_attn_out, w_out, norm_cross, norm_mem, w_xq, w_xkv, w_xo, norm_ffn, w_router_group,
           b_router_group, w_router_expert, b_router_expert, w_gate_up, w_down, norm_out):
    b, s, d = h.shape
    c = conv_w.shape[-1]
    n = b * s
    row = lambda t: t.reshape(1, -1)
    tm = 512 if s % 512 == 0 else MOBA_BLOCK
    ts = tm
    assert s % ts == 0 and s % tm == 0 and c == N_HEADS * HEAD_DIM

    kx, vx = _memkv(mem, row(norm_mem), w_xkv.astype(BF16))

    w_gates = jnp.concatenate([_block_diag(w_rg), _block_diag(w_ig)], axis=1).astype(BF16)
    b_gates = jnp.concatenate([b_rg, b_ig]).reshape(1, -1)
    cos_t, sina_t, sinb_t = _rope_tables(s)
    ylru, q, k, v = _mixer_in(h, row(norm_mix), w_in.astype(BF16), conv_w, row(conv_b), w_gates, b_gates,
                              row(lru_lambda), row(norm_lru_out), cos_t, sina_t, sinb_t, ts)
    yatt = _moba(q, k, v)

    w_out_b = w_out.astype(BF16)
    wr_t = jnp.zeros((ROUTER_ROWS, d), F32)
    wr_t = wr_t.at[0:N_GROUPS].set(w_router_group.T).at[SUBLANES:SUBLANES + N_EXPERTS].set(w_router_expert.T)
    wr_hi = wr_t.astype(BF16)
    wr_t = jnp.concatenate([wr_hi, (wr_t - wr_hi.astype(F32)).astype(BF16)], axis=0)
    br_t = jnp.zeros((ROUTER_ROWS, 1), F32)
    br_t = br_t.at[0:N_GROUPS, 0].set(b_router_group).at[SUBLANES:SUBLANES + N_EXPERTS, 0].set(b_router_expert)
    tri = jnp.asarray(np.triu(np.ones((tm, tm), np.float32), 1), BF16)
    h2, hn2, gcol, counts, pos, tile_cnt = _post(
        h, ylru, yatt, row(norm_attn_out), w_out_b[:c], w_out_b[c:], row(norm_cross), w_xq.astype(BF16),
        kx, vx, w_xo.astype(BF16), row(norm_ffn), wr_t, br_t, tri, tm)

    counts = counts[:, 0].astype(jnp.int32)
    padded = (counts + EXPERT_ROWS - 1) // EXPERT_ROWS * EXPERT_ROWS
    pad_ends = jnp.cumsum(padded)
    pad_starts = pad_ends - padded
    n_blocks = n * EXPERT_TOPK // EXPERT_ROWS + N_EXPERTS
    blk_first = jnp.arange(n_blocks, dtype=jnp.int32) * EXPERT_ROWS
    blk_expert = jnp.minimum(jnp.sum(blk_first[:, None] >= pad_ends[None, :], axis=1), N_EXPERTS - 1).astype(jnp.int32)
    n_used = (pad_ends[-1:] // EXPERT_ROWS).astype(jnp.int32)
    fill = jnp.concatenate([pad_starts + counts, padded - counts, pad_ends[-1:]]).astype(jnp.int32)

    n_tiles = n // tm
    tile_cnt = tile_cnt[:, 0].astype(jnp.int32).reshape(n_tiles, N_EXPERTS)
    run_start = (pad_starts[None, :] + jnp.cumsum(tile_cnt, axis=0) - tile_cnt).astype(jnp.int32).reshape(-1)
    run_len = tile_cnt.reshape(-1)
    x_buf = _dispatch(fill, run_start, run_len, hn2, pos, n_blocks * EXPERT_ROWS, tm)
    y_buf = _experts(blk_expert, n_used, x_buf, w_gate_up, w_down)
    out = _combine(pos.reshape(-1), run_start, run_len, h2, gcol, row(norm_out), y_buf, tm)
    return out.reshape(b, s, d)


def kernel(x, mem, norm_mix, w_in, conv_w, conv_b, w_rg, b_rg, w_ig, b_ig, lru_lambda, norm_lru_out, norm_attn_out,
           w_out, norm_cross, norm_mem, w_xq, w_xkv, w_xo, norm_ffn, w_router_group, b_router_group,
           w_router_expert, b_router_expert, w_gate_up, w_down, norm_final):
    depth = norm_mix.shape[0]
    assert depth == 1, "the fused final norm assumes a single layer"
    l = 0
    return _layer(x, mem, norm_mix[l], w_in[l], conv_w[l], conv_b[l], w_rg[l], b_rg[l], w_ig[l], b_ig[l],
                  lru_lambda[l], norm_lru_out[l], norm_attn_out[l], w_out[l], norm_cross[l], norm_mem[l],
                  w_xq[l], w_xkv[l], w_xo[l], norm_ffn[l], w_router_group[l], b_router_group[l],
                  w_router_expert[l], b_router_expert[l], w_gate_up[l], w_down[l], norm_final)
```

```python
import jax
import jax.numpy as jnp
import numpy as np
from jax import lax
from jax.experimental import pallas as pl
from jax.experimental.pallas import tpu as pltpu

F32 = jnp.float32
BF16 = jnp.bfloat16

CONV_WIDTH = 4
RG_C = 8.0
N_HEADS = 8
HEAD_DIM = 64
ROT_DIM = HEAD_DIM // 4
ROPE_THETA = 500000.0
MOBA_BLOCK = 256
MOBA_TOPK = 3
N_XHEADS = 4
N_GROUPS = 4
EXPERTS_PER_GROUP = 8
N_EXPERTS = N_GROUPS * EXPERTS_PER_GROUP
EXPERT_TOPK = 2
EPS = 1e-6

LANES = 128
SUBLANES = 8
NEG_BIG = -1e30
LOG2_E = 1.4426950408889634
MOBA_QROWS = 128
MOBA_LOOKAHEAD = 5
EXPERT_ROWS = 512
MEMKV_BATCHES = 4
EXPERT_BLOCKS_PER_STEP = 4
RUN_SPLIT = 64
BF16_ROWS = 2 * SUBLANES
ROUTER_ROWS = -(-(SUBLANES + N_EXPERTS) // BF16_ROWS) * BF16_ROWS
VMEM_LIMIT = 56 * 1024 * 1024

NT_DIMS = (((1,), (1,)), ((), ()))


def _rms(x, g):
    return x * lax.rsqrt(jnp.mean(x * x, axis=-1, keepdims=True) + EPS) * g


def _params(sem):
    return pltpu.CompilerParams(dimension_semantics=sem, vmem_limit_bytes=VMEM_LIMIT)


def _memkv_kernel(mem_ref, g_ref, w_ref, k_ref, v_ref):
    nb, m, d = mem_ref.shape
    mn = _rms(mem_ref[...].reshape(nb * m, d), g_ref[...]).astype(BF16)
    kv = jnp.dot(mn, w_ref[...], preferred_element_type=F32)
    k_ref[...] = kv[:, :d].astype(BF16).reshape(nb, m, d)
    v_ref[...] = kv[:, d:].astype(BF16).reshape(nb, m, d)


def _memkv(mem, g, w_xkv):
    b, m, d = mem.shape
    nb = MEMKV_BATCHES if b % MEMKV_BATCHES == 0 else 1
    return pl.pallas_call(
        _memkv_kernel,
        grid=(b // nb,),
        in_specs=[pl.BlockSpec((nb, m, d), lambda i: (i, 0, 0)),
                  pl.BlockSpec((1, d), lambda i: (0, 0)),
                  pl.BlockSpec((d, 2 * d), lambda i: (0, 0))],
        out_specs=[pl.BlockSpec((nb, m, d), lambda i: (i, 0, 0)),
                   pl.BlockSpec((nb, m, d), lambda i: (i, 0, 0))],
        out_shape=[jax.ShapeDtypeStruct((b, m, d), BF16)] * 2,
        compiler_params=_params(("arbitrary",)),
        name="memkv",
    )(mem, g, w_xkv)


def _mixer_in_kernel(x_ref, gmix_ref, win_ref, convw_ref, convb_ref, wg_ref, bg_ref, lam_ref, glru_ref,
                     cos_ref, sina_ref, sinb_ref,
                     ylru_ref, q_ref, k_ref, v_ref, xpad_ref, hcar_ref, ascan_ref, uscan_ref):
    c = ylru_ref.shape[-1]
    si = pl.program_id(1)

    @pl.when(si == 0)
    def _():
        xpad_ref[0:SUBLANES, :] = jnp.zeros((SUBLANES, c), F32)
        hcar_ref[...] = jnp.zeros_like(hcar_ref)

    ts = x_ref.shape[1]
    xn = _rms(x_ref[0], gmix_ref[...]).astype(BF16)
    proj = jnp.dot(xn, win_ref[...], preferred_element_type=F32)
    x_lru = proj[:, 0:c]
    g_lru = proj[:, c:2 * c]
    q = proj[:, 2 * c:3 * c]
    k = proj[:, 3 * c:4 * c]
    v = proj[:, 4 * c:5 * c]

    xpad_ref[SUBLANES:SUBLANES + ts, :] = x_lru
    cw = convw_ref[...]
    xc = convb_ref[...] + cw[3:4] * x_lru
    for j in range(1, CONV_WIDTH):
        xc = xc + cw[3 - j:4 - j] * xpad_ref[SUBLANES - j:SUBLANES - j + ts, :]
    xpad_ref[0:SUBLANES, :] = x_lru[ts - SUBLANES:ts, :]

    gates = jnp.dot(xc.astype(BF16), wg_ref[...], preferred_element_type=F32) + bg_ref[...]
    r = jax.nn.sigmoid(gates[:, 0:c])
    i = jax.nn.sigmoid(gates[:, c:2 * c])
    neg_lam = -lam_ref[...]
    softplus = jnp.maximum(neg_lam, 0.0) + jnp.log1p(jnp.exp(-jnp.abs(neg_lam)))
    log_a = -RG_C * r * softplus
    a = jnp.exp(log_a)
    u = jnp.sqrt(-jnp.tanh(log_a) * (a * a + 1.0)) * (i * xc)

    groups = ts // SUBLANES
    cols = c // LANES
    for j in range(cols):
        ascan_ref[j] = a[:, j * LANES:(j + 1) * LANES]
        uscan_ref[j] = u[:, j * LANES:(j + 1) * LANES]
    for r in range(SUBLANES):
        rows = pl.ds(r, groups, stride=SUBLANES)
        a_r = jnp.concatenate([ascan_ref[j, rows, :] for j in range(cols)], axis=1)
        u_r = jnp.concatenate([uscan_ref[j, rows, :] for j in range(cols)], axis=1)
        if r:
            u_r = a_r * u_acc + u_r
            a_r = a_r * a_acc
            for j in range(cols):
                ascan_ref[j, rows, :] = a_r[:, j * LANES:(j + 1) * LANES]
                uscan_ref[j, rows, :] = u_r[:, j * LANES:(j + 1) * LANES]
        a_acc, u_acc = a_r, u_r
    carry = hcar_ref[...]
    hs = []
    for gi in range(groups):
        blk = slice(gi * SUBLANES, (gi + 1) * SUBLANES)
        a_g = jnp.concatenate([ascan_ref[j, blk, :] for j in range(cols)], axis=1)
        u_g = jnp.concatenate([uscan_ref[j, blk, :] for j in range(cols)], axis=1)
        hs.append(a_g * carry + u_g)
        carry = a_acc[gi:gi + 1] * carry + u_acc[gi:gi + 1]
    h = jnp.concatenate(hs, axis=0)
    hcar_ref[...] = carry

    y = h * jax.nn.gelu(g_lru)
    ylru_ref[0] = _rms(y, glru_ref[...]).astype(BF16)

    wide = lambda t_ref: jnp.concatenate([t_ref[...]] * (c // LANES), axis=1)
    cos, sina, sinb = wide(cos_ref), wide(sina_ref), wide(sinb_ref)
    half = ROT_DIM // 2

    def rope(t):
        return t * cos + pltpu.roll(t, c - half, axis=1) * sina + pltpu.roll(t, half, axis=1) * sinb

    q_ref[0] = rope(q)
    k_ref[0] = rope(k)
    v_ref[0] = v.astype(BF16)


def _mixer_in(x, gmix, w_in, conv_w, conv_b, w_gates, b_gates, lam, glru, cos_t, sina_t, sinb_t, ts):
    b, s, d = x.shape
    c = conv_w.shape[-1]
    full = lambda shape: pl.BlockSpec(shape, lambda bi, si: (0,) * len(shape))
    tab = pl.BlockSpec((ts, LANES), lambda bi, si: (si, 0))
    seq = lambda: pl.BlockSpec((1, ts, c), lambda bi, si: (bi, si, 0))
    return pl.pallas_call(
        _mixer_in_kernel,
        grid=(b, s // ts),
        in_specs=[pl.BlockSpec((1, ts, d), lambda bi, si: (bi, si, 0)),
                  full((1, d)), full(w_in.shape), full(conv_w.shape), full((1, c)),
                  full(w_gates.shape), full((1, 2 * c)), full((1, c)), full((1, c)),
                  tab, tab, tab],
        out_specs=[seq(), seq(), seq(), seq()],
        out_shape=[jax.ShapeDtypeStruct((b, s, c), BF16),
                   jax.ShapeDtypeStruct((b, s, c), F32),
                   jax.ShapeDtypeStruct((b, s, c), F32),
                   jax.ShapeDtypeStruct((b, s, c), BF16)],
        scratch_shapes=[pltpu.VMEM((ts + SUBLANES, c), F32), pltpu.VMEM((1, c), F32),
                        pltpu.VMEM((c // LANES, ts, LANES), F32), pltpu.VMEM((c // LANES, ts, LANES), F32)],
        compiler_params=_params(("arbitrary", "arbitrary")),
        name="mixer_in",
    )(x, gmix, w_in, conv_w, conv_b, w_gates, b_gates, lam, glru, cos_t, sina_t, sinb_t)


def _moba_kernel(q_ref, k_ref, v_ref, kpat_ref, o_ref, qa_ref, ka_ref):
    s_len = q_ref.shape[1]
    nb = s_len // MOBA_BLOCK
    n_pick = min(MOBA_TOPK, nb)
    q = q_ref[0]
    k = k_ref[0]
    lane = lax.broadcasted_iota(jnp.int32, (1, LANES), 1)
    kmean = jnp.concatenate(
        [jnp.sum(k[n * MOBA_BLOCK:(n + 1) * MOBA_BLOCK], axis=0, keepdims=True) for n in range(nb)],
        axis=0) * (1.0 / MOBA_BLOCK)

    blk_row = lax.broadcasted_iota(jnp.int32, (nb, s_len), 0)
    q_blk = lax.broadcasted_iota(jnp.int32, (nb, s_len), 1) // MOBA_BLOCK
    past = blk_row < q_blk

    q_hi = q.astype(BF16)
    q_lo = (q - q_hi.astype(F32)).astype(BF16)
    for h in range(2):
        own = (lane >= h * HEAD_DIM) & (lane < (h + 1) * HEAD_DIM)
        off = (1 - h) * HEAD_DIM
        km = jnp.where(own, kmean, 0.0)
        km_hi = km.astype(BF16)
        km_split = jnp.concatenate([km_hi, (km - km_hi.astype(F32)).astype(BF16)], axis=0)
        parts = (lax.dot_general(km_split, q_hi, NT_DIMS, preferred_element_type=F32)
                 + lax.dot_general(km_split, q_lo, NT_DIMS, preferred_element_type=F32))
        gate = jnp.where(past, parts[0:nb] + parts[nb:2 * nb], -jnp.inf)
        rank = jnp.zeros((nb, s_len), jnp.int32)
        for m in range(nb):
            gm = gate[m:m + 1, :]
            ahead = (gm > gate) | ((gm == gate) & (m < blk_row))
            rank = rank + ahead.astype(jnp.int32)
        allowed = ((rank < n_pick) & past) | (blk_row == q_blk)
        bias = jnp.where(allowed, 0.0, NEG_BIG)
        pieces = []
        if off:
            pieces.append(jnp.zeros((off, s_len), F32))
        pieces.append(bias)
        if LANES - off - nb:
            pieces.append(jnp.zeros((LANES - off - nb, s_len), F32))
        bias_lanes = jnp.concatenate(pieces, axis=0).T
        qa_ref[h] = jnp.where(own, q * (HEAD_DIM ** -0.5 * LOG2_E), bias_lanes).astype(BF16)
        ka_ref[h] = jnp.where(own, k.astype(BF16), kpat_ref[h])

    own0 = lane < HEAD_DIM

    parts = MOBA_BLOCK // MOBA_QROWS

    def q_rows(qi, part):
        return slice(qi * MOBA_BLOCK + part * MOBA_QROWS, qi * MOBA_BLOCK + (part + 1) * MOBA_QROWS)

    def n_keys(qi, part):
        return qi * MOBA_BLOCK + (part + 1) * MOBA_QROWS

    def scores(qi, part, h):
        return lax.dot_general(qa_ref[h, q_rows(qi, part), :], ka_ref[h, 0:n_keys(qi, part), :], NT_DIMS,
                               preferred_element_type=F32)

    units = [(qi, part, h) for qi in range(nb) for part in range(parts) for h in range(2)]
    ahead = [scores(*u) for u in units[:MOBA_LOOKAHEAD]]
    outs = []

    def weighted_values(p, l, qi, part, h):
        outs.append(jnp.dot(p, v_ref[0, 0:n_keys(qi, part), :], preferred_element_type=F32) / l)
        if h:
            o_ref[0, q_rows(qi, part), :] = jnp.where(own0, outs[0], outs[1])
            outs.clear()

    pending = None
    for n, (qi, part, h) in enumerate(units):
        s = ahead.pop(0)
        if n + MOBA_LOOKAHEAD < len(units):
            ahead.append(scores(*units[n + MOBA_LOOKAHEAD]))
        n_past = qi * MOBA_BLOCK
        own_keys = (part + 1) * MOBA_QROWS
        key_in_blk = lax.broadcasted_iota(jnp.int32, (MOBA_QROWS, own_keys), 1)
        q_in_blk = lax.broadcasted_iota(jnp.int32, (MOBA_QROWS, own_keys), 0) + part * MOBA_QROWS
        s_own = jnp.where(key_in_blk <= q_in_blk, s[:, n_past:], NEG_BIG)
        s = jnp.concatenate([s[:, :n_past], s_own], axis=1) if qi else s_own
        p = jnp.exp2(s - jnp.max(s, axis=-1, keepdims=True))
        l = jnp.sum(p, axis=-1, keepdims=True)
        if pending is not None:
            weighted_values(*pending)
        pending = (p.astype(BF16), l, qi, part, h)
    weighted_values(*pending)


def _key_block_pattern(s):
    pat = np.zeros((2, s, LANES), np.float32)
    for h in range(2):
        off = (1 - h) * HEAD_DIM
        pat[h, np.arange(s), off + np.arange(s) // MOBA_BLOCK] = 1.0
    return jnp.asarray(pat, BF16)


def _moba(q, k, v):
    b, s, c = q.shape
    spec = lambda: pl.BlockSpec((1, s, LANES), lambda bi, hi: (bi, 0, hi))
    return pl.pallas_call(
        _moba_kernel,
        grid=(b, c // LANES),
        in_specs=[spec(), spec(), spec(), pl.BlockSpec((2, s, LANES), lambda bi, hi: (0, 0, 0))],
        out_specs=spec(),
        out_shape=jax.ShapeDtypeStruct((b, s, c), F32),
        scratch_shapes=[pltpu.VMEM((2, s, LANES), BF16), pltpu.VMEM((2, s, LANES), BF16)],
        compiler_params=_params(("arbitrary", "arbitrary")),
        name="moba",
    )(q, k, v, _key_block_pattern(s))


def _post_kernel(x_ref, ylru_ref, yatt_ref, gatt_ref, woa_ref, wob_ref, gcross_ref, wxq_ref, kx_ref, vx_ref,
                 wxo_ref, gffn_ref, wr_ref, br_ref, tri_ref, elow_ref,
                 h2_ref, hn2_ref, gcol_ref, cnt_ref, pos_ref, tcnt_ref, carry_ref):
    tm = x_ref.shape[1]
    d = x_ref.shape[2]
    xd = d // N_XHEADS
    first = (pl.program_id(0) == 0) & (pl.program_id(1) == 0)

    @pl.when(first)
    def _():
        carry_ref[...] = jnp.zeros_like(carry_ref)

    ya = _rms(yatt_ref[0], gatt_ref[...]).astype(BF16)
    mix = (jnp.dot(ylru_ref[0], woa_ref[...], preferred_element_type=F32)
           + jnp.dot(ya, wob_ref[...], preferred_element_type=F32))
    h1 = x_ref[0] + mix

    hn = _rms(h1, gcross_ref[...]).astype(BF16)
    qx = jnp.dot(hn, wxq_ref[...], preferred_element_type=F32)
    heads = []
    for hh in range(N_XHEADS):
        qh = (qx[:, hh * xd:(hh + 1) * xd] * (xd ** -0.5)).astype(BF16)
        s = lax.dot_general(qh, kx_ref[0, :, hh * xd:(hh + 1) * xd], NT_DIMS, preferred_element_type=F32)
        p = jnp.exp(s - jnp.max(s, axis=-1, keepdims=True))
        l = jnp.sum(p, axis=-1, keepdims=True)
        o = jnp.dot(p.astype(BF16), vx_ref[0, :, hh * xd:(hh + 1) * xd], preferred_element_type=F32) / l
        heads.append(o.astype(BF16))
    h2 = h1 + jnp.dot(jnp.concatenate(heads, axis=1), wxo_ref[...], preferred_element_type=F32)
    hn2 = _rms(h2, gffn_ref[...])
    h2_ref[...] = h2
    hn2_ref[...] = hn2.astype(BF16)

    x_hi = hn2.astype(BF16)
    x_lo = (hn2 - x_hi.astype(F32)).astype(BF16)
    by_hi = lax.dot_general(wr_ref[...], x_hi, NT_DIMS, preferred_element_type=F32)
    by_lo = lax.dot_general(wr_ref[0:ROUTER_ROWS], x_lo, NT_DIMS, preferred_element_type=F32)
    logits = by_hi[0:ROUTER_ROWS] + by_hi[ROUTER_ROWS:2 * ROUTER_ROWS] + by_lo + br_ref[...]
    gl = logits[0:N_GROUPS]
    gmax = jnp.max(gl, axis=0, keepdims=True)
    gi = lax.broadcasted_iota(jnp.int32, gl.shape, 0)
    grp = jnp.min(jnp.where(gl == gmax, gi, N_GROUPS), axis=0, keepdims=True)
    g_w = 1.0 / jnp.sum(jnp.exp(gl - gmax), axis=0, keepdims=True)
    el = jnp.zeros((EXPERTS_PER_GROUP, tm), F32)
    for g in range(N_GROUPS):
        lo = SUBLANES + g * EXPERTS_PER_GROUP
        el = jnp.where(grp == g, logits[lo:lo + EXPERTS_PER_GROUP], el)
    ee = jnp.exp(el - jnp.max(el, axis=0, keepdims=True))
    ep = ee / jnp.sum(ee, axis=0, keepdims=True)
    ei = lax.broadcasted_iota(jnp.int32, ep.shape, 0)
    p1 = jnp.max(ep, axis=0, keepdims=True)
    i1 = jnp.min(jnp.where(ep == p1, ei, EXPERTS_PER_GROUP), axis=0, keepdims=True)
    ep_rest = jnp.where(ei == i1, -1.0, ep)
    p2 = jnp.max(ep_rest, axis=0, keepdims=True)
    i2 = jnp.min(jnp.where(ep_rest == p2, ei, EXPERTS_PER_GROUP), axis=0, keepdims=True)
    den = p1 + p2
    gate1 = g_w * p1 / den
    gate2 = g_w * p2 / den
    e1 = grp * EXPERTS_PER_GROUP + i1
    e2 = grp * EXPERTS_PER_GROUP + i2
    li = lax.broadcasted_iota(jnp.int32, (LANES, tm), 0)
    gcol_ref[...] = jnp.where(li == 0, gate1, jnp.where(li == 1, gate2, 0.0)).T

    xi = lax.broadcasted_iota(jnp.int32, (N_EXPERTS, tm), 0)
    oh1 = xi == e1
    oh2 = xi == e2
    cnt = oh1.astype(F32) + oh2.astype(F32)
    tile_cnt = jnp.broadcast_to(jnp.sum(cnt, axis=1, keepdims=True), (N_EXPERTS, LANES))
    carry_ref[...] = carry_ref[...] + tile_cnt
    cnt_ref[...] = carry_ref[...]
    tcnt_ref[...] = tile_cnt
    in_tile = jnp.dot(cnt.astype(BF16), tri_ref[...], preferred_element_type=F32)
    tile_start = jnp.dot(elow_ref[...], tile_cnt, precision=lax.Precision.HIGHEST,
                         preferred_element_type=F32)[:, 0:1]
    local = in_tile + tile_start
    pos_ref[0:1, :] = jnp.sum(jnp.where(oh1, local, 0.0), axis=0, keepdims=True).astype(jnp.int32)
    pos_ref[1:2, :] = jnp.sum(jnp.where(oh2, local, 0.0), axis=0, keepdims=True).astype(jnp.int32)


def _post(x, ylru, yatt, gatt, wo_a, wo_b, gcross, wxq, kx, vx, wxo, gffn, wr_t, br_t, tri, tm):
    b, s, d = x.shape
    c = ylru.shape[-1]
    m = kx.shape[1]
    n = b * s
    nt = s // tm
    full = lambda shape: pl.BlockSpec(shape, lambda bi, si: (0,) * len(shape))
    tok = lambda rows: pl.BlockSpec((rows, tm), lambda bi, si: (0, bi * nt + si))
    return pl.pallas_call(
        _post_kernel,
        grid=(b, nt),
        in_specs=[pl.BlockSpec((1, tm, d), lambda bi, si: (bi, si, 0)),
                  pl.BlockSpec((1, tm, c), lambda bi, si: (bi, si, 0)),
                  pl.BlockSpec((1, tm, c), lambda bi, si: (bi, si, 0)),
                  full((1, c)), full((c, d)), full((c, d)), full((1, d)), full((d, d)),
                  pl.BlockSpec((1, m, d), lambda bi, si: (bi, 0, 0)),
                  pl.BlockSpec((1, m, d), lambda bi, si: (bi, 0, 0)),
                  full((d, d)), full((1, d)), full(wr_t.shape), full(br_t.shape), full((tm, tm)),
                  full((N_EXPERTS, N_EXPERTS))],
        out_specs=[pl.BlockSpec((tm, d), lambda bi, si: (bi * nt + si, 0)),
                   pl.BlockSpec((tm, d), lambda bi, si: (bi * nt + si, 0)),
                   pl.BlockSpec((tm, LANES), lambda bi, si: (bi * nt + si, 0)),
                   full((N_EXPERTS, LANES)),
                   tok(EXPERT_TOPK),
                   pl.BlockSpec((N_EXPERTS, LANES), lambda bi, si: (bi * nt + si, 0))],
        out_shape=[jax.ShapeDtypeStruct((n, d), F32),
                   jax.ShapeDtypeStruct((n, d), BF16),
                   jax.ShapeDtypeStruct((n, LANES), F32),
                   jax.ShapeDtypeStruct((N_EXPERTS, LANES), F32),
                   jax.ShapeDtypeStruct((EXPERT_TOPK, n), jnp.int32),
                   jax.ShapeDtypeStruct((n // tm * N_EXPERTS, LANES), F32)],
        scratch_shapes=[pltpu.VMEM((N_EXPERTS, LANES), F32)],
        compiler_params=_params(("arbitrary", "arbitrary")),
        name="post",
    )(x, ylru, yatt, gatt, wo_a, wo_b, gcross, wxq, kx, vx, wxo, gffn, wr_t, br_t, tri,
      jnp.asarray(np.tril(np.ones((N_EXPERTS, N_EXPERTS), np.float32), -1)))


def _pieces(largest):
    piece = largest
    while piece:
        yield piece
        piece //= 2


def _start_run_copies(length, largest, copy_of):
    def start(pieces):
        for piece in pieces:
            done = length & ~(2 * piece - 1)

            @pl.when((length & piece) != 0)
            def _():
                copy_of(done, piece).start()

    @pl.when(length >= RUN_SPLIT)
    def _():
        start([p for p in _pieces(largest) if p >= RUN_SPLIT])

    start([p for p in _pieces(largest) if p < RUN_SPLIT])


def _tile_rows(row, count):
    return pl.ds(pl.multiple_of(row * SUBLANES, SUBLANES), count * SUBLANES)


def _dispatch_kernel(fill_ref, rstart_ref, rlen_ref, hn_ref, pos_ref, xout_ref, sorted_ref, zero_ref, sem, zsem):
    tm = hn_ref.shape[0]
    n_sorted = EXPERT_TOPK * tm
    step = pl.program_id(0)

    zero_rows = zero_ref.shape[0] // SUBLANES
    n_rows = xout_ref.shape[0] // SUBLANES

    def pad_copies(e, start):
        first = fill_ref[e]
        length = fill_ref[N_EXPERTS + e]
        for piece in _pieces(zero_rows):
            offset = first + (length & ~(2 * piece - 1))

            @pl.when((length & piece) != 0)
            def _():
                cp = pltpu.make_async_copy(zero_ref.at[_tile_rows(0, piece)], xout_ref.at[_tile_rows(offset, piece)],
                                           zsem)
                cp.start() if start else cp.wait()

    def tail_copies(start):
        def one(i, carry):
            cp = pltpu.make_async_copy(zero_ref, xout_ref.at[_tile_rows(i * zero_rows, zero_rows)], zsem)
            cp.start() if start else cp.wait()
            return carry

        lax.fori_loop(fill_ref[2 * N_EXPERTS] // zero_rows, n_rows // zero_rows, one, 0)

    @pl.when(step == 0)
    def _():
        zero_ref[...] = jnp.zeros_like(zero_ref)
        lax.fori_loop(0, N_EXPERTS, lambda e, c: (pad_copies(e, True), c)[1], 0)
        tail_copies(True)

    pos = pos_ref[...]
    p_iota = lax.broadcasted_iota(jnp.int32, (n_sorted, tm), 0)
    picks = jnp.where((p_iota == pos[0:1, :]) | (p_iota == pos[1:2, :]), 1.0, 0.0).astype(BF16)
    xs = jnp.dot(picks, hn_ref[...], preferred_element_type=F32)
    for k in range(SUBLANES):
        sorted_ref[pl.ds(k, n_sorted, stride=SUBLANES), :] = xs[:, k * LANES:(k + 1) * LANES]

    def run(e, offset):
        length = rlen_ref[step * N_EXPERTS + e]
        first = rstart_ref[step * N_EXPERTS + e]

        _start_run_copies(length, tm, lambda done, piece: pltpu.make_async_copy(
            sorted_ref.at[_tile_rows(offset + done, piece)], xout_ref.at[_tile_rows(first + done, piece)], sem))
        return offset + length

    lax.fori_loop(0, N_EXPERTS, run, 0)

    pltpu.make_async_copy(sorted_ref, xout_ref.at[_tile_rows(0, n_sorted)], sem).wait()

    @pl.when(step == 0)
    def _():
        lax.fori_loop(0, N_EXPERTS, lambda e, c: (pad_copies(e, False), c)[1], 0)
        tail_copies(False)


def _dispatch(fill, run_start, run_len, hn2, pos, n_rows, tm):
    n, d = hn2.shape
    assert d == SUBLANES * LANES, "a row must be exactly one (8, 128) tile for the tile-contiguous layout"
    return pl.pallas_call(
        _dispatch_kernel,
        grid_spec=pltpu.PrefetchScalarGridSpec(
            num_scalar_prefetch=3,
            grid=(n // tm,),
            in_specs=[pl.BlockSpec((tm, d), lambda i, *_: (i, 0)),
                      pl.BlockSpec((EXPERT_TOPK, tm), lambda i, *_: (0, i))],
            out_specs=pl.BlockSpec(memory_space=pl.ANY),
            scratch_shapes=[pltpu.VMEM((EXPERT_TOPK * tm * SUBLANES, LANES), F32),
                            pltpu.VMEM((EXPERT_ROWS // 2 * SUBLANES, LANES), F32),
                            pltpu.SemaphoreType.DMA(()), pltpu.SemaphoreType.DMA(())]),
        out_shape=jax.ShapeDtypeStruct((n_rows * SUBLANES, LANES), F32),
        compiler_params=_params(("arbitrary",)),
        name="dispatch",
    )(fill, run_start, run_len, hn2, pos)


def _experts_kernel(be_ref, nu_ref, x_ref, wgu_hbm, wd_hbm, y_ref, wgu_f32_ref, wd_f32_ref, wgu_bf_ref, wd_bf_ref,
                    sems):
    rows = EXPERT_ROWS
    lines = rows * SUBLANES
    f = wd_bf_ref.shape[0]
    n_used = nu_ref[0]

    def fetch(e):
        return (pltpu.make_async_copy(wgu_hbm.at[e], wgu_f32_ref, sems.at[0]),
                pltpu.make_async_copy(wd_hbm.at[e], wd_f32_ref, sems.at[1]))

    def block(j, base):
        used = j < n_used

        @pl.when(used & (j == 0))
        def _():
            for cp in fetch(be_ref[0]):
                cp.start()

        @pl.when(used & ((j == 0) | (be_ref[j] != be_ref[jnp.maximum(j, 1) - 1])))
        def _():
            for cp in fetch(be_ref[j]):
                cp.wait()
            wgu_bf_ref[...] = wgu_f32_ref[...].astype(BF16)
            wd_bf_ref[...] = wd_f32_ref[...].astype(BF16)
            nxt = lax.while_loop(lambda k: (k < n_used) & (be_ref[jnp.minimum(k, be_ref.shape[0] - 1)] == be_ref[j]),
                                 lambda k: k + 1, j + 1)

            @pl.when(nxt < n_used)
            def _():
                for cp in fetch(be_ref[jnp.minimum(nxt, be_ref.shape[0] - 1)]):
                    cp.start()

        @pl.when(used)
        def _():
            x = jnp.concatenate([x_ref[pl.ds(base + k, rows, stride=SUBLANES), :] for k in range(SUBLANES)], axis=1)
            gu = jnp.dot(x.astype(BF16), wgu_bf_ref[...], preferred_element_type=F32)
            act = jax.nn.silu(gu[:, 0:f]) * gu[:, f:2 * f]
            y = jnp.dot(act.astype(BF16), wd_bf_ref[...], preferred_element_type=F32)
            for k in range(SUBLANES):
                y_ref[pl.ds(base + k, rows, stride=SUBLANES), :] = y[:, k * LANES:(k + 1) * LANES]

        @pl.when(jnp.logical_not(used))
        def _():
            y_ref[base:base + lines, :] = jnp.zeros((lines, LANES), F32)

    per_step = x_ref.shape[0] // lines
    for sub in range(per_step):
        block(pl.program_id(0) * per_step + sub, sub * lines)


def _experts(blk_expert, n_used, x_buf, w_gate_up, w_down):
    d = w_gate_up.shape[1]
    f = w_down.shape[1]
    r = EXPERT_ROWS
    n_blk = x_buf.shape[0] // (r * SUBLANES)
    per_step = EXPERT_BLOCKS_PER_STEP
    assert d == SUBLANES * LANES and x_buf.shape[1] == LANES, "rows are (8, 128) tiles (tile-contiguous layout)"
    assert n_blk % per_step == 0
    last_step = lambda nu: (jnp.maximum(nu[0], 1) - 1) // per_step
    return pl.pallas_call(
        _experts_kernel,
        grid_spec=pltpu.PrefetchScalarGridSpec(
            num_scalar_prefetch=2,
            grid=(n_blk // per_step,),
            in_specs=[pl.BlockSpec((per_step * r * SUBLANES, LANES), lambda s, be, nu: (jnp.minimum(s, last_step(nu)), 0)),
                      pl.BlockSpec(memory_space=pl.ANY), pl.BlockSpec(memory_space=pl.ANY)],
            out_specs=pl.BlockSpec((per_step * r * SUBLANES, LANES), lambda s, be, nu: (s, 0)),
            scratch_shapes=[pltpu.VMEM((d, 2 * f), F32), pltpu.VMEM((f, d), F32),
                            pltpu.VMEM((d, 2 * f), BF16), pltpu.VMEM((f, d), BF16),
                            pltpu.SemaphoreType.DMA((2,))]),
        out_shape=jax.ShapeDtypeStruct(x_buf.shape, F32),
        compiler_params=_params(("arbitrary",)),
        name="experts",
    )(blk_expert, n_used, x_buf, w_gate_up, w_down)


def _combine_kernel(pos_ref, rstart_ref, rlen_ref, h2_ref, gcol_ref, gfin_ref, y_ref, o_ref,
                    ysort_ref, pick0_ref, pick1_ref, sems):
    tc = h2_ref.shape[0]
    n = pos_ref.shape[0] // EXPERT_TOPK
    n_sorted = EXPERT_TOPK * tc
    step = pl.program_id(0)
    par = step % 2
    base = step * tc

    def start_gather(tile, buf):
        def run(e, offset):
            length = rlen_ref[tile * N_EXPERTS + e]
            first = rstart_ref[tile * N_EXPERTS + e]
            _start_run_copies(length, tc, lambda done, piece: pltpu.make_async_copy(
                y_ref.at[_tile_rows(first + done, piece)], ysort_ref.at[buf, _tile_rows(offset + done, piece)],
                sems.at[buf]))
            return offset + length

        lax.fori_loop(0, N_EXPERTS, run, 0)

    @pl.when(step == 0)
    def _():
        start_gather(0, 0)

    @pl.when(step + 1 < pl.num_programs(0))
    def _():
        start_gather(step + 1, 1 - par)

    pltpu.make_async_copy(y_ref.at[_tile_rows(0, n_sorted)], ysort_ref.at[par], sems.at[par]).wait()
    for t in range(tc):
        for slot, pick_ref in enumerate((pick0_ref, pick1_ref)):
            row = pos_ref[slot * n + base + t]
            pick_ref[t * SUBLANES:(t + 1) * SUBLANES, :] = ysort_ref[par, _tile_rows(row, 1), :]

    def rows_of(y_ref):
        return jnp.concatenate([y_ref[pl.ds(k, tc, stride=SUBLANES), :] for k in range(SUBLANES)], axis=1)

    g = gcol_ref[...]
    moe = g[:, 0:1] * rows_of(pick0_ref) + g[:, 1:2] * rows_of(pick1_ref)
    o_ref[...] = _rms(h2_ref[...] + moe, gfin_ref[...])


def _combine(pos_flat, run_start, run_len, h2, gcol, gfin, y_buf, tc):
    n, d = h2.shape
    return pl.pallas_call(
        _combine_kernel,
        grid_spec=pltpu.PrefetchScalarGridSpec(
            num_scalar_prefetch=3,
            grid=(n // tc,),
            in_specs=[pl.BlockSpec((tc, d), lambda i, *_: (i, 0)),
                      pl.BlockSpec((tc, LANES), lambda i, *_: (i, 0)),
                      pl.BlockSpec((1, d), lambda i, *_: (0, 0)),
                      pl.BlockSpec(memory_space=pl.ANY)],
            out_specs=pl.BlockSpec((tc, d), lambda i, *_: (i, 0)),
            scratch_shapes=[pltpu.VMEM((2, EXPERT_TOPK * tc * SUBLANES, LANES), F32),
                            pltpu.VMEM((tc * SUBLANES, LANES), F32), pltpu.VMEM((tc * SUBLANES, LANES), F32),
                            pltpu.SemaphoreType.DMA((2,))]),
        out_shape=jax.ShapeDtypeStruct((n, d), F32),
        compiler_params=_params(("arbitrary",)),
        name="combine",
    )(pos_flat, run_start, run_len, h2, gcol, gfin, y_buf)


def _rope_tables(s):
    half = ROT_DIM // 2
    inv_freq = (ROPE_THETA ** (-np.arange(0, ROT_DIM, 2, dtype=np.float32) / ROT_DIM)).astype(np.float32)
    ang = np.arange(s, dtype=np.float32)[:, None] * inv_freq[None, :]
    cos, sin = np.cos(ang), np.sin(ang)
    zeros = lambda w: np.zeros((s, w), np.float32)
    cos_h = np.concatenate([cos, cos, np.ones((s, HEAD_DIM - ROT_DIM), np.float32)], axis=1)
    sina_h = np.concatenate([-sin, zeros(HEAD_DIM - half)], axis=1)
    sinb_h = np.concatenate([zeros(half), sin, zeros(HEAD_DIM - ROT_DIM)], axis=1)
    tile = lambda t: jnp.asarray(np.tile(t, (1, LANES // HEAD_DIM)), F32)
    return tile(cos_h), tile(sina_h), tile(sinb_h)


def _block_diag(w):
    nblk, bi, bo = w.shape
    eye = jnp.eye(nblk, dtype=w.dtype)
    return jnp.einsum('hij,hg->higj', w, eye).reshape(nblk * bi, nblk * bo)


def _layer(h, mem, norm_mix, w_in, conv_w, conv_b, w_rg, b_rg, w_ig, b_ig, lru_lambda, norm_lru_out,
           norm_attn_out, w_out, norm_cross, norm_mem, w_xq, w_xkv, w_xo, norm_ffn, w_router_group,
           b_router_group, w_router_expert, b_router_expert, w_gate_up, w_down, norm_out):
    b, s, d = h.shape
    c = conv_w.shape[-1]
    n = b * s
    row = lambda t: t.reshape(1, -1)
    tm = 512 if s % 512 == 0 else MOBA_BLOCK
    ts = tm
    assert s % ts == 0 and s % tm == 0 and c == N_HEADS * HEAD_DIM

    kx, vx = _memkv(mem, row(norm_mem), w_xkv.astype(BF16))

    w_gates = jnp.concatenate([_block_diag(w_rg), _block_diag(w_ig)], axis=1).astype(BF16)
    b_gates = jnp.concatenate([b_rg, b_ig]).reshape(1, -1)
    cos_t, sina_t, sinb_t = _rope_tables(s)
    ylru, q, k, v = _mixer_in(h, row(norm_mix), w_in.astype(BF16), conv_w, row(conv_b), w_gates, b_gates,
                              row(lru_lambda), row(norm_lru_out), cos_t, sina_t, sinb_t, ts)
    yatt = _moba(q, k, v)

    w_out_b = w_out.astype(BF16)
    wr_t = jnp.zeros((ROUTER_ROWS, d), F32)
    wr_t = wr_t.at[0:N_GROUPS].set(w_router_group.T).at[SUBLANES:SUBLANES + N_EXPERTS].set(w_router_expert.T)
    wr_hi = wr_t.astype(BF16)
    wr_t = jnp.concatenate([wr_hi, (wr_t - wr_hi.astype(F32)).astype(BF16)], axis=0)
    br_t = jnp.zeros((ROUTER_ROWS, 1), F32)
    br_t = br_t.at[0:N_GROUPS, 0].set(b_router_group).at[SUBLANES:SUBLANES + N_EXPERTS, 0].set(b_router_expert)
    tri = jnp.asarray(np.triu(np.ones((tm, tm), np.float32), 1), BF16)
    h2, hn2, gcol, counts, pos, tile_cnt = _post(
        h, ylru, yatt, row(norm_attn_out), w_out_b[:c], w_out_b[c:], row(norm_cross), w_xq.astype(BF16),
        kx, vx, w_xo.astype(BF16), row(norm_ffn), wr_t, br_t, tri, tm)

    counts = counts[:, 0].astype(jnp.int32)
    padded = (counts + EXPERT_ROWS - 1) // EXPERT_ROWS * EXPERT_ROWS
    pad_ends = jnp.cumsum(padded)
    pad_starts = pad_ends - padded
    n_blocks = n * EXPERT_TOPK // EXPERT_ROWS + N_EXPERTS
    blk_first = jnp.arange(n_blocks, dtype=jnp.int32) * EXPERT_ROWS
    blk_expert = jnp.minimum(jnp.sum(blk_first[:, None] >= pad_ends[None, :], axis=1), N_EXPERTS - 1).astype(jnp.int32)
    n_used = (pad_ends[-1:] // EXPERT_ROWS).astype(jnp.int32)
    fill = jnp.concatenate([pad_starts + counts, padded - counts, pad_ends[-1:]]).astype(jnp.int32)

    n_tiles = n // tm
    tile_cnt = tile_cnt[:, 0].astype(jnp.int32).reshape(n_tiles, N_EXPERTS)
    run_start = (pad_starts[None, :] + jnp.cumsum(tile_cnt, axis=0) - tile_cnt).astype(jnp.int32).reshape(-1)
    run_len = tile_cnt.reshape(-1)
    x_buf = _dispatch(fill, run_start, run_len, hn2, pos, n_blocks * EXPERT_ROWS, tm)
    y_buf = _experts(blk_expert, n_used, x_buf, w_gate_up, w_down)
    out = _combine(pos.reshape(-1), run_start, run_len, h2, gcol, row(norm_out), y_buf, tm)
    return out.reshape(b, s, d)


def kernel(x, mem, norm_mix, w_in, conv_w, conv_b, w_rg, b_rg, w_ig, b_ig, lru_lambda, norm_lru_out, norm_attn_out,
           w_out, norm_cross, norm_mem, w_xq, w_xkv, w_xo, norm_ffn, w_router_group, b_router_group,
           w_router_expert, b_router_expert, w_gate_up, w_down, norm_final):
    depth = norm_mix.shape[0]
    assert depth == 1, "the fused final norm assumes a single layer"
    l = 0
    return _layer(x, mem, norm_mix[l], w_in[l], conv_w[l], conv_b[l], w_rg[l], b_rg[l], w_ig[l], b_ig[l],
                  lru_lambda[l], norm_lru_out[l], norm_attn_out[l], w_out[l], norm_cross[l], norm_mem[l],
                  w_xq[l], w_xkv[l], w_xo[l], norm_ffn[l], w_router_group[l], b_router_group[l],
                  w_router_expert[l], b_router_expert[l], w_gate_up[l], w_down[l], norm_final)
```
